```python
import math
import jax
import jax.numpy as jnp
from jax import lax
import numpy as np

D_MODEL = 1024
BATCH = 8
SEQ = 4096
DEPTH = 4

CHUNK = 64
Q_BLOCK = 128
N_MIXERS = 3
EPS = 1e-6
NEG_INF = -1e30
D_FF = 4 * D_MODEL
N_MOD = 6

GLA_HEADS = 4
GLA_DK = D_MODEL // (2 * GLA_HEADS)
GLA_DV = D_MODEL // GLA_HEADS
GLA_GATE_RANK = 16
GLA_GATE_TEMP = 16.0
GLA_KW = GLA_HEADS * GLA_DK
GLA_VW = GLA_HEADS * GLA_DV
GLA_IN = 2 * GLA_KW + 2 * GLA_VW + GLA_GATE_RANK
GLA_SPLITS = [GLA_KW, 2 * GLA_KW, 2 * GLA_KW + GLA_VW, 2 * GLA_KW + 2 * GLA_VW]

MLA_HEADS = 16
MLA_NOPE = 64
MLA_ROPE = 32
MLA_V = 64
MLA_Q_RANK = 384
MLA_KV_RANK = 256
MLA_IN = MLA_Q_RANK + MLA_KV_RANK + MLA_ROPE
ROPE_THETA = 10000.0

GDN_K_HEADS = 8
GDN_V_HEADS = 16
GDN_DK = 128
GDN_DV = 128
GDN_CONV = 4
GDN_QKW = GDN_K_HEADS * GDN_DK
GDN_VW = GDN_V_HEADS * GDN_DV
GDN_CONV_W = 2 * GDN_QKW + GDN_VW
GDN_IN = GDN_CONV_W + GDN_VW + 2 * GDN_V_HEADS
GDN_SPLITS = [GDN_CONV_W, GDN_CONV_W + GDN_VW, GDN_CONV_W + GDN_VW + GDN_V_HEADS]

N_GLA = len(range(0, DEPTH, N_MIXERS))
N_MLA = len(range(1, DEPTH, N_MIXERS))
N_GDN = len(range(2, DEPTH, N_MIXERS))

kernel_name = "chunk_causal_hybrid_gla_mla_gdn_trunk"


def rms_norm(x, g):
    xf = x.astype(jnp.float32)
    y = xf * lax.rsqrt(jnp.mean(xf * xf, axis=-1, keepdims=True) + EPS)
    return (y * g.astype(jnp.float32)).astype(x.dtype)


def l2_normalize(x):
    return x * lax.rsqrt(jnp.sum(x * x, axis=-1, keepdims=True) + EPS)


def to_chunks(t, heads):
    b, s = t.shape[0], t.shape[1]
    return t.reshape(b, s // CHUNK, CHUNK, heads, -1).transpose(1, 0, 3, 2, 4)


def from_chunks(t):
    nc, b, h, l, d = t.shape
    return t.transpose(1, 0, 3, 2, 4).reshape(b, nc * l, h, d)


def gla_mixer(h, w_in, w_gate_up, b_gate, head_g, w_out):
    b, s, _ = h.shape
    f32 = jnp.float32
    q, k, v, r, g_lr = jnp.split(h @ w_in, GLA_SPLITS, axis=-1)
    log_a = jax.nn.log_sigmoid((g_lr @ w_gate_up + b_gate).astype(f32)) / GLA_GATE_TEMP
    qc = to_chunks(q.astype(f32) * GLA_DK ** -0.5, GLA_HEADS)
    kc = to_chunks(k.astype(f32), GLA_HEADS)
    vc = to_chunks(v.astype(f32), GLA_HEADS)
    cum = jnp.cumsum(to_chunks(log_a, GLA_HEADS), axis=3)
    k_end = kc * jnp.exp(cum[..., -1:, :] - cum)
    dec = jnp.exp(cum[..., -1, :])

    def step(state, inp):
        d, ke, vv, qq = inp
        state = d[..., :, None] * state + jnp.einsum('bhlk,bhlv->bhkv', ke, vv)
        return state, jnp.einsum('bhlk,bhkv->bhlv', qq, state)

    s0 = jnp.zeros((b, GLA_HEADS, GLA_DK, GLA_DV), f32)
    _, o = lax.scan(step, s0, (dec, k_end, vc, qc))
    o = rms_norm(from_chunks(o), head_g) * jax.nn.silu(r.astype(f32)).reshape(b, s, GLA_HEADS, GLA_DV)
    return o.reshape(b, s, GLA_VW).astype(h.dtype) @ w_out


def rope_tables(positions):
    half = MLA_ROPE // 2
    inv_freq = ROPE_THETA ** (-jnp.arange(half, dtype=jnp.float32) / half)
    ang = positions.astype(jnp.float32)[..., None] * inv_freq
    return jnp.cos(ang), jnp.sin(ang)


def apply_rope(x, cos, sin):
    x1, x2 = jnp.split(x.astype(jnp.float32), 2, axis=-1)
    return jnp.concatenate([x1 * cos - x2 * sin, x1 * sin + x2 * cos], axis=-1).astype(x.dtype)


def mla_mixer(h, positions, w_in, q_norm_g, w_uq, kv_norm_g, w_ukv, w_out):
    b, s, _ = h.shape
    c_q, c_kv, k_rope = jnp.split(h @ w_in, [MLA_Q_RANK, MLA_Q_RANK + MLA_KV_RANK], axis=-1)
    q = (rms_norm(c_q, q_norm_g) @ w_uq).reshape(b, s, MLA_HEADS, MLA_NOPE + MLA_ROPE)
    kv = (rms_norm(c_kv, kv_norm_g) @ w_ukv).reshape(b, s, MLA_HEADS, MLA_NOPE + MLA_V)
    q_nope, q_rope = jnp.split(q, [MLA_NOPE], axis=-1)
    k_nope, v = jnp.split(kv, [MLA_NOPE], axis=-1)
    cos, sin = rope_tables(positions)
    q_rope = apply_rope(q_rope, cos[:, :, None, :], sin[:, :, None, :])
    k_rope = apply_rope(k_rope, cos, sin)
    scale = (MLA_NOPE + MLA_ROPE) ** -0.5
    n_blk = s // Q_BLOCK
    key_chunk = jnp.arange(s) // CHUNK

    def blocks(t):
        return t.reshape(b, n_blk, Q_BLOCK, MLA_HEADS, -1).transpose(1, 0, 2, 3, 4)

    def attend(args):
        qn, qr, blk = args
        sc = (jnp.einsum('bqhd,bkhd->bhqk', qn, k_nope)
              + jnp.einsum('bqhr,bkr->bhqk', qr, k_rope)).astype(jnp.float32) * scale
        q_chunk = (blk * Q_BLOCK + jnp.arange(Q_BLOCK)) // CHUNK
        mask = key_chunk[None, :] <= q_chunk[:, None]
        p = jax.nn.softmax(jnp.where(mask, sc, NEG_INF), axis=-1).astype(v.dtype)
        return jnp.einsum('bhqk,bkhd->bqhd', p, v)

    o = lax.map(attend, (blocks(q_nope), blocks(q_rope), jnp.arange(n_blk)))
    o = o.transpose(1, 0, 2, 3, 4).reshape(b, s, MLA_HEADS * MLA_V)
    return o @ w_out


def causal_conv(x, w):
    s = x.shape[1]
    xp = jnp.pad(x, ((0, 0), (GDN_CONV - 1, 0), (0, 0)))
    out = xp[:, 0:s] * w[0]
    for j in range(1, GDN_CONV):
        out = out + xp[:, j:j + s] * w[j]
    return out


def gdn_mixer(h, w_in, conv_w, a_log, dt_bias, norm_g, w_out):
    b, s, _ = h.shape
    f32 = jnp.float32
    qkv, z, b_raw, a_raw = jnp.split(h @ w_in, GDN_SPLITS, axis=-1)
    qkv = jax.nn.silu(causal_conv(qkv, conv_w))
    q, k, v = jnp.split(qkv.astype(f32), [GDN_QKW, 2 * GDN_QKW], axis=-1)
    rep = GDN_V_HEADS // GDN_K_HEADS
    q = jnp.repeat(l2_normalize(q.reshape(b, s, GDN_K_HEADS, GDN_DK)), rep, axis=2) * GDN_DK ** -0.5
    k = jnp.repeat(l2_normalize(k.reshape(b, s, GDN_K_HEADS, GDN_DK)), rep, axis=2)
    beta = jax.nn.sigmoid(b_raw.astype(f32))
    g = -jnp.exp(a_log.astype(f32)) * jax.nn.softplus(a_raw.astype(f32) + dt_bias.astype(f32))
    qc = to_chunks(q.reshape(b, s, -1), GDN_V_HEADS)
    kc = to_chunks(k.reshape(b, s, -1), GDN_V_HEADS)
    vc = to_chunks(v, GDN_V_HEADS)
    bc = to_chunks(beta, GDN_V_HEADS)[..., 0]
    cum = jnp.cumsum(to_chunks(g, GDN_V_HEADS)[..., 0], axis=-1)
    strict = jnp.tril(jnp.ones((CHUNK, CHUNK), dtype=bool), -1)
    diff = jnp.where(strict, cum[..., :, None] - cum[..., None, :], 0.0)
    kk = jnp.einsum('nbhik,nbhjk->nbhij', kc, kc)
    a_mat = jnp.where(strict, bc[..., :, None] * jnp.exp(diff) * kk, 0.0)
    lhs = a_mat + jnp.eye(CHUNK, dtype=f32)
    rhs = jnp.concatenate([bc[..., None] * vc, (bc * jnp.exp(cum))[..., None] * kc], axis=-1)
    sol = lax.linalg.triangular_solve(lhs, rhs, left_side=True, lower=True, unit_diagonal=True)
    w_v, w_k = jnp.split(sol, [GDN_DV], axis=-1)
    k_end = kc * jnp.exp(cum[..., -1:] - cum)[..., None]
    dec = jnp.exp(cum[..., -1])

    def step(state, inp):
        wv, wk, ke, d, qq = inp
        u = wv - jnp.einsum('bhlk,bhkv->bhlv', wk, state)
        state = d[..., None, None] * state + jnp.einsum('bhlk,bhlv->bhkv', ke, u)
        return state, jnp.einsum('bhlk,bhkv->bhlv', qq, state)

    s0 = jnp.zeros((b, GDN_V_HEADS, GDN_DK, GDN_DV), f32)
    _, o = lax.scan(step, s0, (w_v, w_k, k_end, dec, qc))
    o = rms_norm(from_chunks(o), norm_g) * jax.nn.silu(z.astype(f32)).reshape(b, s, GDN_V_HEADS, GDN_DV)
    return o.reshape(b, s, GDN_VW).astype(h.dtype) @ w_out


def squared_relu_mlp(h, w_up, w_down):
    return jnp.square(jax.nn.relu(h @ w_up)) @ w_down


def _fwd_setup_inputs(seed: int = 0) -> dict:
    key = jax.random.key(seed)
    ks = iter(jax.random.split(key, 40))
    f32 = jnp.float32

    def nrm(shape, scale):
        return jax.random.normal(next(ks), shape, f32) * scale

    def gain(shape):
        return 1.0 + nrm(shape, 0.1)

    x = nrm((BATCH, SEQ, D_MODEL), 1.0)
    c = nrm((BATCH, D_MODEL), 1.0)
    positions = jnp.broadcast_to(jnp.arange(SEQ, dtype=jnp.int32), (BATCH, SEQ))
    ada_w = nrm((DEPTH, D_MODEL, N_MOD * D_MODEL), 0.5 * D_MODEL ** -0.5)
    ada_b = nrm((DEPTH, N_MOD * D_MODEL), 0.01)
    norm_pre_g = gain((DEPTH, 2, D_MODEL))
    norm_post_g = gain((DEPTH, 2, D_MODEL))

    gla_w_in = nrm((N_GLA, D_MODEL, GLA_IN), D_MODEL ** -0.5)
    gla_w_gate_up = nrm((N_GLA, GLA_GATE_RANK, GLA_KW), GLA_GATE_RANK ** -0.5)
    gla_b_gate = nrm((N_GLA, GLA_KW), 0.1)
    gla_head_g = gain((N_GLA, GLA_DV))
    gla_w_out = nrm((N_GLA, GLA_VW, D_MODEL), GLA_VW ** -0.5)

    mla_w_in = nrm((N_MLA, D_MODEL, MLA_IN), D_MODEL ** -0.5)
    mla_q_norm_g = gain((N_MLA, MLA_Q_RANK))
    mla_w_uq = nrm((N_MLA, MLA_Q_RANK, MLA_HEADS * (MLA_NOPE + MLA_ROPE)), MLA_Q_RANK ** -0.5)
    mla_kv_norm_g = gain((N_MLA, MLA_KV_RANK))
    mla_w_ukv = nrm((N_MLA, MLA_KV_RANK, MLA_HEADS * (MLA_NOPE + MLA_V)), MLA_KV_RANK ** -0.5)
    mla_w_out = nrm((N_MLA, MLA_HEADS * MLA_V, D_MODEL), (MLA_HEADS * MLA_V) ** -0.5)

    gdn_w_in = nrm((N_GDN, D_MODEL, GDN_IN), D_MODEL ** -0.5)
    gdn_conv_w = nrm((N_GDN, GDN_CONV, GDN_CONV_W), GDN_CONV ** -0.5)
    gdn_a_log = jnp.log(jax.random.uniform(next(ks), (N_GDN, GDN_V_HEADS), f32, 1.0, 16.0))
    dt = jnp.exp(jax.random.uniform(next(ks), (N_GDN, GDN_V_HEADS), f32, math.log(1e-3), math.log(1e-1)))
    gdn_dt_bias = dt + jnp.log(-jnp.expm1(-dt))
    gdn_norm_g = gain((N_GDN, GDN_DV))
    gdn_w_out = nrm((N_GDN, GDN_VW, D_MODEL), GDN_VW ** -0.5)

    mlp_w_up = nrm((DEPTH, D_MODEL, D_FF), D_MODEL ** -0.5)
    mlp_w_down = nrm((DEPTH, D_FF, D_MODEL), D_FF ** -0.5)
    return {
        "x": x, "c": c, "positions": positions,
        "ada_w": ada_w, "ada_b": ada_b, "norm_pre_g": norm_pre_g, "norm_post_g": norm_post_g,
        "gla_w_in": gla_w_in, "gla_w_gate_up": gla_w_gate_up, "gla_b_gate": gla_b_gate,
        "gla_head_g": gla_head_g, "gla_w_out": gla_w_out,
        "mla_w_in": mla_w_in, "mla_q_norm_g": mla_q_norm_g, "mla_w_uq": mla_w_uq,
        "mla_kv_norm_g": mla_kv_norm_g, "mla_w_ukv": mla_w_ukv, "mla_w_out": mla_w_out,
        "gdn_w_in": gdn_w_in, "gdn_conv_w": gdn_conv_w, "gdn_a_log": gdn_a_log,
        "gdn_dt_bias": gdn_dt_bias, "gdn_norm_g": gdn_norm_g, "gdn_w_out": gdn_w_out,
        "mlp_w_up": mlp_w_up, "mlp_w_down": mlp_w_down,
    }


def _fwd_reference(x, c, positions, ada_w, ada_b, norm_pre_g, norm_post_g,
              gla_w_in, gla_w_gate_up, gla_b_gate, gla_head_g, gla_w_out,
              mla_w_in, mla_q_norm_g, mla_w_uq, mla_kv_norm_g, mla_w_ukv, mla_w_out,
              gdn_w_in, gdn_conv_w, gdn_a_log, gdn_dt_bias, gdn_norm_g, gdn_w_out,
              mlp_w_up, mlp_w_down):
    cond = jax.nn.silu(c)
    for layer in range(DEPTH):
        mod = cond @ ada_w[layer] + ada_b[layer]
        sh_m, sc_m, gt_m, sh_f, sc_f, gt_f = [m[:, None, :] for m in jnp.split(mod, N_MOD, axis=-1)]

        h = rms_norm(x, norm_pre_g[layer, 0]) * (1.0 + sc_m) + sh_m
        kind, j = layer % N_MIXERS, layer // N_MIXERS
        if kind == 0:
            y = gla_mixer(h, gla_w_in[j], gla_w_gate_up[j], gla_b_gate[j], gla_head_g[j], gla_w_out[j])
        elif kind == 1:
            y = mla_mixer(h, positions, mla_w_in[j], mla_q_norm_g[j], mla_w_uq[j],
                          mla_kv_norm_g[j], mla_w_ukv[j], mla_w_out[j])
        else:
            y = gdn_mixer(h, gdn_w_in[j], gdn_conv_w[j], gdn_a_log[j], gdn_dt_bias[j],
                          gdn_norm_g[j], gdn_w_out[j])
        x = x + gt_m * rms_norm(y, norm_post_g[layer, 0])

        h = rms_norm(x, norm_pre_g[layer, 1]) * (1.0 + sc_f) + sh_f
        y = squared_relu_mlp(h, mlp_w_up[layer], mlp_w_down[layer])
        x = x + gt_f * rms_norm(y, norm_post_g[layer, 1])
    return x


import jax as _jax
import jax.numpy as _jnp

TWIN_FORMAT = 'train_step'
FWD_PARAMS = ['x', 'c', 'positions', 'ada_w', 'ada_b', 'norm_pre_g', 'norm_post_g', 'gla_w_in', 'gla_w_gate_up', 'gla_b_gate', 'gla_head_g', 'gla_w_out', 'mla_w_in', 'mla_q_norm_g', 'mla_w_uq', 'mla_kv_norm_g', 'mla_w_ukv', 'mla_w_out', 'gdn_w_in', 'gdn_conv_w', 'gdn_a_log', 'gdn_dt_bias', 'gdn_norm_g', 'gdn_w_out', 'mlp_w_up', 'mlp_w_down']
TWIN_WEIGHTS = ['ada_w', 'ada_b', 'norm_pre_g', 'norm_post_g', 'gla_w_in', 'gla_w_gate_up', 'gla_b_gate', 'gla_head_g', 'gla_w_out', 'mla_w_in', 'mla_q_norm_g', 'mla_w_uq', 'mla_kv_norm_g', 'mla_w_ukv', 'mla_w_out', 'gdn_w_in', 'gdn_conv_w', 'gdn_a_log', 'gdn_dt_bias', 'gdn_norm_g', 'gdn_w_out', 'mlp_w_up', 'mlp_w_down']
TWIN_DIFF_INPUT = 'x'
TWIN_INPUTS = ['x', 'c', 'positions', 'ada_w', 'ada_b', 'norm_pre_g', 'norm_post_g', 'gla_w_in', 'gla_w_gate_up', 'gla_b_gate', 'gla_head_g', 'gla_w_out', 'mla_w_in', 'mla_q_norm_g', 'mla_w_uq', 'mla_kv_norm_g', 'mla_w_ukv', 'mla_w_out', 'gdn_w_in', 'gdn_conv_w', 'gdn_a_log', 'gdn_dt_bias', 'gdn_norm_g', 'gdn_w_out', 'mlp_w_up', 'mlp_w_down', 'loss_target', 'm_ada_w', 'm_ada_b', 'm_norm_pre_g', 'm_norm_post_g', 'm_gla_w_in', 'm_gla_w_gate_up', 'm_gla_b_gate', 'm_gla_head_g', 'm_gla_w_out', 'm_mla_w_in', 'm_mla_q_norm_g', 'm_mla_w_uq', 'm_mla_kv_norm_g', 'm_mla_w_ukv', 'm_mla_w_out', 'm_gdn_w_in', 'm_gdn_conv_w', 'm_gdn_a_log', 'm_gdn_dt_bias', 'm_gdn_norm_g', 'm_gdn_w_out', 'm_mlp_w_up', 'm_mlp_w_down', 'v_ada_w', 'v_ada_b', 'v_norm_pre_g', 'v_norm_post_g', 'v_gla_w_in', 'v_gla_w_gate_up', 'v_gla_b_gate', 'v_gla_head_g', 'v_gla_w_out', 'v_mla_w_in', 'v_mla_q_norm_g', 'v_mla_w_uq', 'v_mla_kv_norm_g', 'v_mla_w_ukv', 'v_mla_w_out', 'v_gdn_w_in', 'v_gdn_conv_w', 'v_gdn_a_log', 'v_gdn_dt_bias', 'v_gdn_norm_g', 'v_gdn_w_out', 'v_mlp_w_up', 'v_mlp_w_down']
TWIN_OUTPUTS = ['loss', 'grad_x', 'grad_ada_w', 'grad_ada_b', 'grad_norm_pre_g', 'grad_norm_post_g', 'grad_gla_w_in', 'grad_gla_w_gate_up', 'grad_gla_b_gate', 'grad_gla_head_g', 'grad_gla_w_out', 'grad_mla_w_in', 'grad_mla_q_norm_g', 'grad_mla_w_uq', 'grad_mla_kv_norm_g', 'grad_mla_w_ukv', 'grad_mla_w_out', 'grad_gdn_w_in', 'grad_gdn_conv_w', 'grad_gdn_a_log', 'grad_gdn_dt_bias', 'grad_gdn_norm_g', 'grad_gdn_w_out', 'grad_mlp_w_up', 'grad_mlp_w_down', 'delta_ada_w', 'delta_ada_b', 'delta_norm_pre_g', 'delta_norm_post_g', 'delta_gla_w_in', 'delta_gla_w_gate_up', 'delta_gla_b_gate', 'delta_gla_head_g', 'delta_gla_w_out', 'delta_mla_w_in', 'delta_mla_q_norm_g', 'delta_mla_w_uq', 'delta_mla_kv_norm_g', 'delta_mla_w_ukv', 'delta_mla_w_out', 'delta_gdn_w_in', 'delta_gdn_conv_w', 'delta_gdn_a_log', 'delta_gdn_dt_bias', 'delta_gdn_norm_g', 'delta_gdn_w_out', 'delta_mlp_w_up', 'delta_mlp_w_down', 'new_m_ada_w', 'new_m_ada_b', 'new_m_norm_pre_g', 'new_m_norm_post_g', 'new_m_gla_w_in', 'new_m_gla_w_gate_up', 'new_m_gla_b_gate', 'new_m_gla_head_g', 'new_m_gla_w_out', 'new_m_mla_w_in', 'new_m_mla_q_norm_g', 'new_m_mla_w_uq', 'new_m_mla_kv_norm_g', 'new_m_mla_w_ukv', 'new_m_mla_w_out', 'new_m_gdn_w_in', 'new_m_gdn_conv_w', 'new_m_gdn_a_log', 'new_m_gdn_dt_bias', 'new_m_gdn_norm_g', 'new_m_gdn_w_out', 'new_m_mlp_w_up', 'new_m_mlp_w_down', 'new_v_ada_w', 'new_v_ada_b', 'new_v_norm_pre_g', 'new_v_norm_post_g', 'new_v_gla_w_in', 'new_v_gla_w_gate_up', 'new_v_gla_b_gate', 'new_v_gla_head_g', 'new_v_gla_w_out', 'new_v_mla_w_in', 'new_v_mla_q_norm_g', 'new_v_mla_w_uq', 'new_v_mla_kv_norm_g', 'new_v_mla_w_ukv', 'new_v_mla_w_out', 'new_v_gdn_w_in', 'new_v_gdn_conv_w', 'new_v_gdn_a_log', 'new_v_gdn_dt_bias', 'new_v_gdn_norm_g', 'new_v_gdn_w_out', 'new_v_mlp_w_up', 'new_v_mlp_w_down']
TWIN_LEAF_KINDS = {'loss': 'loss', 'grad_x': 'grad_x', 'grad_ada_w': 'grad_w', 'grad_ada_b': 'grad_w', 'grad_norm_pre_g': 'grad_w', 'grad_norm_post_g': 'grad_w', 'grad_gla_w_in': 'grad_w', 'grad_gla_w_gate_up': 'grad_w', 'grad_gla_b_gate': 'grad_w', 'grad_gla_head_g': 'grad_w', 'grad_gla_w_out': 'grad_w', 'grad_mla_w_in': 'grad_w', 'grad_mla_q_norm_g': 'grad_w', 'grad_mla_w_uq': 'grad_w', 'grad_mla_kv_norm_g': 'grad_w', 'grad_mla_w_ukv': 'grad_w', 'grad_mla_w_out': 'grad_w', 'grad_gdn_w_in': 'grad_w', 'grad_gdn_conv_w': 'grad_w', 'grad_gdn_a_log': 'grad_w', 'grad_gdn_dt_bias': 'grad_w', 'grad_gdn_norm_g': 'grad_w', 'grad_gdn_w_out': 'grad_w', 'grad_mlp_w_up': 'grad_w', 'grad_mlp_w_down': 'grad_w', 'delta_ada_w': 'delta_w', 'delta_ada_b': 'delta_w', 'delta_norm_pre_g': 'delta_w', 'delta_norm_post_g': 'delta_w', 'delta_gla_w_in': 'delta_w', 'delta_gla_w_gate_up': 'delta_w', 'delta_gla_b_gate': 'delta_w', 'delta_gla_head_g': 'delta_w', 'delta_gla_w_out': 'delta_w', 'delta_mla_w_in': 'delta_w', 'delta_mla_q_norm_g': 'delta_w', 'delta_mla_w_uq': 'delta_w', 'delta_mla_kv_norm_g': 'delta_w', 'delta_mla_w_ukv': 'delta_w', 'delta_mla_w_out': 'delta_w', 'delta_gdn_w_in': 'delta_w', 'delta_gdn_conv_w': 'delta_w', 'delta_gdn_a_log': 'delta_w', 'delta_gdn_dt_bias': 'delta_w', 'delta_gdn_norm_g': 'delta_w', 'delta_gdn_w_out': 'delta_w', 'delta_mlp_w_up': 'delta_w', 'delta_mlp_w_down': 'delta_w', 'new_m_ada_w': 'new_m', 'new_m_ada_b': 'new_m', 'new_m_norm_pre_g': 'new_m', 'new_m_norm_post_g': 'new_m', 'new_m_gla_w_in': 'new_m', 'new_m_gla_w_gate_up': 'new_m', 'new_m_gla_b_gate': 'new_m', 'new_m_gla_head_g': 'new_m', 'new_m_gla_w_out': 'new_m', 'new_m_mla_w_in': 'new_m', 'new_m_mla_q_norm_g': 'new_m', 'new_m_mla_w_uq': 'new_m', 'new_m_mla_kv_norm_g': 'new_m', 'new_m_mla_w_ukv': 'new_m', 'new_m_mla_w_out': 'new_m', 'new_m_gdn_w_in': 'new_m', 'new_m_gdn_conv_w': 'new_m', 'new_m_gdn_a_log': 'new_m', 'new_m_gdn_dt_bias': 'new_m', 'new_m_gdn_norm_g': 'new_m', 'new_m_gdn_w_out': 'new_m', 'new_m_mlp_w_up': 'new_m', 'new_m_mlp_w_down': 'new_m', 'new_v_ada_w': 'new_v', 'new_v_ada_b': 'new_v', 'new_v_norm_pre_g': 'new_v', 'new_v_norm_post_g': 'new_v', 'new_v_gla_w_in': 'new_v', 'new_v_gla_w_gate_up': 'new_v', 'new_v_gla_b_gate': 'new_v', 'new_v_gla_head_g': 'new_v', 'new_v_gla_w_out': 'new_v', 'new_v_mla_w_in': 'new_v', 'new_v_mla_q_norm_g': 'new_v', 'new_v_mla_w_uq': 'new_v', 'new_v_mla_kv_norm_g': 'new_v', 'new_v_mla_w_ukv': 'new_v', 'new_v_mla_w_out': 'new_v', 'new_v_gdn_w_in': 'new_v', 'new_v_gdn_conv_w': 'new_v', 'new_v_gdn_a_log': 'new_v', 'new_v_gdn_dt_bias': 'new_v', 'new_v_gdn_norm_g': 'new_v', 'new_v_gdn_w_out': 'new_v', 'new_v_mlp_w_up': 'new_v', 'new_v_mlp_w_down': 'new_v'}


def _forward(args):
    return _fwd_reference(*[args[k] for k in FWD_PARAMS])


def _output_shape():
    def fwd():
        inp = _fwd_setup_inputs(0)
        return _fwd_reference(*[inp[k] for k in FWD_PARAMS])
    out = _jax.eval_shape(fwd)
    return out.shape, out.dtype

N_MICROBATCH = 1
ADAM_LR = 0.001
ADAM_B1 = 0.9
ADAM_B2 = 0.999
ADAM_EPS = 1e-08
ADAM_WD = 0.01
ADAM_STEP = 10
PER_EXAMPLE_BATCH_AXIS = {'x': 0, 'c': 0, 'positions': 0, 'loss_target': 0}
SHARED_INPUTS = []
_WEIGHT_DTYPES = {'ada_w': _jnp.float32, 'ada_b': _jnp.float32, 'norm_pre_g': _jnp.float32, 'norm_post_g': _jnp.float32, 'gla_w_in': _jnp.float32, 'gla_w_gate_up': _jnp.float32, 'gla_b_gate': _jnp.float32, 'gla_head_g': _jnp.float32, 'gla_w_out': _jnp.float32, 'mla_w_in': _jnp.float32, 'mla_q_norm_g': _jnp.float32, 'mla_w_uq': _jnp.float32, 'mla_kv_norm_g': _jnp.float32, 'mla_w_ukv': _jnp.float32, 'mla_w_out': _jnp.float32, 'gdn_w_in': _jnp.float32, 'gdn_conv_w': _jnp.float32, 'gdn_a_log': _jnp.float32, 'gdn_dt_bias': _jnp.float32, 'gdn_norm_g': _jnp.float32, 'gdn_w_out': _jnp.float32, 'mlp_w_up': _jnp.float32, 'mlp_w_down': _jnp.float32}
MOMENT_SCALE = {'ada_w': 1.820969e+00, 'ada_b': 3.514818e+00, 'norm_pre_g': 3.657196e-01, 'norm_post_g': 3.581906e+00, 'gla_w_in': 2.024990e-01, 'gla_w_gate_up': 5.748133e-02, 'gla_b_gate': 1.431759e-01, 'gla_head_g': 4.114288e-01, 'gla_w_out': 1.888624e-01, 'mla_w_in': 2.283631e+00, 'mla_q_norm_g': 1.299202e-01, 'mla_w_uq': 7.200687e-02, 'mla_kv_norm_g': 3.546331e+00, 'mla_w_ukv': 1.342865e+00, 'mla_w_out': 1.865399e+00, 'gdn_w_in': 1.909326e-01, 'gdn_conv_w': 2.206894e-01, 'gdn_a_log': 5.031928e-01, 'gdn_dt_bias': 4.887639e-01, 'gdn_norm_g': 1.601358e+00, 'gdn_w_out': 4.911288e-01, 'mlp_w_up': 2.195163e-01, 'mlp_w_down': 8.782189e-01}


def _to_microbatches(a, axis):
    t = _jnp.moveaxis(a, axis, 0)
    t = t.reshape((N_MICROBATCH, t.shape[0] // N_MICROBATCH) + t.shape[1:])
    return _jnp.moveaxis(t, 1, axis + 1)


def setup_inputs(seed: int = 0) -> dict:
    inp = _fwd_setup_inputs(seed)
    key = _jax.random.fold_in(_jax.random.key(seed), 7919)
    shape, _ = _output_shape()
    out = dict(inp)
    out["loss_target"] = _jax.random.normal(_jax.random.fold_in(key, 0), shape, _jnp.float32)
    for i, name in enumerate(TWIN_WEIGHTS):
        w = inp[name].astype(_jnp.float32)
        if MOMENT_SCALE is None:
            s = _jnp.sqrt(_jnp.mean(_jnp.square(w)) + 1e-30)
        else:
            s = MOMENT_SCALE[name]
        km, kv = _jax.random.split(_jax.random.fold_in(key, i + 1))
        out[name] = w
        out["m_" + name] = s * _jax.random.normal(km, w.shape, _jnp.float32)
        out["v_" + name] = (s * s) * _jax.random.uniform(kv, w.shape, _jnp.float32, 0.5, 1.5)
    if N_MICROBATCH > 1:
        for name, axis in PER_EXAMPLE_BATCH_AXIS.items():
            out[name] = _to_microbatches(out[name], axis)
    return {'x': out['x'], 'c': out['c'], 'positions': out['positions'], 'ada_w': out['ada_w'], 'ada_b': out['ada_b'], 'norm_pre_g': out['norm_pre_g'], 'norm_post_g': out['norm_post_g'], 'gla_w_in': out['gla_w_in'], 'gla_w_gate_up': out['gla_w_gate_up'], 'gla_b_gate': out['gla_b_gate'], 'gla_head_g': out['gla_head_g'], 'gla_w_out': out['gla_w_out'], 'mla_w_in': out['mla_w_in'], 'mla_q_norm_g': out['mla_q_norm_g'], 'mla_w_uq': out['mla_w_uq'], 'mla_kv_norm_g': out['mla_kv_norm_g'], 'mla_w_ukv': out['mla_w_ukv'], 'mla_w_out': out['mla_w_out'], 'gdn_w_in': out['gdn_w_in'], 'gdn_conv_w': out['gdn_conv_w'], 'gdn_a_log': out['gdn_a_log'], 'gdn_dt_bias': out['gdn_dt_bias'], 'gdn_norm_g': out['gdn_norm_g'], 'gdn_w_out': out['gdn_w_out'], 'mlp_w_up': out['mlp_w_up'], 'mlp_w_down': out['mlp_w_down'], 'loss_target': out['loss_target'], 'm_ada_w': out['m_ada_w'], 'm_ada_b': out['m_ada_b'], 'm_norm_pre_g': out['m_norm_pre_g'], 'm_norm_post_g': out['m_norm_post_g'], 'm_gla_w_in': out['m_gla_w_in'], 'm_gla_w_gate_up': out['m_gla_w_gate_up'], 'm_gla_b_gate': out['m_gla_b_gate'], 'm_gla_head_g': out['m_gla_head_g'], 'm_gla_w_out': out['m_gla_w_out'], 'm_mla_w_in': out['m_mla_w_in'], 'm_mla_q_norm_g': out['m_mla_q_norm_g'], 'm_mla_w_uq': out['m_mla_w_uq'], 'm_mla_kv_norm_g': out['m_mla_kv_norm_g'], 'm_mla_w_ukv': out['m_mla_w_ukv'], 'm_mla_w_out': out['m_mla_w_out'], 'm_gdn_w_in': out['m_gdn_w_in'], 'm_gdn_conv_w': out['m_gdn_conv_w'], 'm_gdn_a_log': out['m_gdn_a_log'], 'm_gdn_dt_bias': out['m_gdn_dt_bias'], 'm_gdn_norm_g': out['m_gdn_norm_g'], 'm_gdn_w_out': out['m_gdn_w_out'], 'm_mlp_w_up': out['m_mlp_w_up'], 'm_mlp_w_down': out['m_mlp_w_down'], 'v_ada_w': out['v_ada_w'], 'v_ada_b': out['v_ada_b'], 'v_norm_pre_g': out['v_norm_pre_g'], 'v_norm_post_g': out['v_norm_post_g'], 'v_gla_w_in': out['v_gla_w_in'], 'v_gla_w_gate_up': out['v_gla_w_gate_up'], 'v_gla_b_gate': out['v_gla_b_gate'], 'v_gla_head_g': out['v_gla_head_g'], 'v_gla_w_out': out['v_gla_w_out'], 'v_mla_w_in': out['v_mla_w_in'], 'v_mla_q_norm_g': out['v_mla_q_norm_g'], 'v_mla_w_uq': out['v_mla_w_uq'], 'v_mla_kv_norm_g': out['v_mla_kv_norm_g'], 'v_mla_w_ukv': out['v_mla_w_ukv'], 'v_mla_w_out': out['v_mla_w_out'], 'v_gdn_w_in': out['v_gdn_w_in'], 'v_gdn_conv_w': out['v_gdn_conv_w'], 'v_gdn_a_log': out['v_gdn_a_log'], 'v_gdn_dt_bias': out['v_gdn_dt_bias'], 'v_gdn_norm_g': out['v_gdn_norm_g'], 'v_gdn_w_out': out['v_gdn_w_out'], 'v_mlp_w_up': out['v_mlp_w_up'], 'v_mlp_w_down': out['v_mlp_w_down']}


def _loss(weights, diff, rest, loss_target):
    with _jax.named_scope("forward"):
        args = {**rest, TWIN_DIFF_INPUT: diff, **{k: w.astype(_WEIGHT_DTYPES[k]) for k, w in weights.items()}}
        y = _forward(args)
    with _jax.named_scope("loss_head"):
        err = _jnp.square(y.astype(_jnp.float32) - loss_target)
        return 0.5 * _jnp.sum(_jnp.mean(err, axis=-1)) if err.ndim else 0.5 * err


def _adamw(w, g, m, v):
    m = ADAM_B1 * m + (1.0 - ADAM_B1) * g
    v = ADAM_B2 * v + (1.0 - ADAM_B2) * _jnp.square(g)
    m_hat = m / (1.0 - ADAM_B1 ** ADAM_STEP)
    v_hat = v / (1.0 - ADAM_B2 ** ADAM_STEP)
    delta = -ADAM_LR * (m_hat / (_jnp.sqrt(v_hat) + ADAM_EPS) + ADAM_WD * w)
    return delta, m, v


def reference(x, c, positions, ada_w, ada_b, norm_pre_g, norm_post_g, gla_w_in, gla_w_gate_up, gla_b_gate, gla_head_g, gla_w_out, mla_w_in, mla_q_norm_g, mla_w_uq, mla_kv_norm_g, mla_w_ukv, mla_w_out, gdn_w_in, gdn_conv_w, gdn_a_log, gdn_dt_bias, gdn_norm_g, gdn_w_out, mlp_w_up, mlp_w_down, loss_target, m_ada_w, m_ada_b, m_norm_pre_g, m_norm_post_g, m_gla_w_in, m_gla_w_gate_up, m_gla_b_gate, m_gla_head_g, m_gla_w_out, m_mla_w_in, m_mla_q_norm_g, m_mla_w_uq, m_mla_kv_norm_g, m_mla_w_ukv, m_mla_w_out, m_gdn_w_in, m_gdn_conv_w, m_gdn_a_log, m_gdn_dt_bias, m_gdn_norm_g, m_gdn_w_out, m_mlp_w_up, m_mlp_w_down, v_ada_w, v_ada_b, v_norm_pre_g, v_norm_post_g, v_gla_w_in, v_gla_w_gate_up, v_gla_b_gate, v_gla_head_g, v_gla_w_out, v_mla_w_in, v_mla_q_norm_g, v_mla_w_uq, v_mla_kv_norm_g, v_mla_w_ukv, v_mla_w_out, v_gdn_w_in, v_gdn_conv_w, v_gdn_a_log, v_gdn_dt_bias, v_gdn_norm_g, v_gdn_w_out, v_mlp_w_up, v_mlp_w_down):
    given = dict(x=x, c=c, positions=positions, ada_w=ada_w, ada_b=ada_b, norm_pre_g=norm_pre_g, norm_post_g=norm_post_g, gla_w_in=gla_w_in, gla_w_gate_up=gla_w_gate_up, gla_b_gate=gla_b_gate, gla_head_g=gla_head_g, gla_w_out=gla_w_out, mla_w_in=mla_w_in, mla_q_norm_g=mla_q_norm_g, mla_w_uq=mla_w_uq, mla_kv_norm_g=mla_kv_norm_g, mla_w_ukv=mla_w_ukv, mla_w_out=mla_w_out, gdn_w_in=gdn_w_in, gdn_conv_w=gdn_conv_w, gdn_a_log=gdn_a_log, gdn_dt_bias=gdn_dt_bias, gdn_norm_g=gdn_norm_g, gdn_w_out=gdn_w_out, mlp_w_up=mlp_w_up, mlp_w_down=mlp_w_down, loss_target=loss_target, m_ada_w=m_ada_w, m_ada_b=m_ada_b, m_norm_pre_g=m_norm_pre_g, m_norm_post_g=m_norm_post_g, m_gla_w_in=m_gla_w_in, m_gla_w_gate_up=m_gla_w_gate_up, m_gla_b_gate=m_gla_b_gate, m_gla_head_g=m_gla_head_g, m_gla_w_out=m_gla_w_out, m_mla_w_in=m_mla_w_in, m_mla_q_norm_g=m_mla_q_norm_g, m_mla_w_uq=m_mla_w_uq, m_mla_kv_norm_g=m_mla_kv_norm_g, m_mla_w_ukv=m_mla_w_ukv, m_mla_w_out=m_mla_w_out, m_gdn_w_in=m_gdn_w_in, m_gdn_conv_w=m_gdn_conv_w, m_gdn_a_log=m_gdn_a_log, m_gdn_dt_bias=m_gdn_dt_bias, m_gdn_norm_g=m_gdn_norm_g, m_gdn_w_out=m_gdn_w_out, m_mlp_w_up=m_mlp_w_up, m_mlp_w_down=m_mlp_w_down, v_ada_w=v_ada_w, v_ada_b=v_ada_b, v_norm_pre_g=v_norm_pre_g, v_norm_post_g=v_norm_post_g, v_gla_w_in=v_gla_w_in, v_gla_w_gate_up=v_gla_w_gate_up, v_gla_b_gate=v_gla_b_gate, v_gla_head_g=v_gla_head_g, v_gla_w_out=v_gla_w_out, v_mla_w_in=v_mla_w_in, v_mla_q_norm_g=v_mla_q_norm_g, v_mla_w_uq=v_mla_w_uq, v_mla_kv_norm_g=v_mla_kv_norm_g, v_mla_w_ukv=v_mla_w_ukv, v_mla_w_out=v_mla_w_out, v_gdn_w_in=v_gdn_w_in, v_gdn_conv_w=v_gdn_conv_w, v_gdn_a_log=v_gdn_a_log, v_gdn_dt_bias=v_gdn_dt_bias, v_gdn_norm_g=v_gdn_norm_g, v_gdn_w_out=v_gdn_w_out, v_mlp_w_up=v_mlp_w_up, v_mlp_w_down=v_mlp_w_down)
    weights = {n: given[n] for n in TWIN_WEIGHTS}
    shared = {n: given[n] for n in SHARED_INPUTS}
    per_example = {n: given[n] for n in ['x', 'c', 'positions']}
    grad_fn = _jax.value_and_grad(_loss, argnums=(0, 1))

    def one_microbatch(ex, loss_target):
        ex = dict(ex)
        diff = ex.pop(TWIN_DIFF_INPUT)
        return grad_fn(weights, diff, {**shared, **ex}, loss_target)

    if N_MICROBATCH == 1:
        loss, (grad_w, grad_x) = one_microbatch(per_example, given["loss_target"])
    else:
        def body(carry, xs):
            loss_sum, grad_sum = carry
            l_k, (gw_k, gx_k) = one_microbatch(xs[0], xs[1])
            with _jax.named_scope("update"):
                return (loss_sum + l_k, _jax.tree.map(_jnp.add, grad_sum, gw_k)), gx_k

        init = (_jnp.zeros((), _jnp.float32), _jax.tree.map(_jnp.zeros_like, weights))
        (loss, grad_w), grad_x = _jax.lax.scan(body, init, (per_example, given["loss_target"]))
    with _jax.named_scope("update"):
        delta_w, new_m, new_v = {}, {}, {}
        for n in TWIN_WEIGHTS:
            delta_w[n], new_m[n], new_v[n] = _adamw(weights[n], grad_w[n], given["m_" + n], given["v_" + n])
    return (loss, grad_x, *[grad_w[n] for n in TWIN_WEIGHTS], *[delta_w[n] for n in TWIN_WEIGHTS],
            *[new_m[n] for n in TWIN_WEIGHTS], *[new_v[n] for n in TWIN_WEIGHTS])
```

```python
import functools
import math

import jax
import jax.numpy as jnp
from jax import lax
from jax.experimental import pallas as pl
from jax.experimental.pallas import tpu as pltpu

F32 = jnp.float32
BF16 = jnp.bfloat16

D_MODEL = 1024
DEPTH = 4
CHUNK = 64
EPS = 1e-6
NEG_INF = -1e30
N_MOD = 6

GLA_HEADS, GLA_DK, GLA_DV, GLA_RANK = 4, 128, 256, 16
GLA_KW, GLA_VW = GLA_HEADS * GLA_DK, GLA_HEADS * GLA_DV
GLA_IN = 2 * GLA_KW + 2 * GLA_VW + GLA_RANK
GLA_IN_PAD = 3200

MLA_HEADS, MLA_NOPE, MLA_ROPE, MLA_V = 16, 64, 32, 64
MLA_Q_RANK, MLA_KV_RANK = 384, 256
MLA_IN = MLA_Q_RANK + MLA_KV_RANK + MLA_ROPE
MLA_IN_PAD = 768
ROPE_THETA = 10000.0
MLA_QK = MLA_NOPE + MLA_ROPE
LANES = 128

GDN_K_HEADS, GDN_V_HEADS, GDN_DK, GDN_DV, GDN_CONV = 8, 16, 128, 128, 4
GDN_QKW, GDN_VW = GDN_K_HEADS * GDN_DK, GDN_V_HEADS * GDN_DV
GDN_CONV_W = 2 * GDN_QKW + GDN_VW
GDN_IN = GDN_CONV_W + GDN_VW + 2 * GDN_V_HEADS
GDN_IN_PAD = 6400

ADAM_LR, ADAM_B1, ADAM_B2, ADAM_EPS, ADAM_WD, ADAM_STEP = 0.001, 0.9, 0.999, 1e-08, 0.01, 10

VMEM_LIMIT = 56 * 1024 * 1024

NN = ((1,), (0,))
NT = ((1,), (1,))
TN = ((0,), (0,))


def _cp(*sem):
    return pltpu.CompilerParams(dimension_semantics=sem, vmem_limit_bytes=VMEM_LIMIT)


def _pick(n, cands):
    for c in cands:
        if n % c == 0:
            return c
    return n


def _dg(a, b, dims=NN):
    return lax.dot_general(a.astype(BF16), b.astype(BF16), (dims, ((), ())), preferred_element_type=F32)


def _hi(a, b, dims=NN):
    return lax.dot_general(a, b, (dims, ((), ())), precision=lax.Precision.HIGHEST, preferred_element_type=F32)


def _dot3(a, b, dims=NN):
    ah = a.astype(BF16)
    al = (a - ah.astype(F32)).astype(BF16)
    bh = b.astype(BF16)
    bl = (b - bh.astype(F32)).astype(BF16)
    d = lambda u, v: lax.dot_general(u, v, (dims, ((), ())), preferred_element_type=F32)
    return d(ah, bh) + (d(ah, bl) + d(al, bh))


def _sigmoid(x):
    return 1.0 / (1.0 + jnp.exp(-x))


def _softplus(x):
    return jnp.maximum(x, 0.0) + jnp.log(1.0 + jnp.exp(-jnp.abs(x)))


def _iota2(shape, dim):
    return lax.broadcasted_iota(jnp.int32, shape, dim)


def _mm(a, b, *, ta=False, tb=False, out_dtype=F32, epi=None, aux=None, name):
    m = a.shape[1] if ta else a.shape[0]
    k = a.shape[0] if ta else a.shape[1]
    n = b.shape[0] if tb else b.shape[1]
    assert k == (b.shape[1] if tb else b.shape[0]), (a.shape, b.shape, ta, tb)
    tm = _pick(m, (512, 384, 256, 128))
    tn = _pick(n, (512, 640, 384, 256, 128))
    tk = _pick(k, (1024, 512, 768, 384, 256, 128))
    nk = k // tk
    dims = ((0 if ta else 1,), (1 if tb else 0,))

    def body(*refs):
        if aux is not None:
            a_ref, b_ref, x_ref, o_ref, acc = refs
        else:
            a_ref, b_ref, o_ref, acc = refs
        kk = pl.program_id(2)

        @pl.when(kk == 0)
        def _():
            acc[...] = jnp.zeros_like(acc)

        acc[...] += _dg(a_ref[...], b_ref[...], dims)

        @pl.when(kk == nk - 1)
        def _():
            r = acc[...]
            if epi == "relu2":
                r = jnp.square(jnp.maximum(r, 0.0))
            elif epi == "dact":
                r = r * (2.0 * jnp.sqrt(x_ref[...].astype(F32)))
            elif epi == "add":
                r = r + x_ref[...]
            o_ref[...] = r.astype(out_dtype)

    a_spec = pl.BlockSpec((tk, tm), lambda i, j, q: (q, i)) if ta else pl.BlockSpec((tm, tk), lambda i, j, q: (i, q))
    b_spec = pl.BlockSpec((tn, tk), lambda i, j, q: (j, q)) if tb else pl.BlockSpec((tk, tn), lambda i, j, q: (q, j))
    o_spec = pl.BlockSpec((tm, tn), lambda i, j, q: (i, j))
    in_specs = [a_spec, b_spec] + ([o_spec] if aux is not None else [])
    args = (a, b) + ((aux,) if aux is not None else ())
    return pl.pallas_call(
        body, grid=(m // tm, n // tn, nk), in_specs=in_specs, out_specs=o_spec,
        out_shape=jax.ShapeDtypeStruct((m, n), out_dtype), scratch_shapes=[pltpu.VMEM((tm, tn), F32)],
        compiler_params=_cp("parallel", "parallel", "arbitrary"), name=name)(*args)


def _row_tile(t):
    return _pick(t, (512, 256, 128, 64, 8))


def _premod_fwd(x, g, sc, sh, *, name):
    t, c = x.shape
    tr = _row_tile(t)

    def body(x_ref, g_ref, sc_ref, sh_ref, h_ref):
        xv = x_ref[...]
        r = lax.rsqrt(jnp.mean(xv * xv, axis=-1, keepdims=True) + EPS)
        h_ref[...] = (((xv * r) * g_ref[...]) * (1.0 + sc_ref[...]) + sh_ref[...]).astype(BF16)

    row = pl.BlockSpec((tr, c), lambda i: (i, 0))
    vec = pl.BlockSpec((1, c), lambda i: (0, 0))
    return pl.pallas_call(body, grid=(t // tr,), in_specs=[row, vec, vec, vec], out_specs=row,
                          out_shape=jax.ShapeDtypeStruct((t, c), BF16), compiler_params=_cp("parallel"), name=name)(x, g, sc, sh)


def _premod_bwd(dh, x, g, sc, gin, *, name):
    t, c = x.shape
    tr = _row_tile(t)

    def body(dh_ref, x_ref, g_ref, sc_ref, gin_ref, gout_ref, dg_ref, dsc_ref, dsh_ref):
        @pl.when(pl.program_id(0) == 0)
        def _():
            dg_ref[...] = jnp.zeros_like(dg_ref)
            dsc_ref[...] = jnp.zeros_like(dsc_ref)
            dsh_ref[...] = jnp.zeros_like(dsh_ref)

        xv = x_ref[...]
        dhv = dh_ref[...].astype(F32)
        gv = g_ref[...]
        one_sc = 1.0 + sc_ref[...]
        r = lax.rsqrt(jnp.mean(xv * xv, axis=-1, keepdims=True) + EPS)
        nv = xv * r
        dsh_ref[...] += jnp.sum(dhv, axis=0, keepdims=True)
        dsc_ref[...] += jnp.sum(dhv * (nv * gv), axis=0, keepdims=True)
        dg_ref[...] += jnp.sum(dhv * nv * one_sc, axis=0, keepdims=True)
        dn = dhv * gv * one_sc
        dx = r * (dn - nv * jnp.mean(dn * nv, axis=-1, keepdims=True))
        gout_ref[...] = gin_ref[...] + dx

    row = pl.BlockSpec((tr, c), lambda i: (i, 0))
    vec = pl.BlockSpec((1, c), lambda i: (0, 0))
    vs = jax.ShapeDtypeStruct((1, c), F32)
    return pl.pallas_call(body, grid=(t // tr,), in_specs=[row, row, vec, vec, row], out_specs=[row, vec, vec, vec],
                          out_shape=[jax.ShapeDtypeStruct((t, c), F32), vs, vs, vs],
                          compiler_params=_cp("arbitrary"), name=name)(dh, x, g, sc, gin)


def _postres_fwd(x, y, g, gt, *, name):
    t, c = x.shape
    tr = _row_tile(t)

    def body(x_ref, y_ref, g_ref, gt_ref, o_ref):
        yv = y_ref[...]
        r = lax.rsqrt(jnp.mean(yv * yv, axis=-1, keepdims=True) + EPS)
        o_ref[...] = x_ref[...] + gt_ref[...] * ((yv * r) * g_ref[...])

    row = pl.BlockSpec((tr, c), lambda i: (i, 0))
    vec = pl.BlockSpec((1, c), lambda i: (0, 0))
    return pl.pallas_call(body, grid=(t // tr,), in_specs=[row, row, vec, vec], out_specs=row,
                          out_shape=jax.ShapeDtypeStruct((t, c), F32), compiler_params=_cp("parallel"), name=name)(x, y, g, gt)


def _postres_bwd(gout, y, g, gt, *, name):
    t, c = y.shape
    tr = _row_tile(t)

    def body(go_ref, y_ref, g_ref, gt_ref, dy_ref, dg_ref, dgt_ref):
        @pl.when(pl.program_id(0) == 0)
        def _():
            dg_ref[...] = jnp.zeros_like(dg_ref)
            dgt_ref[...] = jnp.zeros_like(dgt_ref)

        yv = y_ref[...]
        gov = go_ref[...]
        gv = g_ref[...]
        gtv = gt_ref[...]
        r = lax.rsqrt(jnp.mean(yv * yv, axis=-1, keepdims=True) + EPS)
        z = yv * r
        dgt_ref[...] += jnp.sum(gov * (z * gv), axis=0, keepdims=True)
        dg_ref[...] += jnp.sum(gov * gtv * z, axis=0, keepdims=True)
        dz = gov * gtv * gv
        dy_ref[...] = (r * (dz - z * jnp.mean(dz * z, axis=-1, keepdims=True))).astype(BF16)

    row = pl.BlockSpec((tr, c), lambda i: (i, 0))
    vec = pl.BlockSpec((1, c), lambda i: (0, 0))
    vs = jax.ShapeDtypeStruct((1, c), F32)
    return pl.pallas_call(body, grid=(t // tr,), in_specs=[row, row, vec, vec], out_specs=[row, vec, vec],
                          out_shape=[jax.ShapeDtypeStruct((t, c), BF16), vs, vs],
                          compiler_params=_cp("arbitrary"), name=name)(gout, y, g, gt)


def _loss_head(y, tgt, *, name):
    t, c = y.shape
    tr = _row_tile(t)

    def body(y_ref, t_ref, l_ref, dy_ref):
        @pl.when(pl.program_id(0) == 0)
        def _():
            l_ref[...] = jnp.zeros_like(l_ref)

        d = y_ref[...] - t_ref[...]
        dy_ref[...] = d * (1.0 / c)
        l_ref[...] += 0.5 * jnp.sum(jnp.mean(d * d, axis=-1, keepdims=True))

    row = pl.BlockSpec((tr, c), lambda i: (i, 0))
    return pl.pallas_call(body, grid=(t // tr,), in_specs=[row, row],
                          out_specs=[pl.BlockSpec((1, LANES), lambda i: (0, 0)), row],
                          out_shape=[jax.ShapeDtypeStruct((1, LANES), F32), jax.ShapeDtypeStruct((t, c), F32)],
                          compiler_params=_cp("arbitrary"), name=name)(y, tgt)


def _adamw(w, g, m, v, *, name):
    shape = w.shape
    c = shape[-1]
    r = math.prod(shape[:-1])
    w2, g2, m2, v2 = (a.reshape(r, c) for a in (w, g, m, v))
    tr = r
    for cand in (1024, 512, 256, 128, 64, 32, 16, 8):
        if r % cand == 0 and cand * c * 4 <= (1 << 20):
            tr = cand
            break
    c1 = 1.0 - ADAM_B1 ** ADAM_STEP
    c2 = 1.0 - ADAM_B2 ** ADAM_STEP

    def body(w_ref, g_ref, m_ref, v_ref, d_ref, nm_ref, nv_ref):
        gv = g_ref[...]
        mn = ADAM_B1 * m_ref[...] + (1.0 - ADAM_B1) * gv
        vn = ADAM_B2 * v_ref[...] + (1.0 - ADAM_B2) * jnp.square(gv)
        m_hat = mn / c1
        v_hat = vn / c2
        d_ref[...] = -ADAM_LR * (m_hat / (jnp.sqrt(v_hat) + ADAM_EPS) + ADAM_WD * w_ref[...])
        nm_ref[...] = mn
        nv_ref[...] = vn

    blk = pl.BlockSpec((tr, c), lambda i: (i, 0))
    s = jax.ShapeDtypeStruct((r, c), F32)
    d, nm, nv = pl.pallas_call(body, grid=(r // tr,), in_specs=[blk] * 4, out_specs=[blk] * 3, out_shape=[s, s, s],
                               compiler_params=_cp("parallel"), name=name)(w2, g2, m2, v2)
    return d.reshape(shape), nm.reshape(shape), nv.reshape(shape)


def _gla_parts(p_ref, wg_ref, bg_ref):
    q = p_ref[:, 0:GLA_KW] * (GLA_DK ** -0.5)
    k = p_ref[:, GLA_KW:2 * GLA_KW]
    glr = p_ref[:, 2 * GLA_KW + 2 * GLA_VW:GLA_IN_PAD]
    gate = _dg(glr, wg_ref[...]) + bg_ref[...]
    log_a = (jnp.minimum(gate, 0.0) - jnp.log(1.0 + jnp.exp(-jnp.abs(gate)))) * (1.0 / 16.0)
    tril = (_iota2((CHUNK, CHUNK), 0) >= _iota2((CHUNK, CHUNK), 1)).astype(F32)
    cum = _hi(tril, log_a)
    c_last = cum[CHUNK - 1:CHUNK, :]
    f = jnp.exp(c_last - cum)
    dec = jnp.exp(c_last)
    return q, k, glr, gate, f, k * f, dec


def _gla_fwd(p, wg, bg, hg, *, name):
    t = p.shape[0]
    nc = t // CHUNK

    def body(p_ref, wg_ref, bg_ref, hg_ref, og_ref, s_ref, st):
        @pl.when(pl.program_id(0) == 0)
        def _():
            st[...] = jnp.zeros_like(st)

        q, _, _, _, _, ke, dec = _gla_parts(p_ref, wg_ref, bg_ref)
        for h in range(GLA_HEADS):
            ks = slice(h * GLA_DK, (h + 1) * GLA_DK)
            vs = slice(2 * GLA_KW + h * GLA_DV, 2 * GLA_KW + (h + 1) * GLA_DV)
            rs = slice(2 * GLA_KW + GLA_VW + h * GLA_DV, 2 * GLA_KW + GLA_VW + (h + 1) * GLA_DV)
            s_new = st[h] * dec[:, ks] + _dg(p_ref[:, vs], ke[:, ks], TN)
            st[h] = s_new
            s_ref[0, h] = s_new
            o = _dg(q[:, ks], s_new, NT)
            rn = lax.rsqrt(jnp.mean(o * o, axis=-1, keepdims=True) + EPS)
            rv = p_ref[:, rs]
            og_ref[:, h * GLA_DV:(h + 1) * GLA_DV] = (((o * rn) * hg_ref[...]) * (rv * _sigmoid(rv))).astype(BF16)

    full = lambda a: pl.BlockSpec(a.shape, lambda n: (0,) * a.ndim)
    return pl.pallas_call(
        body, grid=(nc,),
        in_specs=[pl.BlockSpec((CHUNK, GLA_IN_PAD), lambda n: (n, 0)), full(wg), full(bg), full(hg)],
        out_specs=[pl.BlockSpec((CHUNK, GLA_VW), lambda n: (n, 0)),
                   pl.BlockSpec((1, GLA_HEADS, GLA_DV, GLA_DK), lambda n: (n, 0, 0, 0))],
        out_shape=[jax.ShapeDtypeStruct((t, GLA_VW), BF16), jax.ShapeDtypeStruct((nc, GLA_HEADS, GLA_DV, GLA_DK), F32)],
        scratch_shapes=[pltpu.VMEM((GLA_HEADS, GLA_DV, GLA_DK), F32)],
        compiler_params=_cp("arbitrary"), name=name)(p, wg, bg, hg)


def _gla_bwd(p, dog, sall, wg, bg, hg, *, name):
    t = p.shape[0]
    nc = t // CHUNK

    def body(p_ref, dog_ref, s1_ref, s0_ref, wg_ref, bg_ref, hg_ref, dp_ref, dwg_ref, dbg_ref, dhg_ref, gt):
        i = pl.program_id(0)

        @pl.when(i == 0)
        def _():
            gt[...] = jnp.zeros_like(gt)
            dwg_ref[...] = jnp.zeros_like(dwg_ref)
            dbg_ref[...] = jnp.zeros_like(dbg_ref)
            dhg_ref[...] = jnp.zeros_like(dhg_ref)

        has_prev = (i < nc - 1).astype(F32)
        q, k, glr, gate, f, ke, dec = _gla_parts(p_ref, wg_ref, bg_ref)
        hgv = hg_ref[...]
        dke_parts, ddec_parts = [], []
        dhg = jnp.zeros((1, GLA_DV), F32)
        for h in range(GLA_HEADS):
            ks = slice(h * GLA_DK, (h + 1) * GLA_DK)
            vs = slice(2 * GLA_KW + h * GLA_DV, 2 * GLA_KW + (h + 1) * GLA_DV)
            rs = slice(2 * GLA_KW + GLA_VW + h * GLA_DV, 2 * GLA_KW + GLA_VW + (h + 1) * GLA_DV)
            s1 = s1_ref[0, h]
            s0 = s0_ref[0, h] * has_prev
            vh = p_ref[:, vs]
            rv = p_ref[:, rs]
            o = _dg(q[:, ks], s1, NT)
            rn = lax.rsqrt(jnp.mean(o * o, axis=-1, keepdims=True) + EPS)
            z = o * rn
            sg = _sigmoid(rv)
            sl = rv * sg
            dogh = dog_ref[:, h * GLA_DV:(h + 1) * GLA_DV].astype(F32)
            dhg = dhg + jnp.sum(dogh * z * sl, axis=0, keepdims=True)
            dp_ref[:, rs] = (dogh * (z * hgv) * (sg * (1.0 + rv * (1.0 - sg)))).astype(BF16)
            dz = dogh * sl * hgv
            do = rn * (dz - z * jnp.mean(dz * z, axis=-1, keepdims=True))
            g_tot = gt[h] + _dg(do, q[:, ks], TN)
            dp_ref[:, ks] = (_dg(do, s1, NN) * (GLA_DK ** -0.5)).astype(BF16)
            dke_parts.append(_dg(vh, g_tot, NN))
            dp_ref[:, vs] = _dg(ke[:, ks], g_tot, NT).astype(BF16)
            ddec_parts.append(jnp.sum(g_tot * s0, axis=0, keepdims=True))
            gt[h] = g_tot * dec[:, ks]
        dhg_ref[...] += dhg
        dke = jnp.concatenate(dke_parts, axis=1)
        ddec = jnp.concatenate(ddec_parts, axis=1)
        dp_ref[:, GLA_KW:2 * GLA_KW] = (dke * f).astype(BF16)
        stril = (_iota2((CHUNK, CHUNK), 0) > _iota2((CHUNK, CHUNK), 1)).astype(F32)
        dlog_a = _hi(stril, dke * ke) + ddec * dec
        dgate = dlog_a * (1.0 / 16.0) * _sigmoid(-gate)
        dp_ref[:, 2 * GLA_KW + 2 * GLA_VW:GLA_IN_PAD] = _dg(dgate, wg_ref[...], NT).astype(BF16)
        dwg_ref[...] += _dg(glr, dgate, TN)
        dbg_ref[...] += jnp.sum(dgate, axis=0, keepdims=True)

    full = lambda a: pl.BlockSpec(a.shape, lambda n: (0,) * a.ndim)
    sblk = (1, GLA_HEADS, GLA_DV, GLA_DK)
    return pl.pallas_call(
        body, grid=(nc,),
        in_specs=[pl.BlockSpec((CHUNK, GLA_IN_PAD), lambda n: (nc - 1 - n, 0)),
                  pl.BlockSpec((CHUNK, GLA_VW), lambda n: (nc - 1 - n, 0)),
                  pl.BlockSpec(sblk, lambda n: (nc - 1 - n, 0, 0, 0)),
                  pl.BlockSpec(sblk, lambda n: (jnp.maximum(nc - 2 - n, 0), 0, 0, 0)),
                  full(wg), full(bg), full(hg)],
        out_specs=[pl.BlockSpec((CHUNK, GLA_IN_PAD), lambda n: (nc - 1 - n, 0)), full(wg), full(bg), full(hg)],
        out_shape=[jax.ShapeDtypeStruct((t, GLA_IN_PAD), BF16), jax.ShapeDtypeStruct(wg.shape, F32),
                   jax.ShapeDtypeStruct(bg.shape, F32), jax.ShapeDtypeStruct(hg.shape, F32)],
        scratch_shapes=[pltpu.VMEM((GLA_HEADS, GLA_DV, GLA_DK), F32)],
        compiler_params=_cp("arbitrary"), name=name)(p, dog, sall, sall, wg, bg, hg)


def _gla_layer_fwd(h, w, tag):
    p = _mm(h, w["w_in"], name=tag + "_in")
    og, sall = _gla_fwd(p, w["wg"], w["bg"], w["hg"], name=tag + "_scan")
    y = _mm(og, w["w_out"], name=tag + "_out")
    return y, (p, og, sall)


def _gla_layer_bwd(dy, h, saved, w, tag):
    p, og, sall = saved
    dog = _mm(dy, w["w_out"], tb=True, name=tag + "_dog")
    dw_out = _mm(og, dy, ta=True, name=tag + "_dwout")
    dp, dwg, dbg, dhg = _gla_bwd(p, dog, sall, w["wg"], w["bg"], w["hg"], name=tag + "_scanb")
    dw_in = _mm(h, dp, ta=True, name=tag + "_dwin")
    dh = _mm(dp, w["w_in"], tb=True, name=tag + "_dh")
    grads = dict(w_in=dw_in[:, :GLA_IN], w_gate_up=dwg[:GLA_RANK], b_gate=dbg[0], head_g=dhg[0], w_out=dw_out)
    return dh, grads


def _rope_tables(pos, inv_freq, *, name):
    t = pos.shape[0]
    tr = _row_tile(t)
    half = MLA_ROPE // 2

    def body(p_ref, f_ref, c_ref, s1_ref, s2_ref, s1b_ref, s2b_ref):
        ang = p_ref[...].astype(F32) * f_ref[...]
        lane = _iota2((tr, LANES), 1)
        lo = (lane >= MLA_NOPE) & (lane < MLA_NOPE + half)
        hi = (lane >= MLA_NOPE + half) & (lane < MLA_QK)
        cs, sn = jnp.cos(ang), jnp.sin(ang)
        zero = jnp.zeros_like(cs)
        c_ref[...] = jnp.where(lane < MLA_NOPE, 1.0, jnp.where(lane < MLA_QK, cs, 0.0))
        s1_ref[...] = jnp.where(lo, -sn, zero)
        s2_ref[...] = jnp.where(hi, sn, zero)
        s1b_ref[...] = jnp.where(lo, sn, zero)
        s2b_ref[...] = jnp.where(hi, -sn, zero)

    row = pl.BlockSpec((tr, LANES), lambda i: (i, 0))
    s = jax.ShapeDtypeStruct((t, LANES), F32)
    return pl.pallas_call(body, grid=(t // tr,),
                          in_specs=[pl.BlockSpec((tr, 1), lambda i: (i, 0)), pl.BlockSpec((1, LANES), lambda i: (0, 0))],
                          out_specs=[row] * 5, out_shape=[s] * 5, compiler_params=_cp("parallel"), name=name)(pos, inv_freq)


def _rope(x, c, s1, s2, *, out_dtype, sum_heads=False, name):
    nh, t, _ = x.shape
    tr = _row_tile(t)
    half = MLA_ROPE // 2

    def body(x_ref, c_ref, s1_ref, s2_ref, o_ref):
        xv = x_ref[0].astype(F32)
        y = xv * c_ref[...] + pltpu.roll(xv, LANES - half, 1) * s1_ref[...] + pltpu.roll(xv, half, 1) * s2_ref[...]
        if sum_heads:
            @pl.when(pl.program_id(1) == 0)
            def _():
                o_ref[...] = jnp.zeros_like(o_ref)
            o_ref[...] += y
        else:
            o_ref[0] = y.astype(out_dtype)

    tab = pl.BlockSpec((tr, LANES), lambda i, h: (i, 0))
    xs = pl.BlockSpec((1, tr, LANES), lambda i, h: (h, i, 0))
    if sum_heads:
        return pl.pallas_call(body, grid=(t // tr, nh), in_specs=[xs, tab, tab, tab], out_specs=tab,
                              out_shape=jax.ShapeDtypeStruct((t, LANES), F32),
                              compiler_params=_cp("parallel", "arbitrary"), name=name)(x, c, s1, s2)
    return pl.pallas_call(body, grid=(t // tr, nh), in_specs=[xs, tab, tab, tab], out_specs=xs,
                          out_shape=jax.ShapeDtypeStruct(x.shape, out_dtype),
                          compiler_params=_cp("parallel", "parallel"), name=name)(x, c, s1, s2)


FLASH_BLK = 256


def _diag_mask(blk):
    return (_iota2((blk, blk), 1) // CHUNK) <= (_iota2((blk, blk), 0) // CHUNK)


def _flash_fwd(q, k, v, *, name):
    nh, t, _ = q.shape
    blk = min(FLASH_BLK, t)
    scale = MLA_QK ** -0.5

    def body(q_ref, k_ref, v_ref, o_ref, lse_ref):
        i = pl.program_id(1)
        qv = q_ref[0]

        def step(j, carry, masked):
            m, l, acc = carry
            off = pl.multiple_of(j * blk, blk)
            kb = k_ref[0, pl.ds(off, blk), :]
            vb = v_ref[0, pl.ds(off, blk), :]
            s = _dg(qv, kb, NT) * scale
            if masked:
                s = jnp.where(_diag_mask(blk), s, NEG_INF)
            m_new = jnp.maximum(m, jnp.max(s, axis=-1, keepdims=True))
            p = jnp.exp(s - m_new)
            alpha = jnp.exp(m - m_new)
            return m_new, alpha * l + jnp.sum(p, axis=-1, keepdims=True), alpha * acc + _dg(p, vb, NN)

        init = (jnp.full((blk, 1), NEG_INF, F32), jnp.zeros((blk, 1), F32), jnp.zeros((blk, MLA_V), F32))
        carry = lax.fori_loop(0, i, lambda j, c: step(j, c, False), init)
        m, l, acc = step(i, carry, True)
        o_ref[0] = (acc / l).astype(BF16)
        lse_ref[0] = m + jnp.log(l)

    qs = pl.BlockSpec((1, blk, LANES), lambda h, i: (h, i, 0))
    return pl.pallas_call(
        body, grid=(nh, t // blk),
        in_specs=[qs, pl.BlockSpec((1, t, LANES), lambda h, i: (h, 0, 0)), pl.BlockSpec((1, t, MLA_V), lambda h, i: (h, 0, 0))],
        out_specs=[pl.BlockSpec((1, blk, MLA_V), lambda h, i: (h, i, 0)), pl.BlockSpec((1, blk, 1), lambda h, i: (h, i, 0))],
        out_shape=[jax.ShapeDtypeStruct((nh, t, MLA_V), BF16), jax.ShapeDtypeStruct((nh, t, 1), F32)],
        compiler_params=_cp("parallel", "parallel"), name=name)(q, k, v)


def _flash_dq(q, k, v, do, o, lse, *, name):
    nh, t, _ = q.shape
    blk = min(FLASH_BLK, t)
    scale = MLA_QK ** -0.5

    def body(q_ref, k_ref, v_ref, do_ref, o_ref, lse_ref, dq_ref, dl_ref):
        i = pl.program_id(1)
        qv = q_ref[0]
        dov = do_ref[0]
        lse_v = lse_ref[0]
        delta = jnp.sum(dov.astype(F32) * o_ref[0].astype(F32), axis=-1, keepdims=True)
        dl_ref[0] = delta

        def step(j, dq, masked):
            off = pl.multiple_of(j * blk, blk)
            kb = k_ref[0, pl.ds(off, blk), :]
            vb = v_ref[0, pl.ds(off, blk), :]
            s = _dg(qv, kb, NT) * scale
            if masked:
                s = jnp.where(_diag_mask(blk), s, NEG_INF)
            p = jnp.exp(s - lse_v)
            ds = p * (_dg(dov, vb, NT) - delta) * scale
            return dq + _dg(ds, kb, NN)

        dq = lax.fori_loop(0, i, lambda j, c: step(j, c, False), jnp.zeros((blk, LANES), F32))
        dq_ref[0] = step(i, dq, True)

    qs = pl.BlockSpec((1, blk, LANES), lambda h, i: (h, i, 0))
    vs = pl.BlockSpec((1, blk, MLA_V), lambda h, i: (h, i, 0))
    ls = pl.BlockSpec((1, blk, 1), lambda h, i: (h, i, 0))
    return pl.pallas_call(
        body, grid=(nh, t // blk),
        in_specs=[qs, pl.BlockSpec((1, t, LANES), lambda h, i: (h, 0, 0)), pl.BlockSpec((1, t, MLA_V), lambda h, i: (h, 0, 0)), vs, vs, ls],
        out_specs=[qs, ls],
        out_shape=[jax.ShapeDtypeStruct((nh, t, LANES), F32), jax.ShapeDtypeStruct((nh, t, 1), F32)],
        compiler_params=_cp("parallel", "parallel"), name=name)(q, k, v, do, o, lse)


def _flash_dkv(q, k, v, do, lse, delta, *, name):
    nh, t, _ = q.shape
    blk = min(FLASH_BLK, t)
    nq = t // blk
    scale = MLA_QK ** -0.5

    def body(q_ref, k_ref, v_ref, do_ref, lse_ref, dl_ref, dk_ref, dv_ref):
        j = pl.program_id(1)
        kb = k_ref[0]
        vb = v_ref[0]

        def step(i, carry, masked):
            dk, dv = carry
            off = pl.multiple_of(i * blk, blk)
            qb = q_ref[0, pl.ds(off, blk), :]
            dob = do_ref[0, pl.ds(off, blk), :]
            s = _dg(qb, kb, NT) * scale
            if masked:
                s = jnp.where(_diag_mask(blk), s, NEG_INF)
            p = jnp.exp(s - lse_ref[0, pl.ds(off, blk), :])
            ds = p * (_dg(dob, vb, NT) - dl_ref[0, pl.ds(off, blk), :]) * scale
            return dk + _dg(ds, qb, TN), dv + _dg(p, dob, TN)

        carry = step(j, (jnp.zeros((blk, LANES), F32), jnp.zeros((blk, MLA_V), F32)), True)
        dk, dv = lax.fori_loop(j + 1, nq, lambda i, c: step(i, c, False), carry)
        dk_ref[0] = dk
        dv_ref[0] = dv

    ks = pl.BlockSpec((1, blk, LANES), lambda h, j: (h, j, 0))
    vs = pl.BlockSpec((1, blk, MLA_V), lambda h, j: (h, j, 0))
    fl = lambda w: pl.BlockSpec((1, t, w), lambda h, j: (h, 0, 0))
    return pl.pallas_call(
        body, grid=(nh, nq),
        in_specs=[fl(LANES), ks, vs, fl(MLA_V), fl(1), fl(1)],
        out_specs=[ks, vs],
        out_shape=[jax.ShapeDtypeStruct((nh, t, LANES), F32), jax.ShapeDtypeStruct((nh, t, MLA_V), F32)],
        compiler_params=_cp("parallel", "parallel"), name=name)(q, k, v, do, lse, delta)


def _heads_first(a, width):
    t = a.shape[0]
    return a.reshape(t, MLA_HEADS, width).transpose(1, 0, 2)


def _heads_last(a):
    return a.transpose(1, 0, 2).reshape(a.shape[1], -1)


def _mla_layer_fwd(h, pos, w, tag):
    t = h.shape[0]
    zq = jnp.zeros((1, MLA_Q_RANK), F32)
    zkv = jnp.zeros((1, MLA_KV_RANK), F32)
    p = _mm(h, w["w_in"], name=tag + "_in")
    cq, ckv, krp = p[:, :MLA_Q_RANK], p[:, MLA_Q_RANK:MLA_Q_RANK + MLA_KV_RANK], p[:, MLA_Q_RANK + MLA_KV_RANK:MLA_IN]
    qn = _premod_fwd(cq, w["q_norm_g"], zq, zq, name=tag + "_qnorm")
    kvn = _premod_fwd(ckv, w["kv_norm_g"], zkv, zkv, name=tag + "_kvnorm")
    q = _mm(qn, w["w_uq"], name=tag + "_uq")
    kv = _mm(kvn, w["w_ukv"], name=tag + "_ukv")
    q_pre = jnp.pad(_heads_first(q, MLA_QK), ((0, 0), (0, 0), (0, LANES - MLA_QK)))
    kv3 = _heads_first(kv, MLA_NOPE + MLA_V)
    k_pre = jnp.concatenate([kv3[:, :, :MLA_NOPE], jnp.broadcast_to(krp[None], (MLA_HEADS, t, MLA_ROPE)),
                             jnp.zeros((MLA_HEADS, t, LANES - MLA_QK), F32)], axis=-1)
    vh = kv3[:, :, MLA_NOPE:].astype(BF16)
    half = MLA_ROPE // 2
    freq = ROPE_THETA ** (-jnp.arange(half, dtype=F32) / half)
    inv_freq = jnp.concatenate([jnp.zeros((MLA_NOPE,), F32), freq, freq, jnp.zeros((LANES - MLA_QK,), F32)])[None]
    tabs = _rope_tables(pos.reshape(t, 1), inv_freq, name=tag + "_tables")
    qr = _rope(q_pre, tabs[0], tabs[1], tabs[2], out_dtype=BF16, name=tag + "_ropeq")
    kr = _rope(k_pre, tabs[0], tabs[1], tabs[2], out_dtype=BF16, name=tag + "_ropek")
    o, lse = _flash_fwd(qr, kr, vh, name=tag + "_attn")
    of = _heads_last(o)
    y = _mm(of, w["w_out"], name=tag + "_out")
    return y, (cq, ckv, qn, kvn, qr, kr, vh, o, lse, of, tabs)


def _mla_layer_bwd(dy, h, saved, w, tag):
    cq, ckv, qn, kvn, qr, kr, vh, o, lse, of, tabs = saved
    t = h.shape[0]
    zq = jnp.zeros((1, MLA_Q_RANK), F32)
    zkv = jnp.zeros((1, MLA_KV_RANK), F32)
    dof = _mm(dy, w["w_out"], tb=True, out_dtype=BF16, name=tag + "_dof")
    dw_out = _mm(of, dy, ta=True, name=tag + "_dwout")
    do = _heads_first(dof, MLA_V)
    dqr, delta = _flash_dq(qr, kr, vh, do, o, lse, name=tag + "_attn_dq")
    dkr, dv = _flash_dkv(qr, kr, vh, do, lse, delta, name=tag + "_attn_dkv")
    dq_pre = _rope(dqr, tabs[0], tabs[3], tabs[4], out_dtype=F32, name=tag + "_ropeq_b")
    dk_sum = _rope(dkr, tabs[0], tabs[3], tabs[4], out_dtype=F32, sum_heads=True, name=tag + "_ropek_b")
    dq = _heads_last(dq_pre[:, :, :MLA_QK])
    dkv = _heads_last(jnp.concatenate([dkr[:, :, :MLA_NOPE], dv], axis=-1))
    dw_uq = _mm(qn, dq, ta=True, name=tag + "_dwuq")
    dqn = _mm(dq, w["w_uq"], tb=True, name=tag + "_dqn")
    dw_ukv = _mm(kvn, dkv, ta=True, name=tag + "_dwukv")
    dkvn = _mm(dkv, w["w_ukv"], tb=True, name=tag + "_dkvn")
    dcq, dqg, _, _ = _premod_bwd(dqn, cq, w["q_norm_g"], zq, jnp.zeros_like(cq), name=tag + "_qnorm_b")
    dckv, dkvg, _, _ = _premod_bwd(dkvn, ckv, w["kv_norm_g"], zkv, jnp.zeros_like(ckv), name=tag + "_kvnorm_b")
    dp = jnp.concatenate([dcq, dckv, dk_sum[:, MLA_NOPE:MLA_QK], jnp.zeros((t, MLA_IN_PAD - MLA_IN), F32)], axis=1).astype(BF16)
    dw_in = _mm(h, dp, ta=True, name=tag + "_dwin")
    dh = _mm(dp, w["w_in"], tb=True, name=tag + "_dh")
    grads = dict(w_in=dw_in[:, :MLA_IN], q_norm_g=dqg[0], w_uq=dw_uq, kv_norm_g=dkvg[0], w_ukv=dw_ukv, w_out=dw_out)
    return dh, grads


CONV_HALO = 8


def _conv_tiles(t):
    return min(512, t), 512


def _gdn_conv_fwd(p, w, *, name):
    t = p.shape[0]
    tr, tc = _conv_tiles(t)
    hb = tr // CONV_HALO

    def body(x_ref, halo_ref, w_ref, o_ref, buf):
        i = pl.program_id(0)
        buf[0:CONV_HALO, :] = halo_ref[...] * (i > 0).astype(F32)
        buf[CONV_HALO:CONV_HALO + tr, :] = x_ref[...]
        base = CONV_HALO - (GDN_CONV - 1)
        acc = buf[pl.ds(base, tr), :] * w_ref[0:1, :]
        for j in range(1, GDN_CONV):
            acc = acc + buf[pl.ds(base + j, tr), :] * w_ref[j:j + 1, :]
        o_ref[...] = acc * _sigmoid(acc)

    return pl.pallas_call(
        body, grid=(t // tr, GDN_CONV_W // tc),
        in_specs=[pl.BlockSpec((tr, tc), lambda i, j: (i, j)),
                  pl.BlockSpec((CONV_HALO, tc), lambda i, j: (jnp.maximum(i * hb - 1, 0), j)),
                  pl.BlockSpec((GDN_CONV, tc), lambda i, j: (0, j))],
        out_specs=pl.BlockSpec((tr, tc), lambda i, j: (i, j)),
        out_shape=jax.ShapeDtypeStruct((t, GDN_CONV_W), F32),
        scratch_shapes=[pltpu.VMEM((tr + CONV_HALO, tc), F32)],
        compiler_params=_cp("parallel", "parallel"), name=name)(p, p, w)


def _gdn_conv_bwd(dqkv, p, w, *, name):
    t = p.shape[0]
    tr, tc = _conv_tiles(t)
    hb = tr // CONV_HALO
    nr = t // tr
    ext = tr + CONV_HALO

    def body(x_ref, xp_ref, xn_ref, d_ref, dn_ref, w_ref, dx_ref, dw_ref, bufx, bufd):
        i = pl.program_id(1)

        @pl.when(i == 0)
        def _():
            dw_ref[...] = jnp.zeros_like(dw_ref)

        last = (i < nr - 1).astype(F32)
        bufx[0:CONV_HALO, :] = xp_ref[...] * (i > 0).astype(F32)
        bufx[CONV_HALO:CONV_HALO + tr, :] = x_ref[...]
        bufx[CONV_HALO + tr:, :] = xn_ref[...] * last
        base = CONV_HALO - (GDN_CONV - 1)
        acc = bufx[pl.ds(base, ext), :] * w_ref[0:1, :]
        for j in range(1, GDN_CONV):
            acc = acc + bufx[pl.ds(base + j, ext), :] * w_ref[j:j + 1, :]
        sg = _sigmoid(acc)
        dsilu = sg * (1.0 + acc * (1.0 - sg))
        bufd[0:tr, :] = d_ref[...] * dsilu[0:tr, :]
        bufd[tr:, :] = dn_ref[...] * last * dsilu[tr:, :]
        dx = bufd[pl.ds(GDN_CONV - 1, tr), :] * w_ref[0:1, :]
        for j in range(1, GDN_CONV):
            dx = dx + bufd[pl.ds(GDN_CONV - 1 - j, tr), :] * w_ref[j:j + 1, :]
        dx_ref[...] = dx.astype(BF16)
        dc = bufd[0:tr, :]
        for j in range(GDN_CONV):
            dw_ref[j:j + 1, :] += jnp.sum(dc * bufx[pl.ds(base + j, tr), :], axis=0, keepdims=True)

    main = pl.BlockSpec((tr, tc), lambda j, i: (i, j))
    prev = pl.BlockSpec((CONV_HALO, tc), lambda j, i: (jnp.maximum(i * hb - 1, 0), j))
    nxt = pl.BlockSpec((CONV_HALO, tc), lambda j, i: (jnp.minimum((i + 1) * hb, t // CONV_HALO - 1), j))
    wsp = pl.BlockSpec((GDN_CONV, tc), lambda j, i: (0, j))
    return pl.pallas_call(
        body, grid=(GDN_CONV_W // tc, nr),
        in_specs=[main, prev, nxt, main, nxt, wsp], out_specs=[main, wsp],
        out_shape=[jax.ShapeDtypeStruct((t, GDN_CONV_W), BF16), jax.ShapeDtypeStruct((GDN_CONV, GDN_CONV_W), F32)],
        scratch_shapes=[pltpu.VMEM((tr + 2 * CONV_HALO, tc), F32), pltpu.VMEM((ext, tc), F32)],
        compiler_params=_cp("parallel", "arbitrary"), name=name)(p, p, p, dqkv, dqkv, w)


def _gdn_chunk(q_ref, k_ref, v_ref, braw_ref, araw_ref, alog_ref, dtb_ref):
    c = {}
    qraw, kraw = q_ref[...], k_ref[...]
    c["v"] = v_ref[...]
    c["rq"] = lax.rsqrt(jnp.sum(qraw * qraw, axis=-1, keepdims=True) + EPS)
    c["rk"] = lax.rsqrt(jnp.sum(kraw * kraw, axis=-1, keepdims=True) + EPS)
    c["qn"] = qraw * c["rq"]
    c["qh"] = c["qn"] * (GDN_DK ** -0.5)
    c["kh"] = kraw * c["rk"]
    c["beta"] = _sigmoid(braw_ref[0])
    c["ea"] = jnp.exp(alog_ref[0])
    c["xs"] = araw_ref[0] + dtb_ref[0]
    c["g"] = -c["ea"] * _softplus(c["xs"])
    row = _iota2((CHUNK, CHUNK), 0)
    col = _iota2((CHUNK, CHUNK), 1)
    c["strict"] = row > col
    c["triu"] = (row <= col).astype(F32)
    gb = jnp.broadcast_to(c["g"], (CHUNK, LANES))
    cum = _hi((row >= col).astype(F32), gb)
    cum_j = _hi(jnp.ones((CHUNK, CHUNK), F32), gb[:, :CHUNK] * c["triu"])
    diff = jnp.where(c["strict"], cum[:, :CHUNK] - cum_j, 0.0)
    c["dm"] = jnp.where(c["strict"], jnp.exp(diff), 0.0)
    c["kk"] = _dg(c["kh"], c["kh"], NT)
    c["a"] = (c["beta"] * c["dm"]) * c["kk"]
    c_last = cum[CHUNK - 1:CHUNK, :]
    c["e"] = jnp.exp(cum)
    c["f"] = jnp.exp(c_last - cum)
    c["dec"] = jnp.exp(c_last)
    c["rv"] = c["beta"] * c["v"]
    c["rk_rhs"] = (c["beta"] * c["e"]) * c["kh"]
    c["ke"] = c["kh"] * c["f"]
    return c


def _unit_lower_inverse(a):
    eye = (_iota2((CHUNK, CHUNK), 0) == _iota2((CHUNK, CHUNK), 1)).astype(F32)
    t = eye - a
    pw = a
    for _ in range(5):
        pw = _dot3(pw, pw)
        t = t + _dot3(t, pw)
    return t


def _gdn_specs(nc):
    qs = pl.BlockSpec((CHUNK, GDN_DK), lambda h, n: (n, h // 2))
    ks = pl.BlockSpec((CHUNK, GDN_DK), lambda h, n: (n, GDN_K_HEADS + h // 2))
    vs = pl.BlockSpec((CHUNK, GDN_DV), lambda h, n: (n, 2 * GDN_K_HEADS + h))
    zs = pl.BlockSpec((CHUNK, GDN_DV), lambda h, n: (n, GDN_CONV_W // GDN_DV + h))
    col = pl.BlockSpec((1, CHUNK, 1), lambda h, n: (h, n, 0))
    one = pl.BlockSpec((1, 1, 1), lambda h, n: (h, 0, 0))
    ng = pl.BlockSpec((1, GDN_DV), lambda h, n: (0, 0))
    return qs, ks, vs, zs, col, one, ng


def _gdn_fwd(qkv, p, braw, araw, alog, dtb, ng, *, name):
    t = qkv.shape[0]
    nc = t // CHUNK
    nh = GDN_V_HEADS

    def body(q_ref, k_ref, v_ref, z_ref, braw_ref, araw_ref, alog_ref, dtb_ref, ng_ref, og_ref, s_ref, t_ref, st):
        @pl.when(pl.program_id(1) == 0)
        def _():
            st[...] = jnp.zeros_like(st)

        c = _gdn_chunk(q_ref, k_ref, v_ref, braw_ref, araw_ref, alog_ref, dtb_ref)
        tm = _unit_lower_inverse(c["a"])
        t_ref[0, 0] = tm
        s0 = st[...]
        u = _dot3(tm, c["rv"]) - _dg(_dot3(tm, c["rk_rhs"]), s0, NN)
        s1 = c["dec"] * s0 + _dg(c["ke"], u, TN)
        st[...] = s1
        s_ref[0, 0] = s1
        o = _dg(c["qh"], s1, NN)
        rn = lax.rsqrt(jnp.mean(o * o, axis=-1, keepdims=True) + EPS)
        zv = z_ref[...]
        og_ref[...] = (((o * rn) * ng_ref[...]) * (zv * _sigmoid(zv))).astype(BF16)

    qs, ks, vs, zs, col, one, ngs = _gdn_specs(nc)
    return pl.pallas_call(
        body, grid=(nh, nc),
        in_specs=[qs, ks, vs, zs, col, col, one, one, ngs],
        out_specs=[pl.BlockSpec((CHUNK, GDN_DV), lambda h, n: (n, h)),
                   pl.BlockSpec((1, 1, GDN_DK, GDN_DV), lambda h, n: (h, n, 0, 0)),
                   pl.BlockSpec((1, 1, CHUNK, CHUNK), lambda h, n: (h, n, 0, 0))],
        out_shape=[jax.ShapeDtypeStruct((t, GDN_VW), BF16), jax.ShapeDtypeStruct((nh, nc, GDN_DK, GDN_DV), F32),
                   jax.ShapeDtypeStruct((nh, nc, CHUNK, CHUNK), F32)],
        scratch_shapes=[pltpu.VMEM((GDN_DK, GDN_DV), F32)],
        compiler_params=_cp("parallel", "arbitrary"), name=name)(qkv, qkv, qkv, p, braw, araw, alog, dtb, ng)


def _gdn_bwd(qkv, p, braw, araw, alog, dtb, ng, dog, sall, tall, *, name):
    t = qkv.shape[0]
    nc = t // CHUNK
    nh = GDN_V_HEADS

    def body(q_ref, k_ref, v_ref, z_ref, braw_ref, araw_ref, alog_ref, dtb_ref, ng_ref, dog_ref, s1_ref, s0_ref, t_ref,
             dq_ref, dk_ref, dv_ref, dz_ref, dbraw_ref, daraw_ref, dalog_ref, ddtb_ref, dng_ref, gc):
        hh = pl.program_id(0)
        i = pl.program_id(1)

        @pl.when(i == 0)
        def _():
            gc[...] = jnp.zeros_like(gc)
            dalog_ref[...] = jnp.zeros_like(dalog_ref)
            ddtb_ref[...] = jnp.zeros_like(ddtb_ref)

        @pl.when((i == 0) & (hh == 0))
        def _():
            dng_ref[...] = jnp.zeros_like(dng_ref)

        c = _gdn_chunk(q_ref, k_ref, v_ref, braw_ref, araw_ref, alog_ref, dtb_ref)
        beta, kh, qh, dm, kk, e, f, dec, ke = c["beta"], c["kh"], c["qh"], c["dm"], c["kk"], c["e"], c["f"], c["dec"], c["ke"]
        tm = t_ref[0, 0]
        s1 = s1_ref[0, 0]
        s0 = s0_ref[0, 0] * (i < nc - 1).astype(F32)
        wv = _dot3(tm, c["rv"])
        wk = _dot3(tm, c["rk_rhs"])
        u = wv - _dg(wk, s0, NN)
        o = _dg(qh, s1, NN)
        ngv = ng_ref[...]
        zv = z_ref[...]
        dogv = dog_ref[...]
        rn = lax.rsqrt(jnp.mean(o * o, axis=-1, keepdims=True) + EPS)
        zo = o * rn
        sg = _sigmoid(zv)
        sl = zv * sg
        dng_ref[...] += jnp.sum(dogv * zo * sl, axis=0, keepdims=True)
        dz_ref[...] = dogv * (zo * ngv) * (sg * (1.0 + zv * (1.0 - sg)))
        dzo = dogv * sl * ngv
        do = rn * (dzo - zo * jnp.mean(dzo * zo, axis=-1, keepdims=True))
        g_tot = gc[...] + _dg(qh, do, TN)
        dqh = _dg(do, s1, NT)
        dke = _dg(u, g_tot, NT)
        du = _dg(ke, g_tot, NN)
        ddec = jnp.sum(jnp.sum(g_tot * s0, axis=1, keepdims=True), axis=0, keepdims=True)
        gc[...] = dec * g_tot - _dg(wk, du, TN)
        dwk = -_dg(du, s0, NT)
        drv = _dot3(tm, du, TN)
        drk = _dot3(tm, dwk, TN)
        da = jnp.where(c["strict"], -(_dot3(drv, wv, NT) + _dot3(drk, wk, NT)), 0.0)
        mx = da * dm * kk
        dbeta = jnp.sum(mx, axis=1, keepdims=True)
        aa = mx * beta
        ones = jnp.ones((CHUNK, LANES), F32)
        dcum = jnp.sum(aa, axis=1, keepdims=True) - _hi(aa, ones, TN)[:, 0:1]
        bm = (da * beta) * dm
        dkh = _dg(bm, kh, NN) + _dg(bm, kh, TN)
        dv_ref[...] = beta * drv
        dbeta = dbeta + jnp.sum(drv * c["v"], axis=1, keepdims=True)
        dkh = dkh + (beta * e) * drk
        dbeta = dbeta + jnp.sum(drk * (e * kh), axis=1, keepdims=True)
        dcum = dcum + jnp.sum(drk * c["rk_rhs"], axis=1, keepdims=True)
        dkh = dkh + f * dke
        ef = jnp.sum(dke * ke, axis=1, keepdims=True)
        dcum = dcum - ef
        dcl = jnp.sum(ef, axis=0, keepdims=True) + ddec * dec[:, 0:1]
        dg = _hi(c["triu"], jnp.broadcast_to(dcum, (CHUNK, LANES)))[:, 0:1] + dcl
        daraw = dg * (-c["ea"]) * _sigmoid(c["xs"])
        daraw_ref[0] = daraw
        dbraw_ref[0] = dbeta * beta * (1.0 - beta)
        dalog_ref[0] += jnp.sum(dg * c["g"], axis=0, keepdims=True)
        ddtb_ref[0] += jnp.sum(daraw, axis=0, keepdims=True)
        dn = dqh * (GDN_DK ** -0.5)
        dq_ref[...] = c["rq"] * (dn - c["qn"] * jnp.sum(dn * c["qn"], axis=-1, keepdims=True))
        dk_ref[...] = c["rk"] * (dkh - kh * jnp.sum(dkh * kh, axis=-1, keepdims=True))

    rev = lambda n: nc - 1 - n
    qs = pl.BlockSpec((CHUNK, GDN_DK), lambda h, n: (rev(n), h // 2))
    ks = pl.BlockSpec((CHUNK, GDN_DK), lambda h, n: (rev(n), GDN_K_HEADS + h // 2))
    vs = pl.BlockSpec((CHUNK, GDN_DV), lambda h, n: (rev(n), 2 * GDN_K_HEADS + h))
    zs = pl.BlockSpec((CHUNK, GDN_DV), lambda h, n: (rev(n), GDN_CONV_W // GDN_DV + h))
    col = pl.BlockSpec((1, CHUNK, 1), lambda h, n: (h, rev(n), 0))
    one = pl.BlockSpec((1, 1, 1), lambda h, n: (h, 0, 0))
    ngs = pl.BlockSpec((1, GDN_DV), lambda h, n: (0, 0))
    hd = pl.BlockSpec((CHUNK, GDN_DV), lambda h, n: (rev(n), h))
    s1s = pl.BlockSpec((1, 1, GDN_DK, GDN_DV), lambda h, n: (h, rev(n), 0, 0))
    s0s = pl.BlockSpec((1, 1, GDN_DK, GDN_DV), lambda h, n: (h, jnp.maximum(rev(n) - 1, 0), 0, 0))
    ts = pl.BlockSpec((1, 1, CHUNK, CHUNK), lambda h, n: (h, rev(n), 0, 0))
    big = jax.ShapeDtypeStruct((t, GDN_VW), F32)
    cols = jax.ShapeDtypeStruct((nh, t, 1), F32)
    ones_s = jax.ShapeDtypeStruct((nh, 1, 1), F32)
    return pl.pallas_call(
        body, grid=(nh, nc),
        in_specs=[qs, ks, vs, zs, col, col, one, one, ngs, hd, s1s, s0s, ts],
        out_specs=[hd, hd, hd, hd, col, col, one, one, ngs],
        out_shape=[big, big, big, big, cols, cols, ones_s, ones_s, jax.ShapeDtypeStruct((1, GDN_DV), F32)],
        scratch_shapes=[pltpu.VMEM((GDN_DK, GDN_DV), F32)],
        compiler_params=_cp("arbitrary", "arbitrary"), name=name)(qkv, qkv, qkv, p, braw, araw, alog, dtb, ng, dog, sall, sall, tall)


def _gdn_layer_fwd(h, w, tag):
    p = _mm(h, w["w_in"], name=tag + "_in")
    qkv = _gdn_conv_fwd(p, w["conv_w"], name=tag + "_conv")
    braw = p[:, GDN_CONV_W + GDN_VW:GDN_CONV_W + GDN_VW + GDN_V_HEADS].T[:, :, None]
    araw = p[:, GDN_CONV_W + GDN_VW + GDN_V_HEADS:GDN_IN].T[:, :, None]
    og, sall, tall = _gdn_fwd(qkv, p, braw, araw, w["a_log"], w["dt_bias"], w["norm_g"], name=tag + "_scan")
    y = _mm(og, w["w_out"], name=tag + "_out")
    return y, (p, qkv, braw, araw, og, sall, tall)


def _gdn_layer_bwd(dy, h, saved, w, tag):
    p, qkv, braw, araw, og, sall, tall = saved
    t = h.shape[0]
    dog = _mm(dy, w["w_out"], tb=True, name=tag + "_dog")
    dw_out = _mm(og, dy, ta=True, name=tag + "_dwout")
    dq16, dk16, dv, dz, dbraw, daraw, dalog, ddtb, dng = _gdn_bwd(
        qkv, p, braw, araw, w["a_log"], w["dt_bias"], w["norm_g"], dog, sall, tall, name=tag + "_scanb")
    pair = lambda a: a.reshape(t, GDN_K_HEADS, 2, GDN_DK).sum(axis=2).reshape(t, GDN_QKW)
    dqkv = jnp.concatenate([pair(dq16), pair(dk16), dv], axis=1)
    dpre, dconv_w = _gdn_conv_bwd(dqkv, p, w["conv_w"], name=tag + "_convb")
    dp = jnp.concatenate([dpre, dz.astype(BF16), dbraw[:, :, 0].T.astype(BF16), daraw[:, :, 0].T.astype(BF16),
                          jnp.zeros((t, GDN_IN_PAD - GDN_IN), BF16)], axis=1)
    dw_in = _mm(h, dp, ta=True, name=tag + "_dwin")
    dh = _mm(dp, w["w_in"], tb=True, name=tag + "_dh")
    grads = dict(w_in=dw_in[:, :GDN_IN], conv_w=dconv_w, a_log=dalog[:, 0, 0], dt_bias=ddtb[:, 0, 0], norm_g=dng[0], w_out=dw_out)
    return dh, grads


MESH_ID = pl.DeviceIdType.MESH
FLAT_W = 1024
FLAT_ROWS = 13056
FLAT_TILE = 384


def _exchange(name, ins, out_shapes, plan, n_remote, n_local):
    def body(*refs):
        in_refs = refs[:len(ins)]
        out_refs = refs[len(ins):len(ins) + len(out_shapes)]
        ssem, rsem, lsem = refs[len(ins) + len(out_shapes):]
        x, y, c = lax.axis_index("x"), lax.axis_index("y"), lax.axis_index("c")
        stages, local_copies = plan(x, y, c, in_refs, out_refs)
        assert sum(len(s) for s in stages) == n_remote and len(local_copies) == n_local
        locs = [pltpu.make_async_copy(s, d, lsem.at[i]) for i, (s, d) in enumerate(local_copies)]
        for cp in locs:
            cp.start()
        sent = []
        k = 0
        for stage in stages:
            arrivals = []
            for src, dst, peer, landing in stage:
                cp = pltpu.make_async_remote_copy(src_ref=src, dst_ref=dst, send_sem=ssem.at[k], recv_sem=rsem.at[k],
                                                  device_id=peer, device_id_type=MESH_ID)
                cp.start()
                sent.append(cp)
                arrivals.append(pltpu.make_async_remote_copy(src_ref=src, dst_ref=landing, send_sem=ssem.at[k],
                                                             recv_sem=rsem.at[k], device_id=peer, device_id_type=MESH_ID))
                k += 1
            for cp in arrivals:
                cp.wait_recv()
        for cp in sent:
            cp.wait_send()
        for cp in locs:
            cp.wait()

    hbm = pl.BlockSpec(memory_space=pl.ANY)
    return pl.pallas_call(
        body, in_specs=[hbm] * len(ins), out_specs=[hbm] * len(out_shapes), out_shape=out_shapes,
        scratch_shapes=[pltpu.SemaphoreType.DMA((n_remote,)), pltpu.SemaphoreType.DMA((n_remote,)),
                        pltpu.SemaphoreType.DMA((max(n_local, 1),))],
        name=name)(*ins)


def _other_chips(x, y):
    return [(1 - x, y), (x, 1 - y), (1 - x, 1 - y)]


def _all8_gather(a, *, name):
    def plan(x, y, c, ins, outs):
        (src,), (dst,) = ins, outs
        me = 4 * x + 2 * y + c
        stage = []
        for fx, fy, fc in [(0, 0, 1), (0, 1, 0), (0, 1, 1), (1, 0, 0), (1, 0, 1), (1, 1, 0), (1, 1, 1)]:
            px, py, pc = (1 - x if fx else x), (1 - y if fy else y), (1 - c if fc else c)
            stage.append((src, dst.at[me], (px, py, pc), dst.at[4 * px + 2 * py + pc]))
        return [stage], [(src, dst.at[me])]

    return _exchange(name, [a], [jax.ShapeDtypeStruct((8,) + a.shape, a.dtype)], plan, 7, 1)[0]


def _chip_gather(flat, *, name):
    rows = flat.shape[0]
    half = rows // 2

    def plan(x, y, c, ins, outs):
        (src,), (dst,) = ins, outs
        me = 2 * x + y
        mine = pl.ds(c * half, half)
        theirs = pl.ds((1 - c) * half, half)
        ici = [(src.at[mine], dst.at[me, mine], (px, py, c), dst.at[2 * px + py, mine]) for px, py in _other_chips(x, y)]
        d2d = [(dst.at[2 * px + py, mine], dst.at[2 * px + py, mine], (x, y, 1 - c), dst.at[2 * px + py, theirs])
               for px, py in _other_chips(x, y)]
        return [ici, d2d], [(src, dst.at[me])]

    return _exchange(name, [flat], [jax.ShapeDtypeStruct((4,) + flat.shape, flat.dtype)], plan, 6, 1)[0]


def _add_sibling(gf, buf_a, core, *, name):
    _, rows, w = gf.shape
    half = rows // 2
    nb = half // FLAT_TILE

    def body(c_ref, g_ref, a_ref, o_ref):
        o_ref[...] = g_ref[...] + a_ref[...]

    blk = (1, FLAT_TILE, w)
    return pl.pallas_call(
        body,
        grid_spec=pltpu.PrefetchScalarGridSpec(
            num_scalar_prefetch=1, grid=(4, nb),
            in_specs=[pl.BlockSpec(blk, lambda s, i, c_ref: (s, c_ref[0] * nb + i, 0)), pl.BlockSpec(blk, lambda s, i, c_ref: (s, i, 0))],
            out_specs=pl.BlockSpec(blk, lambda s, i, c_ref: (s, i, 0))),
        out_shape=jax.ShapeDtypeStruct((4, half, w), F32), compiler_params=_cp("parallel", "parallel"), name=name)(core, gf, buf_a)


def _sum_slots(buf, *, name):
    n, rows, w = buf.shape
    tr = _pick(rows, (FLAT_TILE, 8))

    def body(b_ref, o_ref):
        acc = b_ref[0]
        for s in range(1, n):
            acc = acc + b_ref[s]
        o_ref[...] = acc

    return pl.pallas_call(body, grid=(rows // tr,), in_specs=[pl.BlockSpec((n, tr, w), lambda i: (0, i, 0))],
                          out_specs=pl.BlockSpec((tr, w), lambda i: (i, 0)), out_shape=jax.ShapeDtypeStruct((rows, w), F32),
                          compiler_params=_cp("parallel"), name=name)(buf)


def _reduce_scatter(gf, core, *, tag):
    _, rows, w = gf.shape
    half = rows // 2

    def plan_a(x, y, c, ins, outs):
        (src,), (dst,) = ins, outs
        return [[(src.at[:, pl.ds((1 - c) * half, half)], dst, (x, y, 1 - c), dst)]], []

    buf_a = _exchange(tag + "_sibling", [gf], [jax.ShapeDtypeStruct((4, half, w), F32)], plan_a, 1, 0)[0]
    hsum = _add_sibling(gf, buf_a, core, name=tag + "_add_sibling")

    def plan_b(x, y, c, ins, outs):
        (src,), (dst,) = ins, outs
        me = 2 * x + y
        stage = [(src.at[2 * px + py], dst.at[me], (px, py, c), dst.at[2 * px + py]) for px, py in _other_chips(x, y)]
        return [stage], [(src.at[me], dst.at[me])]

    buf_b = _exchange(tag + "_chips", [hsum], [jax.ShapeDtypeStruct((4, half, w), F32)], plan_b, 3, 1)[0]
    rh = _sum_slots(buf_b, name=tag + "_sum_chips")

    def plan_c(x, y, c, ins, outs):
        (src,), (dst,) = ins, outs
        return [[(src, dst.at[c], (x, y, 1 - c), dst.at[1 - c])]], [(src, dst.at[c])]

    rf = _exchange(tag + "_halves", [rh], [jax.ShapeDtypeStruct((2, half, w), F32)], plan_c, 1, 1)[0]
    return rf.reshape(rows, w)


WEIGHTS = ["ada_w", "ada_b", "norm_pre_g", "norm_post_g", "gla_w_in", "gla_w_gate_up", "gla_b_gate", "gla_head_g",
           "gla_w_out", "mla_w_in", "mla_q_norm_g", "mla_w_uq", "mla_kv_norm_g", "mla_w_ukv", "mla_w_out", "gdn_w_in",
           "gdn_conv_w", "gdn_a_log", "gdn_dt_bias", "gdn_norm_g", "gdn_w_out", "mlp_w_up", "mlp_w_down"]
PACK_BF16 = [("gla_w_in", 2), ("gla_w_out", 1), ("mla_w_in", 1), ("mla_w_uq", 2), ("mla_w_ukv", 2), ("mla_w_out", 1),
             ("gdn_w_in", 2), ("gdn_w_out", 1), ("mlp_w_up", 2), ("mlp_w_down", 1)]
PACK_F32 = [("norm_pre_g", 2), ("norm_post_g", 2), ("gla_w_gate_up", 2), ("gla_b_gate", 1), ("gla_head_g", 1), ("gdn_conv_w", 2)]
REPLICATED_SMALL = ["mla_q_norm_g", "mla_kv_norm_g", "gdn_a_log", "gdn_dt_bias", "gdn_norm_g"]
MIXERS = ["gla", "mla", "gdn"]


def _silu_rows(a, *, name):
    def body(a_ref, o_ref):
        v = a_ref[...]
        o_ref[...] = v * _sigmoid(v)

    return pl.pallas_call(body, out_shape=jax.ShapeDtypeStruct(a.shape, F32), name=name)(a)


def _pack_weights(shards):
    parts = [shards[n].astype(BF16).reshape(-1) for n, _ in PACK_BF16]
    parts += [lax.bitcast_convert_type(shards[n], BF16).reshape(-1) for n, _ in PACK_F32]
    flat = jnp.concatenate(parts)
    return jnp.pad(flat, (0, FLAT_ROWS * FLAT_W - flat.shape[0])).reshape(FLAT_ROWS, FLAT_W)


def _unpack_weights(gathered, shards):
    flat = gathered.reshape(4, -1)
    full, off = {}, 0
    for n, ax in PACK_BF16:
        size = shards[n].size
        seg = flat[:, off:off + size].reshape((4,) + shards[n].shape)
        full[n] = jnp.concatenate([seg[j] for j in range(4)], axis=ax)
        off += size
    for n, ax in PACK_F32:
        size = 2 * shards[n].size
        seg = lax.bitcast_convert_type(flat[:, off:off + size].reshape((4,) + shards[n].shape + (2,)), F32)
        full[n] = jnp.concatenate([seg[j] for j in range(4)], axis=ax)
        off += size
    return full


def _pack_grads(grads):
    parts = []
    for n, ax in PACK_BF16 + PACK_F32:
        parts.append(jnp.stack(jnp.split(grads[n].astype(F32), 4, axis=ax)).reshape(4, -1))
    flat = jnp.concatenate(parts, axis=1)
    return jnp.pad(flat, ((0, 0), (0, FLAT_ROWS * FLAT_W - flat.shape[1]))).reshape(4, FLAT_ROWS, FLAT_W)


def _unpack_grads(reduced, shards):
    flat = reduced.reshape(-1)
    out, off = {}, 0
    for n, _ in PACK_BF16 + PACK_F32:
        size = shards[n].size
        out[n] = flat[off:off + size].reshape(shards[n].shape)
        off += size
    return out


def _mixer_weights(kind, j, full, rep):
    if kind == "gla":
        return dict(w_in=jnp.pad(full["gla_w_in"][j], ((0, 0), (0, GLA_IN_PAD - GLA_IN))),
                    wg=jnp.pad(full["gla_w_gate_up"][j], ((0, LANES - GLA_RANK), (0, 0))),
                    bg=full["gla_b_gate"][j][None], hg=full["gla_head_g"][j][None], w_out=full["gla_w_out"][j])
    if kind == "mla":
        return dict(w_in=jnp.pad(full["mla_w_in"][j], ((0, 0), (0, MLA_IN_PAD - MLA_IN))), q_norm_g=rep["mla_q_norm_g"][j][None],
                    w_uq=full["mla_w_uq"][j], kv_norm_g=rep["mla_kv_norm_g"][j][None], w_ukv=full["mla_w_ukv"][j],
                    w_out=full["mla_w_out"][j])
    return dict(w_in=jnp.pad(full["gdn_w_in"][j], ((0, 0), (0, GDN_IN_PAD - GDN_IN))), conv_w=full["gdn_conv_w"][j],
                a_log=rep["gdn_a_log"][j][:, None, None], dt_bias=rep["gdn_dt_bias"][j][:, None, None],
                norm_g=rep["gdn_norm_g"][j][None], w_out=full["gdn_w_out"][j])


def _layer_fwd(xin, mod, gains, kind, mw, w_up, w_down, pos, tag):
    sh_m, sc_m, gt_m, sh_f, sc_f, gt_f = mod
    pre0, pre1, post0, post1 = gains
    h = _premod_fwd(xin, pre0, sc_m, sh_m, name=tag + "_pre0")
    if kind == "gla":
        y, saved = _gla_layer_fwd(h, mw, tag + "_gla")
    elif kind == "mla":
        y, saved = _mla_layer_fwd(h, pos, mw, tag + "_mla")
    else:
        y, saved = _gdn_layer_fwd(h, mw, tag + "_gdn")
    x1 = _postres_fwd(xin, y, post0, gt_m, name=tag + "_post0")
    h2 = _premod_fwd(x1, pre1, sc_f, sh_f, name=tag + "_pre1")
    act = _mm(h2, w_up, out_dtype=BF16, epi="relu2", name=tag + "_up")
    y2 = _mm(act, w_down, name=tag + "_down")
    x2 = _postres_fwd(x1, y2, post1, gt_f, name=tag + "_post1")
    return x2, (xin, h, y, saved, x1, h2, act, y2)


def _layer_bwd(g2, kept, mod, gains, kind, mw, w_up, w_down, tag):
    xin, h, y, saved, x1, h2, act, y2 = kept
    sh_m, sc_m, gt_m, sh_f, sc_f, gt_f = mod
    pre0, pre1, post0, post1 = gains
    dy2, dpost1, dgt_f = _postres_bwd(g2, y2, post1, gt_f, name=tag + "_post1_b")
    du = _mm(dy2, w_down, tb=True, out_dtype=BF16, epi="dact", aux=act, name=tag + "_du")
    dw_down = _mm(act, dy2, ta=True, name=tag + "_dwdown")
    dw_up = _mm(h2, du, ta=True, name=tag + "_dwup")
    dh2 = _mm(du, w_up, tb=True, name=tag + "_dh2")
    g1, dpre1, dsc_f, dsh_f = _premod_bwd(dh2, x1, pre1, sc_f, g2, name=tag + "_pre1_b")
    dy, dpost0, dgt_m = _postres_bwd(g1, y, post0, gt_m, name=tag + "_post0_b")
    if kind == "gla":
        dh, mg = _gla_layer_bwd(dy, h, saved, mw, tag + "_gla")
    elif kind == "mla":
        dh, mg = _mla_layer_bwd(dy, h, saved, mw, tag + "_mla")
    else:
        dh, mg = _gdn_layer_bwd(dy, h, saved, mw, tag + "_gdn")
    g0, dpre0, dsc_m, dsh_m = _premod_bwd(dh, xin, pre0, sc_m, g1, name=tag + "_pre0_b")
    dmod = jnp.concatenate([dsh_m, dsc_m, dgt_m, dsh_f, dsc_f, dgt_f], axis=1)
    return g0, dmod, jnp.concatenate([dpre0, dpre1], axis=0), jnp.concatenate([dpost0, dpost1], axis=0), mg, dw_up, dw_down


def kernel(x, c, positions, ada_w, ada_b, norm_pre_g, norm_post_g, gla_w_in, gla_w_gate_up, gla_b_gate, gla_head_g, gla_w_out, mla_w_in, mla_q_norm_g, mla_w_uq, mla_kv_norm_g, mla_w_ukv, mla_w_out, gdn_w_in, gdn_conv_w, gdn_a_log, gdn_dt_bias, gdn_norm_g, gdn_w_out, mlp_w_up, mlp_w_down, loss_target, m_ada_w, m_ada_b, m_norm_pre_g, m_norm_post_g, m_gla_w_in, m_gla_w_gate_up, m_gla_b_gate, m_gla_head_g, m_gla_w_out, m_mla_w_in, m_mla_q_norm_g, m_mla_w_uq, m_mla_kv_norm_g, m_mla_w_ukv, m_mla_w_out, m_gdn_w_in, m_gdn_conv_w, m_gdn_a_log, m_gdn_dt_bias, m_gdn_norm_g, m_gdn_w_out, m_mlp_w_up, m_mlp_w_down, v_ada_w, v_ada_b, v_norm_pre_g, v_norm_post_g, v_gla_w_in, v_gla_w_gate_up, v_gla_b_gate, v_gla_head_g, v_gla_w_out, v_mla_w_in, v_mla_q_norm_g, v_mla_w_uq, v_mla_kv_norm_g, v_mla_w_ukv, v_mla_w_out, v_gdn_w_in, v_gdn_conv_w, v_gdn_a_log, v_gdn_dt_bias, v_gdn_norm_g, v_gdn_w_out, v_mlp_w_up, v_mlp_w_down):
    w = dict(ada_w=ada_w, ada_b=ada_b, norm_pre_g=norm_pre_g, norm_post_g=norm_post_g, gla_w_in=gla_w_in,
             gla_w_gate_up=gla_w_gate_up, gla_b_gate=gla_b_gate, gla_head_g=gla_head_g, gla_w_out=gla_w_out, mla_w_in=mla_w_in,
             mla_q_norm_g=mla_q_norm_g, mla_w_uq=mla_w_uq, mla_kv_norm_g=mla_kv_norm_g, mla_w_ukv=mla_w_ukv, mla_w_out=mla_w_out,
             gdn_w_in=gdn_w_in, gdn_conv_w=gdn_conv_w, gdn_a_log=gdn_a_log, gdn_dt_bias=gdn_dt_bias, gdn_norm_g=gdn_norm_g,
             gdn_w_out=gdn_w_out, mlp_w_up=mlp_w_up, mlp_w_down=mlp_w_down)
    m = dict(zip(WEIGHTS, [m_ada_w, m_ada_b, m_norm_pre_g, m_norm_post_g, m_gla_w_in, m_gla_w_gate_up, m_gla_b_gate, m_gla_head_g,
                           m_gla_w_out, m_mla_w_in, m_mla_q_norm_g, m_mla_w_uq, m_mla_kv_norm_g, m_mla_w_ukv, m_mla_w_out,
                           m_gdn_w_in, m_gdn_conv_w, m_gdn_a_log, m_gdn_dt_bias, m_gdn_norm_g, m_gdn_w_out, m_mlp_w_up, m_mlp_w_down]))
    v = dict(zip(WEIGHTS, [v_ada_w, v_ada_b, v_norm_pre_g, v_norm_post_g, v_gla_w_in, v_gla_w_gate_up, v_gla_b_gate, v_gla_head_g,
                           v_gla_w_out, v_mla_w_in, v_mla_q_norm_g, v_mla_w_uq, v_mla_kv_norm_g, v_mla_w_ukv, v_mla_w_out,
                           v_gdn_w_in, v_gdn_conv_w, v_gdn_a_log, v_gdn_dt_bias, v_gdn_norm_g, v_gdn_w_out, v_mlp_w_up, v_mlp_w_down]))
    t = x.shape[1]
    ix, iy, ic = lax.axis_index("x"), lax.axis_index("y"), lax.axis_index("c")
    me = 4 * ix + 2 * iy + ic
    chip = 2 * ix + iy
    ada_cols = ada_w.shape[2]

    full = _unpack_weights(_chip_gather(_pack_weights(w), name="gather_weights"), w)

    cond8 = _silu_rows(jnp.pad(c, ((0, 7), (0, 0))), name="cond_silu")
    cond16 = jnp.pad(_all8_gather(cond8, name="gather_cond")[:, 0, :], ((0, 8), (0, 0)))
    mod_cols = []
    for layer in range(DEPTH):
        bias = jnp.broadcast_to(lax.dynamic_slice_in_dim(ada_b[layer], chip * ada_cols, ada_cols)[None], (16, ada_cols))
        mod_cols.append(_mm(cond16, ada_w[layer], epi="add", aux=bias, name=f"ada{layer}")[:8])
    mod_all = _all8_gather(jnp.stack(mod_cols).reshape(DEPTH * 8, ada_cols), name="gather_mod")
    mod = jnp.concatenate([lax.dynamic_slice_in_dim(mod_all[2 * j].reshape(DEPTH, 8, ada_cols), me, 1, axis=1)[:, 0]
                           for j in range(4)], axis=1)

    def layer_args(layer):
        kind, j = MIXERS[layer % 3], layer // 3
        mods = [mod[layer, i * D_MODEL:(i + 1) * D_MODEL][None] for i in range(N_MOD)]
        gains = (full["norm_pre_g"][layer, 0:1], full["norm_pre_g"][layer, 1:2], full["norm_post_g"][layer, 0:1],
                 full["norm_post_g"][layer, 1:2])
        return kind, j, mods, gains, _mixer_weights(kind, j, full, w)

    xs = x[0]
    kept = []
    for layer in range(DEPTH):
        kind, j, mods, gains, mw = layer_args(layer)
        xs, keep = _layer_fwd(xs, mods, gains, kind, mw, full["mlp_w_up"][layer], full["mlp_w_down"][layer], positions[0], f"l{layer}")
        kept.append(keep)
    loss_row, g = _loss_head(xs, loss_target[0], name="loss_head")
    loss = lax.psum(loss_row[0, 0], ("x", "y", "c"))

    grads = {n: [None] * w[n].shape[0] for n, _ in PACK_BF16 + PACK_F32}
    rep_grads = {}
    dmods = [None] * DEPTH
    for layer in reversed(range(DEPTH)):
        kind, j, mods, gains, mw = layer_args(layer)
        g, dmods[layer], dpre, dpost, mg, dw_up, dw_down = _layer_bwd(
            g, kept[layer], mods, gains, kind, mw, full["mlp_w_up"][layer], full["mlp_w_down"][layer], f"l{layer}")
        grads["norm_pre_g"][layer], grads["norm_post_g"][layer] = dpre, dpost
        grads["mlp_w_up"][layer], grads["mlp_w_down"][layer] = dw_up, dw_down
        for key, val in mg.items():
            name = kind + "_" + key
            if name in grads:
                grads[name][j] = val
            else:
                rep_grads[name] = val[None]
    grads = {n: jnp.stack(parts) for n, parts in grads.items()}

    rep_flat = jnp.concatenate([rep_grads[n].reshape(-1) for n in REPLICATED_SMALL])
    dbuf = jnp.concatenate([jnp.concatenate(dmods, axis=0), jnp.pad(rep_flat, (0, N_MOD * D_MODEL - rep_flat.shape[0]))[None],
                            jnp.zeros((3, N_MOD * D_MODEL), F32)], axis=0)
    dall = _all8_gather(dbuf, name="gather_dmod")
    dsum = _sum_slots(dall, name="sum_dmod")
    out_grads = {"ada_b": dsum[:DEPTH]}
    off = 0
    for n in REPLICATED_SMALL:
        out_grads[n] = dsum[DEPTH, off:off + w[n].size].reshape(w[n].shape)
        off += w[n].size
    dada = []
    for layer in range(DEPTH):
        dm16 = jnp.pad(lax.dynamic_slice_in_dim(dall[:, layer, :], chip * ada_cols, ada_cols, axis=1), ((0, 8), (0, 0)))
        dada.append(_mm(cond16, dm16, ta=True, name=f"dada{layer}"))
    out_grads["ada_w"] = jnp.stack(dada)

    reduced = _reduce_scatter(_pack_grads(grads), ic.reshape(1).astype(jnp.int32), tag="reduce_grads")
    out_grads.update(_unpack_grads(reduced, w))

    deltas, new_m, new_v = {}, {}, {}
    for n in WEIGHTS:
        deltas[n], new_m[n], new_v[n] = _adamw(w[n], out_grads[n], m[n], v[n], name="adamw_" + n)
    return (loss, g[None], *[out_grads[n] for n in WEIGHTS], *[deltas[n] for n in WEIGHTS],
            *[new_m[n] for n in WEIGHTS], *[new_v[n] for n in WEIGHTS])
```

```python
import functools
import math

import jax
import jax.numpy as jnp
from jax import lax
from jax.experimental import pallas as pl
from jax.experimental.pallas import tpu as pltpu

F32 = jnp.float32
BF16 = jnp.bfloat16

D_MODEL = 1024
DEPTH = 4
CHUNK = 64
EPS = 1e-6
NEG_INF = -1e30
N_MOD = 6

GLA_HEADS, GLA_DK, GLA_DV, GLA_RANK = 4, 128, 256, 16
GLA_KW, GLA_VW = GLA_HEADS * GLA_DK, GLA_HEADS * GLA_DV
GLA_IN = 2 * GLA_KW + 2 * GLA_VW + GLA_RANK
GLA_IN_PAD = 3200

MLA_HEADS, MLA_NOPE, MLA_ROPE, MLA_V = 16, 64, 32, 64
MLA_Q_RANK, MLA_KV_RANK = 384, 256
MLA_IN = MLA_Q_RANK + MLA_KV_RANK + MLA_ROPE
MLA_IN_PAD = 768
ROPE_THETA = 10000.0
MLA_QK = MLA_NOPE + MLA_ROPE
LANES = 128

GDN_K_HEADS, GDN_V_HEADS, GDN_DK, GDN_DV, GDN_CONV = 8, 16, 128, 128, 4
GDN_QKW, GDN_VW = GDN_K_HEADS * GDN_DK, GDN_V_HEADS * GDN_DV
GDN_CONV_W = 2 * GDN_QKW + GDN_VW
GDN_IN = GDN_CONV_W + GDN_VW + 2 * GDN_V_HEADS
GDN_IN_PAD = 6400

ADAM_LR, ADAM_B1, ADAM_B2, ADAM_EPS, ADAM_WD, ADAM_STEP = 0.001, 0.9, 0.999, 1e-08, 0.01, 10

VMEM_LIMIT = 56 * 1024 * 1024

NN = ((1,), (0,))
NT = ((1,), (1,))
TN = ((0,), (0,))


def _cp(*sem):
    return pltpu.CompilerParams(dimension_semantics=sem, vmem_limit_bytes=VMEM_LIMIT)


def _pick(n, cands):
    for c in cands:
        if n % c == 0:
            return c
    return n


def _dg(a, b, dims=NN):
    return lax.dot_general(a.astype(BF16), b.astype(BF16), (dims, ((), ())), preferred_element_type=F32)


def _hi(a, b, dims=NN):
    return lax.dot_general(a, b, (dims, ((), ())), precision=lax.Precision.HIGHEST, preferred_element_type=F32)


def _dot3(a, b, dims=NN):
    ah = a.astype(BF16)
    al = (a - ah.astype(F32)).astype(BF16)
    bh = b.astype(BF16)
    bl = (b - bh.astype(F32)).astype(BF16)
    d = lambda u, v: lax.dot_general(u, v, (dims, ((), ())), preferred_element_type=F32)
    return d(ah, bh) + (d(ah, bl) + d(al, bh))


def _sigmoid(x):
    return 1.0 / (1.0 + jnp.exp(-x))


def _softplus(x):
    return jnp.maximum(x, 0.0) + jnp.log(1.0 + jnp.exp(-jnp.abs(x)))


def _iota2(shape, dim):
    return lax.broadcasted_iota(jnp.int32, shape, dim)


def _mm(a, b, *, ta=False, tb=False, out_dtype=F32, epi=None, aux=None, name):
    m = a.shape[1] if ta else a.shape[0]
    k = a.shape[0] if ta else a.shape[1]
    n = b.shape[0] if tb else b.shape[1]
    assert k == (b.shape[1] if tb else b.shape[0]), (a.shape, b.shape, ta, tb)
    tm = _pick(m, (512, 384, 256, 128))
    tn = _pick(n, (512, 640, 384, 256, 128))
    tk = _pick(k, (1024, 640, 512, 768, 384, 256, 128))
    nk = k // tk
    dims = ((0 if ta else 1,), (1 if tb else 0,))

    def body(*refs):
        if aux is not None:
            a_ref, b_ref, x_ref, o_ref, acc = refs
        else:
            a_ref, b_ref, o_ref, acc = refs
        kk = pl.program_id(2)

        @pl.when(kk == 0)
        def _():
            acc[...] = jnp.zeros_like(acc)

        acc[...] += _dg(a_ref[...], b_ref[...], dims)

        @pl.when(kk == nk - 1)
        def _():
            r = acc[...]
            if epi == "relu2":
                r = jnp.square(jnp.maximum(r, 0.0))
            elif epi == "dact":
                r = r * (2.0 * jnp.sqrt(x_ref[...].astype(F32)))
            elif epi == "add":
                r = r + x_ref[...]
            o_ref[...] = r.astype(out_dtype)

    a_spec = pl.BlockSpec((tk, tm), lambda i, j, q: (q, i)) if ta else pl.BlockSpec((tm, tk), lambda i, j, q: (i, q))
    b_spec = pl.BlockSpec((tn, tk), lambda i, j, q: (j, q)) if tb else pl.BlockSpec((tk, tn), lambda i, j, q: (q, j))
    o_spec = pl.BlockSpec((tm, tn), lambda i, j, q: (i, j))
    in_specs = [a_spec, b_spec] + ([o_spec] if aux is not None else [])
    args = (a, b) + ((aux,) if aux is not None else ())
    return pl.pallas_call(
        body, grid=(m // tm, n // tn, nk), in_specs=in_specs, out_specs=o_spec,
        out_shape=jax.ShapeDtypeStruct((m, n), out_dtype), scratch_shapes=[pltpu.VMEM((tm, tn), F32)],
        compiler_params=_cp("parallel", "parallel", "arbitrary"), name=name)(*args)


def _row_tile(t):
    return _pick(t, (512, 256, 128, 64, 8))


def _premod_fwd(x, g, sc, sh, *, name):
    t, c = x.shape
    tr = _row_tile(t)

    def body(x_ref, g_ref, sc_ref, sh_ref, h_ref):
        xv = x_ref[...]
        r = lax.rsqrt(jnp.mean(xv * xv, axis=-1, keepdims=True) + EPS)
        h_ref[...] = (((xv * r) * g_ref[...]) * (1.0 + sc_ref[...]) + sh_ref[...]).astype(BF16)

    row = pl.BlockSpec((tr, c), lambda i: (i, 0))
    vec = pl.BlockSpec((1, c), lambda i: (0, 0))
    return pl.pallas_call(body, grid=(t // tr,), in_specs=[row, vec, vec, vec], out_specs=row,
                          out_shape=jax.ShapeDtypeStruct((t, c), BF16), compiler_params=_cp("parallel"), name=name)(x, g, sc, sh)


def _premod_bwd(dh, x, g, sc, gin, *, name):
    t, c = x.shape
    tr = _row_tile(t)

    def body(dh_ref, x_ref, g_ref, sc_ref, gin_ref, gout_ref, dg_ref, dsc_ref, dsh_ref):
        @pl.when(pl.program_id(0) == 0)
        def _():
            dg_ref[...] = jnp.zeros_like(dg_ref)
            dsc_ref[...] = jnp.zeros_like(dsc_ref)
            dsh_ref[...] = jnp.zeros_like(dsh_ref)

        xv = x_ref[...]
        dhv = dh_ref[...].astype(F32)
        gv = g_ref[...]
        one_sc = 1.0 + sc_ref[...]
        r = lax.rsqrt(jnp.mean(xv * xv, axis=-1, keepdims=True) + EPS)
        nv = xv * r
        dsh_ref[...] += jnp.sum(dhv, axis=0, keepdims=True)
        dsc_ref[...] += jnp.sum(dhv * (nv * gv), axis=0, keepdims=True)
        dg_ref[...] += jnp.sum(dhv * nv * one_sc, axis=0, keepdims=True)
        dn = dhv * gv * one_sc
        dx = r * (dn - nv * jnp.mean(dn * nv, axis=-1, keepdims=True))
        gout_ref[...] = gin_ref[...] + dx

    row = pl.BlockSpec((tr, c), lambda i: (i, 0))
    vec = pl.BlockSpec((1, c), lambda i: (0, 0))
    vs = jax.ShapeDtypeStruct((1, c), F32)
    return pl.pallas_call(body, grid=(t // tr,), in_specs=[row, row, vec, vec, row], out_specs=[row, vec, vec, vec],
                          out_shape=[jax.ShapeDtypeStruct((t, c), F32), vs, vs, vs],
                          compiler_params=_cp("arbitrary"), name=name)(dh, x, g, sc, gin)


def _postres_fwd(x, y, g, gt, *, name):
    t, c = x.shape
    tr = _row_tile(t)

    def body(x_ref, y_ref, g_ref, gt_ref, o_ref):
        yv = y_ref[...]
        r = lax.rsqrt(jnp.mean(yv * yv, axis=-1, keepdims=True) + EPS)
        o_ref[...] = x_ref[...] + gt_ref[...] * ((yv * r) * g_ref[...])

    row = pl.BlockSpec((tr, c), lambda i: (i, 0))
    vec = pl.BlockSpec((1, c), lambda i: (0, 0))
    return pl.pallas_call(body, grid=(t // tr,), in_specs=[row, row, vec, vec], out_specs=row,
                          out_shape=jax.ShapeDtypeStruct((t, c), F32), compiler_params=_cp("parallel"), name=name)(x, y, g, gt)


def _postres_bwd(gout, y, g, gt, *, name):
    t, c = y.shape
    tr = _row_tile(t)

    def body(go_ref, y_ref, g_ref, gt_ref, dy_ref, dg_ref, dgt_ref):
        @pl.when(pl.program_id(0) == 0)
        def _():
            dg_ref[...] = jnp.zeros_like(dg_ref)
            dgt_ref[...] = jnp.zeros_like(dgt_ref)

        yv = y_ref[...]
        gov = go_ref[...]
        gv = g_ref[...]
        gtv = gt_ref[...]
        r = lax.rsqrt(jnp.mean(yv * yv, axis=-1, keepdims=True) + EPS)
        z = yv * r
        dgt_ref[...] += jnp.sum(gov * (z * gv), axis=0, keepdims=True)
        dg_ref[...] += jnp.sum(gov * gtv * z, axis=0, keepdims=True)
        dz = gov * gtv * gv
        dy_ref[...] = (r * (dz - z * jnp.mean(dz * z, axis=-1, keepdims=True))).astype(BF16)

    row = pl.BlockSpec((tr, c), lambda i: (i, 0))
    vec = pl.BlockSpec((1, c), lambda i: (0, 0))
    vs = jax.ShapeDtypeStruct((1, c), F32)
    return pl.pallas_call(body, grid=(t // tr,), in_specs=[row, row, vec, vec], out_specs=[row, vec, vec],
                          out_shape=[jax.ShapeDtypeStruct((t, c), BF16), vs, vs],
                          compiler_params=_cp("arbitrary"), name=name)(gout, y, g, gt)


def _loss_head(y, tgt, *, name):
    t, c = y.shape
    tr = _row_tile(t)

    def body(y_ref, t_ref, l_ref, dy_ref):
        @pl.when(pl.program_id(0) == 0)
        def _():
            l_ref[...] = jnp.zeros_like(l_ref)

        d = y_ref[...] - t_ref[...]
        dy_ref[...] = d * (1.0 / c)
        l_ref[...] += 0.5 * jnp.sum(jnp.mean(d * d, axis=-1, keepdims=True))

    row = pl.BlockSpec((tr, c), lambda i: (i, 0))
    return pl.pallas_call(body, grid=(t // tr,), in_specs=[row, row],
                          out_specs=[pl.BlockSpec((1, LANES), lambda i: (0, 0)), row],
                          out_shape=[jax.ShapeDtypeStruct((1, LANES), F32), jax.ShapeDtypeStruct((t, c), F32)],
                          compiler_params=_cp("arbitrary"), name=name)(y, tgt)


def _adamw(w, g, m, v, *, name):
    shape = w.shape
    c = shape[-1]
    r = math.prod(shape[:-1])
    w2, g2, m2, v2 = (a.reshape(r, c) for a in (w, g, m, v))
    tr = r
    for cand in (1024, 512, 256, 128, 64, 32, 16, 8):
        if r % cand == 0 and cand * c * 4 <= (1 << 20):
            tr = cand
            break
    c1 = 1.0 - ADAM_B1 ** ADAM_STEP
    c2 = 1.0 - ADAM_B2 ** ADAM_STEP

    def body(w_ref, g_ref, m_ref, v_ref, d_ref, nm_ref, nv_ref):
        gv = g_ref[...]
        mn = ADAM_B1 * m_ref[...] + (1.0 - ADAM_B1) * gv
        vn = ADAM_B2 * v_ref[...] + (1.0 - ADAM_B2) * jnp.square(gv)
        m_hat = mn / c1
        v_hat = vn / c2
        d_ref[...] = -ADAM_LR * (m_hat / (jnp.sqrt(v_hat) + ADAM_EPS) + ADAM_WD * w_ref[...])
        nm_ref[...] = mn
        nv_ref[...] = vn

    blk = pl.BlockSpec((tr, c), lambda i: (i, 0))
    s = jax.ShapeDtypeStruct((r, c), F32)
    d, nm, nv = pl.pallas_call(body, grid=(r // tr,), in_specs=[blk] * 4, out_specs=[blk] * 3, out_shape=[s, s, s],
                               compiler_params=_cp("parallel"), name=name)(w2, g2, m2, v2)
    return d.reshape(shape), nm.reshape(shape), nv.reshape(shape)


def _gla_parts(p_ref, wg_ref, bg_ref):
    q = p_ref[:, 0:GLA_KW] * (GLA_DK ** -0.5)
    k = p_ref[:, GLA_KW:2 * GLA_KW]
    glr = p_ref[:, 2 * GLA_KW + 2 * GLA_VW:GLA_IN_PAD]
    gate = _dg(glr, wg_ref[...]) + bg_ref[...]
    log_a = (jnp.minimum(gate, 0.0) - jnp.log(1.0 + jnp.exp(-jnp.abs(gate)))) * (1.0 / 16.0)
    tril = (_iota2((CHUNK, CHUNK), 0) >= _iota2((CHUNK, CHUNK), 1)).astype(F32)
    cum = _hi(tril, log_a)
    c_last = cum[CHUNK - 1:CHUNK, :]
    f = jnp.exp(c_last - cum)
    dec = jnp.exp(c_last)
    return q, k, glr, gate, f, k * f, dec


def _gla_fwd(p, wg, bg, hg, *, name):
    t = p.shape[0]
    nc = t // CHUNK

    def body(p_ref, wg_ref, bg_ref, hg_ref, og_ref, s_ref, st):
        @pl.when(pl.program_id(0) == 0)
        def _():
            st[...] = jnp.zeros_like(st)

        q, _, _, _, _, ke, dec = _gla_parts(p_ref, wg_ref, bg_ref)
        for h in range(GLA_HEADS):
            ks = slice(h * GLA_DK, (h + 1) * GLA_DK)
            vs = slice(2 * GLA_KW + h * GLA_DV, 2 * GLA_KW + (h + 1) * GLA_DV)
            rs = slice(2 * GLA_KW + GLA_VW + h * GLA_DV, 2 * GLA_KW + GLA_VW + (h + 1) * GLA_DV)
            s_new = st[h] * dec[:, ks] + _dg(p_ref[:, vs], ke[:, ks], TN)
            st[h] = s_new
            s_ref[0, h] = s_new
            o = _dg(q[:, ks], s_new, NT)
            rn = lax.rsqrt(jnp.mean(o * o, axis=-1, keepdims=True) + EPS)
            rv = p_ref[:, rs]
            og_ref[:, h * GLA_DV:(h + 1) * GLA_DV] = (((o * rn) * hg_ref[...]) * (rv * _sigmoid(rv))).astype(BF16)

    full = lambda a: pl.BlockSpec(a.shape, lambda n: (0,) * a.ndim)
    return pl.pallas_call(
        body, grid=(nc,),
        in_specs=[pl.BlockSpec((CHUNK, GLA_IN_PAD), lambda n: (n, 0)), full(wg), full(bg), full(hg)],
        out_specs=[pl.BlockSpec((CHUNK, GLA_VW), lambda n: (n, 0)),
                   pl.BlockSpec((1, GLA_HEADS, GLA_DV, GLA_DK), lambda n: (n, 0, 0, 0))],
        out_shape=[jax.ShapeDtypeStruct((t, GLA_VW), BF16), jax.ShapeDtypeStruct((nc, GLA_HEADS, GLA_DV, GLA_DK), F32)],
        scratch_shapes=[pltpu.VMEM((GLA_HEADS, GLA_DV, GLA_DK), F32)],
        compiler_params=_cp("arbitrary"), name=name)(p, wg, bg, hg)


def _gla_bwd(p, dog, sall, wg, bg, hg, *, name):
    t = p.shape[0]
    nc = t // CHUNK

    def body(p_ref, dog_ref, s1_ref, s0_ref, wg_ref, bg_ref, hg_ref, dp_ref, dwg_ref, dbg_ref, dhg_ref, gt):
        i = pl.program_id(0)

        @pl.when(i == 0)
        def _():
            gt[...] = jnp.zeros_like(gt)
            dwg_ref[...] = jnp.zeros_like(dwg_ref)
            dbg_ref[...] = jnp.zeros_like(dbg_ref)
            dhg_ref[...] = jnp.zeros_like(dhg_ref)

        has_prev = (i < nc - 1).astype(F32)
        q, k, glr, gate, f, ke, dec = _gla_parts(p_ref, wg_ref, bg_ref)
        hgv = hg_ref[...]
        dke_parts, ddec_parts = [], []
        dhg = jnp.zeros((1, GLA_DV), F32)
        for h in range(GLA_HEADS):
            ks = slice(h * GLA_DK, (h + 1) * GLA_DK)
            vs = slice(2 * GLA_KW + h * GLA_DV, 2 * GLA_KW + (h + 1) * GLA_DV)
            rs = slice(2 * GLA_KW + GLA_VW + h * GLA_DV, 2 * GLA_KW + GLA_VW + (h + 1) * GLA_DV)
            s1 = s1_ref[0, h]
            s0 = s0_ref[0, h] * has_prev
            vh = p_ref[:, vs]
            rv = p_ref[:, rs]
            o = _dg(q[:, ks], s1, NT)
            rn = lax.rsqrt(jnp.mean(o * o, axis=-1, keepdims=True) + EPS)
            z = o * rn
            sg = _sigmoid(rv)
            sl = rv * sg
            dogh = dog_ref[:, h * GLA_DV:(h + 1) * GLA_DV].astype(F32)
            dhg = dhg + jnp.sum(dogh * z * sl, axis=0, keepdims=True)
            dp_ref[:, rs] = (dogh * (z * hgv) * (sg * (1.0 + rv * (1.0 - sg)))).astype(BF16)
            dz = dogh * sl * hgv
            do = rn * (dz - z * jnp.mean(dz * z, axis=-1, keepdims=True))
            g_tot = gt[h] + _dg(do, q[:, ks], TN)
            dp_ref[:, ks] = (_dg(do, s1, NN) * (GLA_DK ** -0.5)).astype(BF16)
            dke_parts.append(_dg(vh, g_tot, NN))
            dp_ref[:, vs] = _dg(ke[:, ks], g_tot, NT).astype(BF16)
            ddec_parts.append(jnp.sum(g_tot * s0, axis=0, keepdims=True))
            gt[h] = g_tot * dec[:, ks]
        dhg_ref[...] += dhg
        dke = jnp.concatenate(dke_parts, axis=1)
        ddec = jnp.concatenate(ddec_parts, axis=1)
        dp_ref[:, GLA_KW:2 * GLA_KW] = (dke * f).astype(BF16)
        stril = (_iota2((CHUNK, CHUNK), 0) > _iota2((CHUNK, CHUNK), 1)).astype(F32)
        dlog_a = _hi(stril, dke * ke) + ddec * dec
        dgate = dlog_a * (1.0 / 16.0) * _sigmoid(-gate)
        dp_ref[:, 2 * GLA_KW + 2 * GLA_VW:GLA_IN_PAD] = _dg(dgate, wg_ref[...], NT).astype(BF16)
        dwg_ref[...] += _dg(glr, dgate, TN)
        dbg_ref[...] += jnp.sum(dgate, axis=0, keepdims=True)

    full = lambda a: pl.BlockSpec(a.shape, lambda n: (0,) * a.ndim)
    sblk = (1, GLA_HEADS, GLA_DV, GLA_DK)
    return pl.pallas_call(
        body, grid=(nc,),
        in_specs=[pl.BlockSpec((CHUNK, GLA_IN_PAD), lambda n: (nc - 1 - n, 0)),
                  pl.BlockSpec((CHUNK, GLA_VW), lambda n: (nc - 1 - n, 0)),
                  pl.BlockSpec(sblk, lambda n: (nc - 1 - n, 0, 0, 0)),
                  pl.BlockSpec(sblk, lambda n: (jnp.maximum(nc - 2 - n, 0), 0, 0, 0)),
                  full(wg), full(bg), full(hg)],
        out_specs=[pl.BlockSpec((CHUNK, GLA_IN_PAD), lambda n: (nc - 1 - n, 0)), full(wg), full(bg), full(hg)],
        out_shape=[jax.ShapeDtypeStruct((t, GLA_IN_PAD), BF16), jax.ShapeDtypeStruct(wg.shape, F32),
                   jax.ShapeDtypeStruct(bg.shape, F32), jax.ShapeDtypeStruct(hg.shape, F32)],
        scratch_shapes=[pltpu.VMEM((GLA_HEADS, GLA_DV, GLA_DK), F32)],
        compiler_params=_cp("arbitrary"), name=name)(p, dog, sall, sall, wg, bg, hg)


def _gla_layer_fwd(h, w, tag):
    p = _mm(h, w["w_in"], name=tag + "_in")
    og, sall = _gla_fwd(p, w["wg"], w["bg"], w["hg"], name=tag + "_scan")
    y = _mm(og, w["w_out"], name=tag + "_out")
    return y, (p, og, sall)


def _gla_layer_bwd(dy, h, saved, w, tag):
    p, og, sall = saved
    dog = _mm(dy, w["w_out"], tb=True, name=tag + "_dog")
    dw_out = _mm(og, dy, ta=True, name=tag + "_dwout")
    dp, dwg, dbg, dhg = _gla_bwd(p, dog, sall, w["wg"], w["bg"], w["hg"], name=tag + "_scanb")
    dw_in = _mm(h, dp, ta=True, name=tag + "_dwin")
    dh = _mm(dp, w["w_in"], tb=True, name=tag + "_dh")
    grads = dict(w_in=dw_in[:, :GLA_IN], w_gate_up=dwg[:GLA_RANK], b_gate=dbg[0], head_g=dhg[0], w_out=dw_out)
    return dh, grads


def _rope_tables(pos, inv_freq, *, name):
    t = pos.shape[0]
    tr = _row_tile(t)
    half = MLA_ROPE // 2

    def body(p_ref, f_ref, c_ref, s1_ref, s2_ref, s1b_ref, s2b_ref):
        ang = p_ref[...].astype(F32) * f_ref[...]
        lane = _iota2((tr, LANES), 1)
        lo = (lane >= MLA_NOPE) & (lane < MLA_NOPE + half)
        hi = (lane >= MLA_NOPE + half) & (lane < MLA_QK)
        cs, sn = jnp.cos(ang), jnp.sin(ang)
        zero = jnp.zeros_like(cs)
        c_ref[...] = jnp.where(lane < MLA_NOPE, 1.0, jnp.where(lane < MLA_QK, cs, 0.0))
        s1_ref[...] = jnp.where(lo, -sn, zero)
        s2_ref[...] = jnp.where(hi, sn, zero)
        s1b_ref[...] = jnp.where(lo, sn, zero)
        s2b_ref[...] = jnp.where(hi, -sn, zero)

    row = pl.BlockSpec((tr, LANES), lambda i: (i, 0))
    s = jax.ShapeDtypeStruct((t, LANES), F32)
    return pl.pallas_call(body, grid=(t // tr,),
                          in_specs=[pl.BlockSpec((tr, 1), lambda i: (i, 0)), pl.BlockSpec((1, LANES), lambda i: (0, 0))],
                          out_specs=[row] * 5, out_shape=[s] * 5, compiler_params=_cp("parallel"), name=name)(pos, inv_freq)


def _rope(x, c, s1, s2, *, out_dtype, sum_heads=False, name):
    nh, t, _ = x.shape
    tr = _row_tile(t)
    half = MLA_ROPE // 2

    def body(x_ref, c_ref, s1_ref, s2_ref, o_ref):
        xv = x_ref[0].astype(F32)
        y = xv * c_ref[...] + pltpu.roll(xv, LANES - half, 1) * s1_ref[...] + pltpu.roll(xv, half, 1) * s2_ref[...]
        if sum_heads:
            @pl.when(pl.program_id(1) == 0)
            def _():
                o_ref[...] = jnp.zeros_like(o_ref)
            o_ref[...] += y
        else:
            o_ref[0] = y.astype(out_dtype)

    tab = pl.BlockSpec((tr, LANES), lambda i, h: (i, 0))
    xs = pl.BlockSpec((1, tr, LANES), lambda i, h: (h, i, 0))
    if sum_heads:
        return pl.pallas_call(body, grid=(t // tr, nh), in_specs=[xs, tab, tab, tab], out_specs=tab,
                              out_shape=jax.ShapeDtypeStruct((t, LANES), F32),
                              compiler_params=_cp("parallel", "arbitrary"), name=name)(x, c, s1, s2)
    return pl.pallas_call(body, grid=(t // tr, nh), in_specs=[xs, tab, tab, tab], out_specs=xs,
                          out_shape=jax.ShapeDtypeStruct(x.shape, out_dtype),
                          compiler_params=_cp("parallel", "parallel"), name=name)(x, c, s1, s2)


FLASH_BLK = 512


def _diag_mask(blk):
    return (_iota2((blk, blk), 1) // CHUNK) <= (_iota2((blk, blk), 0) // CHUNK)


def _flash_fwd(q, k, v, *, name):
    nh, t, _ = q.shape
    blk = min(FLASH_BLK, t)
    scale = MLA_QK ** -0.5

    def body(q_ref, k_ref, v_ref, o_ref, lse_ref):
        i = pl.program_id(1)
        qv = q_ref[0]

        def step(j, carry, masked):
            m, l, acc = carry
            off = pl.multiple_of(j * blk, blk)
            kb = k_ref[0, pl.ds(off, blk), :]
            vb = v_ref[0, pl.ds(off, blk), :]
            s = _dg(qv, kb, NT) * scale
            if masked:
                s = jnp.where(_diag_mask(blk), s, NEG_INF)
            m_new = jnp.maximum(m, jnp.max(s, axis=-1, keepdims=True))
            p = jnp.exp(s - m_new)
            alpha = jnp.exp(m - m_new)
            return m_new, alpha * l + jnp.sum(p, axis=-1, keepdims=True), alpha * acc + _dg(p, vb, NN)

        init = (jnp.full((blk, 1), NEG_INF, F32), jnp.zeros((blk, 1), F32), jnp.zeros((blk, MLA_V), F32))
        carry = lax.fori_loop(0, i, lambda j, c: step(j, c, False), init)
        m, l, acc = step(i, carry, True)
        o_ref[0] = (acc / l).astype(BF16)
        lse_ref[0] = m + jnp.log(l)

    qs = pl.BlockSpec((1, blk, LANES), lambda h, i: (h, i, 0))
    return pl.pallas_call(
        body, grid=(nh, t // blk),
        in_specs=[qs, pl.BlockSpec((1, t, LANES), lambda h, i: (h, 0, 0)), pl.BlockSpec((1, t, MLA_V), lambda h, i: (h, 0, 0))],
        out_specs=[pl.BlockSpec((1, blk, MLA_V), lambda h, i: (h, i, 0)), pl.BlockSpec((1, blk, 1), lambda h, i: (h, i, 0))],
        out_shape=[jax.ShapeDtypeStruct((nh, t, MLA_V), BF16), jax.ShapeDtypeStruct((nh, t, 1), F32)],
        compiler_params=_cp("parallel", "parallel"), name=name)(q, k, v)


def _flash_dq(q, k, v, do, o, lse, *, name):
    nh, t, _ = q.shape
    blk = min(FLASH_BLK, t)
    scale = MLA_QK ** -0.5

    def body(q_ref, k_ref, v_ref, do_ref, o_ref, lse_ref, dq_ref, dl_ref):
        i = pl.program_id(1)
        qv = q_ref[0]
        dov = do_ref[0]
        lse_v = lse_ref[0]
        delta = jnp.sum(dov.astype(F32) * o_ref[0].astype(F32), axis=-1, keepdims=True)
        dl_ref[0] = delta

        def step(j, dq, masked):
            off = pl.multiple_of(j * blk, blk)
            kb = k_ref[0, pl.ds(off, blk), :]
            vb = v_ref[0, pl.ds(off, blk), :]
            s = _dg(qv, kb, NT) * scale
            if masked:
                s = jnp.where(_diag_mask(blk), s, NEG_INF)
            p = jnp.exp(s - lse_v)
            ds = p * (_dg(dov, vb, NT) - delta) * scale
            return dq + _dg(ds, kb, NN)

        dq = lax.fori_loop(0, i, lambda j, c: step(j, c, False), jnp.zeros((blk, LANES), F32))
        dq_ref[0] = step(i, dq, True)

    qs = pl.BlockSpec((1, blk, LANES), lambda h, i: (h, i, 0))
    vs = pl.BlockSpec((1, blk, MLA_V), lambda h, i: (h, i, 0))
    ls = pl.BlockSpec((1, blk, 1), lambda h, i: (h, i, 0))
    return pl.pallas_call(
        body, grid=(nh, t // blk),
        in_specs=[qs, pl.BlockSpec((1, t, LANES), lambda h, i: (h, 0, 0)), pl.BlockSpec((1, t, MLA_V), lambda h, i: (h, 0, 0)), vs, vs, ls],
        out_specs=[qs, ls],
        out_shape=[jax.ShapeDtypeStruct((nh, t, LANES), F32), jax.ShapeDtypeStruct((nh, t, 1), F32)],
        compiler_params=_cp("parallel", "parallel"), name=name)(q, k, v, do, o, lse)


def _flash_dkv(q, k, v, do, lse, delta, *, name):
    nh, t, _ = q.shape
    blk = min(FLASH_BLK, t)
    nq = t // blk
    scale = MLA_QK ** -0.5

    def body(q_ref, k_ref, v_ref, do_ref, lse_ref, dl_ref, dk_ref, dv_ref):
        j = pl.program_id(1)
        kb = k_ref[0]
        vb = v_ref[0]

        def step(i, carry, masked):
            dk, dv = carry
            off = pl.multiple_of(i * blk, blk)
            qb = q_ref[0, pl.ds(off, blk), :]
            dob = do_ref[0, pl.ds(off, blk), :]
            s = _dg(qb, kb, NT) * scale
            if masked:
                s = jnp.where(_diag_mask(blk), s, NEG_INF)
            p = jnp.exp(s - lse_ref[0, pl.ds(off, blk), :])
            ds = p * (_dg(dob, vb, NT) - dl_ref[0, pl.ds(off, blk), :]) * scale
            return dk + _dg(ds, qb, TN), dv + _dg(p, dob, TN)

        carry = step(j, (jnp.zeros((blk, LANES), F32), jnp.zeros((blk, MLA_V), F32)), True)
        dk, dv = lax.fori_loop(j + 1, nq, lambda i, c: step(i, c, False), carry)
        dk_ref[0] = dk
        dv_ref[0] = dv

    ks = pl.BlockSpec((1, blk, LANES), lambda h, j: (h, j, 0))
    vs = pl.BlockSpec((1, blk, MLA_V), lambda h, j: (h, j, 0))
    fl = lambda w: pl.BlockSpec((1, t, w), lambda h, j: (h, 0, 0))
    return pl.pallas_call(
        body, grid=(nh, nq),
        in_specs=[fl(LANES), ks, vs, fl(MLA_V), fl(1), fl(1)],
        out_specs=[ks, vs],
        out_shape=[jax.ShapeDtypeStruct((nh, t, LANES), F32), jax.ShapeDtypeStruct((nh, t, MLA_V), F32)],
        compiler_params=_cp("parallel", "parallel"), name=name)(q, k, v, do, lse, delta)


def _heads_first(a, width):
    t = a.shape[0]
    return a.reshape(t, MLA_HEADS, width).transpose(1, 0, 2)


def _heads_last(a):
    return a.transpose(1, 0, 2).reshape(a.shape[1], -1)


def _mla_layer_fwd(h, pos, w, tag):
    t = h.shape[0]
    zq = jnp.zeros((1, MLA_Q_RANK), F32)
    zkv = jnp.zeros((1, MLA_KV_RANK), F32)
    p = _mm(h, w["w_in"], name=tag + "_in")
    cq, ckv, krp = p[:, :MLA_Q_RANK], p[:, MLA_Q_RANK:MLA_Q_RANK + MLA_KV_RANK], p[:, MLA_Q_RANK + MLA_KV_RANK:MLA_IN]
    qn = _premod_fwd(cq, w["q_norm_g"], zq, zq, name=tag + "_qnorm")
    kvn = _premod_fwd(ckv, w["kv_norm_g"], zkv, zkv, name=tag + "_kvnorm")
    q = _mm(qn, w["w_uq"], name=tag + "_uq")
    kv = _mm(kvn, w["w_ukv"], name=tag + "_ukv")
    q_pre = jnp.pad(_heads_first(q, MLA_QK), ((0, 0), (0, 0), (0, LANES - MLA_QK)))
    kv3 = _heads_first(kv, MLA_NOPE + MLA_V)
    k_pre = jnp.concatenate([kv3[:, :, :MLA_NOPE], jnp.broadcast_to(krp[None], (MLA_HEADS, t, MLA_ROPE)),
                             jnp.zeros((MLA_HEADS, t, LANES - MLA_QK), F32)], axis=-1)
    vh = kv3[:, :, MLA_NOPE:].astype(BF16)
    half = MLA_ROPE // 2
    freq = ROPE_THETA ** (-jnp.arange(half, dtype=F32) / half)
    inv_freq = jnp.concatenate([jnp.zeros((MLA_NOPE,), F32), freq, freq, jnp.zeros((LANES - MLA_QK,), F32)])[None]
    tabs = _rope_tables(pos.reshape(t, 1), inv_freq, name=tag + "_tables")
    qr = _rope(q_pre, tabs[0], tabs[1], tabs[2], out_dtype=BF16, name=tag + "_ropeq")
    kr = _rope(k_pre, tabs[0], tabs[1], tabs[2], out_dtype=BF16, name=tag + "_ropek")
    o, lse = _flash_fwd(qr, kr, vh, name=tag + "_attn")
    of = _heads_last(o)
    y = _mm(of, w["w_out"], name=tag + "_out")
    return y, (cq, ckv, qn, kvn, qr, kr, vh, o, lse, of, tabs)


def _mla_layer_bwd(dy, h, saved, w, tag):
    cq, ckv, qn, kvn, qr, kr, vh, o, lse, of, tabs = saved
    t = h.shape[0]
    zq = jnp.zeros((1, MLA_Q_RANK), F32)
    zkv = jnp.zeros((1, MLA_KV_RANK), F32)
    dof = _mm(dy, w["w_out"], tb=True, out_dtype=BF16, name=tag + "_dof")
    dw_out = _mm(of, dy, ta=True, name=tag + "_dwout")
    do = _heads_first(dof, MLA_V)
    dqr, delta = _flash_dq(qr, kr, vh, do, o, lse, name=tag + "_attn_dq")
    dkr, dv = _flash_dkv(qr, kr, vh, do, lse, delta, name=tag + "_attn_dkv")
    dq_pre = _rope(dqr, tabs[0], tabs[3], tabs[4], out_dtype=F32, name=tag + "_ropeq_b")
    dk_sum = _rope(dkr, tabs[0], tabs[3], tabs[4], out_dtype=F32, sum_heads=True, name=tag + "_ropek_b")
    dq = _heads_last(dq_pre[:, :, :MLA_QK])
    dkv = _heads_last(jnp.concatenate([dkr[:, :, :MLA_NOPE], dv], axis=-1))
    dw_uq = _mm(qn, dq, ta=True, name=tag + "_dwuq")
    dqn = _mm(dq, w["w_uq"], tb=True, name=tag + "_dqn")
    dw_ukv = _mm(kvn, dkv, ta=True, name=tag + "_dwukv")
    dkvn = _mm(dkv, w["w_ukv"], tb=True, name=tag + "_dkvn")
    dcq, dqg, _, _ = _premod_bwd(dqn, cq, w["q_norm_g"], zq, jnp.zeros_like(cq), name=tag + "_qnorm_b")
    dckv, dkvg, _, _ = _premod_bwd(dkvn, ckv, w["kv_norm_g"], zkv, jnp.zeros_like(ckv), name=tag + "_kvnorm_b")
    dp = jnp.concatenate([dcq, dckv, dk_sum[:, MLA_NOPE:MLA_QK], jnp.zeros((t, MLA_IN_PAD - MLA_IN), F32)], axis=1).astype(BF16)
    dw_in = _mm(h, dp, ta=True, name=tag + "_dwin")
    dh = _mm(dp, w["w_in"], tb=True, name=tag + "_dh")
    grads = dict(w_in=dw_in[:, :MLA_IN], q_norm_g=dqg[0], w_uq=dw_uq, kv_norm_g=dkvg[0], w_ukv=dw_ukv, w_out=dw_out)
    return dh, grads


CONV_HALO = 8


def _conv_tiles(t):
    return min(512, t), 512


def _gdn_conv_fwd(p, w, *, name):
    t = p.shape[0]
    tr, tc = _conv_tiles(t)
    hb = tr // CONV_HALO

    def body(x_ref, halo_ref, w_ref, o_ref, buf):
        i = pl.program_id(0)
        buf[0:CONV_HALO, :] = halo_ref[...] * (i > 0).astype(F32)
        buf[CONV_HALO:CONV_HALO + tr, :] = x_ref[...]
        base = CONV_HALO - (GDN_CONV - 1)
        acc = buf[pl.ds(base, tr), :] * w_ref[0:1, :]
        for j in range(1, GDN_CONV):
            acc = acc + buf[pl.ds(base + j, tr), :] * w_ref[j:j + 1, :]
        o_ref[...] = acc * _sigmoid(acc)

    return pl.pallas_call(
        body, grid=(t // tr, GDN_CONV_W // tc),
        in_specs=[pl.BlockSpec((tr, tc), lambda i, j: (i, j)),
                  pl.BlockSpec((CONV_HALO, tc), lambda i, j: (jnp.maximum(i * hb - 1, 0), j)),
                  pl.BlockSpec((GDN_CONV, tc), lambda i, j: (0, j))],
        out_specs=pl.BlockSpec((tr, tc), lambda i, j: (i, j)),
        out_shape=jax.ShapeDtypeStruct((t, GDN_CONV_W), F32),
        scratch_shapes=[pltpu.VMEM((tr + CONV_HALO, tc), F32)],
        compiler_params=_cp("parallel", "parallel"), name=name)(p, p, w)


def _gdn_conv_bwd(dqkv, p, w, *, name):
    t = p.shape[0]
    tr, tc = _conv_tiles(t)
    hb = tr // CONV_HALO
    nr = t // tr
    ext = tr + CONV_HALO

    def body(x_ref, xp_ref, xn_ref, d_ref, dn_ref, w_ref, dx_ref, dw_ref, bufx, bufd):
        i = pl.program_id(1)

        @pl.when(i == 0)
        def _():
            dw_ref[...] = jnp.zeros_like(dw_ref)

        last = (i < nr - 1).astype(F32)
        bufx[0:CONV_HALO, :] = xp_ref[...] * (i > 0).astype(F32)
        bufx[CONV_HALO:CONV_HALO + tr, :] = x_ref[...]
        bufx[CONV_HALO + tr:, :] = xn_ref[...] * last
        base = CONV_HALO - (GDN_CONV - 1)
        acc = bufx[pl.ds(base, ext), :] * w_ref[0:1, :]
        for j in range(1, GDN_CONV):
            acc = acc + bufx[pl.ds(base + j, ext), :] * w_ref[j:j + 1, :]
        sg = _sigmoid(acc)
        dsilu = sg * (1.0 + acc * (1.0 - sg))
        bufd[0:tr, :] = d_ref[...] * dsilu[0:tr, :]
        bufd[tr:, :] = dn_ref[...] * last * dsilu[tr:, :]
        dx = bufd[pl.ds(GDN_CONV - 1, tr), :] * w_ref[0:1, :]
        for j in range(1, GDN_CONV):
            dx = dx + bufd[pl.ds(GDN_CONV - 1 - j, tr), :] * w_ref[j:j + 1, :]
        dx_ref[...] = dx.astype(BF16)
        dc = bufd[0:tr, :]
        for j in range(GDN_CONV):
            dw_ref[j:j + 1, :] += jnp.sum(dc * bufx[pl.ds(base + j, tr), :], axis=0, keepdims=True)

    main = pl.BlockSpec((tr, tc), lambda j, i: (i, j))
    prev = pl.BlockSpec((CONV_HALO, tc), lambda j, i: (jnp.maximum(i * hb - 1, 0), j))
    nxt = pl.BlockSpec((CONV_HALO, tc), lambda j, i: (jnp.minimum((i + 1) * hb, t // CONV_HALO - 1), j))
    wsp = pl.BlockSpec((GDN_CONV, tc), lambda j, i: (0, j))
    return pl.pallas_call(
        body, grid=(GDN_CONV_W // tc, nr),
        in_specs=[main, prev, nxt, main, nxt, wsp], out_specs=[main, wsp],
        out_shape=[jax.ShapeDtypeStruct((t, GDN_CONV_W), BF16), jax.ShapeDtypeStruct((GDN_CONV, GDN_CONV_W), F32)],
        scratch_shapes=[pltpu.VMEM((tr + 2 * CONV_HALO, tc), F32), pltpu.VMEM((ext, tc), F32)],
        compiler_params=_cp("parallel", "arbitrary"), name=name)(p, p, p, dqkv, dqkv, w)


def _dot_sel(sel, b, dims=NN, sel_first=True):
    s = sel.astype(BF16)
    b1 = b.astype(BF16)
    r1 = b - b1.astype(F32)
    b2 = r1.astype(BF16)
    b3 = (r1 - b2.astype(F32)).astype(BF16)
    if sel_first:
        d = lambda v: lax.dot_general(s, v, (dims, ((), ())), preferred_element_type=F32)
    else:
        d = lambda v: lax.dot_general(v, s, (dims, ((), ())), preferred_element_type=F32)
    return d(b1) + (d(b2) + d(b3))


def _gdn_chunk(qraw, kraw, v, braw, araw, alog, dtb):
    c = {}
    c["v"] = v
    c["rq"] = lax.rsqrt(jnp.sum(qraw * qraw, axis=-1, keepdims=True) + EPS)
    c["rk"] = lax.rsqrt(jnp.sum(kraw * kraw, axis=-1, keepdims=True) + EPS)
    c["qn"] = qraw * c["rq"]
    c["qh"] = c["qn"] * (GDN_DK ** -0.5)
    c["kh"] = kraw * c["rk"]
    c["beta"] = _sigmoid(braw)
    c["ea"] = jnp.exp(alog)
    c["xs"] = araw + dtb
    c["g"] = -c["ea"] * _softplus(c["xs"])
    row = _iota2((CHUNK, CHUNK), 0)
    col = _iota2((CHUNK, CHUNK), 1)
    c["strict"] = row > col
    c["triu"] = (row <= col).astype(F32)
    gb = jnp.broadcast_to(c["g"], (CHUNK, LANES))
    cum = _dot_sel((row >= col).astype(F32), gb)
    cum_j = _dot_sel(jnp.ones((CHUNK, CHUNK), F32), gb[:, :CHUNK] * c["triu"])
    diff = jnp.where(c["strict"], cum[:, :CHUNK] - cum_j, 0.0)
    c["dm"] = jnp.where(c["strict"], jnp.exp(diff), 0.0)
    c["kk"] = _dg(c["kh"], c["kh"], NT)
    c["a"] = (c["beta"] * c["dm"]) * c["kk"]
    c_last = cum[CHUNK - 1:CHUNK, :]
    c["e"] = jnp.exp(cum)
    c["f"] = jnp.exp(c_last - cum)
    c["dec"] = jnp.exp(c_last)
    c["rv"] = c["beta"] * c["v"]
    c["rk_rhs"] = (c["beta"] * c["e"]) * c["kh"]
    c["ke"] = c["kh"] * c["f"]
    return c


def _unit_lower_inverse(a):
    eye = (_iota2((CHUNK, CHUNK), 0) == _iota2((CHUNK, CHUNK), 1)).astype(F32)
    t = eye - a
    pw = a
    for _ in range(5):
        pw = _dot3(pw, pw)
        t = t + _dot3(t, pw)
    return t


GDN_HB = 4


def _gdn_specs(chunk_of):
    hb = GDN_HB
    kw = hb // 2 * GDN_DK
    vw = hb * GDN_DV
    qs = pl.BlockSpec((CHUNK, kw), lambda g, n: (chunk_of(n), g))
    ks = pl.BlockSpec((CHUNK, kw), lambda g, n: (chunk_of(n), GDN_QKW // kw + g))
    vs = pl.BlockSpec((CHUNK, vw), lambda g, n: (chunk_of(n), 2 * GDN_QKW // vw + g))
    zs = pl.BlockSpec((CHUNK, vw), lambda g, n: (chunk_of(n), GDN_CONV_W // vw + g))
    col = pl.BlockSpec((hb, CHUNK, 1), lambda g, n: (g, chunk_of(n), 0))
    one = pl.BlockSpec((hb, 1, 1), lambda g, n: (g, 0, 0))
    ng = pl.BlockSpec((1, GDN_DV), lambda g, n: (0, 0))
    hd = pl.BlockSpec((CHUNK, vw), lambda g, n: (chunk_of(n), g))
    return qs, ks, vs, zs, col, one, ng, hd


def _gdn_fwd(qkv, p, braw, araw, alog, dtb, ng, *, name):
    t = qkv.shape[0]
    nc = t // CHUNK
    nh = GDN_V_HEADS

    def body(q_ref, k_ref, v_ref, z_ref, braw_ref, araw_ref, alog_ref, dtb_ref, ng_ref, og_ref, s_ref, t_ref, st):
        @pl.when(pl.program_id(1) == 0)
        def _():
            st[...] = jnp.zeros_like(st)

        for hh in range(GDN_HB):
            kq = slice((hh // 2) * GDN_DK, (hh // 2 + 1) * GDN_DK)
            vsl = slice(hh * GDN_DV, (hh + 1) * GDN_DV)
            c = _gdn_chunk(q_ref[:, kq], k_ref[:, kq], v_ref[:, vsl], braw_ref[hh], araw_ref[hh], alog_ref[hh], dtb_ref[hh])
            tm = _unit_lower_inverse(c["a"])
            t_ref[hh, 0] = tm
            s0 = st[hh]
            u = _dot3(tm, c["rv"]) - _dg(_dot3(tm, c["rk_rhs"]), s0, NN)
            s1 = c["dec"] * s0 + _dg(c["ke"], u, TN)
            st[hh] = s1
            s_ref[hh, 0] = s1
            o = _dg(c["qh"], s1, NN)
            rn = lax.rsqrt(jnp.mean(o * o, axis=-1, keepdims=True) + EPS)
            zv = z_ref[:, vsl]
            og_ref[:, vsl] = (((o * rn) * ng_ref[...]) * (zv * _sigmoid(zv))).astype(BF16)

    qs, ks, vs, zs, col, one, ngs, hd = _gdn_specs(lambda n: n)
    return pl.pallas_call(
        body, grid=(nh // GDN_HB, nc),
        in_specs=[qs, ks, vs, zs, col, col, one, one, ngs],
        out_specs=[hd,
                   pl.BlockSpec((GDN_HB, 1, GDN_DK, GDN_DV), lambda g, n: (g, n, 0, 0)),
                   pl.BlockSpec((GDN_HB, 1, CHUNK, CHUNK), lambda g, n: (g, n, 0, 0))],
        out_shape=[jax.ShapeDtypeStruct((t, GDN_VW), BF16), jax.ShapeDtypeStruct((nh, nc, GDN_DK, GDN_DV), F32),
                   jax.ShapeDtypeStruct((nh, nc, CHUNK, CHUNK), F32)],
        scratch_shapes=[pltpu.VMEM((GDN_HB, GDN_DK, GDN_DV), F32)],
        compiler_params=_cp("parallel", "arbitrary"), name=name)(qkv, qkv, qkv, p, braw, araw, alog, dtb, ng)


def _gdn_bwd(qkv, p, braw, araw, alog, dtb, ng, dog, sall, tall, *, name):
    t = qkv.shape[0]
    nc = t // CHUNK
    nh = GDN_V_HEADS

    def body(q_ref, k_ref, v_ref, z_ref, braw_ref, araw_ref, alog_ref, dtb_ref, ng_ref, dog_ref, s1_ref, s0_ref, t_ref,
             dq_ref, dk_ref, dv_ref, dz_ref, dbraw_ref, daraw_ref, dalog_ref, ddtb_ref, dng_ref, gc):
        grp = pl.program_id(0)
        i = pl.program_id(1)

        @pl.when(i == 0)
        def _():
            gc[...] = jnp.zeros_like(gc)
            dalog_ref[...] = jnp.zeros_like(dalog_ref)
            ddtb_ref[...] = jnp.zeros_like(ddtb_ref)

        @pl.when((i == 0) & (grp == 0))
        def _():
            dng_ref[...] = jnp.zeros_like(dng_ref)

        has_prev = (i < nc - 1).astype(F32)
        ngv = ng_ref[...]
        ones = jnp.ones((CHUNK, LANES), F32)
        dng = jnp.zeros((1, GDN_DV), F32)
        for hh in range(GDN_HB):
            kq = slice((hh // 2) * GDN_DK, (hh // 2 + 1) * GDN_DK)
            vsl = slice(hh * GDN_DV, (hh + 1) * GDN_DV)
            c = _gdn_chunk(q_ref[:, kq], k_ref[:, kq], v_ref[:, vsl], braw_ref[hh], araw_ref[hh], alog_ref[hh], dtb_ref[hh])
            beta, kh, qh, dm, kk, e, f, dec, ke = c["beta"], c["kh"], c["qh"], c["dm"], c["kk"], c["e"], c["f"], c["dec"], c["ke"]
            tm = t_ref[hh, 0]
            s1 = s1_ref[hh, 0]
            s0 = s0_ref[hh, 0] * has_prev
            wv = _dot3(tm, c["rv"])
            wk = _dot3(tm, c["rk_rhs"])
            u = wv - _dg(wk, s0, NN)
            o = _dg(qh, s1, NN)
            zv = z_ref[:, vsl]
            dogv = dog_ref[:, vsl]
            rn = lax.rsqrt(jnp.mean(o * o, axis=-1, keepdims=True) + EPS)
            zo = o * rn
            sg = _sigmoid(zv)
            sl = zv * sg
            dng = dng + jnp.sum(dogv * zo * sl, axis=0, keepdims=True)
            dz_ref[:, vsl] = dogv * (zo * ngv) * (sg * (1.0 + zv * (1.0 - sg)))
            dzo = dogv * sl * ngv
            do = rn * (dzo - zo * jnp.mean(dzo * zo, axis=-1, keepdims=True))
            g_tot = gc[hh] + _dg(qh, do, TN)
            dqh = _dg(do, s1, NT)
            dke = _dg(u, g_tot, NT)
            du = _dg(ke, g_tot, NN)
            ddec = jnp.sum(jnp.sum(g_tot * s0, axis=1, keepdims=True), axis=0, keepdims=True)
            gc[hh] = dec * g_tot - _dg(wk, du, TN)
            dwk = -_dg(du, s0, NT)
            drv = _dot3(tm, du, TN)
            drk = _dot3(tm, dwk, TN)
            da = jnp.where(c["strict"], -(_dot3(drv, wv, NT) + _dot3(drk, wk, NT)), 0.0)
            mx = da * dm * kk
            dbeta = jnp.sum(mx, axis=1, keepdims=True)
            aa = mx * beta
            dcum = jnp.sum(aa, axis=1, keepdims=True) - _dot_sel(ones, aa, TN, sel_first=False)[:, 0:1]
            bm = (da * beta) * dm
            dkh = _dg(bm, kh, NN) + _dg(bm, kh, TN)
            dv_ref[:, vsl] = beta * drv
            dbeta = dbeta + jnp.sum(drv * c["v"], axis=1, keepdims=True)
            dkh = dkh + (beta * e) * drk
            dbeta = dbeta + jnp.sum(drk * (e * kh), axis=1, keepdims=True)
            dcum = dcum + jnp.sum(drk * c["rk_rhs"], axis=1, keepdims=True)
            dkh = dkh + f * dke
            ef = jnp.sum(dke * ke, axis=1, keepdims=True)
            dcum = dcum - ef
            dcl = jnp.sum(ef, axis=0, keepdims=True) + ddec * dec[:, 0:1]
            dg = _dot_sel(c["triu"], jnp.broadcast_to(dcum, (CHUNK, LANES)))[:, 0:1] + dcl
            daraw = dg * (-c["ea"]) * _sigmoid(c["xs"])
            daraw_ref[hh] = daraw
            dbraw_ref[hh] = dbeta * beta * (1.0 - beta)
            dalog_ref[hh] += jnp.sum(dg * c["g"], axis=0, keepdims=True)
            ddtb_ref[hh] += jnp.sum(daraw, axis=0, keepdims=True)
            dn = dqh * (GDN_DK ** -0.5)
            dq_ref[:, vsl] = c["rq"] * (dn - c["qn"] * jnp.sum(dn * c["qn"], axis=-1, keepdims=True))
            dk_ref[:, vsl] = c["rk"] * (dkh - kh * jnp.sum(dkh * kh, axis=-1, keepdims=True))
        dng_ref[...] += dng

    rev = lambda n: nc - 1 - n
    qs, ks, vs, zs, col, one, ngs, hd = _gdn_specs(rev)
    s1s = pl.BlockSpec((GDN_HB, 1, GDN_DK, GDN_DV), lambda g, n: (g, rev(n), 0, 0))
    s0s = pl.BlockSpec((GDN_HB, 1, GDN_DK, GDN_DV), lambda g, n: (g, jnp.maximum(rev(n) - 1, 0), 0, 0))
    ts = pl.BlockSpec((GDN_HB, 1, CHUNK, CHUNK), lambda g, n: (g, rev(n), 0, 0))
    big = jax.ShapeDtypeStruct((t, GDN_VW), F32)
    cols = jax.ShapeDtypeStruct((nh, t, 1), F32)
    ones_s = jax.ShapeDtypeStruct((nh, 1, 1), F32)
    return pl.pallas_call(
        body, grid=(nh // GDN_HB, nc),
        in_specs=[qs, ks, vs, zs, col, col, one, one, ngs, hd, s1s, s0s, ts],
        out_specs=[hd, hd, hd, hd, col, col, one, one, ngs],
        out_shape=[big, big, big, big, cols, cols, ones_s, ones_s, jax.ShapeDtypeStruct((1, GDN_DV), F32)],
        scratch_shapes=[pltpu.VMEM((GDN_HB, GDN_DK, GDN_DV), F32)],
        compiler_params=_cp("arbitrary", "arbitrary"), name=name)(qkv, qkv, qkv, p, braw, araw, alog, dtb, ng, dog, sall, sall, tall)


def _gdn_layer_fwd(h, w, tag):
    p = _mm(h, w["w_in"], name=tag + "_in")
    qkv = _gdn_conv_fwd(p, w["conv_w"], name=tag + "_conv")
    braw = p[:, GDN_CONV_W + GDN_VW:GDN_CONV_W + GDN_VW + GDN_V_HEADS].T[:, :, None]
    araw = p[:, GDN_CONV_W + GDN_VW + GDN_V_HEADS:GDN_IN].T[:, :, None]
    og, sall, tall = _gdn_fwd(qkv, p, braw, araw, w["a_log"], w["dt_bias"], w["norm_g"], name=tag + "_scan")
    y = _mm(og, w["w_out"], name=tag + "_out")
    return y, (p, qkv, braw, araw, og, sall, tall)


def _gdn_layer_bwd(dy, h, saved, w, tag):
    p, qkv, braw, araw, og, sall, tall = saved
    t = h.shape[0]
    dog = _mm(dy, w["w_out"], tb=True, name=tag + "_dog")
    dw_out = _mm(og, dy, ta=True, name=tag + "_dwout")
    dq16, dk16, dv, dz, dbraw, daraw, dalog, ddtb, dng = _gdn_bwd(
        qkv, p, braw, araw, w["a_log"], w["dt_bias"], w["norm_g"], dog, sall, tall, name=tag + "_scanb")
    pair = lambda a: a.reshape(t, GDN_K_HEADS, 2, GDN_DK).sum(axis=2).reshape(t, GDN_QKW)
    dqkv = jnp.concatenate([pair(dq16), pair(dk16), dv], axis=1)
    dpre, dconv_w = _gdn_conv_bwd(dqkv, p, w["conv_w"], name=tag + "_convb")
    dp = jnp.concatenate([dpre, dz.astype(BF16), dbraw[:, :, 0].T.astype(BF16), daraw[:, :, 0].T.astype(BF16),
                          jnp.zeros((t, GDN_IN_PAD - GDN_IN), BF16)], axis=1)
    dw_in = _mm(h, dp, ta=True, name=tag + "_dwin")
    dh = _mm(dp, w["w_in"], tb=True, name=tag + "_dh")
    grads = dict(w_in=dw_in[:, :GDN_IN], conv_w=dconv_w, a_log=dalog[:, 0, 0], dt_bias=ddtb[:, 0, 0], norm_g=dng[0], w_out=dw_out)
    return dh, grads


MESH_ID = pl.DeviceIdType.MESH
FLAT_W = 1024
FLAT_ROWS = 13056
FLAT_TILE = 384


def _exchange(name, ins, out_shapes, plan, n_remote, n_local):
    def body(*refs):
        in_refs = refs[:len(ins)]
        out_refs = refs[len(ins):len(ins) + len(out_shapes)]
        ssem, rsem, lsem = refs[len(ins) + len(out_shapes):]
        x, y, c = lax.axis_index("x"), lax.axis_index("y"), lax.axis_index("c")
        stages, local_copies = plan(x, y, c, in_refs, out_refs)
        assert sum(len(s) for s in stages) == n_remote and len(local_copies) == n_local
        locs = [pltpu.make_async_copy(s, d, lsem.at[i]) for i, (s, d) in enumerate(local_copies)]
        for cp in locs:
            cp.start()
        sent = []
        k = 0
        for stage in stages:
            arrivals = []
            for src, dst, peer, landing in stage:
                cp = pltpu.make_async_remote_copy(src_ref=src, dst_ref=dst, send_sem=ssem.at[k], recv_sem=rsem.at[k],
                                                  device_id=peer, device_id_type=MESH_ID)
                cp.start()
                sent.append(cp)
                arrivals.append(pltpu.make_async_remote_copy(src_ref=src, dst_ref=landing, send_sem=ssem.at[k],
                                                             recv_sem=rsem.at[k], device_id=peer, device_id_type=MESH_ID))
                k += 1
            for cp in arrivals:
                cp.wait_recv()
        for cp in sent:
            cp.wait_send()
        for cp in locs:
            cp.wait()

    hbm = pl.BlockSpec(memory_space=pl.ANY)
    return pl.pallas_call(
        body, in_specs=[hbm] * len(ins), out_specs=[hbm] * len(out_shapes), out_shape=out_shapes,
        scratch_shapes=[pltpu.SemaphoreType.DMA((n_remote,)), pltpu.SemaphoreType.DMA((n_remote,)),
                        pltpu.SemaphoreType.DMA((max(n_local, 1),))],
        name=name)(*ins)


def _other_chips(x, y):
    return [(1 - x, y), (x, 1 - y), (1 - x, 1 - y)]


def _all8_gather(a, *, name):
    def plan(x, y, c, ins, outs):
        (src,), (dst,) = ins, outs
        me = 4 * x + 2 * y + c
        stage = []
        for fx, fy, fc in [(0, 0, 1), (0, 1, 0), (0, 1, 1), (1, 0, 0), (1, 0, 1), (1, 1, 0), (1, 1, 1)]:
            px, py, pc = (1 - x if fx else x), (1 - y if fy else y), (1 - c if fc else c)
            stage.append((src, dst.at[me], (px, py, pc), dst.at[4 * px + 2 * py + pc]))
        return [stage], [(src, dst.at[me])]

    return _exchange(name, [a], [jax.ShapeDtypeStruct((8,) + a.shape, a.dtype)], plan, 7, 1)[0]


def _chip_gather(flat, *, name):
    rows = flat.shape[0]
    half = rows // 2

    def plan(x, y, c, ins, outs):
        (src,), (dst,) = ins, outs
        me = 2 * x + y
        mine = pl.ds(c * half, half)
        theirs = pl.ds((1 - c) * half, half)
        ici = [(src.at[mine], dst.at[me, mine], (px, py, c), dst.at[2 * px + py, mine]) for px, py in _other_chips(x, y)]
        d2d = [(dst.at[2 * px + py, mine], dst.at[2 * px + py, mine], (x, y, 1 - c), dst.at[2 * px + py, theirs])
               for px, py in _other_chips(x, y)]
        return [ici, d2d], []

    return _exchange(name, [flat], [jax.ShapeDtypeStruct((4,) + flat.shape, flat.dtype)], plan, 6, 0)[0]


def _add_sibling(gf, buf_a, core, *, name):
    _, rows, w = gf.shape
    half = rows // 2
    nb = half // FLAT_TILE

    def body(c_ref, g_ref, a_ref, o_ref):
        o_ref[...] = (g_ref[...] + a_ref[...]).astype(BF16)

    blk = (1, FLAT_TILE, w)
    return pl.pallas_call(
        body,
        grid_spec=pltpu.PrefetchScalarGridSpec(
            num_scalar_prefetch=1, grid=(4, nb),
            in_specs=[pl.BlockSpec(blk, lambda s, i, c_ref: (s, c_ref[0] * nb + i, 0)), pl.BlockSpec(blk, lambda s, i, c_ref: (s, i, 0))],
            out_specs=pl.BlockSpec(blk, lambda s, i, c_ref: (s, i, 0))),
        out_shape=jax.ShapeDtypeStruct((4, half, w), BF16), compiler_params=_cp("parallel", "parallel"), name=name)(core, gf, buf_a)


def _sum_chips(hsum, buf_b, chip, *, name):
    _, half, w = hsum.shape
    nb = half // FLAT_TILE

    def body(c_ref, h_ref, b0_ref, b1_ref, b2_ref, b3_ref, o_ref):
        me = c_ref[0]
        own = h_ref[0].astype(F32)
        acc = None
        for j, b_ref in enumerate((b0_ref, b1_ref, b2_ref, b3_ref)):
            term = jnp.where(me == j, own, b_ref[0].astype(F32))
            acc = term if acc is None else acc + term
        o_ref[...] = acc

    blk = (1, FLAT_TILE, w)

    def other(j):
        return pl.BlockSpec(blk, lambda i, c_ref: (jnp.where(c_ref[0] == j, (j + 1) % 4, j), i, 0))

    return pl.pallas_call(
        body,
        grid_spec=pltpu.PrefetchScalarGridSpec(
            num_scalar_prefetch=1, grid=(nb,),
            in_specs=[pl.BlockSpec(blk, lambda i, c_ref: (c_ref[0], i, 0))] + [other(j) for j in range(4)],
            out_specs=pl.BlockSpec((FLAT_TILE, w), lambda i, c_ref: (i, 0))),
        out_shape=jax.ShapeDtypeStruct((half, w), F32), compiler_params=_cp("parallel"), name=name)(chip, hsum, buf_b, buf_b, buf_b, buf_b)


def _sum_slots(buf, *, name):
    n, rows, w = buf.shape
    tr = _pick(rows, (FLAT_TILE, 8))

    def body(b_ref, o_ref):
        acc = b_ref[0]
        for s in range(1, n):
            acc = acc + b_ref[s]
        o_ref[...] = acc

    return pl.pallas_call(body, grid=(rows // tr,), in_specs=[pl.BlockSpec((n, tr, w), lambda i: (0, i, 0))],
                          out_specs=pl.BlockSpec((tr, w), lambda i: (i, 0)), out_shape=jax.ShapeDtypeStruct((rows, w), F32),
                          compiler_params=_cp("parallel"), name=name)(buf)


def _reduce_scatter(gf, core, chip, *, tag):
    _, rows, w = gf.shape
    half = rows // 2

    def plan_a(x, y, c, ins, outs):
        (src,), (dst,) = ins, outs
        return [[(src.at[:, pl.ds((1 - c) * half, half)], dst, (x, y, 1 - c), dst)]], []

    buf_a = _exchange(tag + "_sibling", [gf], [jax.ShapeDtypeStruct((4, half, w), F32)], plan_a, 1, 0)[0]
    hsum = _add_sibling(gf, buf_a, core, name=tag + "_add_sibling")

    def plan_b(x, y, c, ins, outs):
        (src,), (dst,) = ins, outs
        me = 2 * x + y
        stage = [(src.at[2 * px + py], dst.at[me], (px, py, c), dst.at[2 * px + py]) for px, py in _other_chips(x, y)]
        return [stage], []

    buf_b = _exchange(tag + "_chips", [hsum], [jax.ShapeDtypeStruct((4, half, w), BF16)], plan_b, 3, 0)[0]
    mine = _sum_chips(hsum, buf_b, chip, name=tag + "_sum_chips")

    def plan_c(x, y, c, ins, outs):
        (src,), (dst,) = ins, outs
        return [[(src, dst, (x, y, 1 - c), dst)]], []

    theirs = _exchange(tag + "_halves", [mine], [jax.ShapeDtypeStruct((half, w), F32)], plan_c, 1, 0)[0]
    first = core[0] == 0
    return jnp.concatenate([jnp.where(first, mine, theirs), jnp.where(first, theirs, mine)], axis=0)


WEIGHTS = ["ada_w", "ada_b", "norm_pre_g", "norm_post_g", "gla_w_in", "gla_w_gate_up", "gla_b_gate", "gla_head_g",
           "gla_w_out", "mla_w_in", "mla_q_norm_g", "mla_w_uq", "mla_kv_norm_g", "mla_w_ukv", "mla_w_out", "gdn_w_in",
           "gdn_conv_w", "gdn_a_log", "gdn_dt_bias", "gdn_norm_g", "gdn_w_out", "mlp_w_up", "mlp_w_down"]
PACK_BF16 = [("gla_w_in", 2), ("gla_w_out", 1), ("mla_w_in", 1), ("mla_w_uq", 2), ("mla_w_ukv", 2), ("mla_w_out", 1),
             ("gdn_w_in", 2), ("gdn_w_out", 1), ("mlp_w_up", 2), ("mlp_w_down", 1)]
PACK_F32 = [("norm_pre_g", 2), ("norm_post_g", 2), ("gla_w_gate_up", 2), ("gla_b_gate", 1), ("gla_head_g", 1), ("gdn_conv_w", 2)]
REPLICATED_SMALL = ["mla_q_norm_g", "mla_kv_norm_g", "gdn_a_log", "gdn_dt_bias", "gdn_norm_g"]
MIXERS = ["gla", "mla", "gdn"]


def _silu_rows(a, *, name):
    def body(a_ref, o_ref):
        v = a_ref[...]
        o_ref[...] = v * _sigmoid(v)

    return pl.pallas_call(body, out_shape=jax.ShapeDtypeStruct(a.shape, F32), name=name)(a)


def _pack_weights(shards):
    parts = [shards[n].astype(BF16).reshape(-1) for n, _ in PACK_BF16]
    parts += [lax.bitcast_convert_type(shards[n], BF16).reshape(-1) for n, _ in PACK_F32]
    flat = jnp.concatenate(parts)
    return jnp.pad(flat, (0, FLAT_ROWS * FLAT_W - flat.shape[0])).reshape(FLAT_ROWS, FLAT_W)


def _unpack_weights(gathered, shards, chip):
    flat = gathered.reshape(4, -1)
    full, off = {}, 0
    for n, ax in PACK_BF16:
        size = shards[n].size
        seg = flat[:, off:off + size].reshape((4,) + shards[n].shape)
        own = shards[n].astype(BF16)
        full[n] = jnp.concatenate([jnp.where(chip == j, own, seg[j]) for j in range(4)], axis=ax)
        off += size
    for n, ax in PACK_F32:
        size = 2 * shards[n].size
        seg = lax.bitcast_convert_type(flat[:, off:off + size].reshape((4,) + shards[n].shape + (2,)), F32)
        full[n] = jnp.concatenate([jnp.where(chip == j, shards[n], seg[j]) for j in range(4)], axis=ax)
        off += size
    return full


def _pack_grads(grads):
    parts = []
    for n, ax in PACK_BF16 + PACK_F32:
        parts.append(jnp.stack(jnp.split(grads[n].astype(F32), 4, axis=ax)).reshape(4, -1))
    flat = jnp.concatenate(parts, axis=1)
    return jnp.pad(flat, ((0, 0), (0, FLAT_ROWS * FLAT_W - flat.shape[1]))).reshape(4, FLAT_ROWS, FLAT_W)


def _unpack_grads(reduced, shards):
    flat = reduced.reshape(-1)
    out, off = {}, 0
    for n, _ in PACK_BF16 + PACK_F32:
        size = shards[n].size
        out[n] = flat[off:off + size].reshape(shards[n].shape)
        off += size
    return out


def _mixer_weights(kind, j, full, rep):
    if kind == "gla":
        return dict(w_in=jnp.pad(full["gla_w_in"][j], ((0, 0), (0, GLA_IN_PAD - GLA_IN))),
                    wg=jnp.pad(full["gla_w_gate_up"][j], ((0, LANES - GLA_RANK), (0, 0))),
                    bg=full["gla_b_gate"][j][None], hg=full["gla_head_g"][j][None], w_out=full["gla_w_out"][j])
    if kind == "mla":
        return dict(w_in=jnp.pad(full["mla_w_in"][j], ((0, 0), (0, MLA_IN_PAD - MLA_IN))), q_norm_g=rep["mla_q_norm_g"][j][None],
                    w_uq=full["mla_w_uq"][j], kv_norm_g=rep["mla_kv_norm_g"][j][None], w_ukv=full["mla_w_ukv"][j],
                    w_out=full["mla_w_out"][j])
    return dict(w_in=jnp.pad(full["gdn_w_in"][j], ((0, 0), (0, GDN_IN_PAD - GDN_IN))), conv_w=full["gdn_conv_w"][j],
                a_log=rep["gdn_a_log"][j][:, None, None], dt_bias=rep["gdn_dt_bias"][j][:, None, None],
                norm_g=rep["gdn_norm_g"][j][None], w_out=full["gdn_w_out"][j])


def _layer_fwd(xin, mod, gains, kind, mw, w_up, w_down, pos, tag):
    sh_m, sc_m, gt_m, sh_f, sc_f, gt_f = mod
    pre0, pre1, post0, post1 = gains
    h = _premod_fwd(xin, pre0, sc_m, sh_m, name=tag + "_pre0")
    if kind == "gla":
        y, saved = _gla_layer_fwd(h, mw, tag + "_gla")
    elif kind == "mla":
        y, saved = _mla_layer_fwd(h, pos, mw, tag + "_mla")
    else:
        y, saved = _gdn_layer_fwd(h, mw, tag + "_gdn")
    x1 = _postres_fwd(xin, y, post0, gt_m, name=tag + "_post0")
    h2 = _premod_fwd(x1, pre1, sc_f, sh_f, name=tag + "_pre1")
    act = _mm(h2, w_up, out_dtype=BF16, epi="relu2", name=tag + "_up")
    y2 = _mm(act, w_down, name=tag + "_down")
    x2 = _postres_fwd(x1, y2, post1, gt_f, name=tag + "_post1")
    return x2, (xin, h, y, saved, x1, h2, act, y2)


def _layer_bwd(g2, kept, mod, gains, kind, mw, w_up, w_down, tag):
    xin, h, y, saved, x1, h2, act, y2 = kept
    sh_m, sc_m, gt_m, sh_f, sc_f, gt_f = mod
    pre0, pre1, post0, post1 = gains
    dy2, dpost1, dgt_f = _postres_bwd(g2, y2, post1, gt_f, name=tag + "_post1_b")
    du = _mm(dy2, w_down, tb=True, out_dtype=BF16, epi="dact", aux=act, name=tag + "_du")
    dw_down = _mm(act, dy2, ta=True, name=tag + "_dwdown")
    dw_up = _mm(h2, du, ta=True, name=tag + "_dwup")
    dh2 = _mm(du, w_up, tb=True, name=tag + "_dh2")
    g1, dpre1, dsc_f, dsh_f = _premod_bwd(dh2, x1, pre1, sc_f, g2, name=tag + "_pre1_b")
    dy, dpost0, dgt_m = _postres_bwd(g1, y, post0, gt_m, name=tag + "_post0_b")
    if kind == "gla":
        dh, mg = _gla_layer_bwd(dy, h, saved, mw, tag + "_gla")
    elif kind == "mla":
        dh, mg = _mla_layer_bwd(dy, h, saved, mw, tag + "_mla")
    else:
        dh, mg = _gdn_layer_bwd(dy, h, saved, mw, tag + "_gdn")
    g0, dpre0, dsc_m, dsh_m = _premod_bwd(dh, xin, pre0, sc_m, g1, name=tag + "_pre0_b")
    dmod = jnp.concatenate([dsh_m, dsc_m, dgt_m, dsh_f, dsc_f, dgt_f], axis=1)
    return g0, dmod, jnp.concatenate([dpre0, dpre1], axis=0), jnp.concatenate([dpost0, dpost1], axis=0), mg, dw_up, dw_down


def kernel(x, c, positions, ada_w, ada_b, norm_pre_g, norm_post_g, gla_w_in, gla_w_gate_up, gla_b_gate, gla_head_g, gla_w_out, mla_w_in, mla_q_norm_g, mla_w_uq, mla_kv_norm_g, mla_w_ukv, mla_w_out, gdn_w_in, gdn_conv_w, gdn_a_log, gdn_dt_bias, gdn_norm_g, gdn_w_out, mlp_w_up, mlp_w_down, loss_target, m_ada_w, m_ada_b, m_norm_pre_g, m_norm_post_g, m_gla_w_in, m_gla_w_gate_up, m_gla_b_gate, m_gla_head_g, m_gla_w_out, m_mla_w_in, m_mla_q_norm_g, m_mla_w_uq, m_mla_kv_norm_g, m_mla_w_ukv, m_mla_w_out, m_gdn_w_in, m_gdn_conv_w, m_gdn_a_log, m_gdn_dt_bias, m_gdn_norm_g, m_gdn_w_out, m_mlp_w_up, m_mlp_w_down, v_ada_w, v_ada_b, v_norm_pre_g, v_norm_post_g, v_gla_w_in, v_gla_w_gate_up, v_gla_b_gate, v_gla_head_g, v_gla_w_out, v_mla_w_in, v_mla_q_norm_g, v_mla_w_uq, v_mla_kv_norm_g, v_mla_w_ukv, v_mla_w_out, v_gdn_w_in, v_gdn_conv_w, v_gdn_a_log, v_gdn_dt_bias, v_gdn_norm_g, v_gdn_w_out, v_mlp_w_up, v_mlp_w_down):
    w = dict(ada_w=ada_w, ada_b=ada_b, norm_pre_g=norm_pre_g, norm_post_g=norm_post_g, gla_w_in=gla_w_in,
             gla_w_gate_up=gla_w_gate_up, gla_b_gate=gla_b_gate, gla_head_g=gla_head_g, gla_w_out=gla_w_out, mla_w_in=mla_w_in,
             mla_q_norm_g=mla_q_norm_g, mla_w_uq=mla_w_uq, mla_kv_norm_g=mla_kv_norm_g, mla_w_ukv=mla_w_ukv, mla_w_out=mla_w_out,
             gdn_w_in=gdn_w_in, gdn_conv_w=gdn_conv_w, gdn_a_log=gdn_a_log, gdn_dt_bias=gdn_dt_bias, gdn_norm_g=gdn_norm_g,
             gdn_w_out=gdn_w_out, mlp_w_up=mlp_w_up, mlp_w_down=mlp_w_down)
    m = dict(zip(WEIGHTS, [m_ada_w, m_ada_b, m_norm_pre_g, m_norm_post_g, m_gla_w_in, m_gla_w_gate_up, m_gla_b_gate, m_gla_head_g,
                           m_gla_w_out, m_mla_w_in, m_mla_q_norm_g, m_mla_w_uq, m_mla_kv_norm_g, m_mla_w_ukv, m_mla_w_out,
                           m_gdn_w_in, m_gdn_conv_w, m_gdn_a_log, m_gdn_dt_bias, m_gdn_norm_g, m_gdn_w_out, m_mlp_w_up, m_mlp_w_down]))
    v = dict(zip(WEIGHTS, [v_ada_w, v_ada_b, v_norm_pre_g, v_norm_post_g, v_gla_w_in, v_gla_w_gate_up, v_gla_b_gate, v_gla_head_g,
                           v_gla_w_out, v_mla_w_in, v_mla_q_norm_g, v_mla_w_uq, v_mla_kv_norm_g, v_mla_w_ukv, v_mla_w_out,
                           v_gdn_w_in, v_gdn_conv_w, v_gdn_a_log, v_gdn_dt_bias, v_gdn_norm_g, v_gdn_w_out, v_mlp_w_up, v_mlp_w_down]))
    t = x.shape[1]
    ix, iy, ic = lax.axis_index("x"), lax.axis_index("y"), lax.axis_index("c")
    me = 4 * ix + 2 * iy + ic
    chip = 2 * ix + iy
    ada_cols = ada_w.shape[2]

    full = _unpack_weights(_chip_gather(_pack_weights(w), name="gather_weights"), w, chip)

    cond8 = _silu_rows(jnp.pad(c, ((0, 7), (0, 0))), name="cond_silu")
    cond16 = jnp.pad(_all8_gather(cond8, name="gather_cond")[:, 0, :], ((0, 8), (0, 0)))
    mod_cols = []
    for layer in range(DEPTH):
        bias = jnp.broadcast_to(lax.dynamic_slice_in_dim(ada_b[layer], chip * ada_cols, ada_cols)[None], (16, ada_cols))
        mod_cols.append(_mm(cond16, ada_w[layer], epi="add", aux=bias, name=f"ada{layer}")[:8])
    mod_all = _all8_gather(jnp.stack(mod_cols).reshape(DEPTH * 8, ada_cols), name="gather_mod")
    mod = jnp.concatenate([lax.dynamic_slice_in_dim(mod_all[2 * j].reshape(DEPTH, 8, ada_cols), me, 1, axis=1)[:, 0]
                           for j in range(4)], axis=1)

    def layer_args(layer):
        kind, j = MIXERS[layer % 3], layer // 3
        mods = [mod[layer, i * D_MODEL:(i + 1) * D_MODEL][None] for i in range(N_MOD)]
        gains = (full["norm_pre_g"][layer, 0:1], full["norm_pre_g"][layer, 1:2], full["norm_post_g"][layer, 0:1],
                 full["norm_post_g"][layer, 1:2])
        return kind, j, mods, gains, _mixer_weights(kind, j, full, w)

    xs = x[0]
    kept = []
    for layer in range(DEPTH):
        kind, j, mods, gains, mw = layer_args(layer)
        xs, keep = _layer_fwd(xs, mods, gains, kind, mw, full["mlp_w_up"][layer], full["mlp_w_down"][layer], positions[0], f"l{layer}")
        kept.append(keep)
    loss_row, g = _loss_head(xs, loss_target[0], name="loss_head")
    loss = lax.psum(loss_row[0, 0], ("x", "y", "c"))

    grads = {n: [None] * w[n].shape[0] for n, _ in PACK_BF16 + PACK_F32}
    rep_grads = {}
    dmods = [None] * DEPTH
    for layer in reversed(range(DEPTH)):
        kind, j, mods, gains, mw = layer_args(layer)
        g, dmods[layer], dpre, dpost, mg, dw_up, dw_down = _layer_bwd(
            g, kept[layer], mods, gains, kind, mw, full["mlp_w_up"][layer], full["mlp_w_down"][layer], f"l{layer}")
        grads["norm_pre_g"][layer], grads["norm_post_g"][layer] = dpre, dpost
        grads["mlp_w_up"][layer], grads["mlp_w_down"][layer] = dw_up, dw_down
        for key, val in mg.items():
            name = kind + "_" + key
            if name in grads:
                grads[name][j] = val
            else:
                rep_grads[name] = val[None]
    grads = {n: jnp.stack(parts) for n, parts in grads.items()}

    rep_flat = jnp.concatenate([rep_grads[n].reshape(-1) for n in REPLICATED_SMALL])
    dbuf = jnp.concatenate([jnp.concatenate(dmods, axis=0), jnp.pad(rep_flat, (0, N_MOD * D_MODEL - rep_flat.shape[0]))[None],
                            jnp.zeros((3, N_MOD * D_MODEL), F32)], axis=0)
    dall = _all8_gather(dbuf, name="gather_dmod")
    dsum = _sum_slots(dall, name="sum_dmod")
    out_grads = {"ada_b": dsum[:DEPTH]}
    off = 0
    for n in REPLICATED_SMALL:
        out_grads[n] = dsum[DEPTH, off:off + w[n].size].reshape(w[n].shape)
        off += w[n].size
    dada = []
    for layer in range(DEPTH):
        dm16 = jnp.pad(lax.dynamic_slice_in_dim(dall[:, layer, :], chip * ada_cols, ada_cols, axis=1), ((0, 8), (0, 0)))
        dada.append(_mm(cond16, dm16, ta=True, name=f"dada{layer}"))
    out_grads["ada_w"] = jnp.stack(dada)

    reduced = _reduce_scatter(_pack_grads(grads), ic.reshape(1).astype(jnp.int32), chip.reshape(1).astype(jnp.int32),
                              tag="reduce_grads")
    out_grads.update(_unpack_grads(reduced, w))

    deltas, new_m, new_v = {}, {}, {}
    for n in WEIGHTS:
        deltas[n], new_m[n], new_v[n] = _adamw(w[n], out_grads[n], m[n], v[n], name="adamw_" + n)
    return (loss, g[None], *[out_grads[n] for n in WEIGHTS], *[deltas[n] for n in WEIGHTS],
            *[new_m[n] for n in WEIGHTS], *[new_v[n] for n in WEIGHTS])
```

```python
import functools
import math

import jax
import jax.numpy as jnp
from jax import lax
from jax.experimental import pallas as pl
from jax.experimental.pallas import tpu as pltpu

F32 = jnp.float32
BF16 = jnp.bfloat16

D_MODEL = 1024
DEPTH = 4
CHUNK = 64
EPS = 1e-6
NEG_INF = -1e30
N_MOD = 6

GLA_HEADS, GLA_DK, GLA_DV, GLA_RANK = 4, 128, 256, 16
GLA_KW, GLA_VW = GLA_HEADS * GLA_DK, GLA_HEADS * GLA_DV
GLA_IN = 2 * GLA_KW + 2 * GLA_VW + GLA_RANK
GLA_IN_PAD = 3200

MLA_HEADS, MLA_NOPE, MLA_ROPE, MLA_V = 16, 64, 32, 64
MLA_Q_RANK, MLA_KV_RANK = 384, 256
MLA_IN = MLA_Q_RANK + MLA_KV_RANK + MLA_ROPE
MLA_IN_PAD = 768
ROPE_THETA = 10000.0
MLA_QK = MLA_NOPE + MLA_ROPE
LANES = 128

GDN_K_HEADS, GDN_V_HEADS, GDN_DK, GDN_DV, GDN_CONV = 8, 16, 128, 128, 4
GDN_QKW, GDN_VW = GDN_K_HEADS * GDN_DK, GDN_V_HEADS * GDN_DV
GDN_CONV_W = 2 * GDN_QKW + GDN_VW
GDN_IN = GDN_CONV_W + GDN_VW + 2 * GDN_V_HEADS
GDN_IN_PAD = 6400

ADAM_LR, ADAM_B1, ADAM_B2, ADAM_EPS, ADAM_WD, ADAM_STEP = 0.001, 0.9, 0.999, 1e-08, 0.01, 10

VMEM_LIMIT = 56 * 1024 * 1024

NN = ((1,), (0,))
NT = ((1,), (1,))
TN = ((0,), (0,))


def _cp(*sem):
    return pltpu.CompilerParams(dimension_semantics=sem, vmem_limit_bytes=VMEM_LIMIT)


def _pick(n, cands):
    for c in cands:
        if n % c == 0:
            return c
    return n


def _dg(a, b, dims=NN):
    return lax.dot_general(a.astype(BF16), b.astype(BF16), (dims, ((), ())), preferred_element_type=F32)


def _hi(a, b, dims=NN):
    return lax.dot_general(a, b, (dims, ((), ())), precision=lax.Precision.HIGHEST, preferred_element_type=F32)


def _dot3(a, b, dims=NN):
    ah = a.astype(BF16)
    al = (a - ah.astype(F32)).astype(BF16)
    bh = b.astype(BF16)
    bl = (b - bh.astype(F32)).astype(BF16)
    d = lambda u, v: lax.dot_general(u, v, (dims, ((), ())), preferred_element_type=F32)
    return d(ah, bh) + (d(ah, bl) + d(al, bh))


def _sigmoid(x):
    return 1.0 / (1.0 + jnp.exp(-x))


def _softplus(x):
    return jnp.maximum(x, 0.0) + jnp.log(1.0 + jnp.exp(-jnp.abs(x)))


def _iota2(shape, dim):
    return lax.broadcasted_iota(jnp.int32, shape, dim)


def _mm(a, b, *, ta=False, tb=False, out_dtype=F32, epi=None, aux=None, name):
    m = a.shape[1] if ta else a.shape[0]
    k = a.shape[0] if ta else a.shape[1]
    n = b.shape[0] if tb else b.shape[1]
    assert k == (b.shape[1] if tb else b.shape[0]), (a.shape, b.shape, ta, tb)
    tm = _pick(m, (512, 384, 256, 128))
    tn = _pick(n, (512, 640, 384, 256, 128))
    tk = _pick(k, (1024, 640, 512, 768, 384, 256, 128))
    nk = k // tk
    dims = ((0 if ta else 1,), (1 if tb else 0,))

    def body(*refs):
        if aux is not None:
            a_ref, b_ref, x_ref, o_ref, acc = refs
        else:
            a_ref, b_ref, o_ref, acc = refs
        kk = pl.program_id(2)

        @pl.when(kk == 0)
        def _():
            acc[...] = jnp.zeros_like(acc)

        acc[...] += _dg(a_ref[...], b_ref[...], dims)

        @pl.when(kk == nk - 1)
        def _():
            r = acc[...]
            if epi == "relu2":
                r = jnp.square(jnp.maximum(r, 0.0))
            elif epi == "dact":
                r = r * (2.0 * jnp.sqrt(x_ref[...].astype(F32)))
            elif epi == "add":
                r = r + x_ref[...]
            o_ref[...] = r.astype(out_dtype)

    a_spec = pl.BlockSpec((tk, tm), lambda i, j, q: (q, i)) if ta else pl.BlockSpec((tm, tk), lambda i, j, q: (i, q))
    b_spec = pl.BlockSpec((tn, tk), lambda i, j, q: (j, q)) if tb else pl.BlockSpec((tk, tn), lambda i, j, q: (q, j))
    o_spec = pl.BlockSpec((tm, tn), lambda i, j, q: (i, j))
    in_specs = [a_spec, b_spec] + ([o_spec] if aux is not None else [])
    args = (a, b) + ((aux,) if aux is not None else ())
    return pl.pallas_call(
        body, grid=(m // tm, n // tn, nk), in_specs=in_specs, out_specs=o_spec,
        out_shape=jax.ShapeDtypeStruct((m, n), out_dtype), scratch_shapes=[pltpu.VMEM((tm, tn), F32)],
        compiler_params=_cp("parallel", "parallel", "arbitrary"), name=name)(*args)


def _row_tile(t):
    return _pick(t, (512, 256, 128, 64, 8))


def _premod_fwd(x, g, sc, sh, *, name):
    t, c = x.shape
    tr = _row_tile(t)

    def body(x_ref, g_ref, sc_ref, sh_ref, h_ref):
        xv = x_ref[...]
        r = lax.rsqrt(jnp.mean(xv * xv, axis=-1, keepdims=True) + EPS)
        h_ref[...] = (((xv * r) * g_ref[...]) * (1.0 + sc_ref[...]) + sh_ref[...]).astype(BF16)

    row = pl.BlockSpec((tr, c), lambda i: (i, 0))
    vec = pl.BlockSpec((1, c), lambda i: (0, 0))
    return pl.pallas_call(body, grid=(t // tr,), in_specs=[row, vec, vec, vec], out_specs=row,
                          out_shape=jax.ShapeDtypeStruct((t, c), BF16), compiler_params=_cp("parallel"), name=name)(x, g, sc, sh)


def _premod_bwd(dh, x, g, sc, gin, *, name):
    t, c = x.shape
    tr = _row_tile(t)

    def body(dh_ref, x_ref, g_ref, sc_ref, gin_ref, gout_ref, dg_ref, dsc_ref, dsh_ref):
        @pl.when(pl.program_id(0) == 0)
        def _():
            dg_ref[...] = jnp.zeros_like(dg_ref)
            dsc_ref[...] = jnp.zeros_like(dsc_ref)
            dsh_ref[...] = jnp.zeros_like(dsh_ref)

        xv = x_ref[...]
        dhv = dh_ref[...].astype(F32)
        gv = g_ref[...]
        one_sc = 1.0 + sc_ref[...]
        r = lax.rsqrt(jnp.mean(xv * xv, axis=-1, keepdims=True) + EPS)
        nv = xv * r
        dsh_ref[...] += jnp.sum(dhv, axis=0, keepdims=True)
        dsc_ref[...] += jnp.sum(dhv * (nv * gv), axis=0, keepdims=True)
        dg_ref[...] += jnp.sum(dhv * nv * one_sc, axis=0, keepdims=True)
        dn = dhv * gv * one_sc
        dx = r * (dn - nv * jnp.mean(dn * nv, axis=-1, keepdims=True))
        gout_ref[...] = gin_ref[...] + dx

    row = pl.BlockSpec((tr, c), lambda i: (i, 0))
    vec = pl.BlockSpec((1, c), lambda i: (0, 0))
    vs = jax.ShapeDtypeStruct((1, c), F32)
    return pl.pallas_call(body, grid=(t // tr,), in_specs=[row, row, vec, vec, row], out_specs=[row, vec, vec, vec],
                          out_shape=[jax.ShapeDtypeStruct((t, c), F32), vs, vs, vs],
                          compiler_params=_cp("arbitrary"), name=name)(dh, x, g, sc, gin)


def _postres_fwd(x, y, g, gt, *, name):
    t, c = x.shape
    tr = _row_tile(t)

    def body(x_ref, y_ref, g_ref, gt_ref, o_ref):
        yv = y_ref[...]
        r = lax.rsqrt(jnp.mean(yv * yv, axis=-1, keepdims=True) + EPS)
        o_ref[...] = x_ref[...] + gt_ref[...] * ((yv * r) * g_ref[...])

    row = pl.BlockSpec((tr, c), lambda i: (i, 0))
    vec = pl.BlockSpec((1, c), lambda i: (0, 0))
    return pl.pallas_call(body, grid=(t // tr,), in_specs=[row, row, vec, vec], out_specs=row,
                          out_shape=jax.ShapeDtypeStruct((t, c), F32), compiler_params=_cp("parallel"), name=name)(x, y, g, gt)


def _postres_bwd(gout, y, g, gt, *, name):
    t, c = y.shape
    tr = _row_tile(t)

    def body(go_ref, y_ref, g_ref, gt_ref, dy_ref, dg_ref, dgt_ref):
        @pl.when(pl.program_id(0) == 0)
        def _():
            dg_ref[...] = jnp.zeros_like(dg_ref)
            dgt_ref[...] = jnp.zeros_like(dgt_ref)

        yv = y_ref[...]
        gov = go_ref[...]
        gv = g_ref[...]
        gtv = gt_ref[...]
        r = lax.rsqrt(jnp.mean(yv * yv, axis=-1, keepdims=True) + EPS)
        z = yv * r
        dgt_ref[...] += jnp.sum(gov * (z * gv), axis=0, keepdims=True)
        dg_ref[...] += jnp.sum(gov * gtv * z, axis=0, keepdims=True)
        dz = gov * gtv * gv
        dy_ref[...] = (r * (dz - z * jnp.mean(dz * z, axis=-1, keepdims=True))).astype(BF16)

    row = pl.BlockSpec((tr, c), lambda i: (i, 0))
    vec = pl.BlockSpec((1, c), lambda i: (0, 0))
    vs = jax.ShapeDtypeStruct((1, c), F32)
    return pl.pallas_call(body, grid=(t // tr,), in_specs=[row, row, vec, vec], out_specs=[row, vec, vec],
                          out_shape=[jax.ShapeDtypeStruct((t, c), BF16), vs, vs],
                          compiler_params=_cp("arbitrary"), name=name)(gout, y, g, gt)


def _loss_head(y, tgt, *, name):
    t, c = y.shape
    tr = _row_tile(t)

    def body(y_ref, t_ref, l_ref, dy_ref):
        @pl.when(pl.program_id(0) == 0)
        def _():
            l_ref[...] = jnp.zeros_like(l_ref)

        d = y_ref[...] - t_ref[...]
        dy_ref[...] = d * (1.0 / c)
        l_ref[...] += 0.5 * jnp.sum(jnp.mean(d * d, axis=-1, keepdims=True))

    row = pl.BlockSpec((tr, c), lambda i: (i, 0))
    return pl.pallas_call(body, grid=(t // tr,), in_specs=[row, row],
                          out_specs=[pl.BlockSpec((1, LANES), lambda i: (0, 0)), row],
                          out_shape=[jax.ShapeDtypeStruct((1, LANES), F32), jax.ShapeDtypeStruct((t, c), F32)],
                          compiler_params=_cp("arbitrary"), name=name)(y, tgt)


def _adamw(w, g, m, v, *, name):
    shape = w.shape
    c = shape[-1]
    r = math.prod(shape[:-1])
    w2, g2, m2, v2 = (a.reshape(r, c) for a in (w, g, m, v))
    tr = r
    for cand in (1024, 512, 256, 128, 64, 32, 16, 8):
        if r % cand == 0 and cand * c * 4 <= (1 << 20):
            tr = cand
            break
    c1 = 1.0 - ADAM_B1 ** ADAM_STEP
    c2 = 1.0 - ADAM_B2 ** ADAM_STEP

    def body(w_ref, g_ref, m_ref, v_ref, d_ref, nm_ref, nv_ref):
        gv = g_ref[...]
        mn = ADAM_B1 * m_ref[...] + (1.0 - ADAM_B1) * gv
        vn = ADAM_B2 * v_ref[...] + (1.0 - ADAM_B2) * jnp.square(gv)
        m_hat = mn / c1
        v_hat = vn / c2
        d_ref[...] = -ADAM_LR * (m_hat / (jnp.sqrt(v_hat) + ADAM_EPS) + ADAM_WD * w_ref[...])
        nm_ref[...] = mn
        nv_ref[...] = vn

    blk = pl.BlockSpec((tr, c), lambda i: (i, 0))
    s = jax.ShapeDtypeStruct((r, c), F32)
    d, nm, nv = pl.pallas_call(body, grid=(r // tr,), in_specs=[blk] * 4, out_specs=[blk] * 3, out_shape=[s, s, s],
                               compiler_params=_cp("parallel"), name=name)(w2, g2, m2, v2)
    return d.reshape(shape), nm.reshape(shape), nv.reshape(shape)


def _gla_parts(p_ref, wg_ref, bg_ref):
    q = p_ref[:, 0:GLA_KW] * (GLA_DK ** -0.5)
    k = p_ref[:, GLA_KW:2 * GLA_KW]
    glr = p_ref[:, 2 * GLA_KW + 2 * GLA_VW:GLA_IN_PAD]
    gate = _dg(glr, wg_ref[...]) + bg_ref[...]
    log_a = (jnp.minimum(gate, 0.0) - jnp.log(1.0 + jnp.exp(-jnp.abs(gate)))) * (1.0 / 16.0)
    tril = (_iota2((CHUNK, CHUNK), 0) >= _iota2((CHUNK, CHUNK), 1)).astype(F32)
    cum = _hi(tril, log_a)
    c_last = cum[CHUNK - 1:CHUNK, :]
    f = jnp.exp(c_last - cum)
    dec = jnp.exp(c_last)
    return q, k, glr, gate, f, k * f, dec


def _gla_fwd(p, wg, bg, hg, *, name):
    t = p.shape[0]
    nc = t // CHUNK

    def body(p_ref, wg_ref, bg_ref, hg_ref, og_ref, s_ref, st):
        @pl.when(pl.program_id(0) == 0)
        def _():
            st[...] = jnp.zeros_like(st)

        q, _, _, _, _, ke, dec = _gla_parts(p_ref, wg_ref, bg_ref)
        hs = range(GLA_HEADS)
        ks = [slice(h * GLA_DK, (h + 1) * GLA_DK) for h in hs]
        vs = [slice(2 * GLA_KW + h * GLA_DV, 2 * GLA_KW + (h + 1) * GLA_DV) for h in hs]
        rs = [slice(2 * GLA_KW + GLA_VW + h * GLA_DV, 2 * GLA_KW + GLA_VW + (h + 1) * GLA_DV) for h in hs]
        s_new = [st[h] * dec[:, ks[h]] + _dg(p_ref[:, vs[h]], ke[:, ks[h]], TN) for h in hs]
        o = [_dg(q[:, ks[h]], s_new[h], NT) for h in hs]
        for h in hs:
            st[h] = s_new[h]
            s_ref[0, h] = s_new[h]
            rn = lax.rsqrt(jnp.mean(o[h] * o[h], axis=-1, keepdims=True) + EPS)
            rv = p_ref[:, rs[h]]
            og_ref[:, h * GLA_DV:(h + 1) * GLA_DV] = (((o[h] * rn) * hg_ref[...]) * (rv * _sigmoid(rv))).astype(BF16)

    full = lambda a: pl.BlockSpec(a.shape, lambda n: (0,) * a.ndim)
    return pl.pallas_call(
        body, grid=(nc,),
        in_specs=[pl.BlockSpec((CHUNK, GLA_IN_PAD), lambda n: (n, 0)), full(wg), full(bg), full(hg)],
        out_specs=[pl.BlockSpec((CHUNK, GLA_VW), lambda n: (n, 0)),
                   pl.BlockSpec((1, GLA_HEADS, GLA_DV, GLA_DK), lambda n: (n, 0, 0, 0))],
        out_shape=[jax.ShapeDtypeStruct((t, GLA_VW), BF16), jax.ShapeDtypeStruct((nc, GLA_HEADS, GLA_DV, GLA_DK), F32)],
        scratch_shapes=[pltpu.VMEM((GLA_HEADS, GLA_DV, GLA_DK), F32)],
        compiler_params=_cp("arbitrary"), name=name)(p, wg, bg, hg)


def _gla_bwd(p, dog, sall, wg, bg, hg, *, name):
    t = p.shape[0]
    nc = t // CHUNK

    def body(p_ref, dog_ref, s1_ref, s0_ref, wg_ref, bg_ref, hg_ref, dp_ref, dwg_ref, dbg_ref, dhg_ref, gt):
        i = pl.program_id(0)

        @pl.when(i == 0)
        def _():
            gt[...] = jnp.zeros_like(gt)
            dwg_ref[...] = jnp.zeros_like(dwg_ref)
            dbg_ref[...] = jnp.zeros_like(dbg_ref)
            dhg_ref[...] = jnp.zeros_like(dhg_ref)

        has_prev = (i < nc - 1).astype(F32)
        q, k, glr, gate, f, ke, dec = _gla_parts(p_ref, wg_ref, bg_ref)
        hgv = hg_ref[...]
        hs = range(GLA_HEADS)
        ks = [slice(h * GLA_DK, (h + 1) * GLA_DK) for h in hs]
        vs = [slice(2 * GLA_KW + h * GLA_DV, 2 * GLA_KW + (h + 1) * GLA_DV) for h in hs]
        rs = [slice(2 * GLA_KW + GLA_VW + h * GLA_DV, 2 * GLA_KW + GLA_VW + (h + 1) * GLA_DV) for h in hs]
        s1 = [s1_ref[0, h] for h in hs]
        o = [_dg(q[:, ks[h]], s1[h], NT) for h in hs]
        dhg = jnp.zeros((1, GLA_DV), F32)
        do = []
        for h in hs:
            rv = p_ref[:, rs[h]]
            rn = lax.rsqrt(jnp.mean(o[h] * o[h], axis=-1, keepdims=True) + EPS)
            z = o[h] * rn
            sg = _sigmoid(rv)
            sl = rv * sg
            dogh = dog_ref[:, h * GLA_DV:(h + 1) * GLA_DV].astype(F32)
            dhg = dhg + jnp.sum(dogh * z * sl, axis=0, keepdims=True)
            dp_ref[:, rs[h]] = (dogh * (z * hgv) * (sg * (1.0 + rv * (1.0 - sg)))).astype(BF16)
            dz = dogh * sl * hgv
            do.append(rn * (dz - z * jnp.mean(dz * z, axis=-1, keepdims=True)))
        dhg_ref[...] += dhg
        g_tot = [gt[h] + _dg(do[h], q[:, ks[h]], TN) for h in hs]
        dq = [_dg(do[h], s1[h], NN) for h in hs]
        dke_parts = [_dg(p_ref[:, vs[h]], g_tot[h], NN) for h in hs]
        dv = [_dg(ke[:, ks[h]], g_tot[h], NT) for h in hs]
        ddec_parts = []
        for h in hs:
            dp_ref[:, ks[h]] = (dq[h] * (GLA_DK ** -0.5)).astype(BF16)
            dp_ref[:, vs[h]] = dv[h].astype(BF16)
            ddec_parts.append(jnp.sum(g_tot[h] * (s0_ref[0, h] * has_prev), axis=0, keepdims=True))
            gt[h] = g_tot[h] * dec[:, ks[h]]
        dke = jnp.concatenate(dke_parts, axis=1)
        ddec = jnp.concatenate(ddec_parts, axis=1)
        dp_ref[:, GLA_KW:2 * GLA_KW] = (dke * f).astype(BF16)
        stril = (_iota2((CHUNK, CHUNK), 0) > _iota2((CHUNK, CHUNK), 1)).astype(F32)
        dlog_a = _hi(stril, dke * ke) + ddec * dec
        dgate = dlog_a * (1.0 / 16.0) * _sigmoid(-gate)
        dp_ref[:, 2 * GLA_KW + 2 * GLA_VW:GLA_IN_PAD] = _dg(dgate, wg_ref[...], NT).astype(BF16)
        dwg_ref[...] += _dg(glr, dgate, TN)
        dbg_ref[...] += jnp.sum(dgate, axis=0, keepdims=True)

    full = lambda a: pl.BlockSpec(a.shape, lambda n: (0,) * a.ndim)
    sblk = (1, GLA_HEADS, GLA_DV, GLA_DK)
    return pl.pallas_call(
        body, grid=(nc,),
        in_specs=[pl.BlockSpec((CHUNK, GLA_IN_PAD), lambda n: (nc - 1 - n, 0)),
                  pl.BlockSpec((CHUNK, GLA_VW), lambda n: (nc - 1 - n, 0)),
                  pl.BlockSpec(sblk, lambda n: (nc - 1 - n, 0, 0, 0)),
                  pl.BlockSpec(sblk, lambda n: (jnp.maximum(nc - 2 - n, 0), 0, 0, 0)),
                  full(wg), full(bg), full(hg)],
        out_specs=[pl.BlockSpec((CHUNK, GLA_IN_PAD), lambda n: (nc - 1 - n, 0)), full(wg), full(bg), full(hg)],
        out_shape=[jax.ShapeDtypeStruct((t, GLA_IN_PAD), BF16), jax.ShapeDtypeStruct(wg.shape, F32),
                   jax.ShapeDtypeStruct(bg.shape, F32), jax.ShapeDtypeStruct(hg.shape, F32)],
        scratch_shapes=[pltpu.VMEM((GLA_HEADS, GLA_DV, GLA_DK), F32)],
        compiler_params=_cp("arbitrary"), name=name)(p, dog, sall, sall, wg, bg, hg)


def _gla_layer_fwd(h, w, tag):
    p = _mm(h, w["w_in"], name=tag + "_in")
    og, sall = _gla_fwd(p, w["wg"], w["bg"], w["hg"], name=tag + "_scan")
    y = _mm(og, w["w_out"], name=tag + "_out")
    return y, (p, og, sall)


def _gla_layer_bwd(dy, h, saved, w, tag):
    p, og, sall = saved
    dog = _mm(dy, w["w_out"], tb=True, name=tag + "_dog")
    dw_out = _mm(og, dy, ta=True, name=tag + "_dwout")
    dp, dwg, dbg, dhg = _gla_bwd(p, dog, sall, w["wg"], w["bg"], w["hg"], name=tag + "_scanb")
    dw_in = _mm(h, dp, ta=True, name=tag + "_dwin")
    dh = _mm(dp, w["w_in"], tb=True, name=tag + "_dh")
    grads = dict(w_in=dw_in[:, :GLA_IN], w_gate_up=dwg[:GLA_RANK], b_gate=dbg[0], head_g=dhg[0], w_out=dw_out)
    return dh, grads


def _rope_tables(pos, inv_freq, *, name):
    t = pos.shape[0]
    tr = _row_tile(t)
    half = MLA_ROPE // 2

    def body(p_ref, f_ref, c_ref, s1_ref, s2_ref, s1b_ref, s2b_ref):
        ang = p_ref[...].astype(F32) * f_ref[...]
        lane = _iota2((tr, LANES), 1)
        lo = (lane >= MLA_NOPE) & (lane < MLA_NOPE + half)
        hi = (lane >= MLA_NOPE + half) & (lane < MLA_QK)
        cs, sn = jnp.cos(ang), jnp.sin(ang)
        zero = jnp.zeros_like(cs)
        c_ref[...] = jnp.where(lane < MLA_NOPE, 1.0, jnp.where(lane < MLA_QK, cs, 0.0))
        s1_ref[...] = jnp.where(lo, -sn, zero)
        s2_ref[...] = jnp.where(hi, sn, zero)
        s1b_ref[...] = jnp.where(lo, sn, zero)
        s2b_ref[...] = jnp.where(hi, -sn, zero)

    row = pl.BlockSpec((tr, LANES), lambda i: (i, 0))
    s = jax.ShapeDtypeStruct((t, LANES), F32)
    return pl.pallas_call(body, grid=(t // tr,),
                          in_specs=[pl.BlockSpec((tr, 1), lambda i: (i, 0)), pl.BlockSpec((1, LANES), lambda i: (0, 0))],
                          out_specs=[row] * 5, out_shape=[s] * 5, compiler_params=_cp("parallel"), name=name)(pos, inv_freq)


def _rope(x, c, s1, s2, *, out_dtype, sum_heads=False, name):
    nh, t, _ = x.shape
    tr = _row_tile(t)
    half = MLA_ROPE // 2

    def body(x_ref, c_ref, s1_ref, s2_ref, o_ref):
        xv = x_ref[0].astype(F32)
        y = xv * c_ref[...] + pltpu.roll(xv, LANES - half, 1) * s1_ref[...] + pltpu.roll(xv, half, 1) * s2_ref[...]
        if sum_heads:
            @pl.when(pl.program_id(1) == 0)
            def _():
                o_ref[...] = jnp.zeros_like(o_ref)
            o_ref[...] += y
        else:
            o_ref[0] = y.astype(out_dtype)

    tab = pl.BlockSpec((tr, LANES), lambda i, h: (i, 0))
    xs = pl.BlockSpec((1, tr, LANES), lambda i, h: (h, i, 0))
    if sum_heads:
        return pl.pallas_call(body, grid=(t // tr, nh), in_specs=[xs, tab, tab, tab], out_specs=tab,
                              out_shape=jax.ShapeDtypeStruct((t, LANES), F32),
                              compiler_params=_cp("parallel", "arbitrary"), name=name)(x, c, s1, s2)
    return pl.pallas_call(body, grid=(t // tr, nh), in_specs=[xs, tab, tab, tab], out_specs=xs,
                          out_shape=jax.ShapeDtypeStruct(x.shape, out_dtype),
                          compiler_params=_cp("parallel", "parallel"), name=name)(x, c, s1, s2)


FLASH_BLK = 512


def _diag_mask(blk):
    return (_iota2((blk, blk), 1) // CHUNK) <= (_iota2((blk, blk), 0) // CHUNK)


def _flash_fwd(q, k, v, *, name):
    nh, t, _ = q.shape
    blk = min(FLASH_BLK, t)
    scale = MLA_QK ** -0.5

    def body(q_ref, k_ref, v_ref, o_ref, lse_ref):
        i = pl.program_id(1)
        qv = q_ref[0]

        def step(j, carry, masked):
            m, l, acc = carry
            off = pl.multiple_of(j * blk, blk)
            kb = k_ref[0, pl.ds(off, blk), :]
            vb = v_ref[0, pl.ds(off, blk), :]
            s = _dg(qv, kb, NT) * scale
            if masked:
                s = jnp.where(_diag_mask(blk), s, NEG_INF)
            m_new = jnp.maximum(m, jnp.max(s, axis=-1, keepdims=True))
            p = jnp.exp(s - m_new)
            alpha = jnp.exp(m - m_new)
            return m_new, alpha * l + jnp.sum(p, axis=-1, keepdims=True), alpha * acc + _dg(p, vb, NN)

        init = (jnp.full((blk, 1), NEG_INF, F32), jnp.zeros((blk, 1), F32), jnp.zeros((blk, MLA_V), F32))
        carry = lax.fori_loop(0, i, lambda j, c: step(j, c, False), init)
        m, l, acc = step(i, carry, True)
        o_ref[0] = (acc / l).astype(BF16)
        lse_ref[0] = m + jnp.log(l)

    qs = pl.BlockSpec((1, blk, LANES), lambda h, i: (h, i, 0))
    return pl.pallas_call(
        body, grid=(nh, t // blk),
        in_specs=[qs, pl.BlockSpec((1, t, LANES), lambda h, i: (h, 0, 0)), pl.BlockSpec((1, t, MLA_V), lambda h, i: (h, 0, 0))],
        out_specs=[pl.BlockSpec((1, blk, MLA_V), lambda h, i: (h, i, 0)), pl.BlockSpec((1, blk, 1), lambda h, i: (h, i, 0))],
        out_shape=[jax.ShapeDtypeStruct((nh, t, MLA_V), BF16), jax.ShapeDtypeStruct((nh, t, 1), F32)],
        compiler_params=_cp("parallel", "parallel"), name=name)(q, k, v)


def _flash_dq(q, k, v, do, o, lse, *, name):
    nh, t, _ = q.shape
    blk = min(FLASH_BLK, t)
    scale = MLA_QK ** -0.5

    def body(q_ref, k_ref, v_ref, do_ref, o_ref, lse_ref, dq_ref, dl_ref):
        i = pl.program_id(1)
        qv = q_ref[0]
        dov = do_ref[0]
        lse_v = lse_ref[0]
        delta = jnp.sum(dov.astype(F32) * o_ref[0].astype(F32), axis=-1, keepdims=True)
        dl_ref[0] = delta

        def step(j, dq, masked):
            off = pl.multiple_of(j * blk, blk)
            kb = k_ref[0, pl.ds(off, blk), :]
            vb = v_ref[0, pl.ds(off, blk), :]
            s = _dg(qv, kb, NT) * scale
            if masked:
                s = jnp.where(_diag_mask(blk), s, NEG_INF)
            p = jnp.exp(s - lse_v)
            ds = p * (_dg(dov, vb, NT) - delta) * scale
            return dq + _dg(ds, kb, NN)

        dq = lax.fori_loop(0, i, lambda j, c: step(j, c, False), jnp.zeros((blk, LANES), F32))
        dq_ref[0] = step(i, dq, True)

    qs = pl.BlockSpec((1, blk, LANES), lambda h, i: (h, i, 0))
    vs = pl.BlockSpec((1, blk, MLA_V), lambda h, i: (h, i, 0))
    ls = pl.BlockSpec((1, blk, 1), lambda h, i: (h, i, 0))
    return pl.pallas_call(
        body, grid=(nh, t // blk),
        in_specs=[qs, pl.BlockSpec((1, t, LANES), lambda h, i: (h, 0, 0)), pl.BlockSpec((1, t, MLA_V), lambda h, i: (h, 0, 0)), vs, vs, ls],
        out_specs=[qs, ls],
        out_shape=[jax.ShapeDtypeStruct((nh, t, LANES), F32), jax.ShapeDtypeStruct((nh, t, 1), F32)],
        compiler_params=_cp("parallel", "parallel"), name=name)(q, k, v, do, o, lse)


def _flash_dkv(q, k, v, do, lse, delta, *, name):
    nh, t, _ = q.shape
    blk = min(FLASH_BLK, t)
    nq = t // blk
    scale = MLA_QK ** -0.5

    def body(q_ref, k_ref, v_ref, do_ref, lse_ref, dl_ref, dk_ref, dv_ref):
        j = pl.program_id(1)
        kb = k_ref[0]
        vb = v_ref[0]

        def step(i, carry, masked):
            dk, dv = carry
            off = pl.multiple_of(i * blk, blk)
            qb = q_ref[0, pl.ds(off, blk), :]
            dob = do_ref[0, pl.ds(off, blk), :]
            s = _dg(qb, kb, NT) * scale
            if masked:
                s = jnp.where(_diag_mask(blk), s, NEG_INF)
            p = jnp.exp(s - lse_ref[0, pl.ds(off, blk), :])
            ds = p * (_dg(dob, vb, NT) - dl_ref[0, pl.ds(off, blk), :]) * scale
            return dk + _dg(ds, qb, TN), dv + _dg(p, dob, TN)

        carry = step(j, (jnp.zeros((blk, LANES), F32), jnp.zeros((blk, MLA_V), F32)), True)
        dk, dv = lax.fori_loop(j + 1, nq, lambda i, c: step(i, c, False), carry)
        dk_ref[0] = dk
        dv_ref[0] = dv

    ks = pl.BlockSpec((1, blk, LANES), lambda h, j: (h, j, 0))
    vs = pl.BlockSpec((1, blk, MLA_V), lambda h, j: (h, j, 0))
    fl = lambda w: pl.BlockSpec((1, t, w), lambda h, j: (h, 0, 0))
    return pl.pallas_call(
        body, grid=(nh, nq),
        in_specs=[fl(LANES), ks, vs, fl(MLA_V), fl(1), fl(1)],
        out_specs=[ks, vs],
        out_shape=[jax.ShapeDtypeStruct((nh, t, LANES), F32), jax.ShapeDtypeStruct((nh, t, MLA_V), F32)],
        compiler_params=_cp("parallel", "parallel"), name=name)(q, k, v, do, lse, delta)


def _heads_first(a, width):
    t = a.shape[0]
    return a.reshape(t, MLA_HEADS, width).transpose(1, 0, 2)


def _heads_last(a):
    return a.transpose(1, 0, 2).reshape(a.shape[1], -1)


def _mla_layer_fwd(h, pos, w, tag):
    t = h.shape[0]
    zq = jnp.zeros((1, MLA_Q_RANK), F32)
    zkv = jnp.zeros((1, MLA_KV_RANK), F32)
    p = _mm(h, w["w_in"], name=tag + "_in")
    cq, ckv, krp = p[:, :MLA_Q_RANK], p[:, MLA_Q_RANK:MLA_Q_RANK + MLA_KV_RANK], p[:, MLA_Q_RANK + MLA_KV_RANK:MLA_IN]
    qn = _premod_fwd(cq, w["q_norm_g"], zq, zq, name=tag + "_qnorm")
    kvn = _premod_fwd(ckv, w["kv_norm_g"], zkv, zkv, name=tag + "_kvnorm")
    q = _mm(qn, w["w_uq"], name=tag + "_uq")
    kv = _mm(kvn, w["w_ukv"], name=tag + "_ukv")
    q_pre = jnp.pad(_heads_first(q, MLA_QK), ((0, 0), (0, 0), (0, LANES - MLA_QK)))
    kv3 = _heads_first(kv, MLA_NOPE + MLA_V)
    k_pre = jnp.concatenate([kv3[:, :, :MLA_NOPE], jnp.broadcast_to(krp[None], (MLA_HEADS, t, MLA_ROPE)),
                             jnp.zeros((MLA_HEADS, t, LANES - MLA_QK), F32)], axis=-1)
    vh = kv3[:, :, MLA_NOPE:].astype(BF16)
    half = MLA_ROPE // 2
    freq = ROPE_THETA ** (-jnp.arange(half, dtype=F32) / half)
    inv_freq = jnp.concatenate([jnp.zeros((MLA_NOPE,), F32), freq, freq, jnp.zeros((LANES - MLA_QK,), F32)])[None]
    tabs = _rope_tables(pos.reshape(t, 1), inv_freq, name=tag + "_tables")
    qr = _rope(q_pre, tabs[0], tabs[1], tabs[2], out_dtype=BF16, name=tag + "_ropeq")
    kr = _rope(k_pre, tabs[0], tabs[1], tabs[2], out_dtype=BF16, name=tag + "_ropek")
    o, lse = _flash_fwd(qr, kr, vh, name=tag + "_attn")
    of = _heads_last(o)
    y = _mm(of, w["w_out"], name=tag + "_out")
    return y, (cq, ckv, qn, kvn, qr, kr, vh, o, lse, of, tabs)


def _mla_layer_bwd(dy, h, saved, w, tag):
    cq, ckv, qn, kvn, qr, kr, vh, o, lse, of, tabs = saved
    t = h.shape[0]
    zq = jnp.zeros((1, MLA_Q_RANK), F32)
    zkv = jnp.zeros((1, MLA_KV_RANK), F32)
    dof = _mm(dy, w["w_out"], tb=True, out_dtype=BF16, name=tag + "_dof")
    dw_out = _mm(of, dy, ta=True, name=tag + "_dwout")
    do = _heads_first(dof, MLA_V)
    dqr, delta = _flash_dq(qr, kr, vh, do, o, lse, name=tag + "_attn_dq")
    dkr, dv = _flash_dkv(qr, kr, vh, do, lse, delta, name=tag + "_attn_dkv")
    dq_pre = _rope(dqr, tabs[0], tabs[3], tabs[4], out_dtype=F32, name=tag + "_ropeq_b")
    dk_sum = _rope(dkr, tabs[0], tabs[3], tabs[4], out_dtype=F32, sum_heads=True, name=tag + "_ropek_b")
    dq = _heads_last(dq_pre[:, :, :MLA_QK])
    dkv = _heads_last(jnp.concatenate([dkr[:, :, :MLA_NOPE], dv], axis=-1))
    dw_uq = _mm(qn, dq, ta=True, name=tag + "_dwuq")
    dqn = _mm(dq, w["w_uq"], tb=True, name=tag + "_dqn")
    dw_ukv = _mm(kvn, dkv, ta=True, name=tag + "_dwukv")
    dkvn = _mm(dkv, w["w_ukv"], tb=True, name=tag + "_dkvn")
    dcq, dqg, _, _ = _premod_bwd(dqn, cq, w["q_norm_g"], zq, jnp.zeros_like(cq), name=tag + "_qnorm_b")
    dckv, dkvg, _, _ = _premod_bwd(dkvn, ckv, w["kv_norm_g"], zkv, jnp.zeros_like(ckv), name=tag + "_kvnorm_b")
    dp = jnp.concatenate([dcq, dckv, dk_sum[:, MLA_NOPE:MLA_QK], jnp.zeros((t, MLA_IN_PAD - MLA_IN), F32)], axis=1).astype(BF16)
    dw_in = _mm(h, dp, ta=True, name=tag + "_dwin")
    dh = _mm(dp, w["w_in"], tb=True, name=tag + "_dh")
    grads = dict(w_in=dw_in[:, :MLA_IN], q_norm_g=dqg[0], w_uq=dw_uq, kv_norm_g=dkvg[0], w_ukv=dw_ukv, w_out=dw_out)
    return dh, grads


CONV_HALO = 8


def _conv_tiles(t):
    return min(512, t), 512


def _gdn_conv_fwd(p, w, *, name):
    t = p.shape[0]
    tr, tc = _conv_tiles(t)
    hb = tr // CONV_HALO

    def body(x_ref, halo_ref, w_ref, o_ref, buf):
        i = pl.program_id(0)
        buf[0:CONV_HALO, :] = halo_ref[...] * (i > 0).astype(F32)
        buf[CONV_HALO:CONV_HALO + tr, :] = x_ref[...]
        base = CONV_HALO - (GDN_CONV - 1)
        acc = buf[pl.ds(base, tr), :] * w_ref[0:1, :]
        for j in range(1, GDN_CONV):
            acc = acc + buf[pl.ds(base + j, tr), :] * w_ref[j:j + 1, :]
        o_ref[...] = acc * _sigmoid(acc)

    return pl.pallas_call(
        body, grid=(t // tr, GDN_CONV_W // tc),
        in_specs=[pl.BlockSpec((tr, tc), lambda i, j: (i, j)),
                  pl.BlockSpec((CONV_HALO, tc), lambda i, j: (jnp.maximum(i * hb - 1, 0), j)),
                  pl.BlockSpec((GDN_CONV, tc), lambda i, j: (0, j))],
        out_specs=pl.BlockSpec((tr, tc), lambda i, j: (i, j)),
        out_shape=jax.ShapeDtypeStruct((t, GDN_CONV_W), F32),
        scratch_shapes=[pltpu.VMEM((tr + CONV_HALO, tc), F32)],
        compiler_params=_cp("parallel", "parallel"), name=name)(p, p, w)


def _gdn_conv_bwd(dqkv, p, w, *, name):
    t = p.shape[0]
    tr, tc = _conv_tiles(t)
    hb = tr // CONV_HALO
    nr = t // tr
    ext = tr + CONV_HALO

    def body(x_ref, xp_ref, xn_ref, d_ref, dn_ref, w_ref, dx_ref, dw_ref, bufx, bufd):
        i = pl.program_id(1)

        @pl.when(i == 0)
        def _():
            dw_ref[...] = jnp.zeros_like(dw_ref)

        last = (i < nr - 1).astype(F32)
        bufx[0:CONV_HALO, :] = xp_ref[...] * (i > 0).astype(F32)
        bufx[CONV_HALO:CONV_HALO + tr, :] = x_ref[...]
        bufx[CONV_HALO + tr:, :] = xn_ref[...] * last
        base = CONV_HALO - (GDN_CONV - 1)
        acc = bufx[pl.ds(base, ext), :] * w_ref[0:1, :]
        for j in range(1, GDN_CONV):
            acc = acc + bufx[pl.ds(base + j, ext), :] * w_ref[j:j + 1, :]
        sg = _sigmoid(acc)
        dsilu = sg * (1.0 + acc * (1.0 - sg))
        bufd[0:tr, :] = d_ref[...] * dsilu[0:tr, :]
        bufd[tr:, :] = dn_ref[...] * last * dsilu[tr:, :]
        dx = bufd[pl.ds(GDN_CONV - 1, tr), :] * w_ref[0:1, :]
        for j in range(1, GDN_CONV):
            dx = dx + bufd[pl.ds(GDN_CONV - 1 - j, tr), :] * w_ref[j:j + 1, :]
        dx_ref[...] = dx.astype(BF16)
        dc = bufd[0:tr, :]
        for j in range(GDN_CONV):
            dw_ref[j:j + 1, :] += jnp.sum(dc * bufx[pl.ds(base + j, tr), :], axis=0, keepdims=True)

    main = pl.BlockSpec((tr, tc), lambda j, i: (i, j))
    prev = pl.BlockSpec((CONV_HALO, tc), lambda j, i: (jnp.maximum(i * hb - 1, 0), j))
    nxt = pl.BlockSpec((CONV_HALO, tc), lambda j, i: (jnp.minimum((i + 1) * hb, t // CONV_HALO - 1), j))
    wsp = pl.BlockSpec((GDN_CONV, tc), lambda j, i: (0, j))
    return pl.pallas_call(
        body, grid=(GDN_CONV_W // tc, nr),
        in_specs=[main, prev, nxt, main, nxt, wsp], out_specs=[main, wsp],
        out_shape=[jax.ShapeDtypeStruct((t, GDN_CONV_W), BF16), jax.ShapeDtypeStruct((GDN_CONV, GDN_CONV_W), F32)],
        scratch_shapes=[pltpu.VMEM((tr + 2 * CONV_HALO, tc), F32), pltpu.VMEM((ext, tc), F32)],
        compiler_params=_cp("parallel", "arbitrary"), name=name)(p, p, p, dqkv, dqkv, w)


def _dot_sel(sel, b, dims=NN, sel_first=True):
    s = sel.astype(BF16)
    b1 = b.astype(BF16)
    r1 = b - b1.astype(F32)
    b2 = r1.astype(BF16)
    b3 = (r1 - b2.astype(F32)).astype(BF16)
    if sel_first:
        d = lambda v: lax.dot_general(s, v, (dims, ((), ())), preferred_element_type=F32)
    else:
        d = lambda v: lax.dot_general(v, s, (dims, ((), ())), preferred_element_type=F32)
    return d(b1) + (d(b2) + d(b3))


def _gdn_chunks(qraws, kraws, vs, braws, araws, alogs, dtbs):
    nv = len(vs)
    row = _iota2((CHUNK, CHUNK), 0)
    col = _iota2((CHUNK, CHUNK), 1)
    strict = row > col
    triu = (row <= col).astype(F32)
    tril = (row >= col).astype(F32)
    ones = jnp.ones((CHUNK, CHUNK), F32)
    keys = []
    for qraw, kraw in zip(qraws, kraws):
        rq = lax.rsqrt(jnp.sum(qraw * qraw, axis=-1, keepdims=True) + EPS)
        rk = lax.rsqrt(jnp.sum(kraw * kraw, axis=-1, keepdims=True) + EPS)
        qn = qraw * rq
        keys.append(dict(rq=rq, rk=rk, qn=qn, qh=qn * (GDN_DK ** -0.5), kh=kraw * rk))
    kks = [_dg(kd["kh"], kd["kh"], NT) for kd in keys]
    cs = []
    for h in range(nv):
        c = dict(keys[h // 2])
        c.update(v=vs[h], kk=kks[h // 2], strict=strict, triu=triu)
        c["beta"] = _sigmoid(braws[h])
        c["ea"] = jnp.exp(alogs[h])
        c["xs"] = araws[h] + dtbs[h]
        c["g"] = -c["ea"] * _softplus(c["xs"])
        cs.append(c)
    gbs = [jnp.broadcast_to(c["g"], (CHUNK, LANES)) for c in cs]
    cums = [_dot_sel(tril, gb) for gb in gbs]
    cum_js = [_dot_sel(ones, gb[:, :CHUNK] * triu) for gb in gbs]
    for c, cum, cum_j in zip(cs, cums, cum_js):
        diff = jnp.where(strict, cum[:, :CHUNK] - cum_j, 0.0)
        c["dm"] = jnp.where(strict, jnp.exp(diff), 0.0)
        c["a"] = (c["beta"] * c["dm"]) * c["kk"]
        c_last = cum[CHUNK - 1:CHUNK, :]
        c["e"] = jnp.exp(cum)
        c["f"] = jnp.exp(c_last - cum)
        c["dec"] = jnp.exp(c_last)
        c["rv"] = c["beta"] * c["v"]
        c["rk_rhs"] = (c["beta"] * c["e"]) * c["kh"]
        c["ke"] = c["kh"] * c["f"]
    return cs


def _unit_lower_inverses(mats):
    eye = (_iota2((CHUNK, CHUNK), 0) == _iota2((CHUNK, CHUNK), 1)).astype(F32)
    ts = [eye - a for a in mats]
    pws = list(mats)
    for _ in range(5):
        pws = [_dot3(pw, pw) for pw in pws]
        ts = [t + _dot3(t, pw) for t, pw in zip(ts, pws)]
    return ts


GDN_HB = 4


def _gdn_specs(chunk_of):
    hb = GDN_HB
    kw = hb // 2 * GDN_DK
    vw = hb * GDN_DV
    qs = pl.BlockSpec((CHUNK, kw), lambda g, n: (chunk_of(n), g))
    ks = pl.BlockSpec((CHUNK, kw), lambda g, n: (chunk_of(n), GDN_QKW // kw + g))
    vs = pl.BlockSpec((CHUNK, vw), lambda g, n: (chunk_of(n), 2 * GDN_QKW // vw + g))
    zs = pl.BlockSpec((CHUNK, vw), lambda g, n: (chunk_of(n), GDN_CONV_W // vw + g))
    col = pl.BlockSpec((hb, CHUNK, 1), lambda g, n: (g, chunk_of(n), 0))
    one = pl.BlockSpec((hb, 1, 1), lambda g, n: (g, 0, 0))
    ng = pl.BlockSpec((1, GDN_DV), lambda g, n: (0, 0))
    hd = pl.BlockSpec((CHUNK, vw), lambda g, n: (chunk_of(n), g))
    return qs, ks, vs, zs, col, one, ng, hd


def _gdn_fwd(qkv, p, braw, araw, alog, dtb, ng, *, name):
    t = qkv.shape[0]
    nc = t // CHUNK
    nh = GDN_V_HEADS

    def body(q_ref, k_ref, v_ref, z_ref, braw_ref, araw_ref, alog_ref, dtb_ref, ng_ref, og_ref, s_ref, t_ref, st):
        @pl.when(pl.program_id(1) == 0)
        def _():
            st[...] = jnp.zeros_like(st)

        hs = range(GDN_HB)
        kqs = [slice(j * GDN_DK, (j + 1) * GDN_DK) for j in range(GDN_HB // 2)]
        vsl = [slice(h * GDN_DV, (h + 1) * GDN_DV) for h in hs]
        cs = _gdn_chunks([q_ref[:, s] for s in kqs], [k_ref[:, s] for s in kqs], [v_ref[:, s] for s in vsl],
                         [braw_ref[h] for h in hs], [araw_ref[h] for h in hs], [alog_ref[h] for h in hs], [dtb_ref[h] for h in hs])
        tms = _unit_lower_inverses([c["a"] for c in cs])
        s0 = [st[h] for h in hs]
        wv = [_dot3(tms[h], cs[h]["rv"]) for h in hs]
        wk = [_dot3(tms[h], cs[h]["rk_rhs"]) for h in hs]
        u = [wv[h] - _dg(wk[h], s0[h], NN) for h in hs]
        s1 = [cs[h]["dec"] * s0[h] + _dg(cs[h]["ke"], u[h], TN) for h in hs]
        o = [_dg(cs[h]["qh"], s1[h], NN) for h in hs]
        for h in hs:
            t_ref[h, 0] = tms[h]
            st[h] = s1[h]
            s_ref[h, 0] = s1[h]
            rn = lax.rsqrt(jnp.mean(o[h] * o[h], axis=-1, keepdims=True) + EPS)
            zv = z_ref[:, vsl[h]]
            og_ref[:, vsl[h]] = (((o[h] * rn) * ng_ref[...]) * (zv * _sigmoid(zv))).astype(BF16)

    qs, ks, vs, zs, col, one, ngs, hd = _gdn_specs(lambda n: n)
    return pl.pallas_call(
        body, grid=(nh // GDN_HB, nc),
        in_specs=[qs, ks, vs, zs, col, col, one, one, ngs],
        out_specs=[hd,
                   pl.BlockSpec((GDN_HB, 1, GDN_DK, GDN_DV), lambda g, n: (g, n, 0, 0)),
                   pl.BlockSpec((GDN_HB, 1, CHUNK, CHUNK), lambda g, n: (g, n, 0, 0))],
        out_shape=[jax.ShapeDtypeStruct((t, GDN_VW), BF16), jax.ShapeDtypeStruct((nh, nc, GDN_DK, GDN_DV), F32),
                   jax.ShapeDtypeStruct((nh, nc, CHUNK, CHUNK), F32)],
        scratch_shapes=[pltpu.VMEM((GDN_HB, GDN_DK, GDN_DV), F32)],
        compiler_params=_cp("parallel", "arbitrary"), name=name)(qkv, qkv, qkv, p, braw, araw, alog, dtb, ng)


def _gdn_bwd(qkv, p, braw, araw, alog, dtb, ng, dog, sall, tall, *, name):
    t = qkv.shape[0]
    nc = t // CHUNK
    nh = GDN_V_HEADS

    def body(q_ref, k_ref, v_ref, z_ref, braw_ref, araw_ref, alog_ref, dtb_ref, ng_ref, dog_ref, s1_ref, s0_ref, t_ref,
             dq_ref, dk_ref, dv_ref, dz_ref, dbraw_ref, daraw_ref, dalog_ref, ddtb_ref, dng_ref, gc):
        grp = pl.program_id(0)
        i = pl.program_id(1)

        @pl.when(i == 0)
        def _():
            gc[...] = jnp.zeros_like(gc)
            dalog_ref[...] = jnp.zeros_like(dalog_ref)
            ddtb_ref[...] = jnp.zeros_like(ddtb_ref)

        @pl.when((i == 0) & (grp == 0))
        def _():
            dng_ref[...] = jnp.zeros_like(dng_ref)

        has_prev = (i < nc - 1).astype(F32)
        ngv = ng_ref[...]
        ones = jnp.ones((CHUNK, LANES), F32)
        hs = range(GDN_HB)
        kqs = [slice(j * GDN_DK, (j + 1) * GDN_DK) for j in range(GDN_HB // 2)]
        vsl = [slice(h * GDN_DV, (h + 1) * GDN_DV) for h in hs]
        cs = _gdn_chunks([q_ref[:, s] for s in kqs], [k_ref[:, s] for s in kqs], [v_ref[:, s] for s in vsl],
                         [braw_ref[h] for h in hs], [araw_ref[h] for h in hs], [alog_ref[h] for h in hs], [dtb_ref[h] for h in hs])
        tms = [t_ref[h, 0] for h in hs]
        s1 = [s1_ref[h, 0] for h in hs]
        s0 = [s0_ref[h, 0] * has_prev for h in hs]
        wv = [_dot3(tms[h], cs[h]["rv"]) for h in hs]
        wk = [_dot3(tms[h], cs[h]["rk_rhs"]) for h in hs]
        u = [wv[h] - _dg(wk[h], s0[h], NN) for h in hs]
        o = [_dg(cs[h]["qh"], s1[h], NN) for h in hs]
        dng = jnp.zeros((1, GDN_DV), F32)
        do = []
        for h in hs:
            zv = z_ref[:, vsl[h]]
            dogv = dog_ref[:, vsl[h]]
            rn = lax.rsqrt(jnp.mean(o[h] * o[h], axis=-1, keepdims=True) + EPS)
            zo = o[h] * rn
            sg = _sigmoid(zv)
            sl = zv * sg
            dng = dng + jnp.sum(dogv * zo * sl, axis=0, keepdims=True)
            dz_ref[:, vsl[h]] = dogv * (zo * ngv) * (sg * (1.0 + zv * (1.0 - sg)))
            dzo = dogv * sl * ngv
            do.append(rn * (dzo - zo * jnp.mean(dzo * zo, axis=-1, keepdims=True)))
        dng_ref[...] += dng
        g_tot = [gc[h] + _dg(cs[h]["qh"], do[h], TN) for h in hs]
        dqh = [_dg(do[h], s1[h], NT) for h in hs]
        dke = [_dg(u[h], g_tot[h], NT) for h in hs]
        du = [_dg(cs[h]["ke"], g_tot[h], NN) for h in hs]
        gnew = [cs[h]["dec"] * g_tot[h] - _dg(wk[h], du[h], TN) for h in hs]
        dwk = [-_dg(du[h], s0[h], NT) for h in hs]
        drv = [_dot3(tms[h], du[h], TN) for h in hs]
        drk = [_dot3(tms[h], dwk[h], TN) for h in hs]
        da = [jnp.where(cs[h]["strict"], -(_dot3(drv[h], wv[h], NT) + _dot3(drk[h], wk[h], NT)), 0.0) for h in hs]
        mx = [da[h] * cs[h]["dm"] * cs[h]["kk"] for h in hs]
        aa = [mx[h] * cs[h]["beta"] for h in hs]
        colsum = [_dot_sel(ones, aa[h], TN, sel_first=False)[:, 0:1] for h in hs]
        bm = [(da[h] * cs[h]["beta"]) * cs[h]["dm"] for h in hs]
        dkh = [_dg(bm[h], cs[h]["kh"], NN) + _dg(bm[h], cs[h]["kh"], TN) for h in hs]
        dcum, dcl, dbeta = [], [], []
        for h in hs:
            c = cs[h]
            beta, kh, e, f, dec, ke = c["beta"], c["kh"], c["e"], c["f"], c["dec"], c["ke"]
            gc[h] = gnew[h]
            ddec = jnp.sum(jnp.sum(g_tot[h] * s0[h], axis=1, keepdims=True), axis=0, keepdims=True)
            dv_ref[:, vsl[h]] = beta * drv[h]
            db = jnp.sum(mx[h], axis=1, keepdims=True) + jnp.sum(drv[h] * c["v"], axis=1, keepdims=True)
            dbeta.append(db + jnp.sum(drk[h] * (e * kh), axis=1, keepdims=True))
            dkh[h] = dkh[h] + (beta * e) * drk[h] + f * dke[h]
            ef = jnp.sum(dke[h] * ke, axis=1, keepdims=True)
            dcum.append(jnp.sum(aa[h], axis=1, keepdims=True) - colsum[h] + jnp.sum(drk[h] * c["rk_rhs"], axis=1, keepdims=True) - ef)
            dcl.append(jnp.sum(ef, axis=0, keepdims=True) + ddec * dec[:, 0:1])
        dg = [_dot_sel(cs[h]["triu"], jnp.broadcast_to(dcum[h], (CHUNK, LANES)))[:, 0:1] + dcl[h] for h in hs]
        for h in hs:
            c = cs[h]
            beta, kh = c["beta"], c["kh"]
            daraw = dg[h] * (-c["ea"]) * _sigmoid(c["xs"])
            daraw_ref[h] = daraw
            dbraw_ref[h] = dbeta[h] * beta * (1.0 - beta)
            dalog_ref[h] += jnp.sum(dg[h] * c["g"], axis=0, keepdims=True)
            ddtb_ref[h] += jnp.sum(daraw, axis=0, keepdims=True)
            dn = dqh[h] * (GDN_DK ** -0.5)
            dq_ref[:, vsl[h]] = c["rq"] * (dn - c["qn"] * jnp.sum(dn * c["qn"], axis=-1, keepdims=True))
            dk_ref[:, vsl[h]] = c["rk"] * (dkh[h] - kh * jnp.sum(dkh[h] * kh, axis=-1, keepdims=True))

    rev = lambda n: nc - 1 - n
    qs, ks, vs, zs, col, one, ngs, hd = _gdn_specs(rev)
    s1s = pl.BlockSpec((GDN_HB, 1, GDN_DK, GDN_DV), lambda g, n: (g, rev(n), 0, 0))
    s0s = pl.BlockSpec((GDN_HB, 1, GDN_DK, GDN_DV), lambda g, n: (g, jnp.maximum(rev(n) - 1, 0), 0, 0))
    ts = pl.BlockSpec((GDN_HB, 1, CHUNK, CHUNK), lambda g, n: (g, rev(n), 0, 0))
    big = jax.ShapeDtypeStruct((t, GDN_VW), F32)
    cols = jax.ShapeDtypeStruct((nh, t, 1), F32)
    ones_s = jax.ShapeDtypeStruct((nh, 1, 1), F32)
    return pl.pallas_call(
        body, grid=(nh // GDN_HB, nc),
        in_specs=[qs, ks, vs, zs, col, col, one, one, ngs, hd, s1s, s0s, ts],
        out_specs=[hd, hd, hd, hd, col, col, one, one, ngs],
        out_shape=[big, big, big, big, cols, cols, ones_s, ones_s, jax.ShapeDtypeStruct((1, GDN_DV), F32)],
        scratch_shapes=[pltpu.VMEM((GDN_HB, GDN_DK, GDN_DV), F32)],
        compiler_params=_cp("arbitrary", "arbitrary"), name=name)(qkv, qkv, qkv, p, braw, araw, alog, dtb, ng, dog, sall, sall, tall)


def _gdn_layer_fwd(h, w, tag):
    p = _mm(h, w["w_in"], name=tag + "_in")
    qkv = _gdn_conv_fwd(p, w["conv_w"], name=tag + "_conv")
    braw = p[:, GDN_CONV_W + GDN_VW:GDN_CONV_W + GDN_VW + GDN_V_HEADS].T[:, :, None]
    araw = p[:, GDN_CONV_W + GDN_VW + GDN_V_HEADS:GDN_IN].T[:, :, None]
    og, sall, tall = _gdn_fwd(qkv, p, braw, araw, w["a_log"], w["dt_bias"], w["norm_g"], name=tag + "_scan")
    y = _mm(og, w["w_out"], name=tag + "_out")
    return y, (p, qkv, braw, araw, og, sall, tall)


def _gdn_layer_bwd(dy, h, saved, w, tag):
    p, qkv, braw, araw, og, sall, tall = saved
    t = h.shape[0]
    dog = _mm(dy, w["w_out"], tb=True, name=tag + "_dog")
    dw_out = _mm(og, dy, ta=True, name=tag + "_dwout")
    dq16, dk16, dv, dz, dbraw, daraw, dalog, ddtb, dng = _gdn_bwd(
        qkv, p, braw, araw, w["a_log"], w["dt_bias"], w["norm_g"], dog, sall, tall, name=tag + "_scanb")
    pair = lambda a: a.reshape(t, GDN_K_HEADS, 2, GDN_DK).sum(axis=2).reshape(t, GDN_QKW)
    dqkv = jnp.concatenate([pair(dq16), pair(dk16), dv], axis=1)
    dpre, dconv_w = _gdn_conv_bwd(dqkv, p, w["conv_w"], name=tag + "_convb")
    dp = jnp.concatenate([dpre, dz.astype(BF16), dbraw[:, :, 0].T.astype(BF16), daraw[:, :, 0].T.astype(BF16),
                          jnp.zeros((t, GDN_IN_PAD - GDN_IN), BF16)], axis=1)
    dw_in = _mm(h, dp, ta=True, name=tag + "_dwin")
    dh = _mm(dp, w["w_in"], tb=True, name=tag + "_dh")
    grads = dict(w_in=dw_in[:, :GDN_IN], conv_w=dconv_w, a_log=dalog[:, 0, 0], dt_bias=ddtb[:, 0, 0], norm_g=dng[0], w_out=dw_out)
    return dh, grads


MESH_ID = pl.DeviceIdType.MESH
FLAT_W = 1024
FLAT_ROWS = 13056
FLAT_TILE = 384


def _exchange(name, ins, out_shapes, plan, n_remote, n_local):
    def body(*refs):
        in_refs = refs[:len(ins)]
        out_refs = refs[len(ins):len(ins) + len(out_shapes)]
        ssem, rsem, lsem = refs[len(ins) + len(out_shapes):]
        x, y, c = lax.axis_index("x"), lax.axis_index("y"), lax.axis_index("c")
        stages, local_copies = plan(x, y, c, in_refs, out_refs)
        assert sum(len(s) for s in stages) == n_remote and len(local_copies) == n_local
        locs = [pltpu.make_async_copy(s, d, lsem.at[i]) for i, (s, d) in enumerate(local_copies)]
        for cp in locs:
            cp.start()
        sent = []
        k = 0
        for stage in stages:
            arrivals = []
            for src, dst, peer, landing in stage:
                cp = pltpu.make_async_remote_copy(src_ref=src, dst_ref=dst, send_sem=ssem.at[k], recv_sem=rsem.at[k],
                                                  device_id=peer, device_id_type=MESH_ID)
                cp.start()
                sent.append(cp)
                arrivals.append(pltpu.make_async_remote_copy(src_ref=src, dst_ref=landing, send_sem=ssem.at[k],
                                                             recv_sem=rsem.at[k], device_id=peer, device_id_type=MESH_ID))
                k += 1
            for cp in arrivals:
                cp.wait_recv()
        for cp in sent:
            cp.wait_send()
        for cp in locs:
            cp.wait()

    hbm = pl.BlockSpec(memory_space=pl.ANY)
    return pl.pallas_call(
        body, in_specs=[hbm] * len(ins), out_specs=[hbm] * len(out_shapes), out_shape=out_shapes,
        scratch_shapes=[pltpu.SemaphoreType.DMA((n_remote,)), pltpu.SemaphoreType.DMA((n_remote,)),
                        pltpu.SemaphoreType.DMA((max(n_local, 1),))],
        name=name)(*ins)


def _other_chips(x, y):
    return [(1 - x, y), (x, 1 - y), (1 - x, 1 - y)]


def _all8_gather(a, *, name):
    def plan(x, y, c, ins, outs):
        (src,), (dst,) = ins, outs
        me = 4 * x + 2 * y + c
        stage = []
        for fx, fy, fc in [(0, 0, 1), (0, 1, 0), (0, 1, 1), (1, 0, 0), (1, 0, 1), (1, 1, 0), (1, 1, 1)]:
            px, py, pc = (1 - x if fx else x), (1 - y if fy else y), (1 - c if fc else c)
            stage.append((src, dst.at[me], (px, py, pc), dst.at[4 * px + 2 * py + pc]))
        return [stage], [(src, dst.at[me])]

    return _exchange(name, [a], [jax.ShapeDtypeStruct((8,) + a.shape, a.dtype)], plan, 7, 1)[0]


def _chip_gather(flat, *, name):
    rows = flat.shape[0]
    half = rows // 2

    def plan(x, y, c, ins, outs):
        (src,), (dst,) = ins, outs
        me = 2 * x + y
        mine = pl.ds(c * half, half)
        theirs = pl.ds((1 - c) * half, half)
        ici = [(src.at[mine], dst.at[me, mine], (px, py, c), dst.at[2 * px + py, mine]) for px, py in _other_chips(x, y)]
        d2d = [(dst.at[2 * px + py, mine], dst.at[2 * px + py, mine], (x, y, 1 - c), dst.at[2 * px + py, theirs])
               for px, py in _other_chips(x, y)]
        return [ici, d2d], []

    return _exchange(name, [flat], [jax.ShapeDtypeStruct((4,) + flat.shape, flat.dtype)], plan, 6, 0)[0]


def _add_sibling(gf, buf_a, core, *, name):
    _, rows, w = gf.shape
    half = rows // 2
    nb = half // FLAT_TILE

    def body(c_ref, g_ref, a_ref, o_ref):
        o_ref[...] = (g_ref[...] + a_ref[...]).astype(BF16)

    blk = (1, FLAT_TILE, w)
    return pl.pallas_call(
        body,
        grid_spec=pltpu.PrefetchScalarGridSpec(
            num_scalar_prefetch=1, grid=(4, nb),
            in_specs=[pl.BlockSpec(blk, lambda s, i, c_ref: (s, c_ref[0] * nb + i, 0)), pl.BlockSpec(blk, lambda s, i, c_ref: (s, i, 0))],
            out_specs=pl.BlockSpec(blk, lambda s, i, c_ref: (s, i, 0))),
        out_shape=jax.ShapeDtypeStruct((4, half, w), BF16), compiler_params=_cp("parallel", "parallel"), name=name)(core, gf, buf_a)


def _sum_chips(hsum, buf_b, chip, *, name):
    _, half, w = hsum.shape
    nb = half // FLAT_TILE

    def body(c_ref, h_ref, b0_ref, b1_ref, b2_ref, b3_ref, o_ref):
        me = c_ref[0]
        own = h_ref[0].astype(F32)
        acc = None
        for j, b_ref in enumerate((b0_ref, b1_ref, b2_ref, b3_ref)):
            term = jnp.where(me == j, own, b_ref[0].astype(F32))
            acc = term if acc is None else acc + term
        o_ref[...] = acc

    blk = (1, FLAT_TILE, w)

    def other(j):
        return pl.BlockSpec(blk, lambda i, c_ref: (jnp.where(c_ref[0] == j, (j + 1) % 4, j), i, 0))

    return pl.pallas_call(
        body,
        grid_spec=pltpu.PrefetchScalarGridSpec(
            num_scalar_prefetch=1, grid=(nb,),
            in_specs=[pl.BlockSpec(blk, lambda i, c_ref: (c_ref[0], i, 0))] + [other(j) for j in range(4)],
            out_specs=pl.BlockSpec((FLAT_TILE, w), lambda i, c_ref: (i, 0))),
        out_shape=jax.ShapeDtypeStruct((half, w), F32), compiler_params=_cp("parallel"), name=name)(chip, hsum, buf_b, buf_b, buf_b, buf_b)


def _sum_slots(buf, *, name):
    n, rows, w = buf.shape
    tr = _pick(rows, (FLAT_TILE, 8))

    def body(b_ref, o_ref):
        acc = b_ref[0]
        for s in range(1, n):
            acc = acc + b_ref[s]
        o_ref[...] = acc

    return pl.pallas_call(body, grid=(rows // tr,), in_specs=[pl.BlockSpec((n, tr, w), lambda i: (0, i, 0))],
                          out_specs=pl.BlockSpec((tr, w), lambda i: (i, 0)), out_shape=jax.ShapeDtypeStruct((rows, w), F32),
                          compiler_params=_cp("parallel"), name=name)(buf)


def _reduce_scatter(gf, core, chip, *, tag):
    _, rows, w = gf.shape
    half = rows // 2

    def plan_a(x, y, c, ins, outs):
        (src,), (dst,) = ins, outs
        return [[(src.at[:, pl.ds((1 - c) * half, half)], dst, (x, y, 1 - c), dst)]], []

    buf_a = _exchange(tag + "_sibling", [gf], [jax.ShapeDtypeStruct((4, half, w), F32)], plan_a, 1, 0)[0]
    hsum = _add_sibling(gf, buf_a, core, name=tag + "_add_sibling")

    def plan_b(x, y, c, ins, outs):
        (src,), (dst,) = ins, outs
        me = 2 * x + y
        stage = [(src.at[2 * px + py], dst.at[me], (px, py, c), dst.at[2 * px + py]) for px, py in _other_chips(x, y)]
        return [stage], []

    buf_b = _exchange(tag + "_chips", [hsum], [jax.ShapeDtypeStruct((4, half, w), BF16)], plan_b, 3, 0)[0]
    mine = _sum_chips(hsum, buf_b, chip, name=tag + "_sum_chips")

    def plan_c(x, y, c, ins, outs):
        (src,), (dst,) = ins, outs
        return [[(src, dst, (x, y, 1 - c), dst)]], []

    theirs = _exchange(tag + "_halves", [mine], [jax.ShapeDtypeStruct((half, w), F32)], plan_c, 1, 0)[0]
    first = core[0] == 0
    return jnp.concatenate([jnp.where(first, mine, theirs), jnp.where(first, theirs, mine)], axis=0)


WEIGHTS = ["ada_w", "ada_b", "norm_pre_g", "norm_post_g", "gla_w_in", "gla_w_gate_up", "gla_b_gate", "gla_head_g",
           "gla_w_out", "mla_w_in", "mla_q_norm_g", "mla_w_uq", "mla_kv_norm_g", "mla_w_ukv", "mla_w_out", "gdn_w_in",
           "gdn_conv_w", "gdn_a_log", "gdn_dt_bias", "gdn_norm_g", "gdn_w_out", "mlp_w_up", "mlp_w_down"]
PACK_BF16 = [("gla_w_in", 2), ("gla_w_out", 1), ("mla_w_in", 1), ("mla_w_uq", 2), ("mla_w_ukv", 2), ("mla_w_out", 1),
             ("gdn_w_in", 2), ("gdn_w_out", 1), ("mlp_w_up", 2), ("mlp_w_down", 1)]
PACK_F32 = [("norm_pre_g", 2), ("norm_post_g", 2), ("gla_w_gate_up", 2), ("gla_b_gate", 1), ("gla_head_g", 1), ("gdn_conv_w", 2)]
REPLICATED_SMALL = ["mla_q_norm_g", "mla_kv_norm_g", "gdn_a_log", "gdn_dt_bias", "gdn_norm_g"]
MIXERS = ["gla", "mla", "gdn"]


def _silu_rows(a, *, name):
    def body(a_ref, o_ref):
        v = a_ref[...]
        o_ref[...] = v * _sigmoid(v)

    return pl.pallas_call(body, out_shape=jax.ShapeDtypeStruct(a.shape, F32), name=name)(a)


SMALL_ROWS = 16


def _piece_rows(size, mult):
    assert size % FLAT_W == 0
    return -(-(size // FLAT_W) // mult) * mult


def _to_rows(a, lead, mult):
    n = math.prod(a.shape[len(lead):])
    r = a.reshape(lead + (n // FLAT_W, FLAT_W))
    extra = _piece_rows(n, mult) - n // FLAT_W
    return jnp.pad(r, [(0, 0)] * len(lead) + [(0, extra), (0, 0)]) if extra else r


def _small_to_rows(parts, lead):
    flat = jnp.concatenate([p.reshape(lead + (-1,)) for p in parts], axis=-1)
    pad = SMALL_ROWS * FLAT_W - flat.shape[-1]
    return jnp.pad(flat, [(0, 0)] * len(lead) + [(0, pad)]).reshape(lead + (SMALL_ROWS, FLAT_W))


def _small_from_rows(rows, shards, lead):
    flat = rows.reshape(lead + (-1,))
    out, off = {}, 0
    for n, _ in PACK_F32:
        out[n] = flat[..., off:off + shards[n].size].reshape(lead + shards[n].shape)
        off += shards[n].size
    return out


def _pack_weights(shards):
    parts = [_to_rows(shards[n].astype(BF16), (), 16) for n, _ in PACK_BF16]
    small = _small_to_rows([shards[n] for n, _ in PACK_F32], ())
    parts.append(lax.bitcast_convert_type(small, BF16).reshape(2 * SMALL_ROWS, FLAT_W))
    flat = jnp.concatenate(parts, axis=0)
    return jnp.pad(flat, ((0, FLAT_ROWS - flat.shape[0]), (0, 0)))


def _unpack_weights(gathered, shards, chip):
    full, off = {}, 0
    for n, ax in PACK_BF16:
        size = shards[n].size
        seg = gathered[:, off:off + size // FLAT_W].reshape((4,) + shards[n].shape)
        own = shards[n].astype(BF16)
        full[n] = jnp.concatenate([jnp.where(chip == j, own, seg[j]) for j in range(4)], axis=ax)
        off += _piece_rows(size, 16)
    small = lax.bitcast_convert_type(gathered[:, off:off + 2 * SMALL_ROWS].reshape(4, SMALL_ROWS, FLAT_W, 2), F32)
    for (n, ax), seg in zip(PACK_F32, _small_from_rows(small, shards, (4,)).values()):
        full[n] = jnp.concatenate([jnp.where(chip == j, shards[n], seg[j]) for j in range(4)], axis=ax)
    return full


def _pack_grads(grads):
    split = lambda n, ax: jnp.stack(jnp.split(grads[n].astype(F32), 4, axis=ax))
    parts = [_to_rows(split(n, ax), (4,), 8) for n, ax in PACK_BF16]
    parts.append(_small_to_rows([split(n, ax) for n, ax in PACK_F32], (4,)))
    flat = jnp.concatenate(parts, axis=1)
    return jnp.pad(flat, ((0, 0), (0, FLAT_ROWS - flat.shape[1]), (0, 0)))


def _unpack_grads(reduced, shards):
    out, off = {}, 0
    for n, _ in PACK_BF16:
        size = shards[n].size
        out[n] = reduced[off:off + size // FLAT_W].reshape(shards[n].shape)
        off += _piece_rows(size, 8)
    out.update(_small_from_rows(reduced[off:off + SMALL_ROWS], shards, ()))
    return out


def _mixer_weights(kind, j, full, rep):
    if kind == "gla":
        return dict(w_in=jnp.pad(full["gla_w_in"][j], ((0, 0), (0, GLA_IN_PAD - GLA_IN))),
                    wg=jnp.pad(full["gla_w_gate_up"][j], ((0, LANES - GLA_RANK), (0, 0))),
                    bg=full["gla_b_gate"][j][None], hg=full["gla_head_g"][j][None], w_out=full["gla_w_out"][j])
    if kind == "mla":
        return dict(w_in=jnp.pad(full["mla_w_in"][j], ((0, 0), (0, MLA_IN_PAD - MLA_IN))), q_norm_g=rep["mla_q_norm_g"][j][None],
                    w_uq=full["mla_w_uq"][j], kv_norm_g=rep["mla_kv_norm_g"][j][None], w_ukv=full["mla_w_ukv"][j],
                    w_out=full["mla_w_out"][j])
    return dict(w_in=jnp.pad(full["gdn_w_in"][j], ((0, 0), (0, GDN_IN_PAD - GDN_IN))), conv_w=full["gdn_conv_w"][j],
                a_log=rep["gdn_a_log"][j][:, None, None], dt_bias=rep["gdn_dt_bias"][j][:, None, None],
                norm_g=rep["gdn_norm_g"][j][None], w_out=full["gdn_w_out"][j])


def _layer_fwd(xin, mod, gains, kind, mw, w_up, w_down, pos, tag):
    sh_m, sc_m, gt_m, sh_f, sc_f, gt_f = mod
    pre0, pre1, post0, post1 = gains
    h = _premod_fwd(xin, pre0, sc_m, sh_m, name=tag + "_pre0")
    if kind == "gla":
        y, saved = _gla_layer_fwd(h, mw, tag + "_gla")
    elif kind == "mla":
        y, saved = _mla_layer_fwd(h, pos, mw, tag + "_mla")
    else:
        y, saved = _gdn_layer_fwd(h, mw, tag + "_gdn")
    x1 = _postres_fwd(xin, y, post0, gt_m, name=tag + "_post0")
    h2 = _premod_fwd(x1, pre1, sc_f, sh_f, name=tag + "_pre1")
    act = _mm(h2, w_up, out_dtype=BF16, epi="relu2", name=tag + "_up")
    y2 = _mm(act, w_down, name=tag + "_down")
    x2 = _postres_fwd(x1, y2, post1, gt_f, name=tag + "_post1")
    return x2, (xin, h, y, saved, x1, h2, act, y2)


def _layer_bwd(g2, kept, mod, gains, kind, mw, w_up, w_down, tag):
    xin, h, y, saved, x1, h2, act, y2 = kept
    sh_m, sc_m, gt_m, sh_f, sc_f, gt_f = mod
    pre0, pre1, post0, post1 = gains
    dy2, dpost1, dgt_f = _postres_bwd(g2, y2, post1, gt_f, name=tag + "_post1_b")
    du = _mm(dy2, w_down, tb=True, out_dtype=BF16, epi="dact", aux=act, name=tag + "_du")
    dw_down = _mm(act, dy2, ta=True, name=tag + "_dwdown")
    dw_up = _mm(h2, du, ta=True, name=tag + "_dwup")
    dh2 = _mm(du, w_up, tb=True, name=tag + "_dh2")
    g1, dpre1, dsc_f, dsh_f = _premod_bwd(dh2, x1, pre1, sc_f, g2, name=tag + "_pre1_b")
    dy, dpost0, dgt_m = _postres_bwd(g1, y, post0, gt_m, name=tag + "_post0_b")
    if kind == "gla":
        dh, mg = _gla_layer_bwd(dy, h, saved, mw, tag + "_gla")
    elif kind == "mla":
        dh, mg = _mla_layer_bwd(dy, h, saved, mw, tag + "_mla")
    else:
        dh, mg = _gdn_layer_bwd(dy, h, saved, mw, tag + "_gdn")
    g0, dpre0, dsc_m, dsh_m = _premod_bwd(dh, xin, pre0, sc_m, g1, name=tag + "_pre0_b")
    dmod = jnp.concatenate([dsh_m, dsc_m, dgt_m, dsh_f, dsc_f, dgt_f], axis=1)
    return g0, dmod, jnp.concatenate([dpre0, dpre1], axis=0), jnp.concatenate([dpost0, dpost1], axis=0), mg, dw_up, dw_down


def kernel(x, c, positions, ada_w, ada_b, norm_pre_g, norm_post_g, gla_w_in, gla_w_gate_up, gla_b_gate, gla_head_g, gla_w_out, mla_w_in, mla_q_norm_g, mla_w_uq, mla_kv_norm_g, mla_w_ukv, mla_w_out, gdn_w_in, gdn_conv_w, gdn_a_log, gdn_dt_bias, gdn_norm_g, gdn_w_out, mlp_w_up, mlp_w_down, loss_target, m_ada_w, m_ada_b, m_norm_pre_g, m_norm_post_g, m_gla_w_in, m_gla_w_gate_up, m_gla_b_gate, m_gla_head_g, m_gla_w_out, m_mla_w_in, m_mla_q_norm_g, m_mla_w_uq, m_mla_kv_norm_g, m_mla_w_ukv, m_mla_w_out, m_gdn_w_in, m_gdn_conv_w, m_gdn_a_log, m_gdn_dt_bias, m_gdn_norm_g, m_gdn_w_out, m_mlp_w_up, m_mlp_w_down, v_ada_w, v_ada_b, v_norm_pre_g, v_norm_post_g, v_gla_w_in, v_gla_w_gate_up, v_gla_b_gate, v_gla_head_g, v_gla_w_out, v_mla_w_in, v_mla_q_norm_g, v_mla_w_uq, v_mla_kv_norm_g, v_mla_w_ukv, v_mla_w_out, v_gdn_w_in, v_gdn_conv_w, v_gdn_a_log, v_gdn_dt_bias, v_gdn_norm_g, v_gdn_w_out, v_mlp_w_up, v_mlp_w_down):
    w = dict(ada_w=ada_w, ada_b=ada_b, norm_pre_g=norm_pre_g, norm_post_g=norm_post_g, gla_w_in=gla_w_in,
             gla_w_gate_up=gla_w_gate_up, gla_b_gate=gla_b_gate, gla_head_g=gla_head_g, gla_w_out=gla_w_out, mla_w_in=mla_w_in,
             mla_q_norm_g=mla_q_norm_g, mla_w_uq=mla_w_uq, mla_kv_norm_g=mla_kv_norm_g, mla_w_ukv=mla_w_ukv, mla_w_out=mla_w_out,
             gdn_w_in=gdn_w_in, gdn_conv_w=gdn_conv_w, gdn_a_log=gdn_a_log, gdn_dt_bias=gdn_dt_bias, gdn_norm_g=gdn_norm_g,
             gdn_w_out=gdn_w_out, mlp_w_up=mlp_w_up, mlp_w_down=mlp_w_down)
    m = dict(zip(WEIGHTS, [m_ada_w, m_ada_b, m_norm_pre_g, m_norm_post_g, m_gla_w_in, m_gla_w_gate_up, m_gla_b_gate, m_gla_head_g,
                           m_gla_w_out, m_mla_w_in, m_mla_q_norm_g, m_mla_w_uq, m_mla_kv_norm_g, m_mla_w_ukv, m_mla_w_out,
                           m_gdn_w_in, m_gdn_conv_w, m_gdn_a_log, m_gdn_dt_bias, m_gdn_norm_g, m_gdn_w_out, m_mlp_w_up, m_mlp_w_down]))
    v = dict(zip(WEIGHTS, [v_ada_w, v_ada_b, v_norm_pre_g, v_norm_post_g, v_gla_w_in, v_gla_w_gate_up, v_gla_b_gate, v_gla_head_g,
                           v_gla_w_out, v_mla_w_in, v_mla_q_norm_g, v_mla_w_uq, v_mla_kv_norm_g, v_mla_w_ukv, v_mla_w_out,
                           v_gdn_w_in, v_gdn_conv_w, v_gdn_a_log, v_gdn_dt_bias, v_gdn_norm_g, v_gdn_w_out, v_mlp_w_up, v_mlp_w_down]))
    t = x.shape[1]
    ix, iy, ic = lax.axis_index("x"), lax.axis_index("y"), lax.axis_index("c")
    me = 4 * ix + 2 * iy + ic
    chip = 2 * ix + iy
    ada_cols = ada_w.shape[2]

    full = _unpack_weights(_chip_gather(_pack_weights(w), name="gather_weights"), w, chip)

    cond8 = _silu_rows(jnp.pad(c, ((0, 7), (0, 0))), name="cond_silu")
    cond16 = jnp.pad(_all8_gather(cond8, name="gather_cond")[:, 0, :], ((0, 8), (0, 0)))
    mod_cols = []
    for layer in range(DEPTH):
        bias = jnp.broadcast_to(lax.dynamic_slice_in_dim(ada_b[layer], chip * ada_cols, ada_cols)[None], (16, ada_cols))
        mod_cols.append(_mm(cond16, ada_w[layer], epi="add", aux=bias, name=f"ada{layer}")[:8])
    mod_all = _all8_gather(jnp.stack(mod_cols).reshape(DEPTH * 8, ada_cols), name="gather_mod")
    mod = jnp.concatenate([lax.dynamic_slice_in_dim(mod_all[2 * j].reshape(DEPTH, 8, ada_cols), me, 1, axis=1)[:, 0]
                           for j in range(4)], axis=1)

    def layer_args(layer):
        kind, j = MIXERS[layer % 3], layer // 3
        mods = [mod[layer, i * D_MODEL:(i + 1) * D_MODEL][None] for i in range(N_MOD)]
        gains = (full["norm_pre_g"][layer, 0:1], full["norm_pre_g"][layer, 1:2], full["norm_post_g"][layer, 0:1],
                 full["norm_post_g"][layer, 1:2])
        return kind, j, mods, gains, _mixer_weights(kind, j, full, w)

    xs = x[0]
    kept = []
    for layer in range(DEPTH):
        kind, j, mods, gains, mw = layer_args(layer)
        xs, keep = _layer_fwd(xs, mods, gains, kind, mw, full["mlp_w_up"][layer], full["mlp_w_down"][layer], positions[0], f"l{layer}")
        kept.append(keep)
    loss_row, g = _loss_head(xs, loss_target[0], name="loss_head")
    loss = lax.psum(loss_row[0, 0], ("x", "y", "c"))

    grads = {n: [None] * w[n].shape[0] for n, _ in PACK_BF16 + PACK_F32}
    rep_grads = {}
    dmods = [None] * DEPTH
    for layer in reversed(range(DEPTH)):
        kind, j, mods, gains, mw = layer_args(layer)
        g, dmods[layer], dpre, dpost, mg, dw_up, dw_down = _layer_bwd(
            g, kept[layer], mods, gains, kind, mw, full["mlp_w_up"][layer], full["mlp_w_down"][layer], f"l{layer}")
        grads["norm_pre_g"][layer], grads["norm_post_g"][layer] = dpre, dpost
        grads["mlp_w_up"][layer], grads["mlp_w_down"][layer] = dw_up, dw_down
        for key, val in mg.items():
            name = kind + "_" + key
            if name in grads:
                grads[name][j] = val
            else:
                rep_grads[name] = val[None]
    grads = {n: jnp.stack(parts) for n, parts in grads.items()}

    rep_flat = jnp.concatenate([rep_grads[n].reshape(-1) for n in REPLICATED_SMALL])
    dbuf = jnp.concatenate([jnp.concatenate(dmods, axis=0), jnp.pad(rep_flat, (0, N_MOD * D_MODEL - rep_flat.shape[0]))[None],
                            jnp.zeros((3, N_MOD * D_MODEL), F32)], axis=0)
    dall = _all8_gather(dbuf, name="gather_dmod")
    dsum = _sum_slots(dall, name="sum_dmod")
    out_grads = {"ada_b": dsum[:DEPTH]}
    off = 0
    for n in REPLICATED_SMALL:
        out_grads[n] = dsum[DEPTH, off:off + w[n].size].reshape(w[n].shape)
        off += w[n].size
    dada = []
    for layer in range(DEPTH):
        dm16 = jnp.pad(lax.dynamic_slice_in_dim(dall[:, layer, :], chip * ada_cols, ada_cols, axis=1), ((0, 8), (0, 0)))
        dada.append(_mm(cond16, dm16, ta=True, name=f"dada{layer}"))
    out_grads["ada_w"] = jnp.stack(dada)

    reduced = _reduce_scatter(_pack_grads(grads), ic.reshape(1).astype(jnp.int32), chip.reshape(1).astype(jnp.int32),
                              tag="reduce_grads")
    out_grads.update(_unpack_grads(reduced, w))

    deltas, new_m, new_v = {}, {}, {}
    for n in WEIGHTS:
        deltas[n], new_m[n], new_v[n] = _adamw(w[n], out_grads[n], m[n], v[n], name="adamw_" + n)
    return (loss, g[None], *[out_grads[n] for n in WEIGHTS], *[deltas[n] for n in WEIGHTS],
            *[new_m[n] for n in WEIGHTS], *[new_v[n] for n in WEIGHTS])
```

```python
import functools
import math

import jax
import jax.numpy as jnp
from jax import lax
from jax.experimental import pallas as pl
from jax.experimental.pallas import tpu as pltpu

F32 = jnp.float32
BF16 = jnp.bfloat16

D_MODEL = 1024
DEPTH = 4
CHUNK = 64
EPS = 1e-6
NEG_INF = -1e30
N_MOD = 6

GLA_HEADS, GLA_DK, GLA_DV, GLA_RANK = 4, 128, 256, 16
GLA_KW, GLA_VW = GLA_HEADS * GLA_DK, GLA_HEADS * GLA_DV
GLA_IN = 2 * GLA_KW + 2 * GLA_VW + GLA_RANK
GLA_IN_PAD = 3200

MLA_HEADS, MLA_NOPE, MLA_ROPE, MLA_V = 16, 64, 32, 64
MLA_Q_RANK, MLA_KV_RANK = 384, 256
MLA_IN = MLA_Q_RANK + MLA_KV_RANK + MLA_ROPE
MLA_IN_PAD = 768
ROPE_THETA = 10000.0
MLA_QK = MLA_NOPE + MLA_ROPE
LANES = 128

GDN_K_HEADS, GDN_V_HEADS, GDN_DK, GDN_DV, GDN_CONV = 8, 16, 128, 128, 4
GDN_QKW, GDN_VW = GDN_K_HEADS * GDN_DK, GDN_V_HEADS * GDN_DV
GDN_CONV_W = 2 * GDN_QKW + GDN_VW
GDN_IN = GDN_CONV_W + GDN_VW + 2 * GDN_V_HEADS
GDN_IN_PAD = 6400

ADAM_LR, ADAM_B1, ADAM_B2, ADAM_EPS, ADAM_WD, ADAM_STEP = 0.001, 0.9, 0.999, 1e-08, 0.01, 10

VMEM_LIMIT = 56 * 1024 * 1024

NN = ((1,), (0,))
NT = ((1,), (1,))
TN = ((0,), (0,))


def _cp(*sem):
    return pltpu.CompilerParams(dimension_semantics=sem, vmem_limit_bytes=VMEM_LIMIT)


def _pick(n, cands):
    for c in cands:
        if n % c == 0:
            return c
    return n


def _dg(a, b, dims=NN):
    return lax.dot_general(a.astype(BF16), b.astype(BF16), (dims, ((), ())), preferred_element_type=F32)


def _hi(a, b, dims=NN):
    return lax.dot_general(a, b, (dims, ((), ())), precision=lax.Precision.HIGHEST, preferred_element_type=F32)


def _dot3(a, b, dims=NN):
    ah = a.astype(BF16)
    al = (a - ah.astype(F32)).astype(BF16)
    bh = b.astype(BF16)
    bl = (b - bh.astype(F32)).astype(BF16)
    d = lambda u, v: lax.dot_general(u, v, (dims, ((), ())), preferred_element_type=F32)
    return d(ah, bh) + (d(ah, bl) + d(al, bh))


def _sigmoid(x):
    return 1.0 / (1.0 + jnp.exp(-x))


def _softplus(x):
    return jnp.maximum(x, 0.0) + jnp.log(1.0 + jnp.exp(-jnp.abs(x)))


def _iota2(shape, dim):
    return lax.broadcasted_iota(jnp.int32, shape, dim)


def _mm(a, b, *, ta=False, tb=False, out_dtype=F32, epi=None, aux=None, name):
    m = a.shape[1] if ta else a.shape[0]
    k = a.shape[0] if ta else a.shape[1]
    n = b.shape[0] if tb else b.shape[1]
    assert k == (b.shape[1] if tb else b.shape[0]), (a.shape, b.shape, ta, tb)
    tm = _pick(m, (1024, 512, 384, 256, 128))
    tn = _pick(n, (1024, 640, 512, 768, 384, 256, 128))
    tk = _pick(k, (1024, 640, 512, 768, 384, 256, 128))
    nk = k // tk
    dims = ((0 if ta else 1,), (1 if tb else 0,))

    def finish(r, x_ref, o_ref):
        if epi == "relu2":
            r = jnp.square(jnp.maximum(r, 0.0))
        elif epi == "dact":
            r = r * (2.0 * jnp.sqrt(x_ref[...].astype(F32)))
        elif epi == "add":
            r = r + x_ref[...]
        o_ref[...] = r.astype(out_dtype)

    def body(*refs):
        a_ref, b_ref = refs[:2]
        x_ref = refs[2] if aux is not None else None
        o_ref = refs[3] if aux is not None else refs[2]
        if nk == 1:
            finish(_dg(a_ref[...], b_ref[...], dims), x_ref, o_ref)
            return
        acc = refs[-1]
        kk = pl.program_id(2)

        @pl.when(kk == 0)
        def _():
            acc[...] = jnp.zeros_like(acc)

        acc[...] += _dg(a_ref[...], b_ref[...], dims)

        @pl.when(kk == nk - 1)
        def _():
            finish(acc[...], x_ref, o_ref)

    a_spec = pl.BlockSpec((tk, tm), lambda i, j, q: (q, i)) if ta else pl.BlockSpec((tm, tk), lambda i, j, q: (i, q))
    b_spec = pl.BlockSpec((tn, tk), lambda i, j, q: (j, q)) if tb else pl.BlockSpec((tk, tn), lambda i, j, q: (q, j))
    o_spec = pl.BlockSpec((tm, tn), lambda i, j, q: (i, j))
    in_specs = [a_spec, b_spec] + ([o_spec] if aux is not None else [])
    args = (a, b) + ((aux,) if aux is not None else ())
    return pl.pallas_call(
        body, grid=(m // tm, n // tn, nk), in_specs=in_specs, out_specs=o_spec,
        out_shape=jax.ShapeDtypeStruct((m, n), out_dtype),
        scratch_shapes=[pltpu.VMEM((tm, tn), F32)] if nk > 1 else [],
        compiler_params=_cp("parallel", "parallel", "arbitrary"), name=name)(*args)


def _row_tile(t):
    return _pick(t, (512, 256, 128, 64, 8))


def _premod_fwd(x, g, sc, sh, *, name):
    t, c = x.shape
    tr = _row_tile(t)

    def body(x_ref, g_ref, sc_ref, sh_ref, h_ref):
        xv = x_ref[...]
        r = lax.rsqrt(jnp.mean(xv * xv, axis=-1, keepdims=True) + EPS)
        h_ref[...] = (((xv * r) * g_ref[...]) * (1.0 + sc_ref[...]) + sh_ref[...]).astype(BF16)

    row = pl.BlockSpec((tr, c), lambda i: (i, 0))
    vec = pl.BlockSpec((1, c), lambda i: (0, 0))
    return pl.pallas_call(body, grid=(t // tr,), in_specs=[row, vec, vec, vec], out_specs=row,
                          out_shape=jax.ShapeDtypeStruct((t, c), BF16), compiler_params=_cp("parallel"), name=name)(x, g, sc, sh)


def _premod_bwd(dh, x, g, sc, gin, *, name):
    t, c = x.shape
    tr = _row_tile(t)

    def body(dh_ref, x_ref, g_ref, sc_ref, gin_ref, gout_ref, dg_ref, dsc_ref, dsh_ref):
        @pl.when(pl.program_id(0) == 0)
        def _():
            dg_ref[...] = jnp.zeros_like(dg_ref)
            dsc_ref[...] = jnp.zeros_like(dsc_ref)
            dsh_ref[...] = jnp.zeros_like(dsh_ref)

        xv = x_ref[...]
        dhv = dh_ref[...].astype(F32)
        gv = g_ref[...]
        one_sc = 1.0 + sc_ref[...]
        r = lax.rsqrt(jnp.mean(xv * xv, axis=-1, keepdims=True) + EPS)
        nv = xv * r
        dsh_ref[...] += jnp.sum(dhv, axis=0, keepdims=True)
        dsc_ref[...] += jnp.sum(dhv * (nv * gv), axis=0, keepdims=True)
        dg_ref[...] += jnp.sum(dhv * nv * one_sc, axis=0, keepdims=True)
        dn = dhv * gv * one_sc
        dx = r * (dn - nv * jnp.mean(dn * nv, axis=-1, keepdims=True))
        gout_ref[...] = gin_ref[...] + dx

    row = pl.BlockSpec((tr, c), lambda i: (i, 0))
    vec = pl.BlockSpec((1, c), lambda i: (0, 0))
    vs = jax.ShapeDtypeStruct((1, c), F32)
    return pl.pallas_call(body, grid=(t // tr,), in_specs=[row, row, vec, vec, row], out_specs=[row, vec, vec, vec],
                          out_shape=[jax.ShapeDtypeStruct((t, c), F32), vs, vs, vs],
                          compiler_params=_cp("arbitrary"), name=name)(dh, x, g, sc, gin)


def _postres_fwd(x, y, g, gt, *, name):
    t, c = x.shape
    tr = _row_tile(t)

    def body(x_ref, y_ref, g_ref, gt_ref, o_ref):
        yv = y_ref[...]
        r = lax.rsqrt(jnp.mean(yv * yv, axis=-1, keepdims=True) + EPS)
        o_ref[...] = x_ref[...] + gt_ref[...] * ((yv * r) * g_ref[...])

    row = pl.BlockSpec((tr, c), lambda i: (i, 0))
    vec = pl.BlockSpec((1, c), lambda i: (0, 0))
    return pl.pallas_call(body, grid=(t // tr,), in_specs=[row, row, vec, vec], out_specs=row,
                          out_shape=jax.ShapeDtypeStruct((t, c), F32), compiler_params=_cp("parallel"), name=name)(x, y, g, gt)


def _postres_bwd(gout, y, g, gt, *, name):
    t, c = y.shape
    tr = _row_tile(t)

    def body(go_ref, y_ref, g_ref, gt_ref, dy_ref, dg_ref, dgt_ref):
        @pl.when(pl.program_id(0) == 0)
        def _():
            dg_ref[...] = jnp.zeros_like(dg_ref)
            dgt_ref[...] = jnp.zeros_like(dgt_ref)

        yv = y_ref[...]
        gov = go_ref[...]
        gv = g_ref[...]
        gtv = gt_ref[...]
        r = lax.rsqrt(jnp.mean(yv * yv, axis=-1, keepdims=True) + EPS)
        z = yv * r
        dgt_ref[...] += jnp.sum(gov * (z * gv), axis=0, keepdims=True)
        dg_ref[...] += jnp.sum(gov * gtv * z, axis=0, keepdims=True)
        dz = gov * gtv * gv
        dy_ref[...] = (r * (dz - z * jnp.mean(dz * z, axis=-1, keepdims=True))).astype(BF16)

    row = pl.BlockSpec((tr, c), lambda i: (i, 0))
    vec = pl.BlockSpec((1, c), lambda i: (0, 0))
    vs = jax.ShapeDtypeStruct((1, c), F32)
    return pl.pallas_call(body, grid=(t // tr,), in_specs=[row, row, vec, vec], out_specs=[row, vec, vec],
                          out_shape=[jax.ShapeDtypeStruct((t, c), BF16), vs, vs],
                          compiler_params=_cp("arbitrary"), name=name)(gout, y, g, gt)


def _loss_head(y, tgt, *, name):
    t, c = y.shape
    tr = _row_tile(t)

    def body(y_ref, t_ref, l_ref, dy_ref):
        @pl.when(pl.program_id(0) == 0)
        def _():
            l_ref[...] = jnp.zeros_like(l_ref)

        d = y_ref[...] - t_ref[...]
        dy_ref[...] = d * (1.0 / c)
        l_ref[...] += 0.5 * jnp.sum(jnp.mean(d * d, axis=-1, keepdims=True))

    row = pl.BlockSpec((tr, c), lambda i: (i, 0))
    return pl.pallas_call(body, grid=(t // tr,), in_specs=[row, row],
                          out_specs=[pl.BlockSpec((1, LANES), lambda i: (0, 0)), row],
                          out_shape=[jax.ShapeDtypeStruct((1, LANES), F32), jax.ShapeDtypeStruct((t, c), F32)],
                          compiler_params=_cp("arbitrary"), name=name)(y, tgt)


def _adamw(w, g, m, v, *, name):
    shape = w.shape
    c = shape[-1]
    r = math.prod(shape[:-1])
    w2, g2, m2, v2 = (a.reshape(r, c) for a in (w, g, m, v))
    tr = r
    for cand in (1024, 512, 256, 128, 64, 32, 16, 8):
        if r % cand == 0 and cand * c * 4 <= (1 << 20):
            tr = cand
            break
    c1 = 1.0 - ADAM_B1 ** ADAM_STEP
    c2 = 1.0 - ADAM_B2 ** ADAM_STEP

    def body(w_ref, g_ref, m_ref, v_ref, d_ref, nm_ref, nv_ref):
        gv = g_ref[...]
        mn = ADAM_B1 * m_ref[...] + (1.0 - ADAM_B1) * gv
        vn = ADAM_B2 * v_ref[...] + (1.0 - ADAM_B2) * jnp.square(gv)
        m_hat = mn / c1
        v_hat = vn / c2
        d_ref[...] = -ADAM_LR * (m_hat / (jnp.sqrt(v_hat) + ADAM_EPS) + ADAM_WD * w_ref[...])
        nm_ref[...] = mn
        nv_ref[...] = vn

    blk = pl.BlockSpec((tr, c), lambda i: (i, 0))
    s = jax.ShapeDtypeStruct((r, c), F32)
    d, nm, nv = pl.pallas_call(body, grid=(r // tr,), in_specs=[blk] * 4, out_specs=[blk] * 3, out_shape=[s, s, s],
                               compiler_params=_cp("parallel"), name=name)(w2, g2, m2, v2)
    return d.reshape(shape), nm.reshape(shape), nv.reshape(shape)


def _gla_parts(p_ref, wg_ref, bg_ref):
    q = p_ref[:, 0:GLA_KW] * (GLA_DK ** -0.5)
    k = p_ref[:, GLA_KW:2 * GLA_KW]
    glr = p_ref[:, 2 * GLA_KW + 2 * GLA_VW:GLA_IN_PAD]
    gate = _dg(glr, wg_ref[...]) + bg_ref[...]
    log_a = (jnp.minimum(gate, 0.0) - jnp.log(1.0 + jnp.exp(-jnp.abs(gate)))) * (1.0 / 16.0)
    tril = (_iota2((CHUNK, CHUNK), 0) >= _iota2((CHUNK, CHUNK), 1)).astype(F32)
    cum = _hi(tril, log_a)
    c_last = cum[CHUNK - 1:CHUNK, :]
    f = jnp.exp(c_last - cum)
    dec = jnp.exp(c_last)
    return q, k, glr, gate, f, k * f, dec


def _gla_fwd(p, wg, bg, hg, *, name):
    t = p.shape[0]
    nc = t // CHUNK

    def body(p_ref, wg_ref, bg_ref, hg_ref, og_ref, s_ref, st):
        @pl.when(pl.program_id(0) == 0)
        def _():
            st[...] = jnp.zeros_like(st)

        q, _, _, _, _, ke, dec = _gla_parts(p_ref, wg_ref, bg_ref)
        hs = range(GLA_HEADS)
        ks = [slice(h * GLA_DK, (h + 1) * GLA_DK) for h in hs]
        vs = [slice(2 * GLA_KW + h * GLA_DV, 2 * GLA_KW + (h + 1) * GLA_DV) for h in hs]
        rs = [slice(2 * GLA_KW + GLA_VW + h * GLA_DV, 2 * GLA_KW + GLA_VW + (h + 1) * GLA_DV) for h in hs]
        s_new = [st[h] * dec[:, ks[h]] + _dg(p_ref[:, vs[h]], ke[:, ks[h]], TN) for h in hs]
        o = [_dg(q[:, ks[h]], s_new[h], NT) for h in hs]
        for h in hs:
            st[h] = s_new[h]
            s_ref[0, h] = s_new[h]
            rn = lax.rsqrt(jnp.mean(o[h] * o[h], axis=-1, keepdims=True) + EPS)
            rv = p_ref[:, rs[h]]
            og_ref[:, h * GLA_DV:(h + 1) * GLA_DV] = (((o[h] * rn) * hg_ref[...]) * (rv * _sigmoid(rv))).astype(BF16)

    full = lambda a: pl.BlockSpec(a.shape, lambda n: (0,) * a.ndim)
    return pl.pallas_call(
        body, grid=(nc,),
        in_specs=[pl.BlockSpec((CHUNK, GLA_IN_PAD), lambda n: (n, 0)), full(wg), full(bg), full(hg)],
        out_specs=[pl.BlockSpec((CHUNK, GLA_VW), lambda n: (n, 0)),
                   pl.BlockSpec((1, GLA_HEADS, GLA_DV, GLA_DK), lambda n: (n, 0, 0, 0))],
        out_shape=[jax.ShapeDtypeStruct((t, GLA_VW), BF16), jax.ShapeDtypeStruct((nc, GLA_HEADS, GLA_DV, GLA_DK), F32)],
        scratch_shapes=[pltpu.VMEM((GLA_HEADS, GLA_DV, GLA_DK), F32)],
        compiler_params=_cp("arbitrary"), name=name)(p, wg, bg, hg)


def _gla_bwd(p, dog, sall, wg, bg, hg, *, name):
    t = p.shape[0]
    nc = t // CHUNK

    def body(p_ref, dog_ref, s1_ref, s0_ref, wg_ref, bg_ref, hg_ref, dp_ref, dwg_ref, dbg_ref, dhg_ref, gt):
        i = pl.program_id(0)

        @pl.when(i == 0)
        def _():
            gt[...] = jnp.zeros_like(gt)
            dwg_ref[...] = jnp.zeros_like(dwg_ref)
            dbg_ref[...] = jnp.zeros_like(dbg_ref)
            dhg_ref[...] = jnp.zeros_like(dhg_ref)

        has_prev = (i < nc - 1).astype(F32)
        q, k, glr, gate, f, ke, dec = _gla_parts(p_ref, wg_ref, bg_ref)
        hgv = hg_ref[...]
        hs = range(GLA_HEADS)
        ks = [slice(h * GLA_DK, (h + 1) * GLA_DK) for h in hs]
        vs = [slice(2 * GLA_KW + h * GLA_DV, 2 * GLA_KW + (h + 1) * GLA_DV) for h in hs]
        rs = [slice(2 * GLA_KW + GLA_VW + h * GLA_DV, 2 * GLA_KW + GLA_VW + (h + 1) * GLA_DV) for h in hs]
        s1 = [s1_ref[0, h] for h in hs]
        o = [_dg(q[:, ks[h]], s1[h], NT) for h in hs]
        dhg = jnp.zeros((1, GLA_DV), F32)
        do = []
        for h in hs:
            rv = p_ref[:, rs[h]]
            rn = lax.rsqrt(jnp.mean(o[h] * o[h], axis=-1, keepdims=True) + EPS)
            z = o[h] * rn
            sg = _sigmoid(rv)
            sl = rv * sg
            dogh = dog_ref[:, h * GLA_DV:(h + 1) * GLA_DV].astype(F32)
            dhg = dhg + jnp.sum(dogh * z * sl, axis=0, keepdims=True)
            dp_ref[:, rs[h]] = (dogh * (z * hgv) * (sg * (1.0 + rv * (1.0 - sg)))).astype(BF16)
            dz = dogh * sl * hgv
            do.append(rn * (dz - z * jnp.mean(dz * z, axis=-1, keepdims=True)))
        dhg_ref[...] += dhg
        g_tot = [gt[h] + _dg(do[h], q[:, ks[h]], TN) for h in hs]
        dq = [_dg(do[h], s1[h], NN) for h in hs]
        dke_parts = [_dg(p_ref[:, vs[h]], g_tot[h], NN) for h in hs]
        dv = [_dg(ke[:, ks[h]], g_tot[h], NT) for h in hs]
        ddec_parts = []
        for h in hs:
            dp_ref[:, ks[h]] = (dq[h] * (GLA_DK ** -0.5)).astype(BF16)
            dp_ref[:, vs[h]] = dv[h].astype(BF16)
            ddec_parts.append(jnp.sum(g_tot[h] * (s0_ref[0, h] * has_prev), axis=0, keepdims=True))
            gt[h] = g_tot[h] * dec[:, ks[h]]
        dke = jnp.concatenate(dke_parts, axis=1)
        ddec = jnp.concatenate(ddec_parts, axis=1)
        dp_ref[:, GLA_KW:2 * GLA_KW] = (dke * f).astype(BF16)
        stril = (_iota2((CHUNK, CHUNK), 0) > _iota2((CHUNK, CHUNK), 1)).astype(F32)
        dlog_a = _hi(stril, dke * ke) + ddec * dec
        dgate = dlog_a * (1.0 / 16.0) * _sigmoid(-gate)
        dp_ref[:, 2 * GLA_KW + 2 * GLA_VW:GLA_IN_PAD] = _dg(dgate, wg_ref[...], NT).astype(BF16)
        dwg_ref[...] += _dg(glr, dgate, TN)
        dbg_ref[...] += jnp.sum(dgate, axis=0, keepdims=True)

    full = lambda a: pl.BlockSpec(a.shape, lambda n: (0,) * a.ndim)
    sblk = (1, GLA_HEADS, GLA_DV, GLA_DK)
    return pl.pallas_call(
        body, grid=(nc,),
        in_specs=[pl.BlockSpec((CHUNK, GLA_IN_PAD), lambda n: (nc - 1 - n, 0)),
                  pl.BlockSpec((CHUNK, GLA_VW), lambda n: (nc - 1 - n, 0)),
                  pl.BlockSpec(sblk, lambda n: (nc - 1 - n, 0, 0, 0)),
                  pl.BlockSpec(sblk, lambda n: (jnp.maximum(nc - 2 - n, 0), 0, 0, 0)),
                  full(wg), full(bg), full(hg)],
        out_specs=[pl.BlockSpec((CHUNK, GLA_IN_PAD), lambda n: (nc - 1 - n, 0)), full(wg), full(bg), full(hg)],
        out_shape=[jax.ShapeDtypeStruct((t, GLA_IN_PAD), BF16), jax.ShapeDtypeStruct(wg.shape, F32),
                   jax.ShapeDtypeStruct(bg.shape, F32), jax.ShapeDtypeStruct(hg.shape, F32)],
        scratch_shapes=[pltpu.VMEM((GLA_HEADS, GLA_DV, GLA_DK), F32)],
        compiler_params=_cp("arbitrary"), name=name)(p, dog, sall, sall, wg, bg, hg)


def _gla_layer_fwd(h, w, tag):
    p = _mm(h, w["w_in"], name=tag + "_in")
    og, sall = _gla_fwd(p, w["wg"], w["bg"], w["hg"], name=tag + "_scan")
    y = _mm(og, w["w_out"], name=tag + "_out")
    return y, (p, og, sall)


def _gla_layer_bwd(dy, h, saved, w, tag):
    p, og, sall = saved
    dog = _mm(dy, w["w_out"], tb=True, name=tag + "_dog")
    dw_out = _mm(og, dy, ta=True, name=tag + "_dwout")
    dp, dwg, dbg, dhg = _gla_bwd(p, dog, sall, w["wg"], w["bg"], w["hg"], name=tag + "_scanb")
    dw_in = _mm(h, dp, ta=True, name=tag + "_dwin")
    dh = _mm(dp, w["w_in"], tb=True, name=tag + "_dh")
    grads = dict(w_in=dw_in[:, :GLA_IN], w_gate_up=dwg[:GLA_RANK], b_gate=dbg[0], head_g=dhg[0], w_out=dw_out)
    return dh, grads


def _rope_tables(pos, inv_freq, *, name):
    t = pos.shape[0]
    tr = _row_tile(t)
    half = MLA_ROPE // 2

    def body(p_ref, f_ref, c_ref, s1_ref, s2_ref, s1b_ref, s2b_ref):
        ang = p_ref[...].astype(F32) * f_ref[...]
        lane = _iota2((tr, LANES), 1)
        lo = (lane >= MLA_NOPE) & (lane < MLA_NOPE + half)
        hi = (lane >= MLA_NOPE + half) & (lane < MLA_QK)
        cs, sn = jnp.cos(ang), jnp.sin(ang)
        zero = jnp.zeros_like(cs)
        c_ref[...] = jnp.where(lane < MLA_NOPE, 1.0, jnp.where(lane < MLA_QK, cs, 0.0))
        s1_ref[...] = jnp.where(lo, -sn, zero)
        s2_ref[...] = jnp.where(hi, sn, zero)
        s1b_ref[...] = jnp.where(lo, sn, zero)
        s2b_ref[...] = jnp.where(hi, -sn, zero)

    row = pl.BlockSpec((tr, LANES), lambda i: (i, 0))
    s = jax.ShapeDtypeStruct((t, LANES), F32)
    return pl.pallas_call(body, grid=(t // tr,),
                          in_specs=[pl.BlockSpec((tr, 1), lambda i: (i, 0)), pl.BlockSpec((1, LANES), lambda i: (0, 0))],
                          out_specs=[row] * 5, out_shape=[s] * 5, compiler_params=_cp("parallel"), name=name)(pos, inv_freq)


def _rope(x, c, s1, s2, *, out_dtype, sum_heads=False, name):
    nh, t, _ = x.shape
    tr = _row_tile(t)
    half = MLA_ROPE // 2

    def body(x_ref, c_ref, s1_ref, s2_ref, o_ref):
        xv = x_ref[0].astype(F32)
        y = xv * c_ref[...] + pltpu.roll(xv, LANES - half, 1) * s1_ref[...] + pltpu.roll(xv, half, 1) * s2_ref[...]
        if sum_heads:
            @pl.when(pl.program_id(1) == 0)
            def _():
                o_ref[...] = jnp.zeros_like(o_ref)
            o_ref[...] += y
        else:
            o_ref[0] = y.astype(out_dtype)

    tab = pl.BlockSpec((tr, LANES), lambda i, h: (i, 0))
    xs = pl.BlockSpec((1, tr, LANES), lambda i, h: (h, i, 0))
    if sum_heads:
        return pl.pallas_call(body, grid=(t // tr, nh), in_specs=[xs, tab, tab, tab], out_specs=tab,
                              out_shape=jax.ShapeDtypeStruct((t, LANES), F32),
                              compiler_params=_cp("parallel", "arbitrary"), name=name)(x, c, s1, s2)
    return pl.pallas_call(body, grid=(t // tr, nh), in_specs=[xs, tab, tab, tab], out_specs=xs,
                          out_shape=jax.ShapeDtypeStruct(x.shape, out_dtype),
                          compiler_params=_cp("parallel", "parallel"), name=name)(x, c, s1, s2)


FLASH_BLK = 512


def _diag_mask(blk):
    return (_iota2((blk, blk), 1) // CHUNK) <= (_iota2((blk, blk), 0) // CHUNK)


def _flash_fwd(q, k, v, *, name):
    nh, t, _ = q.shape
    blk = min(FLASH_BLK, t)
    scale = MLA_QK ** -0.5

    def body(q_ref, k_ref, v_ref, o_ref, lse_ref):
        i = pl.program_id(1)
        qv = q_ref[0]

        def step(j, carry, masked):
            m, l, acc = carry
            off = pl.multiple_of(j * blk, blk)
            kb = k_ref[0, pl.ds(off, blk), :]
            vb = v_ref[0, pl.ds(off, blk), :]
            s = _dg(qv, kb, NT) * scale
            if masked:
                s = jnp.where(_diag_mask(blk), s, NEG_INF)
            m_new = jnp.maximum(m, jnp.max(s, axis=-1, keepdims=True))
            p = jnp.exp(s - m_new)
            alpha = jnp.exp(m - m_new)
            return m_new, alpha * l + jnp.sum(p, axis=-1, keepdims=True), alpha * acc + _dg(p, vb, NN)

        init = (jnp.full((blk, 1), NEG_INF, F32), jnp.zeros((blk, 1), F32), jnp.zeros((blk, MLA_V), F32))
        carry = lax.fori_loop(0, i, lambda j, c: step(j, c, False), init)
        m, l, acc = step(i, carry, True)
        o_ref[0] = (acc / l).astype(BF16)
        lse_ref[0] = m + jnp.log(l)

    qs = pl.BlockSpec((1, blk, LANES), lambda h, i: (h, i, 0))
    return pl.pallas_call(
        body, grid=(nh, t // blk),
        in_specs=[qs, pl.BlockSpec((1, t, LANES), lambda h, i: (h, 0, 0)), pl.BlockSpec((1, t, MLA_V), lambda h, i: (h, 0, 0))],
        out_specs=[pl.BlockSpec((1, blk, MLA_V), lambda h, i: (h, i, 0)), pl.BlockSpec((1, blk, 1), lambda h, i: (h, i, 0))],
        out_shape=[jax.ShapeDtypeStruct((nh, t, MLA_V), BF16), jax.ShapeDtypeStruct((nh, t, 1), F32)],
        compiler_params=_cp("parallel", "parallel"), name=name)(q, k, v)


def _flash_dq(q, k, v, do, o, lse, *, name):
    nh, t, _ = q.shape
    blk = min(FLASH_BLK, t)
    scale = MLA_QK ** -0.5

    def body(q_ref, k_ref, v_ref, do_ref, o_ref, lse_ref, dq_ref, dl_ref):
        i = pl.program_id(1)
        qv = q_ref[0]
        dov = do_ref[0]
        lse_v = lse_ref[0]
        delta = jnp.sum(dov.astype(F32) * o_ref[0].astype(F32), axis=-1, keepdims=True)
        dl_ref[0] = delta

        def step(j, dq, masked):
            off = pl.multiple_of(j * blk, blk)
            kb = k_ref[0, pl.ds(off, blk), :]
            vb = v_ref[0, pl.ds(off, blk), :]
            s = _dg(qv, kb, NT) * scale
            if masked:
                s = jnp.where(_diag_mask(blk), s, NEG_INF)
            p = jnp.exp(s - lse_v)
            ds = p * (_dg(dov, vb, NT) - delta) * scale
            return dq + _dg(ds, kb, NN)

        dq = lax.fori_loop(0, i, lambda j, c: step(j, c, False), jnp.zeros((blk, LANES), F32))
        dq_ref[0] = step(i, dq, True)

    qs = pl.BlockSpec((1, blk, LANES), lambda h, i: (h, i, 0))
    vs = pl.BlockSpec((1, blk, MLA_V), lambda h, i: (h, i, 0))
    ls = pl.BlockSpec((1, blk, 1), lambda h, i: (h, i, 0))
    return pl.pallas_call(
        body, grid=(nh, t // blk),
        in_specs=[qs, pl.BlockSpec((1, t, LANES), lambda h, i: (h, 0, 0)), pl.BlockSpec((1, t, MLA_V), lambda h, i: (h, 0, 0)), vs, vs, ls],
        out_specs=[qs, ls],
        out_shape=[jax.ShapeDtypeStruct((nh, t, LANES), F32), jax.ShapeDtypeStruct((nh, t, 1), F32)],
        compiler_params=_cp("parallel", "parallel"), name=name)(q, k, v, do, o, lse)


def _flash_dkv(q, k, v, do, lse, delta, *, name):
    nh, t, _ = q.shape
    blk = min(FLASH_BLK, t)
    nq = t // blk
    scale = MLA_QK ** -0.5

    def body(q_ref, k_ref, v_ref, do_ref, lse_ref, dl_ref, dk_ref, dv_ref):
        j = pl.program_id(1)
        kb = k_ref[0]
        vb = v_ref[0]

        def step(i, carry, masked):
            dk, dv = carry
            off = pl.multiple_of(i * blk, blk)
            qb = q_ref[0, pl.ds(off, blk), :]
            dob = do_ref[0, pl.ds(off, blk), :]
            s = _dg(qb, kb, NT) * scale
            if masked:
                s = jnp.where(_diag_mask(blk), s, NEG_INF)
            p = jnp.exp(s - lse_ref[0, pl.ds(off, blk), :])
            ds = p * (_dg(dob, vb, NT) - dl_ref[0, pl.ds(off, blk), :]) * scale
            return dk + _dg(ds, qb, TN), dv + _dg(p, dob, TN)

        carry = step(j, (jnp.zeros((blk, LANES), F32), jnp.zeros((blk, MLA_V), F32)), True)
        dk, dv = lax.fori_loop(j + 1, nq, lambda i, c: step(i, c, False), carry)
        dk_ref[0] = dk
        dv_ref[0] = dv

    ks = pl.BlockSpec((1, blk, LANES), lambda h, j: (h, j, 0))
    vs = pl.BlockSpec((1, blk, MLA_V), lambda h, j: (h, j, 0))
    fl = lambda w: pl.BlockSpec((1, t, w), lambda h, j: (h, 0, 0))
    return pl.pallas_call(
        body, grid=(nh, nq),
        in_specs=[fl(LANES), ks, vs, fl(MLA_V), fl(1), fl(1)],
        out_specs=[ks, vs],
        out_shape=[jax.ShapeDtypeStruct((nh, t, LANES), F32), jax.ShapeDtypeStruct((nh, t, MLA_V), F32)],
        compiler_params=_cp("parallel", "parallel"), name=name)(q, k, v, do, lse, delta)


def _heads_first(a, width):
    t = a.shape[0]
    return a.reshape(t, MLA_HEADS, width).transpose(1, 0, 2)


def _heads_last(a):
    return a.transpose(1, 0, 2).reshape(a.shape[1], -1)


def _mla_layer_fwd(h, pos, w, tag):
    t = h.shape[0]
    zq = jnp.zeros((1, MLA_Q_RANK), F32)
    zkv = jnp.zeros((1, MLA_KV_RANK), F32)
    p = _mm(h, w["w_in"], name=tag + "_in")
    cq, ckv, krp = p[:, :MLA_Q_RANK], p[:, MLA_Q_RANK:MLA_Q_RANK + MLA_KV_RANK], p[:, MLA_Q_RANK + MLA_KV_RANK:MLA_IN]
    qn = _premod_fwd(cq, w["q_norm_g"], zq, zq, name=tag + "_qnorm")
    kvn = _premod_fwd(ckv, w["kv_norm_g"], zkv, zkv, name=tag + "_kvnorm")
    q = _mm(qn, w["w_uq"], name=tag + "_uq")
    kv = _mm(kvn, w["w_ukv"], name=tag + "_ukv")
    q_pre = jnp.pad(_heads_first(q, MLA_QK), ((0, 0), (0, 0), (0, LANES - MLA_QK)))
    kv3 = _heads_first(kv, MLA_NOPE + MLA_V)
    k_pre = jnp.concatenate([kv3[:, :, :MLA_NOPE], jnp.broadcast_to(krp[None], (MLA_HEADS, t, MLA_ROPE)),
                             jnp.zeros((MLA_HEADS, t, LANES - MLA_QK), F32)], axis=-1)
    vh = kv3[:, :, MLA_NOPE:].astype(BF16)
    half = MLA_ROPE // 2
    freq = ROPE_THETA ** (-jnp.arange(half, dtype=F32) / half)
    inv_freq = jnp.concatenate([jnp.zeros((MLA_NOPE,), F32), freq, freq, jnp.zeros((LANES - MLA_QK,), F32)])[None]
    tabs = _rope_tables(pos.reshape(t, 1), inv_freq, name=tag + "_tables")
    qr = _rope(q_pre, tabs[0], tabs[1], tabs[2], out_dtype=BF16, name=tag + "_ropeq")
    kr = _rope(k_pre, tabs[0], tabs[1], tabs[2], out_dtype=BF16, name=tag + "_ropek")
    o, lse = _flash_fwd(qr, kr, vh, name=tag + "_attn")
    of = _heads_last(o)
    y = _mm(of, w["w_out"], name=tag + "_out")
    return y, (cq, ckv, qn, kvn, qr, kr, vh, o, lse, of, tabs)


def _mla_layer_bwd(dy, h, saved, w, tag):
    cq, ckv, qn, kvn, qr, kr, vh, o, lse, of, tabs = saved
    t = h.shape[0]
    zq = jnp.zeros((1, MLA_Q_RANK), F32)
    zkv = jnp.zeros((1, MLA_KV_RANK), F32)
    dof = _mm(dy, w["w_out"], tb=True, out_dtype=BF16, name=tag + "_dof")
    dw_out = _mm(of, dy, ta=True, name=tag + "_dwout")
    do = _heads_first(dof, MLA_V)
    dqr, delta = _flash_dq(qr, kr, vh, do, o, lse, name=tag + "_attn_dq")
    dkr, dv = _flash_dkv(qr, kr, vh, do, lse, delta, name=tag + "_attn_dkv")
    dq_pre = _rope(dqr, tabs[0], tabs[3], tabs[4], out_dtype=F32, name=tag + "_ropeq_b")
    dk_sum = _rope(dkr, tabs[0], tabs[3], tabs[4], out_dtype=F32, sum_heads=True, name=tag + "_ropek_b")
    dq = _heads_last(dq_pre[:, :, :MLA_QK])
    dkv = _heads_last(jnp.concatenate([dkr[:, :, :MLA_NOPE], dv], axis=-1))
    dw_uq = _mm(qn, dq, ta=True, name=tag + "_dwuq")
    dqn = _mm(dq, w["w_uq"], tb=True, name=tag + "_dqn")
    dw_ukv = _mm(kvn, dkv, ta=True, name=tag + "_dwukv")
    dkvn = _mm(dkv, w["w_ukv"], tb=True, name=tag + "_dkvn")
    dcq, dqg, _, _ = _premod_bwd(dqn, cq, w["q_norm_g"], zq, jnp.zeros_like(cq), name=tag + "_qnorm_b")
    dckv, dkvg, _, _ = _premod_bwd(dkvn, ckv, w["kv_norm_g"], zkv, jnp.zeros_like(ckv), name=tag + "_kvnorm_b")
    dp = jnp.concatenate([dcq, dckv, dk_sum[:, MLA_NOPE:MLA_QK], jnp.zeros((t, MLA_IN_PAD - MLA_IN), F32)], axis=1).astype(BF16)
    dw_in = _mm(h, dp, ta=True, name=tag + "_dwin")
    dh = _mm(dp, w["w_in"], tb=True, name=tag + "_dh")
    grads = dict(w_in=dw_in[:, :MLA_IN], q_norm_g=dqg[0], w_uq=dw_uq, kv_norm_g=dkvg[0], w_ukv=dw_ukv, w_out=dw_out)
    return dh, grads


CONV_HALO = 8


def _conv_tiles(t):
    return min(512, t), 512


def _gdn_conv_fwd(p, w, *, name):
    t = p.shape[0]
    tr, tc = _conv_tiles(t)
    hb = tr // CONV_HALO

    def body(x_ref, halo_ref, w_ref, o_ref, buf):
        i = pl.program_id(0)
        buf[0:CONV_HALO, :] = halo_ref[...] * (i > 0).astype(F32)
        buf[CONV_HALO:CONV_HALO + tr, :] = x_ref[...]
        base = CONV_HALO - (GDN_CONV - 1)
        acc = buf[pl.ds(base, tr), :] * w_ref[0:1, :]
        for j in range(1, GDN_CONV):
            acc = acc + buf[pl.ds(base + j, tr), :] * w_ref[j:j + 1, :]
        o_ref[...] = acc * _sigmoid(acc)

    return pl.pallas_call(
        body, grid=(t // tr, GDN_CONV_W // tc),
        in_specs=[pl.BlockSpec((tr, tc), lambda i, j: (i, j)),
                  pl.BlockSpec((CONV_HALO, tc), lambda i, j: (jnp.maximum(i * hb - 1, 0), j)),
                  pl.BlockSpec((GDN_CONV, tc), lambda i, j: (0, j))],
        out_specs=pl.BlockSpec((tr, tc), lambda i, j: (i, j)),
        out_shape=jax.ShapeDtypeStruct((t, GDN_CONV_W), F32),
        scratch_shapes=[pltpu.VMEM((tr + CONV_HALO, tc), F32)],
        compiler_params=_cp("parallel", "parallel"), name=name)(p, p, w)


def _gdn_conv_bwd(dqkv, p, w, *, name):
    t = p.shape[0]
    tr, tc = _conv_tiles(t)
    hb = tr // CONV_HALO
    nr = t // tr
    ext = tr + CONV_HALO

    def body(x_ref, xp_ref, xn_ref, d_ref, dn_ref, w_ref, dx_ref, dw_ref, bufx, bufd):
        i = pl.program_id(1)

        @pl.when(i == 0)
        def _():
            dw_ref[...] = jnp.zeros_like(dw_ref)

        last = (i < nr - 1).astype(F32)
        bufx[0:CONV_HALO, :] = xp_ref[...] * (i > 0).astype(F32)
        bufx[CONV_HALO:CONV_HALO + tr, :] = x_ref[...]
        bufx[CONV_HALO + tr:, :] = xn_ref[...] * last
        base = CONV_HALO - (GDN_CONV - 1)
        acc = bufx[pl.ds(base, ext), :] * w_ref[0:1, :]
        for j in range(1, GDN_CONV):
            acc = acc + bufx[pl.ds(base + j, ext), :] * w_ref[j:j + 1, :]
        sg = _sigmoid(acc)
        dsilu = sg * (1.0 + acc * (1.0 - sg))
        bufd[0:tr, :] = d_ref[...] * dsilu[0:tr, :]
        bufd[tr:, :] = dn_ref[...] * last * dsilu[tr:, :]
        dx = bufd[pl.ds(GDN_CONV - 1, tr), :] * w_ref[0:1, :]
        for j in range(1, GDN_CONV):
            dx = dx + bufd[pl.ds(GDN_CONV - 1 - j, tr), :] * w_ref[j:j + 1, :]
        dx_ref[...] = dx.astype(BF16)
        dc = bufd[0:tr, :]
        for j in range(GDN_CONV):
            dw_ref[j:j + 1, :] += jnp.sum(dc * bufx[pl.ds(base + j, tr), :], axis=0, keepdims=True)

    main = pl.BlockSpec((tr, tc), lambda j, i: (i, j))
    prev = pl.BlockSpec((CONV_HALO, tc), lambda j, i: (jnp.maximum(i * hb - 1, 0), j))
    nxt = pl.BlockSpec((CONV_HALO, tc), lambda j, i: (jnp.minimum((i + 1) * hb, t // CONV_HALO - 1), j))
    wsp = pl.BlockSpec((GDN_CONV, tc), lambda j, i: (0, j))
    return pl.pallas_call(
        body, grid=(GDN_CONV_W // tc, nr),
        in_specs=[main, prev, nxt, main, nxt, wsp], out_specs=[main, wsp],
        out_shape=[jax.ShapeDtypeStruct((t, GDN_CONV_W), BF16), jax.ShapeDtypeStruct((GDN_CONV, GDN_CONV_W), F32)],
        scratch_shapes=[pltpu.VMEM((tr + 2 * CONV_HALO, tc), F32), pltpu.VMEM((ext, tc), F32)],
        compiler_params=_cp("parallel", "arbitrary"), name=name)(p, p, p, dqkv, dqkv, w)


def _dot_sel(sel, b, dims=NN, sel_first=True):
    s = sel.astype(BF16)
    b1 = b.astype(BF16)
    r1 = b - b1.astype(F32)
    b2 = r1.astype(BF16)
    b3 = (r1 - b2.astype(F32)).astype(BF16)
    if sel_first:
        d = lambda v: lax.dot_general(s, v, (dims, ((), ())), preferred_element_type=F32)
    else:
        d = lambda v: lax.dot_general(v, s, (dims, ((), ())), preferred_element_type=F32)
    return d(b1) + (d(b2) + d(b3))


def _gdn_chunks(qraws, kraws, vs, braws, araws, alogs, dtbs):
    nv = len(vs)
    row = _iota2((CHUNK, CHUNK), 0)
    col = _iota2((CHUNK, CHUNK), 1)
    strict = row > col
    triu = (row <= col).astype(F32)
    tril = (row >= col).astype(F32)
    ones = jnp.ones((CHUNK, CHUNK), F32)
    keys = []
    for qraw, kraw in zip(qraws, kraws):
        rq = lax.rsqrt(jnp.sum(qraw * qraw, axis=-1, keepdims=True) + EPS)
        rk = lax.rsqrt(jnp.sum(kraw * kraw, axis=-1, keepdims=True) + EPS)
        qn = qraw * rq
        keys.append(dict(rq=rq, rk=rk, qn=qn, qh=qn * (GDN_DK ** -0.5), kh=kraw * rk))
    kks = [_dg(kd["kh"], kd["kh"], NT) for kd in keys]
    cs = []
    for h in range(nv):
        c = dict(keys[h // 2])
        c.update(v=vs[h], kk=kks[h // 2], strict=strict, triu=triu)
        c["beta"] = _sigmoid(braws[h])
        c["ea"] = jnp.exp(alogs[h])
        c["xs"] = araws[h] + dtbs[h]
        c["g"] = -c["ea"] * _softplus(c["xs"])
        cs.append(c)
    gbs = [jnp.broadcast_to(c["g"], (CHUNK, LANES)) for c in cs]
    cums = [_dot_sel(tril, gb) for gb in gbs]
    cum_js = [_dot_sel(ones, gb[:, :CHUNK] * triu) for gb in gbs]
    for c, cum, cum_j in zip(cs, cums, cum_js):
        diff = jnp.where(strict, cum[:, :CHUNK] - cum_j, 0.0)
        c["dm"] = jnp.where(strict, jnp.exp(diff), 0.0)
        c["a"] = (c["beta"] * c["dm"]) * c["kk"]
        c_last = cum[CHUNK - 1:CHUNK, :]
        c["e"] = jnp.exp(cum)
        c["f"] = jnp.exp(c_last - cum)
        c["dec"] = jnp.exp(c_last)
        c["rv"] = c["beta"] * c["v"]
        c["rk_rhs"] = (c["beta"] * c["e"]) * c["kh"]
        c["ke"] = c["kh"] * c["f"]
    return cs


def _unit_lower_inverses(mats):
    eye = (_iota2((CHUNK, CHUNK), 0) == _iota2((CHUNK, CHUNK), 1)).astype(F32)
    ts = [eye - a for a in mats]
    pws = list(mats)
    for _ in range(5):
        pws = [_dot3(pw, pw) for pw in pws]
        ts = [t + _dot3(t, pw) for t, pw in zip(ts, pws)]
    return ts


GDN_HB = 8


def _gdn_specs(chunk_of):
    hb = GDN_HB
    kw = hb // 2 * GDN_DK
    vw = hb * GDN_DV
    qs = pl.BlockSpec((CHUNK, kw), lambda g, n: (chunk_of(n), g))
    ks = pl.BlockSpec((CHUNK, kw), lambda g, n: (chunk_of(n), GDN_QKW // kw + g))
    vs = pl.BlockSpec((CHUNK, vw), lambda g, n: (chunk_of(n), 2 * GDN_QKW // vw + g))
    zs = pl.BlockSpec((CHUNK, vw), lambda g, n: (chunk_of(n), GDN_CONV_W // vw + g))
    col = pl.BlockSpec((hb, CHUNK, 1), lambda g, n: (g, chunk_of(n), 0))
    one = pl.BlockSpec((hb, 1, 1), lambda g, n: (g, 0, 0))
    ng = pl.BlockSpec((1, GDN_DV), lambda g, n: (0, 0))
    hd = pl.BlockSpec((CHUNK, vw), lambda g, n: (chunk_of(n), g))
    return qs, ks, vs, zs, col, one, ng, hd


def _gdn_fwd(qkv, p, braw, araw, alog, dtb, ng, *, name):
    t = qkv.shape[0]
    nc = t // CHUNK
    nh = GDN_V_HEADS

    def body(q_ref, k_ref, v_ref, z_ref, braw_ref, araw_ref, alog_ref, dtb_ref, ng_ref, og_ref, s_ref, t_ref, st):
        @pl.when(pl.program_id(1) == 0)
        def _():
            st[...] = jnp.zeros_like(st)

        hs = range(GDN_HB)
        kqs = [slice(j * GDN_DK, (j + 1) * GDN_DK) for j in range(GDN_HB // 2)]
        vsl = [slice(h * GDN_DV, (h + 1) * GDN_DV) for h in hs]
        cs = _gdn_chunks([q_ref[:, s] for s in kqs], [k_ref[:, s] for s in kqs], [v_ref[:, s] for s in vsl],
                         [braw_ref[h] for h in hs], [araw_ref[h] for h in hs], [alog_ref[h] for h in hs], [dtb_ref[h] for h in hs])
        tms = _unit_lower_inverses([c["a"] for c in cs])
        s0 = [st[h] for h in hs]
        wv = [_dot3(tms[h], cs[h]["rv"]) for h in hs]
        wk = [_dot3(tms[h], cs[h]["rk_rhs"]) for h in hs]
        u = [wv[h] - _dg(wk[h], s0[h], NN) for h in hs]
        s1 = [cs[h]["dec"] * s0[h] + _dg(cs[h]["ke"], u[h], TN) for h in hs]
        o = [_dg(cs[h]["qh"], s1[h], NN) for h in hs]
        for h in hs:
            t_ref[h, 0] = tms[h]
            st[h] = s1[h]
            s_ref[h, 0] = s1[h]
            rn = lax.rsqrt(jnp.mean(o[h] * o[h], axis=-1, keepdims=True) + EPS)
            zv = z_ref[:, vsl[h]]
            og_ref[:, vsl[h]] = (((o[h] * rn) * ng_ref[...]) * (zv * _sigmoid(zv))).astype(BF16)

    qs, ks, vs, zs, col, one, ngs, hd = _gdn_specs(lambda n: n)
    return pl.pallas_call(
        body, grid=(nh // GDN_HB, nc),
        in_specs=[qs, ks, vs, zs, col, col, one, one, ngs],
        out_specs=[hd,
                   pl.BlockSpec((GDN_HB, 1, GDN_DK, GDN_DV), lambda g, n: (g, n, 0, 0)),
                   pl.BlockSpec((GDN_HB, 1, CHUNK, CHUNK), lambda g, n: (g, n, 0, 0))],
        out_shape=[jax.ShapeDtypeStruct((t, GDN_VW), BF16), jax.ShapeDtypeStruct((nh, nc, GDN_DK, GDN_DV), F32),
                   jax.ShapeDtypeStruct((nh, nc, CHUNK, CHUNK), F32)],
        scratch_shapes=[pltpu.VMEM((GDN_HB, GDN_DK, GDN_DV), F32)],
        compiler_params=_cp("parallel", "arbitrary"), name=name)(qkv, qkv, qkv, p, braw, araw, alog, dtb, ng)


def _gdn_bwd(qkv, p, braw, araw, alog, dtb, ng, dog, sall, tall, *, name):
    t = qkv.shape[0]
    nc = t // CHUNK
    nh = GDN_V_HEADS

    def body(q_ref, k_ref, v_ref, z_ref, braw_ref, araw_ref, alog_ref, dtb_ref, ng_ref, dog_ref, s1_ref, s0_ref, t_ref,
             dq_ref, dk_ref, dv_ref, dz_ref, dbraw_ref, daraw_ref, dalog_ref, ddtb_ref, dng_ref, gc):
        grp = pl.program_id(0)
        i = pl.program_id(1)

        @pl.when(i == 0)
        def _():
            gc[...] = jnp.zeros_like(gc)
            dalog_ref[...] = jnp.zeros_like(dalog_ref)
            ddtb_ref[...] = jnp.zeros_like(ddtb_ref)

        @pl.when((i == 0) & (grp == 0))
        def _():
            dng_ref[...] = jnp.zeros_like(dng_ref)

        has_prev = (i < nc - 1).astype(F32)
        ngv = ng_ref[...]
        ones = jnp.ones((CHUNK, LANES), F32)
        hs = range(GDN_HB)
        kqs = [slice(j * GDN_DK, (j + 1) * GDN_DK) for j in range(GDN_HB // 2)]
        vsl = [slice(h * GDN_DV, (h + 1) * GDN_DV) for h in hs]
        cs = _gdn_chunks([q_ref[:, s] for s in kqs], [k_ref[:, s] for s in kqs], [v_ref[:, s] for s in vsl],
                         [braw_ref[h] for h in hs], [araw_ref[h] for h in hs], [alog_ref[h] for h in hs], [dtb_ref[h] for h in hs])
        tms = [t_ref[h, 0] for h in hs]
        s1 = [s1_ref[h, 0] for h in hs]
        s0 = [s0_ref[h, 0] * has_prev for h in hs]
        wv = [_dot3(tms[h], cs[h]["rv"]) for h in hs]
        wk = [_dot3(tms[h], cs[h]["rk_rhs"]) for h in hs]
        u = [wv[h] - _dg(wk[h], s0[h], NN) for h in hs]
        o = [_dg(cs[h]["qh"], s1[h], NN) for h in hs]
        dng = jnp.zeros((1, GDN_DV), F32)
        do = []
        for h in hs:
            zv = z_ref[:, vsl[h]]
            dogv = dog_ref[:, vsl[h]]
            rn = lax.rsqrt(jnp.mean(o[h] * o[h], axis=-1, keepdims=True) + EPS)
            zo = o[h] * rn
            sg = _sigmoid(zv)
            sl = zv * sg
            dng = dng + jnp.sum(dogv * zo * sl, axis=0, keepdims=True)
            dz_ref[:, vsl[h]] = dogv * (zo * ngv) * (sg * (1.0 + zv * (1.0 - sg)))
            dzo = dogv * sl * ngv
            do.append(rn * (dzo - zo * jnp.mean(dzo * zo, axis=-1, keepdims=True)))
        dng_ref[...] += dng
        g_tot = [gc[h] + _dg(cs[h]["qh"], do[h], TN) for h in hs]
        dqh = [_dg(do[h], s1[h], NT) for h in hs]
        dke = [_dg(u[h], g_tot[h], NT) for h in hs]
        du = [_dg(cs[h]["ke"], g_tot[h], NN) for h in hs]
        gnew = [cs[h]["dec"] * g_tot[h] - _dg(wk[h], du[h], TN) for h in hs]
        dwk = [-_dg(du[h], s0[h], NT) for h in hs]
        drv = [_dot3(tms[h], du[h], TN) for h in hs]
        drk = [_dot3(tms[h], dwk[h], TN) for h in hs]
        da = [jnp.where(cs[h]["strict"], -(_dot3(drv[h], wv[h], NT) + _dot3(drk[h], wk[h], NT)), 0.0) for h in hs]
        mx = [da[h] * cs[h]["dm"] * cs[h]["kk"] for h in hs]
        aa = [mx[h] * cs[h]["beta"] for h in hs]
        colsum = [_dot_sel(ones, aa[h], TN, sel_first=False)[:, 0:1] for h in hs]
        bm = [(da[h] * cs[h]["beta"]) * cs[h]["dm"] for h in hs]
        dkh = [_dg(bm[h], cs[h]["kh"], NN) + _dg(bm[h], cs[h]["kh"], TN) for h in hs]
        dcum, dcl, dbeta = [], [], []
        for h in hs:
            c = cs[h]
            beta, kh, e, f, dec, ke = c["beta"], c["kh"], c["e"], c["f"], c["dec"], c["ke"]
            gc[h] = gnew[h]
            ddec = jnp.sum(jnp.sum(g_tot[h] * s0[h], axis=1, keepdims=True), axis=0, keepdims=True)
            dv_ref[:, vsl[h]] = beta * drv[h]
            db = jnp.sum(mx[h], axis=1, keepdims=True) + jnp.sum(drv[h] * c["v"], axis=1, keepdims=True)
            dbeta.append(db + jnp.sum(drk[h] * (e * kh), axis=1, keepdims=True))
            dkh[h] = dkh[h] + (beta * e) * drk[h] + f * dke[h]
            ef = jnp.sum(dke[h] * ke, axis=1, keepdims=True)
            dcum.append(jnp.sum(aa[h], axis=1, keepdims=True) - colsum[h] + jnp.sum(drk[h] * c["rk_rhs"], axis=1, keepdims=True) - ef)
            dcl.append(jnp.sum(ef, axis=0, keepdims=True) + ddec * dec[:, 0:1])
        dg = [_dot_sel(cs[h]["triu"], jnp.broadcast_to(dcum[h], (CHUNK, LANES)))[:, 0:1] + dcl[h] for h in hs]
        for h in hs:
            c = cs[h]
            beta, kh = c["beta"], c["kh"]
            daraw = dg[h] * (-c["ea"]) * _sigmoid(c["xs"])
            daraw_ref[h] = daraw
            dbraw_ref[h] = dbeta[h] * beta * (1.0 - beta)
            dalog_ref[h] += jnp.sum(dg[h] * c["g"], axis=0, keepdims=True)
            ddtb_ref[h] += jnp.sum(daraw, axis=0, keepdims=True)
            dn = dqh[h] * (GDN_DK ** -0.5)
            dq_ref[:, vsl[h]] = c["rq"] * (dn - c["qn"] * jnp.sum(dn * c["qn"], axis=-1, keepdims=True))
            dk_ref[:, vsl[h]] = c["rk"] * (dkh[h] - kh * jnp.sum(dkh[h] * kh, axis=-1, keepdims=True))

    rev = lambda n: nc - 1 - n
    qs, ks, vs, zs, col, one, ngs, hd = _gdn_specs(rev)
    s1s = pl.BlockSpec((GDN_HB, 1, GDN_DK, GDN_DV), lambda g, n: (g, rev(n), 0, 0))
    s0s = pl.BlockSpec((GDN_HB, 1, GDN_DK, GDN_DV), lambda g, n: (g, jnp.maximum(rev(n) - 1, 0), 0, 0))
    ts = pl.BlockSpec((GDN_HB, 1, CHUNK, CHUNK), lambda g, n: (g, rev(n), 0, 0))
    big = jax.ShapeDtypeStruct((t, GDN_VW), F32)
    cols = jax.ShapeDtypeStruct((nh, t, 1), F32)
    ones_s = jax.ShapeDtypeStruct((nh, 1, 1), F32)
    return pl.pallas_call(
        body, grid=(nh // GDN_HB, nc),
        in_specs=[qs, ks, vs, zs, col, col, one, one, ngs, hd, s1s, s0s, ts],
        out_specs=[hd, hd, hd, hd, col, col, one, one, ngs],
        out_shape=[big, big, big, big, cols, cols, ones_s, ones_s, jax.ShapeDtypeStruct((1, GDN_DV), F32)],
        scratch_shapes=[pltpu.VMEM((GDN_HB, GDN_DK, GDN_DV), F32)],
        compiler_params=_cp("arbitrary", "arbitrary"), name=name)(qkv, qkv, qkv, p, braw, araw, alog, dtb, ng, dog, sall, sall, tall)


def _gdn_layer_fwd(h, w, tag):
    p = _mm(h, w["w_in"], name=tag + "_in")
    qkv = _gdn_conv_fwd(p, w["conv_w"], name=tag + "_conv")
    braw = p[:, GDN_CONV_W + GDN_VW:GDN_CONV_W + GDN_VW + GDN_V_HEADS].T[:, :, None]
    araw = p[:, GDN_CONV_W + GDN_VW + GDN_V_HEADS:GDN_IN].T[:, :, None]
    og, sall, tall = _gdn_fwd(qkv, p, braw, araw, w["a_log"], w["dt_bias"], w["norm_g"], name=tag + "_scan")
    y = _mm(og, w["w_out"], name=tag + "_out")
    return y, (p, qkv, braw, araw, og, sall, tall)


def _gdn_layer_bwd(dy, h, saved, w, tag):
    p, qkv, braw, araw, og, sall, tall = saved
    t = h.shape[0]
    dog = _mm(dy, w["w_out"], tb=True, name=tag + "_dog")
    dw_out = _mm(og, dy, ta=True, name=tag + "_dwout")
    dq16, dk16, dv, dz, dbraw, daraw, dalog, ddtb, dng = _gdn_bwd(
        qkv, p, braw, araw, w["a_log"], w["dt_bias"], w["norm_g"], dog, sall, tall, name=tag + "_scanb")
    pair = lambda a: a.reshape(t, GDN_K_HEADS, 2, GDN_DK).sum(axis=2).reshape(t, GDN_QKW)
    dqkv = jnp.concatenate([pair(dq16), pair(dk16), dv], axis=1)
    dpre, dconv_w = _gdn_conv_bwd(dqkv, p, w["conv_w"], name=tag + "_convb")
    dp = jnp.concatenate([dpre, dz.astype(BF16), dbraw[:, :, 0].T.astype(BF16), daraw[:, :, 0].T.astype(BF16),
                          jnp.zeros((t, GDN_IN_PAD - GDN_IN), BF16)], axis=1)
    dw_in = _mm(h, dp, ta=True, name=tag + "_dwin")
    dh = _mm(dp, w["w_in"], tb=True, name=tag + "_dh")
    grads = dict(w_in=dw_in[:, :GDN_IN], conv_w=dconv_w, a_log=dalog[:, 0, 0], dt_bias=ddtb[:, 0, 0], norm_g=dng[0], w_out=dw_out)
    return dh, grads


MESH_ID = pl.DeviceIdType.MESH
FLAT_W = 1024
FLAT_ROWS = 13056
FLAT_TILE = 384


def _exchange(name, ins, out_shapes, plan, n_remote, n_local):
    def body(*refs):
        in_refs = refs[:len(ins)]
        out_refs = refs[len(ins):len(ins) + len(out_shapes)]
        ssem, rsem, lsem = refs[len(ins) + len(out_shapes):]
        x, y, c = lax.axis_index("x"), lax.axis_index("y"), lax.axis_index("c")
        stages, local_copies = plan(x, y, c, in_refs, out_refs)
        assert sum(len(s) for s in stages) == n_remote and len(local_copies) == n_local
        locs = [pltpu.make_async_copy(s, d, lsem.at[i]) for i, (s, d) in enumerate(local_copies)]
        for cp in locs:
            cp.start()
        sent = []
        k = 0
        for stage in stages:
            arrivals = []
            for src, dst, peer, landing in stage:
                cp = pltpu.make_async_remote_copy(src_ref=src, dst_ref=dst, send_sem=ssem.at[k], recv_sem=rsem.at[k],
                                                  device_id=peer, device_id_type=MESH_ID)
                cp.start()
                sent.append(cp)
                arrivals.append(pltpu.make_async_remote_copy(src_ref=src, dst_ref=landing, send_sem=ssem.at[k],
                                                             recv_sem=rsem.at[k], device_id=peer, device_id_type=MESH_ID))
                k += 1
            for cp in arrivals:
                cp.wait_recv()
        for cp in sent:
            cp.wait_send()
        for cp in locs:
            cp.wait()

    hbm = pl.BlockSpec(memory_space=pl.ANY)
    return pl.pallas_call(
        body, in_specs=[hbm] * len(ins), out_specs=[hbm] * len(out_shapes), out_shape=out_shapes,
        scratch_shapes=[pltpu.SemaphoreType.DMA((n_remote,)), pltpu.SemaphoreType.DMA((n_remote,)),
                        pltpu.SemaphoreType.DMA((max(n_local, 1),))],
        name=name)(*ins)


def _other_chips(x, y):
    return [(1 - x, y), (x, 1 - y), (1 - x, 1 - y)]


def _all8_gather(a, *, name):
    def plan(x, y, c, ins, outs):
        (src,), (dst,) = ins, outs
        me = 4 * x + 2 * y + c
        stage = []
        for fx, fy, fc in [(0, 0, 1), (0, 1, 0), (0, 1, 1), (1, 0, 0), (1, 0, 1), (1, 1, 0), (1, 1, 1)]:
            px, py, pc = (1 - x if fx else x), (1 - y if fy else y), (1 - c if fc else c)
            stage.append((src, dst.at[me], (px, py, pc), dst.at[4 * px + 2 * py + pc]))
        return [stage], [(src, dst.at[me])]

    return _exchange(name, [a], [jax.ShapeDtypeStruct((8,) + a.shape, a.dtype)], plan, 7, 1)[0]


def _chip_gather(flat, *, name):
    rows = flat.shape[0]
    half = rows // 2

    def plan(x, y, c, ins, outs):
        (src,), (dst,) = ins, outs
        me = 2 * x + y
        mine = pl.ds(c * half, half)
        theirs = pl.ds((1 - c) * half, half)
        ici = [(src.at[mine], dst.at[me, mine], (px, py, c), dst.at[2 * px + py, mine]) for px, py in _other_chips(x, y)]
        d2d = [(dst.at[2 * px + py, mine], dst.at[2 * px + py, mine], (x, y, 1 - c), dst.at[2 * px + py, theirs])
               for px, py in _other_chips(x, y)]
        return [ici, d2d], []

    return _exchange(name, [flat], [jax.ShapeDtypeStruct((4,) + flat.shape, flat.dtype)], plan, 6, 0)[0]


def _add_sibling(gf, buf_a, core, *, name):
    _, rows, w = gf.shape
    half = rows // 2
    nb = half // FLAT_TILE

    def body(c_ref, g_ref, a_ref, o_ref):
        o_ref[...] = (g_ref[...] + a_ref[...]).astype(BF16)

    blk = (1, FLAT_TILE, w)
    return pl.pallas_call(
        body,
        grid_spec=pltpu.PrefetchScalarGridSpec(
            num_scalar_prefetch=1, grid=(4, nb),
            in_specs=[pl.BlockSpec(blk, lambda s, i, c_ref: (s, c_ref[0] * nb + i, 0)), pl.BlockSpec(blk, lambda s, i, c_ref: (s, i, 0))],
            out_specs=pl.BlockSpec(blk, lambda s, i, c_ref: (s, i, 0))),
        out_shape=jax.ShapeDtypeStruct((4, half, w), BF16), compiler_params=_cp("parallel", "parallel"), name=name)(core, gf, buf_a)


def _sum_chips(hsum, buf_b, chip, *, name):
    _, half, w = hsum.shape
    nb = half // FLAT_TILE

    def body(c_ref, h_ref, b0_ref, b1_ref, b2_ref, b3_ref, o_ref):
        me = c_ref[0]
        own = h_ref[0].astype(F32)
        acc = None
        for j, b_ref in enumerate((b0_ref, b1_ref, b2_ref, b3_ref)):
            term = jnp.where(me == j, own, b_ref[0].astype(F32))
            acc = term if acc is None else acc + term
        o_ref[...] = acc

    blk = (1, FLAT_TILE, w)

    def other(j):
        return pl.BlockSpec(blk, lambda i, c_ref: (jnp.where(c_ref[0] == j, (j + 1) % 4, j), i, 0))

    return pl.pallas_call(
        body,
        grid_spec=pltpu.PrefetchScalarGridSpec(
            num_scalar_prefetch=1, grid=(nb,),
            in_specs=[pl.BlockSpec(blk, lambda i, c_ref: (c_ref[0], i, 0))] + [other(j) for j in range(4)],
            out_specs=pl.BlockSpec((FLAT_TILE, w), lambda i, c_ref: (i, 0))),
        out_shape=jax.ShapeDtypeStruct((half, w), F32), compiler_params=_cp("parallel"), name=name)(chip, hsum, buf_b, buf_b, buf_b, buf_b)


def _sum_slots(buf, *, name):
    n, rows, w = buf.shape
    tr = _pick(rows, (FLAT_TILE, 8))

    def body(b_ref, o_ref):
        acc = b_ref[0]
        for s in range(1, n):
            acc = acc + b_ref[s]
        o_ref[...] = acc

    return pl.pallas_call(body, grid=(rows // tr,), in_specs=[pl.BlockSpec((n, tr, w), lambda i: (0, i, 0))],
                          out_specs=pl.BlockSpec((tr, w), lambda i: (i, 0)), out_shape=jax.ShapeDtypeStruct((rows, w), F32),
                          compiler_params=_cp("parallel"), name=name)(buf)


def _reduce_scatter(gf, core, chip, *, tag):
    _, rows, w = gf.shape
    half = rows // 2

    def plan_a(x, y, c, ins, outs):
        (src,), (dst,) = ins, outs
        return [[(src.at[:, pl.ds((1 - c) * half, half)], dst, (x, y, 1 - c), dst)]], []

    buf_a = _exchange(tag + "_sibling", [gf], [jax.ShapeDtypeStruct((4, half, w), F32)], plan_a, 1, 0)[0]
    hsum = _add_sibling(gf, buf_a, core, name=tag + "_add_sibling")

    def plan_b(x, y, c, ins, outs):
        (src,), (dst,) = ins, outs
        me = 2 * x + y
        stage = [(src.at[2 * px + py], dst.at[me], (px, py, c), dst.at[2 * px + py]) for px, py in _other_chips(x, y)]
        return [stage], []

    buf_b = _exchange(tag + "_chips", [hsum], [jax.ShapeDtypeStruct((4, half, w), BF16)], plan_b, 3, 0)[0]
    mine = _sum_chips(hsum, buf_b, chip, name=tag + "_sum_chips")

    def plan_c(x, y, c, ins, outs):
        (src,), (dst,) = ins, outs
        return [[(src, dst, (x, y, 1 - c), dst)]], []

    theirs = _exchange(tag + "_halves", [mine], [jax.ShapeDtypeStruct((half, w), F32)], plan_c, 1, 0)[0]
    first = core[0] == 0
    return jnp.concatenate([jnp.where(first, mine, theirs), jnp.where(first, theirs, mine)], axis=0)


WEIGHTS = ["ada_w", "ada_b", "norm_pre_g", "norm_post_g", "gla_w_in", "gla_w_gate_up", "gla_b_gate", "gla_head_g",
           "gla_w_out", "mla_w_in", "mla_q_norm_g", "mla_w_uq", "mla_kv_norm_g", "mla_w_ukv", "mla_w_out", "gdn_w_in",
           "gdn_conv_w", "gdn_a_log", "gdn_dt_bias", "gdn_norm_g", "gdn_w_out", "mlp_w_up", "mlp_w_down"]
PACK_BF16 = [("gla_w_in", 2), ("gla_w_out", 1), ("mla_w_in", 1), ("mla_w_uq", 2), ("mla_w_ukv", 2), ("mla_w_out", 1),
             ("gdn_w_in", 2), ("gdn_w_out", 1), ("mlp_w_up", 2), ("mlp_w_down", 1)]
PACK_F32 = [("norm_pre_g", 2), ("norm_post_g", 2), ("gla_w_gate_up", 2), ("gla_b_gate", 1), ("gla_head_g", 1), ("gdn_conv_w", 2)]
REPLICATED_SMALL = ["mla_q_norm_g", "mla_kv_norm_g", "gdn_a_log", "gdn_dt_bias", "gdn_norm_g"]
MIXERS = ["gla", "mla", "gdn"]


def _silu_rows(a, *, name):
    def body(a_ref, o_ref):
        v = a_ref[...]
        o_ref[...] = v * _sigmoid(v)

    return pl.pallas_call(body, out_shape=jax.ShapeDtypeStruct(a.shape, F32), name=name)(a)


SMALL_ROWS = 16


def _piece_rows(size, mult):
    assert size % FLAT_W == 0
    return -(-(size // FLAT_W) // mult) * mult


def _to_rows(a, lead, mult):
    n = math.prod(a.shape[len(lead):])
    r = a.reshape(lead + (n // FLAT_W, FLAT_W))
    extra = _piece_rows(n, mult) - n // FLAT_W
    return jnp.pad(r, [(0, 0)] * len(lead) + [(0, extra), (0, 0)]) if extra else r


def _small_to_rows(parts, lead):
    flat = jnp.concatenate([p.reshape(lead + (-1,)) for p in parts], axis=-1)
    pad = SMALL_ROWS * FLAT_W - flat.shape[-1]
    return jnp.pad(flat, [(0, 0)] * len(lead) + [(0, pad)]).reshape(lead + (SMALL_ROWS, FLAT_W))


def _small_from_rows(rows, shards, lead):
    flat = rows.reshape(lead + (-1,))
    out, off = {}, 0
    for n, _ in PACK_F32:
        out[n] = flat[..., off:off + shards[n].size].reshape(lead + shards[n].shape)
        off += shards[n].size
    return out


def _pack_weights(shards):
    parts = [_to_rows(shards[n].astype(BF16), (), 16) for n, _ in PACK_BF16]
    small = _small_to_rows([shards[n] for n, _ in PACK_F32], ())
    parts.append(lax.bitcast_convert_type(small, BF16).reshape(2 * SMALL_ROWS, FLAT_W))
    flat = jnp.concatenate(parts, axis=0)
    return jnp.pad(flat, ((0, FLAT_ROWS - flat.shape[0]), (0, 0)))


def _unpack_weights(gathered, shards, chip):
    full, off = {}, 0
    for n, ax in PACK_BF16:
        size = shards[n].size
        seg = gathered[:, off:off + size // FLAT_W].reshape((4,) + shards[n].shape)
        own = shards[n].astype(BF16)
        full[n] = jnp.concatenate([jnp.where(chip == j, own, seg[j]) for j in range(4)], axis=ax)
        off += _piece_rows(size, 16)
    small = lax.bitcast_convert_type(gathered[:, off:off + 2 * SMALL_ROWS].reshape(4, SMALL_ROWS, FLAT_W, 2), F32)
    for (n, ax), seg in zip(PACK_F32, _small_from_rows(small, shards, (4,)).values()):
        full[n] = jnp.concatenate([jnp.where(chip == j, shards[n], seg[j]) for j in range(4)], axis=ax)
    return full


def _pack_grads(grads):
    split = lambda n, ax: jnp.stack(jnp.split(grads[n].astype(F32), 4, axis=ax))
    parts = [_to_rows(split(n, ax), (4,), 8) for n, ax in PACK_BF16]
    parts.append(_small_to_rows([split(n, ax) for n, ax in PACK_F32], (4,)))
    flat = jnp.concatenate(parts, axis=1)
    return jnp.pad(flat, ((0, 0), (0, FLAT_ROWS - flat.shape[1]), (0, 0)))


def _unpack_grads(reduced, shards):
    out, off = {}, 0
    for n, _ in PACK_BF16:
        size = shards[n].size
        out[n] = reduced[off:off + size // FLAT_W].reshape(shards[n].shape)
        off += _piece_rows(size, 8)
    out.update(_small_from_rows(reduced[off:off + SMALL_ROWS], shards, ()))
    return out


def _mixer_weights(kind, j, full, rep):
    if kind == "gla":
        return dict(w_in=jnp.pad(full["gla_w_in"][j], ((0, 0), (0, GLA_IN_PAD - GLA_IN))),
                    wg=jnp.pad(full["gla_w_gate_up"][j], ((0, LANES - GLA_RANK), (0, 0))),
                    bg=full["gla_b_gate"][j][None], hg=full["gla_head_g"][j][None], w_out=full["gla_w_out"][j])
    if kind == "mla":
        return dict(w_in=jnp.pad(full["mla_w_in"][j], ((0, 0), (0, MLA_IN_PAD - MLA_IN))), q_norm_g=rep["mla_q_norm_g"][j][None],
                    w_uq=full["mla_w_uq"][j], kv_norm_g=rep["mla_kv_norm_g"][j][None], w_ukv=full["mla_w_ukv"][j],
                    w_out=full["mla_w_out"][j])
    return dict(w_in=jnp.pad(full["gdn_w_in"][j], ((0, 0), (0, GDN_IN_PAD - GDN_IN))), conv_w=full["gdn_conv_w"][j],
                a_log=rep["gdn_a_log"][j][:, None, None], dt_bias=rep["gdn_dt_bias"][j][:, None, None],
                norm_g=rep["gdn_norm_g"][j][None], w_out=full["gdn_w_out"][j])


def _layer_fwd(xin, mod, gains, kind, mw, w_up, w_down, pos, tag):
    sh_m, sc_m, gt_m, sh_f, sc_f, gt_f = mod
    pre0, pre1, post0, post1 = gains
    h = _premod_fwd(xin, pre0, sc_m, sh_m, name=tag + "_pre0")
    if kind == "gla":
        y, saved = _gla_layer_fwd(h, mw, tag + "_gla")
    elif kind == "mla":
        y, saved = _mla_layer_fwd(h, pos, mw, tag + "_mla")
    else:
        y, saved = _gdn_layer_fwd(h, mw, tag + "_gdn")
    x1 = _postres_fwd(xin, y, post0, gt_m, name=tag + "_post0")
    h2 = _premod_fwd(x1, pre1, sc_f, sh_f, name=tag + "_pre1")
    act = _mm(h2, w_up, out_dtype=BF16, epi="relu2", name=tag + "_up")
    y2 = _mm(act, w_down, name=tag + "_down")
    x2 = _postres_fwd(x1, y2, post1, gt_f, name=tag + "_post1")
    return x2, (xin, h, y, saved, x1, h2, act, y2)


def _layer_bwd(g2, kept, mod, gains, kind, mw, w_up, w_down, tag):
    xin, h, y, saved, x1, h2, act, y2 = kept
    sh_m, sc_m, gt_m, sh_f, sc_f, gt_f = mod
    pre0, pre1, post0, post1 = gains
    dy2, dpost1, dgt_f = _postres_bwd(g2, y2, post1, gt_f, name=tag + "_post1_b")
    du = _mm(dy2, w_down, tb=True, out_dtype=BF16, epi="dact", aux=act, name=tag + "_du")
    dw_down = _mm(act, dy2, ta=True, name=tag + "_dwdown")
    dw_up = _mm(h2, du, ta=True, name=tag + "_dwup")
    dh2 = _mm(du, w_up, tb=True, name=tag + "_dh2")
    g1, dpre1, dsc_f, dsh_f = _premod_bwd(dh2, x1, pre1, sc_f, g2, name=tag + "_pre1_b")
    dy, dpost0, dgt_m = _postres_bwd(g1, y, post0, gt_m, name=tag + "_post0_b")
    if kind == "gla":
        dh, mg = _gla_layer_bwd(dy, h, saved, mw, tag + "_gla")
    elif kind == "mla":
        dh, mg = _mla_layer_bwd(dy, h, saved, mw, tag + "_mla")
    else:
        dh, mg = _gdn_layer_bwd(dy, h, saved, mw, tag + "_gdn")
    g0, dpre0, dsc_m, dsh_m = _premod_bwd(dh, xin, pre0, sc_m, g1, name=tag + "_pre0_b")
    dmod = jnp.concatenate([dsh_m, dsc_m, dgt_m, dsh_f, dsc_f, dgt_f], axis=1)
    return g0, dmod, jnp.concatenate([dpre0, dpre1], axis=0), jnp.concatenate([dpost0, dpost1], axis=0), mg, dw_up, dw_down


def kernel(x, c, positions, ada_w, ada_b, norm_pre_g, norm_post_g, gla_w_in, gla_w_gate_up, gla_b_gate, gla_head_g, gla_w_out, mla_w_in, mla_q_norm_g, mla_w_uq, mla_kv_norm_g, mla_w_ukv, mla_w_out, gdn_w_in, gdn_conv_w, gdn_a_log, gdn_dt_bias, gdn_norm_g, gdn_w_out, mlp_w_up, mlp_w_down, loss_target, m_ada_w, m_ada_b, m_norm_pre_g, m_norm_post_g, m_gla_w_in, m_gla_w_gate_up, m_gla_b_gate, m_gla_head_g, m_gla_w_out, m_mla_w_in, m_mla_q_norm_g, m_mla_w_uq, m_mla_kv_norm_g, m_mla_w_ukv, m_mla_w_out, m_gdn_w_in, m_gdn_conv_w, m_gdn_a_log, m_gdn_dt_bias, m_gdn_norm_g, m_gdn_w_out, m_mlp_w_up, m_mlp_w_down, v_ada_w, v_ada_b, v_norm_pre_g, v_norm_post_g, v_gla_w_in, v_gla_w_gate_up, v_gla_b_gate, v_gla_head_g, v_gla_w_out, v_mla_w_in, v_mla_q_norm_g, v_mla_w_uq, v_mla_kv_norm_g, v_mla_w_ukv, v_mla_w_out, v_gdn_w_in, v_gdn_conv_w, v_gdn_a_log, v_gdn_dt_bias, v_gdn_norm_g, v_gdn_w_out, v_mlp_w_up, v_mlp_w_down):
    w = dict(ada_w=ada_w, ada_b=ada_b, norm_pre_g=norm_pre_g, norm_post_g=norm_post_g, gla_w_in=gla_w_in,
             gla_w_gate_up=gla_w_gate_up, gla_b_gate=gla_b_gate, gla_head_g=gla_head_g, gla_w_out=gla_w_out, mla_w_in=mla_w_in,
             mla_q_norm_g=mla_q_norm_g, mla_w_uq=mla_w_uq, mla_kv_norm_g=mla_kv_norm_g, mla_w_ukv=mla_w_ukv, mla_w_out=mla_w_out,
             gdn_w_in=gdn_w_in, gdn_conv_w=gdn_conv_w, gdn_a_log=gdn_a_log, gdn_dt_bias=gdn_dt_bias, gdn_norm_g=gdn_norm_g,
             gdn_w_out=gdn_w_out, mlp_w_up=mlp_w_up, mlp_w_down=mlp_w_down)
    m = dict(zip(WEIGHTS, [m_ada_w, m_ada_b, m_norm_pre_g, m_norm_post_g, m_gla_w_in, m_gla_w_gate_up, m_gla_b_gate, m_gla_head_g,
                           m_gla_w_out, m_mla_w_in, m_mla_q_norm_g, m_mla_w_uq, m_mla_kv_norm_g, m_mla_w_ukv, m_mla_w_out,
                           m_gdn_w_in, m_gdn_conv_w, m_gdn_a_log, m_gdn_dt_bias, m_gdn_norm_g, m_gdn_w_out, m_mlp_w_up, m_mlp_w_down]))
    v = dict(zip(WEIGHTS, [v_ada_w, v_ada_b, v_norm_pre_g, v_norm_post_g, v_gla_w_in, v_gla_w_gate_up, v_gla_b_gate, v_gla_head_g,
                           v_gla_w_out, v_mla_w_in, v_mla_q_norm_g, v_mla_w_uq, v_mla_kv_norm_g, v_mla_w_ukv, v_mla_w_out,
                           v_gdn_w_in, v_gdn_conv_w, v_gdn_a_log, v_gdn_dt_bias, v_gdn_norm_g, v_gdn_w_out, v_mlp_w_up, v_mlp_w_down]))
    t = x.shape[1]
    ix, iy, ic = lax.axis_index("x"), lax.axis_index("y"), lax.axis_index("c")
    me = 4 * ix + 2 * iy + ic
    chip = 2 * ix + iy
    ada_cols = ada_w.shape[2]

    full = _unpack_weights(_chip_gather(_pack_weights(w), name="gather_weights"), w, chip)

    cond8 = _silu_rows(jnp.pad(c, ((0, 7), (0, 0))), name="cond_silu")
    cond16 = jnp.pad(_all8_gather(cond8, name="gather_cond")[:, 0, :], ((0, 8), (0, 0)))
    mod_cols = []
    for layer in range(DEPTH):
        bias = jnp.broadcast_to(lax.dynamic_slice_in_dim(ada_b[layer], chip * ada_cols, ada_cols)[None], (16, ada_cols))
        mod_cols.append(_mm(cond16, ada_w[layer], epi="add", aux=bias, name=f"ada{layer}")[:8])
    mod_all = _all8_gather(jnp.stack(mod_cols).reshape(DEPTH * 8, ada_cols), name="gather_mod")
    mod = jnp.concatenate([lax.dynamic_slice_in_dim(mod_all[2 * j].reshape(DEPTH, 8, ada_cols), me, 1, axis=1)[:, 0]
                           for j in range(4)], axis=1)

    def layer_args(layer):
        kind, j = MIXERS[layer % 3], layer // 3
        mods = [mod[layer, i * D_MODEL:(i + 1) * D_MODEL][None] for i in range(N_MOD)]
        gains = (full["norm_pre_g"][layer, 0:1], full["norm_pre_g"][layer, 1:2], full["norm_post_g"][layer, 0:1],
                 full["norm_post_g"][layer, 1:2])
        return kind, j, mods, gains, _mixer_weights(kind, j, full, w)

    xs = x[0]
    kept = []
    for layer in range(DEPTH):
        kind, j, mods, gains, mw = layer_args(layer)
        xs, keep = _layer_fwd(xs, mods, gains, kind, mw, full["mlp_w_up"][layer], full["mlp_w_down"][layer], positions[0], f"l{layer}")
        kept.append(keep)
    loss_row, g = _loss_head(xs, loss_target[0], name="loss_head")
    loss = lax.psum(loss_row[0, 0], ("x", "y", "c"))

    grads = {n: [None] * w[n].shape[0] for n, _ in PACK_BF16 + PACK_F32}
    rep_grads = {}
    dmods = [None] * DEPTH
    for layer in reversed(range(DEPTH)):
        kind, j, mods, gains, mw = layer_args(layer)
        g, dmods[layer], dpre, dpost, mg, dw_up, dw_down = _layer_bwd(
            g, kept[layer], mods, gains, kind, mw, full["mlp_w_up"][layer], full["mlp_w_down"][layer], f"l{layer}")
        grads["norm_pre_g"][layer], grads["norm_post_g"][layer] = dpre, dpost
        grads["mlp_w_up"][layer], grads["mlp_w_down"][layer] = dw_up, dw_down
        for key, val in mg.items():
            name = kind + "_" + key
            if name in grads:
                grads[name][j] = val
            else:
                rep_grads[name] = val[None]
    grads = {n: jnp.stack(parts) for n, parts in grads.items()}

    rep_flat = jnp.concatenate([rep_grads[n].reshape(-1) for n in REPLICATED_SMALL])
    dbuf = jnp.concatenate([jnp.concatenate(dmods, axis=0), jnp.pad(rep_flat, (0, N_MOD * D_MODEL - rep_flat.shape[0]))[None],
                            jnp.zeros((3, N_MOD * D_MODEL), F32)], axis=0)
    dall = _all8_gather(dbuf, name="gather_dmod")
    dsum = _sum_slots(dall, name="sum_dmod")
    out_grads = {"ada_b": dsum[:DEPTH]}
    off = 0
    for n in REPLICATED_SMALL:
        out_grads[n] = dsum[DEPTH, off:off + w[n].size].reshape(w[n].shape)
        off += w[n].size
    dada = []
    for layer in range(DEPTH):
        dm16 = jnp.pad(lax.dynamic_slice_in_dim(dall[:, layer, :], chip * ada_cols, ada_cols, axis=1), ((0, 8), (0, 0)))
        dada.append(_mm(cond16, dm16, ta=True, name=f"dada{layer}"))
    out_grads["ada_w"] = jnp.stack(dada)

    reduced = _reduce_scatter(_pack_grads(grads), ic.reshape(1).astype(jnp.int32), chip.reshape(1).astype(jnp.int32),
                              tag="reduce_grads")
    out_grads.update(_unpack_grads(reduced, w))

    deltas, new_m, new_v = {}, {}, {}
    for n in WEIGHTS:
        deltas[n], new_m[n], new_v[n] = _adamw(w[n], out_grads[n], m[n], v[n], name="adamw_" + n)
    return (loss, g[None], *[out_grads[n] for n in WEIGHTS], *[deltas[n] for n in WEIGHTS],
            *[new_m[n] for n in WEIGHTS], *[new_v[n] for n in WEIGHTS])
```

```python
import functools
import math

import jax
import jax.numpy as jnp
from jax import lax
from jax.experimental import pallas as pl
from jax.experimental.pallas import tpu as pltpu

F32 = jnp.float32
BF16 = jnp.bfloat16

D_MODEL = 1024
DEPTH = 4
CHUNK = 64
EPS = 1e-6
NEG_INF = -1e30
N_MOD = 6

GLA_HEADS, GLA_DK, GLA_DV, GLA_RANK = 4, 128, 256, 16
GLA_KW, GLA_VW = GLA_HEADS * GLA_DK, GLA_HEADS * GLA_DV
GLA_IN = 2 * GLA_KW + 2 * GLA_VW + GLA_RANK
GLA_IN_PAD = 3200

MLA_HEADS, MLA_NOPE, MLA_ROPE, MLA_V = 16, 64, 32, 64
MLA_Q_RANK, MLA_KV_RANK = 384, 256
MLA_IN = MLA_Q_RANK + MLA_KV_RANK + MLA_ROPE
MLA_IN_PAD = 768
ROPE_THETA = 10000.0
MLA_QK = MLA_NOPE + MLA_ROPE
LANES = 128

GDN_K_HEADS, GDN_V_HEADS, GDN_DK, GDN_DV, GDN_CONV = 8, 16, 128, 128, 4
GDN_QKW, GDN_VW = GDN_K_HEADS * GDN_DK, GDN_V_HEADS * GDN_DV
GDN_CONV_W = 2 * GDN_QKW + GDN_VW
GDN_IN = GDN_CONV_W + GDN_VW + 2 * GDN_V_HEADS
GDN_IN_PAD = 6400

ADAM_LR, ADAM_B1, ADAM_B2, ADAM_EPS, ADAM_WD, ADAM_STEP = 0.001, 0.9, 0.999, 1e-08, 0.01, 10

VMEM_LIMIT = 56 * 1024 * 1024

NN = ((1,), (0,))
NT = ((1,), (1,))
TN = ((0,), (0,))


def _cp(*sem):
    return pltpu.CompilerParams(dimension_semantics=sem, vmem_limit_bytes=VMEM_LIMIT)


def _pick(n, cands):
    for c in cands:
        if n % c == 0:
            return c
    return n


def _dg(a, b, dims=NN):
    return lax.dot_general(a.astype(BF16), b.astype(BF16), (dims, ((), ())), preferred_element_type=F32)


def _dot3(a, b, dims=NN):
    ah = a.astype(BF16)
    al = (a - ah.astype(F32)).astype(BF16)
    bh = b.astype(BF16)
    bl = (b - bh.astype(F32)).astype(BF16)
    d = lambda u, v: lax.dot_general(u, v, (dims, ((), ())), preferred_element_type=F32)
    return d(ah, bh) + (d(ah, bl) + d(al, bh))


def _sigmoid(x):
    return 1.0 / (1.0 + jnp.exp(-x))


def _softplus(x):
    return jnp.maximum(x, 0.0) + jnp.log(1.0 + jnp.exp(-jnp.abs(x)))


def _iota2(shape, dim):
    return lax.broadcasted_iota(jnp.int32, shape, dim)


def _mm(a, b, *, ta=False, tb=False, out_dtype=F32, epi=None, aux=None, name):
    m = a.shape[1] if ta else a.shape[0]
    k = a.shape[0] if ta else a.shape[1]
    n = b.shape[0] if tb else b.shape[1]
    assert k == (b.shape[1] if tb else b.shape[0]), (a.shape, b.shape, ta, tb)
    tm = _pick(m, (1024, 512, 384, 256, 128))
    tn = _pick(n, (1024, 640, 512, 768, 384, 256, 128))
    tk = _pick(k, (1024, 640, 512, 768, 384, 256, 128))
    nk = k // tk
    dims = ((0 if ta else 1,), (1 if tb else 0,))

    def finish(r, x_ref, o_ref):
        if epi == "relu2":
            r = jnp.square(jnp.maximum(r, 0.0))
        elif epi == "dact":
            r = r * (2.0 * jnp.sqrt(x_ref[...].astype(F32)))
        elif epi == "add":
            r = r + x_ref[...]
        o_ref[...] = r.astype(out_dtype)

    def body(*refs):
        a_ref, b_ref = refs[:2]
        x_ref = refs[2] if aux is not None else None
        o_ref = refs[3] if aux is not None else refs[2]
        if nk == 1:
            finish(_dg(a_ref[...], b_ref[...], dims), x_ref, o_ref)
            return
        acc = refs[-1]
        kk = pl.program_id(2)

        @pl.when(kk == 0)
        def _():
            acc[...] = jnp.zeros_like(acc)

        acc[...] += _dg(a_ref[...], b_ref[...], dims)

        @pl.when(kk == nk - 1)
        def _():
            finish(acc[...], x_ref, o_ref)

    a_spec = pl.BlockSpec((tk, tm), lambda i, j, q: (q, i)) if ta else pl.BlockSpec((tm, tk), lambda i, j, q: (i, q))
    b_spec = pl.BlockSpec((tn, tk), lambda i, j, q: (j, q)) if tb else pl.BlockSpec((tk, tn), lambda i, j, q: (q, j))
    o_spec = pl.BlockSpec((tm, tn), lambda i, j, q: (i, j))
    in_specs = [a_spec, b_spec] + ([o_spec] if aux is not None else [])
    args = (a, b) + ((aux,) if aux is not None else ())
    return pl.pallas_call(
        body, grid=(m // tm, n // tn, nk), in_specs=in_specs, out_specs=o_spec,
        out_shape=jax.ShapeDtypeStruct((m, n), out_dtype),
        scratch_shapes=[pltpu.VMEM((tm, tn), F32)] if nk > 1 else [],
        compiler_params=_cp("parallel", "parallel", "arbitrary"), name=name)(*args)


def _row_tile(t):
    return _pick(t, (512, 256, 128, 64, 8))


def _premod_fwd(x, g, sc, sh, *, name):
    t, c = x.shape
    tr = _row_tile(t)

    def body(x_ref, g_ref, sc_ref, sh_ref, h_ref):
        xv = x_ref[...]
        r = lax.rsqrt(jnp.mean(xv * xv, axis=-1, keepdims=True) + EPS)
        h_ref[...] = (((xv * r) * g_ref[...]) * (1.0 + sc_ref[...]) + sh_ref[...]).astype(BF16)

    row = pl.BlockSpec((tr, c), lambda i: (i, 0))
    vec = pl.BlockSpec((1, c), lambda i: (0, 0))
    return pl.pallas_call(body, grid=(t // tr,), in_specs=[row, vec, vec, vec], out_specs=row,
                          out_shape=jax.ShapeDtypeStruct((t, c), BF16), compiler_params=_cp("parallel"), name=name)(x, g, sc, sh)


def _premod_bwd(dh, x, g, sc, gin, *, name):
    t, c = x.shape
    tr = _row_tile(t)

    def body(dh_ref, x_ref, g_ref, sc_ref, gin_ref, gout_ref, dg_ref, dsc_ref, dsh_ref):
        @pl.when(pl.program_id(0) == 0)
        def _():
            dg_ref[...] = jnp.zeros_like(dg_ref)
            dsc_ref[...] = jnp.zeros_like(dsc_ref)
            dsh_ref[...] = jnp.zeros_like(dsh_ref)

        xv = x_ref[...]
        dhv = dh_ref[...].astype(F32)
        gv = g_ref[...]
        one_sc = 1.0 + sc_ref[...]
        r = lax.rsqrt(jnp.mean(xv * xv, axis=-1, keepdims=True) + EPS)
        nv = xv * r
        dsh_ref[...] += jnp.sum(dhv, axis=0, keepdims=True)
        dsc_ref[...] += jnp.sum(dhv * (nv * gv), axis=0, keepdims=True)
        dg_ref[...] += jnp.sum(dhv * nv * one_sc, axis=0, keepdims=True)
        dn = dhv * gv * one_sc
        dx = r * (dn - nv * jnp.mean(dn * nv, axis=-1, keepdims=True))
        gout_ref[...] = gin_ref[...] + dx

    row = pl.BlockSpec((tr, c), lambda i: (i, 0))
    vec = pl.BlockSpec((1, c), lambda i: (0, 0))
    vs = jax.ShapeDtypeStruct((1, c), F32)
    return pl.pallas_call(body, grid=(t // tr,), in_specs=[row, row, vec, vec, row], out_specs=[row, vec, vec, vec],
                          out_shape=[jax.ShapeDtypeStruct((t, c), F32), vs, vs, vs],
                          compiler_params=_cp("arbitrary"), name=name)(dh, x, g, sc, gin)


def _postres_fwd(x, y, g, gt, *, name):
    t, c = x.shape
    tr = _row_tile(t)

    def body(x_ref, y_ref, g_ref, gt_ref, o_ref):
        yv = y_ref[...]
        r = lax.rsqrt(jnp.mean(yv * yv, axis=-1, keepdims=True) + EPS)
        o_ref[...] = x_ref[...] + gt_ref[...] * ((yv * r) * g_ref[...])

    row = pl.BlockSpec((tr, c), lambda i: (i, 0))
    vec = pl.BlockSpec((1, c), lambda i: (0, 0))
    return pl.pallas_call(body, grid=(t // tr,), in_specs=[row, row, vec, vec], out_specs=row,
                          out_shape=jax.ShapeDtypeStruct((t, c), F32), compiler_params=_cp("parallel"), name=name)(x, y, g, gt)


def _postres_bwd(gout, y, g, gt, *, name):
    t, c = y.shape
    tr = _row_tile(t)

    def body(go_ref, y_ref, g_ref, gt_ref, dy_ref, dg_ref, dgt_ref):
        @pl.when(pl.program_id(0) == 0)
        def _():
            dg_ref[...] = jnp.zeros_like(dg_ref)
            dgt_ref[...] = jnp.zeros_like(dgt_ref)

        yv = y_ref[...]
        gov = go_ref[...]
        gv = g_ref[...]
        gtv = gt_ref[...]
        r = lax.rsqrt(jnp.mean(yv * yv, axis=-1, keepdims=True) + EPS)
        z = yv * r
        dgt_ref[...] += jnp.sum(gov * (z * gv), axis=0, keepdims=True)
        dg_ref[...] += jnp.sum(gov * gtv * z, axis=0, keepdims=True)
        dz = gov * gtv * gv
        dy_ref[...] = (r * (dz - z * jnp.mean(dz * z, axis=-1, keepdims=True))).astype(BF16)

    row = pl.BlockSpec((tr, c), lambda i: (i, 0))
    vec = pl.BlockSpec((1, c), lambda i: (0, 0))
    vs = jax.ShapeDtypeStruct((1, c), F32)
    return pl.pallas_call(body, grid=(t // tr,), in_specs=[row, row, vec, vec], out_specs=[row, vec, vec],
                          out_shape=[jax.ShapeDtypeStruct((t, c), BF16), vs, vs],
                          compiler_params=_cp("arbitrary"), name=name)(gout, y, g, gt)


def _loss_head(y, tgt, *, name):
    t, c = y.shape
    tr = _row_tile(t)

    def body(y_ref, t_ref, l_ref, dy_ref):
        @pl.when(pl.program_id(0) == 0)
        def _():
            l_ref[...] = jnp.zeros_like(l_ref)

        d = y_ref[...] - t_ref[...]
        dy_ref[...] = d * (1.0 / c)
        l_ref[...] += 0.5 * jnp.sum(jnp.mean(d * d, axis=-1, keepdims=True))

    row = pl.BlockSpec((tr, c), lambda i: (i, 0))
    return pl.pallas_call(body, grid=(t // tr,), in_specs=[row, row],
                          out_specs=[pl.BlockSpec((1, LANES), lambda i: (0, 0)), row],
                          out_shape=[jax.ShapeDtypeStruct((1, LANES), F32), jax.ShapeDtypeStruct((t, c), F32)],
                          compiler_params=_cp("arbitrary"), name=name)(y, tgt)


def _adamw(w, g, m, v, *, name):
    shape = w.shape
    c = shape[-1]
    r = math.prod(shape[:-1])
    w2, g2, m2, v2 = (a.reshape(r, c) for a in (w, g, m, v))
    tr = r
    for cand in (1024, 512, 256, 128, 64, 32, 16, 8):
        if r % cand == 0 and cand * c * 4 <= (1 << 20):
            tr = cand
            break
    c1 = 1.0 - ADAM_B1 ** ADAM_STEP
    c2 = 1.0 - ADAM_B2 ** ADAM_STEP

    def body(w_ref, g_ref, m_ref, v_ref, d_ref, nm_ref, nv_ref):
        gv = g_ref[...]
        mn = ADAM_B1 * m_ref[...] + (1.0 - ADAM_B1) * gv
        vn = ADAM_B2 * v_ref[...] + (1.0 - ADAM_B2) * jnp.square(gv)
        m_hat = mn / c1
        v_hat = vn / c2
        d_ref[...] = -ADAM_LR * (m_hat / (jnp.sqrt(v_hat) + ADAM_EPS) + ADAM_WD * w_ref[...])
        nm_ref[...] = mn
        nv_ref[...] = vn

    blk = pl.BlockSpec((tr, c), lambda i: (i, 0))
    s = jax.ShapeDtypeStruct((r, c), F32)
    d, nm, nv = pl.pallas_call(body, grid=(r // tr,), in_specs=[blk] * 4, out_specs=[blk] * 3, out_shape=[s, s, s],
                               compiler_params=_cp("parallel"), name=name)(w2, g2, m2, v2)
    return d.reshape(shape), nm.reshape(shape), nv.reshape(shape)


def _gla_parts(p_ref, wg_ref, bg_ref):
    q = p_ref[:, 0:GLA_KW] * (GLA_DK ** -0.5)
    k = p_ref[:, GLA_KW:2 * GLA_KW]
    glr = p_ref[:, 2 * GLA_KW + 2 * GLA_VW:GLA_IN_PAD]
    gate = _dg(glr, wg_ref[...]) + bg_ref[...]
    log_a = (jnp.minimum(gate, 0.0) - jnp.log(1.0 + jnp.exp(-jnp.abs(gate)))) * (1.0 / 16.0)
    tril = (_iota2((CHUNK, CHUNK), 0) >= _iota2((CHUNK, CHUNK), 1)).astype(F32)
    cum = _dot_sel(tril, log_a)
    c_last = cum[CHUNK - 1:CHUNK, :]
    f = jnp.exp(c_last - cum)
    dec = jnp.exp(c_last)
    return q, k, glr, gate, f, k * f, dec


def _gla_fwd(p, wg, bg, hg, *, name):
    t = p.shape[0]
    nc = t // CHUNK

    def body(p_ref, wg_ref, bg_ref, hg_ref, og_ref, s_ref, st):
        @pl.when(pl.program_id(0) == 0)
        def _():
            st[...] = jnp.zeros_like(st)

        q, _, _, _, _, ke, dec = _gla_parts(p_ref, wg_ref, bg_ref)
        hs = range(GLA_HEADS)
        ks = [slice(h * GLA_DK, (h + 1) * GLA_DK) for h in hs]
        vs = [slice(2 * GLA_KW + h * GLA_DV, 2 * GLA_KW + (h + 1) * GLA_DV) for h in hs]
        rs = [slice(2 * GLA_KW + GLA_VW + h * GLA_DV, 2 * GLA_KW + GLA_VW + (h + 1) * GLA_DV) for h in hs]
        s_new = [st[h] * dec[:, ks[h]] + _dg(p_ref[:, vs[h]], ke[:, ks[h]], TN) for h in hs]
        o = [_dg(q[:, ks[h]], s_new[h], NT) for h in hs]
        for h in hs:
            st[h] = s_new[h]
            s_ref[0, h] = s_new[h]
            rn = lax.rsqrt(jnp.mean(o[h] * o[h], axis=-1, keepdims=True) + EPS)
            rv = p_ref[:, rs[h]]
            og_ref[:, h * GLA_DV:(h + 1) * GLA_DV] = (((o[h] * rn) * hg_ref[...]) * (rv * _sigmoid(rv))).astype(BF16)

    full = lambda a: pl.BlockSpec(a.shape, lambda n: (0,) * a.ndim)
    return pl.pallas_call(
        body, grid=(nc,),
        in_specs=[pl.BlockSpec((CHUNK, GLA_IN_PAD), lambda n: (n, 0)), full(wg), full(bg), full(hg)],
        out_specs=[pl.BlockSpec((CHUNK, GLA_VW), lambda n: (n, 0)),
                   pl.BlockSpec((1, GLA_HEADS, GLA_DV, GLA_DK), lambda n: (n, 0, 0, 0))],
        out_shape=[jax.ShapeDtypeStruct((t, GLA_VW), BF16), jax.ShapeDtypeStruct((nc, GLA_HEADS, GLA_DV, GLA_DK), F32)],
        scratch_shapes=[pltpu.VMEM((GLA_HEADS, GLA_DV, GLA_DK), F32)],
        compiler_params=_cp("arbitrary"), name=name)(p, wg, bg, hg)


def _gla_bwd(p, dog, sall, wg, bg, hg, *, name):
    t = p.shape[0]
    nc = t // CHUNK

    def body(p_ref, dog_ref, s1_ref, s0_ref, wg_ref, bg_ref, hg_ref, dp_ref, dwg_ref, dbg_ref, dhg_ref, gt):
        i = pl.program_id(0)

        @pl.when(i == 0)
        def _():
            gt[...] = jnp.zeros_like(gt)
            dwg_ref[...] = jnp.zeros_like(dwg_ref)
            dbg_ref[...] = jnp.zeros_like(dbg_ref)
            dhg_ref[...] = jnp.zeros_like(dhg_ref)

        has_prev = (i < nc - 1).astype(F32)
        q, k, glr, gate, f, ke, dec = _gla_parts(p_ref, wg_ref, bg_ref)
        hgv = hg_ref[...]
        hs = range(GLA_HEADS)
        ks = [slice(h * GLA_DK, (h + 1) * GLA_DK) for h in hs]
        vs = [slice(2 * GLA_KW + h * GLA_DV, 2 * GLA_KW + (h + 1) * GLA_DV) for h in hs]
        rs = [slice(2 * GLA_KW + GLA_VW + h * GLA_DV, 2 * GLA_KW + GLA_VW + (h + 1) * GLA_DV) for h in hs]
        s1 = [s1_ref[0, h] for h in hs]
        o = [_dg(q[:, ks[h]], s1[h], NT) for h in hs]
        dhg = jnp.zeros((1, GLA_DV), F32)
        do = []
        for h in hs:
            rv = p_ref[:, rs[h]]
            rn = lax.rsqrt(jnp.mean(o[h] * o[h], axis=-1, keepdims=True) + EPS)
            z = o[h] * rn
            sg = _sigmoid(rv)
            sl = rv * sg
            dogh = dog_ref[:, h * GLA_DV:(h + 1) * GLA_DV].astype(F32)
            dhg = dhg + jnp.sum(dogh * z * sl, axis=0, keepdims=True)
            dp_ref[:, rs[h]] = (dogh * (z * hgv) * (sg * (1.0 + rv * (1.0 - sg)))).astype(BF16)
            dz = dogh * sl * hgv
            do.append(rn * (dz - z * jnp.mean(dz * z, axis=-1, keepdims=True)))
        dhg_ref[...] += dhg
        g_tot = [gt[h] + _dg(do[h], q[:, ks[h]], TN) for h in hs]
        dq = [_dg(do[h], s1[h], NN) for h in hs]
        dke_parts = [_dg(p_ref[:, vs[h]], g_tot[h], NN) for h in hs]
        dv = [_dg(ke[:, ks[h]], g_tot[h], NT) for h in hs]
        ddec_parts = []
        for h in hs:
            dp_ref[:, ks[h]] = (dq[h] * (GLA_DK ** -0.5)).astype(BF16)
            dp_ref[:, vs[h]] = dv[h].astype(BF16)
            ddec_parts.append(jnp.sum(g_tot[h] * (s0_ref[0, h] * has_prev), axis=0, keepdims=True))
            gt[h] = g_tot[h] * dec[:, ks[h]]
        dke = jnp.concatenate(dke_parts, axis=1)
        ddec = jnp.concatenate(ddec_parts, axis=1)
        dp_ref[:, GLA_KW:2 * GLA_KW] = (dke * f).astype(BF16)
        stril = (_iota2((CHUNK, CHUNK), 0) > _iota2((CHUNK, CHUNK), 1)).astype(F32)
        dlog_a = _dot_sel(stril, dke * ke) + ddec * dec
        dgate = dlog_a * (1.0 / 16.0) * _sigmoid(-gate)
        dp_ref[:, 2 * GLA_KW + 2 * GLA_VW:GLA_IN_PAD] = _dg(dgate, wg_ref[...], NT).astype(BF16)
        dwg_ref[...] += _dg(glr, dgate, TN)
        dbg_ref[...] += jnp.sum(dgate, axis=0, keepdims=True)

    full = lambda a: pl.BlockSpec(a.shape, lambda n: (0,) * a.ndim)
    sblk = (1, GLA_HEADS, GLA_DV, GLA_DK)
    return pl.pallas_call(
        body, grid=(nc,),
        in_specs=[pl.BlockSpec((CHUNK, GLA_IN_PAD), lambda n: (nc - 1 - n, 0)),
                  pl.BlockSpec((CHUNK, GLA_VW), lambda n: (nc - 1 - n, 0)),
                  pl.BlockSpec(sblk, lambda n: (nc - 1 - n, 0, 0, 0)),
                  pl.BlockSpec(sblk, lambda n: (jnp.maximum(nc - 2 - n, 0), 0, 0, 0)),
                  full(wg), full(bg), full(hg)],
        out_specs=[pl.BlockSpec((CHUNK, GLA_IN_PAD), lambda n: (nc - 1 - n, 0)), full(wg), full(bg), full(hg)],
        out_shape=[jax.ShapeDtypeStruct((t, GLA_IN_PAD), BF16), jax.ShapeDtypeStruct(wg.shape, F32),
                   jax.ShapeDtypeStruct(bg.shape, F32), jax.ShapeDtypeStruct(hg.shape, F32)],
        scratch_shapes=[pltpu.VMEM((GLA_HEADS, GLA_DV, GLA_DK), F32)],
        compiler_params=_cp("arbitrary"), name=name)(p, dog, sall, sall, wg, bg, hg)


def _gla_layer_fwd(h, w, tag):
    p = _mm(h, w["w_in"], name=tag + "_in")
    og, sall = _gla_fwd(p, w["wg"], w["bg"], w["hg"], name=tag + "_scan")
    y = _mm(og, w["w_out"], name=tag + "_out")
    return y, (p, og, sall)


def _gla_layer_bwd(dy, h, saved, w, tag):
    p, og, sall = saved
    dog = _mm(dy, w["w_out"], tb=True, name=tag + "_dog")
    dw_out = _mm(og, dy, ta=True, name=tag + "_dwout")
    dp, dwg, dbg, dhg = _gla_bwd(p, dog, sall, w["wg"], w["bg"], w["hg"], name=tag + "_scanb")
    dw_in = _mm(h, dp, ta=True, name=tag + "_dwin")
    dh = _mm(dp, w["w_in"], tb=True, name=tag + "_dh")
    grads = dict(w_in=dw_in[:, :GLA_IN], w_gate_up=dwg[:GLA_RANK], b_gate=dbg[0], head_g=dhg[0], w_out=dw_out)
    return dh, grads


def _rope_tables(pos, inv_freq, *, name):
    t = pos.shape[0]
    tr = _row_tile(t)
    half = MLA_ROPE // 2

    def body(p_ref, f_ref, c_ref, s1_ref, s2_ref, s1b_ref, s2b_ref):
        ang = p_ref[...].astype(F32) * f_ref[...]
        lane = _iota2((tr, LANES), 1)
        lo = (lane >= MLA_NOPE) & (lane < MLA_NOPE + half)
        hi = (lane >= MLA_NOPE + half) & (lane < MLA_QK)
        cs, sn = jnp.cos(ang), jnp.sin(ang)
        zero = jnp.zeros_like(cs)
        c_ref[...] = jnp.where(lane < MLA_NOPE, 1.0, jnp.where(lane < MLA_QK, cs, 0.0))
        s1_ref[...] = jnp.where(lo, -sn, zero)
        s2_ref[...] = jnp.where(hi, sn, zero)
        s1b_ref[...] = jnp.where(lo, sn, zero)
        s2b_ref[...] = jnp.where(hi, -sn, zero)

    row = pl.BlockSpec((tr, LANES), lambda i: (i, 0))
    s = jax.ShapeDtypeStruct((t, LANES), F32)
    return pl.pallas_call(body, grid=(t // tr,),
                          in_specs=[pl.BlockSpec((tr, 1), lambda i: (i, 0)), pl.BlockSpec((1, LANES), lambda i: (0, 0))],
                          out_specs=[row] * 5, out_shape=[s] * 5, compiler_params=_cp("parallel"), name=name)(pos, inv_freq)


def _rope(x, c, s1, s2, *, out_dtype, sum_heads=False, name):
    nh, t, _ = x.shape
    tr = _row_tile(t)
    half = MLA_ROPE // 2

    def body(x_ref, c_ref, s1_ref, s2_ref, o_ref):
        total = None
        for h in range(nh):
            xv = x_ref[h].astype(F32)
            y = xv * c_ref[...] + pltpu.roll(xv, LANES - half, 1) * s1_ref[...] + pltpu.roll(xv, half, 1) * s2_ref[...]
            if sum_heads:
                total = y if total is None else total + y
            else:
                o_ref[h] = y.astype(out_dtype)
        if sum_heads:
            o_ref[...] = total

    tab = pl.BlockSpec((tr, LANES), lambda i: (i, 0))
    xs = pl.BlockSpec((nh, tr, LANES), lambda i: (0, i, 0))
    if sum_heads:
        return pl.pallas_call(body, grid=(t // tr,), in_specs=[xs, tab, tab, tab], out_specs=tab,
                              out_shape=jax.ShapeDtypeStruct((t, LANES), F32),
                              compiler_params=_cp("parallel"), name=name)(x, c, s1, s2)
    return pl.pallas_call(body, grid=(t // tr,), in_specs=[xs, tab, tab, tab], out_specs=xs,
                          out_shape=jax.ShapeDtypeStruct(x.shape, out_dtype),
                          compiler_params=_cp("parallel"), name=name)(x, c, s1, s2)


FLASH_BLK = 512


def _diag_mask(blk):
    return (_iota2((blk, blk), 1) // CHUNK) <= (_iota2((blk, blk), 0) // CHUNK)


def _flash_fwd(q, k, v, *, name):
    nh, t, _ = q.shape
    blk = min(FLASH_BLK, t)
    scale = MLA_QK ** -0.5

    def body(q_ref, k_ref, v_ref, o_ref, lse_ref):
        i = pl.program_id(1)
        qv = q_ref[0]

        def step(j, carry, masked):
            m, l, acc = carry
            off = pl.multiple_of(j * blk, blk)
            kb = k_ref[0, pl.ds(off, blk), :]
            vb = v_ref[0, pl.ds(off, blk), :]
            s = _dg(qv, kb, NT) * scale
            if masked:
                s = jnp.where(_diag_mask(blk), s, NEG_INF)
            m_new = jnp.maximum(m, jnp.max(s, axis=-1, keepdims=True))
            p = jnp.exp(s - m_new)
            alpha = jnp.exp(m - m_new)
            return m_new, alpha * l + jnp.sum(p, axis=-1, keepdims=True), alpha * acc + _dg(p, vb, NN)

        init = (jnp.full((blk, 1), NEG_INF, F32), jnp.zeros((blk, 1), F32), jnp.zeros((blk, MLA_V), F32))
        carry = lax.fori_loop(0, i, lambda j, c: step(j, c, False), init)
        m, l, acc = step(i, carry, True)
        o_ref[0] = (acc / l).astype(BF16)
        lse_ref[0] = m + jnp.log(l)

    qs = pl.BlockSpec((1, blk, LANES), lambda h, i: (h, i, 0))
    return pl.pallas_call(
        body, grid=(nh, t // blk),
        in_specs=[qs, pl.BlockSpec((1, t, LANES), lambda h, i: (h, 0, 0)), pl.BlockSpec((1, t, MLA_V), lambda h, i: (h, 0, 0))],
        out_specs=[pl.BlockSpec((1, blk, MLA_V), lambda h, i: (h, i, 0)), pl.BlockSpec((1, blk, 1), lambda h, i: (h, i, 0))],
        out_shape=[jax.ShapeDtypeStruct((nh, t, MLA_V), BF16), jax.ShapeDtypeStruct((nh, t, 1), F32)],
        compiler_params=_cp("parallel", "parallel"), name=name)(q, k, v)


def _flash_dq(q, k, v, do, o, lse, *, name):
    nh, t, _ = q.shape
    blk = min(FLASH_BLK, t)
    scale = MLA_QK ** -0.5

    def body(q_ref, k_ref, v_ref, do_ref, o_ref, lse_ref, dq_ref, dl_ref):
        i = pl.program_id(1)
        qv = q_ref[0]
        dov = do_ref[0]
        lse_v = lse_ref[0]
        delta = jnp.sum(dov.astype(F32) * o_ref[0].astype(F32), axis=-1, keepdims=True)
        dl_ref[0] = delta

        def step(j, dq, masked):
            off = pl.multiple_of(j * blk, blk)
            kb = k_ref[0, pl.ds(off, blk), :]
            vb = v_ref[0, pl.ds(off, blk), :]
            s = _dg(qv, kb, NT) * scale
            if masked:
                s = jnp.where(_diag_mask(blk), s, NEG_INF)
            p = jnp.exp(s - lse_v)
            ds = p * (_dg(dov, vb, NT) - delta) * scale
            return dq + _dg(ds, kb, NN)

        dq = lax.fori_loop(0, i, lambda j, c: step(j, c, False), jnp.zeros((blk, LANES), F32))
        dq_ref[0] = step(i, dq, True)

    qs = pl.BlockSpec((1, blk, LANES), lambda h, i: (h, i, 0))
    vs = pl.BlockSpec((1, blk, MLA_V), lambda h, i: (h, i, 0))
    ls = pl.BlockSpec((1, blk, 1), lambda h, i: (h, i, 0))
    return pl.pallas_call(
        body, grid=(nh, t // blk),
        in_specs=[qs, pl.BlockSpec((1, t, LANES), lambda h, i: (h, 0, 0)), pl.BlockSpec((1, t, MLA_V), lambda h, i: (h, 0, 0)), vs, vs, ls],
        out_specs=[qs, ls],
        out_shape=[jax.ShapeDtypeStruct((nh, t, LANES), F32), jax.ShapeDtypeStruct((nh, t, 1), F32)],
        compiler_params=_cp("parallel", "parallel"), name=name)(q, k, v, do, o, lse)


def _flash_dkv(q, k, v, do, lse, delta, *, name):
    nh, t, _ = q.shape
    blk = min(FLASH_BLK, t)
    nq = t // blk
    scale = MLA_QK ** -0.5

    def body(q_ref, k_ref, v_ref, do_ref, lse_ref, dl_ref, dk_ref, dv_ref):
        j = pl.program_id(1)
        kb = k_ref[0]
        vb = v_ref[0]

        def step(i, carry, masked):
            dk, dv = carry
            off = pl.multiple_of(i * blk, blk)
            qb = q_ref[0, pl.ds(off, blk), :]
            dob = do_ref[0, pl.ds(off, blk), :]
            s = _dg(qb, kb, NT) * scale
            if masked:
                s = jnp.where(_diag_mask(blk), s, NEG_INF)
            p = jnp.exp(s - lse_ref[0, pl.ds(off, blk), :])
            ds = p * (_dg(dob, vb, NT) - dl_ref[0, pl.ds(off, blk), :]) * scale
            return dk + _dg(ds, qb, TN), dv + _dg(p, dob, TN)

        carry = step(j, (jnp.zeros((blk, LANES), F32), jnp.zeros((blk, MLA_V), F32)), True)
        dk, dv = lax.fori_loop(j + 1, nq, lambda i, c: step(i, c, False), carry)
        dk_ref[0] = dk
        dv_ref[0] = dv

    ks = pl.BlockSpec((1, blk, LANES), lambda h, j: (h, j, 0))
    vs = pl.BlockSpec((1, blk, MLA_V), lambda h, j: (h, j, 0))
    fl = lambda w: pl.BlockSpec((1, t, w), lambda h, j: (h, 0, 0))
    return pl.pallas_call(
        body, grid=(nh, nq),
        in_specs=[fl(LANES), ks, vs, fl(MLA_V), fl(1), fl(1)],
        out_specs=[ks, vs],
        out_shape=[jax.ShapeDtypeStruct((nh, t, LANES), F32), jax.ShapeDtypeStruct((nh, t, MLA_V), F32)],
        compiler_params=_cp("parallel", "parallel"), name=name)(q, k, v, do, lse, delta)


def _heads_first(a, width):
    t = a.shape[0]
    return a.reshape(t, MLA_HEADS, width).transpose(1, 0, 2)


def _heads_last(a):
    return a.transpose(1, 0, 2).reshape(a.shape[1], -1)


def _mla_layer_fwd(h, pos, w, tag):
    t = h.shape[0]
    zq = jnp.zeros((1, MLA_Q_RANK), F32)
    zkv = jnp.zeros((1, MLA_KV_RANK), F32)
    p = _mm(h, w["w_in"], name=tag + "_in")
    cq, ckv, krp = p[:, :MLA_Q_RANK], p[:, MLA_Q_RANK:MLA_Q_RANK + MLA_KV_RANK], p[:, MLA_Q_RANK + MLA_KV_RANK:MLA_IN]
    qn = _premod_fwd(cq, w["q_norm_g"], zq, zq, name=tag + "_qnorm")
    kvn = _premod_fwd(ckv, w["kv_norm_g"], zkv, zkv, name=tag + "_kvnorm")
    q = _mm(qn, w["w_uq"], name=tag + "_uq")
    kv = _mm(kvn, w["w_ukv"], name=tag + "_ukv")
    q_pre = jnp.pad(_heads_first(q, MLA_QK), ((0, 0), (0, 0), (0, LANES - MLA_QK)))
    kv3 = _heads_first(kv, MLA_NOPE + MLA_V)
    k_pre = jnp.concatenate([kv3[:, :, :MLA_NOPE], jnp.broadcast_to(krp[None], (MLA_HEADS, t, MLA_ROPE)),
                             jnp.zeros((MLA_HEADS, t, LANES - MLA_QK), F32)], axis=-1)
    vh = kv3[:, :, MLA_NOPE:].astype(BF16)
    half = MLA_ROPE // 2
    freq = ROPE_THETA ** (-jnp.arange(half, dtype=F32) / half)
    inv_freq = jnp.concatenate([jnp.zeros((MLA_NOPE,), F32), freq, freq, jnp.zeros((LANES - MLA_QK,), F32)])[None]
    tabs = _rope_tables(pos.reshape(t, 1), inv_freq, name=tag + "_tables")
    qr = _rope(q_pre, tabs[0], tabs[1], tabs[2], out_dtype=BF16, name=tag + "_ropeq")
    kr = _rope(k_pre, tabs[0], tabs[1], tabs[2], out_dtype=BF16, name=tag + "_ropek")
    o, lse = _flash_fwd(qr, kr, vh, name=tag + "_attn")
    of = _heads_last(o)
    y = _mm(of, w["w_out"], name=tag + "_out")
    return y, (cq, ckv, qn, kvn, qr, kr, vh, o, lse, of, tabs)


def _mla_layer_bwd(dy, h, saved, w, tag):
    cq, ckv, qn, kvn, qr, kr, vh, o, lse, of, tabs = saved
    t = h.shape[0]
    zq = jnp.zeros((1, MLA_Q_RANK), F32)
    zkv = jnp.zeros((1, MLA_KV_RANK), F32)
    dof = _mm(dy, w["w_out"], tb=True, out_dtype=BF16, name=tag + "_dof")
    dw_out = _mm(of, dy, ta=True, name=tag + "_dwout")
    do = _heads_first(dof, MLA_V)
    dqr, delta = _flash_dq(qr, kr, vh, do, o, lse, name=tag + "_attn_dq")
    dkr, dv = _flash_dkv(qr, kr, vh, do, lse, delta, name=tag + "_attn_dkv")
    dq_pre = _rope(dqr, tabs[0], tabs[3], tabs[4], out_dtype=F32, name=tag + "_ropeq_b")
    dk_sum = _rope(dkr, tabs[0], tabs[3], tabs[4], out_dtype=F32, sum_heads=True, name=tag + "_ropek_b")
    dq = _heads_last(dq_pre[:, :, :MLA_QK])
    dkv = _heads_last(jnp.concatenate([dkr[:, :, :MLA_NOPE], dv], axis=-1))
    dw_uq = _mm(qn, dq, ta=True, name=tag + "_dwuq")
    dqn = _mm(dq, w["w_uq"], tb=True, name=tag + "_dqn")
    dw_ukv = _mm(kvn, dkv, ta=True, name=tag + "_dwukv")
    dkvn = _mm(dkv, w["w_ukv"], tb=True, name=tag + "_dkvn")
    dcq, dqg, _, _ = _premod_bwd(dqn, cq, w["q_norm_g"], zq, jnp.zeros_like(cq), name=tag + "_qnorm_b")
    dckv, dkvg, _, _ = _premod_bwd(dkvn, ckv, w["kv_norm_g"], zkv, jnp.zeros_like(ckv), name=tag + "_kvnorm_b")
    dp = jnp.concatenate([dcq, dckv, dk_sum[:, MLA_NOPE:MLA_QK], jnp.zeros((t, MLA_IN_PAD - MLA_IN), F32)], axis=1).astype(BF16)
    dw_in = _mm(h, dp, ta=True, name=tag + "_dwin")
    dh = _mm(dp, w["w_in"], tb=True, name=tag + "_dh")
    grads = dict(w_in=dw_in[:, :MLA_IN], q_norm_g=dqg[0], w_uq=dw_uq, kv_norm_g=dkvg[0], w_ukv=dw_ukv, w_out=dw_out)
    return dh, grads


CONV_HALO = 8


def _conv_tiles(t):
    return min(512, t), 512


def _gdn_conv_fwd(p, w, *, name):
    t = p.shape[0]
    tr, tc = _conv_tiles(t)
    hb = tr // CONV_HALO

    def body(x_ref, halo_ref, w_ref, o_ref, buf):
        i = pl.program_id(0)
        buf[0:CONV_HALO, :] = halo_ref[...] * (i > 0).astype(F32)
        buf[CONV_HALO:CONV_HALO + tr, :] = x_ref[...]
        base = CONV_HALO - (GDN_CONV - 1)
        acc = buf[pl.ds(base, tr), :] * w_ref[0:1, :]
        for j in range(1, GDN_CONV):
            acc = acc + buf[pl.ds(base + j, tr), :] * w_ref[j:j + 1, :]
        o_ref[...] = acc * _sigmoid(acc)

    return pl.pallas_call(
        body, grid=(t // tr, GDN_CONV_W // tc),
        in_specs=[pl.BlockSpec((tr, tc), lambda i, j: (i, j)),
                  pl.BlockSpec((CONV_HALO, tc), lambda i, j: (jnp.maximum(i * hb - 1, 0), j)),
                  pl.BlockSpec((GDN_CONV, tc), lambda i, j: (0, j))],
        out_specs=pl.BlockSpec((tr, tc), lambda i, j: (i, j)),
        out_shape=jax.ShapeDtypeStruct((t, GDN_CONV_W), F32),
        scratch_shapes=[pltpu.VMEM((tr + CONV_HALO, tc), F32)],
        compiler_params=_cp("parallel", "parallel"), name=name)(p, p, w)


def _gdn_conv_bwd(d, p, w, col0, *, name):
    t = p.shape[0]
    tr, tc = _conv_tiles(t)
    hb = tr // CONV_HALO
    nr = t // tr
    ext = tr + CONV_HALO

    def body(x_ref, xp_ref, xn_ref, d_ref, dn_ref, w_ref, dx_ref, dw_ref, bufx, bufd):
        i = pl.program_id(1)

        @pl.when(i == 0)
        def _():
            dw_ref[...] = jnp.zeros_like(dw_ref)

        last = (i < nr - 1).astype(F32)
        bufx[0:CONV_HALO, :] = xp_ref[...] * (i > 0).astype(F32)
        bufx[CONV_HALO:CONV_HALO + tr, :] = x_ref[...]
        bufx[CONV_HALO + tr:, :] = xn_ref[...] * last
        base = CONV_HALO - (GDN_CONV - 1)
        acc = bufx[pl.ds(base, ext), :] * w_ref[0:1, :]
        for j in range(1, GDN_CONV):
            acc = acc + bufx[pl.ds(base + j, ext), :] * w_ref[j:j + 1, :]
        sg = _sigmoid(acc)
        dsilu = sg * (1.0 + acc * (1.0 - sg))
        bufd[0:tr, :] = d_ref[...] * dsilu[0:tr, :]
        bufd[tr:, :] = dn_ref[...] * last * dsilu[tr:, :]
        dx = bufd[pl.ds(GDN_CONV - 1, tr), :] * w_ref[0:1, :]
        for j in range(1, GDN_CONV):
            dx = dx + bufd[pl.ds(GDN_CONV - 1 - j, tr), :] * w_ref[j:j + 1, :]
        dx_ref[...] = dx.astype(BF16)
        dc = bufd[0:tr, :]
        for j in range(GDN_CONV):
            dw_ref[j:j + 1, :] += jnp.sum(dc * bufx[pl.ds(base + j, tr), :], axis=0, keepdims=True)

    cb = col0 // tc
    width = d.shape[1]
    main = lambda off: pl.BlockSpec((tr, tc), lambda j, i: (i, j + off))
    prev = pl.BlockSpec((CONV_HALO, tc), lambda j, i: (jnp.maximum(i * hb - 1, 0), j + cb))
    nxt = lambda off: pl.BlockSpec((CONV_HALO, tc), lambda j, i: (jnp.minimum((i + 1) * hb, t // CONV_HALO - 1), j + off))
    wsp = lambda off: pl.BlockSpec((GDN_CONV, tc), lambda j, i: (0, j + off))
    return pl.pallas_call(
        body, grid=(width // tc, nr),
        in_specs=[main(cb), prev, nxt(cb), main(0), nxt(0), wsp(cb)], out_specs=[main(0), wsp(0)],
        out_shape=[jax.ShapeDtypeStruct((t, width), BF16), jax.ShapeDtypeStruct((GDN_CONV, width), F32)],
        scratch_shapes=[pltpu.VMEM((tr + 2 * CONV_HALO, tc), F32), pltpu.VMEM((ext, tc), F32)],
        compiler_params=_cp("parallel", "arbitrary"), name=name)(p, p, p, d, d, w)


def _dot_sel(sel, b, dims=NN, sel_first=True):
    s = sel.astype(BF16)
    b1 = b.astype(BF16)
    r1 = b - b1.astype(F32)
    b2 = r1.astype(BF16)
    b3 = (r1 - b2.astype(F32)).astype(BF16)
    if sel_first:
        d = lambda v: lax.dot_general(s, v, (dims, ((), ())), preferred_element_type=F32)
    else:
        d = lambda v: lax.dot_general(v, s, (dims, ((), ())), preferred_element_type=F32)
    return d(b1) + (d(b2) + d(b3))


def _gdn_chunks(qraws, kraws, vs, braws, araws, alogs, dtbs):
    nv = len(vs)
    row = _iota2((CHUNK, CHUNK), 0)
    col = _iota2((CHUNK, CHUNK), 1)
    strict = row > col
    triu = (row <= col).astype(F32)
    tril = (row >= col).astype(F32)
    ones = jnp.ones((CHUNK, CHUNK), F32)
    keys = []
    for qraw, kraw in zip(qraws, kraws):
        rq = lax.rsqrt(jnp.sum(qraw * qraw, axis=-1, keepdims=True) + EPS)
        rk = lax.rsqrt(jnp.sum(kraw * kraw, axis=-1, keepdims=True) + EPS)
        qn = qraw * rq
        keys.append(dict(rq=rq, rk=rk, qn=qn, qh=qn * (GDN_DK ** -0.5), kh=kraw * rk))
    kks = [_dg(kd["kh"], kd["kh"], NT) for kd in keys]
    cs = []
    for h in range(nv):
        c = dict(keys[h // 2])
        c.update(v=vs[h], kk=kks[h // 2], strict=strict, triu=triu)
        c["beta"] = _sigmoid(braws[h])
        c["ea"] = jnp.exp(alogs[h])
        c["xs"] = araws[h] + dtbs[h]
        c["g"] = -c["ea"] * _softplus(c["xs"])
        cs.append(c)
    gbs = [jnp.broadcast_to(c["g"], (CHUNK, LANES)) for c in cs]
    cums = [_dot_sel(tril, gb) for gb in gbs]
    cum_js = [_dot_sel(ones, gb[:, :CHUNK] * triu) for gb in gbs]
    for c, cum, cum_j in zip(cs, cums, cum_js):
        diff = jnp.where(strict, cum[:, :CHUNK] - cum_j, 0.0)
        c["dm"] = jnp.where(strict, jnp.exp(diff), 0.0)
        c["a"] = (c["beta"] * c["dm"]) * c["kk"]
        c_last = cum[CHUNK - 1:CHUNK, :]
        c["e"] = jnp.exp(cum)
        c["f"] = jnp.exp(c_last - cum)
        c["dec"] = jnp.exp(c_last)
        c["rv"] = c["beta"] * c["v"]
        c["rk_rhs"] = (c["beta"] * c["e"]) * c["kh"]
        c["ke"] = c["kh"] * c["f"]
    return cs


def _unit_lower_inverses(mats):
    eye = (_iota2((CHUNK, CHUNK), 0) == _iota2((CHUNK, CHUNK), 1)).astype(F32)
    ts = [eye - a for a in mats]
    pws = list(mats)
    for _ in range(5):
        pws = [_dot3(pw, pw) for pw in pws]
        ts = [t + _dot3(t, pw) for t, pw in zip(ts, pws)]
    return ts


GDN_HB = 8


def _gdn_specs(chunk_of):
    hb = GDN_HB
    kw = hb // 2 * GDN_DK
    vw = hb * GDN_DV
    qs = pl.BlockSpec((CHUNK, kw), lambda g, n: (chunk_of(n), g))
    ks = pl.BlockSpec((CHUNK, kw), lambda g, n: (chunk_of(n), GDN_QKW // kw + g))
    vs = pl.BlockSpec((CHUNK, vw), lambda g, n: (chunk_of(n), 2 * GDN_QKW // vw + g))
    zs = pl.BlockSpec((CHUNK, vw), lambda g, n: (chunk_of(n), GDN_CONV_W // vw + g))
    col = pl.BlockSpec((hb, CHUNK, 1), lambda g, n: (g, chunk_of(n), 0))
    one = pl.BlockSpec((hb, 1, 1), lambda g, n: (g, 0, 0))
    ng = pl.BlockSpec((1, GDN_DV), lambda g, n: (0, 0))
    hd = pl.BlockSpec((CHUNK, vw), lambda g, n: (chunk_of(n), g))
    return qs, ks, vs, zs, col, one, ng, hd


def _gdn_fwd(qkv, p, braw, araw, alog, dtb, ng, *, name):
    t = qkv.shape[0]
    nc = t // CHUNK
    nh = GDN_V_HEADS

    def body(q_ref, k_ref, v_ref, z_ref, braw_ref, araw_ref, alog_ref, dtb_ref, ng_ref, og_ref, s_ref, t_ref, st):
        @pl.when(pl.program_id(1) == 0)
        def _():
            st[...] = jnp.zeros_like(st)

        hs = range(GDN_HB)
        kqs = [slice(j * GDN_DK, (j + 1) * GDN_DK) for j in range(GDN_HB // 2)]
        vsl = [slice(h * GDN_DV, (h + 1) * GDN_DV) for h in hs]
        cs = _gdn_chunks([q_ref[:, s] for s in kqs], [k_ref[:, s] for s in kqs], [v_ref[:, s] for s in vsl],
                         [braw_ref[h] for h in hs], [araw_ref[h] for h in hs], [alog_ref[h] for h in hs], [dtb_ref[h] for h in hs])
        tms = _unit_lower_inverses([c["a"] for c in cs])
        s0 = [st[h] for h in hs]
        wv = [_dot3(tms[h], cs[h]["rv"]) for h in hs]
        wk = [_dot3(tms[h], cs[h]["rk_rhs"]) for h in hs]
        u = [wv[h] - _dg(wk[h], s0[h], NN) for h in hs]
        s1 = [cs[h]["dec"] * s0[h] + _dg(cs[h]["ke"], u[h], TN) for h in hs]
        o = [_dg(cs[h]["qh"], s1[h], NN) for h in hs]
        for h in hs:
            t_ref[h, 0] = tms[h]
            st[h] = s1[h]
            s_ref[h, 0] = s1[h]
            rn = lax.rsqrt(jnp.mean(o[h] * o[h], axis=-1, keepdims=True) + EPS)
            zv = z_ref[:, vsl[h]]
            og_ref[:, vsl[h]] = (((o[h] * rn) * ng_ref[...]) * (zv * _sigmoid(zv))).astype(BF16)

    qs, ks, vs, zs, col, one, ngs, hd = _gdn_specs(lambda n: n)
    return pl.pallas_call(
        body, grid=(nh // GDN_HB, nc),
        in_specs=[qs, ks, vs, zs, col, col, one, one, ngs],
        out_specs=[hd,
                   pl.BlockSpec((GDN_HB, 1, GDN_DK, GDN_DV), lambda g, n: (g, n, 0, 0)),
                   pl.BlockSpec((GDN_HB, 1, CHUNK, CHUNK), lambda g, n: (g, n, 0, 0))],
        out_shape=[jax.ShapeDtypeStruct((t, GDN_VW), BF16), jax.ShapeDtypeStruct((nh, nc, GDN_DK, GDN_DV), F32),
                   jax.ShapeDtypeStruct((nh, nc, CHUNK, CHUNK), F32)],
        scratch_shapes=[pltpu.VMEM((GDN_HB, GDN_DK, GDN_DV), F32)],
        compiler_params=_cp("parallel", "arbitrary"), name=name)(qkv, qkv, qkv, p, braw, araw, alog, dtb, ng)


def _gdn_bwd(qkv, p, braw, araw, alog, dtb, ng, dog, sall, tall, *, name):
    t = qkv.shape[0]
    nc = t // CHUNK
    nh = GDN_V_HEADS

    def body(q_ref, k_ref, v_ref, z_ref, braw_ref, araw_ref, alog_ref, dtb_ref, ng_ref, dog_ref, s1_ref, s0_ref, t_ref,
             dq_ref, dk_ref, dv_ref, dz_ref, dbraw_ref, daraw_ref, dalog_ref, ddtb_ref, dng_ref, gc):
        grp = pl.program_id(0)
        i = pl.program_id(1)

        @pl.when(i == 0)
        def _():
            gc[...] = jnp.zeros_like(gc)
            dalog_ref[...] = jnp.zeros_like(dalog_ref)
            ddtb_ref[...] = jnp.zeros_like(ddtb_ref)

        @pl.when((i == 0) & (grp == 0))
        def _():
            dng_ref[...] = jnp.zeros_like(dng_ref)

        has_prev = (i < nc - 1).astype(F32)
        ngv = ng_ref[...]
        ones = jnp.ones((CHUNK, LANES), F32)
        hs = range(GDN_HB)
        kqs = [slice(j * GDN_DK, (j + 1) * GDN_DK) for j in range(GDN_HB // 2)]
        vsl = [slice(h * GDN_DV, (h + 1) * GDN_DV) for h in hs]
        cs = _gdn_chunks([q_ref[:, s] for s in kqs], [k_ref[:, s] for s in kqs], [v_ref[:, s] for s in vsl],
                         [braw_ref[h] for h in hs], [araw_ref[h] for h in hs], [alog_ref[h] for h in hs], [dtb_ref[h] for h in hs])
        tms = [t_ref[h, 0] for h in hs]
        s1 = [s1_ref[h, 0] for h in hs]
        s0 = [s0_ref[h, 0] * has_prev for h in hs]
        wv = [_dot3(tms[h], cs[h]["rv"]) for h in hs]
        wk = [_dot3(tms[h], cs[h]["rk_rhs"]) for h in hs]
        u = [wv[h] - _dg(wk[h], s0[h], NN) for h in hs]
        o = [_dg(cs[h]["qh"], s1[h], NN) for h in hs]
        dng = jnp.zeros((1, GDN_DV), F32)
        do = []
        for h in hs:
            zv = z_ref[:, vsl[h]]
            dogv = dog_ref[:, vsl[h]]
            rn = lax.rsqrt(jnp.mean(o[h] * o[h], axis=-1, keepdims=True) + EPS)
            zo = o[h] * rn
            sg = _sigmoid(zv)
            sl = zv * sg
            dng = dng + jnp.sum(dogv * zo * sl, axis=0, keepdims=True)
            dz_ref[:, vsl[h]] = (dogv * (zo * ngv) * (sg * (1.0 + zv * (1.0 - sg)))).astype(BF16)
            dzo = dogv * sl * ngv
            do.append(rn * (dzo - zo * jnp.mean(dzo * zo, axis=-1, keepdims=True)))
        dng_ref[...] += dng
        g_tot = [gc[h] + _dg(cs[h]["qh"], do[h], TN) for h in hs]
        dqh = [_dg(do[h], s1[h], NT) for h in hs]
        dke = [_dg(u[h], g_tot[h], NT) for h in hs]
        du = [_dg(cs[h]["ke"], g_tot[h], NN) for h in hs]
        gnew = [cs[h]["dec"] * g_tot[h] - _dg(wk[h], du[h], TN) for h in hs]
        dwk = [-_dg(du[h], s0[h], NT) for h in hs]
        drv = [_dot3(tms[h], du[h], TN) for h in hs]
        drk = [_dot3(tms[h], dwk[h], TN) for h in hs]
        da = [jnp.where(cs[h]["strict"], -(_dot3(drv[h], wv[h], NT) + _dot3(drk[h], wk[h], NT)), 0.0) for h in hs]
        mx = [da[h] * cs[h]["dm"] * cs[h]["kk"] for h in hs]
        aa = [mx[h] * cs[h]["beta"] for h in hs]
        colsum = [_dot_sel(ones, aa[h], TN, sel_first=False)[:, 0:1] for h in hs]
        bm = [(da[h] * cs[h]["beta"]) * cs[h]["dm"] for h in hs]
        dkh = [_dg(bm[h], cs[h]["kh"], NN) + _dg(bm[h], cs[h]["kh"], TN) for h in hs]
        dcum, dcl, dbeta = [], [], []
        for h in hs:
            c = cs[h]
            beta, kh, e, f, dec, ke = c["beta"], c["kh"], c["e"], c["f"], c["dec"], c["ke"]
            gc[h] = gnew[h]
            ddec = jnp.sum(jnp.sum(g_tot[h] * s0[h], axis=1, keepdims=True), axis=0, keepdims=True)
            dv_ref[:, vsl[h]] = beta * drv[h]
            db = jnp.sum(mx[h], axis=1, keepdims=True) + jnp.sum(drv[h] * c["v"], axis=1, keepdims=True)
            dbeta.append(db + jnp.sum(drk[h] * (e * kh), axis=1, keepdims=True))
            dkh[h] = dkh[h] + (beta * e) * drk[h] + f * dke[h]
            ef = jnp.sum(dke[h] * ke, axis=1, keepdims=True)
            dcum.append(jnp.sum(aa[h], axis=1, keepdims=True) - colsum[h] + jnp.sum(drk[h] * c["rk_rhs"], axis=1, keepdims=True) - ef)
            dcl.append(jnp.sum(ef, axis=0, keepdims=True) + ddec * dec[:, 0:1])
        dg = [_dot_sel(cs[h]["triu"], jnp.broadcast_to(dcum[h], (CHUNK, LANES)))[:, 0:1] + dcl[h] for h in hs]
        for h in hs:
            c = cs[h]
            beta, kh = c["beta"], c["kh"]
            daraw = dg[h] * (-c["ea"]) * _sigmoid(c["xs"])
            daraw_ref[h] = daraw
            dbraw_ref[h] = dbeta[h] * beta * (1.0 - beta)
            dalog_ref[h] += jnp.sum(dg[h] * c["g"], axis=0, keepdims=True)
            ddtb_ref[h] += jnp.sum(daraw, axis=0, keepdims=True)
        for j, sl in enumerate(kqs):
            c = cs[2 * j]
            dn = (dqh[2 * j] + dqh[2 * j + 1]) * (GDN_DK ** -0.5)
            dks = dkh[2 * j] + dkh[2 * j + 1]
            dq_ref[:, sl] = c["rq"] * (dn - c["qn"] * jnp.sum(dn * c["qn"], axis=-1, keepdims=True))
            dk_ref[:, sl] = c["rk"] * (dks - c["kh"] * jnp.sum(dks * c["kh"], axis=-1, keepdims=True))

    rev = lambda n: nc - 1 - n
    qs, ks, vs, zs, col, one, ngs, hd = _gdn_specs(rev)
    s1s = pl.BlockSpec((GDN_HB, 1, GDN_DK, GDN_DV), lambda g, n: (g, rev(n), 0, 0))
    s0s = pl.BlockSpec((GDN_HB, 1, GDN_DK, GDN_DV), lambda g, n: (g, jnp.maximum(rev(n) - 1, 0), 0, 0))
    ts = pl.BlockSpec((GDN_HB, 1, CHUNK, CHUNK), lambda g, n: (g, rev(n), 0, 0))
    big = jax.ShapeDtypeStruct((t, GDN_VW), F32)
    keyw = jax.ShapeDtypeStruct((t, GDN_QKW), F32)
    cols = jax.ShapeDtypeStruct((nh, t, 1), F32)
    ones_s = jax.ShapeDtypeStruct((nh, 1, 1), F32)
    return pl.pallas_call(
        body, grid=(nh // GDN_HB, nc),
        in_specs=[qs, ks, vs, zs, col, col, one, one, ngs, hd, s1s, s0s, ts],
        out_specs=[qs, qs, hd, hd, col, col, one, one, ngs],
        out_shape=[keyw, keyw, big, jax.ShapeDtypeStruct((t, GDN_VW), BF16), cols, cols, ones_s, ones_s,
                   jax.ShapeDtypeStruct((1, GDN_DV), F32)],
        scratch_shapes=[pltpu.VMEM((GDN_HB, GDN_DK, GDN_DV), F32)],
        compiler_params=_cp("arbitrary", "arbitrary"), name=name)(qkv, qkv, qkv, p, braw, araw, alog, dtb, ng, dog, sall, sall, tall)


def _gdn_layer_fwd(h, w, tag):
    p = _mm(h, w["w_in"], name=tag + "_in")
    qkv = _gdn_conv_fwd(p, w["conv_w"], name=tag + "_conv")
    braw = p[:, GDN_CONV_W + GDN_VW:GDN_CONV_W + GDN_VW + GDN_V_HEADS].T[:, :, None]
    araw = p[:, GDN_CONV_W + GDN_VW + GDN_V_HEADS:GDN_IN].T[:, :, None]
    og, sall, tall = _gdn_fwd(qkv, p, braw, araw, w["a_log"], w["dt_bias"], w["norm_g"], name=tag + "_scan")
    y = _mm(og, w["w_out"], name=tag + "_out")
    return y, (p, qkv, braw, araw, og, sall, tall)


def _gdn_layer_bwd(dy, h, saved, w, tag):
    p, qkv, braw, araw, og, sall, tall = saved
    t = h.shape[0]
    dog = _mm(dy, w["w_out"], tb=True, name=tag + "_dog")
    dw_out = _mm(og, dy, ta=True, name=tag + "_dwout")
    dq, dk, dv, dz, dbraw, daraw, dalog, ddtb, dng = _gdn_bwd(
        qkv, p, braw, araw, w["a_log"], w["dt_bias"], w["norm_g"], dog, sall, tall, name=tag + "_scanb")
    dpre_q, dcw_q = _gdn_conv_bwd(dq, p, w["conv_w"], 0, name=tag + "_convb_q")
    dpre_k, dcw_k = _gdn_conv_bwd(dk, p, w["conv_w"], GDN_QKW, name=tag + "_convb_k")
    dpre_v, dcw_v = _gdn_conv_bwd(dv, p, w["conv_w"], 2 * GDN_QKW, name=tag + "_convb_v")
    dconv_w = jnp.concatenate([dcw_q, dcw_k, dcw_v], axis=1)
    dp = jnp.concatenate([dpre_q, dpre_k, dpre_v, dz, dbraw[:, :, 0].T.astype(BF16), daraw[:, :, 0].T.astype(BF16),
                          jnp.zeros((t, GDN_IN_PAD - GDN_IN), BF16)], axis=1)
    dw_in = _mm(h, dp, ta=True, name=tag + "_dwin")
    dh = _mm(dp, w["w_in"], tb=True, name=tag + "_dh")
    grads = dict(w_in=dw_in[:, :GDN_IN], conv_w=dconv_w, a_log=dalog[:, 0, 0], dt_bias=ddtb[:, 0, 0], norm_g=dng[0], w_out=dw_out)
    return dh, grads


MESH_ID = pl.DeviceIdType.MESH
FLAT_W = 1024
FLAT_ROWS = 13056
FLAT_TILE = 384


def _exchange(name, ins, out_shapes, plan, n_remote, n_local):
    def body(*refs):
        in_refs = refs[:len(ins)]
        out_refs = refs[len(ins):len(ins) + len(out_shapes)]
        ssem, rsem, lsem = refs[len(ins) + len(out_shapes):]
        x, y, c = lax.axis_index("x"), lax.axis_index("y"), lax.axis_index("c")
        stages, local_copies = plan(x, y, c, in_refs, out_refs)
        assert sum(len(s) for s in stages) == n_remote and len(local_copies) == n_local
        locs = [pltpu.make_async_copy(s, d, lsem.at[i]) for i, (s, d) in enumerate(local_copies)]
        for cp in locs:
            cp.start()
        sent = []
        k = 0
        for stage in stages:
            arrivals = []
            for src, dst, peer, landing in stage:
                cp = pltpu.make_async_remote_copy(src_ref=src, dst_ref=dst, send_sem=ssem.at[k], recv_sem=rsem.at[k],
                                                  device_id=peer, device_id_type=MESH_ID)
                cp.start()
                sent.append(cp)
                arrivals.append(pltpu.make_async_remote_copy(src_ref=src, dst_ref=landing, send_sem=ssem.at[k],
                                                             recv_sem=rsem.at[k], device_id=peer, device_id_type=MESH_ID))
                k += 1
            for cp in arrivals:
                cp.wait_recv()
        for cp in sent:
            cp.wait_send()
        for cp in locs:
            cp.wait()

    hbm = pl.BlockSpec(memory_space=pl.ANY)
    return pl.pallas_call(
        body, in_specs=[hbm] * len(ins), out_specs=[hbm] * len(out_shapes), out_shape=out_shapes,
        scratch_shapes=[pltpu.SemaphoreType.DMA((n_remote,)), pltpu.SemaphoreType.DMA((n_remote,)),
                        pltpu.SemaphoreType.DMA((max(n_local, 1),))],
        name=name)(*ins)


def _other_chips(x, y):
    return [(1 - x, y), (x, 1 - y), (1 - x, 1 - y)]


def _all8_gather(a, *, name):
    def plan(x, y, c, ins, outs):
        (src,), (dst,) = ins, outs
        me = 4 * x + 2 * y + c
        stage = []
        for fx, fy, fc in [(0, 0, 1), (0, 1, 0), (0, 1, 1), (1, 0, 0), (1, 0, 1), (1, 1, 0), (1, 1, 1)]:
            px, py, pc = (1 - x if fx else x), (1 - y if fy else y), (1 - c if fc else c)
            stage.append((src, dst.at[me], (px, py, pc), dst.at[4 * px + 2 * py + pc]))
        return [stage], [(src, dst.at[me])]

    return _exchange(name, [a], [jax.ShapeDtypeStruct((8,) + a.shape, a.dtype)], plan, 7, 1)[0]


def _chip_gather(flat, *, name):
    rows = flat.shape[0]
    half = rows // 2

    def plan(x, y, c, ins, outs):
        (src,), (dst,) = ins, outs
        me = 2 * x + y
        mine = pl.ds(c * half, half)
        theirs = pl.ds((1 - c) * half, half)
        ici = [(src.at[mine], dst.at[me, mine], (px, py, c), dst.at[2 * px + py, mine]) for px, py in _other_chips(x, y)]
        d2d = [(dst.at[2 * px + py, mine], dst.at[2 * px + py, mine], (x, y, 1 - c), dst.at[2 * px + py, theirs])
               for px, py in _other_chips(x, y)]
        return [ici, d2d], []

    return _exchange(name, [flat], [jax.ShapeDtypeStruct((4,) + flat.shape, flat.dtype)], plan, 6, 0)[0]


def _add_sibling(gf, buf_a, core, *, name):
    _, rows, w = gf.shape
    half = rows // 2
    nb = half // FLAT_TILE

    def body(c_ref, g_ref, a_ref, o_ref):
        o_ref[...] = (g_ref[...] + a_ref[...]).astype(BF16)

    blk = (1, FLAT_TILE, w)
    return pl.pallas_call(
        body,
        grid_spec=pltpu.PrefetchScalarGridSpec(
            num_scalar_prefetch=1, grid=(4, nb),
            in_specs=[pl.BlockSpec(blk, lambda s, i, c_ref: (s, c_ref[0] * nb + i, 0)), pl.BlockSpec(blk, lambda s, i, c_ref: (s, i, 0))],
            out_specs=pl.BlockSpec(blk, lambda s, i, c_ref: (s, i, 0))),
        out_shape=jax.ShapeDtypeStruct((4, half, w), BF16), compiler_params=_cp("parallel", "parallel"), name=name)(core, gf, buf_a)


def _sum_chips(hsum, buf_b, chip, *, name):
    _, half, w = hsum.shape
    nb = half // FLAT_TILE

    def body(c_ref, h_ref, b0_ref, b1_ref, b2_ref, b3_ref, o_ref):
        me = c_ref[0]
        own = h_ref[0].astype(F32)
        acc = None
        for j, b_ref in enumerate((b0_ref, b1_ref, b2_ref, b3_ref)):
            term = jnp.where(me == j, own, b_ref[0].astype(F32))
            acc = term if acc is None else acc + term
        o_ref[...] = acc

    blk = (1, FLAT_TILE, w)

    def other(j):
        return pl.BlockSpec(blk, lambda i, c_ref: (jnp.where(c_ref[0] == j, (j + 1) % 4, j), i, 0))

    return pl.pallas_call(
        body,
        grid_spec=pltpu.PrefetchScalarGridSpec(
            num_scalar_prefetch=1, grid=(nb,),
            in_specs=[pl.BlockSpec(blk, lambda i, c_ref: (c_ref[0], i, 0))] + [other(j) for j in range(4)],
            out_specs=pl.BlockSpec((FLAT_TILE, w), lambda i, c_ref: (i, 0))),
        out_shape=jax.ShapeDtypeStruct((half, w), F32), compiler_params=_cp("parallel"), name=name)(chip, hsum, buf_b, buf_b, buf_b, buf_b)


def _sum_slots(buf, *, name):
    n, rows, w = buf.shape
    tr = _pick(rows, (FLAT_TILE, 8))

    def body(b_ref, o_ref):
        acc = b_ref[0]
        for s in range(1, n):
            acc = acc + b_ref[s]
        o_ref[...] = acc

    return pl.pallas_call(body, grid=(rows // tr,), in_specs=[pl.BlockSpec((n, tr, w), lambda i: (0, i, 0))],
                          out_specs=pl.BlockSpec((tr, w), lambda i: (i, 0)), out_shape=jax.ShapeDtypeStruct((rows, w), F32),
                          compiler_params=_cp("parallel"), name=name)(buf)


def _reduce_scatter(gf, core, chip, *, tag):
    _, rows, w = gf.shape
    half = rows // 2

    def plan_a(x, y, c, ins, outs):
        (src,), (dst,) = ins, outs
        return [[(src.at[:, pl.ds((1 - c) * half, half)], dst, (x, y, 1 - c), dst)]], []

    buf_a = _exchange(tag + "_sibling", [gf], [jax.ShapeDtypeStruct((4, half, w), F32)], plan_a, 1, 0)[0]
    hsum = _add_sibling(gf, buf_a, core, name=tag + "_add_sibling")

    def plan_b(x, y, c, ins, outs):
        (src,), (dst,) = ins, outs
        me = 2 * x + y
        stage = [(src.at[2 * px + py], dst.at[me], (px, py, c), dst.at[2 * px + py]) for px, py in _other_chips(x, y)]
        return [stage], []

    buf_b = _exchange(tag + "_chips", [hsum], [jax.ShapeDtypeStruct((4, half, w), BF16)], plan_b, 3, 0)[0]
    mine = _sum_chips(hsum, buf_b, chip, name=tag + "_sum_chips")

    def plan_c(x, y, c, ins, outs):
        (src,), (dst,) = ins, outs
        return [[(src, dst, (x, y, 1 - c), dst)]], []

    theirs = _exchange(tag + "_halves", [mine], [jax.ShapeDtypeStruct((half, w), F32)], plan_c, 1, 0)[0]
    first = core[0] == 0
    return jnp.concatenate([jnp.where(first, mine, theirs), jnp.where(first, theirs, mine)], axis=0)


WEIGHTS = ["ada_w", "ada_b", "norm_pre_g", "norm_post_g", "gla_w_in", "gla_w_gate_up", "gla_b_gate", "gla_head_g",
           "gla_w_out", "mla_w_in", "mla_q_norm_g", "mla_w_uq", "mla_kv_norm_g", "mla_w_ukv", "mla_w_out", "gdn_w_in",
           "gdn_conv_w", "gdn_a_log", "gdn_dt_bias", "gdn_norm_g", "gdn_w_out", "mlp_w_up", "mlp_w_down"]
PACK_BF16 = [("gla_w_in", 2), ("gla_w_out", 1), ("mla_w_in", 1), ("mla_w_uq", 2), ("mla_w_ukv", 2), ("mla_w_out", 1),
             ("gdn_w_in", 2), ("gdn_w_out", 1), ("mlp_w_up", 2), ("mlp_w_down", 1)]
PACK_F32 = [("norm_pre_g", 2), ("norm_post_g", 2), ("gla_w_gate_up", 2), ("gla_b_gate", 1), ("gla_head_g", 1), ("gdn_conv_w", 2)]
REPLICATED_SMALL = ["mla_q_norm_g", "mla_kv_norm_g", "gdn_a_log", "gdn_dt_bias", "gdn_norm_g"]
MIXERS = ["gla", "mla", "gdn"]


def _silu_rows(a, *, name):
    def body(a_ref, o_ref):
        v = a_ref[...]
        o_ref[...] = v * _sigmoid(v)

    return pl.pallas_call(body, out_shape=jax.ShapeDtypeStruct(a.shape, F32), name=name)(a)


SMALL_ROWS = 16


def _piece_rows(size, mult):
    assert size % FLAT_W == 0
    return -(-(size // FLAT_W) // mult) * mult


def _to_rows(a, lead, mult):
    n = math.prod(a.shape[len(lead):])
    r = a.reshape(lead + (n // FLAT_W, FLAT_W))
    extra = _piece_rows(n, mult) - n // FLAT_W
    return jnp.pad(r, [(0, 0)] * len(lead) + [(0, extra), (0, 0)]) if extra else r


def _small_to_rows(parts, lead):
    flat = jnp.concatenate([p.reshape(lead + (-1,)) for p in parts], axis=-1)
    pad = SMALL_ROWS * FLAT_W - flat.shape[-1]
    return jnp.pad(flat, [(0, 0)] * len(lead) + [(0, pad)]).reshape(lead + (SMALL_ROWS, FLAT_W))


def _small_from_rows(rows, shards, lead):
    flat = rows.reshape(lead + (-1,))
    out, off = {}, 0
    for n, _ in PACK_F32:
        out[n] = flat[..., off:off + shards[n].size].reshape(lead + shards[n].shape)
        off += shards[n].size
    return out


def _pack_weights(shards):
    parts = [_to_rows(shards[n].astype(BF16), (), 16) for n, _ in PACK_BF16]
    small = _small_to_rows([shards[n] for n, _ in PACK_F32], ())
    parts.append(lax.bitcast_convert_type(small, BF16).reshape(2 * SMALL_ROWS, FLAT_W))
    flat = jnp.concatenate(parts, axis=0)
    return jnp.pad(flat, ((0, FLAT_ROWS - flat.shape[0]), (0, 0)))


def _unpack_weights(gathered, shards, chip):
    full, off = {}, 0
    for n, ax in PACK_BF16:
        size = shards[n].size
        seg = gathered[:, off:off + size // FLAT_W].reshape((4,) + shards[n].shape)
        own = shards[n].astype(BF16)
        full[n] = jnp.concatenate([jnp.where(chip == j, own, seg[j]) for j in range(4)], axis=ax)
        off += _piece_rows(size, 16)
    small = lax.bitcast_convert_type(gathered[:, off:off + 2 * SMALL_ROWS].reshape(4, SMALL_ROWS, FLAT_W, 2), F32)
    for (n, ax), seg in zip(PACK_F32, _small_from_rows(small, shards, (4,)).values()):
        full[n] = jnp.concatenate([jnp.where(chip == j, shards[n], seg[j]) for j in range(4)], axis=ax)
    return full


def _pack_grads(grads):
    by_chip = lambda g, ax: jnp.stack(jnp.split(g.astype(F32), 4, axis=ax - 1))
    parts = []
    for n, ax in PACK_BF16:
        rows = sum(g.size for g in grads[n]) // (4 * FLAT_W)
        parts += [by_chip(g, ax).reshape(4, -1, FLAT_W) for g in grads[n]]
        if _piece_rows(rows * FLAT_W, 8) > rows:
            parts.append(jnp.zeros((4, _piece_rows(rows * FLAT_W, 8) - rows, FLAT_W), F32))
    parts.append(_small_to_rows([jnp.stack([by_chip(g, ax) for g in grads[n]], axis=1) for n, ax in PACK_F32], (4,)))
    used = sum(p.shape[1] for p in parts)
    parts.append(jnp.zeros((4, FLAT_ROWS - used, FLAT_W), F32))
    return jnp.concatenate(parts, axis=1)


def _unpack_grads(reduced, shards):
    out, off = {}, 0
    for n, _ in PACK_BF16:
        size = shards[n].size
        out[n] = reduced[off:off + size // FLAT_W].reshape(shards[n].shape)
        off += _piece_rows(size, 8)
    out.update(_small_from_rows(reduced[off:off + SMALL_ROWS], shards, ()))
    return out


def _mixer_weights(kind, j, full, rep):
    if kind == "gla":
        return dict(w_in=jnp.pad(full["gla_w_in"][j], ((0, 0), (0, GLA_IN_PAD - GLA_IN))),
                    wg=jnp.pad(full["gla_w_gate_up"][j], ((0, LANES - GLA_RANK), (0, 0))),
                    bg=full["gla_b_gate"][j][None], hg=full["gla_head_g"][j][None], w_out=full["gla_w_out"][j])
    if kind == "mla":
        return dict(w_in=jnp.pad(full["mla_w_in"][j], ((0, 0), (0, MLA_IN_PAD - MLA_IN))), q_norm_g=rep["mla_q_norm_g"][j][None],
                    w_uq=full["mla_w_uq"][j], kv_norm_g=rep["mla_kv_norm_g"][j][None], w_ukv=full["mla_w_ukv"][j],
                    w_out=full["mla_w_out"][j])
    return dict(w_in=jnp.pad(full["gdn_w_in"][j], ((0, 0), (0, GDN_IN_PAD - GDN_IN))), conv_w=full["gdn_conv_w"][j],
                a_log=rep["gdn_a_log"][j][:, None, None], dt_bias=rep["gdn_dt_bias"][j][:, None, None],
                norm_g=rep["gdn_norm_g"][j][None], w_out=full["gdn_w_out"][j])


def _layer_fwd(xin, mod, gains, kind, mw, w_up, w_down, pos, tag):
    sh_m, sc_m, gt_m, sh_f, sc_f, gt_f = mod
    pre0, pre1, post0, post1 = gains
    h = _premod_fwd(xin, pre0, sc_m, sh_m, name=tag + "_pre0")
    if kind == "gla":
        y, saved = _gla_layer_fwd(h, mw, tag + "_gla")
    elif kind == "mla":
        y, saved = _mla_layer_fwd(h, pos, mw, tag + "_mla")
    else:
        y, saved = _gdn_layer_fwd(h, mw, tag + "_gdn")
    x1 = _postres_fwd(xin, y, post0, gt_m, name=tag + "_post0")
    h2 = _premod_fwd(x1, pre1, sc_f, sh_f, name=tag + "_pre1")
    act = _mm(h2, w_up, out_dtype=BF16, epi="relu2", name=tag + "_up")
    y2 = _mm(act, w_down, name=tag + "_down")
    x2 = _postres_fwd(x1, y2, post1, gt_f, name=tag + "_post1")
    return x2, (xin, h, y, saved, x1, h2, act, y2)


def _layer_bwd(g2, kept, mod, gains, kind, mw, w_up, w_down, tag):
    xin, h, y, saved, x1, h2, act, y2 = kept
    sh_m, sc_m, gt_m, sh_f, sc_f, gt_f = mod
    pre0, pre1, post0, post1 = gains
    dy2, dpost1, dgt_f = _postres_bwd(g2, y2, post1, gt_f, name=tag + "_post1_b")
    du = _mm(dy2, w_down, tb=True, out_dtype=BF16, epi="dact", aux=act, name=tag + "_du")
    dw_down = _mm(act, dy2, ta=True, name=tag + "_dwdown")
    dw_up = _mm(h2, du, ta=True, name=tag + "_dwup")
    dh2 = _mm(du, w_up, tb=True, name=tag + "_dh2")
    g1, dpre1, dsc_f, dsh_f = _premod_bwd(dh2, x1, pre1, sc_f, g2, name=tag + "_pre1_b")
    dy, dpost0, dgt_m = _postres_bwd(g1, y, post0, gt_m, name=tag + "_post0_b")
    if kind == "gla":
        dh, mg = _gla_layer_bwd(dy, h, saved, mw, tag + "_gla")
    elif kind == "mla":
        dh, mg = _mla_layer_bwd(dy, h, saved, mw, tag + "_mla")
    else:
        dh, mg = _gdn_layer_bwd(dy, h, saved, mw, tag + "_gdn")
    g0, dpre0, dsc_m, dsh_m = _premod_bwd(dh, xin, pre0, sc_m, g1, name=tag + "_pre0_b")
    dmod = jnp.concatenate([dsh_m, dsc_m, dgt_m, dsh_f, dsc_f, dgt_f], axis=1)
    return g0, dmod, jnp.concatenate([dpre0, dpre1], axis=0), jnp.concatenate([dpost0, dpost1], axis=0), mg, dw_up, dw_down


def kernel(x, c, positions, ada_w, ada_b, norm_pre_g, norm_post_g, gla_w_in, gla_w_gate_up, gla_b_gate, gla_head_g, gla_w_out, mla_w_in, mla_q_norm_g, mla_w_uq, mla_kv_norm_g, mla_w_ukv, mla_w_out, gdn_w_in, gdn_conv_w, gdn_a_log, gdn_dt_bias, gdn_norm_g, gdn_w_out, mlp_w_up, mlp_w_down, loss_target, m_ada_w, m_ada_b, m_norm_pre_g, m_norm_post_g, m_gla_w_in, m_gla_w_gate_up, m_gla_b_gate, m_gla_head_g, m_gla_w_out, m_mla_w_in, m_mla_q_norm_g, m_mla_w_uq, m_mla_kv_norm_g, m_mla_w_ukv, m_mla_w_out, m_gdn_w_in, m_gdn_conv_w, m_gdn_a_log, m_gdn_dt_bias, m_gdn_norm_g, m_gdn_w_out, m_mlp_w_up, m_mlp_w_down, v_ada_w, v_ada_b, v_norm_pre_g, v_norm_post_g, v_gla_w_in, v_gla_w_gate_up, v_gla_b_gate, v_gla_head_g, v_gla_w_out, v_mla_w_in, v_mla_q_norm_g, v_mla_w_uq, v_mla_kv_norm_g, v_mla_w_ukv, v_mla_w_out, v_gdn_w_in, v_gdn_conv_w, v_gdn_a_log, v_gdn_dt_bias, v_gdn_norm_g, v_gdn_w_out, v_mlp_w_up, v_mlp_w_down):
    w = dict(ada_w=ada_w, ada_b=ada_b, norm_pre_g=norm_pre_g, norm_post_g=norm_post_g, gla_w_in=gla_w_in,
             gla_w_gate_up=gla_w_gate_up, gla_b_gate=gla_b_gate, gla_head_g=gla_head_g, gla_w_out=gla_w_out, mla_w_in=mla_w_in,
             mla_q_norm_g=mla_q_norm_g, mla_w_uq=mla_w_uq, mla_kv_norm_g=mla_kv_norm_g, mla_w_ukv=mla_w_ukv, mla_w_out=mla_w_out,
             gdn_w_in=gdn_w_in, gdn_conv_w=gdn_conv_w, gdn_a_log=gdn_a_log, gdn_dt_bias=gdn_dt_bias, gdn_norm_g=gdn_norm_g,
             gdn_w_out=gdn_w_out, mlp_w_up=mlp_w_up, mlp_w_down=mlp_w_down)
    m = dict(zip(WEIGHTS, [m_ada_w, m_ada_b, m_norm_pre_g, m_norm_post_g, m_gla_w_in, m_gla_w_gate_up, m_gla_b_gate, m_gla_head_g,
                           m_gla_w_out, m_mla_w_in, m_mla_q_norm_g, m_mla_w_uq, m_mla_kv_norm_g, m_mla_w_ukv, m_mla_w_out,
                           m_gdn_w_in, m_gdn_conv_w, m_gdn_a_log, m_gdn_dt_bias, m_gdn_norm_g, m_gdn_w_out, m_mlp_w_up, m_mlp_w_down]))
    v = dict(zip(WEIGHTS, [v_ada_w, v_ada_b, v_norm_pre_g, v_norm_post_g, v_gla_w_in, v_gla_w_gate_up, v_gla_b_gate, v_gla_head_g,
                           v_gla_w_out, v_mla_w_in, v_mla_q_norm_g, v_mla_w_uq, v_mla_kv_norm_g, v_mla_w_ukv, v_mla_w_out,
                           v_gdn_w_in, v_gdn_conv_w, v_gdn_a_log, v_gdn_dt_bias, v_gdn_norm_g, v_gdn_w_out, v_mlp_w_up, v_mlp_w_down]))
    t = x.shape[1]
    ix, iy, ic = lax.axis_index("x"), lax.axis_index("y"), lax.axis_index("c")
    me = 4 * ix + 2 * iy + ic
    chip = 2 * ix + iy
    ada_cols = ada_w.shape[2]

    full = _unpack_weights(_chip_gather(_pack_weights(w), name="gather_weights"), w, chip)

    cond8 = _silu_rows(jnp.pad(c, ((0, 7), (0, 0))), name="cond_silu")
    cond16 = jnp.pad(_all8_gather(cond8, name="gather_cond")[:, 0, :], ((0, 8), (0, 0)))
    mod_cols = []
    for layer in range(DEPTH):
        bias = jnp.broadcast_to(lax.dynamic_slice_in_dim(ada_b[layer], chip * ada_cols, ada_cols)[None], (16, ada_cols))
        mod_cols.append(_mm(cond16, ada_w[layer], epi="add", aux=bias, name=f"ada{layer}")[:8])
    mod_all = _all8_gather(jnp.stack(mod_cols).reshape(DEPTH * 8, ada_cols), name="gather_mod")
    mod = jnp.concatenate([lax.dynamic_slice_in_dim(mod_all[2 * j].reshape(DEPTH, 8, ada_cols), me, 1, axis=1)[:, 0]
                           for j in range(4)], axis=1)

    def layer_args(layer):
        kind, j = MIXERS[layer % 3], layer // 3
        mods = [mod[layer, i * D_MODEL:(i + 1) * D_MODEL][None] for i in range(N_MOD)]
        gains = (full["norm_pre_g"][layer, 0:1], full["norm_pre_g"][layer, 1:2], full["norm_post_g"][layer, 0:1],
                 full["norm_post_g"][layer, 1:2])
        return kind, j, mods, gains, _mixer_weights(kind, j, full, w)

    xs = x[0]
    kept = []
    for layer in range(DEPTH):
        kind, j, mods, gains, mw = layer_args(layer)
        xs, keep = _layer_fwd(xs, mods, gains, kind, mw, full["mlp_w_up"][layer], full["mlp_w_down"][layer], positions[0], f"l{layer}")
        kept.append(keep)
    loss_row, g = _loss_head(xs, loss_target[0], name="loss_head")
    loss = lax.psum(loss_row[0, 0], ("x", "y", "c"))

    grads = {n: [None] * w[n].shape[0] for n, _ in PACK_BF16 + PACK_F32}
    rep_grads = {}
    dmods = [None] * DEPTH
    for layer in reversed(range(DEPTH)):
        kind, j, mods, gains, mw = layer_args(layer)
        g, dmods[layer], dpre, dpost, mg, dw_up, dw_down = _layer_bwd(
            g, kept[layer], mods, gains, kind, mw, full["mlp_w_up"][layer], full["mlp_w_down"][layer], f"l{layer}")
        grads["norm_pre_g"][layer], grads["norm_post_g"][layer] = dpre, dpost
        grads["mlp_w_up"][layer], grads["mlp_w_down"][layer] = dw_up, dw_down
        for key, val in mg.items():
            name = kind + "_" + key
            if name in grads:
                grads[name][j] = val
            else:
                rep_grads[name] = val[None]

    rep_flat = jnp.concatenate([rep_grads[n].reshape(-1) for n in REPLICATED_SMALL])
    dbuf = jnp.concatenate([jnp.concatenate(dmods, axis=0), jnp.pad(rep_flat, (0, N_MOD * D_MODEL - rep_flat.shape[0]))[None],
                            jnp.zeros((3, N_MOD * D_MODEL), F32)], axis=0)
    dall = _all8_gather(dbuf, name="gather_dmod")
    dsum = _sum_slots(dall, name="sum_dmod")
    out_grads = {"ada_b": dsum[:DEPTH]}
    off = 0
    for n in REPLICATED_SMALL:
        out_grads[n] = dsum[DEPTH, off:off + w[n].size].reshape(w[n].shape)
        off += w[n].size
    dada = []
    for layer in range(DEPTH):
        dm16 = jnp.pad(lax.dynamic_slice_in_dim(dall[:, layer, :], chip * ada_cols, ada_cols, axis=1), ((0, 8), (0, 0)))
        dada.append(_mm(cond16, dm16, ta=True, name=f"dada{layer}"))
    out_grads["ada_w"] = jnp.stack(dada)

    reduced = _reduce_scatter(_pack_grads(grads), ic.reshape(1).astype(jnp.int32), chip.reshape(1).astype(jnp.int32),
                              tag="reduce_grads")
    out_grads.update(_unpack_grads(reduced, w))

    deltas, new_m, new_v = {}, {}, {}
    for n in WEIGHTS:
        deltas[n], new_m[n], new_v[n] = _adamw(w[n], out_grads[n], m[n], v[n], name="adamw_" + n)
    return (loss, g[None], *[out_grads[n] for n in WEIGHTS], *[deltas[n] for n in WEIGHTS],
            *[new_m[n] for n in WEIGHTS], *[new_v[n] for n in WEIGHTS])
```

```python
import functools
import math

import jax
import jax.numpy as jnp
from jax import lax
from jax.experimental import pallas as pl
from jax.experimental.pallas import tpu as pltpu

F32 = jnp.float32
BF16 = jnp.bfloat16

D_MODEL = 1024
DEPTH = 4
CHUNK = 64
EPS = 1e-6
NEG_INF = -1e30
N_MOD = 6

GLA_HEADS, GLA_DK, GLA_DV, GLA_RANK = 4, 128, 256, 16
GLA_KW, GLA_VW = GLA_HEADS * GLA_DK, GLA_HEADS * GLA_DV
GLA_IN = 2 * GLA_KW + 2 * GLA_VW + GLA_RANK
GLA_IN_PAD = 3200

MLA_HEADS, MLA_NOPE, MLA_ROPE, MLA_V = 16, 64, 32, 64
MLA_Q_RANK, MLA_KV_RANK = 384, 256
MLA_IN = MLA_Q_RANK + MLA_KV_RANK + MLA_ROPE
MLA_IN_PAD = 768
ROPE_THETA = 10000.0
MLA_QK = MLA_NOPE + MLA_ROPE
LANES = 128

GDN_K_HEADS, GDN_V_HEADS, GDN_DK, GDN_DV, GDN_CONV = 8, 16, 128, 128, 4
GDN_QKW, GDN_VW = GDN_K_HEADS * GDN_DK, GDN_V_HEADS * GDN_DV
GDN_CONV_W = 2 * GDN_QKW + GDN_VW
GDN_IN = GDN_CONV_W + GDN_VW + 2 * GDN_V_HEADS
GDN_IN_PAD = 6400

ADAM_LR, ADAM_B1, ADAM_B2, ADAM_EPS, ADAM_WD, ADAM_STEP = 0.001, 0.9, 0.999, 1e-08, 0.01, 10

VMEM_LIMIT = 56 * 1024 * 1024

NN = ((1,), (0,))
NT = ((1,), (1,))
TN = ((0,), (0,))


def _cp(*sem):
    return pltpu.CompilerParams(dimension_semantics=sem, vmem_limit_bytes=VMEM_LIMIT)


def _pick(n, cands):
    for c in cands:
        if n % c == 0:
            return c
    return n


def _dg(a, b, dims=NN):
    return lax.dot_general(a.astype(BF16), b.astype(BF16), (dims, ((), ())), preferred_element_type=F32)


def _dot3(a, b, dims=NN):
    ah = a.astype(BF16)
    al = (a - ah.astype(F32)).astype(BF16)
    bh = b.astype(BF16)
    bl = (b - bh.astype(F32)).astype(BF16)
    d = lambda u, v: lax.dot_general(u, v, (dims, ((), ())), preferred_element_type=F32)
    return d(ah, bh) + (d(ah, bl) + d(al, bh))


def _sigmoid(x):
    return 1.0 / (1.0 + jnp.exp(-x))


def _softplus(x):
    return jnp.maximum(x, 0.0) + jnp.log(1.0 + jnp.exp(-jnp.abs(x)))


def _iota2(shape, dim):
    return lax.broadcasted_iota(jnp.int32, shape, dim)


def _mm(a, b, *, ta=False, tb=False, out_dtype=F32, epi=None, aux=None, into=None, name):
    m = a.shape[1] if ta else a.shape[0]
    k = a.shape[0] if ta else a.shape[1]
    n = b.shape[0] if tb else b.shape[1]
    assert k == (b.shape[1] if tb else b.shape[0]), (a.shape, b.shape, ta, tb)
    m_tile = m // 4 if into is not None and into[2] == "rows" else m
    tm = _pick(m_tile, (1024, 512, 384, 256, 128))
    tn = _pick(n, (1024, 640, 512, 768, 384, 256, 128))
    tk = _pick(k, (1024, 640, 512, 768, 384, 256, 128))
    nk = k // tk
    dims = ((0 if ta else 1,), (1 if tb else 0,))

    def finish(r, x_ref, o_ref):
        if epi == "relu2":
            r = jnp.square(jnp.maximum(r, 0.0))
        elif epi == "dact":
            r = r * (2.0 * jnp.sqrt(x_ref[...].astype(F32)))
        elif epi == "add":
            r = r + x_ref[...]
        o_ref[...] = r.astype(out_dtype)

    n_in = 2 + (aux is not None) + (into is not None)

    def body(*refs):
        a_ref, b_ref = refs[:2]
        x_ref = refs[2] if aux is not None else None
        o_ref = refs[n_in]
        if nk == 1:
            finish(_dg(a_ref[...], b_ref[...], dims), x_ref, o_ref)
            return
        acc = refs[-1]
        kk = pl.program_id(2)

        @pl.when(kk == 0)
        def _():
            acc[...] = jnp.zeros_like(acc)

        acc[...] += _dg(a_ref[...], b_ref[...], dims)

        @pl.when(kk == nk - 1)
        def _():
            finish(acc[...], x_ref, o_ref)

    a_spec = pl.BlockSpec((tk, tm), lambda i, j, q: (q, i)) if ta else pl.BlockSpec((tm, tk), lambda i, j, q: (i, q))
    b_spec = pl.BlockSpec((tn, tk), lambda i, j, q: (j, q)) if tb else pl.BlockSpec((tk, tn), lambda i, j, q: (q, j))
    o_spec = pl.BlockSpec((tm, tn), lambda i, j, q: (i, j))
    in_specs = [a_spec, b_spec] + ([o_spec] if aux is not None else [])
    args = (a, b) + ((aux,) if aux is not None else ())
    out_shape = jax.ShapeDtypeStruct((m, n), out_dtype)
    aliases = {}
    if into is not None:
        dst, row0, axis = into
        assert dst.dtype == out_dtype and row0 % tm == 0 and tn == FLAT_W and n == (FLAT_W if axis == "rows" else 4 * FLAT_W)
        per = m_tile // tm
        if axis == "rows":
            o_spec = pl.BlockSpec((None, tm, tn), lambda i, j, q: (i // per, row0 // tm + i % per, 0))
        else:
            o_spec = pl.BlockSpec((None, tm, tn), lambda i, j, q: (j, row0 // tm + i, 0))
        in_specs.append(pl.BlockSpec(memory_space=pl.ANY))
        args += (dst,)
        out_shape = jax.ShapeDtypeStruct(dst.shape, dst.dtype)
        aliases = {n_in - 1: 0}
    return pl.pallas_call(
        body, grid=(m // tm, n // tn, nk), in_specs=in_specs, out_specs=o_spec, out_shape=out_shape,
        scratch_shapes=[pltpu.VMEM((tm, tn), F32)] if nk > 1 else [], input_output_aliases=aliases,
        compiler_params=_cp("parallel", "parallel", "arbitrary"), name=name)(*args)


def _row_tile(t):
    return _pick(t, (512, 256, 128, 64, 8))


def _premod_fwd(x, g, sc, sh, *, name):
    t, c = x.shape
    tr = _row_tile(t)

    def body(x_ref, g_ref, sc_ref, sh_ref, h_ref):
        xv = x_ref[...]
        r = lax.rsqrt(jnp.mean(xv * xv, axis=-1, keepdims=True) + EPS)
        h_ref[...] = (((xv * r) * g_ref[...]) * (1.0 + sc_ref[...]) + sh_ref[...]).astype(BF16)

    row = pl.BlockSpec((tr, c), lambda i: (i, 0))
    vec = pl.BlockSpec((1, c), lambda i: (0, 0))
    return pl.pallas_call(body, grid=(t // tr,), in_specs=[row, vec, vec, vec], out_specs=row,
                          out_shape=jax.ShapeDtypeStruct((t, c), BF16), compiler_params=_cp("parallel"), name=name)(x, g, sc, sh)


def _premod_bwd(dh, x, g, sc, gin, *, name):
    t, c = x.shape
    tr = _row_tile(t)

    def body(dh_ref, x_ref, g_ref, sc_ref, gin_ref, gout_ref, dg_ref, dsc_ref, dsh_ref):
        @pl.when(pl.program_id(0) == 0)
        def _():
            dg_ref[...] = jnp.zeros_like(dg_ref)
            dsc_ref[...] = jnp.zeros_like(dsc_ref)
            dsh_ref[...] = jnp.zeros_like(dsh_ref)

        xv = x_ref[...]
        dhv = dh_ref[...].astype(F32)
        gv = g_ref[...]
        one_sc = 1.0 + sc_ref[...]
        r = lax.rsqrt(jnp.mean(xv * xv, axis=-1, keepdims=True) + EPS)
        nv = xv * r
        dsh_ref[...] += jnp.sum(dhv, axis=0, keepdims=True)
        dsc_ref[...] += jnp.sum(dhv * (nv * gv), axis=0, keepdims=True)
        dg_ref[...] += jnp.sum(dhv * nv * one_sc, axis=0, keepdims=True)
        dn = dhv * gv * one_sc
        dx = r * (dn - nv * jnp.mean(dn * nv, axis=-1, keepdims=True))
        gout_ref[...] = gin_ref[...] + dx

    row = pl.BlockSpec((tr, c), lambda i: (i, 0))
    vec = pl.BlockSpec((1, c), lambda i: (0, 0))
    vs = jax.ShapeDtypeStruct((1, c), F32)
    return pl.pallas_call(body, grid=(t // tr,), in_specs=[row, row, vec, vec, row], out_specs=[row, vec, vec, vec],
                          out_shape=[jax.ShapeDtypeStruct((t, c), F32), vs, vs, vs],
                          compiler_params=_cp("arbitrary"), name=name)(dh, x, g, sc, gin)


def _postres_fwd(x, y, g, gt, *, name):
    t, c = x.shape
    tr = _row_tile(t)

    def body(x_ref, y_ref, g_ref, gt_ref, o_ref):
        yv = y_ref[...]
        r = lax.rsqrt(jnp.mean(yv * yv, axis=-1, keepdims=True) + EPS)
        o_ref[...] = x_ref[...] + gt_ref[...] * ((yv * r) * g_ref[...])

    row = pl.BlockSpec((tr, c), lambda i: (i, 0))
    vec = pl.BlockSpec((1, c), lambda i: (0, 0))
    return pl.pallas_call(body, grid=(t // tr,), in_specs=[row, row, vec, vec], out_specs=row,
                          out_shape=jax.ShapeDtypeStruct((t, c), F32), compiler_params=_cp("parallel"), name=name)(x, y, g, gt)


def _postres_bwd(gout, y, g, gt, *, name):
    t, c = y.shape
    tr = _row_tile(t)

    def body(go_ref, y_ref, g_ref, gt_ref, dy_ref, dg_ref, dgt_ref):
        @pl.when(pl.program_id(0) == 0)
        def _():
            dg_ref[...] = jnp.zeros_like(dg_ref)
            dgt_ref[...] = jnp.zeros_like(dgt_ref)

        yv = y_ref[...]
        gov = go_ref[...]
        gv = g_ref[...]
        gtv = gt_ref[...]
        r = lax.rsqrt(jnp.mean(yv * yv, axis=-1, keepdims=True) + EPS)
        z = yv * r
        dgt_ref[...] += jnp.sum(gov * (z * gv), axis=0, keepdims=True)
        dg_ref[...] += jnp.sum(gov * gtv * z, axis=0, keepdims=True)
        dz = gov * gtv * gv
        dy_ref[...] = (r * (dz - z * jnp.mean(dz * z, axis=-1, keepdims=True))).astype(BF16)

    row = pl.BlockSpec((tr, c), lambda i: (i, 0))
    vec = pl.BlockSpec((1, c), lambda i: (0, 0))
    vs = jax.ShapeDtypeStruct((1, c), F32)
    return pl.pallas_call(body, grid=(t // tr,), in_specs=[row, row, vec, vec], out_specs=[row, vec, vec],
                          out_shape=[jax.ShapeDtypeStruct((t, c), BF16), vs, vs],
                          compiler_params=_cp("arbitrary"), name=name)(gout, y, g, gt)


def _loss_head(y, tgt, *, name):
    t, c = y.shape
    tr = _row_tile(t)

    def body(y_ref, t_ref, l_ref, dy_ref):
        @pl.when(pl.program_id(0) == 0)
        def _():
            l_ref[...] = jnp.zeros_like(l_ref)

        d = y_ref[...] - t_ref[...]
        dy_ref[...] = d * (1.0 / c)
        l_ref[...] += 0.5 * jnp.sum(jnp.mean(d * d, axis=-1, keepdims=True))

    row = pl.BlockSpec((tr, c), lambda i: (i, 0))
    return pl.pallas_call(body, grid=(t // tr,), in_specs=[row, row],
                          out_specs=[pl.BlockSpec((1, LANES), lambda i: (0, 0)), row],
                          out_shape=[jax.ShapeDtypeStruct((1, LANES), F32), jax.ShapeDtypeStruct((t, c), F32)],
                          compiler_params=_cp("arbitrary"), name=name)(y, tgt)


def _adamw(w, g, m, v, *, name):
    shape = w.shape
    c = shape[-1]
    r = math.prod(shape[:-1])
    w2, g2, m2, v2 = (a.reshape(r, c) for a in (w, g, m, v))
    tr = r
    for cand in (1024, 512, 256, 128, 64, 32, 16, 8):
        if r % cand == 0 and cand * c * 4 <= (1 << 20):
            tr = cand
            break
    c1 = 1.0 - ADAM_B1 ** ADAM_STEP
    c2 = 1.0 - ADAM_B2 ** ADAM_STEP

    def body(w_ref, g_ref, m_ref, v_ref, d_ref, nm_ref, nv_ref):
        gv = g_ref[...]
        mn = ADAM_B1 * m_ref[...] + (1.0 - ADAM_B1) * gv
        vn = ADAM_B2 * v_ref[...] + (1.0 - ADAM_B2) * jnp.square(gv)
        m_hat = mn / c1
        v_hat = vn / c2
        d_ref[...] = -ADAM_LR * (m_hat / (jnp.sqrt(v_hat) + ADAM_EPS) + ADAM_WD * w_ref[...])
        nm_ref[...] = mn
        nv_ref[...] = vn

    blk = pl.BlockSpec((tr, c), lambda i: (i, 0))
    s = jax.ShapeDtypeStruct((r, c), F32)
    d, nm, nv = pl.pallas_call(body, grid=(r // tr,), in_specs=[blk] * 4, out_specs=[blk] * 3, out_shape=[s, s, s],
                               compiler_params=_cp("parallel"), name=name)(w2, g2, m2, v2)
    return d.reshape(shape), nm.reshape(shape), nv.reshape(shape)


def _gla_parts(p_ref, wg_ref, bg_ref):
    q = p_ref[:, 0:GLA_KW] * (GLA_DK ** -0.5)
    k = p_ref[:, GLA_KW:2 * GLA_KW]
    glr = p_ref[:, 2 * GLA_KW + 2 * GLA_VW:GLA_IN_PAD]
    gate = _dg(glr, wg_ref[...]) + bg_ref[...]
    log_a = (jnp.minimum(gate, 0.0) - jnp.log(1.0 + jnp.exp(-jnp.abs(gate)))) * (1.0 / 16.0)
    tril = (_iota2((CHUNK, CHUNK), 0) >= _iota2((CHUNK, CHUNK), 1)).astype(F32)
    cum = _dot_sel(tril, log_a)
    c_last = cum[CHUNK - 1:CHUNK, :]
    f = jnp.exp(c_last - cum)
    dec = jnp.exp(c_last)
    return q, k, glr, gate, f, k * f, dec


def _gla_fwd(p, wg, bg, hg, *, name):
    t = p.shape[0]
    nc = t // CHUNK

    def body(p_ref, wg_ref, bg_ref, hg_ref, og_ref, s_ref, st):
        @pl.when(pl.program_id(0) == 0)
        def _():
            st[...] = jnp.zeros_like(st)

        q, _, _, _, _, ke, dec = _gla_parts(p_ref, wg_ref, bg_ref)
        hs = range(GLA_HEADS)
        ks = [slice(h * GLA_DK, (h + 1) * GLA_DK) for h in hs]
        vs = [slice(2 * GLA_KW + h * GLA_DV, 2 * GLA_KW + (h + 1) * GLA_DV) for h in hs]
        rs = [slice(2 * GLA_KW + GLA_VW + h * GLA_DV, 2 * GLA_KW + GLA_VW + (h + 1) * GLA_DV) for h in hs]
        s_new = [st[h] * dec[:, ks[h]] + _dg(p_ref[:, vs[h]], ke[:, ks[h]], TN) for h in hs]
        o = [_dg(q[:, ks[h]], s_new[h], NT) for h in hs]
        for h in hs:
            st[h] = s_new[h]
            s_ref[0, h] = s_new[h]
            rn = lax.rsqrt(jnp.mean(o[h] * o[h], axis=-1, keepdims=True) + EPS)
            rv = p_ref[:, rs[h]]
            og_ref[:, h * GLA_DV:(h + 1) * GLA_DV] = (((o[h] * rn) * hg_ref[...]) * (rv * _sigmoid(rv))).astype(BF16)

    full = lambda a: pl.BlockSpec(a.shape, lambda n: (0,) * a.ndim)
    return pl.pallas_call(
        body, grid=(nc,),
        in_specs=[pl.BlockSpec((CHUNK, GLA_IN_PAD), lambda n: (n, 0)), full(wg), full(bg), full(hg)],
        out_specs=[pl.BlockSpec((CHUNK, GLA_VW), lambda n: (n, 0)),
                   pl.BlockSpec((1, GLA_HEADS, GLA_DV, GLA_DK), lambda n: (n, 0, 0, 0))],
        out_shape=[jax.ShapeDtypeStruct((t, GLA_VW), BF16), jax.ShapeDtypeStruct((nc, GLA_HEADS, GLA_DV, GLA_DK), F32)],
        scratch_shapes=[pltpu.VMEM((GLA_HEADS, GLA_DV, GLA_DK), F32)],
        compiler_params=_cp("arbitrary"), name=name)(p, wg, bg, hg)


def _gla_bwd(p, dog, sall, wg, bg, hg, *, name):
    t = p.shape[0]
    nc = t // CHUNK

    def body(p_ref, dog_ref, s1_ref, s0_ref, wg_ref, bg_ref, hg_ref, dp_ref, dwg_ref, dbg_ref, dhg_ref, gt):
        i = pl.program_id(0)

        @pl.when(i == 0)
        def _():
            gt[...] = jnp.zeros_like(gt)
            dwg_ref[...] = jnp.zeros_like(dwg_ref)
            dbg_ref[...] = jnp.zeros_like(dbg_ref)
            dhg_ref[...] = jnp.zeros_like(dhg_ref)

        has_prev = (i < nc - 1).astype(F32)
        q, k, glr, gate, f, ke, dec = _gla_parts(p_ref, wg_ref, bg_ref)
        hgv = hg_ref[...]
        hs = range(GLA_HEADS)
        ks = [slice(h * GLA_DK, (h + 1) * GLA_DK) for h in hs]
        vs = [slice(2 * GLA_KW + h * GLA_DV, 2 * GLA_KW + (h + 1) * GLA_DV) for h in hs]
        rs = [slice(2 * GLA_KW + GLA_VW + h * GLA_DV, 2 * GLA_KW + GLA_VW + (h + 1) * GLA_DV) for h in hs]
        s1 = [s1_ref[0, h] for h in hs]
        o = [_dg(q[:, ks[h]], s1[h], NT) for h in hs]
        dhg = jnp.zeros((1, GLA_DV), F32)
        do = []
        for h in hs:
            rv = p_ref[:, rs[h]]
            rn = lax.rsqrt(jnp.mean(o[h] * o[h], axis=-1, keepdims=True) + EPS)
            z = o[h] * rn
            sg = _sigmoid(rv)
            sl = rv * sg
            dogh = dog_ref[:, h * GLA_DV:(h + 1) * GLA_DV].astype(F32)
            dhg = dhg + jnp.sum(dogh * z * sl, axis=0, keepdims=True)
            dp_ref[:, rs[h]] = (dogh * (z * hgv) * (sg * (1.0 + rv * (1.0 - sg)))).astype(BF16)
            dz = dogh * sl * hgv
            do.append(rn * (dz - z * jnp.mean(dz * z, axis=-1, keepdims=True)))
        dhg_ref[...] += dhg
        g_tot = [gt[h] + _dg(do[h], q[:, ks[h]], TN) for h in hs]
        dq = [_dg(do[h], s1[h], NN) for h in hs]
        dke_parts = [_dg(p_ref[:, vs[h]], g_tot[h], NN) for h in hs]
        dv = [_dg(ke[:, ks[h]], g_tot[h], NT) for h in hs]
        ddec_parts = []
        for h in hs:
            dp_ref[:, ks[h]] = (dq[h] * (GLA_DK ** -0.5)).astype(BF16)
            dp_ref[:, vs[h]] = dv[h].astype(BF16)
            ddec_parts.append(jnp.sum(g_tot[h] * (s0_ref[0, h] * has_prev), axis=0, keepdims=True))
            gt[h] = g_tot[h] * dec[:, ks[h]]
        dke = jnp.concatenate(dke_parts, axis=1)
        ddec = jnp.concatenate(ddec_parts, axis=1)
        dp_ref[:, GLA_KW:2 * GLA_KW] = (dke * f).astype(BF16)
        stril = (_iota2((CHUNK, CHUNK), 0) > _iota2((CHUNK, CHUNK), 1)).astype(F32)
        dlog_a = _dot_sel(stril, dke * ke) + ddec * dec
        dgate = dlog_a * (1.0 / 16.0) * _sigmoid(-gate)
        dp_ref[:, 2 * GLA_KW + 2 * GLA_VW:GLA_IN_PAD] = _dg(dgate, wg_ref[...], NT).astype(BF16)
        dwg_ref[...] += _dg(glr, dgate, TN)
        dbg_ref[...] += jnp.sum(dgate, axis=0, keepdims=True)

    full = lambda a: pl.BlockSpec(a.shape, lambda n: (0,) * a.ndim)
    sblk = (1, GLA_HEADS, GLA_DV, GLA_DK)
    return pl.pallas_call(
        body, grid=(nc,),
        in_specs=[pl.BlockSpec((CHUNK, GLA_IN_PAD), lambda n: (nc - 1 - n, 0)),
                  pl.BlockSpec((CHUNK, GLA_VW), lambda n: (nc - 1 - n, 0)),
                  pl.BlockSpec(sblk, lambda n: (nc - 1 - n, 0, 0, 0)),
                  pl.BlockSpec(sblk, lambda n: (jnp.maximum(nc - 2 - n, 0), 0, 0, 0)),
                  full(wg), full(bg), full(hg)],
        out_specs=[pl.BlockSpec((CHUNK, GLA_IN_PAD), lambda n: (nc - 1 - n, 0)), full(wg), full(bg), full(hg)],
        out_shape=[jax.ShapeDtypeStruct((t, GLA_IN_PAD), BF16), jax.ShapeDtypeStruct(wg.shape, F32),
                   jax.ShapeDtypeStruct(bg.shape, F32), jax.ShapeDtypeStruct(hg.shape, F32)],
        scratch_shapes=[pltpu.VMEM((GLA_HEADS, GLA_DV, GLA_DK), F32)],
        compiler_params=_cp("arbitrary"), name=name)(p, dog, sall, sall, wg, bg, hg)


def _gla_layer_fwd(h, w, tag):
    p = _mm(h, w["w_in"], name=tag + "_in")
    og, sall = _gla_fwd(p, w["wg"], w["bg"], w["hg"], name=tag + "_scan")
    y = _mm(og, w["w_out"], name=tag + "_out")
    return y, (p, og, sall)


def _dw_out(act, dy, sink, name):
    if sink is None:
        return _mm(act, dy, ta=True, name=name), None
    return None, _mm(act, dy, ta=True, into=(sink[0], sink[1], "rows"), name=name)


def _gla_layer_bwd(dy, h, saved, w, tag, sink=None):
    p, og, sall = saved
    dog = _mm(dy, w["w_out"], tb=True, name=tag + "_dog")
    dw_out, gf = _dw_out(og, dy, sink, tag + "_dwout")
    dp, dwg, dbg, dhg = _gla_bwd(p, dog, sall, w["wg"], w["bg"], w["hg"], name=tag + "_scanb")
    dw_in = _mm(h, dp, ta=True, name=tag + "_dwin")
    dh = _mm(dp, w["w_in"], tb=True, name=tag + "_dh")
    grads = dict(w_in=dw_in[:, :GLA_IN], w_gate_up=dwg[:GLA_RANK], b_gate=dbg[0], head_g=dhg[0], w_out=dw_out)
    return dh, grads, gf


def _rope_tables(pos, inv_freq, *, name):
    t = pos.shape[0]
    tr = _row_tile(t)
    half = MLA_ROPE // 2

    def body(p_ref, f_ref, c_ref, s1_ref, s2_ref, s1b_ref, s2b_ref):
        ang = p_ref[...].astype(F32) * f_ref[...]
        lane = _iota2((tr, LANES), 1)
        lo = (lane >= MLA_NOPE) & (lane < MLA_NOPE + half)
        hi = (lane >= MLA_NOPE + half) & (lane < MLA_QK)
        cs, sn = jnp.cos(ang), jnp.sin(ang)
        zero = jnp.zeros_like(cs)
        c_ref[...] = jnp.where(lane < MLA_NOPE, 1.0, jnp.where(lane < MLA_QK, cs, 0.0))
        s1_ref[...] = jnp.where(lo, -sn, zero)
        s2_ref[...] = jnp.where(hi, sn, zero)
        s1b_ref[...] = jnp.where(lo, sn, zero)
        s2b_ref[...] = jnp.where(hi, -sn, zero)

    row = pl.BlockSpec((tr, LANES), lambda i: (i, 0))
    s = jax.ShapeDtypeStruct((t, LANES), F32)
    return pl.pallas_call(body, grid=(t // tr,),
                          in_specs=[pl.BlockSpec((tr, 1), lambda i: (i, 0)), pl.BlockSpec((1, LANES), lambda i: (0, 0))],
                          out_specs=[row] * 5, out_shape=[s] * 5, compiler_params=_cp("parallel"), name=name)(pos, inv_freq)


def _rope(x, c, s1, s2, *, out_dtype, sum_heads=False, name):
    nh, t, _ = x.shape
    tr = _row_tile(t)
    half = MLA_ROPE // 2

    def body(x_ref, c_ref, s1_ref, s2_ref, o_ref):
        total = None
        for h in range(nh):
            xv = x_ref[h].astype(F32)
            y = xv * c_ref[...] + pltpu.roll(xv, LANES - half, 1) * s1_ref[...] + pltpu.roll(xv, half, 1) * s2_ref[...]
            if sum_heads:
                total = y if total is None else total + y
            else:
                o_ref[h] = y.astype(out_dtype)
        if sum_heads:
            o_ref[...] = total

    tab = pl.BlockSpec((tr, LANES), lambda i: (i, 0))
    xs = pl.BlockSpec((nh, tr, LANES), lambda i: (0, i, 0))
    if sum_heads:
        return pl.pallas_call(body, grid=(t // tr,), in_specs=[xs, tab, tab, tab], out_specs=tab,
                              out_shape=jax.ShapeDtypeStruct((t, LANES), F32),
                              compiler_params=_cp("parallel"), name=name)(x, c, s1, s2)
    return pl.pallas_call(body, grid=(t // tr,), in_specs=[xs, tab, tab, tab], out_specs=xs,
                          out_shape=jax.ShapeDtypeStruct(x.shape, out_dtype),
                          compiler_params=_cp("parallel"), name=name)(x, c, s1, s2)


FLASH_BLK = 512


def _diag_mask(blk):
    return (_iota2((blk, blk), 1) // CHUNK) <= (_iota2((blk, blk), 0) // CHUNK)


def _flash_fwd(q, k, v, *, name):
    nh, t, _ = q.shape
    blk = min(FLASH_BLK, t)
    scale = MLA_QK ** -0.5

    def body(q_ref, k_ref, v_ref, o_ref, lse_ref):
        i = pl.program_id(1)
        qv = q_ref[0]

        def step(j, carry, masked):
            m, l, acc = carry
            off = pl.multiple_of(j * blk, blk)
            kb = k_ref[0, pl.ds(off, blk), :]
            vb = v_ref[0, pl.ds(off, blk), :]
            s = _dg(qv, kb, NT) * scale
            if masked:
                s = jnp.where(_diag_mask(blk), s, NEG_INF)
            m_new = jnp.maximum(m, jnp.max(s, axis=-1, keepdims=True))
            p = jnp.exp(s - m_new)
            alpha = jnp.exp(m - m_new)
            return m_new, alpha * l + jnp.sum(p, axis=-1, keepdims=True), alpha * acc + _dg(p, vb, NN)

        init = (jnp.full((blk, 1), NEG_INF, F32), jnp.zeros((blk, 1), F32), jnp.zeros((blk, MLA_V), F32))
        carry = lax.fori_loop(0, i, lambda j, c: step(j, c, False), init)
        m, l, acc = step(i, carry, True)
        o_ref[0] = (acc / l).astype(BF16)
        lse_ref[0] = m + jnp.log(l)

    qs = pl.BlockSpec((1, blk, LANES), lambda h, i: (h, i, 0))
    return pl.pallas_call(
        body, grid=(nh, t // blk),
        in_specs=[qs, pl.BlockSpec((1, t, LANES), lambda h, i: (h, 0, 0)), pl.BlockSpec((1, t, MLA_V), lambda h, i: (h, 0, 0))],
        out_specs=[pl.BlockSpec((1, blk, MLA_V), lambda h, i: (h, i, 0)), pl.BlockSpec((1, blk, 1), lambda h, i: (h, i, 0))],
        out_shape=[jax.ShapeDtypeStruct((nh, t, MLA_V), BF16), jax.ShapeDtypeStruct((nh, t, 1), F32)],
        compiler_params=_cp("parallel", "parallel"), name=name)(q, k, v)


def _flash_dq(q, k, v, do, o, lse, *, name):
    nh, t, _ = q.shape
    blk = min(FLASH_BLK, t)
    scale = MLA_QK ** -0.5

    def body(q_ref, k_ref, v_ref, do_ref, o_ref, lse_ref, dq_ref, dl_ref):
        i = pl.program_id(1)
        qv = q_ref[0]
        dov = do_ref[0]
        lse_v = lse_ref[0]
        delta = jnp.sum(dov.astype(F32) * o_ref[0].astype(F32), axis=-1, keepdims=True)
        dl_ref[0] = delta

        def step(j, dq, masked):
            off = pl.multiple_of(j * blk, blk)
            kb = k_ref[0, pl.ds(off, blk), :]
            vb = v_ref[0, pl.ds(off, blk), :]
            s = _dg(qv, kb, NT) * scale
            if masked:
                s = jnp.where(_diag_mask(blk), s, NEG_INF)
            p = jnp.exp(s - lse_v)
            ds = p * (_dg(dov, vb, NT) - delta) * scale
            return dq + _dg(ds, kb, NN)

        dq = lax.fori_loop(0, i, lambda j, c: step(j, c, False), jnp.zeros((blk, LANES), F32))
        dq_ref[0] = step(i, dq, True)

    qs = pl.BlockSpec((1, blk, LANES), lambda h, i: (h, i, 0))
    vs = pl.BlockSpec((1, blk, MLA_V), lambda h, i: (h, i, 0))
    ls = pl.BlockSpec((1, blk, 1), lambda h, i: (h, i, 0))
    return pl.pallas_call(
        body, grid=(nh, t // blk),
        in_specs=[qs, pl.BlockSpec((1, t, LANES), lambda h, i: (h, 0, 0)), pl.BlockSpec((1, t, MLA_V), lambda h, i: (h, 0, 0)), vs, vs, ls],
        out_specs=[qs, ls],
        out_shape=[jax.ShapeDtypeStruct((nh, t, LANES), F32), jax.ShapeDtypeStruct((nh, t, 1), F32)],
        compiler_params=_cp("parallel", "parallel"), name=name)(q, k, v, do, o, lse)


def _flash_dkv(q, k, v, do, lse, delta, *, name):
    nh, t, _ = q.shape
    blk = min(FLASH_BLK, t)
    nq = t // blk
    scale = MLA_QK ** -0.5

    def body(q_ref, k_ref, v_ref, do_ref, lse_ref, dl_ref, dk_ref, dv_ref):
        j = pl.program_id(1)
        kb = k_ref[0]
        vb = v_ref[0]

        def step(i, carry, masked):
            dk, dv = carry
            off = pl.multiple_of(i * blk, blk)
            qb = q_ref[0, pl.ds(off, blk), :]
            dob = do_ref[0, pl.ds(off, blk), :]
            s = _dg(qb, kb, NT) * scale
            if masked:
                s = jnp.where(_diag_mask(blk), s, NEG_INF)
            p = jnp.exp(s - lse_ref[0, pl.ds(off, blk), :])
            ds = p * (_dg(dob, vb, NT) - dl_ref[0, pl.ds(off, blk), :]) * scale
            return dk + _dg(ds, qb, TN), dv + _dg(p, dob, TN)

        carry = step(j, (jnp.zeros((blk, LANES), F32), jnp.zeros((blk, MLA_V), F32)), True)
        dk, dv = lax.fori_loop(j + 1, nq, lambda i, c: step(i, c, False), carry)
        dk_ref[0] = dk
        dv_ref[0] = dv

    ks = pl.BlockSpec((1, blk, LANES), lambda h, j: (h, j, 0))
    vs = pl.BlockSpec((1, blk, MLA_V), lambda h, j: (h, j, 0))
    fl = lambda w: pl.BlockSpec((1, t, w), lambda h, j: (h, 0, 0))
    return pl.pallas_call(
        body, grid=(nh, nq),
        in_specs=[fl(LANES), ks, vs, fl(MLA_V), fl(1), fl(1)],
        out_specs=[ks, vs],
        out_shape=[jax.ShapeDtypeStruct((nh, t, LANES), F32), jax.ShapeDtypeStruct((nh, t, MLA_V), F32)],
        compiler_params=_cp("parallel", "parallel"), name=name)(q, k, v, do, lse, delta)


def _heads_first(a, width):
    t = a.shape[0]
    return a.reshape(t, MLA_HEADS, width).transpose(1, 0, 2)


def _heads_last(a):
    return a.transpose(1, 0, 2).reshape(a.shape[1], -1)


def _mla_layer_fwd(h, pos, w, tag):
    t = h.shape[0]
    zq = jnp.zeros((1, MLA_Q_RANK), F32)
    zkv = jnp.zeros((1, MLA_KV_RANK), F32)
    p = _mm(h, w["w_in"], name=tag + "_in")
    cq, ckv, krp = p[:, :MLA_Q_RANK], p[:, MLA_Q_RANK:MLA_Q_RANK + MLA_KV_RANK], p[:, MLA_Q_RANK + MLA_KV_RANK:MLA_IN]
    qn = _premod_fwd(cq, w["q_norm_g"], zq, zq, name=tag + "_qnorm")
    kvn = _premod_fwd(ckv, w["kv_norm_g"], zkv, zkv, name=tag + "_kvnorm")
    q = _mm(qn, w["w_uq"], name=tag + "_uq")
    kv = _mm(kvn, w["w_ukv"], name=tag + "_ukv")
    q_pre = jnp.pad(_heads_first(q, MLA_QK), ((0, 0), (0, 0), (0, LANES - MLA_QK)))
    kv3 = _heads_first(kv, MLA_NOPE + MLA_V)
    k_pre = jnp.concatenate([kv3[:, :, :MLA_NOPE], jnp.broadcast_to(krp[None], (MLA_HEADS, t, MLA_ROPE)),
                             jnp.zeros((MLA_HEADS, t, LANES - MLA_QK), F32)], axis=-1)
    vh = kv3[:, :, MLA_NOPE:].astype(BF16)
    half = MLA_ROPE // 2
    freq = ROPE_THETA ** (-jnp.arange(half, dtype=F32) / half)
    inv_freq = jnp.concatenate([jnp.zeros((MLA_NOPE,), F32), freq, freq, jnp.zeros((LANES - MLA_QK,), F32)])[None]
    tabs = _rope_tables(pos.reshape(t, 1), inv_freq, name=tag + "_tables")
    qr = _rope(q_pre, tabs[0], tabs[1], tabs[2], out_dtype=BF16, name=tag + "_ropeq")
    kr = _rope(k_pre, tabs[0], tabs[1], tabs[2], out_dtype=BF16, name=tag + "_ropek")
    o, lse = _flash_fwd(qr, kr, vh, name=tag + "_attn")
    of = _heads_last(o)
    y = _mm(of, w["w_out"], name=tag + "_out")
    return y, (cq, ckv, qn, kvn, qr, kr, vh, o, lse, of, tabs)


def _mla_layer_bwd(dy, h, saved, w, tag, sink=None):
    cq, ckv, qn, kvn, qr, kr, vh, o, lse, of, tabs = saved
    t = h.shape[0]
    zq = jnp.zeros((1, MLA_Q_RANK), F32)
    zkv = jnp.zeros((1, MLA_KV_RANK), F32)
    dof = _mm(dy, w["w_out"], tb=True, out_dtype=BF16, name=tag + "_dof")
    dw_out, gf = _dw_out(of, dy, sink, tag + "_dwout")
    do = _heads_first(dof, MLA_V)
    dqr, delta = _flash_dq(qr, kr, vh, do, o, lse, name=tag + "_attn_dq")
    dkr, dv = _flash_dkv(qr, kr, vh, do, lse, delta, name=tag + "_attn_dkv")
    dq_pre = _rope(dqr, tabs[0], tabs[3], tabs[4], out_dtype=F32, name=tag + "_ropeq_b")
    dk_sum = _rope(dkr, tabs[0], tabs[3], tabs[4], out_dtype=F32, sum_heads=True, name=tag + "_ropek_b")
    dq = _heads_last(dq_pre[:, :, :MLA_QK])
    dkv = _heads_last(jnp.concatenate([dkr[:, :, :MLA_NOPE], dv], axis=-1))
    dw_uq = _mm(qn, dq, ta=True, name=tag + "_dwuq")
    dqn = _mm(dq, w["w_uq"], tb=True, name=tag + "_dqn")
    dw_ukv = _mm(kvn, dkv, ta=True, name=tag + "_dwukv")
    dkvn = _mm(dkv, w["w_ukv"], tb=True, name=tag + "_dkvn")
    dcq, dqg, _, _ = _premod_bwd(dqn, cq, w["q_norm_g"], zq, jnp.zeros_like(cq), name=tag + "_qnorm_b")
    dckv, dkvg, _, _ = _premod_bwd(dkvn, ckv, w["kv_norm_g"], zkv, jnp.zeros_like(ckv), name=tag + "_kvnorm_b")
    dp = jnp.concatenate([dcq, dckv, dk_sum[:, MLA_NOPE:MLA_QK], jnp.zeros((t, MLA_IN_PAD - MLA_IN), F32)], axis=1).astype(BF16)
    dw_in = _mm(h, dp, ta=True, name=tag + "_dwin")
    dh = _mm(dp, w["w_in"], tb=True, name=tag + "_dh")
    grads = dict(w_in=dw_in[:, :MLA_IN], q_norm_g=dqg[0], w_uq=dw_uq, kv_norm_g=dkvg[0], w_ukv=dw_ukv, w_out=dw_out)
    return dh, grads, gf


CONV_HALO = 8


def _conv_tiles(t):
    return min(512, t), 512


def _gdn_conv_fwd(p, w, *, name):
    t = p.shape[0]
    tr, tc = _conv_tiles(t)
    hb = tr // CONV_HALO

    def body(x_ref, halo_ref, w_ref, o_ref, buf):
        i = pl.program_id(0)
        buf[0:CONV_HALO, :] = halo_ref[...] * (i > 0).astype(F32)
        buf[CONV_HALO:CONV_HALO + tr, :] = x_ref[...]
        base = CONV_HALO - (GDN_CONV - 1)
        acc = buf[pl.ds(base, tr), :] * w_ref[0:1, :]
        for j in range(1, GDN_CONV):
            acc = acc + buf[pl.ds(base + j, tr), :] * w_ref[j:j + 1, :]
        o_ref[...] = acc * _sigmoid(acc)

    return pl.pallas_call(
        body, grid=(t // tr, GDN_CONV_W // tc),
        in_specs=[pl.BlockSpec((tr, tc), lambda i, j: (i, j)),
                  pl.BlockSpec((CONV_HALO, tc), lambda i, j: (jnp.maximum(i * hb - 1, 0), j)),
                  pl.BlockSpec((GDN_CONV, tc), lambda i, j: (0, j))],
        out_specs=pl.BlockSpec((tr, tc), lambda i, j: (i, j)),
        out_shape=jax.ShapeDtypeStruct((t, GDN_CONV_W), F32),
        scratch_shapes=[pltpu.VMEM((tr + CONV_HALO, tc), F32)],
        compiler_params=_cp("parallel", "parallel"), name=name)(p, p, w)


def _gdn_conv_bwd(d, p, w, col0, *, name):
    t = p.shape[0]
    tr, tc = _conv_tiles(t)
    hb = tr // CONV_HALO
    nr = t // tr
    ext = tr + CONV_HALO

    def body(x_ref, xp_ref, xn_ref, d_ref, dn_ref, w_ref, dx_ref, dw_ref, bufx, bufd):
        i = pl.program_id(1)

        @pl.when(i == 0)
        def _():
            dw_ref[...] = jnp.zeros_like(dw_ref)

        last = (i < nr - 1).astype(F32)
        bufx[0:CONV_HALO, :] = xp_ref[...] * (i > 0).astype(F32)
        bufx[CONV_HALO:CONV_HALO + tr, :] = x_ref[...]
        bufx[CONV_HALO + tr:, :] = xn_ref[...] * last
        base = CONV_HALO - (GDN_CONV - 1)
        acc = bufx[pl.ds(base, ext), :] * w_ref[0:1, :]
        for j in range(1, GDN_CONV):
            acc = acc + bufx[pl.ds(base + j, ext), :] * w_ref[j:j + 1, :]
        sg = _sigmoid(acc)
        dsilu = sg * (1.0 + acc * (1.0 - sg))
        bufd[0:tr, :] = d_ref[...] * dsilu[0:tr, :]
        bufd[tr:, :] = dn_ref[...] * last * dsilu[tr:, :]
        dx = bufd[pl.ds(GDN_CONV - 1, tr), :] * w_ref[0:1, :]
        for j in range(1, GDN_CONV):
            dx = dx + bufd[pl.ds(GDN_CONV - 1 - j, tr), :] * w_ref[j:j + 1, :]
        dx_ref[...] = dx.astype(BF16)
        dc = bufd[0:tr, :]
        for j in range(GDN_CONV):
            dw_ref[j:j + 1, :] += jnp.sum(dc * bufx[pl.ds(base + j, tr), :], axis=0, keepdims=True)

    cb = col0 // tc
    width = d.shape[1]
    main = lambda off: pl.BlockSpec((tr, tc), lambda j, i: (i, j + off))
    prev = pl.BlockSpec((CONV_HALO, tc), lambda j, i: (jnp.maximum(i * hb - 1, 0), j + cb))
    nxt = lambda off: pl.BlockSpec((CONV_HALO, tc), lambda j, i: (jnp.minimum((i + 1) * hb, t // CONV_HALO - 1), j + off))
    wsp = lambda off: pl.BlockSpec((GDN_CONV, tc), lambda j, i: (0, j + off))
    return pl.pallas_call(
        body, grid=(width // tc, nr),
        in_specs=[main(cb), prev, nxt(cb), main(0), nxt(0), wsp(cb)], out_specs=[main(0), wsp(0)],
        out_shape=[jax.ShapeDtypeStruct((t, width), BF16), jax.ShapeDtypeStruct((GDN_CONV, width), F32)],
        scratch_shapes=[pltpu.VMEM((tr + 2 * CONV_HALO, tc), F32), pltpu.VMEM((ext, tc), F32)],
        compiler_params=_cp("parallel", "arbitrary"), name=name)(p, p, p, d, d, w)


def _dot_sel(sel, b, dims=NN, sel_first=True):
    s = sel.astype(BF16)
    b1 = b.astype(BF16)
    r1 = b - b1.astype(F32)
    b2 = r1.astype(BF16)
    b3 = (r1 - b2.astype(F32)).astype(BF16)
    if sel_first:
        d = lambda v: lax.dot_general(s, v, (dims, ((), ())), preferred_element_type=F32)
    else:
        d = lambda v: lax.dot_general(v, s, (dims, ((), ())), preferred_element_type=F32)
    return d(b1) + (d(b2) + d(b3))


def _gdn_chunks(qraws, kraws, vs, braws, araws, alogs, dtbs):
    nv = len(vs)
    row = _iota2((CHUNK, CHUNK), 0)
    col = _iota2((CHUNK, CHUNK), 1)
    strict = row > col
    triu = (row <= col).astype(F32)
    tril = (row >= col).astype(F32)
    ones = jnp.ones((CHUNK, CHUNK), F32)
    keys = []
    for qraw, kraw in zip(qraws, kraws):
        rq = lax.rsqrt(jnp.sum(qraw * qraw, axis=-1, keepdims=True) + EPS)
        rk = lax.rsqrt(jnp.sum(kraw * kraw, axis=-1, keepdims=True) + EPS)
        qn = qraw * rq
        keys.append(dict(rq=rq, rk=rk, qn=qn, qh=qn * (GDN_DK ** -0.5), kh=kraw * rk))
    kks = [_dg(kd["kh"], kd["kh"], NT) for kd in keys]
    cs = []
    for h in range(nv):
        c = dict(keys[h // 2])
        c.update(v=vs[h], kk=kks[h // 2], strict=strict, triu=triu)
        c["beta"] = _sigmoid(braws[h])
        c["ea"] = jnp.exp(alogs[h])
        c["xs"] = araws[h] + dtbs[h]
        c["g"] = -c["ea"] * _softplus(c["xs"])
        cs.append(c)
    gbs = [jnp.broadcast_to(c["g"], (CHUNK, LANES)) for c in cs]
    cums = [_dot_sel(tril, gb) for gb in gbs]
    cum_js = [_dot_sel(ones, gb[:, :CHUNK] * triu) for gb in gbs]
    for c, cum, cum_j in zip(cs, cums, cum_js):
        diff = jnp.where(strict, cum[:, :CHUNK] - cum_j, 0.0)
        c["dm"] = jnp.where(strict, jnp.exp(diff), 0.0)
        c["a"] = (c["beta"] * c["dm"]) * c["kk"]
        c_last = cum[CHUNK - 1:CHUNK, :]
        c["e"] = jnp.exp(cum)
        c["f"] = jnp.exp(c_last - cum)
        c["dec"] = jnp.exp(c_last)
        c["rv"] = c["beta"] * c["v"]
        c["rk_rhs"] = (c["beta"] * c["e"]) * c["kh"]
        c["ke"] = c["kh"] * c["f"]
    return cs


def _unit_lower_inverses(mats):
    eye = (_iota2((CHUNK, CHUNK), 0) == _iota2((CHUNK, CHUNK), 1)).astype(F32)
    ts = [eye - a for a in mats]
    pws = list(mats)
    for _ in range(5):
        pws = [_dot3(pw, pw) for pw in pws]
        ts = [t + _dot3(t, pw) for t, pw in zip(ts, pws)]
    return ts


GDN_HB = 8


def _gdn_specs(chunk_of):
    hb = GDN_HB
    kw = hb // 2 * GDN_DK
    vw = hb * GDN_DV
    qs = pl.BlockSpec((CHUNK, kw), lambda g, n: (chunk_of(n), g))
    ks = pl.BlockSpec((CHUNK, kw), lambda g, n: (chunk_of(n), GDN_QKW // kw + g))
    vs = pl.BlockSpec((CHUNK, vw), lambda g, n: (chunk_of(n), 2 * GDN_QKW // vw + g))
    zs = pl.BlockSpec((CHUNK, vw), lambda g, n: (chunk_of(n), GDN_CONV_W // vw + g))
    col = pl.BlockSpec((hb, CHUNK, 1), lambda g, n: (g, chunk_of(n), 0))
    one = pl.BlockSpec((hb, 1, 1), lambda g, n: (g, 0, 0))
    ng = pl.BlockSpec((1, GDN_DV), lambda g, n: (0, 0))
    hd = pl.BlockSpec((CHUNK, vw), lambda g, n: (chunk_of(n), g))
    return qs, ks, vs, zs, col, one, ng, hd


def _gdn_fwd(qkv, p, braw, araw, alog, dtb, ng, *, name):
    t = qkv.shape[0]
    nc = t // CHUNK
    nh = GDN_V_HEADS

    def body(q_ref, k_ref, v_ref, z_ref, braw_ref, araw_ref, alog_ref, dtb_ref, ng_ref, og_ref, s_ref, t_ref, st):
        @pl.when(pl.program_id(1) == 0)
        def _():
            st[...] = jnp.zeros_like(st)

        hs = range(GDN_HB)
        kqs = [slice(j * GDN_DK, (j + 1) * GDN_DK) for j in range(GDN_HB // 2)]
        vsl = [slice(h * GDN_DV, (h + 1) * GDN_DV) for h in hs]
        cs = _gdn_chunks([q_ref[:, s] for s in kqs], [k_ref[:, s] for s in kqs], [v_ref[:, s] for s in vsl],
                         [braw_ref[h] for h in hs], [araw_ref[h] for h in hs], [alog_ref[h] for h in hs], [dtb_ref[h] for h in hs])
        tms = _unit_lower_inverses([c["a"] for c in cs])
        s0 = [st[h] for h in hs]
        wv = [_dot3(tms[h], cs[h]["rv"]) for h in hs]
        wk = [_dot3(tms[h], cs[h]["rk_rhs"]) for h in hs]
        u = [wv[h] - _dg(wk[h], s0[h], NN) for h in hs]
        s1 = [cs[h]["dec"] * s0[h] + _dg(cs[h]["ke"], u[h], TN) for h in hs]
        o = [_dg(cs[h]["qh"], s1[h], NN) for h in hs]
        for h in hs:
            t_ref[h, 0] = tms[h]
            st[h] = s1[h]
            s_ref[h, 0] = s1[h]
            rn = lax.rsqrt(jnp.mean(o[h] * o[h], axis=-1, keepdims=True) + EPS)
            zv = z_ref[:, vsl[h]]
            og_ref[:, vsl[h]] = (((o[h] * rn) * ng_ref[...]) * (zv * _sigmoid(zv))).astype(BF16)

    qs, ks, vs, zs, col, one, ngs, hd = _gdn_specs(lambda n: n)
    return pl.pallas_call(
        body, grid=(nh // GDN_HB, nc),
        in_specs=[qs, ks, vs, zs, col, col, one, one, ngs],
        out_specs=[hd,
                   pl.BlockSpec((GDN_HB, 1, GDN_DK, GDN_DV), lambda g, n: (g, n, 0, 0)),
                   pl.BlockSpec((GDN_HB, 1, CHUNK, CHUNK), lambda g, n: (g, n, 0, 0))],
        out_shape=[jax.ShapeDtypeStruct((t, GDN_VW), BF16), jax.ShapeDtypeStruct((nh, nc, GDN_DK, GDN_DV), F32),
                   jax.ShapeDtypeStruct((nh, nc, CHUNK, CHUNK), F32)],
        scratch_shapes=[pltpu.VMEM((GDN_HB, GDN_DK, GDN_DV), F32)],
        compiler_params=_cp("parallel", "arbitrary"), name=name)(qkv, qkv, qkv, p, braw, araw, alog, dtb, ng)


def _gdn_bwd(qkv, p, braw, araw, alog, dtb, ng, dog, sall, tall, *, name):
    t = qkv.shape[0]
    nc = t // CHUNK
    nh = GDN_V_HEADS

    def body(q_ref, k_ref, v_ref, z_ref, braw_ref, araw_ref, alog_ref, dtb_ref, ng_ref, dog_ref, s1_ref, s0_ref, t_ref,
             dq_ref, dk_ref, dv_ref, dz_ref, dbraw_ref, daraw_ref, dalog_ref, ddtb_ref, dng_ref, gc):
        grp = pl.program_id(0)
        i = pl.program_id(1)

        @pl.when(i == 0)
        def _():
            gc[...] = jnp.zeros_like(gc)
            dalog_ref[...] = jnp.zeros_like(dalog_ref)
            ddtb_ref[...] = jnp.zeros_like(ddtb_ref)

        @pl.when((i == 0) & (grp == 0))
        def _():
            dng_ref[...] = jnp.zeros_like(dng_ref)

        has_prev = (i < nc - 1).astype(F32)
        ngv = ng_ref[...]
        ones = jnp.ones((CHUNK, LANES), F32)
        hs = range(GDN_HB)
        kqs = [slice(j * GDN_DK, (j + 1) * GDN_DK) for j in range(GDN_HB // 2)]
        vsl = [slice(h * GDN_DV, (h + 1) * GDN_DV) for h in hs]
        cs = _gdn_chunks([q_ref[:, s] for s in kqs], [k_ref[:, s] for s in kqs], [v_ref[:, s] for s in vsl],
                         [braw_ref[h] for h in hs], [araw_ref[h] for h in hs], [alog_ref[h] for h in hs], [dtb_ref[h] for h in hs])
        tms = [t_ref[h, 0] for h in hs]
        s1 = [s1_ref[h, 0] for h in hs]
        s0 = [s0_ref[h, 0] * has_prev for h in hs]
        wv = [_dot3(tms[h], cs[h]["rv"]) for h in hs]
        wk = [_dot3(tms[h], cs[h]["rk_rhs"]) for h in hs]
        u = [wv[h] - _dg(wk[h], s0[h], NN) for h in hs]
        o = [_dg(cs[h]["qh"], s1[h], NN) for h in hs]
        dng = jnp.zeros((1, GDN_DV), F32)
        do = []
        for h in hs:
            zv = z_ref[:, vsl[h]]
            dogv = dog_ref[:, vsl[h]]
            rn = lax.rsqrt(jnp.mean(o[h] * o[h], axis=-1, keepdims=True) + EPS)
            zo = o[h] * rn
            sg = _sigmoid(zv)
            sl = zv * sg
            dng = dng + jnp.sum(dogv * zo * sl, axis=0, keepdims=True)
            dz_ref[:, vsl[h]] = (dogv * (zo * ngv) * (sg * (1.0 + zv * (1.0 - sg)))).astype(BF16)
            dzo = dogv * sl * ngv
            do.append(rn * (dzo - zo * jnp.mean(dzo * zo, axis=-1, keepdims=True)))
        dng_ref[...] += dng
        g_tot = [gc[h] + _dg(cs[h]["qh"], do[h], TN) for h in hs]
        dqh = [_dg(do[h], s1[h], NT) for h in hs]
        dke = [_dg(u[h], g_tot[h], NT) for h in hs]
        du = [_dg(cs[h]["ke"], g_tot[h], NN) for h in hs]
        gnew = [cs[h]["dec"] * g_tot[h] - _dg(wk[h], du[h], TN) for h in hs]
        dwk = [-_dg(du[h], s0[h], NT) for h in hs]
        drv = [_dot3(tms[h], du[h], TN) for h in hs]
        drk = [_dot3(tms[h], dwk[h], TN) for h in hs]
        da = [jnp.where(cs[h]["strict"], -(_dot3(drv[h], wv[h], NT) + _dot3(drk[h], wk[h], NT)), 0.0) for h in hs]
        mx = [da[h] * cs[h]["dm"] * cs[h]["kk"] for h in hs]
        aa = [mx[h] * cs[h]["beta"] for h in hs]
        colsum = [_dot_sel(ones, aa[h], TN, sel_first=False)[:, 0:1] for h in hs]
        bm = [(da[h] * cs[h]["beta"]) * cs[h]["dm"] for h in hs]
        dkh = [_dg(bm[h], cs[h]["kh"], NN) + _dg(bm[h], cs[h]["kh"], TN) for h in hs]
        dcum, dcl, dbeta = [], [], []
        for h in hs:
            c = cs[h]
            beta, kh, e, f, dec, ke = c["beta"], c["kh"], c["e"], c["f"], c["dec"], c["ke"]
            gc[h] = gnew[h]
            ddec = jnp.sum(jnp.sum(g_tot[h] * s0[h], axis=1, keepdims=True), axis=0, keepdims=True)
            dv_ref[:, vsl[h]] = beta * drv[h]
            db = jnp.sum(mx[h], axis=1, keepdims=True) + jnp.sum(drv[h] * c["v"], axis=1, keepdims=True)
            dbeta.append(db + jnp.sum(drk[h] * (e * kh), axis=1, keepdims=True))
            dkh[h] = dkh[h] + (beta * e) * drk[h] + f * dke[h]
            ef = jnp.sum(dke[h] * ke, axis=1, keepdims=True)
            dcum.append(jnp.sum(aa[h], axis=1, keepdims=True) - colsum[h] + jnp.sum(drk[h] * c["rk_rhs"], axis=1, keepdims=True) - ef)
            dcl.append(jnp.sum(ef, axis=0, keepdims=True) + ddec * dec[:, 0:1])
        dg = [_dot_sel(cs[h]["triu"], jnp.broadcast_to(dcum[h], (CHUNK, LANES)))[:, 0:1] + dcl[h] for h in hs]
        for h in hs:
            c = cs[h]
            beta, kh = c["beta"], c["kh"]
            daraw = dg[h] * (-c["ea"]) * _sigmoid(c["xs"])
            daraw_ref[h] = daraw
            dbraw_ref[h] = dbeta[h] * beta * (1.0 - beta)
            dalog_ref[h] += jnp.sum(dg[h] * c["g"], axis=0, keepdims=True)
            ddtb_ref[h] += jnp.sum(daraw, axis=0, keepdims=True)
        for j, sl in enumerate(kqs):
            c = cs[2 * j]
            dn = (dqh[2 * j] + dqh[2 * j + 1]) * (GDN_DK ** -0.5)
            dks = dkh[2 * j] + dkh[2 * j + 1]
            dq_ref[:, sl] = c["rq"] * (dn - c["qn"] * jnp.sum(dn * c["qn"], axis=-1, keepdims=True))
            dk_ref[:, sl] = c["rk"] * (dks - c["kh"] * jnp.sum(dks * c["kh"], axis=-1, keepdims=True))

    rev = lambda n: nc - 1 - n
    qs, ks, vs, zs, col, one, ngs, hd = _gdn_specs(rev)
    s1s = pl.BlockSpec((GDN_HB, 1, GDN_DK, GDN_DV), lambda g, n: (g, rev(n), 0, 0))
    s0s = pl.BlockSpec((GDN_HB, 1, GDN_DK, GDN_DV), lambda g, n: (g, jnp.maximum(rev(n) - 1, 0), 0, 0))
    ts = pl.BlockSpec((GDN_HB, 1, CHUNK, CHUNK), lambda g, n: (g, rev(n), 0, 0))
    big = jax.ShapeDtypeStruct((t, GDN_VW), F32)
    keyw = jax.ShapeDtypeStruct((t, GDN_QKW), F32)
    cols = jax.ShapeDtypeStruct((nh, t, 1), F32)
    ones_s = jax.ShapeDtypeStruct((nh, 1, 1), F32)
    return pl.pallas_call(
        body, grid=(nh // GDN_HB, nc),
        in_specs=[qs, ks, vs, zs, col, col, one, one, ngs, hd, s1s, s0s, ts],
        out_specs=[qs, qs, hd, hd, col, col, one, one, ngs],
        out_shape=[keyw, keyw, big, jax.ShapeDtypeStruct((t, GDN_VW), BF16), cols, cols, ones_s, ones_s,
                   jax.ShapeDtypeStruct((1, GDN_DV), F32)],
        scratch_shapes=[pltpu.VMEM((GDN_HB, GDN_DK, GDN_DV), F32)],
        compiler_params=_cp("arbitrary", "arbitrary"), name=name)(qkv, qkv, qkv, p, braw, araw, alog, dtb, ng, dog, sall, sall, tall)


def _gdn_layer_fwd(h, w, tag):
    p = _mm(h, w["w_in"], name=tag + "_in")
    qkv = _gdn_conv_fwd(p, w["conv_w"], name=tag + "_conv")
    braw = p[:, GDN_CONV_W + GDN_VW:GDN_CONV_W + GDN_VW + GDN_V_HEADS].T[:, :, None]
    araw = p[:, GDN_CONV_W + GDN_VW + GDN_V_HEADS:GDN_IN].T[:, :, None]
    og, sall, tall = _gdn_fwd(qkv, p, braw, araw, w["a_log"], w["dt_bias"], w["norm_g"], name=tag + "_scan")
    y = _mm(og, w["w_out"], name=tag + "_out")
    return y, (p, qkv, braw, araw, og, sall, tall)


def _gdn_layer_bwd(dy, h, saved, w, tag, sink=None):
    p, qkv, braw, araw, og, sall, tall = saved
    t = h.shape[0]
    dog = _mm(dy, w["w_out"], tb=True, name=tag + "_dog")
    dw_out, gf = _dw_out(og, dy, sink, tag + "_dwout")
    dq, dk, dv, dz, dbraw, daraw, dalog, ddtb, dng = _gdn_bwd(
        qkv, p, braw, araw, w["a_log"], w["dt_bias"], w["norm_g"], dog, sall, tall, name=tag + "_scanb")
    dpre_q, dcw_q = _gdn_conv_bwd(dq, p, w["conv_w"], 0, name=tag + "_convb_q")
    dpre_k, dcw_k = _gdn_conv_bwd(dk, p, w["conv_w"], GDN_QKW, name=tag + "_convb_k")
    dpre_v, dcw_v = _gdn_conv_bwd(dv, p, w["conv_w"], 2 * GDN_QKW, name=tag + "_convb_v")
    dconv_w = jnp.concatenate([dcw_q, dcw_k, dcw_v], axis=1)
    dp = jnp.concatenate([dpre_q, dpre_k, dpre_v, dz, dbraw[:, :, 0].T.astype(BF16), daraw[:, :, 0].T.astype(BF16),
                          jnp.zeros((t, GDN_IN_PAD - GDN_IN), BF16)], axis=1)
    dw_in = _mm(h, dp, ta=True, name=tag + "_dwin")
    dh = _mm(dp, w["w_in"], tb=True, name=tag + "_dh")
    grads = dict(w_in=dw_in[:, :GDN_IN], conv_w=dconv_w, a_log=dalog[:, 0, 0], dt_bias=ddtb[:, 0, 0], norm_g=dng[0], w_out=dw_out)
    return dh, grads, gf


MESH_ID = pl.DeviceIdType.MESH
FLAT_W = 1024
FLAT_ROWS = 13056
FLAT_TILE = 384


def _exchange(name, ins, out_shapes, plan, n_remote, n_local):
    def body(*refs):
        in_refs = refs[:len(ins)]
        out_refs = refs[len(ins):len(ins) + len(out_shapes)]
        ssem, rsem, lsem = refs[len(ins) + len(out_shapes):]
        x, y, c = lax.axis_index("x"), lax.axis_index("y"), lax.axis_index("c")
        stages, local_copies = plan(x, y, c, in_refs, out_refs)
        assert sum(len(s) for s in stages) == n_remote and len(local_copies) == n_local
        locs = [pltpu.make_async_copy(s, d, lsem.at[i]) for i, (s, d) in enumerate(local_copies)]
        for cp in locs:
            cp.start()
        sent = []
        k = 0
        for stage in stages:
            arrivals = []
            for src, dst, peer, landing in stage:
                cp = pltpu.make_async_remote_copy(src_ref=src, dst_ref=dst, send_sem=ssem.at[k], recv_sem=rsem.at[k],
                                                  device_id=peer, device_id_type=MESH_ID)
                cp.start()
                sent.append(cp)
                arrivals.append(pltpu.make_async_remote_copy(src_ref=src, dst_ref=landing, send_sem=ssem.at[k],
                                                             recv_sem=rsem.at[k], device_id=peer, device_id_type=MESH_ID))
                k += 1
            for cp in arrivals:
                cp.wait_recv()
        for cp in sent:
            cp.wait_send()
        for cp in locs:
            cp.wait()

    hbm = pl.BlockSpec(memory_space=pl.ANY)
    return pl.pallas_call(
        body, in_specs=[hbm] * len(ins), out_specs=[hbm] * len(out_shapes), out_shape=out_shapes,
        scratch_shapes=[pltpu.SemaphoreType.DMA((n_remote,)), pltpu.SemaphoreType.DMA((n_remote,)),
                        pltpu.SemaphoreType.DMA((max(n_local, 1),))],
        name=name)(*ins)


def _other_chips(x, y):
    return [(1 - x, y), (x, 1 - y), (1 - x, 1 - y)]


def _all8_gather(a, *, name):
    def plan(x, y, c, ins, outs):
        (src,), (dst,) = ins, outs
        me = 4 * x + 2 * y + c
        stage = []
        for fx, fy, fc in [(0, 0, 1), (0, 1, 0), (0, 1, 1), (1, 0, 0), (1, 0, 1), (1, 1, 0), (1, 1, 1)]:
            px, py, pc = (1 - x if fx else x), (1 - y if fy else y), (1 - c if fc else c)
            stage.append((src, dst.at[me], (px, py, pc), dst.at[4 * px + 2 * py + pc]))
        return [stage], [(src, dst.at[me])]

    return _exchange(name, [a], [jax.ShapeDtypeStruct((8,) + a.shape, a.dtype)], plan, 7, 1)[0]


def _chip_gather(flat, *, name):
    rows = flat.shape[0]
    half = rows // 2

    def plan(x, y, c, ins, outs):
        (src,), (dst,) = ins, outs
        me = 2 * x + y
        mine = pl.ds(c * half, half)
        theirs = pl.ds((1 - c) * half, half)
        ici = [(src.at[mine], dst.at[me, mine], (px, py, c), dst.at[2 * px + py, mine]) for px, py in _other_chips(x, y)]
        d2d = [(dst.at[2 * px + py, mine], dst.at[2 * px + py, mine], (x, y, 1 - c), dst.at[2 * px + py, theirs])
               for px, py in _other_chips(x, y)]
        return [ici, d2d], []

    return _exchange(name, [flat], [jax.ShapeDtypeStruct((4,) + flat.shape, flat.dtype)], plan, 6, 0)[0]


def _add_sibling(gf, buf_a, core, *, name):
    _, rows, w = gf.shape
    half = rows // 2
    nb = half // FLAT_TILE

    def body(c_ref, g_ref, a_ref, o_ref):
        o_ref[...] = (g_ref[...] + a_ref[...]).astype(BF16)

    blk = (1, FLAT_TILE, w)
    return pl.pallas_call(
        body,
        grid_spec=pltpu.PrefetchScalarGridSpec(
            num_scalar_prefetch=1, grid=(4, nb),
            in_specs=[pl.BlockSpec(blk, lambda s, i, c_ref: (s, c_ref[0] * nb + i, 0)), pl.BlockSpec(blk, lambda s, i, c_ref: (s, i, 0))],
            out_specs=pl.BlockSpec(blk, lambda s, i, c_ref: (s, i, 0))),
        out_shape=jax.ShapeDtypeStruct((4, half, w), BF16), compiler_params=_cp("parallel", "parallel"), name=name)(core, gf, buf_a)


def _sum_chips(hsum, buf_b, chip, *, name):
    _, half, w = hsum.shape
    nb = half // FLAT_TILE

    def body(c_ref, h_ref, b0_ref, b1_ref, b2_ref, b3_ref, o_ref):
        me = c_ref[0]
        own = h_ref[0].astype(F32)
        acc = None
        for j, b_ref in enumerate((b0_ref, b1_ref, b2_ref, b3_ref)):
            term = jnp.where(me == j, own, b_ref[0].astype(F32))
            acc = term if acc is None else acc + term
        o_ref[...] = acc

    blk = (1, FLAT_TILE, w)

    def other(j):
        return pl.BlockSpec(blk, lambda i, c_ref: (jnp.where(c_ref[0] == j, (j + 1) % 4, j), i, 0))

    return pl.pallas_call(
        body,
        grid_spec=pltpu.PrefetchScalarGridSpec(
            num_scalar_prefetch=1, grid=(nb,),
            in_specs=[pl.BlockSpec(blk, lambda i, c_ref: (c_ref[0], i, 0))] + [other(j) for j in range(4)],
            out_specs=pl.BlockSpec((FLAT_TILE, w), lambda i, c_ref: (i, 0))),
        out_shape=jax.ShapeDtypeStruct((half, w), F32), compiler_params=_cp("parallel"), name=name)(chip, hsum, buf_b, buf_b, buf_b, buf_b)


def _sum_slots(buf, *, name):
    n, rows, w = buf.shape
    tr = _pick(rows, (FLAT_TILE, 8))

    def body(b_ref, o_ref):
        acc = b_ref[0]
        for s in range(1, n):
            acc = acc + b_ref[s]
        o_ref[...] = acc

    return pl.pallas_call(body, grid=(rows // tr,), in_specs=[pl.BlockSpec((n, tr, w), lambda i: (0, i, 0))],
                          out_specs=pl.BlockSpec((tr, w), lambda i: (i, 0)), out_shape=jax.ShapeDtypeStruct((rows, w), F32),
                          compiler_params=_cp("parallel"), name=name)(buf)


def _reduce_scatter(gf, core, chip, *, tag):
    _, rows, w = gf.shape
    half = rows // 2

    def plan_a(x, y, c, ins, outs):
        (src,), (dst,) = ins, outs
        return [[(src.at[:, pl.ds((1 - c) * half, half)], dst, (x, y, 1 - c), dst)]], []

    buf_a = _exchange(tag + "_sibling", [gf], [jax.ShapeDtypeStruct((4, half, w), F32)], plan_a, 1, 0)[0]
    hsum = _add_sibling(gf, buf_a, core, name=tag + "_add_sibling")

    def plan_b(x, y, c, ins, outs):
        (src,), (dst,) = ins, outs
        me = 2 * x + y
        stage = [(src.at[2 * px + py], dst.at[me], (px, py, c), dst.at[2 * px + py]) for px, py in _other_chips(x, y)]
        return [stage], []

    buf_b = _exchange(tag + "_chips", [hsum], [jax.ShapeDtypeStruct((4, half, w), BF16)], plan_b, 3, 0)[0]
    mine = _sum_chips(hsum, buf_b, chip, name=tag + "_sum_chips")

    def plan_c(x, y, c, ins, outs):
        (src,), (dst,) = ins, outs
        return [[(src, dst, (x, y, 1 - c), dst)]], []

    theirs = _exchange(tag + "_halves", [mine], [jax.ShapeDtypeStruct((half, w), F32)], plan_c, 1, 0)[0]
    first = core[0] == 0
    return jnp.concatenate([jnp.where(first, mine, theirs), jnp.where(first, theirs, mine)], axis=0)


WEIGHTS = ["ada_w", "ada_b", "norm_pre_g", "norm_post_g", "gla_w_in", "gla_w_gate_up", "gla_b_gate", "gla_head_g",
           "gla_w_out", "mla_w_in", "mla_q_norm_g", "mla_w_uq", "mla_kv_norm_g", "mla_w_ukv", "mla_w_out", "gdn_w_in",
           "gdn_conv_w", "gdn_a_log", "gdn_dt_bias", "gdn_norm_g", "gdn_w_out", "mlp_w_up", "mlp_w_down"]
PACK_BF16 = [("mlp_w_up", 2), ("mlp_w_down", 1), ("gdn_w_out", 1), ("gla_w_out", 1), ("mla_w_out", 1),
             ("gla_w_in", 2), ("mla_w_in", 1), ("mla_w_uq", 2), ("mla_w_ukv", 2), ("gdn_w_in", 2)]
N_DIRECT = 5
PACK_F32 = [("norm_pre_g", 2), ("norm_post_g", 2), ("gla_w_gate_up", 2), ("gla_b_gate", 1), ("gla_head_g", 1), ("gdn_conv_w", 2)]
REPLICATED_SMALL = ["mla_q_norm_g", "mla_kv_norm_g", "gdn_a_log", "gdn_dt_bias", "gdn_norm_g"]
MIXERS = ["gla", "mla", "gdn"]


def _silu_rows(a, *, name):
    def body(a_ref, o_ref):
        v = a_ref[...]
        o_ref[...] = v * _sigmoid(v)

    return pl.pallas_call(body, out_shape=jax.ShapeDtypeStruct(a.shape, F32), name=name)(a)


SMALL_ROWS = 16


def _piece_rows(size, mult):
    assert size % FLAT_W == 0
    return -(-(size // FLAT_W) // mult) * mult


def _to_rows(a, lead, mult):
    n = math.prod(a.shape[len(lead):])
    r = a.reshape(lead + (n // FLAT_W, FLAT_W))
    extra = _piece_rows(n, mult) - n // FLAT_W
    return jnp.pad(r, [(0, 0)] * len(lead) + [(0, extra), (0, 0)]) if extra else r


def _small_to_rows(parts, lead):
    flat = jnp.concatenate([p.reshape(lead + (-1,)) for p in parts], axis=-1)
    pad = SMALL_ROWS * FLAT_W - flat.shape[-1]
    return jnp.pad(flat, [(0, 0)] * len(lead) + [(0, pad)]).reshape(lead + (SMALL_ROWS, FLAT_W))


def _small_from_rows(rows, shards, lead):
    flat = rows.reshape(lead + (-1,))
    out, off = {}, 0
    for n, _ in PACK_F32:
        out[n] = flat[..., off:off + shards[n].size].reshape(lead + shards[n].shape)
        off += shards[n].size
    return out


def _pack_weights(shards):
    parts = [_to_rows(shards[n].astype(BF16), (), 16) for n, _ in PACK_BF16]
    small = _small_to_rows([shards[n] for n, _ in PACK_F32], ())
    parts.append(lax.bitcast_convert_type(small, BF16).reshape(2 * SMALL_ROWS, FLAT_W))
    flat = jnp.concatenate(parts, axis=0)
    return jnp.pad(flat, ((0, FLAT_ROWS - flat.shape[0]), (0, 0)))


def _unpack_weights(gathered, shards, chip):
    full, off = {}, 0
    for n, ax in PACK_BF16:
        size = shards[n].size
        seg = gathered[:, off:off + size // FLAT_W].reshape((4,) + shards[n].shape)
        own = shards[n].astype(BF16)
        full[n] = jnp.concatenate([jnp.where(chip == j, own, seg[j]) for j in range(4)], axis=ax)
        off += _piece_rows(size, 16)
    small = lax.bitcast_convert_type(gathered[:, off:off + 2 * SMALL_ROWS].reshape(4, SMALL_ROWS, FLAT_W, 2), F32)
    for (n, ax), seg in zip(PACK_F32, _small_from_rows(small, shards, (4,)).values()):
        full[n] = jnp.concatenate([jnp.where(chip == j, shards[n], seg[j]) for j in range(4)], axis=ax)
    return full


def _grad_layout(shards):
    layout, off = {}, 0
    for n, _ in PACK_BF16:
        layout[n] = (off, shards[n].size // shards[n].shape[0] // FLAT_W)
        off += _piece_rows(shards[n].size, 8)
    layout["small"] = (off, SMALL_ROWS)
    return layout


def _pack_grads(gf, grads, layout):
    by_chip = lambda g, ax: jnp.stack(jnp.split(g.astype(F32), 4, axis=ax - 1))
    parts = []
    for n, ax in PACK_BF16[N_DIRECT:]:
        rows = sum(g.size for g in grads[n]) // (4 * FLAT_W)
        parts += [by_chip(g, ax).reshape(4, -1, FLAT_W) for g in grads[n]]
        if _piece_rows(rows * FLAT_W, 8) > rows:
            parts.append(jnp.zeros((4, _piece_rows(rows * FLAT_W, 8) - rows, FLAT_W), F32))
    parts.append(_small_to_rows([jnp.stack([by_chip(g, ax) for g in grads[n]], axis=1) for n, ax in PACK_F32], (4,)))
    first = layout[PACK_BF16[N_DIRECT][0]][0]
    rest = jnp.concatenate(parts, axis=1)
    assert first + rest.shape[1] == layout["small"][0] + SMALL_ROWS
    return lax.dynamic_update_slice(gf, rest, (0, first, 0))


def _unpack_grads(reduced, shards):
    out, off = {}, 0
    for n, _ in PACK_BF16:
        size = shards[n].size
        out[n] = reduced[off:off + size // FLAT_W].reshape(shards[n].shape)
        off += _piece_rows(size, 8)
    out.update(_small_from_rows(reduced[off:off + SMALL_ROWS], shards, ()))
    return out


def _mixer_weights(kind, j, full, rep):
    if kind == "gla":
        return dict(w_in=jnp.pad(full["gla_w_in"][j], ((0, 0), (0, GLA_IN_PAD - GLA_IN))),
                    wg=jnp.pad(full["gla_w_gate_up"][j], ((0, LANES - GLA_RANK), (0, 0))),
                    bg=full["gla_b_gate"][j][None], hg=full["gla_head_g"][j][None], w_out=full["gla_w_out"][j])
    if kind == "mla":
        return dict(w_in=jnp.pad(full["mla_w_in"][j], ((0, 0), (0, MLA_IN_PAD - MLA_IN))), q_norm_g=rep["mla_q_norm_g"][j][None],
                    w_uq=full["mla_w_uq"][j], kv_norm_g=rep["mla_kv_norm_g"][j][None], w_ukv=full["mla_w_ukv"][j],
                    w_out=full["mla_w_out"][j])
    return dict(w_in=jnp.pad(full["gdn_w_in"][j], ((0, 0), (0, GDN_IN_PAD - GDN_IN))), conv_w=full["gdn_conv_w"][j],
                a_log=rep["gdn_a_log"][j][:, None, None], dt_bias=rep["gdn_dt_bias"][j][:, None, None],
                norm_g=rep["gdn_norm_g"][j][None], w_out=full["gdn_w_out"][j])


def _layer_fwd(xin, mod, gains, kind, mw, w_up, w_down, pos, tag):
    sh_m, sc_m, gt_m, sh_f, sc_f, gt_f = mod
    pre0, pre1, post0, post1 = gains
    h = _premod_fwd(xin, pre0, sc_m, sh_m, name=tag + "_pre0")
    if kind == "gla":
        y, saved = _gla_layer_fwd(h, mw, tag + "_gla")
    elif kind == "mla":
        y, saved = _mla_layer_fwd(h, pos, mw, tag + "_mla")
    else:
        y, saved = _gdn_layer_fwd(h, mw, tag + "_gdn")
    x1 = _postres_fwd(xin, y, post0, gt_m, name=tag + "_post0")
    h2 = _premod_fwd(x1, pre1, sc_f, sh_f, name=tag + "_pre1")
    act = _mm(h2, w_up, out_dtype=BF16, epi="relu2", name=tag + "_up")
    y2 = _mm(act, w_down, name=tag + "_down")
    x2 = _postres_fwd(x1, y2, post1, gt_f, name=tag + "_post1")
    return x2, (xin, h, y, saved, x1, h2, act, y2)


def _layer_bwd(g2, kept, mod, gains, kind, mw, w_up, w_down, tag, gf, rows):
    xin, h, y, saved, x1, h2, act, y2 = kept
    sh_m, sc_m, gt_m, sh_f, sc_f, gt_f = mod
    pre0, pre1, post0, post1 = gains
    dy2, dpost1, dgt_f = _postres_bwd(g2, y2, post1, gt_f, name=tag + "_post1_b")
    du = _mm(dy2, w_down, tb=True, out_dtype=BF16, epi="dact", aux=act, name=tag + "_du")
    gf = _mm(act, dy2, ta=True, into=(gf, rows[1], "rows"), name=tag + "_dwdown")
    gf = _mm(h2, du, ta=True, into=(gf, rows[0], "cols"), name=tag + "_dwup")
    dh2 = _mm(du, w_up, tb=True, name=tag + "_dh2")
    g1, dpre1, dsc_f, dsh_f = _premod_bwd(dh2, x1, pre1, sc_f, g2, name=tag + "_pre1_b")
    dy, dpost0, dgt_m = _postres_bwd(g1, y, post0, gt_m, name=tag + "_post0_b")
    mixer_bwd = dict(gla=_gla_layer_bwd, mla=_mla_layer_bwd, gdn=_gdn_layer_bwd)[kind]
    dh, mg, gf = mixer_bwd(dy, h, saved, mw, tag + "_" + kind, sink=(gf, rows[2]))
    g0, dpre0, dsc_m, dsh_m = _premod_bwd(dh, xin, pre0, sc_m, g1, name=tag + "_pre0_b")
    dmod = jnp.concatenate([dsh_m, dsc_m, dgt_m, dsh_f, dsc_f, dgt_f], axis=1)
    return g0, dmod, jnp.concatenate([dpre0, dpre1], axis=0), jnp.concatenate([dpost0, dpost1], axis=0), mg, gf


def kernel(x, c, positions, ada_w, ada_b, norm_pre_g, norm_post_g, gla_w_in, gla_w_gate_up, gla_b_gate, gla_head_g, gla_w_out, mla_w_in, mla_q_norm_g, mla_w_uq, mla_kv_norm_g, mla_w_ukv, mla_w_out, gdn_w_in, gdn_conv_w, gdn_a_log, gdn_dt_bias, gdn_norm_g, gdn_w_out, mlp_w_up, mlp_w_down, loss_target, m_ada_w, m_ada_b, m_norm_pre_g, m_norm_post_g, m_gla_w_in, m_gla_w_gate_up, m_gla_b_gate, m_gla_head_g, m_gla_w_out, m_mla_w_in, m_mla_q_norm_g, m_mla_w_uq, m_mla_kv_norm_g, m_mla_w_ukv, m_mla_w_out, m_gdn_w_in, m_gdn_conv_w, m_gdn_a_log, m_gdn_dt_bias, m_gdn_norm_g, m_gdn_w_out, m_mlp_w_up, m_mlp_w_down, v_ada_w, v_ada_b, v_norm_pre_g, v_norm_post_g, v_gla_w_in, v_gla_w_gate_up, v_gla_b_gate, v_gla_head_g, v_gla_w_out, v_mla_w_in, v_mla_q_norm_g, v_mla_w_uq, v_mla_kv_norm_g, v_mla_w_ukv, v_mla_w_out, v_gdn_w_in, v_gdn_conv_w, v_gdn_a_log, v_gdn_dt_bias, v_gdn_norm_g, v_gdn_w_out, v_mlp_w_up, v_mlp_w_down):
    w = dict(ada_w=ada_w, ada_b=ada_b, norm_pre_g=norm_pre_g, norm_post_g=norm_post_g, gla_w_in=gla_w_in,
             gla_w_gate_up=gla_w_gate_up, gla_b_gate=gla_b_gate, gla_head_g=gla_head_g, gla_w_out=gla_w_out, mla_w_in=mla_w_in,
             mla_q_norm_g=mla_q_norm_g, mla_w_uq=mla_w_uq, mla_kv_norm_g=mla_kv_norm_g, mla_w_ukv=mla_w_ukv, mla_w_out=mla_w_out,
             gdn_w_in=gdn_w_in, gdn_conv_w=gdn_conv_w, gdn_a_log=gdn_a_log, gdn_dt_bias=gdn_dt_bias, gdn_norm_g=gdn_norm_g,
             gdn_w_out=gdn_w_out, mlp_w_up=mlp_w_up, mlp_w_down=mlp_w_down)
    m = dict(zip(WEIGHTS, [m_ada_w, m_ada_b, m_norm_pre_g, m_norm_post_g, m_gla_w_in, m_gla_w_gate_up, m_gla_b_gate, m_gla_head_g,
                           m_gla_w_out, m_mla_w_in, m_mla_q_norm_g, m_mla_w_uq, m_mla_kv_norm_g, m_mla_w_ukv, m_mla_w_out,
                           m_gdn_w_in, m_gdn_conv_w, m_gdn_a_log, m_gdn_dt_bias, m_gdn_norm_g, m_gdn_w_out, m_mlp_w_up, m_mlp_w_down]))
    v = dict(zip(WEIGHTS, [v_ada_w, v_ada_b, v_norm_pre_g, v_norm_post_g, v_gla_w_in, v_gla_w_gate_up, v_gla_b_gate, v_gla_head_g,
                           v_gla_w_out, v_mla_w_in, v_mla_q_norm_g, v_mla_w_uq, v_mla_kv_norm_g, v_mla_w_ukv, v_mla_w_out,
                           v_gdn_w_in, v_gdn_conv_w, v_gdn_a_log, v_gdn_dt_bias, v_gdn_norm_g, v_gdn_w_out, v_mlp_w_up, v_mlp_w_down]))
    t = x.shape[1]
    ix, iy, ic = lax.axis_index("x"), lax.axis_index("y"), lax.axis_index("c")
    me = 4 * ix + 2 * iy + ic
    chip = 2 * ix + iy
    ada_cols = ada_w.shape[2]

    full = _unpack_weights(_chip_gather(_pack_weights(w), name="gather_weights"), w, chip)

    cond8 = _silu_rows(jnp.pad(c, ((0, 7), (0, 0))), name="cond_silu")
    cond16 = jnp.pad(_all8_gather(cond8, name="gather_cond")[:, 0, :], ((0, 8), (0, 0)))
    mod_cols = []
    for layer in range(DEPTH):
        bias = jnp.broadcast_to(lax.dynamic_slice_in_dim(ada_b[layer], chip * ada_cols, ada_cols)[None], (16, ada_cols))
        mod_cols.append(_mm(cond16, ada_w[layer], epi="add", aux=bias, name=f"ada{layer}")[:8])
    mod_all = _all8_gather(jnp.stack(mod_cols).reshape(DEPTH * 8, ada_cols), name="gather_mod")
    mod = jnp.concatenate([lax.dynamic_slice_in_dim(mod_all[2 * j].reshape(DEPTH, 8, ada_cols), me, 1, axis=1)[:, 0]
                           for j in range(4)], axis=1)

    def layer_args(layer):
        kind, j = MIXERS[layer % 3], layer // 3
        mods = [mod[layer, i * D_MODEL:(i + 1) * D_MODEL][None] for i in range(N_MOD)]
        gains = (full["norm_pre_g"][layer, 0:1], full["norm_pre_g"][layer, 1:2], full["norm_post_g"][layer, 0:1],
                 full["norm_post_g"][layer, 1:2])
        return kind, j, mods, gains, _mixer_weights(kind, j, full, w)

    xs = x[0]
    kept = []
    for layer in range(DEPTH):
        kind, j, mods, gains, mw = layer_args(layer)
        xs, keep = _layer_fwd(xs, mods, gains, kind, mw, full["mlp_w_up"][layer], full["mlp_w_down"][layer], positions[0], f"l{layer}")
        kept.append(keep)
    loss_row, g = _loss_head(xs, loss_target[0], name="loss_head")
    loss = lax.psum(loss_row[0, 0], ("x", "y", "c"))

    grads = {n: [None] * w[n].shape[0] for n, _ in PACK_BF16[N_DIRECT:] + PACK_F32}
    rep_grads = {}
    dmods = [None] * DEPTH
    layout = _grad_layout(w)
    row_of = lambda n, idx: layout[n][0] + idx * layout[n][1]
    gf = lax.empty((4, FLAT_ROWS, FLAT_W), F32)
    for layer in reversed(range(DEPTH)):
        kind, j, mods, gains, mw = layer_args(layer)
        rows = (row_of("mlp_w_up", layer), row_of("mlp_w_down", layer), row_of(kind + "_w_out", j))
        g, dmods[layer], dpre, dpost, mg, gf = _layer_bwd(
            g, kept[layer], mods, gains, kind, mw, full["mlp_w_up"][layer], full["mlp_w_down"][layer], f"l{layer}", gf, rows)
        grads["norm_pre_g"][layer], grads["norm_post_g"][layer] = dpre, dpost
        for key, val in mg.items():
            name = kind + "_" + key
            if name in grads:
                grads[name][j] = val
            elif name in REPLICATED_SMALL:
                rep_grads[name] = val[None]

    rep_flat = jnp.concatenate([rep_grads[n].reshape(-1) for n in REPLICATED_SMALL])
    dbuf = jnp.concatenate([jnp.concatenate(dmods, axis=0), jnp.pad(rep_flat, (0, N_MOD * D_MODEL - rep_flat.shape[0]))[None],
                            jnp.zeros((3, N_MOD * D_MODEL), F32)], axis=0)
    dall = _all8_gather(dbuf, name="gather_dmod")
    dsum = _sum_slots(dall, name="sum_dmod")
    out_grads = {"ada_b": dsum[:DEPTH]}
    off = 0
    for n in REPLICATED_SMALL:
        out_grads[n] = dsum[DEPTH, off:off + w[n].size].reshape(w[n].shape)
        off += w[n].size
    dada = []
    for layer in range(DEPTH):
        dm16 = jnp.pad(lax.dynamic_slice_in_dim(dall[:, layer, :], chip * ada_cols, ada_cols, axis=1), ((0, 8), (0, 0)))
        dada.append(_mm(cond16, dm16, ta=True, name=f"dada{layer}"))
    out_grads["ada_w"] = jnp.stack(dada)

    reduced = _reduce_scatter(_pack_grads(gf, grads, layout), ic.reshape(1).astype(jnp.int32),
                              chip.reshape(1).astype(jnp.int32), tag="reduce_grads")
    out_grads.update(_unpack_grads(reduced, w))

    deltas, new_m, new_v = {}, {}, {}
    for n in WEIGHTS:
        deltas[n], new_m[n], new_v[n] = _adamw(w[n], out_grads[n], m[n], v[n], name="adamw_" + n)
    return (loss, g[None], *[out_grads[n] for n in WEIGHTS], *[deltas[n] for n in WEIGHTS],
            *[new_m[n] for n in WEIGHTS], *[new_v[n] for n in WEIGHTS])
```

```python
import functools
import math

import jax
import jax.numpy as jnp
from jax import lax
from jax.experimental import pallas as pl
from jax.experimental.pallas import tpu as pltpu

F32 = jnp.float32
BF16 = jnp.bfloat16

D_MODEL = 1024
DEPTH = 4
CHUNK = 64
EPS = 1e-6
NEG_INF = -1e30
N_MOD = 6

GLA_HEADS, GLA_DK, GLA_DV, GLA_RANK = 4, 128, 256, 16
GLA_KW, GLA_VW = GLA_HEADS * GLA_DK, GLA_HEADS * GLA_DV
GLA_IN = 2 * GLA_KW + 2 * GLA_VW + GLA_RANK
GLA_IN_PAD = 3200

MLA_HEADS, MLA_NOPE, MLA_ROPE, MLA_V = 16, 64, 32, 64
MLA_Q_RANK, MLA_KV_RANK = 384, 256
MLA_IN = MLA_Q_RANK + MLA_KV_RANK + MLA_ROPE
MLA_IN_PAD = 768
ROPE_THETA = 10000.0
MLA_QK = MLA_NOPE + MLA_ROPE
LANES = 128

GDN_K_HEADS, GDN_V_HEADS, GDN_DK, GDN_DV, GDN_CONV = 8, 16, 128, 128, 4
GDN_QKW, GDN_VW = GDN_K_HEADS * GDN_DK, GDN_V_HEADS * GDN_DV
GDN_CONV_W = 2 * GDN_QKW + GDN_VW
GDN_IN = GDN_CONV_W + GDN_VW + 2 * GDN_V_HEADS
GDN_IN_PAD = 6400

ADAM_LR, ADAM_B1, ADAM_B2, ADAM_EPS, ADAM_WD, ADAM_STEP = 0.001, 0.9, 0.999, 1e-08, 0.01, 10

VMEM_LIMIT = 56 * 1024 * 1024

NN = ((1,), (0,))
NT = ((1,), (1,))
TN = ((0,), (0,))


def _cp(*sem):
    return pltpu.CompilerParams(dimension_semantics=sem, vmem_limit_bytes=VMEM_LIMIT)


def _pick(n, cands):
    for c in cands:
        if n % c == 0:
            return c
    return n


def _dg(a, b, dims=NN):
    return lax.dot_general(a.astype(BF16), b.astype(BF16), (dims, ((), ())), preferred_element_type=F32)


def _dot3(a, b, dims=NN):
    ah = a.astype(BF16)
    al = (a - ah.astype(F32)).astype(BF16)
    bh = b.astype(BF16)
    bl = (b - bh.astype(F32)).astype(BF16)
    d = lambda u, v: lax.dot_general(u, v, (dims, ((), ())), preferred_element_type=F32)
    return d(ah, bh) + (d(ah, bl) + d(al, bh))


def _sigmoid(x):
    return 1.0 / (1.0 + jnp.exp(-x))


def _softplus(x):
    return jnp.maximum(x, 0.0) + jnp.log(1.0 + jnp.exp(-jnp.abs(x)))


def _iota2(shape, dim):
    return lax.broadcasted_iota(jnp.int32, shape, dim)


def _mm(a, b, *, ta=False, tb=False, out_dtype=F32, epi=None, aux=None, into=None, name):
    m = a.shape[1] if ta else a.shape[0]
    k = a.shape[0] if ta else a.shape[1]
    n = b.shape[0] if tb else b.shape[1]
    assert k == (b.shape[1] if tb else b.shape[0]), (a.shape, b.shape, ta, tb)
    m_tile = m // 4 if into is not None and into[2] == "rows" else m
    tm = _pick(m_tile, (1024, 512, 384, 256, 128))
    tn = _pick(n, (1024, 640, 512, 768, 384, 256, 128))
    tk = _pick(k, (1024, 640, 512, 768, 384, 256, 128))
    nk = k // tk
    dims = ((0 if ta else 1,), (1 if tb else 0,))

    def finish(r, x_ref, o_ref):
        if epi == "relu2":
            r = jnp.square(jnp.maximum(r, 0.0))
        elif epi == "dact":
            r = r * (2.0 * jnp.sqrt(x_ref[...].astype(F32)))
        elif epi == "add":
            r = r + x_ref[...]
        o_ref[...] = r.astype(out_dtype)

    n_in = 2 + (aux is not None) + (into is not None)

    def body(*refs):
        a_ref, b_ref = refs[:2]
        x_ref = refs[2] if aux is not None else None
        o_ref = refs[n_in]
        if nk == 1:
            finish(_dg(a_ref[...], b_ref[...], dims), x_ref, o_ref)
            return
        acc = refs[-1]
        kk = pl.program_id(2)

        @pl.when(kk == 0)
        def _():
            acc[...] = jnp.zeros_like(acc)

        acc[...] += _dg(a_ref[...], b_ref[...], dims)

        @pl.when(kk == nk - 1)
        def _():
            finish(acc[...], x_ref, o_ref)

    a_spec = pl.BlockSpec((tk, tm), lambda i, j, q: (q, i)) if ta else pl.BlockSpec((tm, tk), lambda i, j, q: (i, q))
    b_spec = pl.BlockSpec((tn, tk), lambda i, j, q: (j, q)) if tb else pl.BlockSpec((tk, tn), lambda i, j, q: (q, j))
    o_spec = pl.BlockSpec((tm, tn), lambda i, j, q: (i, j))
    in_specs = [a_spec, b_spec] + ([o_spec] if aux is not None else [])
    args = (a, b) + ((aux,) if aux is not None else ())
    out_shape = jax.ShapeDtypeStruct((m, n), out_dtype)
    aliases = {}
    if into is not None:
        dst, row0, axis = into
        assert dst.dtype == out_dtype and row0 % tm == 0 and tn == FLAT_W and n == (FLAT_W if axis == "rows" else 4 * FLAT_W)
        per = m_tile // tm
        if axis == "rows":
            o_spec = pl.BlockSpec((None, tm, tn), lambda i, j, q: (i // per, row0 // tm + i % per, 0))
        else:
            o_spec = pl.BlockSpec((None, tm, tn), lambda i, j, q: (j, row0 // tm + i, 0))
        in_specs.append(pl.BlockSpec(memory_space=pl.ANY))
        args += (dst,)
        out_shape = jax.ShapeDtypeStruct(dst.shape, dst.dtype)
        aliases = {n_in - 1: 0}
    return pl.pallas_call(
        body, grid=(m // tm, n // tn, nk), in_specs=in_specs, out_specs=o_spec, out_shape=out_shape,
        scratch_shapes=[pltpu.VMEM((tm, tn), F32)] if nk > 1 else [], input_output_aliases=aliases,
        compiler_params=_cp("parallel", "parallel", "arbitrary"), name=name)(*args)


def _row_tile(t):
    return _pick(t, (512, 256, 128, 64, 8))


def _premod_fwd(x, g, sc, sh, *, name):
    t, c = x.shape
    tr = _row_tile(t)

    def body(x_ref, g_ref, sc_ref, sh_ref, h_ref):
        xv = x_ref[...]
        r = lax.rsqrt(jnp.mean(xv * xv, axis=-1, keepdims=True) + EPS)
        h_ref[...] = (((xv * r) * g_ref[...]) * (1.0 + sc_ref[...]) + sh_ref[...]).astype(BF16)

    row = pl.BlockSpec((tr, c), lambda i: (i, 0))
    vec = pl.BlockSpec((1, c), lambda i: (0, 0))
    return pl.pallas_call(body, grid=(t // tr,), in_specs=[row, vec, vec, vec], out_specs=row,
                          out_shape=jax.ShapeDtypeStruct((t, c), BF16), compiler_params=_cp("parallel"), name=name)(x, g, sc, sh)


def _premod_bwd(dh, x, g, sc, gin, *, name):
    t, c = x.shape
    tr = _row_tile(t)

    def body(dh_ref, x_ref, g_ref, sc_ref, gin_ref, gout_ref, dg_ref, dsc_ref, dsh_ref):
        @pl.when(pl.program_id(0) == 0)
        def _():
            dg_ref[...] = jnp.zeros_like(dg_ref)
            dsc_ref[...] = jnp.zeros_like(dsc_ref)
            dsh_ref[...] = jnp.zeros_like(dsh_ref)

        xv = x_ref[...]
        dhv = dh_ref[...].astype(F32)
        gv = g_ref[...]
        one_sc = 1.0 + sc_ref[...]
        r = lax.rsqrt(jnp.mean(xv * xv, axis=-1, keepdims=True) + EPS)
        nv = xv * r
        dsh_ref[...] += jnp.sum(dhv, axis=0, keepdims=True)
        dsc_ref[...] += jnp.sum(dhv * (nv * gv), axis=0, keepdims=True)
        dg_ref[...] += jnp.sum(dhv * nv * one_sc, axis=0, keepdims=True)
        dn = dhv * gv * one_sc
        dx = r * (dn - nv * jnp.mean(dn * nv, axis=-1, keepdims=True))
        gout_ref[...] = gin_ref[...] + dx

    row = pl.BlockSpec((tr, c), lambda i: (i, 0))
    vec = pl.BlockSpec((1, c), lambda i: (0, 0))
    vs = jax.ShapeDtypeStruct((1, c), F32)
    return pl.pallas_call(body, grid=(t // tr,), in_specs=[row, row, vec, vec, row], out_specs=[row, vec, vec, vec],
                          out_shape=[jax.ShapeDtypeStruct((t, c), F32), vs, vs, vs],
                          compiler_params=_cp("arbitrary"), name=name)(dh, x, g, sc, gin)


def _postres_fwd(x, y, g, gt, *, name):
    t, c = x.shape
    tr = _row_tile(t)

    def body(x_ref, y_ref, g_ref, gt_ref, o_ref):
        yv = y_ref[...]
        r = lax.rsqrt(jnp.mean(yv * yv, axis=-1, keepdims=True) + EPS)
        o_ref[...] = x_ref[...] + gt_ref[...] * ((yv * r) * g_ref[...])

    row = pl.BlockSpec((tr, c), lambda i: (i, 0))
    vec = pl.BlockSpec((1, c), lambda i: (0, 0))
    return pl.pallas_call(body, grid=(t // tr,), in_specs=[row, row, vec, vec], out_specs=row,
                          out_shape=jax.ShapeDtypeStruct((t, c), F32), compiler_params=_cp("parallel"), name=name)(x, y, g, gt)


def _postres_bwd(gout, y, g, gt, *, name):
    t, c = y.shape
    tr = _row_tile(t)

    def body(go_ref, y_ref, g_ref, gt_ref, dy_ref, dg_ref, dgt_ref):
        @pl.when(pl.program_id(0) == 0)
        def _():
            dg_ref[...] = jnp.zeros_like(dg_ref)
            dgt_ref[...] = jnp.zeros_like(dgt_ref)

        yv = y_ref[...]
        gov = go_ref[...]
        gv = g_ref[...]
        gtv = gt_ref[...]
        r = lax.rsqrt(jnp.mean(yv * yv, axis=-1, keepdims=True) + EPS)
        z = yv * r
        dgt_ref[...] += jnp.sum(gov * (z * gv), axis=0, keepdims=True)
        dg_ref[...] += jnp.sum(gov * gtv * z, axis=0, keepdims=True)
        dz = gov * gtv * gv
        dy_ref[...] = (r * (dz - z * jnp.mean(dz * z, axis=-1, keepdims=True))).astype(BF16)

    row = pl.BlockSpec((tr, c), lambda i: (i, 0))
    vec = pl.BlockSpec((1, c), lambda i: (0, 0))
    vs = jax.ShapeDtypeStruct((1, c), F32)
    return pl.pallas_call(body, grid=(t // tr,), in_specs=[row, row, vec, vec], out_specs=[row, vec, vec],
                          out_shape=[jax.ShapeDtypeStruct((t, c), BF16), vs, vs],
                          compiler_params=_cp("arbitrary"), name=name)(gout, y, g, gt)


def _loss_head(y, tgt, *, name):
    t, c = y.shape
    tr = _row_tile(t)

    def body(y_ref, t_ref, l_ref, dy_ref):
        @pl.when(pl.program_id(0) == 0)
        def _():
            l_ref[...] = jnp.zeros_like(l_ref)

        d = y_ref[...] - t_ref[...]
        dy_ref[...] = d * (1.0 / c)
        l_ref[...] += 0.5 * jnp.sum(jnp.mean(d * d, axis=-1, keepdims=True))

    row = pl.BlockSpec((tr, c), lambda i: (i, 0))
    return pl.pallas_call(body, grid=(t // tr,), in_specs=[row, row],
                          out_specs=[pl.BlockSpec((1, LANES), lambda i: (0, 0)), row],
                          out_shape=[jax.ShapeDtypeStruct((1, LANES), F32), jax.ShapeDtypeStruct((t, c), F32)],
                          compiler_params=_cp("arbitrary"), name=name)(y, tgt)


def _adamw(w, g, m, v, *, name):
    shape = w.shape
    c = shape[-1]
    r = math.prod(shape[:-1])
    w2, g2, m2, v2 = (a.reshape(r, c) for a in (w, g, m, v))
    tr = r
    for cand in (1024, 512, 256, 128, 64, 32, 16, 8):
        if r % cand == 0 and cand * c * 4 <= (1 << 20):
            tr = cand
            break
    c1 = 1.0 - ADAM_B1 ** ADAM_STEP
    c2 = 1.0 - ADAM_B2 ** ADAM_STEP

    def body(w_ref, g_ref, m_ref, v_ref, d_ref, nm_ref, nv_ref):
        gv = g_ref[...]
        mn = ADAM_B1 * m_ref[...] + (1.0 - ADAM_B1) * gv
        vn = ADAM_B2 * v_ref[...] + (1.0 - ADAM_B2) * jnp.square(gv)
        m_hat = mn / c1
        v_hat = vn / c2
        d_ref[...] = -ADAM_LR * (m_hat / (jnp.sqrt(v_hat) + ADAM_EPS) + ADAM_WD * w_ref[...])
        nm_ref[...] = mn
        nv_ref[...] = vn

    blk = pl.BlockSpec((tr, c), lambda i: (i, 0))
    s = jax.ShapeDtypeStruct((r, c), F32)
    d, nm, nv = pl.pallas_call(body, grid=(r // tr,), in_specs=[blk] * 4, out_specs=[blk] * 3, out_shape=[s, s, s],
                               compiler_params=_cp("parallel"), name=name)(w2, g2, m2, v2)
    return d.reshape(shape), nm.reshape(shape), nv.reshape(shape)


def _gla_parts(p_ref, wg_ref, bg_ref):
    q = p_ref[:, 0:GLA_KW] * (GLA_DK ** -0.5)
    k = p_ref[:, GLA_KW:2 * GLA_KW]
    glr = p_ref[:, 2 * GLA_KW + 2 * GLA_VW:GLA_IN_PAD]
    gate = _dg(glr, wg_ref[...]) + bg_ref[...]
    log_a = (jnp.minimum(gate, 0.0) - jnp.log(1.0 + jnp.exp(-jnp.abs(gate)))) * (1.0 / 16.0)
    tril = (_iota2((CHUNK, CHUNK), 0) >= _iota2((CHUNK, CHUNK), 1)).astype(F32)
    cum = _dot_sel(tril, log_a)
    c_last = cum[CHUNK - 1:CHUNK, :]
    f = jnp.exp(c_last - cum)
    dec = jnp.exp(c_last)
    return q, k, glr, gate, f, k * f, dec


def _gla_fwd(p, wg, bg, hg, *, name):
    t = p.shape[0]
    nc = t // CHUNK

    def body(p_ref, wg_ref, bg_ref, hg_ref, og_ref, s_ref, st):
        @pl.when(pl.program_id(0) == 0)
        def _():
            st[...] = jnp.zeros_like(st)

        q, _, _, _, _, ke, dec = _gla_parts(p_ref, wg_ref, bg_ref)
        hs = range(GLA_HEADS)
        ks = [slice(h * GLA_DK, (h + 1) * GLA_DK) for h in hs]
        vs = [slice(2 * GLA_KW + h * GLA_DV, 2 * GLA_KW + (h + 1) * GLA_DV) for h in hs]
        rs = [slice(2 * GLA_KW + GLA_VW + h * GLA_DV, 2 * GLA_KW + GLA_VW + (h + 1) * GLA_DV) for h in hs]
        s_new = [st[h] * dec[:, ks[h]] + _dg(p_ref[:, vs[h]], ke[:, ks[h]], TN) for h in hs]
        o = [_dg(q[:, ks[h]], s_new[h], NT) for h in hs]
        for h in hs:
            st[h] = s_new[h]
            s_ref[0, h] = s_new[h]
            rn = lax.rsqrt(jnp.mean(o[h] * o[h], axis=-1, keepdims=True) + EPS)
            rv = p_ref[:, rs[h]]
            og_ref[:, h * GLA_DV:(h + 1) * GLA_DV] = (((o[h] * rn) * hg_ref[...]) * (rv * _sigmoid(rv))).astype(BF16)

    full = lambda a: pl.BlockSpec(a.shape, lambda n: (0,) * a.ndim)
    return pl.pallas_call(
        body, grid=(nc,),
        in_specs=[pl.BlockSpec((CHUNK, GLA_IN_PAD), lambda n: (n, 0)), full(wg), full(bg), full(hg)],
        out_specs=[pl.BlockSpec((CHUNK, GLA_VW), lambda n: (n, 0)),
                   pl.BlockSpec((1, GLA_HEADS, GLA_DV, GLA_DK), lambda n: (n, 0, 0, 0))],
        out_shape=[jax.ShapeDtypeStruct((t, GLA_VW), BF16), jax.ShapeDtypeStruct((nc, GLA_HEADS, GLA_DV, GLA_DK), F32)],
        scratch_shapes=[pltpu.VMEM((GLA_HEADS, GLA_DV, GLA_DK), F32)],
        compiler_params=_cp("arbitrary"), name=name)(p, wg, bg, hg)


def _gla_bwd(p, dog, sall, wg, bg, hg, *, name):
    t = p.shape[0]
    nc = t // CHUNK

    def body(p_ref, dog_ref, s1_ref, s0_ref, wg_ref, bg_ref, hg_ref, dp_ref, dwg_ref, dbg_ref, dhg_ref, gt):
        i = pl.program_id(0)

        @pl.when(i == 0)
        def _():
            gt[...] = jnp.zeros_like(gt)
            dwg_ref[...] = jnp.zeros_like(dwg_ref)
            dbg_ref[...] = jnp.zeros_like(dbg_ref)
            dhg_ref[...] = jnp.zeros_like(dhg_ref)

        has_prev = (i < nc - 1).astype(F32)
        q, k, glr, gate, f, ke, dec = _gla_parts(p_ref, wg_ref, bg_ref)
        hgv = hg_ref[...]
        hs = range(GLA_HEADS)
        ks = [slice(h * GLA_DK, (h + 1) * GLA_DK) for h in hs]
        vs = [slice(2 * GLA_KW + h * GLA_DV, 2 * GLA_KW + (h + 1) * GLA_DV) for h in hs]
        rs = [slice(2 * GLA_KW + GLA_VW + h * GLA_DV, 2 * GLA_KW + GLA_VW + (h + 1) * GLA_DV) for h in hs]
        s1 = [s1_ref[0, h] for h in hs]
        o = [_dg(q[:, ks[h]], s1[h], NT) for h in hs]
        dhg = jnp.zeros((1, GLA_DV), F32)
        do = []
        for h in hs:
            rv = p_ref[:, rs[h]]
            rn = lax.rsqrt(jnp.mean(o[h] * o[h], axis=-1, keepdims=True) + EPS)
            z = o[h] * rn
            sg = _sigmoid(rv)
            sl = rv * sg
            dogh = dog_ref[:, h * GLA_DV:(h + 1) * GLA_DV].astype(F32)
            dhg = dhg + jnp.sum(dogh * z * sl, axis=0, keepdims=True)
            dp_ref[:, rs[h]] = (dogh * (z * hgv) * (sg * (1.0 + rv * (1.0 - sg)))).astype(BF16)
            dz = dogh * sl * hgv
            do.append(rn * (dz - z * jnp.mean(dz * z, axis=-1, keepdims=True)))
        dhg_ref[...] += dhg
        g_tot = [gt[h] + _dg(do[h], q[:, ks[h]], TN) for h in hs]
        dq = [_dg(do[h], s1[h], NN) for h in hs]
        dke_parts = [_dg(p_ref[:, vs[h]], g_tot[h], NN) for h in hs]
        dv = [_dg(ke[:, ks[h]], g_tot[h], NT) for h in hs]
        ddec_parts = []
        for h in hs:
            dp_ref[:, ks[h]] = (dq[h] * (GLA_DK ** -0.5)).astype(BF16)
            dp_ref[:, vs[h]] = dv[h].astype(BF16)
            ddec_parts.append(jnp.sum(g_tot[h] * (s0_ref[0, h] * has_prev), axis=0, keepdims=True))
            gt[h] = g_tot[h] * dec[:, ks[h]]
        dke = jnp.concatenate(dke_parts, axis=1)
        ddec = jnp.concatenate(ddec_parts, axis=1)
        dp_ref[:, GLA_KW:2 * GLA_KW] = (dke * f).astype(BF16)
        stril = (_iota2((CHUNK, CHUNK), 0) > _iota2((CHUNK, CHUNK), 1)).astype(F32)
        dlog_a = _dot_sel(stril, dke * ke) + ddec * dec
        dgate = dlog_a * (1.0 / 16.0) * _sigmoid(-gate)
        dp_ref[:, 2 * GLA_KW + 2 * GLA_VW:GLA_IN_PAD] = _dg(dgate, wg_ref[...], NT).astype(BF16)
        dwg_ref[...] += _dg(glr, dgate, TN)
        dbg_ref[...] += jnp.sum(dgate, axis=0, keepdims=True)

    full = lambda a: pl.BlockSpec(a.shape, lambda n: (0,) * a.ndim)
    sblk = (1, GLA_HEADS, GLA_DV, GLA_DK)
    return pl.pallas_call(
        body, grid=(nc,),
        in_specs=[pl.BlockSpec((CHUNK, GLA_IN_PAD), lambda n: (nc - 1 - n, 0)),
                  pl.BlockSpec((CHUNK, GLA_VW), lambda n: (nc - 1 - n, 0)),
                  pl.BlockSpec(sblk, lambda n: (nc - 1 - n, 0, 0, 0)),
                  pl.BlockSpec(sblk, lambda n: (jnp.maximum(nc - 2 - n, 0), 0, 0, 0)),
                  full(wg), full(bg), full(hg)],
        out_specs=[pl.BlockSpec((CHUNK, GLA_IN_PAD), lambda n: (nc - 1 - n, 0)), full(wg), full(bg), full(hg)],
        out_shape=[jax.ShapeDtypeStruct((t, GLA_IN_PAD), BF16), jax.ShapeDtypeStruct(wg.shape, F32),
                   jax.ShapeDtypeStruct(bg.shape, F32), jax.ShapeDtypeStruct(hg.shape, F32)],
        scratch_shapes=[pltpu.VMEM((GLA_HEADS, GLA_DV, GLA_DK), F32)],
        compiler_params=_cp("arbitrary"), name=name)(p, dog, sall, sall, wg, bg, hg)


def _gla_layer_fwd(h, w, tag):
    p = _mm(h, w["w_in"], name=tag + "_in")
    og, sall = _gla_fwd(p, w["wg"], w["bg"], w["hg"], name=tag + "_scan")
    y = _mm(og, w["w_out"], name=tag + "_out")
    return y, (p, og, sall)


def _dw_out(act, dy, sink, name):
    if sink is None:
        return _mm(act, dy, ta=True, name=name), None
    return None, _mm(act, dy, ta=True, out_dtype=sink[0].dtype, into=(sink[0], sink[1], "rows"), name=name)


def _gla_layer_bwd(dy, h, saved, w, tag, sink=None):
    p, og, sall = saved
    dog = _mm(dy, w["w_out"], tb=True, name=tag + "_dog")
    dw_out, gf = _dw_out(og, dy, sink, tag + "_dwout")
    dp, dwg, dbg, dhg = _gla_bwd(p, dog, sall, w["wg"], w["bg"], w["hg"], name=tag + "_scanb")
    dw_in = _mm(h, dp, ta=True, name=tag + "_dwin")
    dh = _mm(dp, w["w_in"], tb=True, name=tag + "_dh")
    grads = dict(w_in=dw_in[:, :GLA_IN], w_gate_up=dwg[:GLA_RANK], b_gate=dbg[0], head_g=dhg[0], w_out=dw_out)
    return dh, grads, gf


def _rope_tables(pos, inv_freq, *, name):
    t = pos.shape[0]
    tr = _row_tile(t)
    half = MLA_ROPE // 2

    def body(p_ref, f_ref, c_ref, s1_ref, s2_ref, s1b_ref, s2b_ref):
        ang = p_ref[...].astype(F32) * f_ref[...]
        lane = _iota2((tr, LANES), 1)
        lo = (lane >= MLA_NOPE) & (lane < MLA_NOPE + half)
        hi = (lane >= MLA_NOPE + half) & (lane < MLA_QK)
        cs, sn = jnp.cos(ang), jnp.sin(ang)
        zero = jnp.zeros_like(cs)
        c_ref[...] = jnp.where(lane < MLA_NOPE, 1.0, jnp.where(lane < MLA_QK, cs, 0.0))
        s1_ref[...] = jnp.where(lo, -sn, zero)
        s2_ref[...] = jnp.where(hi, sn, zero)
        s1b_ref[...] = jnp.where(lo, sn, zero)
        s2b_ref[...] = jnp.where(hi, -sn, zero)

    row = pl.BlockSpec((tr, LANES), lambda i: (i, 0))
    s = jax.ShapeDtypeStruct((t, LANES), F32)
    return pl.pallas_call(body, grid=(t // tr,),
                          in_specs=[pl.BlockSpec((tr, 1), lambda i: (i, 0)), pl.BlockSpec((1, LANES), lambda i: (0, 0))],
                          out_specs=[row] * 5, out_shape=[s] * 5, compiler_params=_cp("parallel"), name=name)(pos, inv_freq)


def _rope(x, c, s1, s2, *, out_dtype, sum_heads=False, name):
    nh, t, _ = x.shape
    tr = _row_tile(t)
    half = MLA_ROPE // 2

    def body(x_ref, c_ref, s1_ref, s2_ref, o_ref):
        total = None
        for h in range(nh):
            xv = x_ref[h].astype(F32)
            y = xv * c_ref[...] + pltpu.roll(xv, LANES - half, 1) * s1_ref[...] + pltpu.roll(xv, half, 1) * s2_ref[...]
            if sum_heads:
                total = y if total is None else total + y
            else:
                o_ref[h] = y.astype(out_dtype)
        if sum_heads:
            o_ref[...] = total

    tab = pl.BlockSpec((tr, LANES), lambda i: (i, 0))
    xs = pl.BlockSpec((nh, tr, LANES), lambda i: (0, i, 0))
    if sum_heads:
        return pl.pallas_call(body, grid=(t // tr,), in_specs=[xs, tab, tab, tab], out_specs=tab,
                              out_shape=jax.ShapeDtypeStruct((t, LANES), F32),
                              compiler_params=_cp("parallel"), name=name)(x, c, s1, s2)
    return pl.pallas_call(body, grid=(t // tr,), in_specs=[xs, tab, tab, tab], out_specs=xs,
                          out_shape=jax.ShapeDtypeStruct(x.shape, out_dtype),
                          compiler_params=_cp("parallel"), name=name)(x, c, s1, s2)


FLASH_BLK = 512


def _diag_mask(blk):
    return (_iota2((blk, blk), 1) // CHUNK) <= (_iota2((blk, blk), 0) // CHUNK)


def _flash_fwd(q, k, v, *, name):
    nh, t, _ = q.shape
    blk = min(FLASH_BLK, t)
    scale = MLA_QK ** -0.5

    def body(q_ref, k_ref, v_ref, o_ref, lse_ref):
        i = pl.program_id(1)
        qv = q_ref[0]

        def step(j, carry, masked):
            m, l, acc = carry
            off = pl.multiple_of(j * blk, blk)
            kb = k_ref[0, pl.ds(off, blk), :]
            vb = v_ref[0, pl.ds(off, blk), :]
            s = _dg(qv, kb, NT) * scale
            if masked:
                s = jnp.where(_diag_mask(blk), s, NEG_INF)
            m_new = jnp.maximum(m, jnp.max(s, axis=-1, keepdims=True))
            p = jnp.exp(s - m_new)
            alpha = jnp.exp(m - m_new)
            return m_new, alpha * l + jnp.sum(p, axis=-1, keepdims=True), alpha * acc + _dg(p, vb, NN)

        init = (jnp.full((blk, 1), NEG_INF, F32), jnp.zeros((blk, 1), F32), jnp.zeros((blk, MLA_V), F32))
        carry = lax.fori_loop(0, i, lambda j, c: step(j, c, False), init)
        m, l, acc = step(i, carry, True)
        o_ref[0] = (acc / l).astype(BF16)
        lse_ref[0] = m + jnp.log(l)

    qs = pl.BlockSpec((1, blk, LANES), lambda h, i: (h, i, 0))
    return pl.pallas_call(
        body, grid=(nh, t // blk),
        in_specs=[qs, pl.BlockSpec((1, t, LANES), lambda h, i: (h, 0, 0)), pl.BlockSpec((1, t, MLA_V), lambda h, i: (h, 0, 0))],
        out_specs=[pl.BlockSpec((1, blk, MLA_V), lambda h, i: (h, i, 0)), pl.BlockSpec((1, blk, 1), lambda h, i: (h, i, 0))],
        out_shape=[jax.ShapeDtypeStruct((nh, t, MLA_V), BF16), jax.ShapeDtypeStruct((nh, t, 1), F32)],
        compiler_params=_cp("parallel", "parallel"), name=name)(q, k, v)


def _flash_dq(q, k, v, do, o, lse, *, name):
    nh, t, _ = q.shape
    blk = min(FLASH_BLK, t)
    scale = MLA_QK ** -0.5

    def body(q_ref, k_ref, v_ref, do_ref, o_ref, lse_ref, dq_ref, dl_ref):
        i = pl.program_id(1)
        qv = q_ref[0]
        dov = do_ref[0]
        lse_v = lse_ref[0]
        delta = jnp.sum(dov.astype(F32) * o_ref[0].astype(F32), axis=-1, keepdims=True)
        dl_ref[0] = delta

        def step(j, dq, masked):
            off = pl.multiple_of(j * blk, blk)
            kb = k_ref[0, pl.ds(off, blk), :]
            vb = v_ref[0, pl.ds(off, blk), :]
            s = _dg(qv, kb, NT) * scale
            if masked:
                s = jnp.where(_diag_mask(blk), s, NEG_INF)
            p = jnp.exp(s - lse_v)
            ds = p * (_dg(dov, vb, NT) - delta) * scale
            return dq + _dg(ds, kb, NN)

        dq = lax.fori_loop(0, i, lambda j, c: step(j, c, False), jnp.zeros((blk, LANES), F32))
        dq_ref[0] = step(i, dq, True)

    qs = pl.BlockSpec((1, blk, LANES), lambda h, i: (h, i, 0))
    vs = pl.BlockSpec((1, blk, MLA_V), lambda h, i: (h, i, 0))
    ls = pl.BlockSpec((1, blk, 1), lambda h, i: (h, i, 0))
    return pl.pallas_call(
        body, grid=(nh, t // blk),
        in_specs=[qs, pl.BlockSpec((1, t, LANES), lambda h, i: (h, 0, 0)), pl.BlockSpec((1, t, MLA_V), lambda h, i: (h, 0, 0)), vs, vs, ls],
        out_specs=[qs, ls],
        out_shape=[jax.ShapeDtypeStruct((nh, t, LANES), F32), jax.ShapeDtypeStruct((nh, t, 1), F32)],
        compiler_params=_cp("parallel", "parallel"), name=name)(q, k, v, do, o, lse)


def _flash_dkv(q, k, v, do, lse, delta, *, name):
    nh, t, _ = q.shape
    blk = min(FLASH_BLK, t)
    nq = t // blk
    scale = MLA_QK ** -0.5

    def body(q_ref, k_ref, v_ref, do_ref, lse_ref, dl_ref, dk_ref, dv_ref):
        j = pl.program_id(1)
        kb = k_ref[0]
        vb = v_ref[0]

        def step(i, carry, masked):
            dk, dv = carry
            off = pl.multiple_of(i * blk, blk)
            qb = q_ref[0, pl.ds(off, blk), :]
            dob = do_ref[0, pl.ds(off, blk), :]
            s = _dg(qb, kb, NT) * scale
            if masked:
                s = jnp.where(_diag_mask(blk), s, NEG_INF)
            p = jnp.exp(s - lse_ref[0, pl.ds(off, blk), :])
            ds = p * (_dg(dob, vb, NT) - dl_ref[0, pl.ds(off, blk), :]) * scale
            return dk + _dg(ds, qb, TN), dv + _dg(p, dob, TN)

        carry = step(j, (jnp.zeros((blk, LANES), F32), jnp.zeros((blk, MLA_V), F32)), True)
        dk, dv = lax.fori_loop(j + 1, nq, lambda i, c: step(i, c, False), carry)
        dk_ref[0] = dk
        dv_ref[0] = dv

    ks = pl.BlockSpec((1, blk, LANES), lambda h, j: (h, j, 0))
    vs = pl.BlockSpec((1, blk, MLA_V), lambda h, j: (h, j, 0))
    fl = lambda w: pl.BlockSpec((1, t, w), lambda h, j: (h, 0, 0))
    return pl.pallas_call(
        body, grid=(nh, nq),
        in_specs=[fl(LANES), ks, vs, fl(MLA_V), fl(1), fl(1)],
        out_specs=[ks, vs],
        out_shape=[jax.ShapeDtypeStruct((nh, t, LANES), F32), jax.ShapeDtypeStruct((nh, t, MLA_V), F32)],
        compiler_params=_cp("parallel", "parallel"), name=name)(q, k, v, do, lse, delta)


def _heads_first(a, width):
    t = a.shape[0]
    return a.reshape(t, MLA_HEADS, width).transpose(1, 0, 2)


def _heads_last(a):
    return a.transpose(1, 0, 2).reshape(a.shape[1], -1)


def _mla_layer_fwd(h, pos, w, tag):
    t = h.shape[0]
    zq = jnp.zeros((1, MLA_Q_RANK), F32)
    zkv = jnp.zeros((1, MLA_KV_RANK), F32)
    p = _mm(h, w["w_in"], name=tag + "_in")
    cq, ckv, krp = p[:, :MLA_Q_RANK], p[:, MLA_Q_RANK:MLA_Q_RANK + MLA_KV_RANK], p[:, MLA_Q_RANK + MLA_KV_RANK:MLA_IN]
    qn = _premod_fwd(cq, w["q_norm_g"], zq, zq, name=tag + "_qnorm")
    kvn = _premod_fwd(ckv, w["kv_norm_g"], zkv, zkv, name=tag + "_kvnorm")
    q = _mm(qn, w["w_uq"], name=tag + "_uq")
    kv = _mm(kvn, w["w_ukv"], name=tag + "_ukv")
    q_pre = jnp.pad(_heads_first(q, MLA_QK), ((0, 0), (0, 0), (0, LANES - MLA_QK)))
    kv3 = _heads_first(kv, MLA_NOPE + MLA_V)
    k_pre = jnp.concatenate([kv3[:, :, :MLA_NOPE], jnp.broadcast_to(krp[None], (MLA_HEADS, t, MLA_ROPE)),
                             jnp.zeros((MLA_HEADS, t, LANES - MLA_QK), F32)], axis=-1)
    vh = kv3[:, :, MLA_NOPE:].astype(BF16)
    half = MLA_ROPE // 2
    freq = ROPE_THETA ** (-jnp.arange(half, dtype=F32) / half)
    inv_freq = jnp.concatenate([jnp.zeros((MLA_NOPE,), F32), freq, freq, jnp.zeros((LANES - MLA_QK,), F32)])[None]
    tabs = _rope_tables(pos.reshape(t, 1), inv_freq, name=tag + "_tables")
    qr = _rope(q_pre, tabs[0], tabs[1], tabs[2], out_dtype=BF16, name=tag + "_ropeq")
    kr = _rope(k_pre, tabs[0], tabs[1], tabs[2], out_dtype=BF16, name=tag + "_ropek")
    o, lse = _flash_fwd(qr, kr, vh, name=tag + "_attn")
    of = _heads_last(o)
    y = _mm(of, w["w_out"], name=tag + "_out")
    return y, (cq, ckv, qn, kvn, qr, kr, vh, o, lse, of, tabs)


def _mla_layer_bwd(dy, h, saved, w, tag, sink=None):
    cq, ckv, qn, kvn, qr, kr, vh, o, lse, of, tabs = saved
    t = h.shape[0]
    zq = jnp.zeros((1, MLA_Q_RANK), F32)
    zkv = jnp.zeros((1, MLA_KV_RANK), F32)
    dof = _mm(dy, w["w_out"], tb=True, out_dtype=BF16, name=tag + "_dof")
    dw_out, gf = _dw_out(of, dy, sink, tag + "_dwout")
    do = _heads_first(dof, MLA_V)
    dqr, delta = _flash_dq(qr, kr, vh, do, o, lse, name=tag + "_attn_dq")
    dkr, dv = _flash_dkv(qr, kr, vh, do, lse, delta, name=tag + "_attn_dkv")
    dq_pre = _rope(dqr, tabs[0], tabs[3], tabs[4], out_dtype=F32, name=tag + "_ropeq_b")
    dk_sum = _rope(dkr, tabs[0], tabs[3], tabs[4], out_dtype=F32, sum_heads=True, name=tag + "_ropek_b")
    dq = _heads_last(dq_pre[:, :, :MLA_QK])
    dkv = _heads_last(jnp.concatenate([dkr[:, :, :MLA_NOPE], dv], axis=-1))
    dw_uq = _mm(qn, dq, ta=True, name=tag + "_dwuq")
    dqn = _mm(dq, w["w_uq"], tb=True, name=tag + "_dqn")
    dw_ukv = _mm(kvn, dkv, ta=True, name=tag + "_dwukv")
    dkvn = _mm(dkv, w["w_ukv"], tb=True, name=tag + "_dkvn")
    dcq, dqg, _, _ = _premod_bwd(dqn, cq, w["q_norm_g"], zq, jnp.zeros_like(cq), name=tag + "_qnorm_b")
    dckv, dkvg, _, _ = _premod_bwd(dkvn, ckv, w["kv_norm_g"], zkv, jnp.zeros_like(ckv), name=tag + "_kvnorm_b")
    dp = jnp.concatenate([dcq, dckv, dk_sum[:, MLA_NOPE:MLA_QK], jnp.zeros((t, MLA_IN_PAD - MLA_IN), F32)], axis=1).astype(BF16)
    dw_in = _mm(h, dp, ta=True, name=tag + "_dwin")
    dh = _mm(dp, w["w_in"], tb=True, name=tag + "_dh")
    grads = dict(w_in=dw_in[:, :MLA_IN], q_norm_g=dqg[0], w_uq=dw_uq, kv_norm_g=dkvg[0], w_ukv=dw_ukv, w_out=dw_out)
    return dh, grads, gf


CONV_HALO = 8


def _conv_tiles(t):
    return min(512, t), 512


def _gdn_conv_fwd(p, w, *, name):
    t = p.shape[0]
    tr, tc = _conv_tiles(t)
    hb = tr // CONV_HALO

    def body(x_ref, halo_ref, w_ref, o_ref, buf):
        i = pl.program_id(0)
        buf[0:CONV_HALO, :] = halo_ref[...] * (i > 0).astype(F32)
        buf[CONV_HALO:CONV_HALO + tr, :] = x_ref[...]
        base = CONV_HALO - (GDN_CONV - 1)
        acc = buf[pl.ds(base, tr), :] * w_ref[0:1, :]
        for j in range(1, GDN_CONV):
            acc = acc + buf[pl.ds(base + j, tr), :] * w_ref[j:j + 1, :]
        o_ref[...] = acc * _sigmoid(acc)

    return pl.pallas_call(
        body, grid=(t // tr, GDN_CONV_W // tc),
        in_specs=[pl.BlockSpec((tr, tc), lambda i, j: (i, j)),
                  pl.BlockSpec((CONV_HALO, tc), lambda i, j: (jnp.maximum(i * hb - 1, 0), j)),
                  pl.BlockSpec((GDN_CONV, tc), lambda i, j: (0, j))],
        out_specs=pl.BlockSpec((tr, tc), lambda i, j: (i, j)),
        out_shape=jax.ShapeDtypeStruct((t, GDN_CONV_W), F32),
        scratch_shapes=[pltpu.VMEM((tr + CONV_HALO, tc), F32)],
        compiler_params=_cp("parallel", "parallel"), name=name)(p, p, w)


def _gdn_conv_bwd(d, p, w, col0, *, name):
    t = p.shape[0]
    tr, tc = _conv_tiles(t)
    hb = tr // CONV_HALO
    nr = t // tr
    ext = tr + CONV_HALO

    def body(x_ref, xp_ref, xn_ref, d_ref, dn_ref, w_ref, dx_ref, dw_ref, bufx, bufd):
        i = pl.program_id(1)

        @pl.when(i == 0)
        def _():
            dw_ref[...] = jnp.zeros_like(dw_ref)

        last = (i < nr - 1).astype(F32)
        bufx[0:CONV_HALO, :] = xp_ref[...] * (i > 0).astype(F32)
        bufx[CONV_HALO:CONV_HALO + tr, :] = x_ref[...]
        bufx[CONV_HALO + tr:, :] = xn_ref[...] * last
        base = CONV_HALO - (GDN_CONV - 1)
        acc = bufx[pl.ds(base, ext), :] * w_ref[0:1, :]
        for j in range(1, GDN_CONV):
            acc = acc + bufx[pl.ds(base + j, ext), :] * w_ref[j:j + 1, :]
        sg = _sigmoid(acc)
        dsilu = sg * (1.0 + acc * (1.0 - sg))
        bufd[0:tr, :] = d_ref[...] * dsilu[0:tr, :]
        bufd[tr:, :] = dn_ref[...] * last * dsilu[tr:, :]
        dx = bufd[pl.ds(GDN_CONV - 1, tr), :] * w_ref[0:1, :]
        for j in range(1, GDN_CONV):
            dx = dx + bufd[pl.ds(GDN_CONV - 1 - j, tr), :] * w_ref[j:j + 1, :]
        dx_ref[...] = dx.astype(BF16)
        dc = bufd[0:tr, :]
        for j in range(GDN_CONV):
            dw_ref[j:j + 1, :] += jnp.sum(dc * bufx[pl.ds(base + j, tr), :], axis=0, keepdims=True)

    cb = col0 // tc
    width = d.shape[1]
    main = lambda off: pl.BlockSpec((tr, tc), lambda j, i: (i, j + off))
    prev = pl.BlockSpec((CONV_HALO, tc), lambda j, i: (jnp.maximum(i * hb - 1, 0), j + cb))
    nxt = lambda off: pl.BlockSpec((CONV_HALO, tc), lambda j, i: (jnp.minimum((i + 1) * hb, t // CONV_HALO - 1), j + off))
    wsp = lambda off: pl.BlockSpec((GDN_CONV, tc), lambda j, i: (0, j + off))
    return pl.pallas_call(
        body, grid=(width // tc, nr),
        in_specs=[main(cb), prev, nxt(cb), main(0), nxt(0), wsp(cb)], out_specs=[main(0), wsp(0)],
        out_shape=[jax.ShapeDtypeStruct((t, width), BF16), jax.ShapeDtypeStruct((GDN_CONV, width), F32)],
        scratch_shapes=[pltpu.VMEM((tr + 2 * CONV_HALO, tc), F32), pltpu.VMEM((ext, tc), F32)],
        compiler_params=_cp("parallel", "arbitrary"), name=name)(p, p, p, d, d, w)


def _dot_sel(sel, b, dims=NN, sel_first=True):
    s = sel.astype(BF16)
    b1 = b.astype(BF16)
    r1 = b - b1.astype(F32)
    b2 = r1.astype(BF16)
    b3 = (r1 - b2.astype(F32)).astype(BF16)
    if sel_first:
        d = lambda v: lax.dot_general(s, v, (dims, ((), ())), preferred_element_type=F32)
    else:
        d = lambda v: lax.dot_general(v, s, (dims, ((), ())), preferred_element_type=F32)
    return d(b1) + (d(b2) + d(b3))


def _gdn_chunks(qraws, kraws, vs, braws, araws, alogs, dtbs):
    nv = len(vs)
    row = _iota2((CHUNK, CHUNK), 0)
    col = _iota2((CHUNK, CHUNK), 1)
    strict = row > col
    triu = (row <= col).astype(F32)
    tril = (row >= col).astype(F32)
    ones = jnp.ones((CHUNK, CHUNK), F32)
    keys = []
    for qraw, kraw in zip(qraws, kraws):
        rq = lax.rsqrt(jnp.sum(qraw * qraw, axis=-1, keepdims=True) + EPS)
        rk = lax.rsqrt(jnp.sum(kraw * kraw, axis=-1, keepdims=True) + EPS)
        qn = qraw * rq
        keys.append(dict(rq=rq, rk=rk, qn=qn, qh=qn * (GDN_DK ** -0.5), kh=kraw * rk))
    kks = [_dg(kd["kh"], kd["kh"], NT) for kd in keys]
    cs = []
    for h in range(nv):
        c = dict(keys[h // 2])
        c.update(v=vs[h], kk=kks[h // 2], strict=strict, triu=triu)
        c["beta"] = _sigmoid(braws[h])
        c["ea"] = jnp.exp(alogs[h])
        c["xs"] = araws[h] + dtbs[h]
        c["g"] = -c["ea"] * _softplus(c["xs"])
        cs.append(c)
    gbs = [jnp.broadcast_to(c["g"], (CHUNK, LANES)) for c in cs]
    cums = [_dot_sel(tril, gb) for gb in gbs]
    cum_js = [_dot_sel(ones, gb[:, :CHUNK] * triu) for gb in gbs]
    for c, cum, cum_j in zip(cs, cums, cum_js):
        diff = jnp.where(strict, cum[:, :CHUNK] - cum_j, 0.0)
        c["dm"] = jnp.where(strict, jnp.exp(diff), 0.0)
        c["a"] = (c["beta"] * c["dm"]) * c["kk"]
        c_last = cum[CHUNK - 1:CHUNK, :]
        c["e"] = jnp.exp(cum)
        c["f"] = jnp.exp(c_last - cum)
        c["dec"] = jnp.exp(c_last)
        c["rv"] = c["beta"] * c["v"]
        c["rk_rhs"] = (c["beta"] * c["e"]) * c["kh"]
        c["ke"] = c["kh"] * c["f"]
    return cs


def _unit_lower_inverses(mats):
    eye = (_iota2((CHUNK, CHUNK), 0) == _iota2((CHUNK, CHUNK), 1)).astype(F32)
    ts = [eye - a for a in mats]
    pws = list(mats)
    for _ in range(5):
        pws = [_dot3(pw, pw) for pw in pws]
        ts = [t + _dot3(t, pw) for t, pw in zip(ts, pws)]
    return ts


GDN_HB = 16


def _gdn_specs(chunk_of):
    hb = GDN_HB
    kw = hb // 2 * GDN_DK
    vw = hb * GDN_DV
    qs = pl.BlockSpec((CHUNK, kw), lambda g, n: (chunk_of(n), g))
    ks = pl.BlockSpec((CHUNK, kw), lambda g, n: (chunk_of(n), GDN_QKW // kw + g))
    vs = pl.BlockSpec((CHUNK, vw), lambda g, n: (chunk_of(n), 2 * GDN_QKW // vw + g))
    zs = pl.BlockSpec((CHUNK, vw), lambda g, n: (chunk_of(n), GDN_CONV_W // vw + g))
    col = pl.BlockSpec((hb, CHUNK, 1), lambda g, n: (g, chunk_of(n), 0))
    one = pl.BlockSpec((hb, 1, 1), lambda g, n: (g, 0, 0))
    ng = pl.BlockSpec((1, GDN_DV), lambda g, n: (0, 0))
    hd = pl.BlockSpec((CHUNK, vw), lambda g, n: (chunk_of(n), g))
    return qs, ks, vs, zs, col, one, ng, hd


def _gdn_fwd(qkv, p, braw, araw, alog, dtb, ng, *, name):
    t = qkv.shape[0]
    nc = t // CHUNK
    nh = GDN_V_HEADS

    def body(q_ref, k_ref, v_ref, z_ref, braw_ref, araw_ref, alog_ref, dtb_ref, ng_ref, og_ref, s_ref, t_ref, st):
        @pl.when(pl.program_id(1) == 0)
        def _():
            st[...] = jnp.zeros_like(st)

        hs = range(GDN_HB)
        kqs = [slice(j * GDN_DK, (j + 1) * GDN_DK) for j in range(GDN_HB // 2)]
        vsl = [slice(h * GDN_DV, (h + 1) * GDN_DV) for h in hs]
        cs = _gdn_chunks([q_ref[:, s] for s in kqs], [k_ref[:, s] for s in kqs], [v_ref[:, s] for s in vsl],
                         [braw_ref[h] for h in hs], [araw_ref[h] for h in hs], [alog_ref[h] for h in hs], [dtb_ref[h] for h in hs])
        tms = _unit_lower_inverses([c["a"] for c in cs])
        s0 = [st[h] for h in hs]
        wv = [_dot3(tms[h], cs[h]["rv"]) for h in hs]
        wk = [_dot3(tms[h], cs[h]["rk_rhs"]) for h in hs]
        u = [wv[h] - _dg(wk[h], s0[h], NN) for h in hs]
        s1 = [cs[h]["dec"] * s0[h] + _dg(cs[h]["ke"], u[h], TN) for h in hs]
        o = [_dg(cs[h]["qh"], s1[h], NN) for h in hs]
        for h in hs:
            t_ref[h, 0] = tms[h]
            st[h] = s1[h]
            s_ref[h, 0] = s1[h]
            rn = lax.rsqrt(jnp.mean(o[h] * o[h], axis=-1, keepdims=True) + EPS)
            zv = z_ref[:, vsl[h]]
            og_ref[:, vsl[h]] = (((o[h] * rn) * ng_ref[...]) * (zv * _sigmoid(zv))).astype(BF16)

    qs, ks, vs, zs, col, one, ngs, hd = _gdn_specs(lambda n: n)
    return pl.pallas_call(
        body, grid=(nh // GDN_HB, nc),
        in_specs=[qs, ks, vs, zs, col, col, one, one, ngs],
        out_specs=[hd,
                   pl.BlockSpec((GDN_HB, 1, GDN_DK, GDN_DV), lambda g, n: (g, n, 0, 0)),
                   pl.BlockSpec((GDN_HB, 1, CHUNK, CHUNK), lambda g, n: (g, n, 0, 0))],
        out_shape=[jax.ShapeDtypeStruct((t, GDN_VW), BF16), jax.ShapeDtypeStruct((nh, nc, GDN_DK, GDN_DV), F32),
                   jax.ShapeDtypeStruct((nh, nc, CHUNK, CHUNK), F32)],
        scratch_shapes=[pltpu.VMEM((GDN_HB, GDN_DK, GDN_DV), F32)],
        compiler_params=_cp("parallel", "arbitrary"), name=name)(qkv, qkv, qkv, p, braw, araw, alog, dtb, ng)


def _gdn_bwd(qkv, p, braw, araw, alog, dtb, ng, dog, sall, tall, *, name):
    t = qkv.shape[0]
    nc = t // CHUNK
    nh = GDN_V_HEADS

    def body(q_ref, k_ref, v_ref, z_ref, braw_ref, araw_ref, alog_ref, dtb_ref, ng_ref, dog_ref, s1_ref, s0_ref, t_ref,
             dq_ref, dk_ref, dv_ref, dz_ref, dbraw_ref, daraw_ref, dalog_ref, ddtb_ref, dng_ref, gc):
        grp = pl.program_id(0)
        i = pl.program_id(1)

        @pl.when(i == 0)
        def _():
            gc[...] = jnp.zeros_like(gc)
            dalog_ref[...] = jnp.zeros_like(dalog_ref)
            ddtb_ref[...] = jnp.zeros_like(ddtb_ref)

        @pl.when((i == 0) & (grp == 0))
        def _():
            dng_ref[...] = jnp.zeros_like(dng_ref)

        has_prev = (i < nc - 1).astype(F32)
        ngv = ng_ref[...]
        ones = jnp.ones((CHUNK, LANES), F32)
        hs = range(GDN_HB)
        kqs = [slice(j * GDN_DK, (j + 1) * GDN_DK) for j in range(GDN_HB // 2)]
        vsl = [slice(h * GDN_DV, (h + 1) * GDN_DV) for h in hs]
        cs = _gdn_chunks([q_ref[:, s] for s in kqs], [k_ref[:, s] for s in kqs], [v_ref[:, s] for s in vsl],
                         [braw_ref[h] for h in hs], [araw_ref[h] for h in hs], [alog_ref[h] for h in hs], [dtb_ref[h] for h in hs])
        tms = [t_ref[h, 0] for h in hs]
        s1 = [s1_ref[h, 0] for h in hs]
        s0 = [s0_ref[h, 0] * has_prev for h in hs]
        wv = [_dot3(tms[h], cs[h]["rv"]) for h in hs]
        wk = [_dot3(tms[h], cs[h]["rk_rhs"]) for h in hs]
        u = [wv[h] - _dg(wk[h], s0[h], NN) for h in hs]
        o = [_dg(cs[h]["qh"], s1[h], NN) for h in hs]
        dng = jnp.zeros((1, GDN_DV), F32)
        do = []
        for h in hs:
            zv = z_ref[:, vsl[h]]
            dogv = dog_ref[:, vsl[h]]
            rn = lax.rsqrt(jnp.mean(o[h] * o[h], axis=-1, keepdims=True) + EPS)
            zo = o[h] * rn
            sg = _sigmoid(zv)
            sl = zv * sg
            dng = dng + jnp.sum(dogv * zo * sl, axis=0, keepdims=True)
            dz_ref[:, vsl[h]] = (dogv * (zo * ngv) * (sg * (1.0 + zv * (1.0 - sg)))).astype(BF16)
            dzo = dogv * sl * ngv
            do.append(rn * (dzo - zo * jnp.mean(dzo * zo, axis=-1, keepdims=True)))
        dng_ref[...] += dng
        g_tot = [gc[h] + _dg(cs[h]["qh"], do[h], TN) for h in hs]
        dqh = [_dg(do[h], s1[h], NT) for h in hs]
        dke = [_dg(u[h], g_tot[h], NT) for h in hs]
        du = [_dg(cs[h]["ke"], g_tot[h], NN) for h in hs]
        gnew = [cs[h]["dec"] * g_tot[h] - _dg(wk[h], du[h], TN) for h in hs]
        dwk = [-_dg(du[h], s0[h], NT) for h in hs]
        drv = [_dot3(tms[h], du[h], TN) for h in hs]
        drk = [_dot3(tms[h], dwk[h], TN) for h in hs]
        da = [jnp.where(cs[h]["strict"], -(_dot3(drv[h], wv[h], NT) + _dot3(drk[h], wk[h], NT)), 0.0) for h in hs]
        mx = [da[h] * cs[h]["dm"] * cs[h]["kk"] for h in hs]
        aa = [mx[h] * cs[h]["beta"] for h in hs]
        colsum = [_dot_sel(ones, aa[h], TN, sel_first=False)[:, 0:1] for h in hs]
        bm = [(da[h] * cs[h]["beta"]) * cs[h]["dm"] for h in hs]
        dkh = [_dg(bm[h], cs[h]["kh"], NN) + _dg(bm[h], cs[h]["kh"], TN) for h in hs]
        dcum, dcl, dbeta = [], [], []
        for h in hs:
            c = cs[h]
            beta, kh, e, f, dec, ke = c["beta"], c["kh"], c["e"], c["f"], c["dec"], c["ke"]
            gc[h] = gnew[h]
            ddec = jnp.sum(jnp.sum(g_tot[h] * s0[h], axis=1, keepdims=True), axis=0, keepdims=True)
            dv_ref[:, vsl[h]] = beta * drv[h]
            db = jnp.sum(mx[h], axis=1, keepdims=True) + jnp.sum(drv[h] * c["v"], axis=1, keepdims=True)
            dbeta.append(db + jnp.sum(drk[h] * (e * kh), axis=1, keepdims=True))
            dkh[h] = dkh[h] + (beta * e) * drk[h] + f * dke[h]
            ef = jnp.sum(dke[h] * ke, axis=1, keepdims=True)
            dcum.append(jnp.sum(aa[h], axis=1, keepdims=True) - colsum[h] + jnp.sum(drk[h] * c["rk_rhs"], axis=1, keepdims=True) - ef)
            dcl.append(jnp.sum(ef, axis=0, keepdims=True) + ddec * dec[:, 0:1])
        dg = [_dot_sel(cs[h]["triu"], jnp.broadcast_to(dcum[h], (CHUNK, LANES)))[:, 0:1] + dcl[h] for h in hs]
        for h in hs:
            c = cs[h]
            beta, kh = c["beta"], c["kh"]
            daraw = dg[h] * (-c["ea"]) * _sigmoid(c["xs"])
            daraw_ref[h] = daraw
            dbraw_ref[h] = dbeta[h] * beta * (1.0 - beta)
            dalog_ref[h] += jnp.sum(dg[h] * c["g"], axis=0, keepdims=True)
            ddtb_ref[h] += jnp.sum(daraw, axis=0, keepdims=True)
        for j, sl in enumerate(kqs):
            c = cs[2 * j]
            dn = (dqh[2 * j] + dqh[2 * j + 1]) * (GDN_DK ** -0.5)
            dks = dkh[2 * j] + dkh[2 * j + 1]
            dq_ref[:, sl] = c["rq"] * (dn - c["qn"] * jnp.sum(dn * c["qn"], axis=-1, keepdims=True))
            dk_ref[:, sl] = c["rk"] * (dks - c["kh"] * jnp.sum(dks * c["kh"], axis=-1, keepdims=True))

    rev = lambda n: nc - 1 - n
    qs, ks, vs, zs, col, one, ngs, hd = _gdn_specs(rev)
    s1s = pl.BlockSpec((GDN_HB, 1, GDN_DK, GDN_DV), lambda g, n: (g, rev(n), 0, 0))
    s0s = pl.BlockSpec((GDN_HB, 1, GDN_DK, GDN_DV), lambda g, n: (g, jnp.maximum(rev(n) - 1, 0), 0, 0))
    ts = pl.BlockSpec((GDN_HB, 1, CHUNK, CHUNK), lambda g, n: (g, rev(n), 0, 0))
    big = jax.ShapeDtypeStruct((t, GDN_VW), F32)
    keyw = jax.ShapeDtypeStruct((t, GDN_QKW), F32)
    cols = jax.ShapeDtypeStruct((nh, t, 1), F32)
    ones_s = jax.ShapeDtypeStruct((nh, 1, 1), F32)
    return pl.pallas_call(
        body, grid=(nh // GDN_HB, nc),
        in_specs=[qs, ks, vs, zs, col, col, one, one, ngs, hd, s1s, s0s, ts],
        out_specs=[qs, qs, hd, hd, col, col, one, one, ngs],
        out_shape=[keyw, keyw, big, jax.ShapeDtypeStruct((t, GDN_VW), BF16), cols, cols, ones_s, ones_s,
                   jax.ShapeDtypeStruct((1, GDN_DV), F32)],
        scratch_shapes=[pltpu.VMEM((GDN_HB, GDN_DK, GDN_DV), F32)],
        compiler_params=_cp("arbitrary", "arbitrary"), name=name)(qkv, qkv, qkv, p, braw, araw, alog, dtb, ng, dog, sall, sall, tall)


def _gdn_layer_fwd(h, w, tag):
    p = _mm(h, w["w_in"], name=tag + "_in")
    qkv = _gdn_conv_fwd(p, w["conv_w"], name=tag + "_conv")
    braw = p[:, GDN_CONV_W + GDN_VW:GDN_CONV_W + GDN_VW + GDN_V_HEADS].T[:, :, None]
    araw = p[:, GDN_CONV_W + GDN_VW + GDN_V_HEADS:GDN_IN].T[:, :, None]
    og, sall, tall = _gdn_fwd(qkv, p, braw, araw, w["a_log"], w["dt_bias"], w["norm_g"], name=tag + "_scan")
    y = _mm(og, w["w_out"], name=tag + "_out")
    return y, (p, qkv, braw, araw, og, sall, tall)


def _gdn_layer_bwd(dy, h, saved, w, tag, sink=None):
    p, qkv, braw, araw, og, sall, tall = saved
    t = h.shape[0]
    dog = _mm(dy, w["w_out"], tb=True, name=tag + "_dog")
    dw_out, gf = _dw_out(og, dy, sink, tag + "_dwout")
    dq, dk, dv, dz, dbraw, daraw, dalog, ddtb, dng = _gdn_bwd(
        qkv, p, braw, araw, w["a_log"], w["dt_bias"], w["norm_g"], dog, sall, tall, name=tag + "_scanb")
    dpre_q, dcw_q = _gdn_conv_bwd(dq, p, w["conv_w"], 0, name=tag + "_convb_q")
    dpre_k, dcw_k = _gdn_conv_bwd(dk, p, w["conv_w"], GDN_QKW, name=tag + "_convb_k")
    dpre_v, dcw_v = _gdn_conv_bwd(dv, p, w["conv_w"], 2 * GDN_QKW, name=tag + "_convb_v")
    dconv_w = jnp.concatenate([dcw_q, dcw_k, dcw_v], axis=1)
    dp = jnp.concatenate([dpre_q, dpre_k, dpre_v, dz, dbraw[:, :, 0].T.astype(BF16), daraw[:, :, 0].T.astype(BF16),
                          jnp.zeros((t, GDN_IN_PAD - GDN_IN), BF16)], axis=1)
    dw_in = _mm(h, dp, ta=True, name=tag + "_dwin")
    dh = _mm(dp, w["w_in"], tb=True, name=tag + "_dh")
    grads = dict(w_in=dw_in[:, :GDN_IN], conv_w=dconv_w, a_log=dalog[:, 0, 0], dt_bias=ddtb[:, 0, 0], norm_g=dng[0], w_out=dw_out)
    return dh, grads, gf


MESH_ID = pl.DeviceIdType.MESH
FLAT_W = 1024
FLAT_ROWS = 13056
FLAT_TILE = 384


def _exchange(name, ins, out_shapes, plan, n_remote, n_local):
    def body(*refs):
        in_refs = refs[:len(ins)]
        out_refs = refs[len(ins):len(ins) + len(out_shapes)]
        ssem, rsem, lsem = refs[len(ins) + len(out_shapes):]
        x, y, c = lax.axis_index("x"), lax.axis_index("y"), lax.axis_index("c")
        stages, local_copies = plan(x, y, c, in_refs, out_refs)
        assert sum(len(s) for s in stages) == n_remote and len(local_copies) == n_local
        locs = [pltpu.make_async_copy(s, d, lsem.at[i]) for i, (s, d) in enumerate(local_copies)]
        for cp in locs:
            cp.start()
        sent = []
        k = 0
        for stage in stages:
            arrivals = []
            for src, dst, peer, landing in stage:
                cp = pltpu.make_async_remote_copy(src_ref=src, dst_ref=dst, send_sem=ssem.at[k], recv_sem=rsem.at[k],
                                                  device_id=peer, device_id_type=MESH_ID)
                cp.start()
                sent.append(cp)
                arrivals.append(pltpu.make_async_remote_copy(src_ref=src, dst_ref=landing, send_sem=ssem.at[k],
                                                             recv_sem=rsem.at[k], device_id=peer, device_id_type=MESH_ID))
                k += 1
            for cp in arrivals:
                cp.wait_recv()
        for cp in sent:
            cp.wait_send()
        for cp in locs:
            cp.wait()

    hbm = pl.BlockSpec(memory_space=pl.ANY)
    return pl.pallas_call(
        body, in_specs=[hbm] * len(ins), out_specs=[hbm] * len(out_shapes), out_shape=out_shapes,
        scratch_shapes=[pltpu.SemaphoreType.DMA((n_remote,)), pltpu.SemaphoreType.DMA((n_remote,)),
                        pltpu.SemaphoreType.DMA((max(n_local, 1),))],
        name=name)(*ins)


def _other_chips(x, y):
    return [(1 - x, y), (x, 1 - y), (1 - x, 1 - y)]


def _all8_gather(a, *, name):
    def plan(x, y, c, ins, outs):
        (src,), (dst,) = ins, outs
        me = 4 * x + 2 * y + c
        stage = []
        for fx, fy, fc in [(0, 0, 1), (0, 1, 0), (0, 1, 1), (1, 0, 0), (1, 0, 1), (1, 1, 0), (1, 1, 1)]:
            px, py, pc = (1 - x if fx else x), (1 - y if fy else y), (1 - c if fc else c)
            stage.append((src, dst.at[me], (px, py, pc), dst.at[4 * px + 2 * py + pc]))
        return [stage], [(src, dst.at[me])]

    return _exchange(name, [a], [jax.ShapeDtypeStruct((8,) + a.shape, a.dtype)], plan, 7, 1)[0]


def _chip_gather(flat, *, name):
    rows = flat.shape[0]
    half = rows // 2

    def plan(x, y, c, ins, outs):
        (src,), (dst,) = ins, outs
        me = 2 * x + y
        mine = pl.ds(c * half, half)
        theirs = pl.ds((1 - c) * half, half)
        ici = [(src.at[mine], dst.at[me, mine], (px, py, c), dst.at[2 * px + py, mine]) for px, py in _other_chips(x, y)]
        d2d = [(dst.at[2 * px + py, mine], dst.at[2 * px + py, mine], (x, y, 1 - c), dst.at[2 * px + py, theirs])
               for px, py in _other_chips(x, y)]
        return [ici, d2d], []

    return _exchange(name, [flat], [jax.ShapeDtypeStruct((4,) + flat.shape, flat.dtype)], plan, 6, 0)[0]


def _add_sibling(gf, buf_a, core, *, name):
    _, rows, w = gf.shape
    half = rows // 2
    nb = half // FLAT_TILE

    def body(c_ref, g_ref, a_ref, o_ref):
        o_ref[...] = (g_ref[...].astype(F32) + a_ref[...].astype(F32)).astype(BF16)

    blk = (1, FLAT_TILE, w)
    return pl.pallas_call(
        body,
        grid_spec=pltpu.PrefetchScalarGridSpec(
            num_scalar_prefetch=1, grid=(4, nb),
            in_specs=[pl.BlockSpec(blk, lambda s, i, c_ref: (s, c_ref[0] * nb + i, 0)), pl.BlockSpec(blk, lambda s, i, c_ref: (s, i, 0))],
            out_specs=pl.BlockSpec(blk, lambda s, i, c_ref: (s, i, 0))),
        out_shape=jax.ShapeDtypeStruct((4, half, w), BF16), compiler_params=_cp("parallel", "parallel"), name=name)(core, gf, buf_a)


def _sum_chips(hsum, buf_b, chip, *, name):
    _, half, w = hsum.shape
    nb = half // FLAT_TILE

    def body(c_ref, h_ref, b0_ref, b1_ref, b2_ref, b3_ref, o_ref):
        me = c_ref[0]
        own = h_ref[0].astype(F32)
        acc = None
        for j, b_ref in enumerate((b0_ref, b1_ref, b2_ref, b3_ref)):
            term = jnp.where(me == j, own, b_ref[0].astype(F32))
            acc = term if acc is None else acc + term
        o_ref[...] = acc

    blk = (1, FLAT_TILE, w)

    def other(j):
        return pl.BlockSpec(blk, lambda i, c_ref: (jnp.where(c_ref[0] == j, (j + 1) % 4, j), i, 0))

    return pl.pallas_call(
        body,
        grid_spec=pltpu.PrefetchScalarGridSpec(
            num_scalar_prefetch=1, grid=(nb,),
            in_specs=[pl.BlockSpec(blk, lambda i, c_ref: (c_ref[0], i, 0))] + [other(j) for j in range(4)],
            out_specs=pl.BlockSpec((FLAT_TILE, w), lambda i, c_ref: (i, 0))),
        out_shape=jax.ShapeDtypeStruct((half, w), F32), compiler_params=_cp("parallel"), name=name)(chip, hsum, buf_b, buf_b, buf_b, buf_b)


def _sum_slots(buf, *, name):
    n, rows, w = buf.shape
    tr = _pick(rows, (FLAT_TILE, 8))

    def body(b_ref, o_ref):
        acc = b_ref[0]
        for s in range(1, n):
            acc = acc + b_ref[s]
        o_ref[...] = acc

    return pl.pallas_call(body, grid=(rows // tr,), in_specs=[pl.BlockSpec((n, tr, w), lambda i: (0, i, 0))],
                          out_specs=pl.BlockSpec((tr, w), lambda i: (i, 0)), out_shape=jax.ShapeDtypeStruct((rows, w), F32),
                          compiler_params=_cp("parallel"), name=name)(buf)


def _reduce_scatter(gf, core, chip, *, tag):
    _, rows, w = gf.shape
    half = rows // 2

    def plan_a(x, y, c, ins, outs):
        (src,), (dst,) = ins, outs
        return [[(src.at[:, pl.ds((1 - c) * half, half)], dst, (x, y, 1 - c), dst)]], []

    buf_a = _exchange(tag + "_sibling", [gf], [jax.ShapeDtypeStruct((4, half, w), gf.dtype)], plan_a, 1, 0)[0]
    hsum = _add_sibling(gf, buf_a, core, name=tag + "_add_sibling")

    def plan_b(x, y, c, ins, outs):
        (src,), (dst,) = ins, outs
        me = 2 * x + y
        stage = [(src.at[2 * px + py], dst.at[me], (px, py, c), dst.at[2 * px + py]) for px, py in _other_chips(x, y)]
        return [stage], []

    buf_b = _exchange(tag + "_chips", [hsum], [jax.ShapeDtypeStruct((4, half, w), BF16)], plan_b, 3, 0)[0]
    mine = _sum_chips(hsum, buf_b, chip, name=tag + "_sum_chips")

    def plan_c(x, y, c, ins, outs):
        (src,), (dst,) = ins, outs
        return [[(src, dst, (x, y, 1 - c), dst)]], []

    theirs = _exchange(tag + "_halves", [mine], [jax.ShapeDtypeStruct((half, w), F32)], plan_c, 1, 0)[0]
    first = core[0] == 0
    return jnp.concatenate([jnp.where(first, mine, theirs), jnp.where(first, theirs, mine)], axis=0)


WEIGHTS = ["ada_w", "ada_b", "norm_pre_g", "norm_post_g", "gla_w_in", "gla_w_gate_up", "gla_b_gate", "gla_head_g",
           "gla_w_out", "mla_w_in", "mla_q_norm_g", "mla_w_uq", "mla_kv_norm_g", "mla_w_ukv", "mla_w_out", "gdn_w_in",
           "gdn_conv_w", "gdn_a_log", "gdn_dt_bias", "gdn_norm_g", "gdn_w_out", "mlp_w_up", "mlp_w_down"]
PACK_BF16 = [("mlp_w_up", 2), ("mlp_w_down", 1), ("gdn_w_out", 1), ("gla_w_out", 1), ("mla_w_out", 1),
             ("gla_w_in", 2), ("mla_w_in", 1), ("mla_w_uq", 2), ("mla_w_ukv", 2), ("gdn_w_in", 2)]
N_DIRECT = 5
PACK_F32 = [("norm_pre_g", 2), ("norm_post_g", 2), ("gla_w_gate_up", 2), ("gla_b_gate", 1), ("gla_head_g", 1), ("gdn_conv_w", 2)]
REPLICATED_SMALL = ["mla_q_norm_g", "mla_kv_norm_g", "gdn_a_log", "gdn_dt_bias", "gdn_norm_g"]
MIXERS = ["gla", "mla", "gdn"]


def _silu_rows(a, *, name):
    def body(a_ref, o_ref):
        v = a_ref[...]
        o_ref[...] = v * _sigmoid(v)

    return pl.pallas_call(body, out_shape=jax.ShapeDtypeStruct(a.shape, F32), name=name)(a)


SMALL_ROWS = 16


def _piece_rows(size, mult):
    assert size % FLAT_W == 0
    return -(-(size // FLAT_W) // mult) * mult


def _to_rows(a, lead, mult):
    n = math.prod(a.shape[len(lead):])
    r = a.reshape(lead + (n // FLAT_W, FLAT_W))
    extra = _piece_rows(n, mult) - n // FLAT_W
    return jnp.pad(r, [(0, 0)] * len(lead) + [(0, extra), (0, 0)]) if extra else r


def _small_to_rows(parts, lead):
    flat = jnp.concatenate([p.reshape(lead + (-1,)) for p in parts], axis=-1)
    pad = SMALL_ROWS * FLAT_W - flat.shape[-1]
    return jnp.pad(flat, [(0, 0)] * len(lead) + [(0, pad)]).reshape(lead + (SMALL_ROWS, FLAT_W))


def _small_from_rows(rows, shards, lead):
    flat = rows.reshape(lead + (-1,))
    out, off = {}, 0
    for n, _ in PACK_F32:
        out[n] = flat[..., off:off + shards[n].size].reshape(lead + shards[n].shape)
        off += shards[n].size
    return out


def _pack_weights(shards):
    parts = [_to_rows(shards[n].astype(BF16), (), 16) for n, _ in PACK_BF16]
    small = _small_to_rows([shards[n] for n, _ in PACK_F32], ())
    parts.append(lax.bitcast_convert_type(small, BF16).reshape(2 * SMALL_ROWS, FLAT_W))
    flat = jnp.concatenate(parts, axis=0)
    return jnp.pad(flat, ((0, FLAT_ROWS - flat.shape[0]), (0, 0)))


def _unpack_weights(gathered, shards):
    full, off = {}, 0
    for n, ax in PACK_BF16:
        size = shards[n].size
        seg = gathered[:, off:off + size // FLAT_W].reshape((4,) + shards[n].shape)
        full[n] = jnp.concatenate([seg[j] for j in range(4)], axis=ax)
        off += _piece_rows(size, 16)
    small = lax.bitcast_convert_type(gathered[:, off:off + 2 * SMALL_ROWS].reshape(4, SMALL_ROWS, FLAT_W, 2), F32)
    for (n, ax), seg in zip(PACK_F32, _small_from_rows(small, shards, (4,)).values()):
        full[n] = jnp.concatenate([seg[j] for j in range(4)], axis=ax)
    return full


def _grad_layout(shards):
    layout, off = {}, 0
    for n, _ in PACK_BF16:
        layout[n] = (off, shards[n].size // shards[n].shape[0] // FLAT_W)
        off += _piece_rows(shards[n].size, 16)
    layout["small"] = (off, SMALL_ROWS)
    return layout


def _pack_grads(gf, grads, layout):
    by_chip = lambda g, ax: jnp.stack(jnp.split(g.astype(gf.dtype), 4, axis=ax - 1))
    parts = []
    for n, ax in PACK_BF16[N_DIRECT:]:
        rows = sum(g.size for g in grads[n]) // (4 * FLAT_W)
        parts += [by_chip(g, ax).reshape(4, -1, FLAT_W) for g in grads[n]]
        if _piece_rows(rows * FLAT_W, 16) > rows:
            parts.append(jnp.zeros((4, _piece_rows(rows * FLAT_W, 16) - rows, FLAT_W), gf.dtype))
    parts.append(_small_to_rows([jnp.stack([by_chip(g, ax) for g in grads[n]], axis=1) for n, ax in PACK_F32], (4,)))
    first = layout[PACK_BF16[N_DIRECT][0]][0]
    rest = jnp.concatenate(parts, axis=1)
    assert first + rest.shape[1] == layout["small"][0] + SMALL_ROWS
    return lax.dynamic_update_slice(gf, rest, (0, first, 0))


def _unpack_grads(reduced, shards):
    out, off = {}, 0
    for n, _ in PACK_BF16:
        size = shards[n].size
        out[n] = reduced[off:off + size // FLAT_W].reshape(shards[n].shape)
        off += _piece_rows(size, 16)
    out.update(_small_from_rows(reduced[off:off + SMALL_ROWS], shards, ()))
    return out


def _mixer_weights(kind, j, full, rep):
    if kind == "gla":
        return dict(w_in=jnp.pad(full["gla_w_in"][j], ((0, 0), (0, GLA_IN_PAD - GLA_IN))),
                    wg=jnp.pad(full["gla_w_gate_up"][j], ((0, LANES - GLA_RANK), (0, 0))),
                    bg=full["gla_b_gate"][j][None], hg=full["gla_head_g"][j][None], w_out=full["gla_w_out"][j])
    if kind == "mla":
        return dict(w_in=jnp.pad(full["mla_w_in"][j], ((0, 0), (0, MLA_IN_PAD - MLA_IN))), q_norm_g=rep["mla_q_norm_g"][j][None],
                    w_uq=full["mla_w_uq"][j], kv_norm_g=rep["mla_kv_norm_g"][j][None], w_ukv=full["mla_w_ukv"][j],
                    w_out=full["mla_w_out"][j])
    return dict(w_in=jnp.pad(full["gdn_w_in"][j], ((0, 0), (0, GDN_IN_PAD - GDN_IN))), conv_w=full["gdn_conv_w"][j],
                a_log=rep["gdn_a_log"][j][:, None, None], dt_bias=rep["gdn_dt_bias"][j][:, None, None],
                norm_g=rep["gdn_norm_g"][j][None], w_out=full["gdn_w_out"][j])


def _layer_fwd(xin, mod, gains, kind, mw, w_up, w_down, pos, tag):
    sh_m, sc_m, gt_m, sh_f, sc_f, gt_f = mod
    pre0, pre1, post0, post1 = gains
    h = _premod_fwd(xin, pre0, sc_m, sh_m, name=tag + "_pre0")
    if kind == "gla":
        y, saved = _gla_layer_fwd(h, mw, tag + "_gla")
    elif kind == "mla":
        y, saved = _mla_layer_fwd(h, pos, mw, tag + "_mla")
    else:
        y, saved = _gdn_layer_fwd(h, mw, tag + "_gdn")
    x1 = _postres_fwd(xin, y, post0, gt_m, name=tag + "_post0")
    h2 = _premod_fwd(x1, pre1, sc_f, sh_f, name=tag + "_pre1")
    act = _mm(h2, w_up, out_dtype=BF16, epi="relu2", name=tag + "_up")
    y2 = _mm(act, w_down, name=tag + "_down")
    x2 = _postres_fwd(x1, y2, post1, gt_f, name=tag + "_post1")
    return x2, (xin, h, y, saved, x1, h2, act, y2)


def _layer_bwd(g2, kept, mod, gains, kind, mw, w_up, w_down, tag, gf, rows):
    xin, h, y, saved, x1, h2, act, y2 = kept
    sh_m, sc_m, gt_m, sh_f, sc_f, gt_f = mod
    pre0, pre1, post0, post1 = gains
    dy2, dpost1, dgt_f = _postres_bwd(g2, y2, post1, gt_f, name=tag + "_post1_b")
    du = _mm(dy2, w_down, tb=True, out_dtype=BF16, epi="dact", aux=act, name=tag + "_du")
    gf = _mm(act, dy2, ta=True, out_dtype=gf.dtype, into=(gf, rows[1], "rows"), name=tag + "_dwdown")
    gf = _mm(h2, du, ta=True, out_dtype=gf.dtype, into=(gf, rows[0], "cols"), name=tag + "_dwup")
    dh2 = _mm(du, w_up, tb=True, name=tag + "_dh2")
    g1, dpre1, dsc_f, dsh_f = _premod_bwd(dh2, x1, pre1, sc_f, g2, name=tag + "_pre1_b")
    dy, dpost0, dgt_m = _postres_bwd(g1, y, post0, gt_m, name=tag + "_post0_b")
    mixer_bwd = dict(gla=_gla_layer_bwd, mla=_mla_layer_bwd, gdn=_gdn_layer_bwd)[kind]
    dh, mg, gf = mixer_bwd(dy, h, saved, mw, tag + "_" + kind, sink=(gf, rows[2]))
    g0, dpre0, dsc_m, dsh_m = _premod_bwd(dh, xin, pre0, sc_m, g1, name=tag + "_pre0_b")
    dmod = jnp.concatenate([dsh_m, dsc_m, dgt_m, dsh_f, dsc_f, dgt_f], axis=1)
    return g0, dmod, jnp.concatenate([dpre0, dpre1], axis=0), jnp.concatenate([dpost0, dpost1], axis=0), mg, gf


def kernel(x, c, positions, ada_w, ada_b, norm_pre_g, norm_post_g, gla_w_in, gla_w_gate_up, gla_b_gate, gla_head_g, gla_w_out, mla_w_in, mla_q_norm_g, mla_w_uq, mla_kv_norm_g, mla_w_ukv, mla_w_out, gdn_w_in, gdn_conv_w, gdn_a_log, gdn_dt_bias, gdn_norm_g, gdn_w_out, mlp_w_up, mlp_w_down, loss_target, m_ada_w, m_ada_b, m_norm_pre_g, m_norm_post_g, m_gla_w_in, m_gla_w_gate_up, m_gla_b_gate, m_gla_head_g, m_gla_w_out, m_mla_w_in, m_mla_q_norm_g, m_mla_w_uq, m_mla_kv_norm_g, m_mla_w_ukv, m_mla_w_out, m_gdn_w_in, m_gdn_conv_w, m_gdn_a_log, m_gdn_dt_bias, m_gdn_norm_g, m_gdn_w_out, m_mlp_w_up, m_mlp_w_down, v_ada_w, v_ada_b, v_norm_pre_g, v_norm_post_g, v_gla_w_in, v_gla_w_gate_up, v_gla_b_gate, v_gla_head_g, v_gla_w_out, v_mla_w_in, v_mla_q_norm_g, v_mla_w_uq, v_mla_kv_norm_g, v_mla_w_ukv, v_mla_w_out, v_gdn_w_in, v_gdn_conv_w, v_gdn_a_log, v_gdn_dt_bias, v_gdn_norm_g, v_gdn_w_out, v_mlp_w_up, v_mlp_w_down):
    w = dict(ada_w=ada_w, ada_b=ada_b, norm_pre_g=norm_pre_g, norm_post_g=norm_post_g, gla_w_in=gla_w_in,
             gla_w_gate_up=gla_w_gate_up, gla_b_gate=gla_b_gate, gla_head_g=gla_head_g, gla_w_out=gla_w_out, mla_w_in=mla_w_in,
             mla_q_norm_g=mla_q_norm_g, mla_w_uq=mla_w_uq, mla_kv_norm_g=mla_kv_norm_g, mla_w_ukv=mla_w_ukv, mla_w_out=mla_w_out,
             gdn_w_in=gdn_w_in, gdn_conv_w=gdn_conv_w, gdn_a_log=gdn_a_log, gdn_dt_bias=gdn_dt_bias, gdn_norm_g=gdn_norm_g,
             gdn_w_out=gdn_w_out, mlp_w_up=mlp_w_up, mlp_w_down=mlp_w_down)
    m = dict(zip(WEIGHTS, [m_ada_w, m_ada_b, m_norm_pre_g, m_norm_post_g, m_gla_w_in, m_gla_w_gate_up, m_gla_b_gate, m_gla_head_g,
                           m_gla_w_out, m_mla_w_in, m_mla_q_norm_g, m_mla_w_uq, m_mla_kv_norm_g, m_mla_w_ukv, m_mla_w_out,
                           m_gdn_w_in, m_gdn_conv_w, m_gdn_a_log, m_gdn_dt_bias, m_gdn_norm_g, m_gdn_w_out, m_mlp_w_up, m_mlp_w_down]))
    v = dict(zip(WEIGHTS, [v_ada_w, v_ada_b, v_norm_pre_g, v_norm_post_g, v_gla_w_in, v_gla_w_gate_up, v_gla_b_gate, v_gla_head_g,
                           v_gla_w_out, v_mla_w_in, v_mla_q_norm_g, v_mla_w_uq, v_mla_kv_norm_g, v_mla_w_ukv, v_mla_w_out,
                           v_gdn_w_in, v_gdn_conv_w, v_gdn_a_log, v_gdn_dt_bias, v_gdn_norm_g, v_gdn_w_out, v_mlp_w_up, v_mlp_w_down]))
    t = x.shape[1]
    ix, iy, ic = lax.axis_index("x"), lax.axis_index("y"), lax.axis_index("c")
    me = 4 * ix + 2 * iy + ic
    chip = 2 * ix + iy
    ada_cols = ada_w.shape[2]

    packed = _pack_weights(w)
    zero = jnp.zeros((), jnp.int32)
    gathered = lax.dynamic_update_slice(_chip_gather(packed, name="gather_weights"), packed[None], (chip, zero, zero))
    full = _unpack_weights(gathered, w)

    cond8 = _silu_rows(jnp.pad(c, ((0, 7), (0, 0))), name="cond_silu")
    cond16 = jnp.pad(_all8_gather(cond8, name="gather_cond")[:, 0, :], ((0, 8), (0, 0)))
    mod_cols = []
    for layer in range(DEPTH):
        bias = jnp.broadcast_to(lax.dynamic_slice_in_dim(ada_b[layer], chip * ada_cols, ada_cols)[None], (16, ada_cols))
        mod_cols.append(_mm(cond16, ada_w[layer], epi="add", aux=bias, name=f"ada{layer}")[:8])
    mod_all = _all8_gather(jnp.stack(mod_cols).reshape(DEPTH * 8, ada_cols), name="gather_mod")
    mod = jnp.concatenate([lax.dynamic_slice_in_dim(mod_all[2 * j].reshape(DEPTH, 8, ada_cols), me, 1, axis=1)[:, 0]
                           for j in range(4)], axis=1)

    def layer_args(layer):
        kind, j = MIXERS[layer % 3], layer // 3
        mods = [mod[layer, i * D_MODEL:(i + 1) * D_MODEL][None] for i in range(N_MOD)]
        gains = (full["norm_pre_g"][layer, 0:1], full["norm_pre_g"][layer, 1:2], full["norm_post_g"][layer, 0:1],
                 full["norm_post_g"][layer, 1:2])
        return kind, j, mods, gains, _mixer_weights(kind, j, full, w)

    xs = x[0]
    kept = []
    for layer in range(DEPTH):
        kind, j, mods, gains, mw = layer_args(layer)
        xs, keep = _layer_fwd(xs, mods, gains, kind, mw, full["mlp_w_up"][layer], full["mlp_w_down"][layer], positions[0], f"l{layer}")
        kept.append(keep)
    loss_row, g = _loss_head(xs, loss_target[0], name="loss_head")
    loss = lax.psum(loss_row[0, 0], ("x", "y", "c"))

    grads = {n: [None] * w[n].shape[0] for n, _ in PACK_BF16[N_DIRECT:] + PACK_F32}
    rep_grads = {}
    dmods = [None] * DEPTH
    layout = _grad_layout(w)
    row_of = lambda n, idx: layout[n][0] + idx * layout[n][1]
    gf = lax.empty((4, FLAT_ROWS, FLAT_W), BF16)
    for layer in reversed(range(DEPTH)):
        kind, j, mods, gains, mw = layer_args(layer)
        rows = (row_of("mlp_w_up", layer), row_of("mlp_w_down", layer), row_of(kind + "_w_out", j))
        g, dmods[layer], dpre, dpost, mg, gf = _layer_bwd(
            g, kept[layer], mods, gains, kind, mw, full["mlp_w_up"][layer], full["mlp_w_down"][layer], f"l{layer}", gf, rows)
        grads["norm_pre_g"][layer], grads["norm_post_g"][layer] = dpre, dpost
        for key, val in mg.items():
            name = kind + "_" + key
            if name in grads:
                grads[name][j] = val
            elif name in REPLICATED_SMALL:
                rep_grads[name] = val[None]

    rep_flat = jnp.concatenate([rep_grads[n].reshape(-1) for n in REPLICATED_SMALL])
    dbuf = jnp.concatenate([jnp.concatenate(dmods, axis=0), jnp.pad(rep_flat, (0, N_MOD * D_MODEL - rep_flat.shape[0]))[None],
                            jnp.zeros((3, N_MOD * D_MODEL), F32)], axis=0)
    dall = _all8_gather(dbuf, name="gather_dmod")
    dsum = _sum_slots(dall, name="sum_dmod")
    out_grads = {"ada_b": dsum[:DEPTH]}
    off = 0
    for n in REPLICATED_SMALL:
        out_grads[n] = dsum[DEPTH, off:off + w[n].size].reshape(w[n].shape)
        off += w[n].size
    dada = []
    for layer in range(DEPTH):
        dm16 = jnp.pad(lax.dynamic_slice_in_dim(dall[:, layer, :], chip * ada_cols, ada_cols, axis=1), ((0, 8), (0, 0)))
        dada.append(_mm(cond16, dm16, ta=True, name=f"dada{layer}"))
    out_grads["ada_w"] = jnp.stack(dada)

    reduced = _reduce_scatter(_pack_grads(gf, grads, layout), ic.reshape(1).astype(jnp.int32),
                              chip.reshape(1).astype(jnp.int32), tag="reduce_grads")
    out_grads.update(_unpack_grads(reduced, w))

    deltas, new_m, new_v = {}, {}, {}
    for n in WEIGHTS:
        deltas[n], new_m[n], new_v[n] = _adamw(w[n], out_grads[n], m[n], v[n], name="adamw_" + n)
    return (loss, g[None], *[out_grads[n] for n in WEIGHTS], *[deltas[n] for n in WEIGHTS],
            *[new_m[n] for n in WEIGHTS], *[new_v[n] for n in WEIGHTS])
```

```python
import functools
import math

import jax
import jax.numpy as jnp
from jax import lax
from jax.experimental import pallas as pl
from jax.experimental.pallas import tpu as pltpu

F32 = jnp.float32
BF16 = jnp.bfloat16

D_MODEL = 1024
DEPTH = 4
CHUNK = 64
EPS = 1e-6
NEG_INF = -1e30
N_MOD = 6

GLA_HEADS, GLA_DK, GLA_DV, GLA_RANK = 4, 128, 256, 16
GLA_KW, GLA_VW = GLA_HEADS * GLA_DK, GLA_HEADS * GLA_DV
GLA_IN = 2 * GLA_KW + 2 * GLA_VW + GLA_RANK
GLA_IN_PAD = 3200

MLA_HEADS, MLA_NOPE, MLA_ROPE, MLA_V = 16, 64, 32, 64
MLA_Q_RANK, MLA_KV_RANK = 384, 256
MLA_IN = MLA_Q_RANK + MLA_KV_RANK + MLA_ROPE
MLA_IN_PAD = 768
ROPE_THETA = 10000.0
MLA_QK = MLA_NOPE + MLA_ROPE
LANES = 128

GDN_K_HEADS, GDN_V_HEADS, GDN_DK, GDN_DV, GDN_CONV = 8, 16, 128, 128, 4
GDN_QKW, GDN_VW = GDN_K_HEADS * GDN_DK, GDN_V_HEADS * GDN_DV
GDN_CONV_W = 2 * GDN_QKW + GDN_VW
GDN_IN = GDN_CONV_W + GDN_VW + 2 * GDN_V_HEADS
GDN_IN_PAD = 6400

ADAM_LR, ADAM_B1, ADAM_B2, ADAM_EPS, ADAM_WD, ADAM_STEP = 0.001, 0.9, 0.999, 1e-08, 0.01, 10

VMEM_LIMIT = 56 * 1024 * 1024

NN = ((1,), (0,))
NT = ((1,), (1,))
TN = ((0,), (0,))


def _cp(*sem):
    return pltpu.CompilerParams(dimension_semantics=sem, vmem_limit_bytes=VMEM_LIMIT)


def _pick(n, cands):
    for c in cands:
        if n % c == 0:
            return c
    return n


def _dg(a, b, dims=NN):
    return lax.dot_general(a.astype(BF16), b.astype(BF16), (dims, ((), ())), preferred_element_type=F32)


def _dot3(a, b, dims=NN):
    ah = a.astype(BF16)
    al = (a - ah.astype(F32)).astype(BF16)
    bh = b.astype(BF16)
    bl = (b - bh.astype(F32)).astype(BF16)
    d = lambda u, v: lax.dot_general(u, v, (dims, ((), ())), preferred_element_type=F32)
    return d(ah, bh) + (d(ah, bl) + d(al, bh))


def _sigmoid(x):
    return 1.0 / (1.0 + jnp.exp(-x))


def _softplus(x):
    return jnp.maximum(x, 0.0) + jnp.log(1.0 + jnp.exp(-jnp.abs(x)))


def _iota2(shape, dim):
    return lax.broadcasted_iota(jnp.int32, shape, dim)


def _mm(a, b, *, ta=False, tb=False, out_dtype=F32, epi=None, aux=None, into=None, name):
    m = a.shape[1] if ta else a.shape[0]
    k = a.shape[0] if ta else a.shape[1]
    n = b.shape[0] if tb else b.shape[1]
    assert k == (b.shape[1] if tb else b.shape[0]), (a.shape, b.shape, ta, tb)
    m_tile = m // 4 if into is not None and into[2] == "rows" else m
    tm = _pick(m_tile, (1024, 512, 384, 256, 128))
    tn = _pick(n, (1024, 640, 512, 768, 384, 256, 128))
    tk = _pick(k, (1024, 640, 512, 768, 384, 256, 128))
    nk = k // tk
    dims = ((0 if ta else 1,), (1 if tb else 0,))

    def finish(r, x_ref, o_ref):
        if epi == "relu2":
            r = jnp.square(jnp.maximum(r, 0.0))
        elif epi == "dact":
            r = r * (2.0 * jnp.sqrt(x_ref[...].astype(F32)))
        elif epi == "add":
            r = r + x_ref[...]
        o_ref[...] = r.astype(out_dtype)

    n_in = 2 + (aux is not None) + (into is not None)

    def body(*refs):
        a_ref, b_ref = refs[:2]
        x_ref = refs[2] if aux is not None else None
        o_ref = refs[n_in]
        if nk == 1:
            finish(_dg(a_ref[...], b_ref[...], dims), x_ref, o_ref)
            return
        acc = refs[-1]
        kk = pl.program_id(2)

        @pl.when(kk == 0)
        def _():
            acc[...] = jnp.zeros_like(acc)

        acc[...] += _dg(a_ref[...], b_ref[...], dims)

        @pl.when(kk == nk - 1)
        def _():
            finish(acc[...], x_ref, o_ref)

    a_spec = pl.BlockSpec((tk, tm), lambda i, j, q: (q, i)) if ta else pl.BlockSpec((tm, tk), lambda i, j, q: (i, q))
    b_spec = pl.BlockSpec((tn, tk), lambda i, j, q: (j, q)) if tb else pl.BlockSpec((tk, tn), lambda i, j, q: (q, j))
    o_spec = pl.BlockSpec((tm, tn), lambda i, j, q: (i, j))
    in_specs = [a_spec, b_spec] + ([o_spec] if aux is not None else [])
    args = (a, b) + ((aux,) if aux is not None else ())
    out_shape = jax.ShapeDtypeStruct((m, n), out_dtype)
    aliases = {}
    if into is not None:
        dst, row0, axis = into
        assert dst.dtype == out_dtype and row0 % tm == 0 and tn == FLAT_W and n == (FLAT_W if axis == "rows" else 4 * FLAT_W)
        per = m_tile // tm
        if axis == "rows":
            o_spec = pl.BlockSpec((None, tm, tn), lambda i, j, q: (i // per, row0 // tm + i % per, 0))
        else:
            o_spec = pl.BlockSpec((None, tm, tn), lambda i, j, q: (j, row0 // tm + i, 0))
        in_specs.append(pl.BlockSpec(memory_space=pl.ANY))
        args += (dst,)
        out_shape = jax.ShapeDtypeStruct(dst.shape, dst.dtype)
        aliases = {n_in - 1: 0}
    return pl.pallas_call(
        body, grid=(m // tm, n // tn, nk), in_specs=in_specs, out_specs=o_spec, out_shape=out_shape,
        scratch_shapes=[pltpu.VMEM((tm, tn), F32)] if nk > 1 else [], input_output_aliases=aliases,
        compiler_params=_cp("parallel", "parallel", "arbitrary"), name=name)(*args)


def _row_tile(t):
    return _pick(t, (512, 256, 128, 64, 8))


def _premod_fwd(x, g, sc, sh, *, name):
    t, c = x.shape
    tr = _row_tile(t)

    def body(x_ref, g_ref, sc_ref, sh_ref, h_ref):
        xv = x_ref[...]
        r = lax.rsqrt(jnp.mean(xv * xv, axis=-1, keepdims=True) + EPS)
        h_ref[...] = (((xv * r) * g_ref[...]) * (1.0 + sc_ref[...]) + sh_ref[...]).astype(BF16)

    row = pl.BlockSpec((tr, c), lambda i: (i, 0))
    vec = pl.BlockSpec((1, c), lambda i: (0, 0))
    return pl.pallas_call(body, grid=(t // tr,), in_specs=[row, vec, vec, vec], out_specs=row,
                          out_shape=jax.ShapeDtypeStruct((t, c), BF16), compiler_params=_cp("parallel"), name=name)(x, g, sc, sh)


def _premod_bwd(dh, x, g, sc, gin, *, name):
    t, c = x.shape
    tr = _row_tile(t)

    def body(dh_ref, x_ref, g_ref, sc_ref, gin_ref, gout_ref, dg_ref, dsc_ref, dsh_ref):
        @pl.when(pl.program_id(0) == 0)
        def _():
            dg_ref[...] = jnp.zeros_like(dg_ref)
            dsc_ref[...] = jnp.zeros_like(dsc_ref)
            dsh_ref[...] = jnp.zeros_like(dsh_ref)

        xv = x_ref[...]
        dhv = dh_ref[...].astype(F32)
        gv = g_ref[...]
        one_sc = 1.0 + sc_ref[...]
        r = lax.rsqrt(jnp.mean(xv * xv, axis=-1, keepdims=True) + EPS)
        nv = xv * r
        dsh_ref[...] += jnp.sum(dhv, axis=0, keepdims=True)
        dsc_ref[...] += jnp.sum(dhv * (nv * gv), axis=0, keepdims=True)
        dg_ref[...] += jnp.sum(dhv * nv * one_sc, axis=0, keepdims=True)
        dn = dhv * gv * one_sc
        dx = r * (dn - nv * jnp.mean(dn * nv, axis=-1, keepdims=True))
        gout_ref[...] = gin_ref[...] + dx

    row = pl.BlockSpec((tr, c), lambda i: (i, 0))
    vec = pl.BlockSpec((1, c), lambda i: (0, 0))
    vs = jax.ShapeDtypeStruct((1, c), F32)
    return pl.pallas_call(body, grid=(t // tr,), in_specs=[row, row, vec, vec, row], out_specs=[row, vec, vec, vec],
                          out_shape=[jax.ShapeDtypeStruct((t, c), F32), vs, vs, vs],
                          compiler_params=_cp("arbitrary"), name=name)(dh, x, g, sc, gin)


def _postres_fwd(x, y, g, gt, *, name):
    t, c = x.shape
    tr = _row_tile(t)

    def body(x_ref, y_ref, g_ref, gt_ref, o_ref):
        yv = y_ref[...]
        r = lax.rsqrt(jnp.mean(yv * yv, axis=-1, keepdims=True) + EPS)
        o_ref[...] = x_ref[...] + gt_ref[...] * ((yv * r) * g_ref[...])

    row = pl.BlockSpec((tr, c), lambda i: (i, 0))
    vec = pl.BlockSpec((1, c), lambda i: (0, 0))
    return pl.pallas_call(body, grid=(t // tr,), in_specs=[row, row, vec, vec], out_specs=row,
                          out_shape=jax.ShapeDtypeStruct((t, c), F32), compiler_params=_cp("parallel"), name=name)(x, y, g, gt)


def _postres_bwd(gout, y, g, gt, *, name):
    t, c = y.shape
    tr = _row_tile(t)

    def body(go_ref, y_ref, g_ref, gt_ref, dy_ref, dg_ref, dgt_ref):
        @pl.when(pl.program_id(0) == 0)
        def _():
            dg_ref[...] = jnp.zeros_like(dg_ref)
            dgt_ref[...] = jnp.zeros_like(dgt_ref)

        yv = y_ref[...]
        gov = go_ref[...]
        gv = g_ref[...]
        gtv = gt_ref[...]
        r = lax.rsqrt(jnp.mean(yv * yv, axis=-1, keepdims=True) + EPS)
        z = yv * r
        dgt_ref[...] += jnp.sum(gov * (z * gv), axis=0, keepdims=True)
        dg_ref[...] += jnp.sum(gov * gtv * z, axis=0, keepdims=True)
        dz = gov * gtv * gv
        dy_ref[...] = (r * (dz - z * jnp.mean(dz * z, axis=-1, keepdims=True))).astype(BF16)

    row = pl.BlockSpec((tr, c), lambda i: (i, 0))
    vec = pl.BlockSpec((1, c), lambda i: (0, 0))
    vs = jax.ShapeDtypeStruct((1, c), F32)
    return pl.pallas_call(body, grid=(t // tr,), in_specs=[row, row, vec, vec], out_specs=[row, vec, vec],
                          out_shape=[jax.ShapeDtypeStruct((t, c), BF16), vs, vs],
                          compiler_params=_cp("arbitrary"), name=name)(gout, y, g, gt)


def _loss_head(y, tgt, *, name):
    t, c = y.shape
    tr = _row_tile(t)

    def body(y_ref, t_ref, l_ref, dy_ref):
        @pl.when(pl.program_id(0) == 0)
        def _():
            l_ref[...] = jnp.zeros_like(l_ref)

        d = y_ref[...] - t_ref[...]
        dy_ref[...] = d * (1.0 / c)
        l_ref[...] += 0.5 * jnp.sum(jnp.mean(d * d, axis=-1, keepdims=True))

    row = pl.BlockSpec((tr, c), lambda i: (i, 0))
    return pl.pallas_call(body, grid=(t // tr,), in_specs=[row, row],
                          out_specs=[pl.BlockSpec((1, LANES), lambda i: (0, 0)), row],
                          out_shape=[jax.ShapeDtypeStruct((1, LANES), F32), jax.ShapeDtypeStruct((t, c), F32)],
                          compiler_params=_cp("arbitrary"), name=name)(y, tgt)


def _adamw(w, g, m, v, *, name):
    shape = w.shape
    c = shape[-1]
    r = math.prod(shape[:-1])
    w2, g2, m2, v2 = (a.reshape(r, c) for a in (w, g, m, v))
    tr = r
    for cand in (1024, 512, 256, 128, 64, 32, 16, 8):
        if r % cand == 0 and cand * c * 4 <= (1 << 20):
            tr = cand
            break
    c1 = 1.0 - ADAM_B1 ** ADAM_STEP
    c2 = 1.0 - ADAM_B2 ** ADAM_STEP

    def body(w_ref, g_ref, m_ref, v_ref, d_ref, nm_ref, nv_ref):
        gv = g_ref[...]
        mn = ADAM_B1 * m_ref[...] + (1.0 - ADAM_B1) * gv
        vn = ADAM_B2 * v_ref[...] + (1.0 - ADAM_B2) * jnp.square(gv)
        m_hat = mn / c1
        v_hat = vn / c2
        d_ref[...] = -ADAM_LR * (m_hat / (jnp.sqrt(v_hat) + ADAM_EPS) + ADAM_WD * w_ref[...])
        nm_ref[...] = mn
        nv_ref[...] = vn

    blk = pl.BlockSpec((tr, c), lambda i: (i, 0))
    s = jax.ShapeDtypeStruct((r, c), F32)
    d, nm, nv = pl.pallas_call(body, grid=(r // tr,), in_specs=[blk] * 4, out_specs=[blk] * 3, out_shape=[s, s, s],
                               compiler_params=_cp("parallel"), name=name)(w2, g2, m2, v2)
    return d.reshape(shape), nm.reshape(shape), nv.reshape(shape)


def _gla_parts(p_ref, wg_ref, bg_ref):
    q = p_ref[:, 0:GLA_KW] * (GLA_DK ** -0.5)
    k = p_ref[:, GLA_KW:2 * GLA_KW]
    glr = p_ref[:, 2 * GLA_KW + 2 * GLA_VW:GLA_IN_PAD]
    gate = _dg(glr, wg_ref[...]) + bg_ref[...]
    log_a = (jnp.minimum(gate, 0.0) - jnp.log(1.0 + jnp.exp(-jnp.abs(gate)))) * (1.0 / 16.0)
    tril = (_iota2((CHUNK, CHUNK), 0) >= _iota2((CHUNK, CHUNK), 1)).astype(F32)
    cum = _dot_sel(tril, log_a)
    c_last = cum[CHUNK - 1:CHUNK, :]
    f = jnp.exp(c_last - cum)
    dec = jnp.exp(c_last)
    return q, k, glr, gate, f, k * f, dec


def _gla_fwd(p, wg, bg, hg, *, name):
    t = p.shape[0]
    nc = t // CHUNK

    def body(p_ref, wg_ref, bg_ref, hg_ref, og_ref, s_ref, st):
        @pl.when(pl.program_id(0) == 0)
        def _():
            st[...] = jnp.zeros_like(st)

        q, _, _, _, _, ke, dec = _gla_parts(p_ref, wg_ref, bg_ref)
        hs = range(GLA_HEADS)
        ks = [slice(h * GLA_DK, (h + 1) * GLA_DK) for h in hs]
        vs = [slice(2 * GLA_KW + h * GLA_DV, 2 * GLA_KW + (h + 1) * GLA_DV) for h in hs]
        rs = [slice(2 * GLA_KW + GLA_VW + h * GLA_DV, 2 * GLA_KW + GLA_VW + (h + 1) * GLA_DV) for h in hs]
        s_new = [st[h] * dec[:, ks[h]] + _dg(p_ref[:, vs[h]], ke[:, ks[h]], TN) for h in hs]
        o = [_dg(q[:, ks[h]], s_new[h], NT) for h in hs]
        for h in hs:
            st[h] = s_new[h]
            s_ref[0, h] = s_new[h]
            rn = lax.rsqrt(jnp.mean(o[h] * o[h], axis=-1, keepdims=True) + EPS)
            rv = p_ref[:, rs[h]]
            og_ref[:, h * GLA_DV:(h + 1) * GLA_DV] = (((o[h] * rn) * hg_ref[...]) * (rv * _sigmoid(rv))).astype(BF16)

    full = lambda a: pl.BlockSpec(a.shape, lambda n: (0,) * a.ndim)
    return pl.pallas_call(
        body, grid=(nc,),
        in_specs=[pl.BlockSpec((CHUNK, GLA_IN_PAD), lambda n: (n, 0)), full(wg), full(bg), full(hg)],
        out_specs=[pl.BlockSpec((CHUNK, GLA_VW), lambda n: (n, 0)),
                   pl.BlockSpec((1, GLA_HEADS, GLA_DV, GLA_DK), lambda n: (n, 0, 0, 0))],
        out_shape=[jax.ShapeDtypeStruct((t, GLA_VW), BF16), jax.ShapeDtypeStruct((nc, GLA_HEADS, GLA_DV, GLA_DK), F32)],
        scratch_shapes=[pltpu.VMEM((GLA_HEADS, GLA_DV, GLA_DK), F32)],
        compiler_params=_cp("arbitrary"), name=name)(p, wg, bg, hg)


def _gla_bwd(p, dog, sall, wg, bg, hg, *, name):
    t = p.shape[0]
    nc = t // CHUNK

    def body(p_ref, dog_ref, s1_ref, s0_ref, wg_ref, bg_ref, hg_ref, dp_ref, dwg_ref, dbg_ref, dhg_ref, gt):
        i = pl.program_id(0)

        @pl.when(i == 0)
        def _():
            gt[...] = jnp.zeros_like(gt)
            dwg_ref[...] = jnp.zeros_like(dwg_ref)
            dbg_ref[...] = jnp.zeros_like(dbg_ref)
            dhg_ref[...] = jnp.zeros_like(dhg_ref)

        has_prev = (i < nc - 1).astype(F32)
        q, k, glr, gate, f, ke, dec = _gla_parts(p_ref, wg_ref, bg_ref)
        hgv = hg_ref[...]
        hs = range(GLA_HEADS)
        ks = [slice(h * GLA_DK, (h + 1) * GLA_DK) for h in hs]
        vs = [slice(2 * GLA_KW + h * GLA_DV, 2 * GLA_KW + (h + 1) * GLA_DV) for h in hs]
        rs = [slice(2 * GLA_KW + GLA_VW + h * GLA_DV, 2 * GLA_KW + GLA_VW + (h + 1) * GLA_DV) for h in hs]
        s1 = [s1_ref[0, h] for h in hs]
        o = [_dg(q[:, ks[h]], s1[h], NT) for h in hs]
        dhg = jnp.zeros((1, GLA_DV), F32)
        do = []
        for h in hs:
            rv = p_ref[:, rs[h]]
            rn = lax.rsqrt(jnp.mean(o[h] * o[h], axis=-1, keepdims=True) + EPS)
            z = o[h] * rn
            sg = _sigmoid(rv)
            sl = rv * sg
            dogh = dog_ref[:, h * GLA_DV:(h + 1) * GLA_DV].astype(F32)
            dhg = dhg + jnp.sum(dogh * z * sl, axis=0, keepdims=True)
            dp_ref[:, rs[h]] = (dogh * (z * hgv) * (sg * (1.0 + rv * (1.0 - sg)))).astype(BF16)
            dz = dogh * sl * hgv
            do.append(rn * (dz - z * jnp.mean(dz * z, axis=-1, keepdims=True)))
        dhg_ref[...] += dhg
        g_tot = [gt[h] + _dg(do[h], q[:, ks[h]], TN) for h in hs]
        dq = [_dg(do[h], s1[h], NN) for h in hs]
        dke_parts = [_dg(p_ref[:, vs[h]], g_tot[h], NN) for h in hs]
        dv = [_dg(ke[:, ks[h]], g_tot[h], NT) for h in hs]
        ddec_parts = []
        for h in hs:
            dp_ref[:, ks[h]] = (dq[h] * (GLA_DK ** -0.5)).astype(BF16)
            dp_ref[:, vs[h]] = dv[h].astype(BF16)
            ddec_parts.append(jnp.sum(g_tot[h] * (s0_ref[0, h] * has_prev), axis=0, keepdims=True))
            gt[h] = g_tot[h] * dec[:, ks[h]]
        dke = jnp.concatenate(dke_parts, axis=1)
        ddec = jnp.concatenate(ddec_parts, axis=1)
        dp_ref[:, GLA_KW:2 * GLA_KW] = (dke * f).astype(BF16)
        stril = (_iota2((CHUNK, CHUNK), 0) > _iota2((CHUNK, CHUNK), 1)).astype(F32)
        dlog_a = _dot_sel(stril, dke * ke) + ddec * dec
        dgate = dlog_a * (1.0 / 16.0) * _sigmoid(-gate)
        dp_ref[:, 2 * GLA_KW + 2 * GLA_VW:GLA_IN_PAD] = _dg(dgate, wg_ref[...], NT).astype(BF16)
        dwg_ref[...] += _dg(glr, dgate, TN)
        dbg_ref[...] += jnp.sum(dgate, axis=0, keepdims=True)

    full = lambda a: pl.BlockSpec(a.shape, lambda n: (0,) * a.ndim)
    sblk = (1, GLA_HEADS, GLA_DV, GLA_DK)
    return pl.pallas_call(
        body, grid=(nc,),
        in_specs=[pl.BlockSpec((CHUNK, GLA_IN_PAD), lambda n: (nc - 1 - n, 0)),
                  pl.BlockSpec((CHUNK, GLA_VW), lambda n: (nc - 1 - n, 0)),
                  pl.BlockSpec(sblk, lambda n: (nc - 1 - n, 0, 0, 0)),
                  pl.BlockSpec(sblk, lambda n: (jnp.maximum(nc - 2 - n, 0), 0, 0, 0)),
                  full(wg), full(bg), full(hg)],
        out_specs=[pl.BlockSpec((CHUNK, GLA_IN_PAD), lambda n: (nc - 1 - n, 0)), full(wg), full(bg), full(hg)],
        out_shape=[jax.ShapeDtypeStruct((t, GLA_IN_PAD), BF16), jax.ShapeDtypeStruct(wg.shape, F32),
                   jax.ShapeDtypeStruct(bg.shape, F32), jax.ShapeDtypeStruct(hg.shape, F32)],
        scratch_shapes=[pltpu.VMEM((GLA_HEADS, GLA_DV, GLA_DK), F32)],
        compiler_params=_cp("arbitrary"), name=name)(p, dog, sall, sall, wg, bg, hg)


def _gla_layer_fwd(h, w, tag):
    p = _mm(h, w["w_in"], name=tag + "_in")
    og, sall = _gla_fwd(p, w["wg"], w["bg"], w["hg"], name=tag + "_scan")
    y = _mm(og, w["w_out"], name=tag + "_out")
    return y, (p, og, sall)


def _dw_out(act, dy, sink, name):
    if sink is None:
        return _mm(act, dy, ta=True, name=name), None
    return None, _mm(act, dy, ta=True, out_dtype=sink[0].dtype, into=(sink[0], sink[1], "rows"), name=name)


def _gla_layer_bwd(dy, h, saved, w, tag, sink=None):
    p, og, sall = saved
    dog = _mm(dy, w["w_out"], tb=True, name=tag + "_dog")
    dw_out, gf = _dw_out(og, dy, sink, tag + "_dwout")
    dp, dwg, dbg, dhg = _gla_bwd(p, dog, sall, w["wg"], w["bg"], w["hg"], name=tag + "_scanb")
    dw_in = _mm(h, dp, ta=True, name=tag + "_dwin")
    dh = _mm(dp, w["w_in"], tb=True, name=tag + "_dh")
    grads = dict(w_in=dw_in[:, :GLA_IN], w_gate_up=dwg[:GLA_RANK], b_gate=dbg[0], head_g=dhg[0], w_out=dw_out)
    return dh, grads, gf


def _rope_tables(pos, inv_freq, *, name):
    t = pos.shape[0]
    tr = _row_tile(t)
    half = MLA_ROPE // 2

    def body(p_ref, f_ref, c_ref, s1_ref, s2_ref, s1b_ref, s2b_ref):
        ang = p_ref[...].astype(F32) * f_ref[...]
        lane = _iota2((tr, LANES), 1)
        lo = (lane >= MLA_NOPE) & (lane < MLA_NOPE + half)
        hi = (lane >= MLA_NOPE + half) & (lane < MLA_QK)
        cs, sn = jnp.cos(ang), jnp.sin(ang)
        zero = jnp.zeros_like(cs)
        c_ref[...] = jnp.where(lane < MLA_NOPE, 1.0, jnp.where(lane < MLA_QK, cs, 0.0))
        s1_ref[...] = jnp.where(lo, -sn, zero)
        s2_ref[...] = jnp.where(hi, sn, zero)
        s1b_ref[...] = jnp.where(lo, sn, zero)
        s2b_ref[...] = jnp.where(hi, -sn, zero)

    row = pl.BlockSpec((tr, LANES), lambda i: (i, 0))
    s = jax.ShapeDtypeStruct((t, LANES), F32)
    return pl.pallas_call(body, grid=(t // tr,),
                          in_specs=[pl.BlockSpec((tr, 1), lambda i: (i, 0)), pl.BlockSpec((1, LANES), lambda i: (0, 0))],
                          out_specs=[row] * 5, out_shape=[s] * 5, compiler_params=_cp("parallel"), name=name)(pos, inv_freq)


def _rope(x, c, s1, s2, *, out_dtype, sum_heads=False, name):
    nh, t, _ = x.shape
    tr = _row_tile(t)
    half = MLA_ROPE // 2

    def body(x_ref, c_ref, s1_ref, s2_ref, o_ref):
        total = None
        for h in range(nh):
            xv = x_ref[h].astype(F32)
            y = xv * c_ref[...] + pltpu.roll(xv, LANES - half, 1) * s1_ref[...] + pltpu.roll(xv, half, 1) * s2_ref[...]
            if sum_heads:
                total = y if total is None else total + y
            else:
                o_ref[h] = y.astype(out_dtype)
        if sum_heads:
            o_ref[...] = total

    tab = pl.BlockSpec((tr, LANES), lambda i: (i, 0))
    xs = pl.BlockSpec((nh, tr, LANES), lambda i: (0, i, 0))
    if sum_heads:
        return pl.pallas_call(body, grid=(t // tr,), in_specs=[xs, tab, tab, tab], out_specs=tab,
                              out_shape=jax.ShapeDtypeStruct((t, LANES), F32),
                              compiler_params=_cp("parallel"), name=name)(x, c, s1, s2)
    return pl.pallas_call(body, grid=(t // tr,), in_specs=[xs, tab, tab, tab], out_specs=xs,
                          out_shape=jax.ShapeDtypeStruct(x.shape, out_dtype),
                          compiler_params=_cp("parallel"), name=name)(x, c, s1, s2)


FLASH_BLK = 512


def _diag_mask(blk):
    return (_iota2((blk, blk), 1) // CHUNK) <= (_iota2((blk, blk), 0) // CHUNK)


def _flash_fwd(q, k, v, *, name):
    nh, t, _ = q.shape
    blk = min(FLASH_BLK, t)
    scale = MLA_QK ** -0.5

    def body(q_ref, k_ref, v_ref, o_ref, lse_ref):
        i = pl.program_id(1)
        qv = q_ref[0]

        def step(j, carry, masked):
            m, l, acc = carry
            off = pl.multiple_of(j * blk, blk)
            kb = k_ref[0, pl.ds(off, blk), :]
            vb = v_ref[0, pl.ds(off, blk), :]
            s = _dg(qv, kb, NT) * scale
            if masked:
                s = jnp.where(_diag_mask(blk), s, NEG_INF)
            m_new = jnp.maximum(m, jnp.max(s, axis=-1, keepdims=True))
            p = jnp.exp(s - m_new)
            alpha = jnp.exp(m - m_new)
            return m_new, alpha * l + jnp.sum(p, axis=-1, keepdims=True), alpha * acc + _dg(p, vb, NN)

        init = (jnp.full((blk, 1), NEG_INF, F32), jnp.zeros((blk, 1), F32), jnp.zeros((blk, MLA_V), F32))
        carry = lax.fori_loop(0, i, lambda j, c: step(j, c, False), init)
        m, l, acc = step(i, carry, True)
        o_ref[0] = (acc / l).astype(BF16)
        lse_ref[0] = m + jnp.log(l)

    qs = pl.BlockSpec((1, blk, LANES), lambda h, i: (h, i, 0))
    return pl.pallas_call(
        body, grid=(nh, t // blk),
        in_specs=[qs, pl.BlockSpec((1, t, LANES), lambda h, i: (h, 0, 0)), pl.BlockSpec((1, t, MLA_V), lambda h, i: (h, 0, 0))],
        out_specs=[pl.BlockSpec((1, blk, MLA_V), lambda h, i: (h, i, 0)), pl.BlockSpec((1, blk, 1), lambda h, i: (h, i, 0))],
        out_shape=[jax.ShapeDtypeStruct((nh, t, MLA_V), BF16), jax.ShapeDtypeStruct((nh, t, 1), F32)],
        compiler_params=_cp("parallel", "parallel"), name=name)(q, k, v)


def _flash_bwd(q, k, v, do, o, lse, *, name):
    nh, t, _ = q.shape
    blk = min(FLASH_BLK, t)
    nq = t // blk
    scale = MLA_QK ** -0.5

    def body(q_ref, k_ref, v_ref, do_ref, o_ref, lse_ref, dq_ref, dk_ref, dv_ref, dl):
        j = pl.program_id(1)

        @pl.when(j == 0)
        def _():
            dq_ref[...] = jnp.zeros_like(dq_ref)
            dl[...] = jnp.sum(do_ref[0].astype(F32) * o_ref[0].astype(F32), axis=-1, keepdims=True)

        kb = k_ref[0]
        vb = v_ref[0]

        def step(i, carry, masked):
            dk, dv = carry
            rows = pl.ds(pl.multiple_of(i * blk, blk), blk)
            qb = q_ref[0, rows, :]
            dob = do_ref[0, rows, :]
            s = _dg(qb, kb, NT) * scale
            if masked:
                s = jnp.where(_diag_mask(blk), s, NEG_INF)
            p = jnp.exp(s - lse_ref[0, rows, :])
            ds = p * (_dg(dob, vb, NT) - dl[rows, :]) * scale
            dq_ref[0, rows, :] += _dg(ds, kb, NN)
            return dk + _dg(ds, qb, TN), dv + _dg(p, dob, TN)

        carry = step(j, (jnp.zeros((blk, LANES), F32), jnp.zeros((blk, MLA_V), F32)), True)
        dk, dv = lax.fori_loop(j + 1, nq, lambda i, c: step(i, c, False), carry)
        dk_ref[0] = dk
        dv_ref[0] = dv

    ks = pl.BlockSpec((1, blk, LANES), lambda h, j: (h, j, 0))
    vs = pl.BlockSpec((1, blk, MLA_V), lambda h, j: (h, j, 0))
    fl = lambda w: pl.BlockSpec((1, t, w), lambda h, j: (h, 0, 0))
    return pl.pallas_call(
        body, grid=(nh, nq),
        in_specs=[fl(LANES), ks, vs, fl(MLA_V), fl(MLA_V), fl(1)],
        out_specs=[fl(LANES), ks, vs],
        out_shape=[jax.ShapeDtypeStruct((nh, t, LANES), F32), jax.ShapeDtypeStruct((nh, t, LANES), F32),
                   jax.ShapeDtypeStruct((nh, t, MLA_V), F32)],
        scratch_shapes=[pltpu.VMEM((t, 1), F32)],
        compiler_params=_cp("parallel", "arbitrary"), name=name)(q, k, v, do, o, lse)


def _heads_first(a, width):
    t = a.shape[0]
    return a.reshape(t, MLA_HEADS, width).transpose(1, 0, 2)


def _heads_last(a):
    return a.transpose(1, 0, 2).reshape(a.shape[1], -1)


def _mla_layer_fwd(h, pos, w, tag):
    t = h.shape[0]
    zq = jnp.zeros((1, MLA_Q_RANK), F32)
    zkv = jnp.zeros((1, MLA_KV_RANK), F32)
    p = _mm(h, w["w_in"], name=tag + "_in")
    cq, ckv, krp = p[:, :MLA_Q_RANK], p[:, MLA_Q_RANK:MLA_Q_RANK + MLA_KV_RANK], p[:, MLA_Q_RANK + MLA_KV_RANK:MLA_IN]
    qn = _premod_fwd(cq, w["q_norm_g"], zq, zq, name=tag + "_qnorm")
    kvn = _premod_fwd(ckv, w["kv_norm_g"], zkv, zkv, name=tag + "_kvnorm")
    q = _mm(qn, w["w_uq"], name=tag + "_uq")
    kv = _mm(kvn, w["w_ukv"], name=tag + "_ukv")
    q_pre = jnp.pad(_heads_first(q, MLA_QK), ((0, 0), (0, 0), (0, LANES - MLA_QK)))
    kv3 = _heads_first(kv, MLA_NOPE + MLA_V)
    k_pre = jnp.concatenate([kv3[:, :, :MLA_NOPE], jnp.broadcast_to(krp[None], (MLA_HEADS, t, MLA_ROPE)),
                             jnp.zeros((MLA_HEADS, t, LANES - MLA_QK), F32)], axis=-1)
    vh = kv3[:, :, MLA_NOPE:].astype(BF16)
    half = MLA_ROPE // 2
    freq = ROPE_THETA ** (-jnp.arange(half, dtype=F32) / half)
    inv_freq = jnp.concatenate([jnp.zeros((MLA_NOPE,), F32), freq, freq, jnp.zeros((LANES - MLA_QK,), F32)])[None]
    tabs = _rope_tables(pos.reshape(t, 1), inv_freq, name=tag + "_tables")
    qr = _rope(q_pre, tabs[0], tabs[1], tabs[2], out_dtype=BF16, name=tag + "_ropeq")
    kr = _rope(k_pre, tabs[0], tabs[1], tabs[2], out_dtype=BF16, name=tag + "_ropek")
    o, lse = _flash_fwd(qr, kr, vh, name=tag + "_attn")
    of = _heads_last(o)
    y = _mm(of, w["w_out"], name=tag + "_out")
    return y, (cq, ckv, qn, kvn, qr, kr, vh, o, lse, of, tabs)


def _mla_layer_bwd(dy, h, saved, w, tag, sink=None):
    cq, ckv, qn, kvn, qr, kr, vh, o, lse, of, tabs = saved
    t = h.shape[0]
    zq = jnp.zeros((1, MLA_Q_RANK), F32)
    zkv = jnp.zeros((1, MLA_KV_RANK), F32)
    dof = _mm(dy, w["w_out"], tb=True, out_dtype=BF16, name=tag + "_dof")
    dw_out, gf = _dw_out(of, dy, sink, tag + "_dwout")
    do = _heads_first(dof, MLA_V)
    dqr, dkr, dv = _flash_bwd(qr, kr, vh, do, o, lse, name=tag + "_attn_b")
    dq_pre = _rope(dqr, tabs[0], tabs[3], tabs[4], out_dtype=F32, name=tag + "_ropeq_b")
    dk_sum = _rope(dkr, tabs[0], tabs[3], tabs[4], out_dtype=F32, sum_heads=True, name=tag + "_ropek_b")
    dq = _heads_last(dq_pre[:, :, :MLA_QK])
    dkv = _heads_last(jnp.concatenate([dkr[:, :, :MLA_NOPE], dv], axis=-1))
    dw_uq = _mm(qn, dq, ta=True, name=tag + "_dwuq")
    dqn = _mm(dq, w["w_uq"], tb=True, name=tag + "_dqn")
    dw_ukv = _mm(kvn, dkv, ta=True, name=tag + "_dwukv")
    dkvn = _mm(dkv, w["w_ukv"], tb=True, name=tag + "_dkvn")
    dcq, dqg, _, _ = _premod_bwd(dqn, cq, w["q_norm_g"], zq, jnp.zeros_like(cq), name=tag + "_qnorm_b")
    dckv, dkvg, _, _ = _premod_bwd(dkvn, ckv, w["kv_norm_g"], zkv, jnp.zeros_like(ckv), name=tag + "_kvnorm_b")
    dp = jnp.concatenate([dcq, dckv, dk_sum[:, MLA_NOPE:MLA_QK], jnp.zeros((t, MLA_IN_PAD - MLA_IN), F32)], axis=1).astype(BF16)
    dw_in = _mm(h, dp, ta=True, name=tag + "_dwin")
    dh = _mm(dp, w["w_in"], tb=True, name=tag + "_dh")
    grads = dict(w_in=dw_in[:, :MLA_IN], q_norm_g=dqg[0], w_uq=dw_uq, kv_norm_g=dkvg[0], w_ukv=dw_ukv, w_out=dw_out)
    return dh, grads, gf


CONV_HALO = 8


def _conv_tiles(t):
    return min(512, t), 512


def _gdn_conv_fwd(p, w, *, name):
    t = p.shape[0]
    tr, tc = _conv_tiles(t)
    hb = tr // CONV_HALO

    def body(x_ref, halo_ref, w_ref, o_ref, buf):
        i = pl.program_id(0)
        buf[0:CONV_HALO, :] = halo_ref[...] * (i > 0).astype(F32)
        buf[CONV_HALO:CONV_HALO + tr, :] = x_ref[...]
        base = CONV_HALO - (GDN_CONV - 1)
        acc = buf[pl.ds(base, tr), :] * w_ref[0:1, :]
        for j in range(1, GDN_CONV):
            acc = acc + buf[pl.ds(base + j, tr), :] * w_ref[j:j + 1, :]
        o_ref[...] = acc * _sigmoid(acc)

    return pl.pallas_call(
        body, grid=(t // tr, GDN_CONV_W // tc),
        in_specs=[pl.BlockSpec((tr, tc), lambda i, j: (i, j)),
                  pl.BlockSpec((CONV_HALO, tc), lambda i, j: (jnp.maximum(i * hb - 1, 0), j)),
                  pl.BlockSpec((GDN_CONV, tc), lambda i, j: (0, j))],
        out_specs=pl.BlockSpec((tr, tc), lambda i, j: (i, j)),
        out_shape=jax.ShapeDtypeStruct((t, GDN_CONV_W), F32),
        scratch_shapes=[pltpu.VMEM((tr + CONV_HALO, tc), F32)],
        compiler_params=_cp("parallel", "parallel"), name=name)(p, p, w)


def _gdn_conv_bwd(d, p, w, col0, *, name):
    t = p.shape[0]
    tr, tc = _conv_tiles(t)
    hb = tr // CONV_HALO
    nr = t // tr
    ext = tr + CONV_HALO

    def body(x_ref, xp_ref, xn_ref, d_ref, dn_ref, w_ref, dx_ref, dw_ref, bufx, bufd):
        i = pl.program_id(1)

        @pl.when(i == 0)
        def _():
            dw_ref[...] = jnp.zeros_like(dw_ref)

        last = (i < nr - 1).astype(F32)
        bufx[0:CONV_HALO, :] = xp_ref[...] * (i > 0).astype(F32)
        bufx[CONV_HALO:CONV_HALO + tr, :] = x_ref[...]
        bufx[CONV_HALO + tr:, :] = xn_ref[...] * last
        base = CONV_HALO - (GDN_CONV - 1)
        acc = bufx[pl.ds(base, ext), :] * w_ref[0:1, :]
        for j in range(1, GDN_CONV):
            acc = acc + bufx[pl.ds(base + j, ext), :] * w_ref[j:j + 1, :]
        sg = _sigmoid(acc)
        dsilu = sg * (1.0 + acc * (1.0 - sg))
        bufd[0:tr, :] = d_ref[...] * dsilu[0:tr, :]
        bufd[tr:, :] = dn_ref[...] * last * dsilu[tr:, :]
        dx = bufd[pl.ds(GDN_CONV - 1, tr), :] * w_ref[0:1, :]
        for j in range(1, GDN_CONV):
            dx = dx + bufd[pl.ds(GDN_CONV - 1 - j, tr), :] * w_ref[j:j + 1, :]
        dx_ref[...] = dx.astype(BF16)
        dc = bufd[0:tr, :]
        for j in range(GDN_CONV):
            dw_ref[j:j + 1, :] += jnp.sum(dc * bufx[pl.ds(base + j, tr), :], axis=0, keepdims=True)

    cb = col0 // tc
    width = d.shape[1]
    main = lambda off: pl.BlockSpec((tr, tc), lambda j, i: (i, j + off))
    prev = pl.BlockSpec((CONV_HALO, tc), lambda j, i: (jnp.maximum(i * hb - 1, 0), j + cb))
    nxt = lambda off: pl.BlockSpec((CONV_HALO, tc), lambda j, i: (jnp.minimum((i + 1) * hb, t // CONV_HALO - 1), j + off))
    wsp = lambda off: pl.BlockSpec((GDN_CONV, tc), lambda j, i: (0, j + off))
    return pl.pallas_call(
        body, grid=(width // tc, nr),
        in_specs=[main(cb), prev, nxt(cb), main(0), nxt(0), wsp(cb)], out_specs=[main(0), wsp(0)],
        out_shape=[jax.ShapeDtypeStruct((t, width), BF16), jax.ShapeDtypeStruct((GDN_CONV, width), F32)],
        scratch_shapes=[pltpu.VMEM((tr + 2 * CONV_HALO, tc), F32), pltpu.VMEM((ext, tc), F32)],
        compiler_params=_cp("parallel", "arbitrary"), name=name)(p, p, p, d, d, w)


def _dot_sel(sel, b, dims=NN, sel_first=True):
    s = sel.astype(BF16)
    b1 = b.astype(BF16)
    r1 = b - b1.astype(F32)
    b2 = r1.astype(BF16)
    b3 = (r1 - b2.astype(F32)).astype(BF16)
    if sel_first:
        d = lambda v: lax.dot_general(s, v, (dims, ((), ())), preferred_element_type=F32)
    else:
        d = lambda v: lax.dot_general(v, s, (dims, ((), ())), preferred_element_type=F32)
    return d(b1) + (d(b2) + d(b3))


def _gdn_chunks(qraws, kraws, vs, braws, araws, alogs, dtbs):
    nv = len(vs)
    row = _iota2((CHUNK, CHUNK), 0)
    col = _iota2((CHUNK, CHUNK), 1)
    strict = row > col
    triu = (row <= col).astype(F32)
    tril = (row >= col).astype(F32)
    ones = jnp.ones((CHUNK, CHUNK), F32)
    keys = []
    for qraw, kraw in zip(qraws, kraws):
        rq = lax.rsqrt(jnp.sum(qraw * qraw, axis=-1, keepdims=True) + EPS)
        rk = lax.rsqrt(jnp.sum(kraw * kraw, axis=-1, keepdims=True) + EPS)
        qn = qraw * rq
        keys.append(dict(rq=rq, rk=rk, qn=qn, qh=qn * (GDN_DK ** -0.5), kh=kraw * rk))
    kks = [_dg(kd["kh"], kd["kh"], NT) for kd in keys]
    cs = []
    for h in range(nv):
        c = dict(keys[h // 2])
        c.update(v=vs[h], kk=kks[h // 2], strict=strict, triu=triu)
        c["beta"] = _sigmoid(braws[h])
        c["ea"] = jnp.exp(alogs[h])
        c["xs"] = araws[h] + dtbs[h]
        c["g"] = -c["ea"] * _softplus(c["xs"])
        cs.append(c)
    gbs = [jnp.broadcast_to(c["g"], (CHUNK, LANES)) for c in cs]
    cums = [_dot_sel(tril, gb) for gb in gbs]
    cum_js = [_dot_sel(ones, gb[:, :CHUNK] * triu) for gb in gbs]
    for c, cum, cum_j in zip(cs, cums, cum_js):
        diff = jnp.where(strict, cum[:, :CHUNK] - cum_j, 0.0)
        c["dm"] = jnp.where(strict, jnp.exp(diff), 0.0)
        c["a"] = (c["beta"] * c["dm"]) * c["kk"]
        c_last = cum[CHUNK - 1:CHUNK, :]
        c["e"] = jnp.exp(cum)
        c["f"] = jnp.exp(c_last - cum)
        c["dec"] = jnp.exp(c_last)
        c["rv"] = c["beta"] * c["v"]
        c["rk_rhs"] = (c["beta"] * c["e"]) * c["kh"]
        c["ke"] = c["kh"] * c["f"]
    return cs


def _unit_lower_inverses(mats):
    eye = (_iota2((CHUNK, CHUNK), 0) == _iota2((CHUNK, CHUNK), 1)).astype(F32)
    ts = [eye - a for a in mats]
    pws = list(mats)
    for _ in range(5):
        pws = [_dot3(pw, pw) for pw in pws]
        ts = [t + _dot3(t, pw) for t, pw in zip(ts, pws)]
    return ts


GDN_HB = 16


def _gdn_specs(chunk_of):
    hb = GDN_HB
    kw = hb // 2 * GDN_DK
    vw = hb * GDN_DV
    qs = pl.BlockSpec((CHUNK, kw), lambda g, n: (chunk_of(n), g))
    ks = pl.BlockSpec((CHUNK, kw), lambda g, n: (chunk_of(n), GDN_QKW // kw + g))
    vs = pl.BlockSpec((CHUNK, vw), lambda g, n: (chunk_of(n), 2 * GDN_QKW // vw + g))
    zs = pl.BlockSpec((CHUNK, vw), lambda g, n: (chunk_of(n), GDN_CONV_W // vw + g))
    col = pl.BlockSpec((hb, CHUNK, 1), lambda g, n: (g, chunk_of(n), 0))
    one = pl.BlockSpec((hb, 1, 1), lambda g, n: (g, 0, 0))
    ng = pl.BlockSpec((1, GDN_DV), lambda g, n: (0, 0))
    hd = pl.BlockSpec((CHUNK, vw), lambda g, n: (chunk_of(n), g))
    return qs, ks, vs, zs, col, one, ng, hd


def _gdn_fwd(qkv, p, braw, araw, alog, dtb, ng, *, name):
    t = qkv.shape[0]
    nc = t // CHUNK
    nh = GDN_V_HEADS

    def body(q_ref, k_ref, v_ref, z_ref, braw_ref, araw_ref, alog_ref, dtb_ref, ng_ref, og_ref, s_ref, t_ref, st):
        @pl.when(pl.program_id(1) == 0)
        def _():
            st[...] = jnp.zeros_like(st)

        hs = range(GDN_HB)
        kqs = [slice(j * GDN_DK, (j + 1) * GDN_DK) for j in range(GDN_HB // 2)]
        vsl = [slice(h * GDN_DV, (h + 1) * GDN_DV) for h in hs]
        cs = _gdn_chunks([q_ref[:, s] for s in kqs], [k_ref[:, s] for s in kqs], [v_ref[:, s] for s in vsl],
                         [braw_ref[h] for h in hs], [araw_ref[h] for h in hs], [alog_ref[h] for h in hs], [dtb_ref[h] for h in hs])
        tms = _unit_lower_inverses([c["a"] for c in cs])
        s0 = [st[h] for h in hs]
        wv = [_dot3(tms[h], cs[h]["rv"]) for h in hs]
        wk = [_dot3(tms[h], cs[h]["rk_rhs"]) for h in hs]
        u = [wv[h] - _dg(wk[h], s0[h], NN) for h in hs]
        s1 = [cs[h]["dec"] * s0[h] + _dg(cs[h]["ke"], u[h], TN) for h in hs]
        o = [_dg(cs[h]["qh"], s1[h], NN) for h in hs]
        for h in hs:
            t_ref[h, 0] = tms[h]
            st[h] = s1[h]
            s_ref[h, 0] = s1[h]
            rn = lax.rsqrt(jnp.mean(o[h] * o[h], axis=-1, keepdims=True) + EPS)
            zv = z_ref[:, vsl[h]]
            og_ref[:, vsl[h]] = (((o[h] * rn) * ng_ref[...]) * (zv * _sigmoid(zv))).astype(BF16)

    qs, ks, vs, zs, col, one, ngs, hd = _gdn_specs(lambda n: n)
    return pl.pallas_call(
        body, grid=(nh // GDN_HB, nc),
        in_specs=[qs, ks, vs, zs, col, col, one, one, ngs],
        out_specs=[hd,
                   pl.BlockSpec((GDN_HB, 1, GDN_DK, GDN_DV), lambda g, n: (g, n, 0, 0)),
                   pl.BlockSpec((GDN_HB, 1, CHUNK, CHUNK), lambda g, n: (g, n, 0, 0))],
        out_shape=[jax.ShapeDtypeStruct((t, GDN_VW), BF16), jax.ShapeDtypeStruct((nh, nc, GDN_DK, GDN_DV), F32),
                   jax.ShapeDtypeStruct((nh, nc, CHUNK, CHUNK), F32)],
        scratch_shapes=[pltpu.VMEM((GDN_HB, GDN_DK, GDN_DV), F32)],
        compiler_params=_cp("parallel", "arbitrary"), name=name)(qkv, qkv, qkv, p, braw, araw, alog, dtb, ng)


def _gdn_bwd(qkv, p, braw, araw, alog, dtb, ng, dog, sall, tall, *, name):
    t = qkv.shape[0]
    nc = t // CHUNK
    nh = GDN_V_HEADS

    def body(q_ref, k_ref, v_ref, z_ref, braw_ref, araw_ref, alog_ref, dtb_ref, ng_ref, dog_ref, s1_ref, s0_ref, t_ref,
             dq_ref, dk_ref, dv_ref, dz_ref, dbraw_ref, daraw_ref, dalog_ref, ddtb_ref, dng_ref, gc):
        grp = pl.program_id(0)
        i = pl.program_id(1)

        @pl.when(i == 0)
        def _():
            gc[...] = jnp.zeros_like(gc)
            dalog_ref[...] = jnp.zeros_like(dalog_ref)
            ddtb_ref[...] = jnp.zeros_like(ddtb_ref)

        @pl.when((i == 0) & (grp == 0))
        def _():
            dng_ref[...] = jnp.zeros_like(dng_ref)

        has_prev = (i < nc - 1).astype(F32)
        ngv = ng_ref[...]
        ones = jnp.ones((CHUNK, LANES), F32)
        hs = range(GDN_HB)
        kqs = [slice(j * GDN_DK, (j + 1) * GDN_DK) for j in range(GDN_HB // 2)]
        vsl = [slice(h * GDN_DV, (h + 1) * GDN_DV) for h in hs]
        cs = _gdn_chunks([q_ref[:, s] for s in kqs], [k_ref[:, s] for s in kqs], [v_ref[:, s] for s in vsl],
                         [braw_ref[h] for h in hs], [araw_ref[h] for h in hs], [alog_ref[h] for h in hs], [dtb_ref[h] for h in hs])
        tms = [t_ref[h, 0] for h in hs]
        s1 = [s1_ref[h, 0] for h in hs]
        s0 = [s0_ref[h, 0] * has_prev for h in hs]
        wv = [_dot3(tms[h], cs[h]["rv"]) for h in hs]
        wk = [_dot3(tms[h], cs[h]["rk_rhs"]) for h in hs]
        u = [wv[h] - _dg(wk[h], s0[h], NN) for h in hs]
        o = [_dg(cs[h]["qh"], s1[h], NN) for h in hs]
        dng = jnp.zeros((1, GDN_DV), F32)
        do = []
        for h in hs:
            zv = z_ref[:, vsl[h]]
            dogv = dog_ref[:, vsl[h]]
            rn = lax.rsqrt(jnp.mean(o[h] * o[h], axis=-1, keepdims=True) + EPS)
            zo = o[h] * rn
            sg = _sigmoid(zv)
            sl = zv * sg
            dng = dng + jnp.sum(dogv * zo * sl, axis=0, keepdims=True)
            dz_ref[:, vsl[h]] = (dogv * (zo * ngv) * (sg * (1.0 + zv * (1.0 - sg)))).astype(BF16)
            dzo = dogv * sl * ngv
            do.append(rn * (dzo - zo * jnp.mean(dzo * zo, axis=-1, keepdims=True)))
        dng_ref[...] += dng
        g_tot = [gc[h] + _dg(cs[h]["qh"], do[h], TN) for h in hs]
        dqh = [_dg(do[h], s1[h], NT) for h in hs]
        dke = [_dg(u[h], g_tot[h], NT) for h in hs]
        du = [_dg(cs[h]["ke"], g_tot[h], NN) for h in hs]
        gnew = [cs[h]["dec"] * g_tot[h] - _dg(wk[h], du[h], TN) for h in hs]
        dwk = [-_dg(du[h], s0[h], NT) for h in hs]
        drv = [_dot3(tms[h], du[h], TN) for h in hs]
        drk = [_dot3(tms[h], dwk[h], TN) for h in hs]
        da = [jnp.where(cs[h]["strict"], -(_dot3(drv[h], wv[h], NT) + _dot3(drk[h], wk[h], NT)), 0.0) for h in hs]
        mx = [da[h] * cs[h]["dm"] * cs[h]["kk"] for h in hs]
        aa = [mx[h] * cs[h]["beta"] for h in hs]
        colsum = [_dot_sel(ones, aa[h], TN, sel_first=False)[:, 0:1] for h in hs]
        bm = [(da[h] * cs[h]["beta"]) * cs[h]["dm"] for h in hs]
        dkh = [_dg(bm[h], cs[h]["kh"], NN) + _dg(bm[h], cs[h]["kh"], TN) for h in hs]
        dcum, dcl, dbeta = [], [], []
        for h in hs:
            c = cs[h]
            beta, kh, e, f, dec, ke = c["beta"], c["kh"], c["e"], c["f"], c["dec"], c["ke"]
            gc[h] = gnew[h]
            ddec = jnp.sum(jnp.sum(g_tot[h] * s0[h], axis=1, keepdims=True), axis=0, keepdims=True)
            dv_ref[:, vsl[h]] = beta * drv[h]
            db = jnp.sum(mx[h], axis=1, keepdims=True) + jnp.sum(drv[h] * c["v"], axis=1, keepdims=True)
            dbeta.append(db + jnp.sum(drk[h] * (e * kh), axis=1, keepdims=True))
            dkh[h] = dkh[h] + (beta * e) * drk[h] + f * dke[h]
            ef = jnp.sum(dke[h] * ke, axis=1, keepdims=True)
            dcum.append(jnp.sum(aa[h], axis=1, keepdims=True) - colsum[h] + jnp.sum(drk[h] * c["rk_rhs"], axis=1, keepdims=True) - ef)
            dcl.append(jnp.sum(ef, axis=0, keepdims=True) + ddec * dec[:, 0:1])
        dg = [_dot_sel(cs[h]["triu"], jnp.broadcast_to(dcum[h], (CHUNK, LANES)))[:, 0:1] + dcl[h] for h in hs]
        for h in hs:
            c = cs[h]
            beta, kh = c["beta"], c["kh"]
            daraw = dg[h] * (-c["ea"]) * _sigmoid(c["xs"])
            daraw_ref[h] = daraw
            dbraw_ref[h] = dbeta[h] * beta * (1.0 - beta)
            dalog_ref[h] += jnp.sum(dg[h] * c["g"], axis=0, keepdims=True)
            ddtb_ref[h] += jnp.sum(daraw, axis=0, keepdims=True)
        for j, sl in enumerate(kqs):
            c = cs[2 * j]
            dn = (dqh[2 * j] + dqh[2 * j + 1]) * (GDN_DK ** -0.5)
            dks = dkh[2 * j] + dkh[2 * j + 1]
            dq_ref[:, sl] = c["rq"] * (dn - c["qn"] * jnp.sum(dn * c["qn"], axis=-1, keepdims=True))
            dk_ref[:, sl] = c["rk"] * (dks - c["kh"] * jnp.sum(dks * c["kh"], axis=-1, keepdims=True))

    rev = lambda n: nc - 1 - n
    qs, ks, vs, zs, col, one, ngs, hd = _gdn_specs(rev)
    s1s = pl.BlockSpec((GDN_HB, 1, GDN_DK, GDN_DV), lambda g, n: (g, rev(n), 0, 0))
    s0s = pl.BlockSpec((GDN_HB, 1, GDN_DK, GDN_DV), lambda g, n: (g, jnp.maximum(rev(n) - 1, 0), 0, 0))
    ts = pl.BlockSpec((GDN_HB, 1, CHUNK, CHUNK), lambda g, n: (g, rev(n), 0, 0))
    big = jax.ShapeDtypeStruct((t, GDN_VW), F32)
    keyw = jax.ShapeDtypeStruct((t, GDN_QKW), F32)
    cols = jax.ShapeDtypeStruct((nh, t, 1), F32)
    ones_s = jax.ShapeDtypeStruct((nh, 1, 1), F32)
    return pl.pallas_call(
        body, grid=(nh // GDN_HB, nc),
        in_specs=[qs, ks, vs, zs, col, col, one, one, ngs, hd, s1s, s0s, ts],
        out_specs=[qs, qs, hd, hd, col, col, one, one, ngs],
        out_shape=[keyw, keyw, big, jax.ShapeDtypeStruct((t, GDN_VW), BF16), cols, cols, ones_s, ones_s,
                   jax.ShapeDtypeStruct((1, GDN_DV), F32)],
        scratch_shapes=[pltpu.VMEM((GDN_HB, GDN_DK, GDN_DV), F32)],
        compiler_params=_cp("arbitrary", "arbitrary"), name=name)(qkv, qkv, qkv, p, braw, araw, alog, dtb, ng, dog, sall, sall, tall)


def _gdn_layer_fwd(h, w, tag):
    p = _mm(h, w["w_in"], name=tag + "_in")
    qkv = _gdn_conv_fwd(p, w["conv_w"], name=tag + "_conv")
    braw = p[:, GDN_CONV_W + GDN_VW:GDN_CONV_W + GDN_VW + GDN_V_HEADS].T[:, :, None]
    araw = p[:, GDN_CONV_W + GDN_VW + GDN_V_HEADS:GDN_IN].T[:, :, None]
    og, sall, tall = _gdn_fwd(qkv, p, braw, araw, w["a_log"], w["dt_bias"], w["norm_g"], name=tag + "_scan")
    y = _mm(og, w["w_out"], name=tag + "_out")
    return y, (p, qkv, braw, araw, og, sall, tall)


def _gdn_layer_bwd(dy, h, saved, w, tag, sink=None):
    p, qkv, braw, araw, og, sall, tall = saved
    t = h.shape[0]
    dog = _mm(dy, w["w_out"], tb=True, name=tag + "_dog")
    dw_out, gf = _dw_out(og, dy, sink, tag + "_dwout")
    dq, dk, dv, dz, dbraw, daraw, dalog, ddtb, dng = _gdn_bwd(
        qkv, p, braw, araw, w["a_log"], w["dt_bias"], w["norm_g"], dog, sall, tall, name=tag + "_scanb")
    dpre_q, dcw_q = _gdn_conv_bwd(dq, p, w["conv_w"], 0, name=tag + "_convb_q")
    dpre_k, dcw_k = _gdn_conv_bwd(dk, p, w["conv_w"], GDN_QKW, name=tag + "_convb_k")
    dpre_v, dcw_v = _gdn_conv_bwd(dv, p, w["conv_w"], 2 * GDN_QKW, name=tag + "_convb_v")
    dconv_w = jnp.concatenate([dcw_q, dcw_k, dcw_v], axis=1)
    dp = jnp.concatenate([dpre_q, dpre_k, dpre_v, dz, dbraw[:, :, 0].T.astype(BF16), daraw[:, :, 0].T.astype(BF16),
                          jnp.zeros((t, GDN_IN_PAD - GDN_IN), BF16)], axis=1)
    dw_in = _mm(h, dp, ta=True, name=tag + "_dwin")
    dh = _mm(dp, w["w_in"], tb=True, name=tag + "_dh")
    grads = dict(w_in=dw_in[:, :GDN_IN], conv_w=dconv_w, a_log=dalog[:, 0, 0], dt_bias=ddtb[:, 0, 0], norm_g=dng[0], w_out=dw_out)
    return dh, grads, gf


MESH_ID = pl.DeviceIdType.MESH
FLAT_W = 1024
FLAT_ROWS = 13056
FLAT_TILE = 384


def _exchange(name, ins, out_shapes, plan, n_remote, n_local):
    def body(*refs):
        in_refs = refs[:len(ins)]
        out_refs = refs[len(ins):len(ins) + len(out_shapes)]
        ssem, rsem, lsem = refs[len(ins) + len(out_shapes):]
        x, y, c = lax.axis_index("x"), lax.axis_index("y"), lax.axis_index("c")
        stages, local_copies = plan(x, y, c, in_refs, out_refs)
        assert sum(len(s) for s in stages) == n_remote and len(local_copies) == n_local
        locs = [pltpu.make_async_copy(s, d, lsem.at[i]) for i, (s, d) in enumerate(local_copies)]
        for cp in locs:
            cp.start()
        sent = []
        k = 0
        for stage in stages:
            arrivals = []
            for src, dst, peer, landing in stage:
                cp = pltpu.make_async_remote_copy(src_ref=src, dst_ref=dst, send_sem=ssem.at[k], recv_sem=rsem.at[k],
                                                  device_id=peer, device_id_type=MESH_ID)
                cp.start()
                sent.append(cp)
                arrivals.append(pltpu.make_async_remote_copy(src_ref=src, dst_ref=landing, send_sem=ssem.at[k],
                                                             recv_sem=rsem.at[k], device_id=peer, device_id_type=MESH_ID))
                k += 1
            for cp in arrivals:
                cp.wait_recv()
        for cp in sent:
            cp.wait_send()
        for cp in locs:
            cp.wait()

    hbm = pl.BlockSpec(memory_space=pl.ANY)
    return pl.pallas_call(
        body, in_specs=[hbm] * len(ins), out_specs=[hbm] * len(out_shapes), out_shape=out_shapes,
        scratch_shapes=[pltpu.SemaphoreType.DMA((n_remote,)), pltpu.SemaphoreType.DMA((n_remote,)),
                        pltpu.SemaphoreType.DMA((max(n_local, 1),))],
        name=name)(*ins)


def _other_chips(x, y):
    return [(1 - x, y), (x, 1 - y), (1 - x, 1 - y)]


def _all8_gather(a, *, name):
    def plan(x, y, c, ins, outs):
        (src,), (dst,) = ins, outs
        me = 4 * x + 2 * y + c
        stage = []
        for fx, fy, fc in [(0, 0, 1), (0, 1, 0), (0, 1, 1), (1, 0, 0), (1, 0, 1), (1, 1, 0), (1, 1, 1)]:
            px, py, pc = (1 - x if fx else x), (1 - y if fy else y), (1 - c if fc else c)
            stage.append((src, dst.at[me], (px, py, pc), dst.at[4 * px + 2 * py + pc]))
        return [stage], [(src, dst.at[me])]

    return _exchange(name, [a], [jax.ShapeDtypeStruct((8,) + a.shape, a.dtype)], plan, 7, 1)[0]


def _chip_gather(flat, *, name):
    rows = flat.shape[0]
    half = rows // 2

    def plan(x, y, c, ins, outs):
        (src,), (dst,) = ins, outs
        me = 2 * x + y
        mine = pl.ds(c * half, half)
        theirs = pl.ds((1 - c) * half, half)
        ici = [(src.at[mine], dst.at[me, mine], (px, py, c), dst.at[2 * px + py, mine]) for px, py in _other_chips(x, y)]
        d2d = [(dst.at[2 * px + py, mine], dst.at[2 * px + py, mine], (x, y, 1 - c), dst.at[2 * px + py, theirs])
               for px, py in _other_chips(x, y)]
        return [ici, d2d], []

    return _exchange(name, [flat], [jax.ShapeDtypeStruct((4,) + flat.shape, flat.dtype)], plan, 6, 0)[0]


def _add_sibling(gf, buf_a, core, *, name):
    _, rows, w = gf.shape
    half = rows // 2
    nb = half // FLAT_TILE

    def body(c_ref, g_ref, a_ref, o_ref):
        o_ref[...] = (g_ref[...].astype(F32) + a_ref[...].astype(F32)).astype(BF16)

    blk = (1, FLAT_TILE, w)
    return pl.pallas_call(
        body,
        grid_spec=pltpu.PrefetchScalarGridSpec(
            num_scalar_prefetch=1, grid=(4, nb),
            in_specs=[pl.BlockSpec(blk, lambda s, i, c_ref: (s, c_ref[0] * nb + i, 0)), pl.BlockSpec(blk, lambda s, i, c_ref: (s, i, 0))],
            out_specs=pl.BlockSpec(blk, lambda s, i, c_ref: (s, i, 0))),
        out_shape=jax.ShapeDtypeStruct((4, half, w), BF16), compiler_params=_cp("parallel", "parallel"), name=name)(core, gf, buf_a)


def _sum_chips(hsum, buf_b, chip, *, name):
    _, half, w = hsum.shape
    nb = half // FLAT_TILE

    def body(c_ref, h_ref, b0_ref, b1_ref, b2_ref, b3_ref, o_ref):
        me = c_ref[0]
        own = h_ref[0].astype(F32)
        acc = None
        for j, b_ref in enumerate((b0_ref, b1_ref, b2_ref, b3_ref)):
            term = jnp.where(me == j, own, b_ref[0].astype(F32))
            acc = term if acc is None else acc + term
        o_ref[...] = acc

    blk = (1, FLAT_TILE, w)

    def other(j):
        return pl.BlockSpec(blk, lambda i, c_ref: (jnp.where(c_ref[0] == j, (j + 1) % 4, j), i, 0))

    return pl.pallas_call(
        body,
        grid_spec=pltpu.PrefetchScalarGridSpec(
            num_scalar_prefetch=1, grid=(nb,),
            in_specs=[pl.BlockSpec(blk, lambda i, c_ref: (c_ref[0], i, 0))] + [other(j) for j in range(4)],
            out_specs=pl.BlockSpec((FLAT_TILE, w), lambda i, c_ref: (i, 0))),
        out_shape=jax.ShapeDtypeStruct((half, w), F32), compiler_params=_cp("parallel"), name=name)(chip, hsum, buf_b, buf_b, buf_b, buf_b)


def _sum_slots(buf, *, name):
    n, rows, w = buf.shape
    tr = _pick(rows, (FLAT_TILE, 8))

    def body(b_ref, o_ref):
        acc = b_ref[0]
        for s in range(1, n):
            acc = acc + b_ref[s]
        o_ref[...] = acc

    return pl.pallas_call(body, grid=(rows // tr,), in_specs=[pl.BlockSpec((n, tr, w), lambda i: (0, i, 0))],
                          out_specs=pl.BlockSpec((tr, w), lambda i: (i, 0)), out_shape=jax.ShapeDtypeStruct((rows, w), F32),
                          compiler_params=_cp("parallel"), name=name)(buf)


def _reduce_scatter(gf, core, chip, *, tag):
    _, rows, w = gf.shape
    half = rows // 2

    def plan_a(x, y, c, ins, outs):
        (src,), (dst,) = ins, outs
        return [[(src.at[:, pl.ds((1 - c) * half, half)], dst, (x, y, 1 - c), dst)]], []

    buf_a = _exchange(tag + "_sibling", [gf], [jax.ShapeDtypeStruct((4, half, w), gf.dtype)], plan_a, 1, 0)[0]
    hsum = _add_sibling(gf, buf_a, core, name=tag + "_add_sibling")

    def plan_b(x, y, c, ins, outs):
        (src,), (dst,) = ins, outs
        me = 2 * x + y
        stage = [(src.at[2 * px + py], dst.at[me], (px, py, c), dst.at[2 * px + py]) for px, py in _other_chips(x, y)]
        return [stage], []

    buf_b = _exchange(tag + "_chips", [hsum], [jax.ShapeDtypeStruct((4, half, w), BF16)], plan_b, 3, 0)[0]
    mine = _sum_chips(hsum, buf_b, chip, name=tag + "_sum_chips")

    def plan_c(x, y, c, ins, outs):
        (src,), (dst,) = ins, outs
        return [[(src, dst, (x, y, 1 - c), dst)]], []

    theirs = _exchange(tag + "_halves", [mine], [jax.ShapeDtypeStruct((half, w), F32)], plan_c, 1, 0)[0]
    first = core[0] == 0
    return jnp.concatenate([jnp.where(first, mine, theirs), jnp.where(first, theirs, mine)], axis=0)


WEIGHTS = ["ada_w", "ada_b", "norm_pre_g", "norm_post_g", "gla_w_in", "gla_w_gate_up", "gla_b_gate", "gla_head_g",
           "gla_w_out", "mla_w_in", "mla_q_norm_g", "mla_w_uq", "mla_kv_norm_g", "mla_w_ukv", "mla_w_out", "gdn_w_in",
           "gdn_conv_w", "gdn_a_log", "gdn_dt_bias", "gdn_norm_g", "gdn_w_out", "mlp_w_up", "mlp_w_down"]
PACK_BF16 = [("mlp_w_up", 2), ("mlp_w_down", 1), ("gdn_w_out", 1), ("gla_w_out", 1), ("mla_w_out", 1),
             ("gla_w_in", 2), ("mla_w_in", 1), ("mla_w_uq", 2), ("mla_w_ukv", 2), ("gdn_w_in", 2)]
N_DIRECT = 5
PACK_F32 = [("norm_pre_g", 2), ("norm_post_g", 2), ("gla_w_gate_up", 2), ("gla_b_gate", 1), ("gla_head_g", 1), ("gdn_conv_w", 2)]
REPLICATED_SMALL = ["mla_q_norm_g", "mla_kv_norm_g", "gdn_a_log", "gdn_dt_bias", "gdn_norm_g"]
MIXERS = ["gla", "mla", "gdn"]


def _silu_rows(a, *, name):
    def body(a_ref, o_ref):
        v = a_ref[...]
        o_ref[...] = v * _sigmoid(v)

    return pl.pallas_call(body, out_shape=jax.ShapeDtypeStruct(a.shape, F32), name=name)(a)


SMALL_ROWS = 16


def _piece_rows(size, mult):
    assert size % FLAT_W == 0
    return -(-(size // FLAT_W) // mult) * mult


def _to_rows(a, lead, mult):
    n = math.prod(a.shape[len(lead):])
    r = a.reshape(lead + (n // FLAT_W, FLAT_W))
    extra = _piece_rows(n, mult) - n // FLAT_W
    return jnp.pad(r, [(0, 0)] * len(lead) + [(0, extra), (0, 0)]) if extra else r


def _small_to_rows(parts, lead):
    flat = jnp.concatenate([p.reshape(lead + (-1,)) for p in parts], axis=-1)
    pad = SMALL_ROWS * FLAT_W - flat.shape[-1]
    return jnp.pad(flat, [(0, 0)] * len(lead) + [(0, pad)]).reshape(lead + (SMALL_ROWS, FLAT_W))


def _small_from_rows(rows, shards, lead):
    flat = rows.reshape(lead + (-1,))
    out, off = {}, 0
    for n, _ in PACK_F32:
        out[n] = flat[..., off:off + shards[n].size].reshape(lead + shards[n].shape)
        off += shards[n].size
    return out


def _pack_weights(shards):
    parts = [_to_rows(shards[n].astype(BF16), (), 16) for n, _ in PACK_BF16]
    small = _small_to_rows([shards[n] for n, _ in PACK_F32], ())
    parts.append(lax.bitcast_convert_type(small, BF16).reshape(2 * SMALL_ROWS, FLAT_W))
    flat = jnp.concatenate(parts, axis=0)
    return jnp.pad(flat, ((0, FLAT_ROWS - flat.shape[0]), (0, 0)))


def _unpack_weights(gathered, shards):
    full, off = {}, 0
    for n, ax in PACK_BF16:
        size = shards[n].size
        seg = gathered[:, off:off + size // FLAT_W].reshape((4,) + shards[n].shape)
        full[n] = jnp.concatenate([seg[j] for j in range(4)], axis=ax)
        off += _piece_rows(size, 16)
    small = lax.bitcast_convert_type(gathered[:, off:off + 2 * SMALL_ROWS].reshape(4, SMALL_ROWS, FLAT_W, 2), F32)
    for (n, ax), seg in zip(PACK_F32, _small_from_rows(small, shards, (4,)).values()):
        full[n] = jnp.concatenate([seg[j] for j in range(4)], axis=ax)
    return full


def _grad_layout(shards):
    layout, off = {}, 0
    for n, _ in PACK_BF16:
        layout[n] = (off, shards[n].size // shards[n].shape[0] // FLAT_W)
        off += _piece_rows(shards[n].size, 16)
    layout["small"] = (off, SMALL_ROWS)
    return layout


def _pack_grads(gf, grads, layout):
    by_chip = lambda g, ax: jnp.stack(jnp.split(g.astype(gf.dtype), 4, axis=ax - 1))
    parts = []
    for n, ax in PACK_BF16[N_DIRECT:]:
        rows = sum(g.size for g in grads[n]) // (4 * FLAT_W)
        parts += [by_chip(g, ax).reshape(4, -1, FLAT_W) for g in grads[n]]
        if _piece_rows(rows * FLAT_W, 16) > rows:
            parts.append(jnp.zeros((4, _piece_rows(rows * FLAT_W, 16) - rows, FLAT_W), gf.dtype))
    parts.append(_small_to_rows([jnp.stack([by_chip(g, ax) for g in grads[n]], axis=1) for n, ax in PACK_F32], (4,)))
    first = layout[PACK_BF16[N_DIRECT][0]][0]
    rest = jnp.concatenate(parts, axis=1)
    assert first + rest.shape[1] == layout["small"][0] + SMALL_ROWS
    return lax.dynamic_update_slice(gf, rest, (0, first, 0))


def _unpack_grads(reduced, shards):
    out, off = {}, 0
    for n, _ in PACK_BF16:
        size = shards[n].size
        out[n] = reduced[off:off + size // FLAT_W].reshape(shards[n].shape)
        off += _piece_rows(size, 16)
    out.update(_small_from_rows(reduced[off:off + SMALL_ROWS], shards, ()))
    return out


def _mixer_weights(kind, j, full, rep):
    if kind == "gla":
        return dict(w_in=jnp.pad(full["gla_w_in"][j], ((0, 0), (0, GLA_IN_PAD - GLA_IN))),
                    wg=jnp.pad(full["gla_w_gate_up"][j], ((0, LANES - GLA_RANK), (0, 0))),
                    bg=full["gla_b_gate"][j][None], hg=full["gla_head_g"][j][None], w_out=full["gla_w_out"][j])
    if kind == "mla":
        return dict(w_in=jnp.pad(full["mla_w_in"][j], ((0, 0), (0, MLA_IN_PAD - MLA_IN))), q_norm_g=rep["mla_q_norm_g"][j][None],
                    w_uq=full["mla_w_uq"][j], kv_norm_g=rep["mla_kv_norm_g"][j][None], w_ukv=full["mla_w_ukv"][j],
                    w_out=full["mla_w_out"][j])
    return dict(w_in=jnp.pad(full["gdn_w_in"][j], ((0, 0), (0, GDN_IN_PAD - GDN_IN))), conv_w=full["gdn_conv_w"][j],
                a_log=rep["gdn_a_log"][j][:, None, None], dt_bias=rep["gdn_dt_bias"][j][:, None, None],
                norm_g=rep["gdn_norm_g"][j][None], w_out=full["gdn_w_out"][j])


def _layer_fwd(xin, mod, gains, kind, mw, w_up, w_down, pos, tag):
    sh_m, sc_m, gt_m, sh_f, sc_f, gt_f = mod
    pre0, pre1, post0, post1 = gains
    h = _premod_fwd(xin, pre0, sc_m, sh_m, name=tag + "_pre0")
    if kind == "gla":
        y, saved = _gla_layer_fwd(h, mw, tag + "_gla")
    elif kind == "mla":
        y, saved = _mla_layer_fwd(h, pos, mw, tag + "_mla")
    else:
        y, saved = _gdn_layer_fwd(h, mw, tag + "_gdn")
    x1 = _postres_fwd(xin, y, post0, gt_m, name=tag + "_post0")
    h2 = _premod_fwd(x1, pre1, sc_f, sh_f, name=tag + "_pre1")
    act = _mm(h2, w_up, out_dtype=BF16, epi="relu2", name=tag + "_up")
    y2 = _mm(act, w_down, name=tag + "_down")
    x2 = _postres_fwd(x1, y2, post1, gt_f, name=tag + "_post1")
    return x2, (xin, h, y, saved, x1, h2, act, y2)


def _layer_bwd(g2, kept, mod, gains, kind, mw, w_up, w_down, tag, gf, rows):
    xin, h, y, saved, x1, h2, act, y2 = kept
    sh_m, sc_m, gt_m, sh_f, sc_f, gt_f = mod
    pre0, pre1, post0, post1 = gains
    dy2, dpost1, dgt_f = _postres_bwd(g2, y2, post1, gt_f, name=tag + "_post1_b")
    du = _mm(dy2, w_down, tb=True, out_dtype=BF16, epi="dact", aux=act, name=tag + "_du")
    gf = _mm(act, dy2, ta=True, out_dtype=gf.dtype, into=(gf, rows[1], "rows"), name=tag + "_dwdown")
    gf = _mm(h2, du, ta=True, out_dtype=gf.dtype, into=(gf, rows[0], "cols"), name=tag + "_dwup")
    dh2 = _mm(du, w_up, tb=True, name=tag + "_dh2")
    g1, dpre1, dsc_f, dsh_f = _premod_bwd(dh2, x1, pre1, sc_f, g2, name=tag + "_pre1_b")
    dy, dpost0, dgt_m = _postres_bwd(g1, y, post0, gt_m, name=tag + "_post0_b")
    mixer_bwd = dict(gla=_gla_layer_bwd, mla=_mla_layer_bwd, gdn=_gdn_layer_bwd)[kind]
    dh, mg, gf = mixer_bwd(dy, h, saved, mw, tag + "_" + kind, sink=(gf, rows[2]))
    g0, dpre0, dsc_m, dsh_m = _premod_bwd(dh, xin, pre0, sc_m, g1, name=tag + "_pre0_b")
    dmod = jnp.concatenate([dsh_m, dsc_m, dgt_m, dsh_f, dsc_f, dgt_f], axis=1)
    return g0, dmod, jnp.concatenate([dpre0, dpre1], axis=0), jnp.concatenate([dpost0, dpost1], axis=0), mg, gf


def kernel(x, c, positions, ada_w, ada_b, norm_pre_g, norm_post_g, gla_w_in, gla_w_gate_up, gla_b_gate, gla_head_g, gla_w_out, mla_w_in, mla_q_norm_g, mla_w_uq, mla_kv_norm_g, mla_w_ukv, mla_w_out, gdn_w_in, gdn_conv_w, gdn_a_log, gdn_dt_bias, gdn_norm_g, gdn_w_out, mlp_w_up, mlp_w_down, loss_target, m_ada_w, m_ada_b, m_norm_pre_g, m_norm_post_g, m_gla_w_in, m_gla_w_gate_up, m_gla_b_gate, m_gla_head_g, m_gla_w_out, m_mla_w_in, m_mla_q_norm_g, m_mla_w_uq, m_mla_kv_norm_g, m_mla_w_ukv, m_mla_w_out, m_gdn_w_in, m_gdn_conv_w, m_gdn_a_log, m_gdn_dt_bias, m_gdn_norm_g, m_gdn_w_out, m_mlp_w_up, m_mlp_w_down, v_ada_w, v_ada_b, v_norm_pre_g, v_norm_post_g, v_gla_w_in, v_gla_w_gate_up, v_gla_b_gate, v_gla_head_g, v_gla_w_out, v_mla_w_in, v_mla_q_norm_g, v_mla_w_uq, v_mla_kv_norm_g, v_mla_w_ukv, v_mla_w_out, v_gdn_w_in, v_gdn_conv_w, v_gdn_a_log, v_gdn_dt_bias, v_gdn_norm_g, v_gdn_w_out, v_mlp_w_up, v_mlp_w_down):
    w = dict(ada_w=ada_w, ada_b=ada_b, norm_pre_g=norm_pre_g, norm_post_g=norm_post_g, gla_w_in=gla_w_in,
             gla_w_gate_up=gla_w_gate_up, gla_b_gate=gla_b_gate, gla_head_g=gla_head_g, gla_w_out=gla_w_out, mla_w_in=mla_w_in,
             mla_q_norm_g=mla_q_norm_g, mla_w_uq=mla_w_uq, mla_kv_norm_g=mla_kv_norm_g, mla_w_ukv=mla_w_ukv, mla_w_out=mla_w_out,
             gdn_w_in=gdn_w_in, gdn_conv_w=gdn_conv_w, gdn_a_log=gdn_a_log, gdn_dt_bias=gdn_dt_bias, gdn_norm_g=gdn_norm_g,
             gdn_w_out=gdn_w_out, mlp_w_up=mlp_w_up, mlp_w_down=mlp_w_down)
    m = dict(zip(WEIGHTS, [m_ada_w, m_ada_b, m_norm_pre_g, m_norm_post_g, m_gla_w_in, m_gla_w_gate_up, m_gla_b_gate, m_gla_head_g,
                           m_gla_w_out, m_mla_w_in, m_mla_q_norm_g, m_mla_w_uq, m_mla_kv_norm_g, m_mla_w_ukv, m_mla_w_out,
                           m_gdn_w_in, m_gdn_conv_w, m_gdn_a_log, m_gdn_dt_bias, m_gdn_norm_g, m_gdn_w_out, m_mlp_w_up, m_mlp_w_down]))
    v = dict(zip(WEIGHTS, [v_ada_w, v_ada_b, v_norm_pre_g, v_norm_post_g, v_gla_w_in, v_gla_w_gate_up, v_gla_b_gate, v_gla_head_g,
                           v_gla_w_out, v_mla_w_in, v_mla_q_norm_g, v_mla_w_uq, v_mla_kv_norm_g, v_mla_w_ukv, v_mla_w_out,
                           v_gdn_w_in, v_gdn_conv_w, v_gdn_a_log, v_gdn_dt_bias, v_gdn_norm_g, v_gdn_w_out, v_mlp_w_up, v_mlp_w_down]))
    t = x.shape[1]
    ix, iy, ic = lax.axis_index("x"), lax.axis_index("y"), lax.axis_index("c")
    me = 4 * ix + 2 * iy + ic
    chip = 2 * ix + iy
    ada_cols = ada_w.shape[2]

    packed = _pack_weights(w)
    zero = jnp.zeros((), jnp.int32)
    gathered = lax.dynamic_update_slice(_chip_gather(packed, name="gather_weights"), packed[None], (chip, zero, zero))
    full = _unpack_weights(gathered, w)

    cond8 = _silu_rows(jnp.pad(c, ((0, 7), (0, 0))), name="cond_silu")
    cond16 = jnp.pad(_all8_gather(cond8, name="gather_cond")[:, 0, :], ((0, 8), (0, 0)))
    mod_cols = []
    for layer in range(DEPTH):
        bias = jnp.broadcast_to(lax.dynamic_slice_in_dim(ada_b[layer], chip * ada_cols, ada_cols)[None], (16, ada_cols))
        mod_cols.append(_mm(cond16, ada_w[layer], epi="add", aux=bias, name=f"ada{layer}")[:8])
    mod_all = _all8_gather(jnp.stack(mod_cols).reshape(DEPTH * 8, ada_cols), name="gather_mod")
    mod = jnp.concatenate([lax.dynamic_slice_in_dim(mod_all[2 * j].reshape(DEPTH, 8, ada_cols), me, 1, axis=1)[:, 0]
                           for j in range(4)], axis=1)

    def layer_args(layer):
        kind, j = MIXERS[layer % 3], layer // 3
        mods = [mod[layer, i * D_MODEL:(i + 1) * D_MODEL][None] for i in range(N_MOD)]
        gains = (full["norm_pre_g"][layer, 0:1], full["norm_pre_g"][layer, 1:2], full["norm_post_g"][layer, 0:1],
                 full["norm_post_g"][layer, 1:2])
        return kind, j, mods, gains, _mixer_weights(kind, j, full, w)

    xs = x[0]
    kept = []
    for layer in range(DEPTH):
        kind, j, mods, gains, mw = layer_args(layer)
        xs, keep = _layer_fwd(xs, mods, gains, kind, mw, full["mlp_w_up"][layer], full["mlp_w_down"][layer], positions[0], f"l{layer}")
        kept.append(keep)
    loss_row, g = _loss_head(xs, loss_target[0], name="loss_head")
    loss = lax.psum(loss_row[0, 0], ("x", "y", "c"))

    grads = {n: [None] * w[n].shape[0] for n, _ in PACK_BF16[N_DIRECT:] + PACK_F32}
    rep_grads = {}
    dmods = [None] * DEPTH
    layout = _grad_layout(w)
    row_of = lambda n, idx: layout[n][0] + idx * layout[n][1]
    gf = lax.empty((4, FLAT_ROWS, FLAT_W), BF16)
    for layer in reversed(range(DEPTH)):
        kind, j, mods, gains, mw = layer_args(layer)
        rows = (row_of("mlp_w_up", layer), row_of("mlp_w_down", layer), row_of(kind + "_w_out", j))
        g, dmods[layer], dpre, dpost, mg, gf = _layer_bwd(
            g, kept[layer], mods, gains, kind, mw, full["mlp_w_up"][layer], full["mlp_w_down"][layer], f"l{layer}", gf, rows)
        grads["norm_pre_g"][layer], grads["norm_post_g"][layer] = dpre, dpost
        for key, val in mg.items():
            name = kind + "_" + key
            if name in grads:
                grads[name][j] = val
            elif name in REPLICATED_SMALL:
                rep_grads[name] = val[None]

    rep_flat = jnp.concatenate([rep_grads[n].reshape(-1) for n in REPLICATED_SMALL])
    dbuf = jnp.concatenate([jnp.concatenate(dmods, axis=0), jnp.pad(rep_flat, (0, N_MOD * D_MODEL - rep_flat.shape[0]))[None],
                            jnp.zeros((3, N_MOD * D_MODEL), F32)], axis=0)
    dall = _all8_gather(dbuf, name="gather_dmod")
    dsum = _sum_slots(dall, name="sum_dmod")
    out_grads = {"ada_b": dsum[:DEPTH]}
    off = 0
    for n in REPLICATED_SMALL:
        out_grads[n] = dsum[DEPTH, off:off + w[n].size].reshape(w[n].shape)
        off += w[n].size
    dada = []
    for layer in range(DEPTH):
        dm16 = jnp.pad(lax.dynamic_slice_in_dim(dall[:, layer, :], chip * ada_cols, ada_cols, axis=1), ((0, 8), (0, 0)))
        dada.append(_mm(cond16, dm16, ta=True, name=f"dada{layer}"))
    out_grads["ada_w"] = jnp.stack(dada)

    reduced = _reduce_scatter(_pack_grads(gf, grads, layout), ic.reshape(1).astype(jnp.int32),
                              chip.reshape(1).astype(jnp.int32), tag="reduce_grads")
    out_grads.update(_unpack_grads(reduced, w))

    deltas, new_m, new_v = {}, {}, {}
    for n in WEIGHTS:
        deltas[n], new_m[n], new_v[n] = _adamw(w[n], out_grads[n], m[n], v[n], name="adamw_" + n)
    return (loss, g[None], *[out_grads[n] for n in WEIGHTS], *[deltas[n] for n in WEIGHTS],
            *[new_m[n] for n in WEIGHTS], *[new_v[n] for n in WEIGHTS])
```

```python
import functools
import math

import jax
import jax.numpy as jnp
from jax import lax
from jax.experimental import pallas as pl
from jax.experimental.pallas import tpu as pltpu

F32 = jnp.float32
BF16 = jnp.bfloat16

D_MODEL = 1024
DEPTH = 4
CHUNK = 64
EPS = 1e-6
NEG_INF = -1e30
N_MOD = 6

GLA_HEADS, GLA_DK, GLA_DV, GLA_RANK = 4, 128, 256, 16
GLA_KW, GLA_VW = GLA_HEADS * GLA_DK, GLA_HEADS * GLA_DV
GLA_IN = 2 * GLA_KW + 2 * GLA_VW + GLA_RANK
GLA_IN_PAD = 3200

MLA_HEADS, MLA_NOPE, MLA_ROPE, MLA_V = 16, 64, 32, 64
MLA_Q_RANK, MLA_KV_RANK = 384, 256
MLA_IN = MLA_Q_RANK + MLA_KV_RANK + MLA_ROPE
MLA_IN_PAD = 768
ROPE_THETA = 10000.0
MLA_QK = MLA_NOPE + MLA_ROPE
LANES = 128

GDN_K_HEADS, GDN_V_HEADS, GDN_DK, GDN_DV, GDN_CONV = 8, 16, 128, 128, 4
GDN_QKW, GDN_VW = GDN_K_HEADS * GDN_DK, GDN_V_HEADS * GDN_DV
GDN_CONV_W = 2 * GDN_QKW + GDN_VW
GDN_IN = GDN_CONV_W + GDN_VW + 2 * GDN_V_HEADS
GDN_IN_PAD = 6400

ADAM_LR, ADAM_B1, ADAM_B2, ADAM_EPS, ADAM_WD, ADAM_STEP = 0.001, 0.9, 0.999, 1e-08, 0.01, 10

VMEM_LIMIT = 56 * 1024 * 1024

NN = ((1,), (0,))
NT = ((1,), (1,))
TN = ((0,), (0,))


def _cp(*sem):
    return pltpu.CompilerParams(dimension_semantics=sem, vmem_limit_bytes=VMEM_LIMIT)


def _pick(n, cands):
    for c in cands:
        if n % c == 0:
            return c
    return n


def _dg(a, b, dims=NN):
    return lax.dot_general(a.astype(BF16), b.astype(BF16), (dims, ((), ())), preferred_element_type=F32)


def _dot3(a, b, dims=NN):
    ah = a.astype(BF16)
    al = (a - ah.astype(F32)).astype(BF16)
    bh = b.astype(BF16)
    bl = (b - bh.astype(F32)).astype(BF16)
    d = lambda u, v: lax.dot_general(u, v, (dims, ((), ())), preferred_element_type=F32)
    return d(ah, bh) + (d(ah, bl) + d(al, bh))


def _sigmoid(x):
    return 1.0 / (1.0 + jnp.exp(-x))


def _softplus(x):
    return jnp.maximum(x, 0.0) + jnp.log(1.0 + jnp.exp(-jnp.abs(x)))


def _iota2(shape, dim):
    return lax.broadcasted_iota(jnp.int32, shape, dim)


def _mm(a, b, *, ta=False, tb=False, out_dtype=F32, epi=None, aux=None, into=None, name):
    m = a.shape[1] if ta else a.shape[0]
    k = a.shape[0] if ta else a.shape[1]
    n = b.shape[0] if tb else b.shape[1]
    assert k == (b.shape[1] if tb else b.shape[0]), (a.shape, b.shape, ta, tb)
    m_tile = m // 4 if into is not None and into[2] == "rows" else m
    tm = _pick(m_tile, (1024, 512, 384, 256, 128))
    tn = _pick(n, (1024, 640, 512, 768, 384, 256, 128))
    tk = _pick(k, (1024, 640, 512, 768, 384, 256, 128))
    nk = k // tk
    dims = ((0 if ta else 1,), (1 if tb else 0,))

    def finish(r, x_ref, o_ref):
        if epi == "relu2":
            r = jnp.square(jnp.maximum(r, 0.0))
        elif epi == "dact":
            r = r * (2.0 * jnp.sqrt(x_ref[...].astype(F32)))
        elif epi == "add":
            r = r + x_ref[...]
        o_ref[...] = r.astype(out_dtype)

    n_in = 2 + (aux is not None) + (into is not None)

    def body(*refs):
        a_ref, b_ref = refs[:2]
        x_ref = refs[2] if aux is not None else None
        o_ref = refs[n_in]
        if nk == 1:
            finish(_dg(a_ref[...], b_ref[...], dims), x_ref, o_ref)
            return
        acc = refs[-1]
        kk = pl.program_id(2)

        @pl.when(kk == 0)
        def _():
            acc[...] = jnp.zeros_like(acc)

        acc[...] += _dg(a_ref[...], b_ref[...], dims)

        @pl.when(kk == nk - 1)
        def _():
            finish(acc[...], x_ref, o_ref)

    a_spec = pl.BlockSpec((tk, tm), lambda i, j, q: (q, i)) if ta else pl.BlockSpec((tm, tk), lambda i, j, q: (i, q))
    b_spec = pl.BlockSpec((tn, tk), lambda i, j, q: (j, q)) if tb else pl.BlockSpec((tk, tn), lambda i, j, q: (q, j))
    o_spec = pl.BlockSpec((tm, tn), lambda i, j, q: (i, j))
    in_specs = [a_spec, b_spec] + ([o_spec] if aux is not None else [])
    args = (a, b) + ((aux,) if aux is not None else ())
    out_shape = jax.ShapeDtypeStruct((m, n), out_dtype)
    aliases = {}
    if into is not None:
        dst, row0, axis = into
        assert dst.dtype == out_dtype and row0 % tm == 0 and tn == FLAT_W and n == (FLAT_W if axis == "rows" else 4 * FLAT_W)
        per = m_tile // tm
        if axis == "rows":
            o_spec = pl.BlockSpec((None, tm, tn), lambda i, j, q: (i // per, row0 // tm + i % per, 0))
        else:
            o_spec = pl.BlockSpec((None, tm, tn), lambda i, j, q: (j, row0 // tm + i, 0))
        in_specs.append(pl.BlockSpec(memory_space=pl.ANY))
        args += (dst,)
        out_shape = jax.ShapeDtypeStruct(dst.shape, dst.dtype)
        aliases = {n_in - 1: 0}
    return pl.pallas_call(
        body, grid=(m // tm, n // tn, nk), in_specs=in_specs, out_specs=o_spec, out_shape=out_shape,
        scratch_shapes=[pltpu.VMEM((tm, tn), F32)] if nk > 1 else [], input_output_aliases=aliases,
        compiler_params=_cp("parallel", "parallel", "arbitrary"), name=name)(*args)


def _row_tile(t):
    return _pick(t, (512, 256, 128, 64, 8))


def _premod_fwd(x, g, sc, sh, *, name):
    t, c = x.shape
    tr = _row_tile(t)

    def body(x_ref, g_ref, sc_ref, sh_ref, h_ref):
        xv = x_ref[...]
        r = lax.rsqrt(jnp.mean(xv * xv, axis=-1, keepdims=True) + EPS)
        h_ref[...] = (((xv * r) * g_ref[...]) * (1.0 + sc_ref[...]) + sh_ref[...]).astype(BF16)

    row = pl.BlockSpec((tr, c), lambda i: (i, 0))
    vec = pl.BlockSpec((1, c), lambda i: (0, 0))
    return pl.pallas_call(body, grid=(t // tr,), in_specs=[row, vec, vec, vec], out_specs=row,
                          out_shape=jax.ShapeDtypeStruct((t, c), BF16), compiler_params=_cp("parallel"), name=name)(x, g, sc, sh)


def _premod_bwd(dh, x, g, sc, gin, *, name):
    t, c = x.shape
    tr = _row_tile(t)

    def body(dh_ref, x_ref, g_ref, sc_ref, gin_ref, gout_ref, dg_ref, dsc_ref, dsh_ref):
        @pl.when(pl.program_id(0) == 0)
        def _():
            dg_ref[...] = jnp.zeros_like(dg_ref)
            dsc_ref[...] = jnp.zeros_like(dsc_ref)
            dsh_ref[...] = jnp.zeros_like(dsh_ref)

        xv = x_ref[...]
        dhv = dh_ref[...].astype(F32)
        gv = g_ref[...]
        one_sc = 1.0 + sc_ref[...]
        r = lax.rsqrt(jnp.mean(xv * xv, axis=-1, keepdims=True) + EPS)
        nv = xv * r
        dsh_ref[...] += jnp.sum(dhv, axis=0, keepdims=True)
        dsc_ref[...] += jnp.sum(dhv * (nv * gv), axis=0, keepdims=True)
        dg_ref[...] += jnp.sum(dhv * nv * one_sc, axis=0, keepdims=True)
        dn = dhv * gv * one_sc
        dx = r * (dn - nv * jnp.mean(dn * nv, axis=-1, keepdims=True))
        gout_ref[...] = gin_ref[...] + dx

    row = pl.BlockSpec((tr, c), lambda i: (i, 0))
    vec = pl.BlockSpec((1, c), lambda i: (0, 0))
    vs = jax.ShapeDtypeStruct((1, c), F32)
    return pl.pallas_call(body, grid=(t // tr,), in_specs=[row, row, vec, vec, row], out_specs=[row, vec, vec, vec],
                          out_shape=[jax.ShapeDtypeStruct((t, c), F32), vs, vs, vs],
                          compiler_params=_cp("arbitrary"), name=name)(dh, x, g, sc, gin)


def _postres_fwd(x, y, g, gt, *, name):
    t, c = x.shape
    tr = _row_tile(t)

    def body(x_ref, y_ref, g_ref, gt_ref, o_ref):
        yv = y_ref[...]
        r = lax.rsqrt(jnp.mean(yv * yv, axis=-1, keepdims=True) + EPS)
        o_ref[...] = x_ref[...] + gt_ref[...] * ((yv * r) * g_ref[...])

    row = pl.BlockSpec((tr, c), lambda i: (i, 0))
    vec = pl.BlockSpec((1, c), lambda i: (0, 0))
    return pl.pallas_call(body, grid=(t // tr,), in_specs=[row, row, vec, vec], out_specs=row,
                          out_shape=jax.ShapeDtypeStruct((t, c), F32), compiler_params=_cp("parallel"), name=name)(x, y, g, gt)


def _postres_bwd(gout, y, g, gt, *, name):
    t, c = y.shape
    tr = _row_tile(t)

    def body(go_ref, y_ref, g_ref, gt_ref, dy_ref, dg_ref, dgt_ref):
        @pl.when(pl.program_id(0) == 0)
        def _():
            dg_ref[...] = jnp.zeros_like(dg_ref)
            dgt_ref[...] = jnp.zeros_like(dgt_ref)

        yv = y_ref[...]
        gov = go_ref[...]
        gv = g_ref[...]
        gtv = gt_ref[...]
        r = lax.rsqrt(jnp.mean(yv * yv, axis=-1, keepdims=True) + EPS)
        z = yv * r
        dgt_ref[...] += jnp.sum(gov * (z * gv), axis=0, keepdims=True)
        dg_ref[...] += jnp.sum(gov * gtv * z, axis=0, keepdims=True)
        dz = gov * gtv * gv
        dy_ref[...] = (r * (dz - z * jnp.mean(dz * z, axis=-1, keepdims=True))).astype(BF16)

    row = pl.BlockSpec((tr, c), lambda i: (i, 0))
    vec = pl.BlockSpec((1, c), lambda i: (0, 0))
    vs = jax.ShapeDtypeStruct((1, c), F32)
    return pl.pallas_call(body, grid=(t // tr,), in_specs=[row, row, vec, vec], out_specs=[row, vec, vec],
                          out_shape=[jax.ShapeDtypeStruct((t, c), BF16), vs, vs],
                          compiler_params=_cp("arbitrary"), name=name)(gout, y, g, gt)


def _loss_head(y, tgt, *, name):
    t, c = y.shape
    tr = _row_tile(t)

    def body(y_ref, t_ref, l_ref, dy_ref):
        @pl.when(pl.program_id(0) == 0)
        def _():
            l_ref[...] = jnp.zeros_like(l_ref)

        d = y_ref[...] - t_ref[...]
        dy_ref[...] = d * (1.0 / c)
        l_ref[...] += 0.5 * jnp.sum(jnp.mean(d * d, axis=-1, keepdims=True))

    row = pl.BlockSpec((tr, c), lambda i: (i, 0))
    return pl.pallas_call(body, grid=(t // tr,), in_specs=[row, row],
                          out_specs=[pl.BlockSpec((1, LANES), lambda i: (0, 0)), row],
                          out_shape=[jax.ShapeDtypeStruct((1, LANES), F32), jax.ShapeDtypeStruct((t, c), F32)],
                          compiler_params=_cp("arbitrary"), name=name)(y, tgt)


def _adamw(w, g, m, v, *, name):
    shape = w.shape
    c = shape[-1]
    r = math.prod(shape[:-1])
    w2, g2, m2, v2 = (a.reshape(r, c) for a in (w, g, m, v))
    tr = r
    for cand in (1024, 512, 256, 128, 64, 32, 16, 8):
        if r % cand == 0 and cand * c * 4 <= (1 << 20):
            tr = cand
            break
    c1 = 1.0 - ADAM_B1 ** ADAM_STEP
    c2 = 1.0 - ADAM_B2 ** ADAM_STEP

    def body(w_ref, g_ref, m_ref, v_ref, d_ref, nm_ref, nv_ref):
        gv = g_ref[...]
        mn = ADAM_B1 * m_ref[...] + (1.0 - ADAM_B1) * gv
        vn = ADAM_B2 * v_ref[...] + (1.0 - ADAM_B2) * jnp.square(gv)
        m_hat = mn / c1
        v_hat = vn / c2
        d_ref[...] = -ADAM_LR * (m_hat / (jnp.sqrt(v_hat) + ADAM_EPS) + ADAM_WD * w_ref[...])
        nm_ref[...] = mn
        nv_ref[...] = vn

    blk = pl.BlockSpec((tr, c), lambda i: (i, 0))
    s = jax.ShapeDtypeStruct((r, c), F32)
    d, nm, nv = pl.pallas_call(body, grid=(r // tr,), in_specs=[blk] * 4, out_specs=[blk] * 3, out_shape=[s, s, s],
                               compiler_params=_cp("parallel"), name=name)(w2, g2, m2, v2)
    return d.reshape(shape), nm.reshape(shape), nv.reshape(shape)


def _gla_parts(p_ref, wg_ref, bg_ref):
    q = p_ref[:, 0:GLA_KW] * (GLA_DK ** -0.5)
    k = p_ref[:, GLA_KW:2 * GLA_KW]
    glr = p_ref[:, 2 * GLA_KW + 2 * GLA_VW:GLA_IN_PAD]
    gate = _dg(glr, wg_ref[...]) + bg_ref[...]
    log_a = (jnp.minimum(gate, 0.0) - jnp.log(1.0 + jnp.exp(-jnp.abs(gate)))) * (1.0 / 16.0)
    tril = (_iota2((CHUNK, CHUNK), 0) >= _iota2((CHUNK, CHUNK), 1)).astype(F32)
    cum = _dot_sel(tril, log_a)
    c_last = cum[CHUNK - 1:CHUNK, :]
    f = jnp.exp(c_last - cum)
    dec = jnp.exp(c_last)
    return q, k, glr, gate, f, k * f, dec


def _gla_fwd(p, wg, bg, hg, *, name):
    t = p.shape[0]
    nc = t // CHUNK

    def body(p_ref, wg_ref, bg_ref, hg_ref, og_ref, s_ref, st):
        @pl.when(pl.program_id(0) == 0)
        def _():
            st[...] = jnp.zeros_like(st)

        q, _, _, _, _, ke, dec = _gla_parts(p_ref, wg_ref, bg_ref)
        hs = range(GLA_HEADS)
        ks = [slice(h * GLA_DK, (h + 1) * GLA_DK) for h in hs]
        vs = [slice(2 * GLA_KW + h * GLA_DV, 2 * GLA_KW + (h + 1) * GLA_DV) for h in hs]
        rs = [slice(2 * GLA_KW + GLA_VW + h * GLA_DV, 2 * GLA_KW + GLA_VW + (h + 1) * GLA_DV) for h in hs]
        s_new = [st[h] * dec[:, ks[h]] + _dg(p_ref[:, vs[h]], ke[:, ks[h]], TN) for h in hs]
        o = [_dg(q[:, ks[h]], s_new[h], NT) for h in hs]
        for h in hs:
            st[h] = s_new[h]
            s_ref[0, h] = s_new[h]
            rn = lax.rsqrt(jnp.mean(o[h] * o[h], axis=-1, keepdims=True) + EPS)
            rv = p_ref[:, rs[h]]
            og_ref[:, h * GLA_DV:(h + 1) * GLA_DV] = (((o[h] * rn) * hg_ref[...]) * (rv * _sigmoid(rv))).astype(BF16)

    full = lambda a: pl.BlockSpec(a.shape, lambda n: (0,) * a.ndim)
    return pl.pallas_call(
        body, grid=(nc,),
        in_specs=[pl.BlockSpec((CHUNK, GLA_IN_PAD), lambda n: (n, 0)), full(wg), full(bg), full(hg)],
        out_specs=[pl.BlockSpec((CHUNK, GLA_VW), lambda n: (n, 0)),
                   pl.BlockSpec((1, GLA_HEADS, GLA_DV, GLA_DK), lambda n: (n, 0, 0, 0))],
        out_shape=[jax.ShapeDtypeStruct((t, GLA_VW), BF16), jax.ShapeDtypeStruct((nc, GLA_HEADS, GLA_DV, GLA_DK), F32)],
        scratch_shapes=[pltpu.VMEM((GLA_HEADS, GLA_DV, GLA_DK), F32)],
        compiler_params=_cp("arbitrary"), name=name)(p, wg, bg, hg)


def _gla_bwd(p, dog, sall, wg, bg, hg, *, name):
    t = p.shape[0]
    nc = t // CHUNK

    def body(p_ref, dog_ref, s1_ref, s0_ref, wg_ref, bg_ref, hg_ref, dp_ref, dwg_ref, dbg_ref, dhg_ref, gt):
        i = pl.program_id(0)

        @pl.when(i == 0)
        def _():
            gt[...] = jnp.zeros_like(gt)
            dwg_ref[...] = jnp.zeros_like(dwg_ref)
            dbg_ref[...] = jnp.zeros_like(dbg_ref)
            dhg_ref[...] = jnp.zeros_like(dhg_ref)

        has_prev = (i < nc - 1).astype(F32)
        q, k, glr, gate, f, ke, dec = _gla_parts(p_ref, wg_ref, bg_ref)
        hgv = hg_ref[...]
        hs = range(GLA_HEADS)
        ks = [slice(h * GLA_DK, (h + 1) * GLA_DK) for h in hs]
        vs = [slice(2 * GLA_KW + h * GLA_DV, 2 * GLA_KW + (h + 1) * GLA_DV) for h in hs]
        rs = [slice(2 * GLA_KW + GLA_VW + h * GLA_DV, 2 * GLA_KW + GLA_VW + (h + 1) * GLA_DV) for h in hs]
        s1 = [s1_ref[0, h] for h in hs]
        o = [_dg(q[:, ks[h]], s1[h], NT) for h in hs]
        dhg = jnp.zeros((1, GLA_DV), F32)
        do = []
        for h in hs:
            rv = p_ref[:, rs[h]]
            rn = lax.rsqrt(jnp.mean(o[h] * o[h], axis=-1, keepdims=True) + EPS)
            z = o[h] * rn
            sg = _sigmoid(rv)
            sl = rv * sg
            dogh = dog_ref[:, h * GLA_DV:(h + 1) * GLA_DV].astype(F32)
            dhg = dhg + jnp.sum(dogh * z * sl, axis=0, keepdims=True)
            dp_ref[:, rs[h]] = (dogh * (z * hgv) * (sg * (1.0 + rv * (1.0 - sg)))).astype(BF16)
            dz = dogh * sl * hgv
            do.append(rn * (dz - z * jnp.mean(dz * z, axis=-1, keepdims=True)))
        dhg_ref[...] += dhg
        g_tot = [gt[h] + _dg(do[h], q[:, ks[h]], TN) for h in hs]
        dq = [_dg(do[h], s1[h], NN) for h in hs]
        dke_parts = [_dg(p_ref[:, vs[h]], g_tot[h], NN) for h in hs]
        dv = [_dg(ke[:, ks[h]], g_tot[h], NT) for h in hs]
        ddec_parts = []
        for h in hs:
            dp_ref[:, ks[h]] = (dq[h] * (GLA_DK ** -0.5)).astype(BF16)
            dp_ref[:, vs[h]] = dv[h].astype(BF16)
            ddec_parts.append(jnp.sum(g_tot[h] * (s0_ref[0, h] * has_prev), axis=0, keepdims=True))
            gt[h] = g_tot[h] * dec[:, ks[h]]
        dke = jnp.concatenate(dke_parts, axis=1)
        ddec = jnp.concatenate(ddec_parts, axis=1)
        dp_ref[:, GLA_KW:2 * GLA_KW] = (dke * f).astype(BF16)
        stril = (_iota2((CHUNK, CHUNK), 0) > _iota2((CHUNK, CHUNK), 1)).astype(F32)
        dlog_a = _dot_sel(stril, dke * ke) + ddec * dec
        dgate = dlog_a * (1.0 / 16.0) * _sigmoid(-gate)
        dp_ref[:, 2 * GLA_KW + 2 * GLA_VW:GLA_IN_PAD] = _dg(dgate, wg_ref[...], NT).astype(BF16)
        dwg_ref[...] += _dg(glr, dgate, TN)
        dbg_ref[...] += jnp.sum(dgate, axis=0, keepdims=True)

    full = lambda a: pl.BlockSpec(a.shape, lambda n: (0,) * a.ndim)
    sblk = (1, GLA_HEADS, GLA_DV, GLA_DK)
    return pl.pallas_call(
        body, grid=(nc,),
        in_specs=[pl.BlockSpec((CHUNK, GLA_IN_PAD), lambda n: (nc - 1 - n, 0)),
                  pl.BlockSpec((CHUNK, GLA_VW), lambda n: (nc - 1 - n, 0)),
                  pl.BlockSpec(sblk, lambda n: (nc - 1 - n, 0, 0, 0)),
                  pl.BlockSpec(sblk, lambda n: (jnp.maximum(nc - 2 - n, 0), 0, 0, 0)),
                  full(wg), full(bg), full(hg)],
        out_specs=[pl.BlockSpec((CHUNK, GLA_IN_PAD), lambda n: (nc - 1 - n, 0)), full(wg), full(bg), full(hg)],
        out_shape=[jax.ShapeDtypeStruct((t, GLA_IN_PAD), BF16), jax.ShapeDtypeStruct(wg.shape, F32),
                   jax.ShapeDtypeStruct(bg.shape, F32), jax.ShapeDtypeStruct(hg.shape, F32)],
        scratch_shapes=[pltpu.VMEM((GLA_HEADS, GLA_DV, GLA_DK), F32)],
        compiler_params=_cp("arbitrary"), name=name)(p, dog, sall, sall, wg, bg, hg)


def _gla_layer_fwd(h, w, tag):
    p = _mm(h, w["w_in"], name=tag + "_in")
    og, sall = _gla_fwd(p, w["wg"], w["bg"], w["hg"], name=tag + "_scan")
    y = _mm(og, w["w_out"], name=tag + "_out")
    return y, (p, og, sall)


def _dw_out(act, dy, sink, name):
    if sink is None:
        return _mm(act, dy, ta=True, name=name), None
    return None, _mm(act, dy, ta=True, out_dtype=sink[0].dtype, into=(sink[0], sink[1], "rows"), name=name)


def _gla_layer_bwd(dy, h, saved, w, tag, sink=None):
    p, og, sall = saved
    dog = _mm(dy, w["w_out"], tb=True, name=tag + "_dog")
    dw_out, gf = _dw_out(og, dy, sink, tag + "_dwout")
    dp, dwg, dbg, dhg = _gla_bwd(p, dog, sall, w["wg"], w["bg"], w["hg"], name=tag + "_scanb")
    dw_in = _mm(h, dp, ta=True, name=tag + "_dwin")
    dh = _mm(dp, w["w_in"], tb=True, out_dtype=BF16, name=tag + "_dh")
    grads = dict(w_in=dw_in[:, :GLA_IN], w_gate_up=dwg[:GLA_RANK], b_gate=dbg[0], head_g=dhg[0], w_out=dw_out)
    return dh, grads, gf


def _rope_tables(pos, inv_freq, *, name):
    t = pos.shape[0]
    tr = _row_tile(t)
    half = MLA_ROPE // 2

    def body(p_ref, f_ref, c_ref, s1_ref, s2_ref, s1b_ref, s2b_ref):
        ang = p_ref[...].astype(F32) * f_ref[...]
        lane = _iota2((tr, LANES), 1)
        lo = (lane >= MLA_NOPE) & (lane < MLA_NOPE + half)
        hi = (lane >= MLA_NOPE + half) & (lane < MLA_QK)
        cs, sn = jnp.cos(ang), jnp.sin(ang)
        zero = jnp.zeros_like(cs)
        c_ref[...] = jnp.where(lane < MLA_NOPE, 1.0, jnp.where(lane < MLA_QK, cs, 0.0))
        s1_ref[...] = jnp.where(lo, -sn, zero)
        s2_ref[...] = jnp.where(hi, sn, zero)
        s1b_ref[...] = jnp.where(lo, sn, zero)
        s2b_ref[...] = jnp.where(hi, -sn, zero)

    row = pl.BlockSpec((tr, LANES), lambda i: (i, 0))
    s = jax.ShapeDtypeStruct((t, LANES), F32)
    return pl.pallas_call(body, grid=(t // tr,),
                          in_specs=[pl.BlockSpec((tr, 1), lambda i: (i, 0)), pl.BlockSpec((1, LANES), lambda i: (0, 0))],
                          out_specs=[row] * 5, out_shape=[s] * 5, compiler_params=_cp("parallel"), name=name)(pos, inv_freq)


def _rope(x, c, s1, s2, *, out_dtype, sum_heads=False, name):
    nh, t, _ = x.shape
    tr = _row_tile(t)
    half = MLA_ROPE // 2

    def body(x_ref, c_ref, s1_ref, s2_ref, o_ref):
        total = None
        for h in range(nh):
            xv = x_ref[h].astype(F32)
            y = xv * c_ref[...] + pltpu.roll(xv, LANES - half, 1) * s1_ref[...] + pltpu.roll(xv, half, 1) * s2_ref[...]
            if sum_heads:
                total = y if total is None else total + y
            else:
                o_ref[h] = y.astype(out_dtype)
        if sum_heads:
            o_ref[...] = total

    tab = pl.BlockSpec((tr, LANES), lambda i: (i, 0))
    xs = pl.BlockSpec((nh, tr, LANES), lambda i: (0, i, 0))
    if sum_heads:
        return pl.pallas_call(body, grid=(t // tr,), in_specs=[xs, tab, tab, tab], out_specs=tab,
                              out_shape=jax.ShapeDtypeStruct((t, LANES), F32),
                              compiler_params=_cp("parallel"), name=name)(x, c, s1, s2)
    return pl.pallas_call(body, grid=(t // tr,), in_specs=[xs, tab, tab, tab], out_specs=xs,
                          out_shape=jax.ShapeDtypeStruct(x.shape, out_dtype),
                          compiler_params=_cp("parallel"), name=name)(x, c, s1, s2)


FLASH_BLK = 512


def _diag_mask(blk):
    return (_iota2((blk, blk), 1) // CHUNK) <= (_iota2((blk, blk), 0) // CHUNK)


def _flash_fwd(q, k, v, *, name):
    nh, t, _ = q.shape
    blk = min(FLASH_BLK, t)
    scale = MLA_QK ** -0.5

    def body(q_ref, k_ref, v_ref, o_ref, lse_ref):
        i = pl.program_id(1)
        qv = q_ref[0]

        def step(j, carry, masked):
            m, l, acc = carry
            off = pl.multiple_of(j * blk, blk)
            kb = k_ref[0, pl.ds(off, blk), :]
            vb = v_ref[0, pl.ds(off, blk), :]
            s = _dg(qv, kb, NT) * scale
            if masked:
                s = jnp.where(_diag_mask(blk), s, NEG_INF)
            m_new = jnp.maximum(m, jnp.max(s, axis=-1, keepdims=True))
            p = jnp.exp(s - m_new)
            alpha = jnp.exp(m - m_new)
            return m_new, alpha * l + jnp.sum(p, axis=-1, keepdims=True), alpha * acc + _dg(p, vb, NN)

        init = (jnp.full((blk, 1), NEG_INF, F32), jnp.zeros((blk, 1), F32), jnp.zeros((blk, MLA_V), F32))
        carry = lax.fori_loop(0, i, lambda j, c: step(j, c, False), init)
        m, l, acc = step(i, carry, True)
        o_ref[0] = (acc / l).astype(BF16)
        lse_ref[0] = m + jnp.log(l)

    qs = pl.BlockSpec((1, blk, LANES), lambda h, i: (h, i, 0))
    return pl.pallas_call(
        body, grid=(nh, t // blk),
        in_specs=[qs, pl.BlockSpec((1, t, LANES), lambda h, i: (h, 0, 0)), pl.BlockSpec((1, t, MLA_V), lambda h, i: (h, 0, 0))],
        out_specs=[pl.BlockSpec((1, blk, MLA_V), lambda h, i: (h, i, 0)), pl.BlockSpec((1, blk, 1), lambda h, i: (h, i, 0))],
        out_shape=[jax.ShapeDtypeStruct((nh, t, MLA_V), BF16), jax.ShapeDtypeStruct((nh, t, 1), F32)],
        compiler_params=_cp("parallel", "parallel"), name=name)(q, k, v)


def _flash_bwd(q, k, v, do, o, lse, *, name):
    nh, t, _ = q.shape
    blk = min(FLASH_BLK, t)
    nq = t // blk
    scale = MLA_QK ** -0.5

    def body(q_ref, k_ref, v_ref, do_ref, o_ref, lse_ref, dq_ref, dk_ref, dv_ref, dl):
        j = pl.program_id(1)

        @pl.when(j == 0)
        def _():
            dq_ref[...] = jnp.zeros_like(dq_ref)
            dl[...] = jnp.sum(do_ref[0].astype(F32) * o_ref[0].astype(F32), axis=-1, keepdims=True)

        kb = k_ref[0]
        vb = v_ref[0]

        def step(i, carry, masked):
            dk, dv = carry
            rows = pl.ds(pl.multiple_of(i * blk, blk), blk)
            qb = q_ref[0, rows, :]
            dob = do_ref[0, rows, :]
            s = _dg(qb, kb, NT) * scale
            if masked:
                s = jnp.where(_diag_mask(blk), s, NEG_INF)
            p = jnp.exp(s - lse_ref[0, rows, :])
            ds = p * (_dg(dob, vb, NT) - dl[rows, :]) * scale
            dq_ref[0, rows, :] += _dg(ds, kb, NN)
            return dk + _dg(ds, qb, TN), dv + _dg(p, dob, TN)

        carry = step(j, (jnp.zeros((blk, LANES), F32), jnp.zeros((blk, MLA_V), F32)), True)
        dk, dv = lax.fori_loop(j + 1, nq, lambda i, c: step(i, c, False), carry)
        dk_ref[0] = dk
        dv_ref[0] = dv

    ks = pl.BlockSpec((1, blk, LANES), lambda h, j: (h, j, 0))
    vs = pl.BlockSpec((1, blk, MLA_V), lambda h, j: (h, j, 0))
    fl = lambda w: pl.BlockSpec((1, t, w), lambda h, j: (h, 0, 0))
    return pl.pallas_call(
        body, grid=(nh, nq),
        in_specs=[fl(LANES), ks, vs, fl(MLA_V), fl(MLA_V), fl(1)],
        out_specs=[fl(LANES), ks, vs],
        out_shape=[jax.ShapeDtypeStruct((nh, t, LANES), F32), jax.ShapeDtypeStruct((nh, t, LANES), F32),
                   jax.ShapeDtypeStruct((nh, t, MLA_V), F32)],
        scratch_shapes=[pltpu.VMEM((t, 1), F32)],
        compiler_params=_cp("parallel", "arbitrary"), name=name)(q, k, v, do, o, lse)


def _heads_first(a, width):
    t = a.shape[0]
    return a.reshape(t, MLA_HEADS, width).transpose(1, 0, 2)


def _heads_last(a):
    return a.transpose(1, 0, 2).reshape(a.shape[1], -1)


def _mla_layer_fwd(h, pos, w, tag):
    t = h.shape[0]
    zq = jnp.zeros((1, MLA_Q_RANK), F32)
    zkv = jnp.zeros((1, MLA_KV_RANK), F32)
    p = _mm(h, w["w_in"], name=tag + "_in")
    cq, ckv, krp = p[:, :MLA_Q_RANK], p[:, MLA_Q_RANK:MLA_Q_RANK + MLA_KV_RANK], p[:, MLA_Q_RANK + MLA_KV_RANK:MLA_IN]
    qn = _premod_fwd(cq, w["q_norm_g"], zq, zq, name=tag + "_qnorm")
    kvn = _premod_fwd(ckv, w["kv_norm_g"], zkv, zkv, name=tag + "_kvnorm")
    q = _mm(qn, w["w_uq"], name=tag + "_uq")
    kv = _mm(kvn, w["w_ukv"], name=tag + "_ukv")
    q_pre = jnp.pad(_heads_first(q, MLA_QK), ((0, 0), (0, 0), (0, LANES - MLA_QK)))
    kv3 = _heads_first(kv, MLA_NOPE + MLA_V)
    k_pre = jnp.concatenate([kv3[:, :, :MLA_NOPE], jnp.broadcast_to(krp[None], (MLA_HEADS, t, MLA_ROPE)),
                             jnp.zeros((MLA_HEADS, t, LANES - MLA_QK), F32)], axis=-1)
    vh = kv3[:, :, MLA_NOPE:].astype(BF16)
    half = MLA_ROPE // 2
    freq = ROPE_THETA ** (-jnp.arange(half, dtype=F32) / half)
    inv_freq = jnp.concatenate([jnp.zeros((MLA_NOPE,), F32), freq, freq, jnp.zeros((LANES - MLA_QK,), F32)])[None]
    tabs = _rope_tables(pos.reshape(t, 1), inv_freq, name=tag + "_tables")
    qr = _rope(q_pre, tabs[0], tabs[1], tabs[2], out_dtype=BF16, name=tag + "_ropeq")
    kr = _rope(k_pre, tabs[0], tabs[1], tabs[2], out_dtype=BF16, name=tag + "_ropek")
    o, lse = _flash_fwd(qr, kr, vh, name=tag + "_attn")
    of = _heads_last(o)
    y = _mm(of, w["w_out"], name=tag + "_out")
    return y, (cq, ckv, qn, kvn, qr, kr, vh, o, lse, of, tabs)


def _mla_layer_bwd(dy, h, saved, w, tag, sink=None):
    cq, ckv, qn, kvn, qr, kr, vh, o, lse, of, tabs = saved
    t = h.shape[0]
    zq = jnp.zeros((1, MLA_Q_RANK), F32)
    zkv = jnp.zeros((1, MLA_KV_RANK), F32)
    dof = _mm(dy, w["w_out"], tb=True, out_dtype=BF16, name=tag + "_dof")
    dw_out, gf = _dw_out(of, dy, sink, tag + "_dwout")
    do = _heads_first(dof, MLA_V)
    dqr, dkr, dv = _flash_bwd(qr, kr, vh, do, o, lse, name=tag + "_attn_b")
    dq_pre = _rope(dqr, tabs[0], tabs[3], tabs[4], out_dtype=F32, name=tag + "_ropeq_b")
    dk_sum = _rope(dkr, tabs[0], tabs[3], tabs[4], out_dtype=F32, sum_heads=True, name=tag + "_ropek_b")
    dq = _heads_last(dq_pre[:, :, :MLA_QK])
    dkv = _heads_last(jnp.concatenate([dkr[:, :, :MLA_NOPE], dv], axis=-1))
    dw_uq = _mm(qn, dq, ta=True, name=tag + "_dwuq")
    dqn = _mm(dq, w["w_uq"], tb=True, name=tag + "_dqn")
    dw_ukv = _mm(kvn, dkv, ta=True, name=tag + "_dwukv")
    dkvn = _mm(dkv, w["w_ukv"], tb=True, name=tag + "_dkvn")
    dcq, dqg, _, _ = _premod_bwd(dqn, cq, w["q_norm_g"], zq, jnp.zeros_like(cq), name=tag + "_qnorm_b")
    dckv, dkvg, _, _ = _premod_bwd(dkvn, ckv, w["kv_norm_g"], zkv, jnp.zeros_like(ckv), name=tag + "_kvnorm_b")
    dp = jnp.concatenate([dcq, dckv, dk_sum[:, MLA_NOPE:MLA_QK], jnp.zeros((t, MLA_IN_PAD - MLA_IN), F32)], axis=1).astype(BF16)
    dw_in = _mm(h, dp, ta=True, name=tag + "_dwin")
    dh = _mm(dp, w["w_in"], tb=True, out_dtype=BF16, name=tag + "_dh")
    grads = dict(w_in=dw_in[:, :MLA_IN], q_norm_g=dqg[0], w_uq=dw_uq, kv_norm_g=dkvg[0], w_ukv=dw_ukv, w_out=dw_out)
    return dh, grads, gf


CONV_HALO = 8


def _conv_tiles(t):
    return min(512, t), 512


def _gdn_conv_fwd(p, w, *, name):
    t = p.shape[0]
    tr, tc = _conv_tiles(t)
    hb = tr // CONV_HALO

    def body(x_ref, halo_ref, w_ref, o_ref, buf):
        i = pl.program_id(0)
        buf[0:CONV_HALO, :] = halo_ref[...] * (i > 0).astype(F32)
        buf[CONV_HALO:CONV_HALO + tr, :] = x_ref[...]
        base = CONV_HALO - (GDN_CONV - 1)
        acc = buf[pl.ds(base, tr), :] * w_ref[0:1, :]
        for j in range(1, GDN_CONV):
            acc = acc + buf[pl.ds(base + j, tr), :] * w_ref[j:j + 1, :]
        o_ref[...] = acc * _sigmoid(acc)

    return pl.pallas_call(
        body, grid=(t // tr, GDN_CONV_W // tc),
        in_specs=[pl.BlockSpec((tr, tc), lambda i, j: (i, j)),
                  pl.BlockSpec((CONV_HALO, tc), lambda i, j: (jnp.maximum(i * hb - 1, 0), j)),
                  pl.BlockSpec((GDN_CONV, tc), lambda i, j: (0, j))],
        out_specs=pl.BlockSpec((tr, tc), lambda i, j: (i, j)),
        out_shape=jax.ShapeDtypeStruct((t, GDN_CONV_W), F32),
        scratch_shapes=[pltpu.VMEM((tr + CONV_HALO, tc), F32)],
        compiler_params=_cp("parallel", "parallel"), name=name)(p, p, w)


def _gdn_conv_bwd(d, p, w, col0, *, name):
    t = p.shape[0]
    tr, tc = _conv_tiles(t)
    hb = tr // CONV_HALO
    nr = t // tr
    ext = tr + CONV_HALO

    def body(x_ref, xp_ref, xn_ref, d_ref, dn_ref, w_ref, dx_ref, dw_ref, bufx, bufd):
        i = pl.program_id(1)

        @pl.when(i == 0)
        def _():
            dw_ref[...] = jnp.zeros_like(dw_ref)

        last = (i < nr - 1).astype(F32)
        bufx[0:CONV_HALO, :] = xp_ref[...] * (i > 0).astype(F32)
        bufx[CONV_HALO:CONV_HALO + tr, :] = x_ref[...]
        bufx[CONV_HALO + tr:, :] = xn_ref[...] * last
        base = CONV_HALO - (GDN_CONV - 1)
        acc = bufx[pl.ds(base, ext), :] * w_ref[0:1, :]
        for j in range(1, GDN_CONV):
            acc = acc + bufx[pl.ds(base + j, ext), :] * w_ref[j:j + 1, :]
        sg = _sigmoid(acc)
        dsilu = sg * (1.0 + acc * (1.0 - sg))
        bufd[0:tr, :] = d_ref[...] * dsilu[0:tr, :]
        bufd[tr:, :] = dn_ref[...] * last * dsilu[tr:, :]
        dx = bufd[pl.ds(GDN_CONV - 1, tr), :] * w_ref[0:1, :]
        for j in range(1, GDN_CONV):
            dx = dx + bufd[pl.ds(GDN_CONV - 1 - j, tr), :] * w_ref[j:j + 1, :]
        dx_ref[...] = dx.astype(BF16)
        dc = bufd[0:tr, :]
        for j in range(GDN_CONV):
            dw_ref[j:j + 1, :] += jnp.sum(dc * bufx[pl.ds(base + j, tr), :], axis=0, keepdims=True)

    cb = col0 // tc
    width = d.shape[1]
    main = lambda off: pl.BlockSpec((tr, tc), lambda j, i: (i, j + off))
    prev = pl.BlockSpec((CONV_HALO, tc), lambda j, i: (jnp.maximum(i * hb - 1, 0), j + cb))
    nxt = lambda off: pl.BlockSpec((CONV_HALO, tc), lambda j, i: (jnp.minimum((i + 1) * hb, t // CONV_HALO - 1), j + off))
    wsp = lambda off: pl.BlockSpec((GDN_CONV, tc), lambda j, i: (0, j + off))
    return pl.pallas_call(
        body, grid=(width // tc, nr),
        in_specs=[main(cb), prev, nxt(cb), main(0), nxt(0), wsp(cb)], out_specs=[main(0), wsp(0)],
        out_shape=[jax.ShapeDtypeStruct((t, width), BF16), jax.ShapeDtypeStruct((GDN_CONV, width), F32)],
        scratch_shapes=[pltpu.VMEM((tr + 2 * CONV_HALO, tc), F32), pltpu.VMEM((ext, tc), F32)],
        compiler_params=_cp("parallel", "arbitrary"), name=name)(p, p, p, d, d, w)


def _dot_sel(sel, b, dims=NN, sel_first=True):
    s = sel.astype(BF16)
    b1 = b.astype(BF16)
    r1 = b - b1.astype(F32)
    b2 = r1.astype(BF16)
    b3 = (r1 - b2.astype(F32)).astype(BF16)
    if sel_first:
        d = lambda v: lax.dot_general(s, v, (dims, ((), ())), preferred_element_type=F32)
    else:
        d = lambda v: lax.dot_general(v, s, (dims, ((), ())), preferred_element_type=F32)
    return d(b1) + (d(b2) + d(b3))


def _gdn_chunks(qraws, kraws, vs, braws, araws, alogs, dtbs):
    nv = len(vs)
    row = _iota2((CHUNK, CHUNK), 0)
    col = _iota2((CHUNK, CHUNK), 1)
    strict = row > col
    triu = (row <= col).astype(F32)
    tril = (row >= col).astype(F32)
    ones = jnp.ones((CHUNK, CHUNK), F32)
    keys = []
    for qraw, kraw in zip(qraws, kraws):
        rq = lax.rsqrt(jnp.sum(qraw * qraw, axis=-1, keepdims=True) + EPS)
        rk = lax.rsqrt(jnp.sum(kraw * kraw, axis=-1, keepdims=True) + EPS)
        qn = qraw * rq
        keys.append(dict(rq=rq, rk=rk, qn=qn, qh=qn * (GDN_DK ** -0.5), kh=kraw * rk))
    kks = [_dg(kd["kh"], kd["kh"], NT) for kd in keys]
    cs = []
    for h in range(nv):
        c = dict(keys[h // 2])
        c.update(v=vs[h], kk=kks[h // 2], strict=strict, triu=triu)
        c["beta"] = _sigmoid(braws[h])
        c["ea"] = jnp.exp(alogs[h])
        c["xs"] = araws[h] + dtbs[h]
        c["g"] = -c["ea"] * _softplus(c["xs"])
        cs.append(c)
    gbs = [jnp.broadcast_to(c["g"], (CHUNK, LANES)) for c in cs]
    cums = [_dot_sel(tril, gb) for gb in gbs]
    cum_js = [_dot_sel(ones, gb[:, :CHUNK] * triu) for gb in gbs]
    for c, cum, cum_j in zip(cs, cums, cum_js):
        diff = jnp.where(strict, cum[:, :CHUNK] - cum_j, 0.0)
        c["dm"] = jnp.where(strict, jnp.exp(diff), 0.0)
        c["a"] = (c["beta"] * c["dm"]) * c["kk"]
        c_last = cum[CHUNK - 1:CHUNK, :]
        c["e"] = jnp.exp(cum)
        c["f"] = jnp.exp(c_last - cum)
        c["dec"] = jnp.exp(c_last)
        c["rv"] = c["beta"] * c["v"]
        c["rk_rhs"] = (c["beta"] * c["e"]) * c["kh"]
        c["ke"] = c["kh"] * c["f"]
    return cs


def _unit_lower_inverses(mats):
    eye = (_iota2((CHUNK, CHUNK), 0) == _iota2((CHUNK, CHUNK), 1)).astype(F32)
    ts = [eye - a for a in mats]
    pws = list(mats)
    for _ in range(5):
        pws = [_dot3(pw, pw) for pw in pws]
        ts = [t + _dot3(t, pw) for t, pw in zip(ts, pws)]
    return ts


GDN_HB = 16


def _gdn_specs(chunk_of):
    hb = GDN_HB
    kw = hb // 2 * GDN_DK
    vw = hb * GDN_DV
    qs = pl.BlockSpec((CHUNK, kw), lambda g, n: (chunk_of(n), g))
    ks = pl.BlockSpec((CHUNK, kw), lambda g, n: (chunk_of(n), GDN_QKW // kw + g))
    vs = pl.BlockSpec((CHUNK, vw), lambda g, n: (chunk_of(n), 2 * GDN_QKW // vw + g))
    zs = pl.BlockSpec((CHUNK, vw), lambda g, n: (chunk_of(n), GDN_CONV_W // vw + g))
    assert hb == GDN_V_HEADS and (GDN_CONV_W + GDN_VW) % LANES == 0
    gates = pl.BlockSpec((CHUNK, LANES), lambda g, n: (chunk_of(n), (GDN_CONV_W + GDN_VW) // LANES))
    one = pl.BlockSpec((hb, 1, 1), lambda g, n: (g, 0, 0))
    ng = pl.BlockSpec((1, GDN_DV), lambda g, n: (0, 0))
    hd = pl.BlockSpec((CHUNK, vw), lambda g, n: (chunk_of(n), g))
    return qs, ks, vs, zs, gates, one, ng, hd


def _gate_columns(gates):
    return ([gates[:, h:h + 1] for h in range(GDN_V_HEADS)],
            [gates[:, GDN_V_HEADS + h:GDN_V_HEADS + h + 1] for h in range(GDN_V_HEADS)])


def _gdn_fwd(qkv, p, alog, dtb, ng, *, name):
    t = qkv.shape[0]
    nc = t // CHUNK
    nh = GDN_V_HEADS

    def body(q_ref, k_ref, v_ref, z_ref, gates_ref, alog_ref, dtb_ref, ng_ref, og_ref, s_ref, t_ref, st):
        @pl.when(pl.program_id(1) == 0)
        def _():
            st[...] = jnp.zeros_like(st)

        hs = range(GDN_HB)
        kqs = [slice(j * GDN_DK, (j + 1) * GDN_DK) for j in range(GDN_HB // 2)]
        vsl = [slice(h * GDN_DV, (h + 1) * GDN_DV) for h in hs]
        braws, araws = _gate_columns(gates_ref[...])
        cs = _gdn_chunks([q_ref[:, s] for s in kqs], [k_ref[:, s] for s in kqs], [v_ref[:, s] for s in vsl],
                         braws, araws, [alog_ref[h] for h in hs], [dtb_ref[h] for h in hs])
        tms = _unit_lower_inverses([c["a"] for c in cs])
        s0 = [st[h] for h in hs]
        wv = [_dot3(tms[h], cs[h]["rv"]) for h in hs]
        wk = [_dot3(tms[h], cs[h]["rk_rhs"]) for h in hs]
        u = [wv[h] - _dg(wk[h], s0[h], NN) for h in hs]
        s1 = [cs[h]["dec"] * s0[h] + _dg(cs[h]["ke"], u[h], TN) for h in hs]
        o = [_dg(cs[h]["qh"], s1[h], NN) for h in hs]
        for h in hs:
            t_ref[h, 0] = tms[h]
            st[h] = s1[h]
            s_ref[h, 0] = s1[h]
            rn = lax.rsqrt(jnp.mean(o[h] * o[h], axis=-1, keepdims=True) + EPS)
            zv = z_ref[:, vsl[h]]
            og_ref[:, vsl[h]] = (((o[h] * rn) * ng_ref[...]) * (zv * _sigmoid(zv))).astype(BF16)

    qs, ks, vs, zs, gates, one, ngs, hd = _gdn_specs(lambda n: n)
    return pl.pallas_call(
        body, grid=(nh // GDN_HB, nc),
        in_specs=[qs, ks, vs, zs, gates, one, one, ngs],
        out_specs=[hd,
                   pl.BlockSpec((GDN_HB, 1, GDN_DK, GDN_DV), lambda g, n: (g, n, 0, 0)),
                   pl.BlockSpec((GDN_HB, 1, CHUNK, CHUNK), lambda g, n: (g, n, 0, 0))],
        out_shape=[jax.ShapeDtypeStruct((t, GDN_VW), BF16), jax.ShapeDtypeStruct((nh, nc, GDN_DK, GDN_DV), F32),
                   jax.ShapeDtypeStruct((nh, nc, CHUNK, CHUNK), F32)],
        scratch_shapes=[pltpu.VMEM((GDN_HB, GDN_DK, GDN_DV), F32)],
        compiler_params=_cp("parallel", "arbitrary"), name=name)(qkv, qkv, qkv, p, p, alog, dtb, ng)


def _gdn_bwd(qkv, p, alog, dtb, ng, dog, sall, tall, *, name):
    t = qkv.shape[0]
    nc = t // CHUNK
    nh = GDN_V_HEADS

    def body(q_ref, k_ref, v_ref, z_ref, gates_ref, alog_ref, dtb_ref, ng_ref, dog_ref, s1_ref, s0_ref, t_ref,
             dq_ref, dk_ref, dv_ref, dz_ref, dgates_ref, dalog_ref, ddtb_ref, dng_ref, gc):
        grp = pl.program_id(0)
        i = pl.program_id(1)

        @pl.when(i == 0)
        def _():
            gc[...] = jnp.zeros_like(gc)
            dalog_ref[...] = jnp.zeros_like(dalog_ref)
            ddtb_ref[...] = jnp.zeros_like(ddtb_ref)

        @pl.when((i == 0) & (grp == 0))
        def _():
            dng_ref[...] = jnp.zeros_like(dng_ref)

        has_prev = (i < nc - 1).astype(F32)
        ngv = ng_ref[...]
        ones = jnp.ones((CHUNK, LANES), F32)
        hs = range(GDN_HB)
        kqs = [slice(j * GDN_DK, (j + 1) * GDN_DK) for j in range(GDN_HB // 2)]
        vsl = [slice(h * GDN_DV, (h + 1) * GDN_DV) for h in hs]
        braws, araws = _gate_columns(gates_ref[...])
        cs = _gdn_chunks([q_ref[:, s] for s in kqs], [k_ref[:, s] for s in kqs], [v_ref[:, s] for s in vsl],
                         braws, araws, [alog_ref[h] for h in hs], [dtb_ref[h] for h in hs])
        tms = [t_ref[h, 0] for h in hs]
        s1 = [s1_ref[h, 0] for h in hs]
        s0 = [s0_ref[h, 0] * has_prev for h in hs]
        wv = [_dot3(tms[h], cs[h]["rv"]) for h in hs]
        wk = [_dot3(tms[h], cs[h]["rk_rhs"]) for h in hs]
        u = [wv[h] - _dg(wk[h], s0[h], NN) for h in hs]
        o = [_dg(cs[h]["qh"], s1[h], NN) for h in hs]
        dng = jnp.zeros((1, GDN_DV), F32)
        do = []
        for h in hs:
            zv = z_ref[:, vsl[h]]
            dogv = dog_ref[:, vsl[h]]
            rn = lax.rsqrt(jnp.mean(o[h] * o[h], axis=-1, keepdims=True) + EPS)
            zo = o[h] * rn
            sg = _sigmoid(zv)
            sl = zv * sg
            dng = dng + jnp.sum(dogv * zo * sl, axis=0, keepdims=True)
            dz_ref[:, vsl[h]] = (dogv * (zo * ngv) * (sg * (1.0 + zv * (1.0 - sg)))).astype(BF16)
            dzo = dogv * sl * ngv
            do.append(rn * (dzo - zo * jnp.mean(dzo * zo, axis=-1, keepdims=True)))
        dng_ref[...] += dng
        g_tot = [gc[h] + _dg(cs[h]["qh"], do[h], TN) for h in hs]
        dqh = [_dg(do[h], s1[h], NT) for h in hs]
        dke = [_dg(u[h], g_tot[h], NT) for h in hs]
        du = [_dg(cs[h]["ke"], g_tot[h], NN) for h in hs]
        gnew = [cs[h]["dec"] * g_tot[h] - _dg(wk[h], du[h], TN) for h in hs]
        dwk = [-_dg(du[h], s0[h], NT) for h in hs]
        drv = [_dot3(tms[h], du[h], TN) for h in hs]
        drk = [_dot3(tms[h], dwk[h], TN) for h in hs]
        da = [jnp.where(cs[h]["strict"], -(_dot3(drv[h], wv[h], NT) + _dot3(drk[h], wk[h], NT)), 0.0) for h in hs]
        mx = [da[h] * cs[h]["dm"] * cs[h]["kk"] for h in hs]
        aa = [mx[h] * cs[h]["beta"] for h in hs]
        colsum = [_dot_sel(ones, aa[h], TN, sel_first=False)[:, 0:1] for h in hs]
        bm = [(da[h] * cs[h]["beta"]) * cs[h]["dm"] for h in hs]
        dkh = [_dg(bm[h], cs[h]["kh"], NN) + _dg(bm[h], cs[h]["kh"], TN) for h in hs]
        dcum, dcl, dbeta = [], [], []
        for h in hs:
            c = cs[h]
            beta, kh, e, f, dec, ke = c["beta"], c["kh"], c["e"], c["f"], c["dec"], c["ke"]
            gc[h] = gnew[h]
            ddec = jnp.sum(jnp.sum(g_tot[h] * s0[h], axis=1, keepdims=True), axis=0, keepdims=True)
            dv_ref[:, vsl[h]] = beta * drv[h]
            db = jnp.sum(mx[h], axis=1, keepdims=True) + jnp.sum(drv[h] * c["v"], axis=1, keepdims=True)
            dbeta.append(db + jnp.sum(drk[h] * (e * kh), axis=1, keepdims=True))
            dkh[h] = dkh[h] + (beta * e) * drk[h] + f * dke[h]
            ef = jnp.sum(dke[h] * ke, axis=1, keepdims=True)
            dcum.append(jnp.sum(aa[h], axis=1, keepdims=True) - colsum[h] + jnp.sum(drk[h] * c["rk_rhs"], axis=1, keepdims=True) - ef)
            dcl.append(jnp.sum(ef, axis=0, keepdims=True) + ddec * dec[:, 0:1])
        dg = [_dot_sel(cs[h]["triu"], jnp.broadcast_to(dcum[h], (CHUNK, LANES)))[:, 0:1] + dcl[h] for h in hs]
        lane = _iota2((CHUNK, LANES), 1)
        dgates = jnp.zeros((CHUNK, LANES), F32)
        for h in hs:
            c = cs[h]
            beta = c["beta"]
            daraw = dg[h] * (-c["ea"]) * _sigmoid(c["xs"])
            dgates = jnp.where(lane == h, dbeta[h] * beta * (1.0 - beta), dgates)
            dgates = jnp.where(lane == GDN_V_HEADS + h, daraw, dgates)
            dalog_ref[h] += jnp.sum(dg[h] * c["g"], axis=0, keepdims=True)
            ddtb_ref[h] += jnp.sum(daraw, axis=0, keepdims=True)
        dgates_ref[...] = dgates.astype(BF16)
        for j, sl in enumerate(kqs):
            c = cs[2 * j]
            dn = (dqh[2 * j] + dqh[2 * j + 1]) * (GDN_DK ** -0.5)
            dks = dkh[2 * j] + dkh[2 * j + 1]
            dq_ref[:, sl] = c["rq"] * (dn - c["qn"] * jnp.sum(dn * c["qn"], axis=-1, keepdims=True))
            dk_ref[:, sl] = c["rk"] * (dks - c["kh"] * jnp.sum(dks * c["kh"], axis=-1, keepdims=True))

    rev = lambda n: nc - 1 - n
    qs, ks, vs, zs, gates, one, ngs, hd = _gdn_specs(rev)
    s1s = pl.BlockSpec((GDN_HB, 1, GDN_DK, GDN_DV), lambda g, n: (g, rev(n), 0, 0))
    s0s = pl.BlockSpec((GDN_HB, 1, GDN_DK, GDN_DV), lambda g, n: (g, jnp.maximum(rev(n) - 1, 0), 0, 0))
    ts = pl.BlockSpec((GDN_HB, 1, CHUNK, CHUNK), lambda g, n: (g, rev(n), 0, 0))
    dgs = pl.BlockSpec((CHUNK, LANES), lambda g, n: (rev(n), 0))
    big = jax.ShapeDtypeStruct((t, GDN_VW), F32)
    keyw = jax.ShapeDtypeStruct((t, GDN_QKW), F32)
    ones_s = jax.ShapeDtypeStruct((nh, 1, 1), F32)
    return pl.pallas_call(
        body, grid=(nh // GDN_HB, nc),
        in_specs=[qs, ks, vs, zs, gates, one, one, ngs, hd, s1s, s0s, ts],
        out_specs=[qs, qs, hd, hd, dgs, one, one, ngs],
        out_shape=[keyw, keyw, big, jax.ShapeDtypeStruct((t, GDN_VW), BF16), jax.ShapeDtypeStruct((t, LANES), BF16),
                   ones_s, ones_s, jax.ShapeDtypeStruct((1, GDN_DV), F32)],
        scratch_shapes=[pltpu.VMEM((GDN_HB, GDN_DK, GDN_DV), F32)],
        compiler_params=_cp("arbitrary", "arbitrary"), name=name)(qkv, qkv, qkv, p, p, alog, dtb, ng, dog, sall, sall, tall)


def _gdn_layer_fwd(h, w, tag):
    p = _mm(h, w["w_in"], name=tag + "_in")
    qkv = _gdn_conv_fwd(p, w["conv_w"], name=tag + "_conv")
    og, sall, tall = _gdn_fwd(qkv, p, w["a_log"], w["dt_bias"], w["norm_g"], name=tag + "_scan")
    y = _mm(og, w["w_out"], name=tag + "_out")
    return y, (p, qkv, og, sall, tall)


def _gdn_layer_bwd(dy, h, saved, w, tag, sink=None):
    p, qkv, og, sall, tall = saved
    t = h.shape[0]
    dog = _mm(dy, w["w_out"], tb=True, name=tag + "_dog")
    dw_out, gf = _dw_out(og, dy, sink, tag + "_dwout")
    dq, dk, dv, dz, dgates, dalog, ddtb, dng = _gdn_bwd(
        qkv, p, w["a_log"], w["dt_bias"], w["norm_g"], dog, sall, tall, name=tag + "_scanb")
    dpre_q, dcw_q = _gdn_conv_bwd(dq, p, w["conv_w"], 0, name=tag + "_convb_q")
    dpre_k, dcw_k = _gdn_conv_bwd(dk, p, w["conv_w"], GDN_QKW, name=tag + "_convb_k")
    dpre_v, dcw_v = _gdn_conv_bwd(dv, p, w["conv_w"], 2 * GDN_QKW, name=tag + "_convb_v")
    dconv_w = jnp.concatenate([dcw_q, dcw_k, dcw_v], axis=1)
    dp = jnp.concatenate([dpre_q, dpre_k, dpre_v, dz, dgates,
                          jnp.zeros((t, GDN_IN_PAD - GDN_CONV_W - GDN_VW - LANES), BF16)], axis=1)
    dw_in = _mm(h, dp, ta=True, name=tag + "_dwin")
    dh = _mm(dp, w["w_in"], tb=True, out_dtype=BF16, name=tag + "_dh")
    grads = dict(w_in=dw_in[:, :GDN_IN], conv_w=dconv_w, a_log=dalog[:, 0, 0], dt_bias=ddtb[:, 0, 0], norm_g=dng[0], w_out=dw_out)
    return dh, grads, gf


MESH_ID = pl.DeviceIdType.MESH
FLAT_W = 1024
FLAT_ROWS = 13056
FLAT_TILE = 384


def _exchange(name, ins, out_shapes, plan, n_remote, n_local):
    def body(*refs):
        in_refs = refs[:len(ins)]
        out_refs = refs[len(ins):len(ins) + len(out_shapes)]
        ssem, rsem, lsem = refs[len(ins) + len(out_shapes):]
        x, y, c = lax.axis_index("x"), lax.axis_index("y"), lax.axis_index("c")
        stages, local_copies = plan(x, y, c, in_refs, out_refs)
        assert sum(len(s) for s in stages) == n_remote and len(local_copies) == n_local
        locs = [pltpu.make_async_copy(s, d, lsem.at[i]) for i, (s, d) in enumerate(local_copies)]
        for cp in locs:
            cp.start()
        sent = []
        k = 0
        for stage in stages:
            arrivals = []
            for src, dst, peer, landing in stage:
                cp = pltpu.make_async_remote_copy(src_ref=src, dst_ref=dst, send_sem=ssem.at[k], recv_sem=rsem.at[k],
                                                  device_id=peer, device_id_type=MESH_ID)
                cp.start()
                sent.append(cp)
                arrivals.append(pltpu.make_async_remote_copy(src_ref=src, dst_ref=landing, send_sem=ssem.at[k],
                                                             recv_sem=rsem.at[k], device_id=peer, device_id_type=MESH_ID))
                k += 1
            for cp in arrivals:
                cp.wait_recv()
        for cp in sent:
            cp.wait_send()
        for cp in locs:
            cp.wait()

    hbm = pl.BlockSpec(memory_space=pl.ANY)
    return pl.pallas_call(
        body, in_specs=[hbm] * len(ins), out_specs=[hbm] * len(out_shapes), out_shape=out_shapes,
        scratch_shapes=[pltpu.SemaphoreType.DMA((n_remote,)), pltpu.SemaphoreType.DMA((n_remote,)),
                        pltpu.SemaphoreType.DMA((max(n_local, 1),))],
        name=name)(*ins)


def _other_chips(x, y):
    return [(1 - x, y), (x, 1 - y), (1 - x, 1 - y)]


def _all8_gather(a, *, name):
    def plan(x, y, c, ins, outs):
        (src,), (dst,) = ins, outs
        me = 4 * x + 2 * y + c
        stage = []
        for fx, fy, fc in [(0, 0, 1), (0, 1, 0), (0, 1, 1), (1, 0, 0), (1, 0, 1), (1, 1, 0), (1, 1, 1)]:
            px, py, pc = (1 - x if fx else x), (1 - y if fy else y), (1 - c if fc else c)
            stage.append((src, dst.at[me], (px, py, pc), dst.at[4 * px + 2 * py + pc]))
        return [stage], [(src, dst.at[me])]

    return _exchange(name, [a], [jax.ShapeDtypeStruct((8,) + a.shape, a.dtype)], plan, 7, 1)[0]


def _chip_gather(flat, *, name):
    rows = flat.shape[0]
    half = rows // 2

    def plan(x, y, c, ins, outs):
        (src,), (dst,) = ins, outs
        me = 2 * x + y
        mine = pl.ds(c * half, half)
        theirs = pl.ds((1 - c) * half, half)
        ici = [(src.at[mine], dst.at[me, mine], (px, py, c), dst.at[2 * px + py, mine]) for px, py in _other_chips(x, y)]
        d2d = [(dst.at[2 * px + py, mine], dst.at[2 * px + py, mine], (x, y, 1 - c), dst.at[2 * px + py, theirs])
               for px, py in _other_chips(x, y)]
        return [ici, d2d], []

    return _exchange(name, [flat], [jax.ShapeDtypeStruct((4,) + flat.shape, flat.dtype)], plan, 6, 0)[0]


def _add_sibling(gf, buf_a, core, *, name):
    _, rows, w = gf.shape
    half = rows // 2
    nb = half // FLAT_TILE

    def body(c_ref, g_ref, a_ref, o_ref):
        o_ref[...] = (g_ref[...].astype(F32) + a_ref[...].astype(F32)).astype(BF16)

    blk = (1, FLAT_TILE, w)
    return pl.pallas_call(
        body,
        grid_spec=pltpu.PrefetchScalarGridSpec(
            num_scalar_prefetch=1, grid=(4, nb),
            in_specs=[pl.BlockSpec(blk, lambda s, i, c_ref: (s, c_ref[0] * nb + i, 0)), pl.BlockSpec(blk, lambda s, i, c_ref: (s, i, 0))],
            out_specs=pl.BlockSpec(blk, lambda s, i, c_ref: (s, i, 0))),
        out_shape=jax.ShapeDtypeStruct((4, half, w), BF16), compiler_params=_cp("parallel", "parallel"), name=name)(core, gf, buf_a)


def _sum_chips(hsum, buf_b, chip, *, name):
    _, half, w = hsum.shape
    nb = half // FLAT_TILE

    def body(c_ref, h_ref, b0_ref, b1_ref, b2_ref, b3_ref, o_ref):
        me = c_ref[0]
        own = h_ref[0].astype(F32)
        acc = None
        for j, b_ref in enumerate((b0_ref, b1_ref, b2_ref, b3_ref)):
            term = jnp.where(me == j, own, b_ref[0].astype(F32))
            acc = term if acc is None else acc + term
        o_ref[...] = acc

    blk = (1, FLAT_TILE, w)

    def other(j):
        return pl.BlockSpec(blk, lambda i, c_ref: (jnp.where(c_ref[0] == j, (j + 1) % 4, j), i, 0))

    return pl.pallas_call(
        body,
        grid_spec=pltpu.PrefetchScalarGridSpec(
            num_scalar_prefetch=1, grid=(nb,),
            in_specs=[pl.BlockSpec(blk, lambda i, c_ref: (c_ref[0], i, 0))] + [other(j) for j in range(4)],
            out_specs=pl.BlockSpec((FLAT_TILE, w), lambda i, c_ref: (i, 0))),
        out_shape=jax.ShapeDtypeStruct((half, w), F32), compiler_params=_cp("parallel"), name=name)(chip, hsum, buf_b, buf_b, buf_b, buf_b)


def _sum_slots(buf, *, name):
    n, rows, w = buf.shape
    tr = _pick(rows, (FLAT_TILE, 8))

    def body(b_ref, o_ref):
        acc = b_ref[0]
        for s in range(1, n):
            acc = acc + b_ref[s]
        o_ref[...] = acc

    return pl.pallas_call(body, grid=(rows // tr,), in_specs=[pl.BlockSpec((n, tr, w), lambda i: (0, i, 0))],
                          out_specs=pl.BlockSpec((tr, w), lambda i: (i, 0)), out_shape=jax.ShapeDtypeStruct((rows, w), F32),
                          compiler_params=_cp("parallel"), name=name)(buf)


def _reduce_scatter(gf, core, chip, *, tag):
    _, rows, w = gf.shape
    half = rows // 2

    def plan_a(x, y, c, ins, outs):
        (src,), (dst,) = ins, outs
        return [[(src.at[:, pl.ds((1 - c) * half, half)], dst, (x, y, 1 - c), dst)]], []

    buf_a = _exchange(tag + "_sibling", [gf], [jax.ShapeDtypeStruct((4, half, w), gf.dtype)], plan_a, 1, 0)[0]
    hsum = _add_sibling(gf, buf_a, core, name=tag + "_add_sibling")

    def plan_b(x, y, c, ins, outs):
        (src,), (dst,) = ins, outs
        me = 2 * x + y
        stage = [(src.at[2 * px + py], dst.at[me], (px, py, c), dst.at[2 * px + py]) for px, py in _other_chips(x, y)]
        return [stage], []

    buf_b = _exchange(tag + "_chips", [hsum], [jax.ShapeDtypeStruct((4, half, w), BF16)], plan_b, 3, 0)[0]
    mine = _sum_chips(hsum, buf_b, chip, name=tag + "_sum_chips")

    def plan_c(x, y, c, ins, outs):
        (src,), (dst,) = ins, outs
        return [[(src, dst, (x, y, 1 - c), dst)]], []

    theirs = _exchange(tag + "_halves", [mine], [jax.ShapeDtypeStruct((half, w), F32)], plan_c, 1, 0)[0]
    first = core[0] == 0
    return jnp.concatenate([jnp.where(first, mine, theirs), jnp.where(first, theirs, mine)], axis=0)


WEIGHTS = ["ada_w", "ada_b", "norm_pre_g", "norm_post_g", "gla_w_in", "gla_w_gate_up", "gla_b_gate", "gla_head_g",
           "gla_w_out", "mla_w_in", "mla_q_norm_g", "mla_w_uq", "mla_kv_norm_g", "mla_w_ukv", "mla_w_out", "gdn_w_in",
           "gdn_conv_w", "gdn_a_log", "gdn_dt_bias", "gdn_norm_g", "gdn_w_out", "mlp_w_up", "mlp_w_down"]
PACK_BF16 = [("mlp_w_up", 2), ("mlp_w_down", 1), ("gdn_w_out", 1), ("gla_w_out", 1), ("mla_w_out", 1),
             ("gla_w_in", 2), ("mla_w_in", 1), ("mla_w_uq", 2), ("mla_w_ukv", 2), ("gdn_w_in", 2)]
N_DIRECT = 5
PACK_F32 = [("norm_pre_g", 2), ("norm_post_g", 2), ("gla_w_gate_up", 2), ("gla_b_gate", 1), ("gla_head_g", 1), ("gdn_conv_w", 2)]
REPLICATED_SMALL = ["mla_q_norm_g", "mla_kv_norm_g", "gdn_a_log", "gdn_dt_bias", "gdn_norm_g"]
MIXERS = ["gla", "mla", "gdn"]


def _silu_rows(a, *, name):
    def body(a_ref, o_ref):
        v = a_ref[...]
        o_ref[...] = v * _sigmoid(v)

    return pl.pallas_call(body, out_shape=jax.ShapeDtypeStruct(a.shape, F32), name=name)(a)


SMALL_ROWS = 16


def _piece_rows(size, mult):
    assert size % FLAT_W == 0
    return -(-(size // FLAT_W) // mult) * mult


def _to_rows(a, lead, mult):
    n = math.prod(a.shape[len(lead):])
    r = a.reshape(lead + (n // FLAT_W, FLAT_W))
    extra = _piece_rows(n, mult) - n // FLAT_W
    return jnp.pad(r, [(0, 0)] * len(lead) + [(0, extra), (0, 0)]) if extra else r


def _small_to_rows(parts, lead):
    flat = jnp.concatenate([p.reshape(lead + (-1,)) for p in parts], axis=-1)
    pad = SMALL_ROWS * FLAT_W - flat.shape[-1]
    return jnp.pad(flat, [(0, 0)] * len(lead) + [(0, pad)]).reshape(lead + (SMALL_ROWS, FLAT_W))


def _small_from_rows(rows, shards, lead):
    flat = rows.reshape(lead + (-1,))
    out, off = {}, 0
    for n, _ in PACK_F32:
        out[n] = flat[..., off:off + shards[n].size].reshape(lead + shards[n].shape)
        off += shards[n].size
    return out


def _pack_weights(shards):
    parts = [_to_rows(shards[n].astype(BF16), (), 16) for n, _ in PACK_BF16]
    small = _small_to_rows([shards[n] for n, _ in PACK_F32], ())
    parts.append(lax.bitcast_convert_type(small, BF16).reshape(2 * SMALL_ROWS, FLAT_W))
    flat = jnp.concatenate(parts, axis=0)
    return jnp.pad(flat, ((0, FLAT_ROWS - flat.shape[0]), (0, 0)))


def _unpack_weights(gathered, shards):
    full, off = {}, 0
    for n, ax in PACK_BF16:
        size = shards[n].size
        seg = gathered[:, off:off + size // FLAT_W].reshape((4,) + shards[n].shape)
        full[n] = jnp.concatenate([seg[j] for j in range(4)], axis=ax)
        off += _piece_rows(size, 16)
    small = lax.bitcast_convert_type(gathered[:, off:off + 2 * SMALL_ROWS].reshape(4, SMALL_ROWS, FLAT_W, 2), F32)
    for (n, ax), seg in zip(PACK_F32, _small_from_rows(small, shards, (4,)).values()):
        full[n] = jnp.concatenate([seg[j] for j in range(4)], axis=ax)
    return full


def _grad_layout(shards):
    layout, off = {}, 0
    for n, _ in PACK_BF16:
        layout[n] = (off, shards[n].size // shards[n].shape[0] // FLAT_W)
        off += _piece_rows(shards[n].size, 16)
    layout["small"] = (off, SMALL_ROWS)
    return layout


def _pack_grads(gf, grads, layout):
    by_chip = lambda g, ax: jnp.stack(jnp.split(g.astype(gf.dtype), 4, axis=ax - 1))
    parts = []
    for n, ax in PACK_BF16[N_DIRECT:]:
        rows = sum(g.size for g in grads[n]) // (4 * FLAT_W)
        parts += [by_chip(g, ax).reshape(4, -1, FLAT_W) for g in grads[n]]
        if _piece_rows(rows * FLAT_W, 16) > rows:
            parts.append(jnp.zeros((4, _piece_rows(rows * FLAT_W, 16) - rows, FLAT_W), gf.dtype))
    parts.append(_small_to_rows([jnp.stack([by_chip(g, ax) for g in grads[n]], axis=1) for n, ax in PACK_F32], (4,)))
    first = layout[PACK_BF16[N_DIRECT][0]][0]
    rest = jnp.concatenate(parts, axis=1)
    assert first + rest.shape[1] == layout["small"][0] + SMALL_ROWS
    return lax.dynamic_update_slice(gf, rest, (0, first, 0))


def _unpack_grads(reduced, shards):
    out, off = {}, 0
    for n, _ in PACK_BF16:
        size = shards[n].size
        out[n] = reduced[off:off + size // FLAT_W].reshape(shards[n].shape)
        off += _piece_rows(size, 16)
    out.update(_small_from_rows(reduced[off:off + SMALL_ROWS], shards, ()))
    return out


def _mixer_weights(kind, j, full, rep):
    if kind == "gla":
        return dict(w_in=jnp.pad(full["gla_w_in"][j], ((0, 0), (0, GLA_IN_PAD - GLA_IN))),
                    wg=jnp.pad(full["gla_w_gate_up"][j], ((0, LANES - GLA_RANK), (0, 0))),
                    bg=full["gla_b_gate"][j][None], hg=full["gla_head_g"][j][None], w_out=full["gla_w_out"][j])
    if kind == "mla":
        return dict(w_in=jnp.pad(full["mla_w_in"][j], ((0, 0), (0, MLA_IN_PAD - MLA_IN))), q_norm_g=rep["mla_q_norm_g"][j][None],
                    w_uq=full["mla_w_uq"][j], kv_norm_g=rep["mla_kv_norm_g"][j][None], w_ukv=full["mla_w_ukv"][j],
                    w_out=full["mla_w_out"][j])
    return dict(w_in=jnp.pad(full["gdn_w_in"][j], ((0, 0), (0, GDN_IN_PAD - GDN_IN))), conv_w=full["gdn_conv_w"][j],
                a_log=rep["gdn_a_log"][j][:, None, None], dt_bias=rep["gdn_dt_bias"][j][:, None, None],
                norm_g=rep["gdn_norm_g"][j][None], w_out=full["gdn_w_out"][j])


def _layer_fwd(xin, mod, gains, kind, mw, w_up, w_down, pos, tag):
    sh_m, sc_m, gt_m, sh_f, sc_f, gt_f = mod
    pre0, pre1, post0, post1 = gains
    h = _premod_fwd(xin, pre0, sc_m, sh_m, name=tag + "_pre0")
    if kind == "gla":
        y, saved = _gla_layer_fwd(h, mw, tag + "_gla")
    elif kind == "mla":
        y, saved = _mla_layer_fwd(h, pos, mw, tag + "_mla")
    else:
        y, saved = _gdn_layer_fwd(h, mw, tag + "_gdn")
    x1 = _postres_fwd(xin, y, post0, gt_m, name=tag + "_post0")
    h2 = _premod_fwd(x1, pre1, sc_f, sh_f, name=tag + "_pre1")
    act = _mm(h2, w_up, out_dtype=BF16, epi="relu2", name=tag + "_up")
    y2 = _mm(act, w_down, name=tag + "_down")
    x2 = _postres_fwd(x1, y2, post1, gt_f, name=tag + "_post1")
    return x2, (xin, h, y, saved, x1, h2, act, y2)


def _layer_bwd(g2, kept, mod, gains, kind, mw, w_up, w_down, tag, gf, rows):
    xin, h, y, saved, x1, h2, act, y2 = kept
    sh_m, sc_m, gt_m, sh_f, sc_f, gt_f = mod
    pre0, pre1, post0, post1 = gains
    dy2, dpost1, dgt_f = _postres_bwd(g2, y2, post1, gt_f, name=tag + "_post1_b")
    du = _mm(dy2, w_down, tb=True, out_dtype=BF16, epi="dact", aux=act, name=tag + "_du")
    gf = _mm(act, dy2, ta=True, out_dtype=gf.dtype, into=(gf, rows[1], "rows"), name=tag + "_dwdown")
    gf = _mm(h2, du, ta=True, out_dtype=gf.dtype, into=(gf, rows[0], "cols"), name=tag + "_dwup")
    dh2 = _mm(du, w_up, tb=True, out_dtype=BF16, name=tag + "_dh2")
    g1, dpre1, dsc_f, dsh_f = _premod_bwd(dh2, x1, pre1, sc_f, g2, name=tag + "_pre1_b")
    dy, dpost0, dgt_m = _postres_bwd(g1, y, post0, gt_m, name=tag + "_post0_b")
    mixer_bwd = dict(gla=_gla_layer_bwd, mla=_mla_layer_bwd, gdn=_gdn_layer_bwd)[kind]
    dh, mg, gf = mixer_bwd(dy, h, saved, mw, tag + "_" + kind, sink=(gf, rows[2]))
    g0, dpre0, dsc_m, dsh_m = _premod_bwd(dh, xin, pre0, sc_m, g1, name=tag + "_pre0_b")
    dmod = jnp.concatenate([dsh_m, dsc_m, dgt_m, dsh_f, dsc_f, dgt_f], axis=1)
    return g0, dmod, jnp.concatenate([dpre0, dpre1], axis=0), jnp.concatenate([dpost0, dpost1], axis=0), mg, gf


def kernel(x, c, positions, ada_w, ada_b, norm_pre_g, norm_post_g, gla_w_in, gla_w_gate_up, gla_b_gate, gla_head_g, gla_w_out, mla_w_in, mla_q_norm_g, mla_w_uq, mla_kv_norm_g, mla_w_ukv, mla_w_out, gdn_w_in, gdn_conv_w, gdn_a_log, gdn_dt_bias, gdn_norm_g, gdn_w_out, mlp_w_up, mlp_w_down, loss_target, m_ada_w, m_ada_b, m_norm_pre_g, m_norm_post_g, m_gla_w_in, m_gla_w_gate_up, m_gla_b_gate, m_gla_head_g, m_gla_w_out, m_mla_w_in, m_mla_q_norm_g, m_mla_w_uq, m_mla_kv_norm_g, m_mla_w_ukv, m_mla_w_out, m_gdn_w_in, m_gdn_conv_w, m_gdn_a_log, m_gdn_dt_bias, m_gdn_norm_g, m_gdn_w_out, m_mlp_w_up, m_mlp_w_down, v_ada_w, v_ada_b, v_norm_pre_g, v_norm_post_g, v_gla_w_in, v_gla_w_gate_up, v_gla_b_gate, v_gla_head_g, v_gla_w_out, v_mla_w_in, v_mla_q_norm_g, v_mla_w_uq, v_mla_kv_norm_g, v_mla_w_ukv, v_mla_w_out, v_gdn_w_in, v_gdn_conv_w, v_gdn_a_log, v_gdn_dt_bias, v_gdn_norm_g, v_gdn_w_out, v_mlp_w_up, v_mlp_w_down):
    w = dict(ada_w=ada_w, ada_b=ada_b, norm_pre_g=norm_pre_g, norm_post_g=norm_post_g, gla_w_in=gla_w_in,
             gla_w_gate_up=gla_w_gate_up, gla_b_gate=gla_b_gate, gla_head_g=gla_head_g, gla_w_out=gla_w_out, mla_w_in=mla_w_in,
             mla_q_norm_g=mla_q_norm_g, mla_w_uq=mla_w_uq, mla_kv_norm_g=mla_kv_norm_g, mla_w_ukv=mla_w_ukv, mla_w_out=mla_w_out,
             gdn_w_in=gdn_w_in, gdn_conv_w=gdn_conv_w, gdn_a_log=gdn_a_log, gdn_dt_bias=gdn_dt_bias, gdn_norm_g=gdn_norm_g,
             gdn_w_out=gdn_w_out, mlp_w_up=mlp_w_up, mlp_w_down=mlp_w_down)
    m = dict(zip(WEIGHTS, [m_ada_w, m_ada_b, m_norm_pre_g, m_norm_post_g, m_gla_w_in, m_gla_w_gate_up, m_gla_b_gate, m_gla_head_g,
                           m_gla_w_out, m_mla_w_in, m_mla_q_norm_g, m_mla_w_uq, m_mla_kv_norm_g, m_mla_w_ukv, m_mla_w_out,
                           m_gdn_w_in, m_gdn_conv_w, m_gdn_a_log, m_gdn_dt_bias, m_gdn_norm_g, m_gdn_w_out, m_mlp_w_up, m_mlp_w_down]))
    v = dict(zip(WEIGHTS, [v_ada_w, v_ada_b, v_norm_pre_g, v_norm_post_g, v_gla_w_in, v_gla_w_gate_up, v_gla_b_gate, v_gla_head_g,
                           v_gla_w_out, v_mla_w_in, v_mla_q_norm_g, v_mla_w_uq, v_mla_kv_norm_g, v_mla_w_ukv, v_mla_w_out,
                           v_gdn_w_in, v_gdn_conv_w, v_gdn_a_log, v_gdn_dt_bias, v_gdn_norm_g, v_gdn_w_out, v_mlp_w_up, v_mlp_w_down]))
    t = x.shape[1]
    ix, iy, ic = lax.axis_index("x"), lax.axis_index("y"), lax.axis_index("c")
    me = 4 * ix + 2 * iy + ic
    chip = 2 * ix + iy
    ada_cols = ada_w.shape[2]

    packed = _pack_weights(w)
    zero = jnp.zeros((), jnp.int32)
    gathered = lax.dynamic_update_slice(_chip_gather(packed, name="gather_weights"), packed[None], (chip, zero, zero))
    full = _unpack_weights(gathered, w)

    cond8 = _silu_rows(jnp.pad(c, ((0, 7), (0, 0))), name="cond_silu")
    cond16 = jnp.pad(_all8_gather(cond8, name="gather_cond")[:, 0, :], ((0, 8), (0, 0)))
    mod_cols = []
    for layer in range(DEPTH):
        bias = jnp.broadcast_to(lax.dynamic_slice_in_dim(ada_b[layer], chip * ada_cols, ada_cols)[None], (16, ada_cols))
        mod_cols.append(_mm(cond16, ada_w[layer], epi="add", aux=bias, name=f"ada{layer}")[:8])
    mod_all = _all8_gather(jnp.stack(mod_cols).reshape(DEPTH * 8, ada_cols), name="gather_mod")
    mod = jnp.concatenate([lax.dynamic_slice_in_dim(mod_all[2 * j].reshape(DEPTH, 8, ada_cols), me, 1, axis=1)[:, 0]
                           for j in range(4)], axis=1)

    def layer_args(layer):
        kind, j = MIXERS[layer % 3], layer // 3
        mods = [mod[layer, i * D_MODEL:(i + 1) * D_MODEL][None] for i in range(N_MOD)]
        gains = (full["norm_pre_g"][layer, 0:1], full["norm_pre_g"][layer, 1:2], full["norm_post_g"][layer, 0:1],
                 full["norm_post_g"][layer, 1:2])
        return kind, j, mods, gains, _mixer_weights(kind, j, full, w)

    xs = x[0]
    kept = []
    for layer in range(DEPTH):
        kind, j, mods, gains, mw = layer_args(layer)
        xs, keep = _layer_fwd(xs, mods, gains, kind, mw, full["mlp_w_up"][layer], full["mlp_w_down"][layer], positions[0], f"l{layer}")
        kept.append(keep)
    loss_row, g = _loss_head(xs, loss_target[0], name="loss_head")
    loss = lax.psum(loss_row[0, 0], ("x", "y", "c"))

    grads = {n: [None] * w[n].shape[0] for n, _ in PACK_BF16[N_DIRECT:] + PACK_F32}
    rep_grads = {}
    dmods = [None] * DEPTH
    layout = _grad_layout(w)
    row_of = lambda n, idx: layout[n][0] + idx * layout[n][1]
    gf = lax.empty((4, FLAT_ROWS, FLAT_W), BF16)
    for layer in reversed(range(DEPTH)):
        kind, j, mods, gains, mw = layer_args(layer)
        rows = (row_of("mlp_w_up", layer), row_of("mlp_w_down", layer), row_of(kind + "_w_out", j))
        g, dmods[layer], dpre, dpost, mg, gf = _layer_bwd(
            g, kept[layer], mods, gains, kind, mw, full["mlp_w_up"][layer], full["mlp_w_down"][layer], f"l{layer}", gf, rows)
        grads["norm_pre_g"][layer], grads["norm_post_g"][layer] = dpre, dpost
        for key, val in mg.items():
            name = kind + "_" + key
            if name in grads:
                grads[name][j] = val
            elif name in REPLICATED_SMALL:
                rep_grads[name] = val[None]

    rep_flat = jnp.concatenate([rep_grads[n].reshape(-1) for n in REPLICATED_SMALL])
    dbuf = jnp.concatenate([jnp.concatenate(dmods, axis=0), jnp.pad(rep_flat, (0, N_MOD * D_MODEL - rep_flat.shape[0]))[None],
                            jnp.zeros((3, N_MOD * D_MODEL), F32)], axis=0)
    dall = _all8_gather(dbuf, name="gather_dmod")
    dsum = _sum_slots(dall, name="sum_dmod")
    out_grads = {"ada_b": dsum[:DEPTH]}
    off = 0
    for n in REPLICATED_SMALL:
        out_grads[n] = dsum[DEPTH, off:off + w[n].size].reshape(w[n].shape)
        off += w[n].size
    dada = []
    for layer in range(DEPTH):
        dm16 = jnp.pad(lax.dynamic_slice_in_dim(dall[:, layer, :], chip * ada_cols, ada_cols, axis=1), ((0, 8), (0, 0)))
        dada.append(_mm(cond16, dm16, ta=True, name=f"dada{layer}"))
    out_grads["ada_w"] = jnp.stack(dada)

    reduced = _reduce_scatter(_pack_grads(gf, grads, layout), ic.reshape(1).astype(jnp.int32),
                              chip.reshape(1).astype(jnp.int32), tag="reduce_grads")
    out_grads.update(_unpack_grads(reduced, w))

    deltas, new_m, new_v = {}, {}, {}
    for n in WEIGHTS:
        deltas[n], new_m[n], new_v[n] = _adamw(w[n], out_grads[n], m[n], v[n], name="adamw_" + n)
    return (loss, g[None], *[out_grads[n] for n in WEIGHTS], *[deltas[n] for n in WEIGHTS],
            *[new_m[n] for n in WEIGHTS], *[new_v[n] for n in WEIGHTS])
```

```python
import functools
import math

import jax
import jax.numpy as jnp
from jax import lax
from jax.experimental import pallas as pl
from jax.experimental.pallas import tpu as pltpu

F32 = jnp.float32
BF16 = jnp.bfloat16

D_MODEL = 1024
DEPTH = 4
CHUNK = 64
EPS = 1e-6
NEG_INF = -1e30
N_MOD = 6

GLA_HEADS, GLA_DK, GLA_DV, GLA_RANK = 4, 128, 256, 16
GLA_KW, GLA_VW = GLA_HEADS * GLA_DK, GLA_HEADS * GLA_DV
GLA_IN = 2 * GLA_KW + 2 * GLA_VW + GLA_RANK
GLA_IN_PAD = 3200

MLA_HEADS, MLA_NOPE, MLA_ROPE, MLA_V = 16, 64, 32, 64
MLA_Q_RANK, MLA_KV_RANK = 384, 256
MLA_IN = MLA_Q_RANK + MLA_KV_RANK + MLA_ROPE
MLA_IN_PAD = 768
ROPE_THETA = 10000.0
MLA_QK = MLA_NOPE + MLA_ROPE
LANES = 128

GDN_K_HEADS, GDN_V_HEADS, GDN_DK, GDN_DV, GDN_CONV = 8, 16, 128, 128, 4
GDN_QKW, GDN_VW = GDN_K_HEADS * GDN_DK, GDN_V_HEADS * GDN_DV
GDN_CONV_W = 2 * GDN_QKW + GDN_VW
GDN_IN = GDN_CONV_W + GDN_VW + 2 * GDN_V_HEADS
GDN_IN_PAD = 6400

ADAM_LR, ADAM_B1, ADAM_B2, ADAM_EPS, ADAM_WD, ADAM_STEP = 0.001, 0.9, 0.999, 1e-08, 0.01, 10

VMEM_LIMIT = 56 * 1024 * 1024

NN = ((1,), (0,))
NT = ((1,), (1,))
TN = ((0,), (0,))


def _cp(*sem):
    return pltpu.CompilerParams(dimension_semantics=sem, vmem_limit_bytes=VMEM_LIMIT)


def _pick(n, cands):
    for c in cands:
        if n % c == 0:
            return c
    return n


def _dg(a, b, dims=NN):
    return lax.dot_general(a.astype(BF16), b.astype(BF16), (dims, ((), ())), preferred_element_type=F32)


def _dot3(a, b, dims=NN):
    ah = a.astype(BF16)
    al = (a - ah.astype(F32)).astype(BF16)
    bh = b.astype(BF16)
    bl = (b - bh.astype(F32)).astype(BF16)
    d = lambda u, v: lax.dot_general(u, v, (dims, ((), ())), preferred_element_type=F32)
    return d(ah, bh) + (d(ah, bl) + d(al, bh))


def _sigmoid(x):
    return 1.0 / (1.0 + jnp.exp(-x))


def _softplus(x):
    return jnp.maximum(x, 0.0) + jnp.log(1.0 + jnp.exp(-jnp.abs(x)))


def _iota2(shape, dim):
    return lax.broadcasted_iota(jnp.int32, shape, dim)


def _mm(a, b, *, ta=False, tb=False, out_dtype=F32, epi=None, aux=None, into=None, name):
    m = a.shape[1] if ta else a.shape[0]
    k = a.shape[0] if ta else a.shape[1]
    n = b.shape[0] if tb else b.shape[1]
    assert k == (b.shape[1] if tb else b.shape[0]), (a.shape, b.shape, ta, tb)
    m_tile = m // 4 if into is not None and into[2] == "rows" else m
    tm = _pick(m_tile, (1024, 512, 384, 256, 128))
    tn = _pick(n, (1024, 640, 512, 768, 384, 256, 128))
    tk = _pick(k, (1024, 640, 512, 768, 384, 256, 128))
    nk = k // tk
    dims = ((0 if ta else 1,), (1 if tb else 0,))

    def finish(r, x_ref, o_ref):
        if epi == "relu2":
            r = jnp.square(jnp.maximum(r, 0.0))
        elif epi == "dact":
            r = r * (2.0 * jnp.sqrt(x_ref[...].astype(F32)))
        elif epi == "add":
            r = r + x_ref[...]
        o_ref[...] = r.astype(out_dtype)

    n_in = 2 + (aux is not None) + (into is not None)

    def body(*refs):
        a_ref, b_ref = refs[:2]
        x_ref = refs[2] if aux is not None else None
        o_ref = refs[n_in]
        if nk == 1:
            finish(_dg(a_ref[...], b_ref[...], dims), x_ref, o_ref)
            return
        acc = refs[-1]
        kk = pl.program_id(2)

        @pl.when(kk == 0)
        def _():
            acc[...] = jnp.zeros_like(acc)

        acc[...] += _dg(a_ref[...], b_ref[...], dims)

        @pl.when(kk == nk - 1)
        def _():
            finish(acc[...], x_ref, o_ref)

    a_spec = pl.BlockSpec((tk, tm), lambda i, j, q: (q, i)) if ta else pl.BlockSpec((tm, tk), lambda i, j, q: (i, q))
    b_spec = pl.BlockSpec((tn, tk), lambda i, j, q: (j, q)) if tb else pl.BlockSpec((tk, tn), lambda i, j, q: (q, j))
    o_spec = pl.BlockSpec((tm, tn), lambda i, j, q: (i, j))
    in_specs = [a_spec, b_spec] + ([o_spec] if aux is not None else [])
    args = (a, b) + ((aux,) if aux is not None else ())
    out_shape = jax.ShapeDtypeStruct((m, n), out_dtype)
    aliases = {}
    if into is not None:
        dst, row0, axis = into
        assert dst.dtype == out_dtype and row0 % tm == 0 and tn == FLAT_W and n == (FLAT_W if axis == "rows" else 4 * FLAT_W)
        per = m_tile // tm
        if axis == "rows":
            o_spec = pl.BlockSpec((None, tm, tn), lambda i, j, q: (i // per, row0 // tm + i % per, 0))
        else:
            o_spec = pl.BlockSpec((None, tm, tn), lambda i, j, q: (j, row0 // tm + i, 0))
        in_specs.append(pl.BlockSpec(memory_space=pl.ANY))
        args += (dst,)
        out_shape = jax.ShapeDtypeStruct(dst.shape, dst.dtype)
        aliases = {n_in - 1: 0}
    return pl.pallas_call(
        body, grid=(m // tm, n // tn, nk), in_specs=in_specs, out_specs=o_spec, out_shape=out_shape,
        scratch_shapes=[pltpu.VMEM((tm, tn), F32)] if nk > 1 else [], input_output_aliases=aliases,
        compiler_params=_cp("parallel", "parallel", "arbitrary"), name=name)(*args)


def _row_tile(t):
    return _pick(t, (512, 256, 128, 64, 8))


def _premod_fwd(x, g, sc, sh, *, name):
    t, c = x.shape
    tr = _row_tile(t)

    def body(x_ref, g_ref, sc_ref, sh_ref, h_ref):
        xv = x_ref[...]
        r = lax.rsqrt(jnp.mean(xv * xv, axis=-1, keepdims=True) + EPS)
        h_ref[...] = (((xv * r) * g_ref[...]) * (1.0 + sc_ref[...]) + sh_ref[...]).astype(BF16)

    row = pl.BlockSpec((tr, c), lambda i: (i, 0))
    vec = pl.BlockSpec((1, c), lambda i: (0, 0))
    return pl.pallas_call(body, grid=(t // tr,), in_specs=[row, vec, vec, vec], out_specs=row,
                          out_shape=jax.ShapeDtypeStruct((t, c), BF16), compiler_params=_cp("parallel"), name=name)(x, g, sc, sh)


def _premod_bwd(dh, x, g, sc, gin, *, name):
    t, c = x.shape
    tr = _row_tile(t)

    def body(dh_ref, x_ref, g_ref, sc_ref, gin_ref, gout_ref, dg_ref, dsc_ref, dsh_ref):
        @pl.when(pl.program_id(0) == 0)
        def _():
            dg_ref[...] = jnp.zeros_like(dg_ref)
            dsc_ref[...] = jnp.zeros_like(dsc_ref)
            dsh_ref[...] = jnp.zeros_like(dsh_ref)

        xv = x_ref[...]
        dhv = dh_ref[...].astype(F32)
        gv = g_ref[...]
        one_sc = 1.0 + sc_ref[...]
        r = lax.rsqrt(jnp.mean(xv * xv, axis=-1, keepdims=True) + EPS)
        nv = xv * r
        dsh_ref[...] += jnp.sum(dhv, axis=0, keepdims=True)
        dsc_ref[...] += jnp.sum(dhv * (nv * gv), axis=0, keepdims=True)
        dg_ref[...] += jnp.sum(dhv * nv * one_sc, axis=0, keepdims=True)
        dn = dhv * gv * one_sc
        dx = r * (dn - nv * jnp.mean(dn * nv, axis=-1, keepdims=True))
        gout_ref[...] = gin_ref[...] + dx

    row = pl.BlockSpec((tr, c), lambda i: (i, 0))
    vec = pl.BlockSpec((1, c), lambda i: (0, 0))
    vs = jax.ShapeDtypeStruct((1, c), F32)
    return pl.pallas_call(body, grid=(t // tr,), in_specs=[row, row, vec, vec, row], out_specs=[row, vec, vec, vec],
                          out_shape=[jax.ShapeDtypeStruct((t, c), F32), vs, vs, vs],
                          compiler_params=_cp("arbitrary"), name=name)(dh, x, g, sc, gin)


def _postres_fwd(x, y, g, gt, *, name):
    t, c = x.shape
    tr = _row_tile(t)

    def body(x_ref, y_ref, g_ref, gt_ref, o_ref):
        yv = y_ref[...]
        r = lax.rsqrt(jnp.mean(yv * yv, axis=-1, keepdims=True) + EPS)
        o_ref[...] = x_ref[...] + gt_ref[...] * ((yv * r) * g_ref[...])

    row = pl.BlockSpec((tr, c), lambda i: (i, 0))
    vec = pl.BlockSpec((1, c), lambda i: (0, 0))
    return pl.pallas_call(body, grid=(t // tr,), in_specs=[row, row, vec, vec], out_specs=row,
                          out_shape=jax.ShapeDtypeStruct((t, c), F32), compiler_params=_cp("parallel"), name=name)(x, y, g, gt)


def _postres_bwd(gout, y, g, gt, *, name):
    t, c = y.shape
    tr = _row_tile(t)

    def body(go_ref, y_ref, g_ref, gt_ref, dy_ref, dg_ref, dgt_ref):
        @pl.when(pl.program_id(0) == 0)
        def _():
            dg_ref[...] = jnp.zeros_like(dg_ref)
            dgt_ref[...] = jnp.zeros_like(dgt_ref)

        yv = y_ref[...]
        gov = go_ref[...]
        gv = g_ref[...]
        gtv = gt_ref[...]
        r = lax.rsqrt(jnp.mean(yv * yv, axis=-1, keepdims=True) + EPS)
        z = yv * r
        dgt_ref[...] += jnp.sum(gov * (z * gv), axis=0, keepdims=True)
        dg_ref[...] += jnp.sum(gov * gtv * z, axis=0, keepdims=True)
        dz = gov * gtv * gv
        dy_ref[...] = (r * (dz - z * jnp.mean(dz * z, axis=-1, keepdims=True))).astype(BF16)

    row = pl.BlockSpec((tr, c), lambda i: (i, 0))
    vec = pl.BlockSpec((1, c), lambda i: (0, 0))
    vs = jax.ShapeDtypeStruct((1, c), F32)
    return pl.pallas_call(body, grid=(t // tr,), in_specs=[row, row, vec, vec], out_specs=[row, vec, vec],
                          out_shape=[jax.ShapeDtypeStruct((t, c), BF16), vs, vs],
                          compiler_params=_cp("arbitrary"), name=name)(gout, y, g, gt)


def _loss_head(y, tgt, *, name):
    t, c = y.shape
    tr = _row_tile(t)

    def body(y_ref, t_ref, l_ref, dy_ref):
        @pl.when(pl.program_id(0) == 0)
        def _():
            l_ref[...] = jnp.zeros_like(l_ref)

        d = y_ref[...] - t_ref[...]
        dy_ref[...] = d * (1.0 / c)
        l_ref[...] += 0.5 * jnp.sum(jnp.mean(d * d, axis=-1, keepdims=True))

    row = pl.BlockSpec((tr, c), lambda i: (i, 0))
    return pl.pallas_call(body, grid=(t // tr,), in_specs=[row, row],
                          out_specs=[pl.BlockSpec((1, LANES), lambda i: (0, 0)), row],
                          out_shape=[jax.ShapeDtypeStruct((1, LANES), F32), jax.ShapeDtypeStruct((t, c), F32)],
                          compiler_params=_cp("arbitrary"), name=name)(y, tgt)


def _adamw(w, g, m, v, *, name):
    shape = w.shape
    c = shape[-1]
    r = math.prod(shape[:-1])
    w2, g2, m2, v2 = (a.reshape(r, c) for a in (w, g, m, v))
    tr = r
    for cand in (1024, 512, 256, 128, 64, 32, 16, 8):
        if r % cand == 0 and cand * c * 4 <= (1 << 20):
            tr = cand
            break
    c1 = 1.0 - ADAM_B1 ** ADAM_STEP
    c2 = 1.0 - ADAM_B2 ** ADAM_STEP

    def body(w_ref, g_ref, m_ref, v_ref, d_ref, nm_ref, nv_ref):
        gv = g_ref[...]
        mn = ADAM_B1 * m_ref[...] + (1.0 - ADAM_B1) * gv
        vn = ADAM_B2 * v_ref[...] + (1.0 - ADAM_B2) * jnp.square(gv)
        m_hat = mn / c1
        v_hat = vn / c2
        d_ref[...] = -ADAM_LR * (m_hat / (jnp.sqrt(v_hat) + ADAM_EPS) + ADAM_WD * w_ref[...])
        nm_ref[...] = mn
        nv_ref[...] = vn

    blk = pl.BlockSpec((tr, c), lambda i: (i, 0))
    s = jax.ShapeDtypeStruct((r, c), F32)
    d, nm, nv = pl.pallas_call(body, grid=(r // tr,), in_specs=[blk] * 4, out_specs=[blk] * 3, out_shape=[s, s, s],
                               compiler_params=_cp("parallel"), name=name)(w2, g2, m2, v2)
    return d.reshape(shape), nm.reshape(shape), nv.reshape(shape)


def _gla_parts(p_ref, wg_ref, bg_ref):
    q = p_ref[:, 0:GLA_KW] * (GLA_DK ** -0.5)
    k = p_ref[:, GLA_KW:2 * GLA_KW]
    glr = p_ref[:, 2 * GLA_KW + 2 * GLA_VW:GLA_IN_PAD]
    gate = _dg(glr, wg_ref[...]) + bg_ref[...]
    log_a = (jnp.minimum(gate, 0.0) - jnp.log(1.0 + jnp.exp(-jnp.abs(gate)))) * (1.0 / 16.0)
    tril = (_iota2((CHUNK, CHUNK), 0) >= _iota2((CHUNK, CHUNK), 1)).astype(F32)
    cum = _dot_sel(tril, log_a)
    c_last = cum[CHUNK - 1:CHUNK, :]
    f = jnp.exp(c_last - cum)
    dec = jnp.exp(c_last)
    return q, k, glr, gate, f, k * f, dec


def _gla_fwd(p, wg, bg, hg, *, name):
    t = p.shape[0]
    nc = t // CHUNK

    def body(p_ref, wg_ref, bg_ref, hg_ref, og_ref, s_ref, st):
        @pl.when(pl.program_id(0) == 0)
        def _():
            st[...] = jnp.zeros_like(st)

        q, _, _, _, _, ke, dec = _gla_parts(p_ref, wg_ref, bg_ref)
        hs = range(GLA_HEADS)
        ks = [slice(h * GLA_DK, (h + 1) * GLA_DK) for h in hs]
        vs = [slice(2 * GLA_KW + h * GLA_DV, 2 * GLA_KW + (h + 1) * GLA_DV) for h in hs]
        rs = [slice(2 * GLA_KW + GLA_VW + h * GLA_DV, 2 * GLA_KW + GLA_VW + (h + 1) * GLA_DV) for h in hs]
        s_new = [st[h] * dec[:, ks[h]] + _dg(p_ref[:, vs[h]], ke[:, ks[h]], TN) for h in hs]
        o = [_dg(q[:, ks[h]], s_new[h], NT) for h in hs]
        for h in hs:
            st[h] = s_new[h]
            s_ref[0, h] = s_new[h]
            rn = lax.rsqrt(jnp.mean(o[h] * o[h], axis=-1, keepdims=True) + EPS)
            rv = p_ref[:, rs[h]]
            og_ref[:, h * GLA_DV:(h + 1) * GLA_DV] = (((o[h] * rn) * hg_ref[...]) * (rv * _sigmoid(rv))).astype(BF16)

    full = lambda a: pl.BlockSpec(a.shape, lambda n: (0,) * a.ndim)
    return pl.pallas_call(
        body, grid=(nc,),
        in_specs=[pl.BlockSpec((CHUNK, GLA_IN_PAD), lambda n: (n, 0)), full(wg), full(bg), full(hg)],
        out_specs=[pl.BlockSpec((CHUNK, GLA_VW), lambda n: (n, 0)),
                   pl.BlockSpec((1, GLA_HEADS, GLA_DV, GLA_DK), lambda n: (n, 0, 0, 0))],
        out_shape=[jax.ShapeDtypeStruct((t, GLA_VW), BF16), jax.ShapeDtypeStruct((nc, GLA_HEADS, GLA_DV, GLA_DK), F32)],
        scratch_shapes=[pltpu.VMEM((GLA_HEADS, GLA_DV, GLA_DK), F32)],
        compiler_params=_cp("arbitrary"), name=name)(p, wg, bg, hg)


def _gla_bwd(p, dog, sall, wg, bg, hg, *, name):
    t = p.shape[0]
    nc = t // CHUNK

    def body(p_ref, dog_ref, s1_ref, s0_ref, wg_ref, bg_ref, hg_ref, dp_ref, dwg_ref, dbg_ref, dhg_ref, gt):
        i = pl.program_id(0)

        @pl.when(i == 0)
        def _():
            gt[...] = jnp.zeros_like(gt)
            dwg_ref[...] = jnp.zeros_like(dwg_ref)
            dbg_ref[...] = jnp.zeros_like(dbg_ref)
            dhg_ref[...] = jnp.zeros_like(dhg_ref)

        has_prev = (i < nc - 1).astype(F32)
        q, k, glr, gate, f, ke, dec = _gla_parts(p_ref, wg_ref, bg_ref)
        hgv = hg_ref[...]
        hs = range(GLA_HEADS)
        ks = [slice(h * GLA_DK, (h + 1) * GLA_DK) for h in hs]
        vs = [slice(2 * GLA_KW + h * GLA_DV, 2 * GLA_KW + (h + 1) * GLA_DV) for h in hs]
        rs = [slice(2 * GLA_KW + GLA_VW + h * GLA_DV, 2 * GLA_KW + GLA_VW + (h + 1) * GLA_DV) for h in hs]
        s1 = [s1_ref[0, h] for h in hs]
        o = [_dg(q[:, ks[h]], s1[h], NT) for h in hs]
        dhg = jnp.zeros((1, GLA_DV), F32)
        do = []
        for h in hs:
            rv = p_ref[:, rs[h]]
            rn = lax.rsqrt(jnp.mean(o[h] * o[h], axis=-1, keepdims=True) + EPS)
            z = o[h] * rn
            sg = _sigmoid(rv)
            sl = rv * sg
            dogh = dog_ref[:, h * GLA_DV:(h + 1) * GLA_DV].astype(F32)
            dhg = dhg + jnp.sum(dogh * z * sl, axis=0, keepdims=True)
            dp_ref[:, rs[h]] = (dogh * (z * hgv) * (sg * (1.0 + rv * (1.0 - sg)))).astype(BF16)
            dz = dogh * sl * hgv
            do.append(rn * (dz - z * jnp.mean(dz * z, axis=-1, keepdims=True)))
        dhg_ref[...] += dhg
        g_tot = [gt[h] + _dg(do[h], q[:, ks[h]], TN) for h in hs]
        dq = [_dg(do[h], s1[h], NN) for h in hs]
        dke_parts = [_dg(p_ref[:, vs[h]], g_tot[h], NN) for h in hs]
        dv = [_dg(ke[:, ks[h]], g_tot[h], NT) for h in hs]
        ddec_parts = []
        for h in hs:
            dp_ref[:, ks[h]] = (dq[h] * (GLA_DK ** -0.5)).astype(BF16)
            dp_ref[:, vs[h]] = dv[h].astype(BF16)
            ddec_parts.append(jnp.sum(g_tot[h] * (s0_ref[0, h] * has_prev), axis=0, keepdims=True))
            gt[h] = g_tot[h] * dec[:, ks[h]]
        dke = jnp.concatenate(dke_parts, axis=1)
        ddec = jnp.concatenate(ddec_parts, axis=1)
        dp_ref[:, GLA_KW:2 * GLA_KW] = (dke * f).astype(BF16)
        stril = (_iota2((CHUNK, CHUNK), 0) > _iota2((CHUNK, CHUNK), 1)).astype(F32)
        dlog_a = _dot_sel(stril, dke * ke) + ddec * dec
        dgate = dlog_a * (1.0 / 16.0) * _sigmoid(-gate)
        dp_ref[:, 2 * GLA_KW + 2 * GLA_VW:GLA_IN_PAD] = _dg(dgate, wg_ref[...], NT).astype(BF16)
        dwg_ref[...] += _dg(glr, dgate, TN)
        dbg_ref[...] += jnp.sum(dgate, axis=0, keepdims=True)

    full = lambda a: pl.BlockSpec(a.shape, lambda n: (0,) * a.ndim)
    sblk = (1, GLA_HEADS, GLA_DV, GLA_DK)
    return pl.pallas_call(
        body, grid=(nc,),
        in_specs=[pl.BlockSpec((CHUNK, GLA_IN_PAD), lambda n: (nc - 1 - n, 0)),
                  pl.BlockSpec((CHUNK, GLA_VW), lambda n: (nc - 1 - n, 0)),
                  pl.BlockSpec(sblk, lambda n: (nc - 1 - n, 0, 0, 0)),
                  pl.BlockSpec(sblk, lambda n: (jnp.maximum(nc - 2 - n, 0), 0, 0, 0)),
                  full(wg), full(bg), full(hg)],
        out_specs=[pl.BlockSpec((CHUNK, GLA_IN_PAD), lambda n: (nc - 1 - n, 0)), full(wg), full(bg), full(hg)],
        out_shape=[jax.ShapeDtypeStruct((t, GLA_IN_PAD), BF16), jax.ShapeDtypeStruct(wg.shape, F32),
                   jax.ShapeDtypeStruct(bg.shape, F32), jax.ShapeDtypeStruct(hg.shape, F32)],
        scratch_shapes=[pltpu.VMEM((GLA_HEADS, GLA_DV, GLA_DK), F32)],
        compiler_params=_cp("arbitrary"), name=name)(p, dog, sall, sall, wg, bg, hg)


def _gla_layer_fwd(h, w, tag):
    p = _mm(h, w["w_in"], name=tag + "_in")
    og, sall = _gla_fwd(p, w["wg"], w["bg"], w["hg"], name=tag + "_scan")
    y = _mm(og, w["w_out"], name=tag + "_out")
    return y, (p, og, sall)


def _dw_out(act, dy, sink, name):
    if sink is None:
        return _mm(act, dy, ta=True, name=name), None
    return None, _mm(act, dy, ta=True, out_dtype=sink[0].dtype, into=(sink[0], sink[1], "rows"), name=name)


def _gla_layer_bwd(dy, h, saved, w, tag, sink=None):
    p, og, sall = saved
    dog = _mm(dy, w["w_out"], tb=True, name=tag + "_dog")
    dw_out, gf = _dw_out(og, dy, sink, tag + "_dwout")
    dp, dwg, dbg, dhg = _gla_bwd(p, dog, sall, w["wg"], w["bg"], w["hg"], name=tag + "_scanb")
    dw_in = _mm(h, dp, ta=True, out_dtype=BF16, name=tag + "_dwin")
    dh = _mm(dp, w["w_in"], tb=True, out_dtype=BF16, name=tag + "_dh")
    grads = dict(w_in=dw_in[:, :GLA_IN], w_gate_up=dwg[:GLA_RANK], b_gate=dbg[0], head_g=dhg[0], w_out=dw_out)
    return dh, grads, gf


def _rope_tables(pos, inv_freq, *, name):
    t = pos.shape[0]
    tr = _row_tile(t)
    half = MLA_ROPE // 2

    def body(p_ref, f_ref, c_ref, s1_ref, s2_ref, s1b_ref, s2b_ref):
        ang = p_ref[...].astype(F32) * f_ref[...]
        lane = _iota2((tr, LANES), 1)
        lo = (lane >= MLA_NOPE) & (lane < MLA_NOPE + half)
        hi = (lane >= MLA_NOPE + half) & (lane < MLA_QK)
        cs, sn = jnp.cos(ang), jnp.sin(ang)
        zero = jnp.zeros_like(cs)
        c_ref[...] = jnp.where(lane < MLA_NOPE, 1.0, jnp.where(lane < MLA_QK, cs, 0.0))
        s1_ref[...] = jnp.where(lo, -sn, zero)
        s2_ref[...] = jnp.where(hi, sn, zero)
        s1b_ref[...] = jnp.where(lo, sn, zero)
        s2b_ref[...] = jnp.where(hi, -sn, zero)

    row = pl.BlockSpec((tr, LANES), lambda i: (i, 0))
    s = jax.ShapeDtypeStruct((t, LANES), F32)
    return pl.pallas_call(body, grid=(t // tr,),
                          in_specs=[pl.BlockSpec((tr, 1), lambda i: (i, 0)), pl.BlockSpec((1, LANES), lambda i: (0, 0))],
                          out_specs=[row] * 5, out_shape=[s] * 5, compiler_params=_cp("parallel"), name=name)(pos, inv_freq)


def _rope(x, c, s1, s2, *, out_dtype, sum_heads=False, name):
    nh, t, _ = x.shape
    tr = _row_tile(t)
    half = MLA_ROPE // 2

    def body(x_ref, c_ref, s1_ref, s2_ref, o_ref):
        total = None
        for h in range(nh):
            xv = x_ref[h].astype(F32)
            y = xv * c_ref[...] + pltpu.roll(xv, LANES - half, 1) * s1_ref[...] + pltpu.roll(xv, half, 1) * s2_ref[...]
            if sum_heads:
                total = y if total is None else total + y
            else:
                o_ref[h] = y.astype(out_dtype)
        if sum_heads:
            o_ref[...] = total

    tab = pl.BlockSpec((tr, LANES), lambda i: (i, 0))
    xs = pl.BlockSpec((nh, tr, LANES), lambda i: (0, i, 0))
    if sum_heads:
        return pl.pallas_call(body, grid=(t // tr,), in_specs=[xs, tab, tab, tab], out_specs=tab,
                              out_shape=jax.ShapeDtypeStruct((t, LANES), F32),
                              compiler_params=_cp("parallel"), name=name)(x, c, s1, s2)
    return pl.pallas_call(body, grid=(t // tr,), in_specs=[xs, tab, tab, tab], out_specs=xs,
                          out_shape=jax.ShapeDtypeStruct(x.shape, out_dtype),
                          compiler_params=_cp("parallel"), name=name)(x, c, s1, s2)


FLASH_BLK = 512


def _diag_mask(blk):
    return (_iota2((blk, blk), 1) // CHUNK) <= (_iota2((blk, blk), 0) // CHUNK)


def _flash_fwd(q, k, v, *, name):
    nh, t, _ = q.shape
    blk = min(FLASH_BLK, t)
    scale = MLA_QK ** -0.5

    def body(q_ref, k_ref, v_ref, o_ref, lse_ref):
        i = pl.program_id(1)
        qv = q_ref[0]

        def step(j, carry, masked):
            m, l, acc = carry
            off = pl.multiple_of(j * blk, blk)
            kb = k_ref[0, pl.ds(off, blk), :]
            vb = v_ref[0, pl.ds(off, blk), :]
            s = _dg(qv, kb, NT) * scale
            if masked:
                s = jnp.where(_diag_mask(blk), s, NEG_INF)
            m_new = jnp.maximum(m, jnp.max(s, axis=-1, keepdims=True))
            p = jnp.exp(s - m_new)
            alpha = jnp.exp(m - m_new)
            return m_new, alpha * l + jnp.sum(p, axis=-1, keepdims=True), alpha * acc + _dg(p, vb, NN)

        init = (jnp.full((blk, 1), NEG_INF, F32), jnp.zeros((blk, 1), F32), jnp.zeros((blk, MLA_V), F32))
        carry = lax.fori_loop(0, i, lambda j, c: step(j, c, False), init)
        m, l, acc = step(i, carry, True)
        o_ref[0] = (acc / l).astype(BF16)
        lse_ref[0] = m + jnp.log(l)

    qs = pl.BlockSpec((1, blk, LANES), lambda h, i: (h, i, 0))
    return pl.pallas_call(
        body, grid=(nh, t // blk),
        in_specs=[qs, pl.BlockSpec((1, t, LANES), lambda h, i: (h, 0, 0)), pl.BlockSpec((1, t, MLA_V), lambda h, i: (h, 0, 0))],
        out_specs=[pl.BlockSpec((1, blk, MLA_V), lambda h, i: (h, i, 0)), pl.BlockSpec((1, blk, 1), lambda h, i: (h, i, 0))],
        out_shape=[jax.ShapeDtypeStruct((nh, t, MLA_V), BF16), jax.ShapeDtypeStruct((nh, t, 1), F32)],
        compiler_params=_cp("parallel", "parallel"), name=name)(q, k, v)


def _flash_bwd(q, k, v, do, o, lse, *, name):
    nh, t, _ = q.shape
    blk = min(FLASH_BLK, t)
    nq = t // blk
    scale = MLA_QK ** -0.5

    def body(q_ref, k_ref, v_ref, do_ref, o_ref, lse_ref, dq_ref, dk_ref, dv_ref, dl):
        j = pl.program_id(1)

        @pl.when(j == 0)
        def _():
            dq_ref[...] = jnp.zeros_like(dq_ref)
            dl[...] = jnp.sum(do_ref[0].astype(F32) * o_ref[0].astype(F32), axis=-1, keepdims=True)

        kb = k_ref[0]
        vb = v_ref[0]

        def step(i, carry, masked):
            dk, dv = carry
            rows = pl.ds(pl.multiple_of(i * blk, blk), blk)
            qb = q_ref[0, rows, :]
            dob = do_ref[0, rows, :]
            s = _dg(qb, kb, NT) * scale
            if masked:
                s = jnp.where(_diag_mask(blk), s, NEG_INF)
            p = jnp.exp(s - lse_ref[0, rows, :])
            ds = p * (_dg(dob, vb, NT) - dl[rows, :]) * scale
            dq_ref[0, rows, :] += _dg(ds, kb, NN)
            return dk + _dg(ds, qb, TN), dv + _dg(p, dob, TN)

        carry = step(j, (jnp.zeros((blk, LANES), F32), jnp.zeros((blk, MLA_V), F32)), True)
        dk, dv = lax.fori_loop(j + 1, nq, lambda i, c: step(i, c, False), carry)
        dk_ref[0] = dk
        dv_ref[0] = dv

    ks = pl.BlockSpec((1, blk, LANES), lambda h, j: (h, j, 0))
    vs = pl.BlockSpec((1, blk, MLA_V), lambda h, j: (h, j, 0))
    fl = lambda w: pl.BlockSpec((1, t, w), lambda h, j: (h, 0, 0))
    return pl.pallas_call(
        body, grid=(nh, nq),
        in_specs=[fl(LANES), ks, vs, fl(MLA_V), fl(MLA_V), fl(1)],
        out_specs=[fl(LANES), ks, vs],
        out_shape=[jax.ShapeDtypeStruct((nh, t, LANES), F32), jax.ShapeDtypeStruct((nh, t, LANES), F32),
                   jax.ShapeDtypeStruct((nh, t, MLA_V), F32)],
        scratch_shapes=[pltpu.VMEM((t, 1), F32)],
        compiler_params=_cp("parallel", "arbitrary"), name=name)(q, k, v, do, o, lse)


def _heads_first(a, width):
    t = a.shape[0]
    return a.reshape(t, MLA_HEADS, width).transpose(1, 0, 2)


def _heads_last(a):
    return a.transpose(1, 0, 2).reshape(a.shape[1], -1)


def _mla_layer_fwd(h, pos, w, tag):
    t = h.shape[0]
    zq = jnp.zeros((1, MLA_Q_RANK), F32)
    zkv = jnp.zeros((1, MLA_KV_RANK), F32)
    p = _mm(h, w["w_in"], name=tag + "_in")
    cq, ckv, krp = p[:, :MLA_Q_RANK], p[:, MLA_Q_RANK:MLA_Q_RANK + MLA_KV_RANK], p[:, MLA_Q_RANK + MLA_KV_RANK:MLA_IN]
    qn = _premod_fwd(cq, w["q_norm_g"], zq, zq, name=tag + "_qnorm")
    kvn = _premod_fwd(ckv, w["kv_norm_g"], zkv, zkv, name=tag + "_kvnorm")
    q = _mm(qn, w["w_uq"], name=tag + "_uq")
    kv = _mm(kvn, w["w_ukv"], name=tag + "_ukv")
    q_pre = jnp.pad(_heads_first(q, MLA_QK), ((0, 0), (0, 0), (0, LANES - MLA_QK)))
    kv3 = _heads_first(kv, MLA_NOPE + MLA_V)
    k_pre = jnp.concatenate([kv3[:, :, :MLA_NOPE], jnp.broadcast_to(krp[None], (MLA_HEADS, t, MLA_ROPE)),
                             jnp.zeros((MLA_HEADS, t, LANES - MLA_QK), F32)], axis=-1)
    vh = kv3[:, :, MLA_NOPE:].astype(BF16)
    half = MLA_ROPE // 2
    freq = ROPE_THETA ** (-jnp.arange(half, dtype=F32) / half)
    inv_freq = jnp.concatenate([jnp.zeros((MLA_NOPE,), F32), freq, freq, jnp.zeros((LANES - MLA_QK,), F32)])[None]
    tabs = _rope_tables(pos.reshape(t, 1), inv_freq, name=tag + "_tables")
    qr = _rope(q_pre, tabs[0], tabs[1], tabs[2], out_dtype=BF16, name=tag + "_ropeq")
    kr = _rope(k_pre, tabs[0], tabs[1], tabs[2], out_dtype=BF16, name=tag + "_ropek")
    o, lse = _flash_fwd(qr, kr, vh, name=tag + "_attn")
    of = _heads_last(o)
    y = _mm(of, w["w_out"], name=tag + "_out")
    return y, (cq, ckv, qn, kvn, qr, kr, vh, o, lse, of, tabs)


def _mla_layer_bwd(dy, h, saved, w, tag, sink=None):
    cq, ckv, qn, kvn, qr, kr, vh, o, lse, of, tabs = saved
    t = h.shape[0]
    zq = jnp.zeros((1, MLA_Q_RANK), F32)
    zkv = jnp.zeros((1, MLA_KV_RANK), F32)
    dof = _mm(dy, w["w_out"], tb=True, out_dtype=BF16, name=tag + "_dof")
    dw_out, gf = _dw_out(of, dy, sink, tag + "_dwout")
    do = _heads_first(dof, MLA_V)
    dqr, dkr, dv = _flash_bwd(qr, kr, vh, do, o, lse, name=tag + "_attn_b")
    dq_pre = _rope(dqr, tabs[0], tabs[3], tabs[4], out_dtype=F32, name=tag + "_ropeq_b")
    dk_sum = _rope(dkr, tabs[0], tabs[3], tabs[4], out_dtype=F32, sum_heads=True, name=tag + "_ropek_b")
    dq = _heads_last(dq_pre[:, :, :MLA_QK])
    dkv = _heads_last(jnp.concatenate([dkr[:, :, :MLA_NOPE], dv], axis=-1))
    dw_uq = _mm(qn, dq, ta=True, out_dtype=BF16, name=tag + "_dwuq")
    dqn = _mm(dq, w["w_uq"], tb=True, name=tag + "_dqn")
    dw_ukv = _mm(kvn, dkv, ta=True, out_dtype=BF16, name=tag + "_dwukv")
    dkvn = _mm(dkv, w["w_ukv"], tb=True, name=tag + "_dkvn")
    dcq, dqg, _, _ = _premod_bwd(dqn, cq, w["q_norm_g"], zq, jnp.zeros_like(cq), name=tag + "_qnorm_b")
    dckv, dkvg, _, _ = _premod_bwd(dkvn, ckv, w["kv_norm_g"], zkv, jnp.zeros_like(ckv), name=tag + "_kvnorm_b")
    dp = jnp.concatenate([dcq, dckv, dk_sum[:, MLA_NOPE:MLA_QK], jnp.zeros((t, MLA_IN_PAD - MLA_IN), F32)], axis=1).astype(BF16)
    dw_in = _mm(h, dp, ta=True, out_dtype=BF16, name=tag + "_dwin")
    dh = _mm(dp, w["w_in"], tb=True, out_dtype=BF16, name=tag + "_dh")
    grads = dict(w_in=dw_in[:, :MLA_IN], q_norm_g=dqg[0], w_uq=dw_uq, kv_norm_g=dkvg[0], w_ukv=dw_ukv, w_out=dw_out)
    return dh, grads, gf


CONV_HALO = 8


def _conv_tiles(t):
    return min(512, t), 512


def _gdn_conv_fwd(p, w, *, name):
    t = p.shape[0]
    tr, tc = _conv_tiles(t)
    hb = tr // CONV_HALO

    def body(x_ref, halo_ref, w_ref, o_ref, buf):
        i = pl.program_id(0)
        buf[0:CONV_HALO, :] = halo_ref[...] * (i > 0).astype(F32)
        buf[CONV_HALO:CONV_HALO + tr, :] = x_ref[...]
        base = CONV_HALO - (GDN_CONV - 1)
        acc = buf[pl.ds(base, tr), :] * w_ref[0:1, :]
        for j in range(1, GDN_CONV):
            acc = acc + buf[pl.ds(base + j, tr), :] * w_ref[j:j + 1, :]
        o_ref[...] = acc * _sigmoid(acc)

    return pl.pallas_call(
        body, grid=(t // tr, GDN_CONV_W // tc),
        in_specs=[pl.BlockSpec((tr, tc), lambda i, j: (i, j)),
                  pl.BlockSpec((CONV_HALO, tc), lambda i, j: (jnp.maximum(i * hb - 1, 0), j)),
                  pl.BlockSpec((GDN_CONV, tc), lambda i, j: (0, j))],
        out_specs=pl.BlockSpec((tr, tc), lambda i, j: (i, j)),
        out_shape=jax.ShapeDtypeStruct((t, GDN_CONV_W), F32),
        scratch_shapes=[pltpu.VMEM((tr + CONV_HALO, tc), F32)],
        compiler_params=_cp("parallel", "parallel"), name=name)(p, p, w)


def _gdn_conv_bwd(d, p, w, col0, *, name):
    t = p.shape[0]
    tr, tc = _conv_tiles(t)
    hb = tr // CONV_HALO
    nr = t // tr
    ext = tr + CONV_HALO

    def body(x_ref, xp_ref, xn_ref, d_ref, dn_ref, w_ref, dx_ref, dw_ref, bufx, bufd):
        i = pl.program_id(1)

        @pl.when(i == 0)
        def _():
            dw_ref[...] = jnp.zeros_like(dw_ref)

        last = (i < nr - 1).astype(F32)
        bufx[0:CONV_HALO, :] = xp_ref[...] * (i > 0).astype(F32)
        bufx[CONV_HALO:CONV_HALO + tr, :] = x_ref[...]
        bufx[CONV_HALO + tr:, :] = xn_ref[...] * last
        base = CONV_HALO - (GDN_CONV - 1)
        acc = bufx[pl.ds(base, ext), :] * w_ref[0:1, :]
        for j in range(1, GDN_CONV):
            acc = acc + bufx[pl.ds(base + j, ext), :] * w_ref[j:j + 1, :]
        sg = _sigmoid(acc)
        dsilu = sg * (1.0 + acc * (1.0 - sg))
        bufd[0:tr, :] = d_ref[...] * dsilu[0:tr, :]
        bufd[tr:, :] = dn_ref[...] * last * dsilu[tr:, :]
        dx = bufd[pl.ds(GDN_CONV - 1, tr), :] * w_ref[0:1, :]
        for j in range(1, GDN_CONV):
            dx = dx + bufd[pl.ds(GDN_CONV - 1 - j, tr), :] * w_ref[j:j + 1, :]
        dx_ref[...] = dx.astype(BF16)
        dc = bufd[0:tr, :]
        for j in range(GDN_CONV):
            dw_ref[j:j + 1, :] += jnp.sum(dc * bufx[pl.ds(base + j, tr), :], axis=0, keepdims=True)

    cb = col0 // tc
    width = d.shape[1]
    main = lambda off: pl.BlockSpec((tr, tc), lambda j, i: (i, j + off))
    prev = pl.BlockSpec((CONV_HALO, tc), lambda j, i: (jnp.maximum(i * hb - 1, 0), j + cb))
    nxt = lambda off: pl.BlockSpec((CONV_HALO, tc), lambda j, i: (jnp.minimum((i + 1) * hb, t // CONV_HALO - 1), j + off))
    wsp = lambda off: pl.BlockSpec((GDN_CONV, tc), lambda j, i: (0, j + off))
    return pl.pallas_call(
        body, grid=(width // tc, nr),
        in_specs=[main(cb), prev, nxt(cb), main(0), nxt(0), wsp(cb)], out_specs=[main(0), wsp(0)],
        out_shape=[jax.ShapeDtypeStruct((t, width), BF16), jax.ShapeDtypeStruct((GDN_CONV, width), F32)],
        scratch_shapes=[pltpu.VMEM((tr + 2 * CONV_HALO, tc), F32), pltpu.VMEM((ext, tc), F32)],
        compiler_params=_cp("parallel", "arbitrary"), name=name)(p, p, p, d, d, w)


def _dot_sel(sel, b, dims=NN, sel_first=True):
    s = sel.astype(BF16)
    b1 = b.astype(BF16)
    r1 = b - b1.astype(F32)
    b2 = r1.astype(BF16)
    b3 = (r1 - b2.astype(F32)).astype(BF16)
    if sel_first:
        d = lambda v: lax.dot_general(s, v, (dims, ((), ())), preferred_element_type=F32)
    else:
        d = lambda v: lax.dot_general(v, s, (dims, ((), ())), preferred_element_type=F32)
    return d(b1) + (d(b2) + d(b3))


def _gdn_chunks(qraws, kraws, vs, braws, araws, alogs, dtbs):
    nv = len(vs)
    row = _iota2((CHUNK, CHUNK), 0)
    col = _iota2((CHUNK, CHUNK), 1)
    strict = row > col
    triu = (row <= col).astype(F32)
    tril = (row >= col).astype(F32)
    ones = jnp.ones((CHUNK, CHUNK), F32)
    keys = []
    for qraw, kraw in zip(qraws, kraws):
        rq = lax.rsqrt(jnp.sum(qraw * qraw, axis=-1, keepdims=True) + EPS)
        rk = lax.rsqrt(jnp.sum(kraw * kraw, axis=-1, keepdims=True) + EPS)
        qn = qraw * rq
        keys.append(dict(rq=rq, rk=rk, qn=qn, qh=qn * (GDN_DK ** -0.5), kh=kraw * rk))
    kks = [_dg(kd["kh"], kd["kh"], NT) for kd in keys]
    cs = []
    for h in range(nv):
        c = dict(keys[h // 2])
        c.update(v=vs[h], kk=kks[h // 2], strict=strict, triu=triu)
        c["beta"] = _sigmoid(braws[h])
        c["ea"] = jnp.exp(alogs[h])
        c["xs"] = araws[h] + dtbs[h]
        c["g"] = -c["ea"] * _softplus(c["xs"])
        cs.append(c)
    gbs = [jnp.broadcast_to(c["g"], (CHUNK, LANES)) for c in cs]
    cums = [_dot_sel(tril, gb) for gb in gbs]
    cum_js = [_dot_sel(ones, gb[:, :CHUNK] * triu) for gb in gbs]
    for c, cum, cum_j in zip(cs, cums, cum_js):
        diff = jnp.where(strict, cum[:, :CHUNK] - cum_j, 0.0)
        c["dm"] = jnp.where(strict, jnp.exp(diff), 0.0)
        c["a"] = (c["beta"] * c["dm"]) * c["kk"]
        c_last = cum[CHUNK - 1:CHUNK, :]
        c["e"] = jnp.exp(cum)
        c["f"] = jnp.exp(c_last - cum)
        c["dec"] = jnp.exp(c_last)
        c["rv"] = c["beta"] * c["v"]
        c["rk_rhs"] = (c["beta"] * c["e"]) * c["kh"]
        c["ke"] = c["kh"] * c["f"]
    return cs


def _unit_lower_inverses(mats):
    eye = (_iota2((CHUNK, CHUNK), 0) == _iota2((CHUNK, CHUNK), 1)).astype(F32)
    ts = [eye - a for a in mats]
    pws = list(mats)
    for _ in range(5):
        pws = [_dot3(pw, pw) for pw in pws]
        ts = [t + _dot3(t, pw) for t, pw in zip(ts, pws)]
    return ts


GDN_HB = 16


def _gdn_specs(chunk_of):
    hb = GDN_HB
    kw = hb // 2 * GDN_DK
    vw = hb * GDN_DV
    qs = pl.BlockSpec((CHUNK, kw), lambda g, n: (chunk_of(n), g))
    ks = pl.BlockSpec((CHUNK, kw), lambda g, n: (chunk_of(n), GDN_QKW // kw + g))
    vs = pl.BlockSpec((CHUNK, vw), lambda g, n: (chunk_of(n), 2 * GDN_QKW // vw + g))
    zs = pl.BlockSpec((CHUNK, vw), lambda g, n: (chunk_of(n), GDN_CONV_W // vw + g))
    assert hb == GDN_V_HEADS and (GDN_CONV_W + GDN_VW) % LANES == 0
    gates = pl.BlockSpec((CHUNK, LANES), lambda g, n: (chunk_of(n), (GDN_CONV_W + GDN_VW) // LANES))
    one = pl.BlockSpec((hb, 1, 1), lambda g, n: (g, 0, 0))
    ng = pl.BlockSpec((1, GDN_DV), lambda g, n: (0, 0))
    hd = pl.BlockSpec((CHUNK, vw), lambda g, n: (chunk_of(n), g))
    return qs, ks, vs, zs, gates, one, ng, hd


def _gate_columns(gates):
    return ([gates[:, h:h + 1] for h in range(GDN_V_HEADS)],
            [gates[:, GDN_V_HEADS + h:GDN_V_HEADS + h + 1] for h in range(GDN_V_HEADS)])


def _gdn_fwd(qkv, p, alog, dtb, ng, *, name):
    t = qkv.shape[0]
    nc = t // CHUNK
    nh = GDN_V_HEADS

    def body(q_ref, k_ref, v_ref, z_ref, gates_ref, alog_ref, dtb_ref, ng_ref, og_ref, s_ref, t_ref, st):
        @pl.when(pl.program_id(1) == 0)
        def _():
            st[...] = jnp.zeros_like(st)

        hs = range(GDN_HB)
        kqs = [slice(j * GDN_DK, (j + 1) * GDN_DK) for j in range(GDN_HB // 2)]
        vsl = [slice(h * GDN_DV, (h + 1) * GDN_DV) for h in hs]
        braws, araws = _gate_columns(gates_ref[...])
        cs = _gdn_chunks([q_ref[:, s] for s in kqs], [k_ref[:, s] for s in kqs], [v_ref[:, s] for s in vsl],
                         braws, araws, [alog_ref[h] for h in hs], [dtb_ref[h] for h in hs])
        tms = _unit_lower_inverses([c["a"] for c in cs])
        s0 = [st[h] for h in hs]
        wv = [_dot3(tms[h], cs[h]["rv"]) for h in hs]
        wk = [_dot3(tms[h], cs[h]["rk_rhs"]) for h in hs]
        u = [wv[h] - _dg(wk[h], s0[h], NN) for h in hs]
        s1 = [cs[h]["dec"] * s0[h] + _dg(cs[h]["ke"], u[h], TN) for h in hs]
        o = [_dg(cs[h]["qh"], s1[h], NN) for h in hs]
        for h in hs:
            t_ref[h, 0] = tms[h]
            st[h] = s1[h]
            s_ref[h, 0] = s1[h]
            rn = lax.rsqrt(jnp.mean(o[h] * o[h], axis=-1, keepdims=True) + EPS)
            zv = z_ref[:, vsl[h]]
            og_ref[:, vsl[h]] = (((o[h] * rn) * ng_ref[...]) * (zv * _sigmoid(zv))).astype(BF16)

    qs, ks, vs, zs, gates, one, ngs, hd = _gdn_specs(lambda n: n)
    return pl.pallas_call(
        body, grid=(nh // GDN_HB, nc),
        in_specs=[qs, ks, vs, zs, gates, one, one, ngs],
        out_specs=[hd,
                   pl.BlockSpec((GDN_HB, 1, GDN_DK, GDN_DV), lambda g, n: (g, n, 0, 0)),
                   pl.BlockSpec((GDN_HB, 1, CHUNK, CHUNK), lambda g, n: (g, n, 0, 0))],
        out_shape=[jax.ShapeDtypeStruct((t, GDN_VW), BF16), jax.ShapeDtypeStruct((nh, nc, GDN_DK, GDN_DV), F32),
                   jax.ShapeDtypeStruct((nh, nc, CHUNK, CHUNK), F32)],
        scratch_shapes=[pltpu.VMEM((GDN_HB, GDN_DK, GDN_DV), F32)],
        compiler_params=_cp("parallel", "arbitrary"), name=name)(qkv, qkv, qkv, p, p, alog, dtb, ng)


def _gdn_bwd(qkv, p, alog, dtb, ng, dog, sall, tall, *, name):
    t = qkv.shape[0]
    nc = t // CHUNK
    nh = GDN_V_HEADS

    def body(q_ref, k_ref, v_ref, z_ref, gates_ref, alog_ref, dtb_ref, ng_ref, dog_ref, s1_ref, s0_ref, t_ref,
             dq_ref, dk_ref, dv_ref, dz_ref, dgates_ref, dalog_ref, ddtb_ref, dng_ref, gc):
        grp = pl.program_id(0)
        i = pl.program_id(1)

        @pl.when(i == 0)
        def _():
            gc[...] = jnp.zeros_like(gc)
            dalog_ref[...] = jnp.zeros_like(dalog_ref)
            ddtb_ref[...] = jnp.zeros_like(ddtb_ref)

        @pl.when((i == 0) & (grp == 0))
        def _():
            dng_ref[...] = jnp.zeros_like(dng_ref)

        has_prev = (i < nc - 1).astype(F32)
        ngv = ng_ref[...]
        ones = jnp.ones((CHUNK, LANES), F32)
        hs = range(GDN_HB)
        kqs = [slice(j * GDN_DK, (j + 1) * GDN_DK) for j in range(GDN_HB // 2)]
        vsl = [slice(h * GDN_DV, (h + 1) * GDN_DV) for h in hs]
        braws, araws = _gate_columns(gates_ref[...])
        cs = _gdn_chunks([q_ref[:, s] for s in kqs], [k_ref[:, s] for s in kqs], [v_ref[:, s] for s in vsl],
                         braws, araws, [alog_ref[h] for h in hs], [dtb_ref[h] for h in hs])
        tms = [t_ref[h, 0] for h in hs]
        s1 = [s1_ref[h, 0] for h in hs]
        s0 = [s0_ref[h, 0] * has_prev for h in hs]
        wv = [_dot3(tms[h], cs[h]["rv"]) for h in hs]
        wk = [_dot3(tms[h], cs[h]["rk_rhs"]) for h in hs]
        u = [wv[h] - _dg(wk[h], s0[h], NN) for h in hs]
        o = [_dg(cs[h]["qh"], s1[h], NN) for h in hs]
        dng = jnp.zeros((1, GDN_DV), F32)
        do = []
        for h in hs:
            zv = z_ref[:, vsl[h]]
            dogv = dog_ref[:, vsl[h]]
            rn = lax.rsqrt(jnp.mean(o[h] * o[h], axis=-1, keepdims=True) + EPS)
            zo = o[h] * rn
            sg = _sigmoid(zv)
            sl = zv * sg
            dng = dng + jnp.sum(dogv * zo * sl, axis=0, keepdims=True)
            dz_ref[:, vsl[h]] = (dogv * (zo * ngv) * (sg * (1.0 + zv * (1.0 - sg)))).astype(BF16)
            dzo = dogv * sl * ngv
            do.append(rn * (dzo - zo * jnp.mean(dzo * zo, axis=-1, keepdims=True)))
        dng_ref[...] += dng
        g_tot = [gc[h] + _dg(cs[h]["qh"], do[h], TN) for h in hs]
        dqh = [_dg(do[h], s1[h], NT) for h in hs]
        dke = [_dg(u[h], g_tot[h], NT) for h in hs]
        du = [_dg(cs[h]["ke"], g_tot[h], NN) for h in hs]
        gnew = [cs[h]["dec"] * g_tot[h] - _dg(wk[h], du[h], TN) for h in hs]
        dwk = [-_dg(du[h], s0[h], NT) for h in hs]
        drv = [_dot3(tms[h], du[h], TN) for h in hs]
        drk = [_dot3(tms[h], dwk[h], TN) for h in hs]
        da = [jnp.where(cs[h]["strict"], -(_dot3(drv[h], wv[h], NT) + _dot3(drk[h], wk[h], NT)), 0.0) for h in hs]
        mx = [da[h] * cs[h]["dm"] * cs[h]["kk"] for h in hs]
        aa = [mx[h] * cs[h]["beta"] for h in hs]
        colsum = [_dot_sel(ones, aa[h], TN, sel_first=False)[:, 0:1] for h in hs]
        bm = [(da[h] * cs[h]["beta"]) * cs[h]["dm"] for h in hs]
        dkh = [_dg(bm[h], cs[h]["kh"], NN) + _dg(bm[h], cs[h]["kh"], TN) for h in hs]
        dcum, dcl, dbeta = [], [], []
        for h in hs:
            c = cs[h]
            beta, kh, e, f, dec, ke = c["beta"], c["kh"], c["e"], c["f"], c["dec"], c["ke"]
            gc[h] = gnew[h]
            ddec = jnp.sum(jnp.sum(g_tot[h] * s0[h], axis=1, keepdims=True), axis=0, keepdims=True)
            dv_ref[:, vsl[h]] = beta * drv[h]
            db = jnp.sum(mx[h], axis=1, keepdims=True) + jnp.sum(drv[h] * c["v"], axis=1, keepdims=True)
            dbeta.append(db + jnp.sum(drk[h] * (e * kh), axis=1, keepdims=True))
            dkh[h] = dkh[h] + (beta * e) * drk[h] + f * dke[h]
            ef = jnp.sum(dke[h] * ke, axis=1, keepdims=True)
            dcum.append(jnp.sum(aa[h], axis=1, keepdims=True) - colsum[h] + jnp.sum(drk[h] * c["rk_rhs"], axis=1, keepdims=True) - ef)
            dcl.append(jnp.sum(ef, axis=0, keepdims=True) + ddec * dec[:, 0:1])
        dg = [_dot_sel(cs[h]["triu"], jnp.broadcast_to(dcum[h], (CHUNK, LANES)))[:, 0:1] + dcl[h] for h in hs]
        lane = _iota2((CHUNK, LANES), 1)
        dgates = jnp.zeros((CHUNK, LANES), F32)
        for h in hs:
            c = cs[h]
            beta = c["beta"]
            daraw = dg[h] * (-c["ea"]) * _sigmoid(c["xs"])
            dgates = jnp.where(lane == h, dbeta[h] * beta * (1.0 - beta), dgates)
            dgates = jnp.where(lane == GDN_V_HEADS + h, daraw, dgates)
            dalog_ref[h] += jnp.sum(dg[h] * c["g"], axis=0, keepdims=True)
            ddtb_ref[h] += jnp.sum(daraw, axis=0, keepdims=True)
        dgates_ref[...] = dgates.astype(BF16)
        for j, sl in enumerate(kqs):
            c = cs[2 * j]
            dn = (dqh[2 * j] + dqh[2 * j + 1]) * (GDN_DK ** -0.5)
            dks = dkh[2 * j] + dkh[2 * j + 1]
            dq_ref[:, sl] = c["rq"] * (dn - c["qn"] * jnp.sum(dn * c["qn"], axis=-1, keepdims=True))
            dk_ref[:, sl] = c["rk"] * (dks - c["kh"] * jnp.sum(dks * c["kh"], axis=-1, keepdims=True))

    rev = lambda n: nc - 1 - n
    qs, ks, vs, zs, gates, one, ngs, hd = _gdn_specs(rev)
    s1s = pl.BlockSpec((GDN_HB, 1, GDN_DK, GDN_DV), lambda g, n: (g, rev(n), 0, 0))
    s0s = pl.BlockSpec((GDN_HB, 1, GDN_DK, GDN_DV), lambda g, n: (g, jnp.maximum(rev(n) - 1, 0), 0, 0))
    ts = pl.BlockSpec((GDN_HB, 1, CHUNK, CHUNK), lambda g, n: (g, rev(n), 0, 0))
    dgs = pl.BlockSpec((CHUNK, LANES), lambda g, n: (rev(n), 0))
    big = jax.ShapeDtypeStruct((t, GDN_VW), F32)
    keyw = jax.ShapeDtypeStruct((t, GDN_QKW), F32)
    ones_s = jax.ShapeDtypeStruct((nh, 1, 1), F32)
    return pl.pallas_call(
        body, grid=(nh // GDN_HB, nc),
        in_specs=[qs, ks, vs, zs, gates, one, one, ngs, hd, s1s, s0s, ts],
        out_specs=[qs, qs, hd, hd, dgs, one, one, ngs],
        out_shape=[keyw, keyw, big, jax.ShapeDtypeStruct((t, GDN_VW), BF16), jax.ShapeDtypeStruct((t, LANES), BF16),
                   ones_s, ones_s, jax.ShapeDtypeStruct((1, GDN_DV), F32)],
        scratch_shapes=[pltpu.VMEM((GDN_HB, GDN_DK, GDN_DV), F32)],
        compiler_params=_cp("arbitrary", "arbitrary"), name=name)(qkv, qkv, qkv, p, p, alog, dtb, ng, dog, sall, sall, tall)


def _gdn_layer_fwd(h, w, tag):
    p = _mm(h, w["w_in"], name=tag + "_in")
    qkv = _gdn_conv_fwd(p, w["conv_w"], name=tag + "_conv")
    og, sall, tall = _gdn_fwd(qkv, p, w["a_log"], w["dt_bias"], w["norm_g"], name=tag + "_scan")
    y = _mm(og, w["w_out"], name=tag + "_out")
    return y, (p, qkv, og, sall, tall)


def _gdn_layer_bwd(dy, h, saved, w, tag, sink=None):
    p, qkv, og, sall, tall = saved
    t = h.shape[0]
    dog = _mm(dy, w["w_out"], tb=True, name=tag + "_dog")
    dw_out, gf = _dw_out(og, dy, sink, tag + "_dwout")
    dq, dk, dv, dz, dgates, dalog, ddtb, dng = _gdn_bwd(
        qkv, p, w["a_log"], w["dt_bias"], w["norm_g"], dog, sall, tall, name=tag + "_scanb")
    dpre_q, dcw_q = _gdn_conv_bwd(dq, p, w["conv_w"], 0, name=tag + "_convb_q")
    dpre_k, dcw_k = _gdn_conv_bwd(dk, p, w["conv_w"], GDN_QKW, name=tag + "_convb_k")
    dpre_v, dcw_v = _gdn_conv_bwd(dv, p, w["conv_w"], 2 * GDN_QKW, name=tag + "_convb_v")
    dconv_w = jnp.concatenate([dcw_q, dcw_k, dcw_v], axis=1)
    dp = jnp.concatenate([dpre_q, dpre_k, dpre_v, dz, dgates,
                          jnp.zeros((t, GDN_IN_PAD - GDN_CONV_W - GDN_VW - LANES), BF16)], axis=1)
    dw_in = _mm(h, dp, ta=True, out_dtype=BF16, name=tag + "_dwin")
    dh = _mm(dp, w["w_in"], tb=True, out_dtype=BF16, name=tag + "_dh")
    grads = dict(w_in=dw_in[:, :GDN_IN], conv_w=dconv_w, a_log=dalog[:, 0, 0], dt_bias=ddtb[:, 0, 0], norm_g=dng[0], w_out=dw_out)
    return dh, grads, gf


MESH_ID = pl.DeviceIdType.MESH
FLAT_W = 1024
FLAT_ROWS = 13056
FLAT_TILE = 384


def _exchange(name, ins, out_shapes, plan, n_remote, n_local):
    def body(*refs):
        in_refs = refs[:len(ins)]
        out_refs = refs[len(ins):len(ins) + len(out_shapes)]
        ssem, rsem, lsem = refs[len(ins) + len(out_shapes):]
        x, y, c = lax.axis_index("x"), lax.axis_index("y"), lax.axis_index("c")
        stages, local_copies = plan(x, y, c, in_refs, out_refs)
        assert sum(len(s) for s in stages) == n_remote and len(local_copies) == n_local
        locs = [pltpu.make_async_copy(s, d, lsem.at[i]) for i, (s, d) in enumerate(local_copies)]
        for cp in locs:
            cp.start()
        sent = []
        k = 0
        for stage in stages:
            arrivals = []
            for src, dst, peer, landing in stage:
                cp = pltpu.make_async_remote_copy(src_ref=src, dst_ref=dst, send_sem=ssem.at[k], recv_sem=rsem.at[k],
                                                  device_id=peer, device_id_type=MESH_ID)
                cp.start()
                sent.append(cp)
                arrivals.append(pltpu.make_async_remote_copy(src_ref=src, dst_ref=landing, send_sem=ssem.at[k],
                                                             recv_sem=rsem.at[k], device_id=peer, device_id_type=MESH_ID))
                k += 1
            for cp in arrivals:
                cp.wait_recv()
        for cp in sent:
            cp.wait_send()
        for cp in locs:
            cp.wait()

    hbm = pl.BlockSpec(memory_space=pl.ANY)
    return pl.pallas_call(
        body, in_specs=[hbm] * len(ins), out_specs=[hbm] * len(out_shapes), out_shape=out_shapes,
        scratch_shapes=[pltpu.SemaphoreType.DMA((n_remote,)), pltpu.SemaphoreType.DMA((n_remote,)),
                        pltpu.SemaphoreType.DMA((max(n_local, 1),))],
        name=name)(*ins)


def _other_chips(x, y):
    return [(1 - x, y), (x, 1 - y), (1 - x, 1 - y)]


def _all8_gather(a, *, name):
    def plan(x, y, c, ins, outs):
        (src,), (dst,) = ins, outs
        me = 4 * x + 2 * y + c
        stage = []
        for fx, fy, fc in [(0, 0, 1), (0, 1, 0), (0, 1, 1), (1, 0, 0), (1, 0, 1), (1, 1, 0), (1, 1, 1)]:
            px, py, pc = (1 - x if fx else x), (1 - y if fy else y), (1 - c if fc else c)
            stage.append((src, dst.at[me], (px, py, pc), dst.at[4 * px + 2 * py + pc]))
        return [stage], [(src, dst.at[me])]

    return _exchange(name, [a], [jax.ShapeDtypeStruct((8,) + a.shape, a.dtype)], plan, 7, 1)[0]


def _chip_gather(flat, *, name):
    rows = flat.shape[0]
    half = rows // 2

    def plan(x, y, c, ins, outs):
        (src,), (dst,) = ins, outs
        me = 2 * x + y
        mine = pl.ds(c * half, half)
        theirs = pl.ds((1 - c) * half, half)
        ici = [(src.at[mine], dst.at[me, mine], (px, py, c), dst.at[2 * px + py, mine]) for px, py in _other_chips(x, y)]
        d2d = [(dst.at[2 * px + py, mine], dst.at[2 * px + py, mine], (x, y, 1 - c), dst.at[2 * px + py, theirs])
               for px, py in _other_chips(x, y)]
        return [ici, d2d], []

    return _exchange(name, [flat], [jax.ShapeDtypeStruct((4,) + flat.shape, flat.dtype)], plan, 6, 0)[0]


def _add_sibling(gf, buf_a, core, *, name):
    _, rows, w = gf.shape
    half = rows // 2
    nb = half // FLAT_TILE

    def body(c_ref, g_ref, a_ref, o_ref):
        o_ref[...] = (g_ref[...].astype(F32) + a_ref[...].astype(F32)).astype(BF16)

    blk = (1, FLAT_TILE, w)
    return pl.pallas_call(
        body,
        grid_spec=pltpu.PrefetchScalarGridSpec(
            num_scalar_prefetch=1, grid=(4, nb),
            in_specs=[pl.BlockSpec(blk, lambda s, i, c_ref: (s, c_ref[0] * nb + i, 0)), pl.BlockSpec(blk, lambda s, i, c_ref: (s, i, 0))],
            out_specs=pl.BlockSpec(blk, lambda s, i, c_ref: (s, i, 0))),
        out_shape=jax.ShapeDtypeStruct((4, half, w), BF16), compiler_params=_cp("parallel", "parallel"), name=name)(core, gf, buf_a)


def _sum_chips(hsum, buf_b, chip, *, name):
    _, half, w = hsum.shape
    nb = half // FLAT_TILE

    def body(c_ref, h_ref, b0_ref, b1_ref, b2_ref, b3_ref, o_ref):
        me = c_ref[0]
        own = h_ref[0].astype(F32)
        acc = None
        for j, b_ref in enumerate((b0_ref, b1_ref, b2_ref, b3_ref)):
            term = jnp.where(me == j, own, b_ref[0].astype(F32))
            acc = term if acc is None else acc + term
        o_ref[...] = acc

    blk = (1, FLAT_TILE, w)

    def other(j):
        return pl.BlockSpec(blk, lambda i, c_ref: (jnp.where(c_ref[0] == j, (j + 1) % 4, j), i, 0))

    return pl.pallas_call(
        body,
        grid_spec=pltpu.PrefetchScalarGridSpec(
            num_scalar_prefetch=1, grid=(nb,),
            in_specs=[pl.BlockSpec(blk, lambda i, c_ref: (c_ref[0], i, 0))] + [other(j) for j in range(4)],
            out_specs=pl.BlockSpec((FLAT_TILE, w), lambda i, c_ref: (i, 0))),
        out_shape=jax.ShapeDtypeStruct((half, w), F32), compiler_params=_cp("parallel"), name=name)(chip, hsum, buf_b, buf_b, buf_b, buf_b)


def _sum_slots(buf, *, name):
    n, rows, w = buf.shape
    tr = _pick(rows, (FLAT_TILE, 8))

    def body(b_ref, o_ref):
        acc = b_ref[0]
        for s in range(1, n):
            acc = acc + b_ref[s]
        o_ref[...] = acc

    return pl.pallas_call(body, grid=(rows // tr,), in_specs=[pl.BlockSpec((n, tr, w), lambda i: (0, i, 0))],
                          out_specs=pl.BlockSpec((tr, w), lambda i: (i, 0)), out_shape=jax.ShapeDtypeStruct((rows, w), F32),
                          compiler_params=_cp("parallel"), name=name)(buf)


def _reduce_scatter(gf, core, chip, *, tag):
    _, rows, w = gf.shape
    half = rows // 2

    def plan_a(x, y, c, ins, outs):
        (src,), (dst,) = ins, outs
        return [[(src.at[:, pl.ds((1 - c) * half, half)], dst, (x, y, 1 - c), dst)]], []

    buf_a = _exchange(tag + "_sibling", [gf], [jax.ShapeDtypeStruct((4, half, w), gf.dtype)], plan_a, 1, 0)[0]
    hsum = _add_sibling(gf, buf_a, core, name=tag + "_add_sibling")

    def plan_b(x, y, c, ins, outs):
        (src,), (dst,) = ins, outs
        me = 2 * x + y
        stage = [(src.at[2 * px + py], dst.at[me], (px, py, c), dst.at[2 * px + py]) for px, py in _other_chips(x, y)]
        return [stage], []

    buf_b = _exchange(tag + "_chips", [hsum], [jax.ShapeDtypeStruct((4, half, w), BF16)], plan_b, 3, 0)[0]
    mine = _sum_chips(hsum, buf_b, chip, name=tag + "_sum_chips")

    def plan_c(x, y, c, ins, outs):
        (src,), (dst,) = ins, outs
        return [[(src, dst, (x, y, 1 - c), dst)]], []

    theirs = _exchange(tag + "_halves", [mine], [jax.ShapeDtypeStruct((half, w), F32)], plan_c, 1, 0)[0]
    first = core[0] == 0
    return jnp.concatenate([jnp.where(first, mine, theirs), jnp.where(first, theirs, mine)], axis=0)


WEIGHTS = ["ada_w", "ada_b", "norm_pre_g", "norm_post_g", "gla_w_in", "gla_w_gate_up", "gla_b_gate", "gla_head_g",
           "gla_w_out", "mla_w_in", "mla_q_norm_g", "mla_w_uq", "mla_kv_norm_g", "mla_w_ukv", "mla_w_out", "gdn_w_in",
           "gdn_conv_w", "gdn_a_log", "gdn_dt_bias", "gdn_norm_g", "gdn_w_out", "mlp_w_up", "mlp_w_down"]
PACK_BF16 = [("mlp_w_up", 2), ("mlp_w_down", 1), ("gdn_w_out", 1), ("gla_w_out", 1), ("mla_w_out", 1),
             ("gla_w_in", 2), ("mla_w_in", 1), ("mla_w_uq", 2), ("mla_w_ukv", 2), ("gdn_w_in", 2)]
N_DIRECT = 5
PACK_F32 = [("norm_pre_g", 2), ("norm_post_g", 2), ("gla_w_gate_up", 2), ("gla_b_gate", 1), ("gla_head_g", 1), ("gdn_conv_w", 2)]
REPLICATED_SMALL = ["mla_q_norm_g", "mla_kv_norm_g", "gdn_a_log", "gdn_dt_bias", "gdn_norm_g"]
MIXERS = ["gla", "mla", "gdn"]


def _silu_rows(a, *, name):
    def body(a_ref, o_ref):
        v = a_ref[...]
        o_ref[...] = v * _sigmoid(v)

    return pl.pallas_call(body, out_shape=jax.ShapeDtypeStruct(a.shape, F32), name=name)(a)


SMALL_ROWS = 16


def _piece_rows(size, mult):
    assert size % FLAT_W == 0
    return -(-(size // FLAT_W) // mult) * mult


def _to_rows(a, lead, mult):
    n = math.prod(a.shape[len(lead):])
    r = a.reshape(lead + (n // FLAT_W, FLAT_W))
    extra = _piece_rows(n, mult) - n // FLAT_W
    return jnp.pad(r, [(0, 0)] * len(lead) + [(0, extra), (0, 0)]) if extra else r


def _small_to_rows(parts, lead):
    flat = jnp.concatenate([p.reshape(lead + (-1,)) for p in parts], axis=-1)
    pad = SMALL_ROWS * FLAT_W - flat.shape[-1]
    return jnp.pad(flat, [(0, 0)] * len(lead) + [(0, pad)]).reshape(lead + (SMALL_ROWS, FLAT_W))


def _small_from_rows(rows, shards, lead):
    flat = rows.reshape(lead + (-1,))
    out, off = {}, 0
    for n, _ in PACK_F32:
        out[n] = flat[..., off:off + shards[n].size].reshape(lead + shards[n].shape)
        off += shards[n].size
    return out


def _pack_weights(shards):
    parts = [_to_rows(shards[n].astype(BF16), (), 16) for n, _ in PACK_BF16]
    small = _small_to_rows([shards[n] for n, _ in PACK_F32], ())
    parts.append(lax.bitcast_convert_type(small, BF16).reshape(2 * SMALL_ROWS, FLAT_W))
    flat = jnp.concatenate(parts, axis=0)
    return jnp.pad(flat, ((0, FLAT_ROWS - flat.shape[0]), (0, 0)))


def _unpack_weights(gathered, shards):
    full, off = {}, 0
    for n, ax in PACK_BF16:
        size = shards[n].size
        seg = gathered[:, off:off + size // FLAT_W].reshape((4,) + shards[n].shape)
        full[n] = jnp.concatenate([seg[j] for j in range(4)], axis=ax)
        off += _piece_rows(size, 16)
    small = lax.bitcast_convert_type(gathered[:, off:off + 2 * SMALL_ROWS].reshape(4, SMALL_ROWS, FLAT_W, 2), F32)
    for (n, ax), seg in zip(PACK_F32, _small_from_rows(small, shards, (4,)).values()):
        full[n] = jnp.concatenate([seg[j] for j in range(4)], axis=ax)
    return full


def _grad_layout(shards):
    layout, off = {}, 0
    for n, _ in PACK_BF16:
        layout[n] = (off, shards[n].size // shards[n].shape[0] // FLAT_W)
        off += _piece_rows(shards[n].size, 16)
    layout["small"] = (off, SMALL_ROWS)
    return layout


def _pack_grads(gf, grads, layout):
    by_chip = lambda g, ax: jnp.stack(jnp.split(g.astype(gf.dtype), 4, axis=ax - 1))
    parts = []
    for n, ax in PACK_BF16[N_DIRECT:]:
        rows = sum(g.size for g in grads[n]) // (4 * FLAT_W)
        parts += [by_chip(g, ax).reshape(4, -1, FLAT_W) for g in grads[n]]
        if _piece_rows(rows * FLAT_W, 16) > rows:
            parts.append(jnp.zeros((4, _piece_rows(rows * FLAT_W, 16) - rows, FLAT_W), gf.dtype))
    parts.append(_small_to_rows([jnp.stack([by_chip(g, ax) for g in grads[n]], axis=1) for n, ax in PACK_F32], (4,)))
    first = layout[PACK_BF16[N_DIRECT][0]][0]
    rest = jnp.concatenate(parts, axis=1)
    assert first + rest.shape[1] == layout["small"][0] + SMALL_ROWS
    return lax.dynamic_update_slice(gf, rest, (0, first, 0))


def _unpack_grads(reduced, shards):
    out, off = {}, 0
    for n, _ in PACK_BF16:
        size = shards[n].size
        out[n] = reduced[off:off + size // FLAT_W].reshape(shards[n].shape)
        off += _piece_rows(size, 16)
    out.update(_small_from_rows(reduced[off:off + SMALL_ROWS], shards, ()))
    return out


def _mixer_weights(kind, j, full, rep):
    if kind == "gla":
        return dict(w_in=jnp.pad(full["gla_w_in"][j], ((0, 0), (0, GLA_IN_PAD - GLA_IN))),
                    wg=jnp.pad(full["gla_w_gate_up"][j], ((0, LANES - GLA_RANK), (0, 0))),
                    bg=full["gla_b_gate"][j][None], hg=full["gla_head_g"][j][None], w_out=full["gla_w_out"][j])
    if kind == "mla":
        return dict(w_in=jnp.pad(full["mla_w_in"][j], ((0, 0), (0, MLA_IN_PAD - MLA_IN))), q_norm_g=rep["mla_q_norm_g"][j][None],
                    w_uq=full["mla_w_uq"][j], kv_norm_g=rep["mla_kv_norm_g"][j][None], w_ukv=full["mla_w_ukv"][j],
                    w_out=full["mla_w_out"][j])
    return dict(w_in=jnp.pad(full["gdn_w_in"][j], ((0, 0), (0, GDN_IN_PAD - GDN_IN))), conv_w=full["gdn_conv_w"][j],
                a_log=rep["gdn_a_log"][j][:, None, None], dt_bias=rep["gdn_dt_bias"][j][:, None, None],
                norm_g=rep["gdn_norm_g"][j][None], w_out=full["gdn_w_out"][j])


def _layer_fwd(xin, mod, gains, kind, mw, w_up, w_down, pos, tag):
    sh_m, sc_m, gt_m, sh_f, sc_f, gt_f = mod
    pre0, pre1, post0, post1 = gains
    h = _premod_fwd(xin, pre0, sc_m, sh_m, name=tag + "_pre0")
    if kind == "gla":
        y, saved = _gla_layer_fwd(h, mw, tag + "_gla")
    elif kind == "mla":
        y, saved = _mla_layer_fwd(h, pos, mw, tag + "_mla")
    else:
        y, saved = _gdn_layer_fwd(h, mw, tag + "_gdn")
    x1 = _postres_fwd(xin, y, post0, gt_m, name=tag + "_post0")
    h2 = _premod_fwd(x1, pre1, sc_f, sh_f, name=tag + "_pre1")
    act = _mm(h2, w_up, out_dtype=BF16, epi="relu2", name=tag + "_up")
    y2 = _mm(act, w_down, name=tag + "_down")
    x2 = _postres_fwd(x1, y2, post1, gt_f, name=tag + "_post1")
    return x2, (xin, h, y, saved, x1, h2, act, y2)


def _layer_bwd(g2, kept, mod, gains, kind, mw, w_up, w_down, tag, gf, rows):
    xin, h, y, saved, x1, h2, act, y2 = kept
    sh_m, sc_m, gt_m, sh_f, sc_f, gt_f = mod
    pre0, pre1, post0, post1 = gains
    dy2, dpost1, dgt_f = _postres_bwd(g2, y2, post1, gt_f, name=tag + "_post1_b")
    du = _mm(dy2, w_down, tb=True, out_dtype=BF16, epi="dact", aux=act, name=tag + "_du")
    gf = _mm(act, dy2, ta=True, out_dtype=gf.dtype, into=(gf, rows[1], "rows"), name=tag + "_dwdown")
    gf = _mm(h2, du, ta=True, out_dtype=gf.dtype, into=(gf, rows[0], "cols"), name=tag + "_dwup")
    dh2 = _mm(du, w_up, tb=True, out_dtype=BF16, name=tag + "_dh2")
    g1, dpre1, dsc_f, dsh_f = _premod_bwd(dh2, x1, pre1, sc_f, g2, name=tag + "_pre1_b")
    dy, dpost0, dgt_m = _postres_bwd(g1, y, post0, gt_m, name=tag + "_post0_b")
    mixer_bwd = dict(gla=_gla_layer_bwd, mla=_mla_layer_bwd, gdn=_gdn_layer_bwd)[kind]
    dh, mg, gf = mixer_bwd(dy, h, saved, mw, tag + "_" + kind, sink=(gf, rows[2]))
    g0, dpre0, dsc_m, dsh_m = _premod_bwd(dh, xin, pre0, sc_m, g1, name=tag + "_pre0_b")
    dmod = jnp.concatenate([dsh_m, dsc_m, dgt_m, dsh_f, dsc_f, dgt_f], axis=1)
    return g0, dmod, jnp.concatenate([dpre0, dpre1], axis=0), jnp.concatenate([dpost0, dpost1], axis=0), mg, gf


def kernel(x, c, positions, ada_w, ada_b, norm_pre_g, norm_post_g, gla_w_in, gla_w_gate_up, gla_b_gate, gla_head_g, gla_w_out, mla_w_in, mla_q_norm_g, mla_w_uq, mla_kv_norm_g, mla_w_ukv, mla_w_out, gdn_w_in, gdn_conv_w, gdn_a_log, gdn_dt_bias, gdn_norm_g, gdn_w_out, mlp_w_up, mlp_w_down, loss_target, m_ada_w, m_ada_b, m_norm_pre_g, m_norm_post_g, m_gla_w_in, m_gla_w_gate_up, m_gla_b_gate, m_gla_head_g, m_gla_w_out, m_mla_w_in, m_mla_q_norm_g, m_mla_w_uq, m_mla_kv_norm_g, m_mla_w_ukv, m_mla_w_out, m_gdn_w_in, m_gdn_conv_w, m_gdn_a_log, m_gdn_dt_bias, m_gdn_norm_g, m_gdn_w_out, m_mlp_w_up, m_mlp_w_down, v_ada_w, v_ada_b, v_norm_pre_g, v_norm_post_g, v_gla_w_in, v_gla_w_gate_up, v_gla_b_gate, v_gla_head_g, v_gla_w_out, v_mla_w_in, v_mla_q_norm_g, v_mla_w_uq, v_mla_kv_norm_g, v_mla_w_ukv, v_mla_w_out, v_gdn_w_in, v_gdn_conv_w, v_gdn_a_log, v_gdn_dt_bias, v_gdn_norm_g, v_gdn_w_out, v_mlp_w_up, v_mlp_w_down):
    w = dict(ada_w=ada_w, ada_b=ada_b, norm_pre_g=norm_pre_g, norm_post_g=norm_post_g, gla_w_in=gla_w_in,
             gla_w_gate_up=gla_w_gate_up, gla_b_gate=gla_b_gate, gla_head_g=gla_head_g, gla_w_out=gla_w_out, mla_w_in=mla_w_in,
             mla_q_norm_g=mla_q_norm_g, mla_w_uq=mla_w_uq, mla_kv_norm_g=mla_kv_norm_g, mla_w_ukv=mla_w_ukv, mla_w_out=mla_w_out,
             gdn_w_in=gdn_w_in, gdn_conv_w=gdn_conv_w, gdn_a_log=gdn_a_log, gdn_dt_bias=gdn_dt_bias, gdn_norm_g=gdn_norm_g,
             gdn_w_out=gdn_w_out, mlp_w_up=mlp_w_up, mlp_w_down=mlp_w_down)
    m = dict(zip(WEIGHTS, [m_ada_w, m_ada_b, m_norm_pre_g, m_norm_post_g, m_gla_w_in, m_gla_w_gate_up, m_gla_b_gate, m_gla_head_g,
                           m_gla_w_out, m_mla_w_in, m_mla_q_norm_g, m_mla_w_uq, m_mla_kv_norm_g, m_mla_w_ukv, m_mla_w_out,
                           m_gdn_w_in, m_gdn_conv_w, m_gdn_a_log, m_gdn_dt_bias, m_gdn_norm_g, m_gdn_w_out, m_mlp_w_up, m_mlp_w_down]))
    v = dict(zip(WEIGHTS, [v_ada_w, v_ada_b, v_norm_pre_g, v_norm_post_g, v_gla_w_in, v_gla_w_gate_up, v_gla_b_gate, v_gla_head_g,
                           v_gla_w_out, v_mla_w_in, v_mla_q_norm_g, v_mla_w_uq, v_mla_kv_norm_g, v_mla_w_ukv, v_mla_w_out,
                           v_gdn_w_in, v_gdn_conv_w, v_gdn_a_log, v_gdn_dt_bias, v_gdn_norm_g, v_gdn_w_out, v_mlp_w_up, v_mlp_w_down]))
    t = x.shape[1]
    ix, iy, ic = lax.axis_index("x"), lax.axis_index("y"), lax.axis_index("c")
    me = 4 * ix + 2 * iy + ic
    chip = 2 * ix + iy
    ada_cols = ada_w.shape[2]

    packed = _pack_weights(w)
    zero = jnp.zeros((), jnp.int32)
    gathered = lax.dynamic_update_slice(_chip_gather(packed, name="gather_weights"), packed[None], (chip, zero, zero))
    full = _unpack_weights(gathered, w)

    cond8 = _silu_rows(jnp.pad(c, ((0, 7), (0, 0))), name="cond_silu")
    cond16 = jnp.pad(_all8_gather(cond8, name="gather_cond")[:, 0, :], ((0, 8), (0, 0)))
    mod_cols = []
    for layer in range(DEPTH):
        bias = jnp.broadcast_to(lax.dynamic_slice_in_dim(ada_b[layer], chip * ada_cols, ada_cols)[None], (16, ada_cols))
        mod_cols.append(_mm(cond16, ada_w[layer], epi="add", aux=bias, name=f"ada{layer}")[:8])
    mod_all = _all8_gather(jnp.stack(mod_cols).reshape(DEPTH * 8, ada_cols), name="gather_mod")
    mod = jnp.concatenate([lax.dynamic_slice_in_dim(mod_all[2 * j].reshape(DEPTH, 8, ada_cols), me, 1, axis=1)[:, 0]
                           for j in range(4)], axis=1)

    def layer_args(layer):
        kind, j = MIXERS[layer % 3], layer // 3
        mods = [mod[layer, i * D_MODEL:(i + 1) * D_MODEL][None] for i in range(N_MOD)]
        gains = (full["norm_pre_g"][layer, 0:1], full["norm_pre_g"][layer, 1:2], full["norm_post_g"][layer, 0:1],
                 full["norm_post_g"][layer, 1:2])
        return kind, j, mods, gains, _mixer_weights(kind, j, full, w)

    xs = x[0]
    kept = []
    for layer in range(DEPTH):
        kind, j, mods, gains, mw = layer_args(layer)
        xs, keep = _layer_fwd(xs, mods, gains, kind, mw, full["mlp_w_up"][layer], full["mlp_w_down"][layer], positions[0], f"l{layer}")
        kept.append(keep)
    loss_row, g = _loss_head(xs, loss_target[0], name="loss_head")
    loss = lax.psum(loss_row[0, 0], ("x", "y", "c"))

    grads = {n: [None] * w[n].shape[0] for n, _ in PACK_BF16[N_DIRECT:] + PACK_F32}
    rep_grads = {}
    dmods = [None] * DEPTH
    layout = _grad_layout(w)
    row_of = lambda n, idx: layout[n][0] + idx * layout[n][1]
    gf = lax.empty((4, FLAT_ROWS, FLAT_W), BF16)
    for layer in reversed(range(DEPTH)):
        kind, j, mods, gains, mw = layer_args(layer)
        rows = (row_of("mlp_w_up", layer), row_of("mlp_w_down", layer), row_of(kind + "_w_out", j))
        g, dmods[layer], dpre, dpost, mg, gf = _layer_bwd(
            g, kept[layer], mods, gains, kind, mw, full["mlp_w_up"][layer], full["mlp_w_down"][layer], f"l{layer}", gf, rows)
        grads["norm_pre_g"][layer], grads["norm_post_g"][layer] = dpre, dpost
        for key, val in mg.items():
            name = kind + "_" + key
            if name in grads:
                grads[name][j] = val
            elif name in REPLICATED_SMALL:
                rep_grads[name] = val[None]

    rep_flat = jnp.concatenate([rep_grads[n].reshape(-1) for n in REPLICATED_SMALL])
    dbuf = jnp.concatenate([jnp.concatenate(dmods, axis=0), jnp.pad(rep_flat, (0, N_MOD * D_MODEL - rep_flat.shape[0]))[None],
                            jnp.zeros((3, N_MOD * D_MODEL), F32)], axis=0)
    dall = _all8_gather(dbuf, name="gather_dmod")
    dsum = _sum_slots(dall, name="sum_dmod")
    out_grads = {"ada_b": dsum[:DEPTH]}
    off = 0
    for n in REPLICATED_SMALL:
        out_grads[n] = dsum[DEPTH, off:off + w[n].size].reshape(w[n].shape)
        off += w[n].size
    dada = []
    for layer in range(DEPTH):
        dm16 = jnp.pad(lax.dynamic_slice_in_dim(dall[:, layer, :], chip * ada_cols, ada_cols, axis=1), ((0, 8), (0, 0)))
        dada.append(_mm(cond16, dm16, ta=True, name=f"dada{layer}"))
    out_grads["ada_w"] = jnp.stack(dada)

    reduced = _reduce_scatter(_pack_grads(gf, grads, layout), ic.reshape(1).astype(jnp.int32),
                              chip.reshape(1).astype(jnp.int32), tag="reduce_grads")
    out_grads.update(_unpack_grads(reduced, w))

    deltas, new_m, new_v = {}, {}, {}
    for n in WEIGHTS:
        deltas[n], new_m[n], new_v[n] = _adamw(w[n], out_grads[n], m[n], v[n], name="adamw_" + n)
    return (loss, g[None], *[out_grads[n] for n in WEIGHTS], *[deltas[n] for n in WEIGHTS],
            *[new_m[n] for n in WEIGHTS], *[new_v[n] for n in WEIGHTS])
```

```python
import functools
import math

import jax
import jax.numpy as jnp
from jax import lax
from jax.experimental import pallas as pl
from jax.experimental.pallas import tpu as pltpu

F32 = jnp.float32
BF16 = jnp.bfloat16

D_MODEL = 1024
DEPTH = 4
CHUNK = 64
EPS = 1e-6
NEG_INF = -1e30
N_MOD = 6

GLA_HEADS, GLA_DK, GLA_DV, GLA_RANK = 4, 128, 256, 16
GLA_KW, GLA_VW = GLA_HEADS * GLA_DK, GLA_HEADS * GLA_DV
GLA_IN = 2 * GLA_KW + 2 * GLA_VW + GLA_RANK
GLA_IN_PAD = 3200

MLA_HEADS, MLA_NOPE, MLA_ROPE, MLA_V = 16, 64, 32, 64
MLA_Q_RANK, MLA_KV_RANK = 384, 256
MLA_IN = MLA_Q_RANK + MLA_KV_RANK + MLA_ROPE
MLA_IN_PAD = 768
ROPE_THETA = 10000.0
MLA_QK = MLA_NOPE + MLA_ROPE
LANES = 128

GDN_K_HEADS, GDN_V_HEADS, GDN_DK, GDN_DV, GDN_CONV = 8, 16, 128, 128, 4
GDN_QKW, GDN_VW = GDN_K_HEADS * GDN_DK, GDN_V_HEADS * GDN_DV
GDN_CONV_W = 2 * GDN_QKW + GDN_VW
GDN_IN = GDN_CONV_W + GDN_VW + 2 * GDN_V_HEADS
GDN_IN_PAD = 6400

ADAM_LR, ADAM_B1, ADAM_B2, ADAM_EPS, ADAM_WD, ADAM_STEP = 0.001, 0.9, 0.999, 1e-08, 0.01, 10

VMEM_LIMIT = 56 * 1024 * 1024

NN = ((1,), (0,))
NT = ((1,), (1,))
TN = ((0,), (0,))


def _cp(*sem):
    return pltpu.CompilerParams(dimension_semantics=sem, vmem_limit_bytes=VMEM_LIMIT)


def _pick(n, cands):
    for c in cands:
        if n % c == 0:
            return c
    return n


def _dg(a, b, dims=NN):
    return lax.dot_general(a.astype(BF16), b.astype(BF16), (dims, ((), ())), preferred_element_type=F32)


def _dot3(a, b, dims=NN):
    ah = a.astype(BF16)
    al = (a - ah.astype(F32)).astype(BF16)
    bh = b.astype(BF16)
    bl = (b - bh.astype(F32)).astype(BF16)
    d = lambda u, v: lax.dot_general(u, v, (dims, ((), ())), preferred_element_type=F32)
    return d(ah, bh) + (d(ah, bl) + d(al, bh))


def _sigmoid(x):
    return 1.0 / (1.0 + jnp.exp(-x))


def _softplus(x):
    return jnp.maximum(x, 0.0) + jnp.log(1.0 + jnp.exp(-jnp.abs(x)))


def _iota2(shape, dim):
    return lax.broadcasted_iota(jnp.int32, shape, dim)


def _mm(a, b, *, ta=False, tb=False, out_dtype=F32, epi=None, aux=None, into=None, name):
    m = a.shape[1] if ta else a.shape[0]
    k = a.shape[0] if ta else a.shape[1]
    n = b.shape[0] if tb else b.shape[1]
    assert k == (b.shape[1] if tb else b.shape[0]), (a.shape, b.shape, ta, tb)
    m_tile = m // 4 if into is not None and into[2] == "rows" else m
    tm = _pick(m_tile, (1024, 512, 384, 256, 128))
    tn = _pick(n, (1024, 640, 512, 768, 384, 256, 128))
    tk = _pick(k, (1024, 640, 512, 768, 384, 256, 128))
    nk = k // tk
    dims = ((0 if ta else 1,), (1 if tb else 0,))

    def finish(r, x_ref, o_ref):
        if epi == "relu2":
            r = jnp.square(jnp.maximum(r, 0.0))
        elif epi == "dact":
            r = r * (2.0 * jnp.sqrt(x_ref[...].astype(F32)))
        elif epi == "add":
            r = r + x_ref[...]
        o_ref[...] = r.astype(out_dtype)

    n_in = 2 + (aux is not None) + (into is not None)

    def body(*refs):
        a_ref, b_ref = refs[:2]
        x_ref = refs[2] if aux is not None else None
        o_ref = refs[n_in]
        if nk == 1:
            finish(_dg(a_ref[...], b_ref[...], dims), x_ref, o_ref)
            return
        acc = refs[-1]
        kk = pl.program_id(2)

        @pl.when(kk == 0)
        def _():
            acc[...] = jnp.zeros_like(acc)

        acc[...] += _dg(a_ref[...], b_ref[...], dims)

        @pl.when(kk == nk - 1)
        def _():
            finish(acc[...], x_ref, o_ref)

    a_spec = pl.BlockSpec((tk, tm), lambda i, j, q: (q, i)) if ta else pl.BlockSpec((tm, tk), lambda i, j, q: (i, q))
    b_spec = pl.BlockSpec((tn, tk), lambda i, j, q: (j, q)) if tb else pl.BlockSpec((tk, tn), lambda i, j, q: (q, j))
    o_spec = pl.BlockSpec((tm, tn), lambda i, j, q: (i, j))
    in_specs = [a_spec, b_spec] + ([o_spec] if aux is not None else [])
    args = (a, b) + ((aux,) if aux is not None else ())
    out_shape = jax.ShapeDtypeStruct((m, n), out_dtype)
    aliases = {}
    if into is not None:
        dst, row0, axis = into
        assert dst.dtype == out_dtype and row0 % tm == 0 and tn == FLAT_W and n == (FLAT_W if axis == "rows" else 4 * FLAT_W)
        per = m_tile // tm
        if axis == "rows":
            o_spec = pl.BlockSpec((None, tm, tn), lambda i, j, q: (i // per, row0 // tm + i % per, 0))
        else:
            o_spec = pl.BlockSpec((None, tm, tn), lambda i, j, q: (j, row0 // tm + i, 0))
        in_specs.append(pl.BlockSpec(memory_space=pl.ANY))
        args += (dst,)
        out_shape = jax.ShapeDtypeStruct(dst.shape, dst.dtype)
        aliases = {n_in - 1: 0}
    return pl.pallas_call(
        body, grid=(m // tm, n // tn, nk), in_specs=in_specs, out_specs=o_spec, out_shape=out_shape,
        scratch_shapes=[pltpu.VMEM((tm, tn), F32)] if nk > 1 else [], input_output_aliases=aliases,
        compiler_params=_cp("parallel", "parallel", "arbitrary"), name=name)(*args)


def _row_tile(t):
    return _pick(t, (1024, 512, 256, 128, 64, 8))


def _premod_fwd(x, g, sc, sh, *, name):
    t, c = x.shape
    tr = _row_tile(t)

    def body(x_ref, g_ref, sc_ref, sh_ref, h_ref):
        xv = x_ref[...]
        r = lax.rsqrt(jnp.mean(xv * xv, axis=-1, keepdims=True) + EPS)
        h_ref[...] = (((xv * r) * g_ref[...]) * (1.0 + sc_ref[...]) + sh_ref[...]).astype(BF16)

    row = pl.BlockSpec((tr, c), lambda i: (i, 0))
    vec = pl.BlockSpec((1, c), lambda i: (0, 0))
    return pl.pallas_call(body, grid=(t // tr,), in_specs=[row, vec, vec, vec], out_specs=row,
                          out_shape=jax.ShapeDtypeStruct((t, c), BF16), compiler_params=_cp("parallel"), name=name)(x, g, sc, sh)


def _premod_bwd(dh, x, g, sc, gin, *, name):
    t, c = x.shape
    tr = _row_tile(t)

    def body(dh_ref, x_ref, g_ref, sc_ref, gin_ref, gout_ref, dg_ref, dsc_ref, dsh_ref):
        @pl.when(pl.program_id(0) == 0)
        def _():
            dg_ref[...] = jnp.zeros_like(dg_ref)
            dsc_ref[...] = jnp.zeros_like(dsc_ref)
            dsh_ref[...] = jnp.zeros_like(dsh_ref)

        xv = x_ref[...]
        dhv = dh_ref[...].astype(F32)
        gv = g_ref[...]
        one_sc = 1.0 + sc_ref[...]
        r = lax.rsqrt(jnp.mean(xv * xv, axis=-1, keepdims=True) + EPS)
        nv = xv * r
        dsh_ref[...] += jnp.sum(dhv, axis=0, keepdims=True)
        dsc_ref[...] += jnp.sum(dhv * (nv * gv), axis=0, keepdims=True)
        dg_ref[...] += jnp.sum(dhv * nv * one_sc, axis=0, keepdims=True)
        dn = dhv * gv * one_sc
        dx = r * (dn - nv * jnp.mean(dn * nv, axis=-1, keepdims=True))
        gout_ref[...] = gin_ref[...] + dx

    row = pl.BlockSpec((tr, c), lambda i: (i, 0))
    vec = pl.BlockSpec((1, c), lambda i: (0, 0))
    vs = jax.ShapeDtypeStruct((1, c), F32)
    return pl.pallas_call(body, grid=(t // tr,), in_specs=[row, row, vec, vec, row], out_specs=[row, vec, vec, vec],
                          out_shape=[jax.ShapeDtypeStruct((t, c), F32), vs, vs, vs],
                          compiler_params=_cp("arbitrary"), name=name)(dh, x, g, sc, gin)


def _postres_fwd(x, y, g, gt, *, name):
    t, c = x.shape
    tr = _row_tile(t)

    def body(x_ref, y_ref, g_ref, gt_ref, o_ref):
        yv = y_ref[...]
        r = lax.rsqrt(jnp.mean(yv * yv, axis=-1, keepdims=True) + EPS)
        o_ref[...] = x_ref[...] + gt_ref[...] * ((yv * r) * g_ref[...])

    row = pl.BlockSpec((tr, c), lambda i: (i, 0))
    vec = pl.BlockSpec((1, c), lambda i: (0, 0))
    return pl.pallas_call(body, grid=(t // tr,), in_specs=[row, row, vec, vec], out_specs=row,
                          out_shape=jax.ShapeDtypeStruct((t, c), F32), compiler_params=_cp("parallel"), name=name)(x, y, g, gt)


def _postres_bwd(gout, y, g, gt, *, name):
    t, c = y.shape
    tr = _row_tile(t)

    def body(go_ref, y_ref, g_ref, gt_ref, dy_ref, dg_ref, dgt_ref):
        @pl.when(pl.program_id(0) == 0)
        def _():
            dg_ref[...] = jnp.zeros_like(dg_ref)
            dgt_ref[...] = jnp.zeros_like(dgt_ref)

        yv = y_ref[...]
        gov = go_ref[...]
        gv = g_ref[...]
        gtv = gt_ref[...]
        r = lax.rsqrt(jnp.mean(yv * yv, axis=-1, keepdims=True) + EPS)
        z = yv * r
        dgt_ref[...] += jnp.sum(gov * (z * gv), axis=0, keepdims=True)
        dg_ref[...] += jnp.sum(gov * gtv * z, axis=0, keepdims=True)
        dz = gov * gtv * gv
        dy_ref[...] = (r * (dz - z * jnp.mean(dz * z, axis=-1, keepdims=True))).astype(BF16)

    row = pl.BlockSpec((tr, c), lambda i: (i, 0))
    vec = pl.BlockSpec((1, c), lambda i: (0, 0))
    vs = jax.ShapeDtypeStruct((1, c), F32)
    return pl.pallas_call(body, grid=(t // tr,), in_specs=[row, row, vec, vec], out_specs=[row, vec, vec],
                          out_shape=[jax.ShapeDtypeStruct((t, c), BF16), vs, vs],
                          compiler_params=_cp("arbitrary"), name=name)(gout, y, g, gt)


def _loss_head(y, tgt, *, name):
    t, c = y.shape
    tr = _row_tile(t)

    def body(y_ref, t_ref, l_ref, dy_ref):
        @pl.when(pl.program_id(0) == 0)
        def _():
            l_ref[...] = jnp.zeros_like(l_ref)

        d = y_ref[...] - t_ref[...]
        dy_ref[...] = d * (1.0 / c)
        l_ref[...] += 0.5 * jnp.sum(jnp.mean(d * d, axis=-1, keepdims=True))

    row = pl.BlockSpec((tr, c), lambda i: (i, 0))
    return pl.pallas_call(body, grid=(t // tr,), in_specs=[row, row],
                          out_specs=[pl.BlockSpec((1, LANES), lambda i: (0, 0)), row],
                          out_shape=[jax.ShapeDtypeStruct((1, LANES), F32), jax.ShapeDtypeStruct((t, c), F32)],
                          compiler_params=_cp("arbitrary"), name=name)(y, tgt)


def _adamw(w, g, m, v, *, name):
    shape = w.shape
    c = shape[-1]
    r = math.prod(shape[:-1])
    w2, g2, m2, v2 = (a.reshape(r, c) for a in (w, g, m, v))
    tr = r
    for cand in (1024, 512, 256, 128, 64, 32, 16, 8):
        if r % cand == 0 and cand * c * 4 <= (1 << 20):
            tr = cand
            break
    c1 = 1.0 - ADAM_B1 ** ADAM_STEP
    c2 = 1.0 - ADAM_B2 ** ADAM_STEP

    def body(w_ref, g_ref, m_ref, v_ref, d_ref, nm_ref, nv_ref):
        gv = g_ref[...]
        mn = ADAM_B1 * m_ref[...] + (1.0 - ADAM_B1) * gv
        vn = ADAM_B2 * v_ref[...] + (1.0 - ADAM_B2) * jnp.square(gv)
        m_hat = mn / c1
        v_hat = vn / c2
        d_ref[...] = -ADAM_LR * (m_hat / (jnp.sqrt(v_hat) + ADAM_EPS) + ADAM_WD * w_ref[...])
        nm_ref[...] = mn
        nv_ref[...] = vn

    blk = pl.BlockSpec((tr, c), lambda i: (i, 0))
    s = jax.ShapeDtypeStruct((r, c), F32)
    d, nm, nv = pl.pallas_call(body, grid=(r // tr,), in_specs=[blk] * 4, out_specs=[blk] * 3, out_shape=[s, s, s],
                               compiler_params=_cp("parallel"), name=name)(w2, g2, m2, v2)
    return d.reshape(shape), nm.reshape(shape), nv.reshape(shape)


def _gla_parts(p_ref, wg_ref, bg_ref):
    q = p_ref[:, 0:GLA_KW] * (GLA_DK ** -0.5)
    k = p_ref[:, GLA_KW:2 * GLA_KW]
    glr = p_ref[:, 2 * GLA_KW + 2 * GLA_VW:GLA_IN_PAD]
    gate = _dg(glr, wg_ref[...]) + bg_ref[...]
    log_a = (jnp.minimum(gate, 0.0) - jnp.log(1.0 + jnp.exp(-jnp.abs(gate)))) * (1.0 / 16.0)
    tril = (_iota2((CHUNK, CHUNK), 0) >= _iota2((CHUNK, CHUNK), 1)).astype(F32)
    cum = _dot_sel(tril, log_a)
    c_last = cum[CHUNK - 1:CHUNK, :]
    f = jnp.exp(c_last - cum)
    dec = jnp.exp(c_last)
    return q, k, glr, gate, f, k * f, dec


def _gla_fwd(p, wg, bg, hg, *, name):
    t = p.shape[0]
    nc = t // CHUNK

    def body(p_ref, wg_ref, bg_ref, hg_ref, og_ref, s_ref, st):
        @pl.when(pl.program_id(0) == 0)
        def _():
            st[...] = jnp.zeros_like(st)

        q, _, _, _, _, ke, dec = _gla_parts(p_ref, wg_ref, bg_ref)
        hs = range(GLA_HEADS)
        ks = [slice(h * GLA_DK, (h + 1) * GLA_DK) for h in hs]
        vs = [slice(2 * GLA_KW + h * GLA_DV, 2 * GLA_KW + (h + 1) * GLA_DV) for h in hs]
        rs = [slice(2 * GLA_KW + GLA_VW + h * GLA_DV, 2 * GLA_KW + GLA_VW + (h + 1) * GLA_DV) for h in hs]
        s_new = [st[h] * dec[:, ks[h]] + _dg(p_ref[:, vs[h]], ke[:, ks[h]], TN) for h in hs]
        o = [_dg(q[:, ks[h]], s_new[h], NT) for h in hs]
        for h in hs:
            st[h] = s_new[h]
            s_ref[0, h] = s_new[h]
            rn = lax.rsqrt(jnp.mean(o[h] * o[h], axis=-1, keepdims=True) + EPS)
            rv = p_ref[:, rs[h]]
            og_ref[:, h * GLA_DV:(h + 1) * GLA_DV] = (((o[h] * rn) * hg_ref[...]) * (rv * _sigmoid(rv))).astype(BF16)

    full = lambda a: pl.BlockSpec(a.shape, lambda n: (0,) * a.ndim)
    return pl.pallas_call(
        body, grid=(nc,),
        in_specs=[pl.BlockSpec((CHUNK, GLA_IN_PAD), lambda n: (n, 0)), full(wg), full(bg), full(hg)],
        out_specs=[pl.BlockSpec((CHUNK, GLA_VW), lambda n: (n, 0)),
                   pl.BlockSpec((1, GLA_HEADS, GLA_DV, GLA_DK), lambda n: (n, 0, 0, 0))],
        out_shape=[jax.ShapeDtypeStruct((t, GLA_VW), BF16), jax.ShapeDtypeStruct((nc, GLA_HEADS, GLA_DV, GLA_DK), F32)],
        scratch_shapes=[pltpu.VMEM((GLA_HEADS, GLA_DV, GLA_DK), F32)],
        compiler_params=_cp("arbitrary"), name=name)(p, wg, bg, hg)


def _gla_bwd(p, dog, sall, wg, bg, hg, *, name):
    t = p.shape[0]
    nc = t // CHUNK

    def body(p_ref, dog_ref, s1_ref, s0_ref, wg_ref, bg_ref, hg_ref, dp_ref, dwg_ref, dbg_ref, dhg_ref, gt):
        i = pl.program_id(0)

        @pl.when(i == 0)
        def _():
            gt[...] = jnp.zeros_like(gt)
            dwg_ref[...] = jnp.zeros_like(dwg_ref)
            dbg_ref[...] = jnp.zeros_like(dbg_ref)
            dhg_ref[...] = jnp.zeros_like(dhg_ref)

        has_prev = (i < nc - 1).astype(F32)
        q, k, glr, gate, f, ke, dec = _gla_parts(p_ref, wg_ref, bg_ref)
        hgv = hg_ref[...]
        hs = range(GLA_HEADS)
        ks = [slice(h * GLA_DK, (h + 1) * GLA_DK) for h in hs]
        vs = [slice(2 * GLA_KW + h * GLA_DV, 2 * GLA_KW + (h + 1) * GLA_DV) for h in hs]
        rs = [slice(2 * GLA_KW + GLA_VW + h * GLA_DV, 2 * GLA_KW + GLA_VW + (h + 1) * GLA_DV) for h in hs]
        s1 = [s1_ref[0, h] for h in hs]
        o = [_dg(q[:, ks[h]], s1[h], NT) for h in hs]
        dhg = jnp.zeros((1, GLA_DV), F32)
        do = []
        for h in hs:
            rv = p_ref[:, rs[h]]
            rn = lax.rsqrt(jnp.mean(o[h] * o[h], axis=-1, keepdims=True) + EPS)
            z = o[h] * rn
            sg = _sigmoid(rv)
            sl = rv * sg
            dogh = dog_ref[:, h * GLA_DV:(h + 1) * GLA_DV].astype(F32)
            dhg = dhg + jnp.sum(dogh * z * sl, axis=0, keepdims=True)
            dp_ref[:, rs[h]] = (dogh * (z * hgv) * (sg * (1.0 + rv * (1.0 - sg)))).astype(BF16)
            dz = dogh * sl * hgv
            do.append(rn * (dz - z * jnp.mean(dz * z, axis=-1, keepdims=True)))
        dhg_ref[...] += dhg
        g_tot = [gt[h] + _dg(do[h], q[:, ks[h]], TN) for h in hs]
        dq = [_dg(do[h], s1[h], NN) for h in hs]
        dke_parts = [_dg(p_ref[:, vs[h]], g_tot[h], NN) for h in hs]
        dv = [_dg(ke[:, ks[h]], g_tot[h], NT) for h in hs]
        ddec_parts = []
        for h in hs:
            dp_ref[:, ks[h]] = (dq[h] * (GLA_DK ** -0.5)).astype(BF16)
            dp_ref[:, vs[h]] = dv[h].astype(BF16)
            ddec_parts.append(jnp.sum(g_tot[h] * (s0_ref[0, h] * has_prev), axis=0, keepdims=True))
            gt[h] = g_tot[h] * dec[:, ks[h]]
        dke = jnp.concatenate(dke_parts, axis=1)
        ddec = jnp.concatenate(ddec_parts, axis=1)
        dp_ref[:, GLA_KW:2 * GLA_KW] = (dke * f).astype(BF16)
        stril = (_iota2((CHUNK, CHUNK), 0) > _iota2((CHUNK, CHUNK), 1)).astype(F32)
        dlog_a = _dot_sel(stril, dke * ke) + ddec * dec
        dgate = dlog_a * (1.0 / 16.0) * _sigmoid(-gate)
        dp_ref[:, 2 * GLA_KW + 2 * GLA_VW:GLA_IN_PAD] = _dg(dgate, wg_ref[...], NT).astype(BF16)
        dwg_ref[...] += _dg(glr, dgate, TN)
        dbg_ref[...] += jnp.sum(dgate, axis=0, keepdims=True)

    full = lambda a: pl.BlockSpec(a.shape, lambda n: (0,) * a.ndim)
    sblk = (1, GLA_HEADS, GLA_DV, GLA_DK)
    return pl.pallas_call(
        body, grid=(nc,),
        in_specs=[pl.BlockSpec((CHUNK, GLA_IN_PAD), lambda n: (nc - 1 - n, 0)),
                  pl.BlockSpec((CHUNK, GLA_VW), lambda n: (nc - 1 - n, 0)),
                  pl.BlockSpec(sblk, lambda n: (nc - 1 - n, 0, 0, 0)),
                  pl.BlockSpec(sblk, lambda n: (jnp.maximum(nc - 2 - n, 0), 0, 0, 0)),
                  full(wg), full(bg), full(hg)],
        out_specs=[pl.BlockSpec((CHUNK, GLA_IN_PAD), lambda n: (nc - 1 - n, 0)), full(wg), full(bg), full(hg)],
        out_shape=[jax.ShapeDtypeStruct((t, GLA_IN_PAD), BF16), jax.ShapeDtypeStruct(wg.shape, F32),
                   jax.ShapeDtypeStruct(bg.shape, F32), jax.ShapeDtypeStruct(hg.shape, F32)],
        scratch_shapes=[pltpu.VMEM((GLA_HEADS, GLA_DV, GLA_DK), F32)],
        compiler_params=_cp("arbitrary"), name=name)(p, dog, sall, sall, wg, bg, hg)


def _gla_layer_fwd(h, w, tag):
    p = _mm(h, w["w_in"], name=tag + "_in")
    og, sall = _gla_fwd(p, w["wg"], w["bg"], w["hg"], name=tag + "_scan")
    y = _mm(og, w["w_out"], name=tag + "_out")
    return y, (p, og, sall)


def _dw_out(act, dy, sink, name):
    if sink is None:
        return _mm(act, dy, ta=True, name=name), None
    return None, _mm(act, dy, ta=True, out_dtype=sink[0].dtype, into=(sink[0], sink[1], "rows"), name=name)


def _gla_layer_bwd(dy, h, saved, w, tag, sink=None):
    p, og, sall = saved
    dog = _mm(dy, w["w_out"], tb=True, name=tag + "_dog")
    dw_out, gf = _dw_out(og, dy, sink, tag + "_dwout")
    dp, dwg, dbg, dhg = _gla_bwd(p, dog, sall, w["wg"], w["bg"], w["hg"], name=tag + "_scanb")
    dw_in = _mm(h, dp, ta=True, out_dtype=BF16, name=tag + "_dwin")
    dh = _mm(dp, w["w_in"], tb=True, out_dtype=BF16, name=tag + "_dh")
    grads = dict(w_in=dw_in[:, :GLA_IN], w_gate_up=dwg[:GLA_RANK], b_gate=dbg[0], head_g=dhg[0], w_out=dw_out)
    return dh, grads, gf


def _rope_tables(pos, inv_freq, *, name):
    t = pos.shape[0]
    tr = _row_tile(t)
    half = MLA_ROPE // 2

    def body(p_ref, f_ref, c_ref, s1_ref, s2_ref, s1b_ref, s2b_ref):
        ang = p_ref[...].astype(F32) * f_ref[...]
        lane = _iota2((tr, LANES), 1)
        lo = (lane >= MLA_NOPE) & (lane < MLA_NOPE + half)
        hi = (lane >= MLA_NOPE + half) & (lane < MLA_QK)
        cs, sn = jnp.cos(ang), jnp.sin(ang)
        zero = jnp.zeros_like(cs)
        c_ref[...] = jnp.where(lane < MLA_NOPE, 1.0, jnp.where(lane < MLA_QK, cs, 0.0))
        s1_ref[...] = jnp.where(lo, -sn, zero)
        s2_ref[...] = jnp.where(hi, sn, zero)
        s1b_ref[...] = jnp.where(lo, sn, zero)
        s2b_ref[...] = jnp.where(hi, -sn, zero)

    row = pl.BlockSpec((tr, LANES), lambda i: (i, 0))
    s = jax.ShapeDtypeStruct((t, LANES), F32)
    return pl.pallas_call(body, grid=(t // tr,),
                          in_specs=[pl.BlockSpec((tr, 1), lambda i: (i, 0)), pl.BlockSpec((1, LANES), lambda i: (0, 0))],
                          out_specs=[row] * 5, out_shape=[s] * 5, compiler_params=_cp("parallel"), name=name)(pos, inv_freq)


def _rope(x, c, s1, s2, *, out_dtype, sum_heads=False, name):
    nh, t, _ = x.shape
    tr = _row_tile(t)
    half = MLA_ROPE // 2

    def body(x_ref, c_ref, s1_ref, s2_ref, o_ref):
        total = None
        for h in range(nh):
            xv = x_ref[h].astype(F32)
            y = xv * c_ref[...] + pltpu.roll(xv, LANES - half, 1) * s1_ref[...] + pltpu.roll(xv, half, 1) * s2_ref[...]
            if sum_heads:
                total = y if total is None else total + y
            else:
                o_ref[h] = y.astype(out_dtype)
        if sum_heads:
            o_ref[...] = total

    tab = pl.BlockSpec((tr, LANES), lambda i: (i, 0))
    xs = pl.BlockSpec((nh, tr, LANES), lambda i: (0, i, 0))
    if sum_heads:
        return pl.pallas_call(body, grid=(t // tr,), in_specs=[xs, tab, tab, tab], out_specs=tab,
                              out_shape=jax.ShapeDtypeStruct((t, LANES), F32),
                              compiler_params=_cp("parallel"), name=name)(x, c, s1, s2)
    return pl.pallas_call(body, grid=(t // tr,), in_specs=[xs, tab, tab, tab], out_specs=xs,
                          out_shape=jax.ShapeDtypeStruct(x.shape, out_dtype),
                          compiler_params=_cp("parallel"), name=name)(x, c, s1, s2)


FLASH_BLK = 512


def _diag_mask(blk):
    return (_iota2((blk, blk), 1) // CHUNK) <= (_iota2((blk, blk), 0) // CHUNK)


def _flash_fwd(q, k, v, *, name):
    nh, t, _ = q.shape
    blk = min(FLASH_BLK, t)
    scale = MLA_QK ** -0.5

    def body(q_ref, k_ref, v_ref, o_ref, lse_ref):
        i = pl.program_id(1)
        qv = q_ref[0]

        def step(j, carry, masked):
            m, l, acc = carry
            off = pl.multiple_of(j * blk, blk)
            kb = k_ref[0, pl.ds(off, blk), :]
            vb = v_ref[0, pl.ds(off, blk), :]
            s = _dg(qv, kb, NT) * scale
            if masked:
                s = jnp.where(_diag_mask(blk), s, NEG_INF)
            m_new = jnp.maximum(m, jnp.max(s, axis=-1, keepdims=True))
            p = jnp.exp(s - m_new)
            alpha = jnp.exp(m - m_new)
            return m_new, alpha * l + jnp.sum(p, axis=-1, keepdims=True), alpha * acc + _dg(p, vb, NN)

        init = (jnp.full((blk, 1), NEG_INF, F32), jnp.zeros((blk, 1), F32), jnp.zeros((blk, MLA_V), F32))
        carry = lax.fori_loop(0, i, lambda j, c: step(j, c, False), init)
        m, l, acc = step(i, carry, True)
        o_ref[0] = (acc / l).astype(BF16)
        lse_ref[0] = m + jnp.log(l)

    qs = pl.BlockSpec((1, blk, LANES), lambda h, i: (h, i, 0))
    return pl.pallas_call(
        body, grid=(nh, t // blk),
        in_specs=[qs, pl.BlockSpec((1, t, LANES), lambda h, i: (h, 0, 0)), pl.BlockSpec((1, t, MLA_V), lambda h, i: (h, 0, 0))],
        out_specs=[pl.BlockSpec((1, blk, MLA_V), lambda h, i: (h, i, 0)), pl.BlockSpec((1, blk, 1), lambda h, i: (h, i, 0))],
        out_shape=[jax.ShapeDtypeStruct((nh, t, MLA_V), BF16), jax.ShapeDtypeStruct((nh, t, 1), F32)],
        compiler_params=_cp("parallel", "parallel"), name=name)(q, k, v)


def _flash_bwd(q, k, v, do, o, lse, *, name):
    nh, t, _ = q.shape
    blk = min(FLASH_BLK, t)
    nq = t // blk
    scale = MLA_QK ** -0.5

    def body(q_ref, k_ref, v_ref, do_ref, o_ref, lse_ref, dq_ref, dk_ref, dv_ref, dl):
        j = pl.program_id(1)

        @pl.when(j == 0)
        def _():
            dq_ref[...] = jnp.zeros_like(dq_ref)
            dl[...] = jnp.sum(do_ref[0].astype(F32) * o_ref[0].astype(F32), axis=-1, keepdims=True)

        kb = k_ref[0]
        vb = v_ref[0]

        def step(i, carry, masked):
            dk, dv = carry
            rows = pl.ds(pl.multiple_of(i * blk, blk), blk)
            qb = q_ref[0, rows, :]
            dob = do_ref[0, rows, :]
            s = _dg(qb, kb, NT) * scale
            if masked:
                s = jnp.where(_diag_mask(blk), s, NEG_INF)
            p = jnp.exp(s - lse_ref[0, rows, :])
            ds = p * (_dg(dob, vb, NT) - dl[rows, :]) * scale
            dq_ref[0, rows, :] += _dg(ds, kb, NN)
            return dk + _dg(ds, qb, TN), dv + _dg(p, dob, TN)

        carry = step(j, (jnp.zeros((blk, LANES), F32), jnp.zeros((blk, MLA_V), F32)), True)
        dk, dv = lax.fori_loop(j + 1, nq, lambda i, c: step(i, c, False), carry)
        dk_ref[0] = dk
        dv_ref[0] = dv

    ks = pl.BlockSpec((1, blk, LANES), lambda h, j: (h, j, 0))
    vs = pl.BlockSpec((1, blk, MLA_V), lambda h, j: (h, j, 0))
    fl = lambda w: pl.BlockSpec((1, t, w), lambda h, j: (h, 0, 0))
    return pl.pallas_call(
        body, grid=(nh, nq),
        in_specs=[fl(LANES), ks, vs, fl(MLA_V), fl(MLA_V), fl(1)],
        out_specs=[fl(LANES), ks, vs],
        out_shape=[jax.ShapeDtypeStruct((nh, t, LANES), F32), jax.ShapeDtypeStruct((nh, t, LANES), F32),
                   jax.ShapeDtypeStruct((nh, t, MLA_V), F32)],
        scratch_shapes=[pltpu.VMEM((t, 1), F32)],
        compiler_params=_cp("parallel", "arbitrary"), name=name)(q, k, v, do, o, lse)


def _heads_first(a, width):
    t = a.shape[0]
    return a.reshape(t, MLA_HEADS, width).transpose(1, 0, 2)


def _heads_last(a):
    return a.transpose(1, 0, 2).reshape(a.shape[1], -1)


def _mla_layer_fwd(h, pos, w, tag):
    t = h.shape[0]
    zq = jnp.zeros((1, MLA_Q_RANK), F32)
    zkv = jnp.zeros((1, MLA_KV_RANK), F32)
    p = _mm(h, w["w_in"], name=tag + "_in")
    cq, ckv, krp = p[:, :MLA_Q_RANK], p[:, MLA_Q_RANK:MLA_Q_RANK + MLA_KV_RANK], p[:, MLA_Q_RANK + MLA_KV_RANK:MLA_IN]
    qn = _premod_fwd(cq, w["q_norm_g"], zq, zq, name=tag + "_qnorm")
    kvn = _premod_fwd(ckv, w["kv_norm_g"], zkv, zkv, name=tag + "_kvnorm")
    q = _mm(qn, w["w_uq"], name=tag + "_uq")
    kv = _mm(kvn, w["w_ukv"], name=tag + "_ukv")
    q_pre = jnp.pad(_heads_first(q, MLA_QK), ((0, 0), (0, 0), (0, LANES - MLA_QK)))
    kv3 = _heads_first(kv, MLA_NOPE + MLA_V)
    k_pre = jnp.concatenate([kv3[:, :, :MLA_NOPE], jnp.broadcast_to(krp[None], (MLA_HEADS, t, MLA_ROPE)),
                             jnp.zeros((MLA_HEADS, t, LANES - MLA_QK), F32)], axis=-1)
    vh = kv3[:, :, MLA_NOPE:].astype(BF16)
    half = MLA_ROPE // 2
    freq = ROPE_THETA ** (-jnp.arange(half, dtype=F32) / half)
    inv_freq = jnp.concatenate([jnp.zeros((MLA_NOPE,), F32), freq, freq, jnp.zeros((LANES - MLA_QK,), F32)])[None]
    tabs = _rope_tables(pos.reshape(t, 1), inv_freq, name=tag + "_tables")
    qr = _rope(q_pre, tabs[0], tabs[1], tabs[2], out_dtype=BF16, name=tag + "_ropeq")
    kr = _rope(k_pre, tabs[0], tabs[1], tabs[2], out_dtype=BF16, name=tag + "_ropek")
    o, lse = _flash_fwd(qr, kr, vh, name=tag + "_attn")
    of = _heads_last(o)
    y = _mm(of, w["w_out"], name=tag + "_out")
    return y, (cq, ckv, qn, kvn, qr, kr, vh, o, lse, of, tabs)


def _mla_layer_bwd(dy, h, saved, w, tag, sink=None):
    cq, ckv, qn, kvn, qr, kr, vh, o, lse, of, tabs = saved
    t = h.shape[0]
    zq = jnp.zeros((1, MLA_Q_RANK), F32)
    zkv = jnp.zeros((1, MLA_KV_RANK), F32)
    dof = _mm(dy, w["w_out"], tb=True, out_dtype=BF16, name=tag + "_dof")
    dw_out, gf = _dw_out(of, dy, sink, tag + "_dwout")
    do = _heads_first(dof, MLA_V)
    dqr, dkr, dv = _flash_bwd(qr, kr, vh, do, o, lse, name=tag + "_attn_b")
    dq_pre = _rope(dqr, tabs[0], tabs[3], tabs[4], out_dtype=F32, name=tag + "_ropeq_b")
    dk_sum = _rope(dkr, tabs[0], tabs[3], tabs[4], out_dtype=F32, sum_heads=True, name=tag + "_ropek_b")
    dq = _heads_last(dq_pre[:, :, :MLA_QK])
    dkv = _heads_last(jnp.concatenate([dkr[:, :, :MLA_NOPE], dv], axis=-1))
    dw_uq = _mm(qn, dq, ta=True, out_dtype=BF16, name=tag + "_dwuq")
    dqn = _mm(dq, w["w_uq"], tb=True, name=tag + "_dqn")
    dw_ukv = _mm(kvn, dkv, ta=True, out_dtype=BF16, name=tag + "_dwukv")
    dkvn = _mm(dkv, w["w_ukv"], tb=True, name=tag + "_dkvn")
    dcq, dqg, _, _ = _premod_bwd(dqn, cq, w["q_norm_g"], zq, jnp.zeros_like(cq), name=tag + "_qnorm_b")
    dckv, dkvg, _, _ = _premod_bwd(dkvn, ckv, w["kv_norm_g"], zkv, jnp.zeros_like(ckv), name=tag + "_kvnorm_b")
    dp = jnp.concatenate([dcq, dckv, dk_sum[:, MLA_NOPE:MLA_QK], jnp.zeros((t, MLA_IN_PAD - MLA_IN), F32)], axis=1).astype(BF16)
    dw_in = _mm(h, dp, ta=True, out_dtype=BF16, name=tag + "_dwin")
    dh = _mm(dp, w["w_in"], tb=True, out_dtype=BF16, name=tag + "_dh")
    grads = dict(w_in=dw_in[:, :MLA_IN], q_norm_g=dqg[0], w_uq=dw_uq, kv_norm_g=dkvg[0], w_ukv=dw_ukv, w_out=dw_out)
    return dh, grads, gf


CONV_HALO = 8


def _conv_tiles(t):
    return min(512, t), 1024


def _gdn_conv_fwd(p, w, *, name):
    t = p.shape[0]
    tr, tc = _conv_tiles(t)
    hb = tr // CONV_HALO

    def body(x_ref, halo_ref, w_ref, o_ref, buf):
        i = pl.program_id(0)
        buf[0:CONV_HALO, :] = halo_ref[...] * (i > 0).astype(F32)
        buf[CONV_HALO:CONV_HALO + tr, :] = x_ref[...]
        base = CONV_HALO - (GDN_CONV - 1)
        acc = buf[pl.ds(base, tr), :] * w_ref[0:1, :]
        for j in range(1, GDN_CONV):
            acc = acc + buf[pl.ds(base + j, tr), :] * w_ref[j:j + 1, :]
        o_ref[...] = acc * _sigmoid(acc)

    return pl.pallas_call(
        body, grid=(t // tr, GDN_CONV_W // tc),
        in_specs=[pl.BlockSpec((tr, tc), lambda i, j: (i, j)),
                  pl.BlockSpec((CONV_HALO, tc), lambda i, j: (jnp.maximum(i * hb - 1, 0), j)),
                  pl.BlockSpec((GDN_CONV, tc), lambda i, j: (0, j))],
        out_specs=pl.BlockSpec((tr, tc), lambda i, j: (i, j)),
        out_shape=jax.ShapeDtypeStruct((t, GDN_CONV_W), F32),
        scratch_shapes=[pltpu.VMEM((tr + CONV_HALO, tc), F32)],
        compiler_params=_cp("parallel", "parallel"), name=name)(p, p, w)


def _gdn_conv_bwd(d, p, w, col0, *, name):
    t = p.shape[0]
    tr, tc = _conv_tiles(t)
    hb = tr // CONV_HALO
    nr = t // tr
    ext = tr + CONV_HALO

    def body(x_ref, xp_ref, xn_ref, d_ref, dn_ref, w_ref, dx_ref, dw_ref, bufx, bufd):
        i = pl.program_id(1)

        @pl.when(i == 0)
        def _():
            dw_ref[...] = jnp.zeros_like(dw_ref)

        last = (i < nr - 1).astype(F32)
        bufx[0:CONV_HALO, :] = xp_ref[...] * (i > 0).astype(F32)
        bufx[CONV_HALO:CONV_HALO + tr, :] = x_ref[...]
        bufx[CONV_HALO + tr:, :] = xn_ref[...] * last
        base = CONV_HALO - (GDN_CONV - 1)
        acc = bufx[pl.ds(base, ext), :] * w_ref[0:1, :]
        for j in range(1, GDN_CONV):
            acc = acc + bufx[pl.ds(base + j, ext), :] * w_ref[j:j + 1, :]
        sg = _sigmoid(acc)
        dsilu = sg * (1.0 + acc * (1.0 - sg))
        bufd[0:tr, :] = d_ref[...] * dsilu[0:tr, :]
        bufd[tr:, :] = dn_ref[...] * last * dsilu[tr:, :]
        dx = bufd[pl.ds(GDN_CONV - 1, tr), :] * w_ref[0:1, :]
        for j in range(1, GDN_CONV):
            dx = dx + bufd[pl.ds(GDN_CONV - 1 - j, tr), :] * w_ref[j:j + 1, :]
        dx_ref[...] = dx.astype(BF16)
        dc = bufd[0:tr, :]
        for j in range(GDN_CONV):
            dw_ref[j:j + 1, :] += jnp.sum(dc * bufx[pl.ds(base + j, tr), :], axis=0, keepdims=True)

    cb = col0 // tc
    width = d.shape[1]
    main = lambda off: pl.BlockSpec((tr, tc), lambda j, i: (i, j + off))
    prev = pl.BlockSpec((CONV_HALO, tc), lambda j, i: (jnp.maximum(i * hb - 1, 0), j + cb))
    nxt = lambda off: pl.BlockSpec((CONV_HALO, tc), lambda j, i: (jnp.minimum((i + 1) * hb, t // CONV_HALO - 1), j + off))
    wsp = lambda off: pl.BlockSpec((GDN_CONV, tc), lambda j, i: (0, j + off))
    return pl.pallas_call(
        body, grid=(width // tc, nr),
        in_specs=[main(cb), prev, nxt(cb), main(0), nxt(0), wsp(cb)], out_specs=[main(0), wsp(0)],
        out_shape=[jax.ShapeDtypeStruct((t, width), BF16), jax.ShapeDtypeStruct((GDN_CONV, width), F32)],
        scratch_shapes=[pltpu.VMEM((tr + 2 * CONV_HALO, tc), F32), pltpu.VMEM((ext, tc), F32)],
        compiler_params=_cp("parallel", "arbitrary"), name=name)(p, p, p, d, d, w)


def _dot_sel(sel, b, dims=NN, sel_first=True):
    s = sel.astype(BF16)
    b1 = b.astype(BF16)
    r1 = b - b1.astype(F32)
    b2 = r1.astype(BF16)
    b3 = (r1 - b2.astype(F32)).astype(BF16)
    if sel_first:
        d = lambda v: lax.dot_general(s, v, (dims, ((), ())), preferred_element_type=F32)
    else:
        d = lambda v: lax.dot_general(v, s, (dims, ((), ())), preferred_element_type=F32)
    return d(b1) + (d(b2) + d(b3))


def _gdn_chunks(qraws, kraws, vs, braws, araws, alogs, dtbs):
    nv = len(vs)
    row = _iota2((CHUNK, CHUNK), 0)
    col = _iota2((CHUNK, CHUNK), 1)
    strict = row > col
    triu = (row <= col).astype(F32)
    tril = (row >= col).astype(F32)
    ones = jnp.ones((CHUNK, CHUNK), F32)
    keys = []
    for qraw, kraw in zip(qraws, kraws):
        rq = lax.rsqrt(jnp.sum(qraw * qraw, axis=-1, keepdims=True) + EPS)
        rk = lax.rsqrt(jnp.sum(kraw * kraw, axis=-1, keepdims=True) + EPS)
        qn = qraw * rq
        keys.append(dict(rq=rq, rk=rk, qn=qn, qh=qn * (GDN_DK ** -0.5), kh=kraw * rk))
    kks = [_dg(kd["kh"], kd["kh"], NT) for kd in keys]
    cs = []
    for h in range(nv):
        c = dict(keys[h // 2])
        c.update(v=vs[h], kk=kks[h // 2], strict=strict, triu=triu)
        c["beta"] = _sigmoid(braws[h])
        c["ea"] = jnp.exp(alogs[h])
        c["xs"] = araws[h] + dtbs[h]
        c["g"] = -c["ea"] * _softplus(c["xs"])
        cs.append(c)
    gbs = [jnp.broadcast_to(c["g"], (CHUNK, LANES)) for c in cs]
    cums = [_dot_sel(tril, gb) for gb in gbs]
    cum_js = [_dot_sel(ones, gb[:, :CHUNK] * triu) for gb in gbs]
    for c, cum, cum_j in zip(cs, cums, cum_js):
        diff = jnp.where(strict, cum[:, :CHUNK] - cum_j, 0.0)
        c["dm"] = jnp.where(strict, jnp.exp(diff), 0.0)
        c["a"] = (c["beta"] * c["dm"]) * c["kk"]
        c_last = cum[CHUNK - 1:CHUNK, :]
        c["e"] = jnp.exp(cum)
        c["f"] = jnp.exp(c_last - cum)
        c["dec"] = jnp.exp(c_last)
        c["rv"] = c["beta"] * c["v"]
        c["rk_rhs"] = (c["beta"] * c["e"]) * c["kh"]
        c["ke"] = c["kh"] * c["f"]
    return cs


def _unit_lower_inverses(mats):
    eye = (_iota2((CHUNK, CHUNK), 0) == _iota2((CHUNK, CHUNK), 1)).astype(F32)
    ts = [eye - a for a in mats]
    pws = list(mats)
    for _ in range(5):
        pws = [_dot3(pw, pw) for pw in pws]
        ts = [t + _dot3(t, pw) for t, pw in zip(ts, pws)]
    return ts


GDN_HB = 16


def _gdn_specs(chunk_of):
    hb = GDN_HB
    kw = hb // 2 * GDN_DK
    vw = hb * GDN_DV
    qs = pl.BlockSpec((CHUNK, kw), lambda g, n: (chunk_of(n), g))
    ks = pl.BlockSpec((CHUNK, kw), lambda g, n: (chunk_of(n), GDN_QKW // kw + g))
    vs = pl.BlockSpec((CHUNK, vw), lambda g, n: (chunk_of(n), 2 * GDN_QKW // vw + g))
    zs = pl.BlockSpec((CHUNK, vw), lambda g, n: (chunk_of(n), GDN_CONV_W // vw + g))
    assert hb == GDN_V_HEADS and (GDN_CONV_W + GDN_VW) % LANES == 0
    gates = pl.BlockSpec((CHUNK, LANES), lambda g, n: (chunk_of(n), (GDN_CONV_W + GDN_VW) // LANES))
    one = pl.BlockSpec((hb, 1, 1), lambda g, n: (g, 0, 0))
    ng = pl.BlockSpec((1, GDN_DV), lambda g, n: (0, 0))
    hd = pl.BlockSpec((CHUNK, vw), lambda g, n: (chunk_of(n), g))
    return qs, ks, vs, zs, gates, one, ng, hd


def _gate_columns(gates):
    return ([gates[:, h:h + 1] for h in range(GDN_V_HEADS)],
            [gates[:, GDN_V_HEADS + h:GDN_V_HEADS + h + 1] for h in range(GDN_V_HEADS)])


def _gdn_fwd(qkv, p, alog, dtb, ng, *, name):
    t = qkv.shape[0]
    nc = t // CHUNK
    nh = GDN_V_HEADS

    def body(q_ref, k_ref, v_ref, z_ref, gates_ref, alog_ref, dtb_ref, ng_ref, og_ref, s_ref, t_ref, st):
        @pl.when(pl.program_id(1) == 0)
        def _():
            st[...] = jnp.zeros_like(st)

        hs = range(GDN_HB)
        kqs = [slice(j * GDN_DK, (j + 1) * GDN_DK) for j in range(GDN_HB // 2)]
        vsl = [slice(h * GDN_DV, (h + 1) * GDN_DV) for h in hs]
        braws, araws = _gate_columns(gates_ref[...])
        cs = _gdn_chunks([q_ref[:, s] for s in kqs], [k_ref[:, s] for s in kqs], [v_ref[:, s] for s in vsl],
                         braws, araws, [alog_ref[h] for h in hs], [dtb_ref[h] for h in hs])
        tms = _unit_lower_inverses([c["a"] for c in cs])
        s0 = [st[h] for h in hs]
        wv = [_dot3(tms[h], cs[h]["rv"]) for h in hs]
        wk = [_dot3(tms[h], cs[h]["rk_rhs"]) for h in hs]
        u = [wv[h] - _dg(wk[h], s0[h], NN) for h in hs]
        s1 = [cs[h]["dec"] * s0[h] + _dg(cs[h]["ke"], u[h], TN) for h in hs]
        o = [_dg(cs[h]["qh"], s1[h], NN) for h in hs]
        for h in hs:
            t_ref[h, 0] = tms[h]
            st[h] = s1[h]
            s_ref[h, 0] = s1[h]
            rn = lax.rsqrt(jnp.mean(o[h] * o[h], axis=-1, keepdims=True) + EPS)
            zv = z_ref[:, vsl[h]]
            og_ref[:, vsl[h]] = (((o[h] * rn) * ng_ref[...]) * (zv * _sigmoid(zv))).astype(BF16)

    qs, ks, vs, zs, gates, one, ngs, hd = _gdn_specs(lambda n: n)
    return pl.pallas_call(
        body, grid=(nh // GDN_HB, nc),
        in_specs=[qs, ks, vs, zs, gates, one, one, ngs],
        out_specs=[hd,
                   pl.BlockSpec((GDN_HB, 1, GDN_DK, GDN_DV), lambda g, n: (g, n, 0, 0)),
                   pl.BlockSpec((GDN_HB, 1, CHUNK, CHUNK), lambda g, n: (g, n, 0, 0))],
        out_shape=[jax.ShapeDtypeStruct((t, GDN_VW), BF16), jax.ShapeDtypeStruct((nh, nc, GDN_DK, GDN_DV), F32),
                   jax.ShapeDtypeStruct((nh, nc, CHUNK, CHUNK), F32)],
        scratch_shapes=[pltpu.VMEM((GDN_HB, GDN_DK, GDN_DV), F32)],
        compiler_params=_cp("parallel", "arbitrary"), name=name)(qkv, qkv, qkv, p, p, alog, dtb, ng)


def _gdn_bwd(qkv, p, alog, dtb, ng, dog, sall, tall, *, name):
    t = qkv.shape[0]
    nc = t // CHUNK
    nh = GDN_V_HEADS

    def body(q_ref, k_ref, v_ref, z_ref, gates_ref, alog_ref, dtb_ref, ng_ref, dog_ref, s1_ref, s0_ref, t_ref,
             dq_ref, dk_ref, dv_ref, dz_ref, dgates_ref, dalog_ref, ddtb_ref, dng_ref, gc):
        grp = pl.program_id(0)
        i = pl.program_id(1)

        @pl.when(i == 0)
        def _():
            gc[...] = jnp.zeros_like(gc)
            dalog_ref[...] = jnp.zeros_like(dalog_ref)
            ddtb_ref[...] = jnp.zeros_like(ddtb_ref)

        @pl.when((i == 0) & (grp == 0))
        def _():
            dng_ref[...] = jnp.zeros_like(dng_ref)

        has_prev = (i < nc - 1).astype(F32)
        ngv = ng_ref[...]
        ones = jnp.ones((CHUNK, LANES), F32)
        hs = range(GDN_HB)
        kqs = [slice(j * GDN_DK, (j + 1) * GDN_DK) for j in range(GDN_HB // 2)]
        vsl = [slice(h * GDN_DV, (h + 1) * GDN_DV) for h in hs]
        braws, araws = _gate_columns(gates_ref[...])
        cs = _gdn_chunks([q_ref[:, s] for s in kqs], [k_ref[:, s] for s in kqs], [v_ref[:, s] for s in vsl],
                         braws, araws, [alog_ref[h] for h in hs], [dtb_ref[h] for h in hs])
        tms = [t_ref[h, 0] for h in hs]
        s1 = [s1_ref[h, 0] for h in hs]
        s0 = [s0_ref[h, 0] * has_prev for h in hs]
        wv = [_dot3(tms[h], cs[h]["rv"]) for h in hs]
        wk = [_dot3(tms[h], cs[h]["rk_rhs"]) for h in hs]
        u = [wv[h] - _dg(wk[h], s0[h], NN) for h in hs]
        o = [_dg(cs[h]["qh"], s1[h], NN) for h in hs]
        dng = jnp.zeros((1, GDN_DV), F32)
        do = []
        for h in hs:
            zv = z_ref[:, vsl[h]]
            dogv = dog_ref[:, vsl[h]]
            rn = lax.rsqrt(jnp.mean(o[h] * o[h], axis=-1, keepdims=True) + EPS)
            zo = o[h] * rn
            sg = _sigmoid(zv)
            sl = zv * sg
            dng = dng + jnp.sum(dogv * zo * sl, axis=0, keepdims=True)
            dz_ref[:, vsl[h]] = (dogv * (zo * ngv) * (sg * (1.0 + zv * (1.0 - sg)))).astype(BF16)
            dzo = dogv * sl * ngv
            do.append(rn * (dzo - zo * jnp.mean(dzo * zo, axis=-1, keepdims=True)))
        dng_ref[...] += dng
        g_tot = [gc[h] + _dg(cs[h]["qh"], do[h], TN) for h in hs]
        dqh = [_dg(do[h], s1[h], NT) for h in hs]
        dke = [_dg(u[h], g_tot[h], NT) for h in hs]
        du = [_dg(cs[h]["ke"], g_tot[h], NN) for h in hs]
        gnew = [cs[h]["dec"] * g_tot[h] - _dg(wk[h], du[h], TN) for h in hs]
        dwk = [-_dg(du[h], s0[h], NT) for h in hs]
        drv = [_dot3(tms[h], du[h], TN) for h in hs]
        drk = [_dot3(tms[h], dwk[h], TN) for h in hs]
        da = [jnp.where(cs[h]["strict"], -(_dot3(drv[h], wv[h], NT) + _dot3(drk[h], wk[h], NT)), 0.0) for h in hs]
        mx = [da[h] * cs[h]["dm"] * cs[h]["kk"] for h in hs]
        aa = [mx[h] * cs[h]["beta"] for h in hs]
        colsum = [_dot_sel(ones, aa[h], TN, sel_first=False)[:, 0:1] for h in hs]
        bm = [(da[h] * cs[h]["beta"]) * cs[h]["dm"] for h in hs]
        dkh = [_dg(bm[h], cs[h]["kh"], NN) + _dg(bm[h], cs[h]["kh"], TN) for h in hs]
        dcum, dcl, dbeta = [], [], []
        for h in hs:
            c = cs[h]
            beta, kh, e, f, dec, ke = c["beta"], c["kh"], c["e"], c["f"], c["dec"], c["ke"]
            gc[h] = gnew[h]
            ddec = jnp.sum(jnp.sum(g_tot[h] * s0[h], axis=1, keepdims=True), axis=0, keepdims=True)
            dv_ref[:, vsl[h]] = beta * drv[h]
            db = jnp.sum(mx[h], axis=1, keepdims=True) + jnp.sum(drv[h] * c["v"], axis=1, keepdims=True)
            dbeta.append(db + jnp.sum(drk[h] * (e * kh), axis=1, keepdims=True))
            dkh[h] = dkh[h] + (beta * e) * drk[h] + f * dke[h]
            ef = jnp.sum(dke[h] * ke, axis=1, keepdims=True)
            dcum.append(jnp.sum(aa[h], axis=1, keepdims=True) - colsum[h] + jnp.sum(drk[h] * c["rk_rhs"], axis=1, keepdims=True) - ef)
            dcl.append(jnp.sum(ef, axis=0, keepdims=True) + ddec * dec[:, 0:1])
        dg = [_dot_sel(cs[h]["triu"], jnp.broadcast_to(dcum[h], (CHUNK, LANES)))[:, 0:1] + dcl[h] for h in hs]
        lane = _iota2((CHUNK, LANES), 1)
        dgates = jnp.zeros((CHUNK, LANES), F32)
        for h in hs:
            c = cs[h]
            beta = c["beta"]
            daraw = dg[h] * (-c["ea"]) * _sigmoid(c["xs"])
            dgates = jnp.where(lane == h, dbeta[h] * beta * (1.0 - beta), dgates)
            dgates = jnp.where(lane == GDN_V_HEADS + h, daraw, dgates)
            dalog_ref[h] += jnp.sum(dg[h] * c["g"], axis=0, keepdims=True)
            ddtb_ref[h] += jnp.sum(daraw, axis=0, keepdims=True)
        dgates_ref[...] = dgates.astype(BF16)
        for j, sl in enumerate(kqs):
            c = cs[2 * j]
            dn = (dqh[2 * j] + dqh[2 * j + 1]) * (GDN_DK ** -0.5)
            dks = dkh[2 * j] + dkh[2 * j + 1]
            dq_ref[:, sl] = c["rq"] * (dn - c["qn"] * jnp.sum(dn * c["qn"], axis=-1, keepdims=True))
            dk_ref[:, sl] = c["rk"] * (dks - c["kh"] * jnp.sum(dks * c["kh"], axis=-1, keepdims=True))

    rev = lambda n: nc - 1 - n
    qs, ks, vs, zs, gates, one, ngs, hd = _gdn_specs(rev)
    s1s = pl.BlockSpec((GDN_HB, 1, GDN_DK, GDN_DV), lambda g, n: (g, rev(n), 0, 0))
    s0s = pl.BlockSpec((GDN_HB, 1, GDN_DK, GDN_DV), lambda g, n: (g, jnp.maximum(rev(n) - 1, 0), 0, 0))
    ts = pl.BlockSpec((GDN_HB, 1, CHUNK, CHUNK), lambda g, n: (g, rev(n), 0, 0))
    dgs = pl.BlockSpec((CHUNK, LANES), lambda g, n: (rev(n), 0))
    big = jax.ShapeDtypeStruct((t, GDN_VW), F32)
    keyw = jax.ShapeDtypeStruct((t, GDN_QKW), F32)
    ones_s = jax.ShapeDtypeStruct((nh, 1, 1), F32)
    return pl.pallas_call(
        body, grid=(nh // GDN_HB, nc),
        in_specs=[qs, ks, vs, zs, gates, one, one, ngs, hd, s1s, s0s, ts],
        out_specs=[qs, qs, hd, hd, dgs, one, one, ngs],
        out_shape=[keyw, keyw, big, jax.ShapeDtypeStruct((t, GDN_VW), BF16), jax.ShapeDtypeStruct((t, LANES), BF16),
                   ones_s, ones_s, jax.ShapeDtypeStruct((1, GDN_DV), F32)],
        scratch_shapes=[pltpu.VMEM((GDN_HB, GDN_DK, GDN_DV), F32)],
        compiler_params=_cp("arbitrary", "arbitrary"), name=name)(qkv, qkv, qkv, p, p, alog, dtb, ng, dog, sall, sall, tall)


def _gdn_layer_fwd(h, w, tag):
    p = _mm(h, w["w_in"], name=tag + "_in")
    qkv = _gdn_conv_fwd(p, w["conv_w"], name=tag + "_conv")
    og, sall, tall = _gdn_fwd(qkv, p, w["a_log"], w["dt_bias"], w["norm_g"], name=tag + "_scan")
    y = _mm(og, w["w_out"], name=tag + "_out")
    return y, (p, qkv, og, sall, tall)


def _gdn_layer_bwd(dy, h, saved, w, tag, sink=None):
    p, qkv, og, sall, tall = saved
    t = h.shape[0]
    dog = _mm(dy, w["w_out"], tb=True, name=tag + "_dog")
    dw_out, gf = _dw_out(og, dy, sink, tag + "_dwout")
    dq, dk, dv, dz, dgates, dalog, ddtb, dng = _gdn_bwd(
        qkv, p, w["a_log"], w["dt_bias"], w["norm_g"], dog, sall, tall, name=tag + "_scanb")
    dpre_q, dcw_q = _gdn_conv_bwd(dq, p, w["conv_w"], 0, name=tag + "_convb_q")
    dpre_k, dcw_k = _gdn_conv_bwd(dk, p, w["conv_w"], GDN_QKW, name=tag + "_convb_k")
    dpre_v, dcw_v = _gdn_conv_bwd(dv, p, w["conv_w"], 2 * GDN_QKW, name=tag + "_convb_v")
    dconv_w = jnp.concatenate([dcw_q, dcw_k, dcw_v], axis=1)
    dp = jnp.concatenate([dpre_q, dpre_k, dpre_v, dz, dgates,
                          jnp.zeros((t, GDN_IN_PAD - GDN_CONV_W - GDN_VW - LANES), BF16)], axis=1)
    dw_in = _mm(h, dp, ta=True, out_dtype=BF16, name=tag + "_dwin")
    dh = _mm(dp, w["w_in"], tb=True, out_dtype=BF16, name=tag + "_dh")
    grads = dict(w_in=dw_in[:, :GDN_IN], conv_w=dconv_w, a_log=dalog[:, 0, 0], dt_bias=ddtb[:, 0, 0], norm_g=dng[0], w_out=dw_out)
    return dh, grads, gf


MESH_ID = pl.DeviceIdType.MESH
FLAT_W = 1024
FLAT_ROWS = 13056
FLAT_TILE = 384


def _exchange(name, ins, out_shapes, plan, n_remote, n_local):
    def body(*refs):
        in_refs = refs[:len(ins)]
        out_refs = refs[len(ins):len(ins) + len(out_shapes)]
        ssem, rsem, lsem = refs[len(ins) + len(out_shapes):]
        x, y, c = lax.axis_index("x"), lax.axis_index("y"), lax.axis_index("c")
        stages, local_copies = plan(x, y, c, in_refs, out_refs)
        assert sum(len(s) for s in stages) == n_remote and len(local_copies) == n_local
        locs = [pltpu.make_async_copy(s, d, lsem.at[i]) for i, (s, d) in enumerate(local_copies)]
        for cp in locs:
            cp.start()
        sent = []
        k = 0
        for stage in stages:
            arrivals = []
            for src, dst, peer, landing in stage:
                cp = pltpu.make_async_remote_copy(src_ref=src, dst_ref=dst, send_sem=ssem.at[k], recv_sem=rsem.at[k],
                                                  device_id=peer, device_id_type=MESH_ID)
                cp.start()
                sent.append(cp)
                arrivals.append(pltpu.make_async_remote_copy(src_ref=src, dst_ref=landing, send_sem=ssem.at[k],
                                                             recv_sem=rsem.at[k], device_id=peer, device_id_type=MESH_ID))
                k += 1
            for cp in arrivals:
                cp.wait_recv()
        for cp in sent:
            cp.wait_send()
        for cp in locs:
            cp.wait()

    hbm = pl.BlockSpec(memory_space=pl.ANY)
    return pl.pallas_call(
        body, in_specs=[hbm] * len(ins), out_specs=[hbm] * len(out_shapes), out_shape=out_shapes,
        scratch_shapes=[pltpu.SemaphoreType.DMA((n_remote,)), pltpu.SemaphoreType.DMA((n_remote,)),
                        pltpu.SemaphoreType.DMA((max(n_local, 1),))],
        name=name)(*ins)


def _other_chips(x, y):
    return [(1 - x, y), (x, 1 - y), (1 - x, 1 - y)]


def _all8_gather(a, *, name):
    def plan(x, y, c, ins, outs):
        (src,), (dst,) = ins, outs
        me = 4 * x + 2 * y + c
        stage = []
        for fx, fy, fc in [(0, 0, 1), (0, 1, 0), (0, 1, 1), (1, 0, 0), (1, 0, 1), (1, 1, 0), (1, 1, 1)]:
            px, py, pc = (1 - x if fx else x), (1 - y if fy else y), (1 - c if fc else c)
            stage.append((src, dst.at[me], (px, py, pc), dst.at[4 * px + 2 * py + pc]))
        return [stage], [(src, dst.at[me])]

    return _exchange(name, [a], [jax.ShapeDtypeStruct((8,) + a.shape, a.dtype)], plan, 7, 1)[0]


def _chip_gather(flat, *, name):
    rows = flat.shape[0]
    half = rows // 2

    def plan(x, y, c, ins, outs):
        (src,), (dst,) = ins, outs
        me = 2 * x + y
        mine = pl.ds(c * half, half)
        theirs = pl.ds((1 - c) * half, half)
        ici = [(src.at[mine], dst.at[me, mine], (px, py, c), dst.at[2 * px + py, mine]) for px, py in _other_chips(x, y)]
        d2d = [(dst.at[2 * px + py, mine], dst.at[2 * px + py, mine], (x, y, 1 - c), dst.at[2 * px + py, theirs])
               for px, py in _other_chips(x, y)]
        return [ici, d2d], []

    return _exchange(name, [flat], [jax.ShapeDtypeStruct((4,) + flat.shape, flat.dtype)], plan, 6, 0)[0]


def _add_sibling(gf, buf_a, core, *, name):
    _, rows, w = gf.shape
    half = rows // 2
    nb = half // FLAT_TILE

    def body(c_ref, g_ref, a_ref, o_ref):
        o_ref[...] = (g_ref[...].astype(F32) + a_ref[...].astype(F32)).astype(BF16)

    blk = (1, FLAT_TILE, w)
    return pl.pallas_call(
        body,
        grid_spec=pltpu.PrefetchScalarGridSpec(
            num_scalar_prefetch=1, grid=(4, nb),
            in_specs=[pl.BlockSpec(blk, lambda s, i, c_ref: (s, c_ref[0] * nb + i, 0)), pl.BlockSpec(blk, lambda s, i, c_ref: (s, i, 0))],
            out_specs=pl.BlockSpec(blk, lambda s, i, c_ref: (s, i, 0))),
        out_shape=jax.ShapeDtypeStruct((4, half, w), BF16), compiler_params=_cp("parallel", "parallel"), name=name)(core, gf, buf_a)


def _sum_chips(hsum, buf_b, chip, *, name):
    _, half, w = hsum.shape
    nb = half // FLAT_TILE

    def body(c_ref, h_ref, b0_ref, b1_ref, b2_ref, b3_ref, o_ref):
        me = c_ref[0]
        own = h_ref[0].astype(F32)
        acc = None
        for j, b_ref in enumerate((b0_ref, b1_ref, b2_ref, b3_ref)):
            term = jnp.where(me == j, own, b_ref[0].astype(F32))
            acc = term if acc is None else acc + term
        o_ref[...] = acc

    blk = (1, FLAT_TILE, w)

    def other(j):
        return pl.BlockSpec(blk, lambda i, c_ref: (jnp.where(c_ref[0] == j, (j + 1) % 4, j), i, 0))

    return pl.pallas_call(
        body,
        grid_spec=pltpu.PrefetchScalarGridSpec(
            num_scalar_prefetch=1, grid=(nb,),
            in_specs=[pl.BlockSpec(blk, lambda i, c_ref: (c_ref[0], i, 0))] + [other(j) for j in range(4)],
            out_specs=pl.BlockSpec((FLAT_TILE, w), lambda i, c_ref: (i, 0))),
        out_shape=jax.ShapeDtypeStruct((half, w), F32), compiler_params=_cp("parallel"), name=name)(chip, hsum, buf_b, buf_b, buf_b, buf_b)


def _sum_slots(buf, *, name):
    n, rows, w = buf.shape
    tr = _pick(rows, (FLAT_TILE, 8))

    def body(b_ref, o_ref):
        acc = b_ref[0]
        for s in range(1, n):
            acc = acc + b_ref[s]
        o_ref[...] = acc

    return pl.pallas_call(body, grid=(rows // tr,), in_specs=[pl.BlockSpec((n, tr, w), lambda i: (0, i, 0))],
                          out_specs=pl.BlockSpec((tr, w), lambda i: (i, 0)), out_shape=jax.ShapeDtypeStruct((rows, w), F32),
                          compiler_params=_cp("parallel"), name=name)(buf)


def _reduce_scatter(gf, core, chip, *, tag):
    _, rows, w = gf.shape
    half = rows // 2

    def plan_a(x, y, c, ins, outs):
        (src,), (dst,) = ins, outs
        return [[(src.at[:, pl.ds((1 - c) * half, half)], dst, (x, y, 1 - c), dst)]], []

    buf_a = _exchange(tag + "_sibling", [gf], [jax.ShapeDtypeStruct((4, half, w), gf.dtype)], plan_a, 1, 0)[0]
    hsum = _add_sibling(gf, buf_a, core, name=tag + "_add_sibling")

    def plan_b(x, y, c, ins, outs):
        (src,), (dst,) = ins, outs
        me = 2 * x + y
        stage = [(src.at[2 * px + py], dst.at[me], (px, py, c), dst.at[2 * px + py]) for px, py in _other_chips(x, y)]
        return [stage], []

    buf_b = _exchange(tag + "_chips", [hsum], [jax.ShapeDtypeStruct((4, half, w), BF16)], plan_b, 3, 0)[0]
    mine = _sum_chips(hsum, buf_b, chip, name=tag + "_sum_chips")

    def plan_c(x, y, c, ins, outs):
        (src,), (dst,) = ins, outs
        return [[(src, dst, (x, y, 1 - c), dst)]], []

    theirs = _exchange(tag + "_halves", [mine], [jax.ShapeDtypeStruct((half, w), F32)], plan_c, 1, 0)[0]
    first = core[0] == 0
    return jnp.concatenate([jnp.where(first, mine, theirs), jnp.where(first, theirs, mine)], axis=0)


WEIGHTS = ["ada_w", "ada_b", "norm_pre_g", "norm_post_g", "gla_w_in", "gla_w_gate_up", "gla_b_gate", "gla_head_g",
           "gla_w_out", "mla_w_in", "mla_q_norm_g", "mla_w_uq", "mla_kv_norm_g", "mla_w_ukv", "mla_w_out", "gdn_w_in",
           "gdn_conv_w", "gdn_a_log", "gdn_dt_bias", "gdn_norm_g", "gdn_w_out", "mlp_w_up", "mlp_w_down"]
PACK_BF16 = [("mlp_w_up", 2), ("mlp_w_down", 1), ("gdn_w_out", 1), ("gla_w_out", 1), ("mla_w_out", 1),
             ("gla_w_in", 2), ("mla_w_in", 1), ("mla_w_uq", 2), ("mla_w_ukv", 2), ("gdn_w_in", 2)]
N_DIRECT = 5
PACK_F32 = [("norm_pre_g", 2), ("norm_post_g", 2), ("gla_w_gate_up", 2), ("gla_b_gate", 1), ("gla_head_g", 1), ("gdn_conv_w", 2)]
REPLICATED_SMALL = ["mla_q_norm_g", "mla_kv_norm_g", "gdn_a_log", "gdn_dt_bias", "gdn_norm_g"]
MIXERS = ["gla", "mla", "gdn"]


def _silu_rows(a, *, name):
    def body(a_ref, o_ref):
        v = a_ref[...]
        o_ref[...] = v * _sigmoid(v)

    return pl.pallas_call(body, out_shape=jax.ShapeDtypeStruct(a.shape, F32), name=name)(a)


SMALL_ROWS = 16


def _piece_rows(size, mult):
    assert size % FLAT_W == 0
    return -(-(size // FLAT_W) // mult) * mult


def _to_rows(a, lead, mult):
    n = math.prod(a.shape[len(lead):])
    r = a.reshape(lead + (n // FLAT_W, FLAT_W))
    extra = _piece_rows(n, mult) - n // FLAT_W
    return jnp.pad(r, [(0, 0)] * len(lead) + [(0, extra), (0, 0)]) if extra else r


def _small_to_rows(parts, lead):
    flat = jnp.concatenate([p.reshape(lead + (-1,)) for p in parts], axis=-1)
    pad = SMALL_ROWS * FLAT_W - flat.shape[-1]
    return jnp.pad(flat, [(0, 0)] * len(lead) + [(0, pad)]).reshape(lead + (SMALL_ROWS, FLAT_W))


def _small_from_rows(rows, shards, lead):
    flat = rows.reshape(lead + (-1,))
    out, off = {}, 0
    for n, _ in PACK_F32:
        out[n] = flat[..., off:off + shards[n].size].reshape(lead + shards[n].shape)
        off += shards[n].size
    return out


def _pack_weights(shards):
    parts = [_to_rows(shards[n].astype(BF16), (), 16) for n, _ in PACK_BF16]
    small = _small_to_rows([shards[n] for n, _ in PACK_F32], ())
    parts.append(lax.bitcast_convert_type(small, BF16).reshape(2 * SMALL_ROWS, FLAT_W))
    flat = jnp.concatenate(parts, axis=0)
    return jnp.pad(flat, ((0, FLAT_ROWS - flat.shape[0]), (0, 0)))


def _unpack_weights(gathered, shards):
    full, off = {}, 0
    for n, ax in PACK_BF16:
        size = shards[n].size
        seg = gathered[:, off:off + size // FLAT_W].reshape((4,) + shards[n].shape)
        full[n] = jnp.concatenate([seg[j] for j in range(4)], axis=ax)
        off += _piece_rows(size, 16)
    small = lax.bitcast_convert_type(gathered[:, off:off + 2 * SMALL_ROWS].reshape(4, SMALL_ROWS, FLAT_W, 2), F32)
    for (n, ax), seg in zip(PACK_F32, _small_from_rows(small, shards, (4,)).values()):
        full[n] = jnp.concatenate([seg[j] for j in range(4)], axis=ax)
    return full


def _grad_layout(shards):
    layout, off = {}, 0
    for n, _ in PACK_BF16:
        layout[n] = (off, shards[n].size // shards[n].shape[0] // FLAT_W)
        off += _piece_rows(shards[n].size, 16)
    layout["small"] = (off, SMALL_ROWS)
    return layout


def _pack_grads(gf, grads, layout):
    by_chip = lambda g, ax: jnp.stack(jnp.split(g.astype(gf.dtype), 4, axis=ax - 1))
    parts = []
    for n, ax in PACK_BF16[N_DIRECT:]:
        rows = sum(g.size for g in grads[n]) // (4 * FLAT_W)
        parts += [by_chip(g, ax).reshape(4, -1, FLAT_W) for g in grads[n]]
        if _piece_rows(rows * FLAT_W, 16) > rows:
            parts.append(jnp.zeros((4, _piece_rows(rows * FLAT_W, 16) - rows, FLAT_W), gf.dtype))
    parts.append(_small_to_rows([jnp.stack([by_chip(g, ax) for g in grads[n]], axis=1) for n, ax in PACK_F32], (4,)))
    first = layout[PACK_BF16[N_DIRECT][0]][0]
    rest = jnp.concatenate(parts, axis=1)
    assert first + rest.shape[1] == layout["small"][0] + SMALL_ROWS
    return lax.dynamic_update_slice(gf, rest, (0, first, 0))


def _unpack_grads(reduced, shards):
    out, off = {}, 0
    for n, _ in PACK_BF16:
        size = shards[n].size
        out[n] = reduced[off:off + size // FLAT_W].reshape(shards[n].shape)
        off += _piece_rows(size, 16)
    out.update(_small_from_rows(reduced[off:off + SMALL_ROWS], shards, ()))
    return out


def _mixer_weights(kind, j, full, rep):
    if kind == "gla":
        return dict(w_in=jnp.pad(full["gla_w_in"][j], ((0, 0), (0, GLA_IN_PAD - GLA_IN))),
                    wg=jnp.pad(full["gla_w_gate_up"][j], ((0, LANES - GLA_RANK), (0, 0))),
                    bg=full["gla_b_gate"][j][None], hg=full["gla_head_g"][j][None], w_out=full["gla_w_out"][j])
    if kind == "mla":
        return dict(w_in=jnp.pad(full["mla_w_in"][j], ((0, 0), (0, MLA_IN_PAD - MLA_IN))), q_norm_g=rep["mla_q_norm_g"][j][None],
                    w_uq=full["mla_w_uq"][j], kv_norm_g=rep["mla_kv_norm_g"][j][None], w_ukv=full["mla_w_ukv"][j],
                    w_out=full["mla_w_out"][j])
    return dict(w_in=jnp.pad(full["gdn_w_in"][j], ((0, 0), (0, GDN_IN_PAD - GDN_IN))), conv_w=full["gdn_conv_w"][j],
                a_log=rep["gdn_a_log"][j][:, None, None], dt_bias=rep["gdn_dt_bias"][j][:, None, None],
                norm_g=rep["gdn_norm_g"][j][None], w_out=full["gdn_w_out"][j])


def _layer_fwd(xin, mod, gains, kind, mw, w_up, w_down, pos, tag):
    sh_m, sc_m, gt_m, sh_f, sc_f, gt_f = mod
    pre0, pre1, post0, post1 = gains
    h = _premod_fwd(xin, pre0, sc_m, sh_m, name=tag + "_pre0")
    if kind == "gla":
        y, saved = _gla_layer_fwd(h, mw, tag + "_gla")
    elif kind == "mla":
        y, saved = _mla_layer_fwd(h, pos, mw, tag + "_mla")
    else:
        y, saved = _gdn_layer_fwd(h, mw, tag + "_gdn")
    x1 = _postres_fwd(xin, y, post0, gt_m, name=tag + "_post0")
    h2 = _premod_fwd(x1, pre1, sc_f, sh_f, name=tag + "_pre1")
    act = _mm(h2, w_up, out_dtype=BF16, epi="relu2", name=tag + "_up")
    y2 = _mm(act, w_down, name=tag + "_down")
    x2 = _postres_fwd(x1, y2, post1, gt_f, name=tag + "_post1")
    return x2, (xin, h, y, saved, x1, h2, act, y2)


def _layer_bwd(g2, kept, mod, gains, kind, mw, w_up, w_down, tag, gf, rows):
    xin, h, y, saved, x1, h2, act, y2 = kept
    sh_m, sc_m, gt_m, sh_f, sc_f, gt_f = mod
    pre0, pre1, post0, post1 = gains
    dy2, dpost1, dgt_f = _postres_bwd(g2, y2, post1, gt_f, name=tag + "_post1_b")
    du = _mm(dy2, w_down, tb=True, out_dtype=BF16, epi="dact", aux=act, name=tag + "_du")
    gf = _mm(act, dy2, ta=True, out_dtype=gf.dtype, into=(gf, rows[1], "rows"), name=tag + "_dwdown")
    gf = _mm(h2, du, ta=True, out_dtype=gf.dtype, into=(gf, rows[0], "cols"), name=tag + "_dwup")
    dh2 = _mm(du, w_up, tb=True, out_dtype=BF16, name=tag + "_dh2")
    g1, dpre1, dsc_f, dsh_f = _premod_bwd(dh2, x1, pre1, sc_f, g2, name=tag + "_pre1_b")
    dy, dpost0, dgt_m = _postres_bwd(g1, y, post0, gt_m, name=tag + "_post0_b")
    mixer_bwd = dict(gla=_gla_layer_bwd, mla=_mla_layer_bwd, gdn=_gdn_layer_bwd)[kind]
    dh, mg, gf = mixer_bwd(dy, h, saved, mw, tag + "_" + kind, sink=(gf, rows[2]))
    g0, dpre0, dsc_m, dsh_m = _premod_bwd(dh, xin, pre0, sc_m, g1, name=tag + "_pre0_b")
    dmod = jnp.concatenate([dsh_m, dsc_m, dgt_m, dsh_f, dsc_f, dgt_f], axis=1)
    return g0, dmod, jnp.concatenate([dpre0, dpre1], axis=0), jnp.concatenate([dpost0, dpost1], axis=0), mg, gf


def kernel(x, c, positions, ada_w, ada_b, norm_pre_g, norm_post_g, gla_w_in, gla_w_gate_up, gla_b_gate, gla_head_g, gla_w_out, mla_w_in, mla_q_norm_g, mla_w_uq, mla_kv_norm_g, mla_w_ukv, mla_w_out, gdn_w_in, gdn_conv_w, gdn_a_log, gdn_dt_bias, gdn_norm_g, gdn_w_out, mlp_w_up, mlp_w_down, loss_target, m_ada_w, m_ada_b, m_norm_pre_g, m_norm_post_g, m_gla_w_in, m_gla_w_gate_up, m_gla_b_gate, m_gla_head_g, m_gla_w_out, m_mla_w_in, m_mla_q_norm_g, m_mla_w_uq, m_mla_kv_norm_g, m_mla_w_ukv, m_mla_w_out, m_gdn_w_in, m_gdn_conv_w, m_gdn_a_log, m_gdn_dt_bias, m_gdn_norm_g, m_gdn_w_out, m_mlp_w_up, m_mlp_w_down, v_ada_w, v_ada_b, v_norm_pre_g, v_norm_post_g, v_gla_w_in, v_gla_w_gate_up, v_gla_b_gate, v_gla_head_g, v_gla_w_out, v_mla_w_in, v_mla_q_norm_g, v_mla_w_uq, v_mla_kv_norm_g, v_mla_w_ukv, v_mla_w_out, v_gdn_w_in, v_gdn_conv_w, v_gdn_a_log, v_gdn_dt_bias, v_gdn_norm_g, v_gdn_w_out, v_mlp_w_up, v_mlp_w_down):
    w = dict(ada_w=ada_w, ada_b=ada_b, norm_pre_g=norm_pre_g, norm_post_g=norm_post_g, gla_w_in=gla_w_in,
             gla_w_gate_up=gla_w_gate_up, gla_b_gate=gla_b_gate, gla_head_g=gla_head_g, gla_w_out=gla_w_out, mla_w_in=mla_w_in,
             mla_q_norm_g=mla_q_norm_g, mla_w_uq=mla_w_uq, mla_kv_norm_g=mla_kv_norm_g, mla_w_ukv=mla_w_ukv, mla_w_out=mla_w_out,
             gdn_w_in=gdn_w_in, gdn_conv_w=gdn_conv_w, gdn_a_log=gdn_a_log, gdn_dt_bias=gdn_dt_bias, gdn_norm_g=gdn_norm_g,
             gdn_w_out=gdn_w_out, mlp_w_up=mlp_w_up, mlp_w_down=mlp_w_down)
    m = dict(zip(WEIGHTS, [m_ada_w, m_ada_b, m_norm_pre_g, m_norm_post_g, m_gla_w_in, m_gla_w_gate_up, m_gla_b_gate, m_gla_head_g,
                           m_gla_w_out, m_mla_w_in, m_mla_q_norm_g, m_mla_w_uq, m_mla_kv_norm_g, m_mla_w_ukv, m_mla_w_out,
                           m_gdn_w_in, m_gdn_conv_w, m_gdn_a_log, m_gdn_dt_bias, m_gdn_norm_g, m_gdn_w_out, m_mlp_w_up, m_mlp_w_down]))
    v = dict(zip(WEIGHTS, [v_ada_w, v_ada_b, v_norm_pre_g, v_norm_post_g, v_gla_w_in, v_gla_w_gate_up, v_gla_b_gate, v_gla_head_g,
                           v_gla_w_out, v_mla_w_in, v_mla_q_norm_g, v_mla_w_uq, v_mla_kv_norm_g, v_mla_w_ukv, v_mla_w_out,
                           v_gdn_w_in, v_gdn_conv_w, v_gdn_a_log, v_gdn_dt_bias, v_gdn_norm_g, v_gdn_w_out, v_mlp_w_up, v_mlp_w_down]))
    t = x.shape[1]
    ix, iy, ic = lax.axis_index("x"), lax.axis_index("y"), lax.axis_index("c")
    me = 4 * ix + 2 * iy + ic
    chip = 2 * ix + iy
    ada_cols = ada_w.shape[2]

    packed = _pack_weights(w)
    zero = jnp.zeros((), jnp.int32)
    gathered = lax.dynamic_update_slice(_chip_gather(packed, name="gather_weights"), packed[None], (chip, zero, zero))
    full = _unpack_weights(gathered, w)

    cond8 = _silu_rows(jnp.pad(c, ((0, 7), (0, 0))), name="cond_silu")
    cond16 = jnp.pad(_all8_gather(cond8, name="gather_cond")[:, 0, :], ((0, 8), (0, 0)))
    mod_cols = []
    for layer in range(DEPTH):
        bias = jnp.broadcast_to(lax.dynamic_slice_in_dim(ada_b[layer], chip * ada_cols, ada_cols)[None], (16, ada_cols))
        mod_cols.append(_mm(cond16, ada_w[layer], epi="add", aux=bias, name=f"ada{layer}")[:8])
    mod_all = _all8_gather(jnp.stack(mod_cols).reshape(DEPTH * 8, ada_cols), name="gather_mod")
    mod = jnp.concatenate([lax.dynamic_slice_in_dim(mod_all[2 * j].reshape(DEPTH, 8, ada_cols), me, 1, axis=1)[:, 0]
                           for j in range(4)], axis=1)

    def layer_args(layer):
        kind, j = MIXERS[layer % 3], layer // 3
        mods = [mod[layer, i * D_MODEL:(i + 1) * D_MODEL][None] for i in range(N_MOD)]
        gains = (full["norm_pre_g"][layer, 0:1], full["norm_pre_g"][layer, 1:2], full["norm_post_g"][layer, 0:1],
                 full["norm_post_g"][layer, 1:2])
        return kind, j, mods, gains, _mixer_weights(kind, j, full, w)

    xs = x[0]
    kept = []
    for layer in range(DEPTH):
        kind, j, mods, gains, mw = layer_args(layer)
        xs, keep = _layer_fwd(xs, mods, gains, kind, mw, full["mlp_w_up"][layer], full["mlp_w_down"][layer], positions[0], f"l{layer}")
        kept.append(keep)
    loss_row, g = _loss_head(xs, loss_target[0], name="loss_head")
    loss = lax.psum(loss_row[0, 0], ("x", "y", "c"))

    grads = {n: [None] * w[n].shape[0] for n, _ in PACK_BF16[N_DIRECT:] + PACK_F32}
    rep_grads = {}
    dmods = [None] * DEPTH
    layout = _grad_layout(w)
    row_of = lambda n, idx: layout[n][0] + idx * layout[n][1]
    gf = lax.empty((4, FLAT_ROWS, FLAT_W), BF16)
    for layer in reversed(range(DEPTH)):
        kind, j, mods, gains, mw = layer_args(layer)
        rows = (row_of("mlp_w_up", layer), row_of("mlp_w_down", layer), row_of(kind + "_w_out", j))
        g, dmods[layer], dpre, dpost, mg, gf = _layer_bwd(
            g, kept[layer], mods, gains, kind, mw, full["mlp_w_up"][layer], full["mlp_w_down"][layer], f"l{layer}", gf, rows)
        grads["norm_pre_g"][layer], grads["norm_post_g"][layer] = dpre, dpost
        for key, val in mg.items():
            name = kind + "_" + key
            if name in grads:
                grads[name][j] = val
            elif name in REPLICATED_SMALL:
                rep_grads[name] = val[None]

    rep_flat = jnp.concatenate([rep_grads[n].reshape(-1) for n in REPLICATED_SMALL])
    dbuf = jnp.concatenate([jnp.concatenate(dmods, axis=0), jnp.pad(rep_flat, (0, N_MOD * D_MODEL - rep_flat.shape[0]))[None],
                            jnp.zeros((3, N_MOD * D_MODEL), F32)], axis=0)
    dall = _all8_gather(dbuf, name="gather_dmod")
    dsum = _sum_slots(dall, name="sum_dmod")
    out_grads = {"ada_b": dsum[:DEPTH]}
    off = 0
    for n in REPLICATED_SMALL:
        out_grads[n] = dsum[DEPTH, off:off + w[n].size].reshape(w[n].shape)
        off += w[n].size
    dada = []
    for layer in range(DEPTH):
        dm16 = jnp.pad(lax.dynamic_slice_in_dim(dall[:, layer, :], chip * ada_cols, ada_cols, axis=1), ((0, 8), (0, 0)))
        dada.append(_mm(cond16, dm16, ta=True, name=f"dada{layer}"))
    out_grads["ada_w"] = jnp.stack(dada)

    reduced = _reduce_scatter(_pack_grads(gf, grads, layout), ic.reshape(1).astype(jnp.int32),
                              chip.reshape(1).astype(jnp.int32), tag="reduce_grads")
    out_grads.update(_unpack_grads(reduced, w))

    deltas, new_m, new_v = {}, {}, {}
    for n in WEIGHTS:
        deltas[n], new_m[n], new_v[n] = _adamw(w[n], out_grads[n], m[n], v[n], name="adamw_" + n)
    return (loss, g[None], *[out_grads[n] for n in WEIGHTS], *[deltas[n] for n in WEIGHTS],
            *[new_m[n] for n in WEIGHTS], *[new_v[n] for n in WEIGHTS])
```

```python
import functools
import math

import jax
import jax.numpy as jnp
from jax import lax
from jax.experimental import pallas as pl
from jax.experimental.pallas import tpu as pltpu

F32 = jnp.float32
BF16 = jnp.bfloat16

D_MODEL = 1024
DEPTH = 4
CHUNK = 64
EPS = 1e-6
NEG_INF = -1e30
N_MOD = 6

GLA_HEADS, GLA_DK, GLA_DV, GLA_RANK = 4, 128, 256, 16
GLA_KW, GLA_VW = GLA_HEADS * GLA_DK, GLA_HEADS * GLA_DV
GLA_IN = 2 * GLA_KW + 2 * GLA_VW + GLA_RANK
GLA_IN_PAD = 3200

MLA_HEADS, MLA_NOPE, MLA_ROPE, MLA_V = 16, 64, 32, 64
MLA_Q_RANK, MLA_KV_RANK = 384, 256
MLA_IN = MLA_Q_RANK + MLA_KV_RANK + MLA_ROPE
MLA_IN_PAD = 768
ROPE_THETA = 10000.0
MLA_QK = MLA_NOPE + MLA_ROPE
LANES = 128

GDN_K_HEADS, GDN_V_HEADS, GDN_DK, GDN_DV, GDN_CONV = 8, 16, 128, 128, 4
GDN_QKW, GDN_VW = GDN_K_HEADS * GDN_DK, GDN_V_HEADS * GDN_DV
GDN_CONV_W = 2 * GDN_QKW + GDN_VW
GDN_IN = GDN_CONV_W + GDN_VW + 2 * GDN_V_HEADS
GDN_IN_PAD = 6400

ADAM_LR, ADAM_B1, ADAM_B2, ADAM_EPS, ADAM_WD, ADAM_STEP = 0.001, 0.9, 0.999, 1e-08, 0.01, 10

VMEM_LIMIT = 56 * 1024 * 1024

NN = ((1,), (0,))
NT = ((1,), (1,))
TN = ((0,), (0,))


def _cp(*sem):
    return pltpu.CompilerParams(dimension_semantics=sem, vmem_limit_bytes=VMEM_LIMIT)


def _pick(n, cands):
    for c in cands:
        if n % c == 0:
            return c
    return n


def _dg(a, b, dims=NN):
    return lax.dot_general(a.astype(BF16), b.astype(BF16), (dims, ((), ())), preferred_element_type=F32)


def _dot3(a, b, dims=NN):
    ah = a.astype(BF16)
    al = (a - ah.astype(F32)).astype(BF16)
    bh = b.astype(BF16)
    bl = (b - bh.astype(F32)).astype(BF16)
    d = lambda u, v: lax.dot_general(u, v, (dims, ((), ())), preferred_element_type=F32)
    return d(ah, bh) + (d(ah, bl) + d(al, bh))


def _sigmoid(x):
    return 1.0 / (1.0 + jnp.exp(-x))


def _softplus(x):
    return jnp.maximum(x, 0.0) + jnp.log(1.0 + jnp.exp(-jnp.abs(x)))


def _iota2(shape, dim):
    return lax.broadcasted_iota(jnp.int32, shape, dim)


def _mm(a, b, *, ta=False, tb=False, out_dtype=F32, epi=None, aux=None, into=None, name):
    m = a.shape[1] if ta else a.shape[0]
    k = a.shape[0] if ta else a.shape[1]
    n = b.shape[0] if tb else b.shape[1]
    assert k == (b.shape[1] if tb else b.shape[0]), (a.shape, b.shape, ta, tb)
    m_tile = m // 4 if into is not None and into[2] == "rows" else m
    tm = _pick(m_tile, (2048, 1024, 512, 384, 256, 128))
    tn = _pick(n, (1024, 640, 512, 768, 384, 256, 128))
    tk = _pick(k, (1024, 640, 512, 768, 384, 256, 128))
    nk = k // tk
    dims = ((0 if ta else 1,), (1 if tb else 0,))

    def finish(r, x_ref, o_ref):
        if epi == "relu2":
            r = jnp.square(jnp.maximum(r, 0.0))
        elif epi == "dact":
            r = r * (2.0 * jnp.sqrt(x_ref[...].astype(F32)))
        elif epi == "add":
            r = r + x_ref[...]
        o_ref[...] = r.astype(out_dtype)

    n_in = 2 + (aux is not None) + (into is not None)

    def body(*refs):
        a_ref, b_ref = refs[:2]
        x_ref = refs[2] if aux is not None else None
        o_ref = refs[n_in]
        if nk == 1:
            finish(_dg(a_ref[...], b_ref[...], dims), x_ref, o_ref)
            return
        acc = refs[-1]
        kk = pl.program_id(2)

        @pl.when(kk == 0)
        def _():
            acc[...] = jnp.zeros_like(acc)

        acc[...] += _dg(a_ref[...], b_ref[...], dims)

        @pl.when(kk == nk - 1)
        def _():
            finish(acc[...], x_ref, o_ref)

    a_spec = pl.BlockSpec((tk, tm), lambda i, j, q: (q, i)) if ta else pl.BlockSpec((tm, tk), lambda i, j, q: (i, q))
    b_spec = pl.BlockSpec((tn, tk), lambda i, j, q: (j, q)) if tb else pl.BlockSpec((tk, tn), lambda i, j, q: (q, j))
    o_spec = pl.BlockSpec((tm, tn), lambda i, j, q: (i, j))
    in_specs = [a_spec, b_spec] + ([o_spec] if aux is not None else [])
    args = (a, b) + ((aux,) if aux is not None else ())
    out_shape = jax.ShapeDtypeStruct((m, n), out_dtype)
    aliases = {}
    if into is not None:
        dst, row0, axis = into
        assert dst.dtype == out_dtype and row0 % tm == 0 and tn == FLAT_W and n == (FLAT_W if axis == "rows" else 4 * FLAT_W)
        per = m_tile // tm
        if axis == "rows":
            o_spec = pl.BlockSpec((None, tm, tn), lambda i, j, q: (i // per, row0 // tm + i % per, 0))
        else:
            o_spec = pl.BlockSpec((None, tm, tn), lambda i, j, q: (j, row0 // tm + i, 0))
        in_specs.append(pl.BlockSpec(memory_space=pl.ANY))
        args += (dst,)
        out_shape = jax.ShapeDtypeStruct(dst.shape, dst.dtype)
        aliases = {n_in - 1: 0}
    return pl.pallas_call(
        body, grid=(m // tm, n // tn, nk), in_specs=in_specs, out_specs=o_spec, out_shape=out_shape,
        scratch_shapes=[pltpu.VMEM((tm, tn), F32)] if nk > 1 else [], input_output_aliases=aliases,
        compiler_params=_cp("parallel", "parallel", "arbitrary"), name=name)(*args)


def _row_tile(t):
    return _pick(t, (1024, 512, 256, 128, 64, 8))


def _premod_fwd(x, g, sc, sh, *, name):
    t, c = x.shape
    tr = _row_tile(t)

    def body(x_ref, g_ref, sc_ref, sh_ref, h_ref):
        xv = x_ref[...]
        r = lax.rsqrt(jnp.mean(xv * xv, axis=-1, keepdims=True) + EPS)
        h_ref[...] = (((xv * r) * g_ref[...]) * (1.0 + sc_ref[...]) + sh_ref[...]).astype(BF16)

    row = pl.BlockSpec((tr, c), lambda i: (i, 0))
    vec = pl.BlockSpec((1, c), lambda i: (0, 0))
    return pl.pallas_call(body, grid=(t // tr,), in_specs=[row, vec, vec, vec], out_specs=row,
                          out_shape=jax.ShapeDtypeStruct((t, c), BF16), compiler_params=_cp("parallel"), name=name)(x, g, sc, sh)


def _premod_bwd(dh, x, g, sc, gin, *, name):
    t, c = x.shape
    tr = _row_tile(t)

    def body(dh_ref, x_ref, g_ref, sc_ref, gin_ref, gout_ref, dg_ref, dsc_ref, dsh_ref):
        @pl.when(pl.program_id(0) == 0)
        def _():
            dg_ref[...] = jnp.zeros_like(dg_ref)
            dsc_ref[...] = jnp.zeros_like(dsc_ref)
            dsh_ref[...] = jnp.zeros_like(dsh_ref)

        xv = x_ref[...]
        dhv = dh_ref[...].astype(F32)
        gv = g_ref[...]
        one_sc = 1.0 + sc_ref[...]
        r = lax.rsqrt(jnp.mean(xv * xv, axis=-1, keepdims=True) + EPS)
        nv = xv * r
        dsh_ref[...] += jnp.sum(dhv, axis=0, keepdims=True)
        dsc_ref[...] += jnp.sum(dhv * (nv * gv), axis=0, keepdims=True)
        dg_ref[...] += jnp.sum(dhv * nv * one_sc, axis=0, keepdims=True)
        dn = dhv * gv * one_sc
        dx = r * (dn - nv * jnp.mean(dn * nv, axis=-1, keepdims=True))
        gout_ref[...] = gin_ref[...] + dx

    row = pl.BlockSpec((tr, c), lambda i: (i, 0))
    vec = pl.BlockSpec((1, c), lambda i: (0, 0))
    vs = jax.ShapeDtypeStruct((1, c), F32)
    return pl.pallas_call(body, grid=(t // tr,), in_specs=[row, row, vec, vec, row], out_specs=[row, vec, vec, vec],
                          out_shape=[jax.ShapeDtypeStruct((t, c), F32), vs, vs, vs],
                          compiler_params=_cp("arbitrary"), name=name)(dh, x, g, sc, gin)


def _postres_fwd(x, y, g, gt, *, name):
    t, c = x.shape
    tr = _row_tile(t)

    def body(x_ref, y_ref, g_ref, gt_ref, o_ref):
        yv = y_ref[...]
        r = lax.rsqrt(jnp.mean(yv * yv, axis=-1, keepdims=True) + EPS)
        o_ref[...] = x_ref[...] + gt_ref[...] * ((yv * r) * g_ref[...])

    row = pl.BlockSpec((tr, c), lambda i: (i, 0))
    vec = pl.BlockSpec((1, c), lambda i: (0, 0))
    return pl.pallas_call(body, grid=(t // tr,), in_specs=[row, row, vec, vec], out_specs=row,
                          out_shape=jax.ShapeDtypeStruct((t, c), F32), compiler_params=_cp("parallel"), name=name)(x, y, g, gt)


def _postres_bwd(gout, y, g, gt, *, name):
    t, c = y.shape
    tr = _row_tile(t)

    def body(go_ref, y_ref, g_ref, gt_ref, dy_ref, dg_ref, dgt_ref):
        @pl.when(pl.program_id(0) == 0)
        def _():
            dg_ref[...] = jnp.zeros_like(dg_ref)
            dgt_ref[...] = jnp.zeros_like(dgt_ref)

        yv = y_ref[...]
        gov = go_ref[...]
        gv = g_ref[...]
        gtv = gt_ref[...]
        r = lax.rsqrt(jnp.mean(yv * yv, axis=-1, keepdims=True) + EPS)
        z = yv * r
        dgt_ref[...] += jnp.sum(gov * (z * gv), axis=0, keepdims=True)
        dg_ref[...] += jnp.sum(gov * gtv * z, axis=0, keepdims=True)
        dz = gov * gtv * gv
        dy_ref[...] = (r * (dz - z * jnp.mean(dz * z, axis=-1, keepdims=True))).astype(BF16)

    row = pl.BlockSpec((tr, c), lambda i: (i, 0))
    vec = pl.BlockSpec((1, c), lambda i: (0, 0))
    vs = jax.ShapeDtypeStruct((1, c), F32)
    return pl.pallas_call(body, grid=(t // tr,), in_specs=[row, row, vec, vec], out_specs=[row, vec, vec],
                          out_shape=[jax.ShapeDtypeStruct((t, c), BF16), vs, vs],
                          compiler_params=_cp("arbitrary"), name=name)(gout, y, g, gt)


def _loss_head(y, tgt, *, name):
    t, c = y.shape
    tr = _row_tile(t)

    def body(y_ref, t_ref, l_ref, dy_ref):
        @pl.when(pl.program_id(0) == 0)
        def _():
            l_ref[...] = jnp.zeros_like(l_ref)

        d = y_ref[...] - t_ref[...]
        dy_ref[...] = d * (1.0 / c)
        l_ref[...] += 0.5 * jnp.sum(jnp.mean(d * d, axis=-1, keepdims=True))

    row = pl.BlockSpec((tr, c), lambda i: (i, 0))
    return pl.pallas_call(body, grid=(t // tr,), in_specs=[row, row],
                          out_specs=[pl.BlockSpec((1, LANES), lambda i: (0, 0)), row],
                          out_shape=[jax.ShapeDtypeStruct((1, LANES), F32), jax.ShapeDtypeStruct((t, c), F32)],
                          compiler_params=_cp("arbitrary"), name=name)(y, tgt)


def _adamw(w, g, m, v, *, name):
    shape = w.shape
    c = shape[-1]
    r = math.prod(shape[:-1])
    w2, g2, m2, v2 = (a.reshape(r, c) for a in (w, g, m, v))
    tr = r
    for cand in (1024, 512, 256, 128, 64, 32, 16, 8):
        if r % cand == 0 and cand * c * 4 <= (1 << 20):
            tr = cand
            break
    c1 = 1.0 - ADAM_B1 ** ADAM_STEP
    c2 = 1.0 - ADAM_B2 ** ADAM_STEP

    def body(w_ref, g_ref, m_ref, v_ref, d_ref, nm_ref, nv_ref):
        gv = g_ref[...]
        mn = ADAM_B1 * m_ref[...] + (1.0 - ADAM_B1) * gv
        vn = ADAM_B2 * v_ref[...] + (1.0 - ADAM_B2) * jnp.square(gv)
        m_hat = mn / c1
        v_hat = vn / c2
        d_ref[...] = -ADAM_LR * (m_hat / (jnp.sqrt(v_hat) + ADAM_EPS) + ADAM_WD * w_ref[...])
        nm_ref[...] = mn
        nv_ref[...] = vn

    blk = pl.BlockSpec((tr, c), lambda i: (i, 0))
    s = jax.ShapeDtypeStruct((r, c), F32)
    d, nm, nv = pl.pallas_call(body, grid=(r // tr,), in_specs=[blk] * 4, out_specs=[blk] * 3, out_shape=[s, s, s],
                               compiler_params=_cp("parallel"), name=name)(w2, g2, m2, v2)
    return d.reshape(shape), nm.reshape(shape), nv.reshape(shape)


def _gla_parts(p_ref, wg_ref, bg_ref):
    q = p_ref[:, 0:GLA_KW] * (GLA_DK ** -0.5)
    k = p_ref[:, GLA_KW:2 * GLA_KW]
    glr = p_ref[:, 2 * GLA_KW + 2 * GLA_VW:GLA_IN_PAD]
    gate = _dg(glr, wg_ref[...]) + bg_ref[...]
    log_a = (jnp.minimum(gate, 0.0) - jnp.log(1.0 + jnp.exp(-jnp.abs(gate)))) * (1.0 / 16.0)
    tril = (_iota2((CHUNK, CHUNK), 0) >= _iota2((CHUNK, CHUNK), 1)).astype(F32)
    cum = _dot_sel(tril, log_a)
    c_last = cum[CHUNK - 1:CHUNK, :]
    f = jnp.exp(c_last - cum)
    dec = jnp.exp(c_last)
    return q, k, glr, gate, f, k * f, dec


def _gla_fwd(p, wg, bg, hg, *, name):
    t = p.shape[0]
    nc = t // CHUNK

    def body(p_ref, wg_ref, bg_ref, hg_ref, og_ref, s_ref, st):
        @pl.when(pl.program_id(0) == 0)
        def _():
            st[...] = jnp.zeros_like(st)

        q, _, _, _, _, ke, dec = _gla_parts(p_ref, wg_ref, bg_ref)
        hs = range(GLA_HEADS)
        ks = [slice(h * GLA_DK, (h + 1) * GLA_DK) for h in hs]
        vs = [slice(2 * GLA_KW + h * GLA_DV, 2 * GLA_KW + (h + 1) * GLA_DV) for h in hs]
        rs = [slice(2 * GLA_KW + GLA_VW + h * GLA_DV, 2 * GLA_KW + GLA_VW + (h + 1) * GLA_DV) for h in hs]
        s_new = [st[h] * dec[:, ks[h]] + _dg(p_ref[:, vs[h]], ke[:, ks[h]], TN) for h in hs]
        o = [_dg(q[:, ks[h]], s_new[h], NT) for h in hs]
        for h in hs:
            st[h] = s_new[h]
            s_ref[0, h] = s_new[h]
            rn = lax.rsqrt(jnp.mean(o[h] * o[h], axis=-1, keepdims=True) + EPS)
            rv = p_ref[:, rs[h]]
            og_ref[:, h * GLA_DV:(h + 1) * GLA_DV] = (((o[h] * rn) * hg_ref[...]) * (rv * _sigmoid(rv))).astype(BF16)

    full = lambda a: pl.BlockSpec(a.shape, lambda n: (0,) * a.ndim)
    return pl.pallas_call(
        body, grid=(nc,),
        in_specs=[pl.BlockSpec((CHUNK, GLA_IN_PAD), lambda n: (n, 0)), full(wg), full(bg), full(hg)],
        out_specs=[pl.BlockSpec((CHUNK, GLA_VW), lambda n: (n, 0)),
                   pl.BlockSpec((1, GLA_HEADS, GLA_DV, GLA_DK), lambda n: (n, 0, 0, 0))],
        out_shape=[jax.ShapeDtypeStruct((t, GLA_VW), BF16), jax.ShapeDtypeStruct((nc, GLA_HEADS, GLA_DV, GLA_DK), F32)],
        scratch_shapes=[pltpu.VMEM((GLA_HEADS, GLA_DV, GLA_DK), F32)],
        compiler_params=_cp("arbitrary"), name=name)(p, wg, bg, hg)


def _gla_bwd(p, dog, sall, wg, bg, hg, *, name):
    t = p.shape[0]
    nc = t // CHUNK

    def body(p_ref, dog_ref, s1_ref, s0_ref, wg_ref, bg_ref, hg_ref, dp_ref, dwg_ref, dbg_ref, dhg_ref, gt):
        i = pl.program_id(0)

        @pl.when(i == 0)
        def _():
            gt[...] = jnp.zeros_like(gt)
            dwg_ref[...] = jnp.zeros_like(dwg_ref)
            dbg_ref[...] = jnp.zeros_like(dbg_ref)
            dhg_ref[...] = jnp.zeros_like(dhg_ref)

        has_prev = (i < nc - 1).astype(F32)
        q, k, glr, gate, f, ke, dec = _gla_parts(p_ref, wg_ref, bg_ref)
        hgv = hg_ref[...]
        hs = range(GLA_HEADS)
        ks = [slice(h * GLA_DK, (h + 1) * GLA_DK) for h in hs]
        vs = [slice(2 * GLA_KW + h * GLA_DV, 2 * GLA_KW + (h + 1) * GLA_DV) for h in hs]
        rs = [slice(2 * GLA_KW + GLA_VW + h * GLA_DV, 2 * GLA_KW + GLA_VW + (h + 1) * GLA_DV) for h in hs]
        s1 = [s1_ref[0, h] for h in hs]
        o = [_dg(q[:, ks[h]], s1[h], NT) for h in hs]
        dhg = jnp.zeros((1, GLA_DV), F32)
        do = []
        for h in hs:
            rv = p_ref[:, rs[h]]
            rn = lax.rsqrt(jnp.mean(o[h] * o[h], axis=-1, keepdims=True) + EPS)
            z = o[h] * rn
            sg = _sigmoid(rv)
            sl = rv * sg
            dogh = dog_ref[:, h * GLA_DV:(h + 1) * GLA_DV].astype(F32)
            dhg = dhg + jnp.sum(dogh * z * sl, axis=0, keepdims=True)
            dp_ref[:, rs[h]] = (dogh * (z * hgv) * (sg * (1.0 + rv * (1.0 - sg)))).astype(BF16)
            dz = dogh * sl * hgv
            do.append(rn * (dz - z * jnp.mean(dz * z, axis=-1, keepdims=True)))
        dhg_ref[...] += dhg
        g_tot = [gt[h] + _dg(do[h], q[:, ks[h]], TN) for h in hs]
        dq = [_dg(do[h], s1[h], NN) for h in hs]
        dke_parts = [_dg(p_ref[:, vs[h]], g_tot[h], NN) for h in hs]
        dv = [_dg(ke[:, ks[h]], g_tot[h], NT) for h in hs]
        ddec_parts = []
        for h in hs:
            dp_ref[:, ks[h]] = (dq[h] * (GLA_DK ** -0.5)).astype(BF16)
            dp_ref[:, vs[h]] = dv[h].astype(BF16)
            ddec_parts.append(jnp.sum(g_tot[h] * (s0_ref[0, h] * has_prev), axis=0, keepdims=True))
            gt[h] = g_tot[h] * dec[:, ks[h]]
        dke = jnp.concatenate(dke_parts, axis=1)
        ddec = jnp.concatenate(ddec_parts, axis=1)
        dp_ref[:, GLA_KW:2 * GLA_KW] = (dke * f).astype(BF16)
        stril = (_iota2((CHUNK, CHUNK), 0) > _iota2((CHUNK, CHUNK), 1)).astype(F32)
        dlog_a = _dot_sel(stril, dke * ke) + ddec * dec
        dgate = dlog_a * (1.0 / 16.0) * _sigmoid(-gate)
        dp_ref[:, 2 * GLA_KW + 2 * GLA_VW:GLA_IN_PAD] = _dg(dgate, wg_ref[...], NT).astype(BF16)
        dwg_ref[...] += _dg(glr, dgate, TN)
        dbg_ref[...] += jnp.sum(dgate, axis=0, keepdims=True)

    full = lambda a: pl.BlockSpec(a.shape, lambda n: (0,) * a.ndim)
    sblk = (1, GLA_HEADS, GLA_DV, GLA_DK)
    return pl.pallas_call(
        body, grid=(nc,),
        in_specs=[pl.BlockSpec((CHUNK, GLA_IN_PAD), lambda n: (nc - 1 - n, 0)),
                  pl.BlockSpec((CHUNK, GLA_VW), lambda n: (nc - 1 - n, 0)),
                  pl.BlockSpec(sblk, lambda n: (nc - 1 - n, 0, 0, 0)),
                  pl.BlockSpec(sblk, lambda n: (jnp.maximum(nc - 2 - n, 0), 0, 0, 0)),
                  full(wg), full(bg), full(hg)],
        out_specs=[pl.BlockSpec((CHUNK, GLA_IN_PAD), lambda n: (nc - 1 - n, 0)), full(wg), full(bg), full(hg)],
        out_shape=[jax.ShapeDtypeStruct((t, GLA_IN_PAD), BF16), jax.ShapeDtypeStruct(wg.shape, F32),
                   jax.ShapeDtypeStruct(bg.shape, F32), jax.ShapeDtypeStruct(hg.shape, F32)],
        scratch_shapes=[pltpu.VMEM((GLA_HEADS, GLA_DV, GLA_DK), F32)],
        compiler_params=_cp("arbitrary"), name=name)(p, dog, sall, sall, wg, bg, hg)


def _gla_layer_fwd(h, w, tag):
    p = _mm(h, w["w_in"], name=tag + "_in")
    og, sall = _gla_fwd(p, w["wg"], w["bg"], w["hg"], name=tag + "_scan")
    y = _mm(og, w["w_out"], name=tag + "_out")
    return y, (p, og, sall)


def _dw_out(act, dy, sink, name):
    if sink is None:
        return _mm(act, dy, ta=True, name=name), None
    return None, _mm(act, dy, ta=True, out_dtype=sink[0].dtype, into=(sink[0], sink[1], "rows"), name=name)


def _gla_layer_bwd(dy, h, saved, w, tag, sink=None):
    p, og, sall = saved
    dog = _mm(dy, w["w_out"], tb=True, name=tag + "_dog")
    dw_out, gf = _dw_out(og, dy, sink, tag + "_dwout")
    dp, dwg, dbg, dhg = _gla_bwd(p, dog, sall, w["wg"], w["bg"], w["hg"], name=tag + "_scanb")
    dw_in = _mm(h, dp, ta=True, out_dtype=BF16, name=tag + "_dwin")
    dh = _mm(dp, w["w_in"], tb=True, out_dtype=BF16, name=tag + "_dh")
    grads = dict(w_in=dw_in[:, :GLA_IN], w_gate_up=dwg[:GLA_RANK], b_gate=dbg[0], head_g=dhg[0], w_out=dw_out)
    return dh, grads, gf


def _rope_tables(pos, inv_freq, *, name):
    t = pos.shape[0]
    tr = _row_tile(t)
    half = MLA_ROPE // 2

    def body(p_ref, f_ref, c_ref, s1_ref, s2_ref, s1b_ref, s2b_ref):
        ang = p_ref[...].astype(F32) * f_ref[...]
        lane = _iota2((tr, LANES), 1)
        lo = (lane >= MLA_NOPE) & (lane < MLA_NOPE + half)
        hi = (lane >= MLA_NOPE + half) & (lane < MLA_QK)
        cs, sn = jnp.cos(ang), jnp.sin(ang)
        zero = jnp.zeros_like(cs)
        c_ref[...] = jnp.where(lane < MLA_NOPE, 1.0, jnp.where(lane < MLA_QK, cs, 0.0))
        s1_ref[...] = jnp.where(lo, -sn, zero)
        s2_ref[...] = jnp.where(hi, sn, zero)
        s1b_ref[...] = jnp.where(lo, sn, zero)
        s2b_ref[...] = jnp.where(hi, -sn, zero)

    row = pl.BlockSpec((tr, LANES), lambda i: (i, 0))
    s = jax.ShapeDtypeStruct((t, LANES), F32)
    return pl.pallas_call(body, grid=(t // tr,),
                          in_specs=[pl.BlockSpec((tr, 1), lambda i: (i, 0)), pl.BlockSpec((1, LANES), lambda i: (0, 0))],
                          out_specs=[row] * 5, out_shape=[s] * 5, compiler_params=_cp("parallel"), name=name)(pos, inv_freq)


def _rope(x, c, s1, s2, *, out_dtype, sum_heads=False, name):
    nh, t, _ = x.shape
    tr = _row_tile(t)
    half = MLA_ROPE // 2

    def body(x_ref, c_ref, s1_ref, s2_ref, o_ref):
        total = None
        for h in range(nh):
            xv = x_ref[h].astype(F32)
            y = xv * c_ref[...] + pltpu.roll(xv, LANES - half, 1) * s1_ref[...] + pltpu.roll(xv, half, 1) * s2_ref[...]
            if sum_heads:
                total = y if total is None else total + y
            else:
                o_ref[h] = y.astype(out_dtype)
        if sum_heads:
            o_ref[...] = total

    tab = pl.BlockSpec((tr, LANES), lambda i: (i, 0))
    xs = pl.BlockSpec((nh, tr, LANES), lambda i: (0, i, 0))
    if sum_heads:
        return pl.pallas_call(body, grid=(t // tr,), in_specs=[xs, tab, tab, tab], out_specs=tab,
                              out_shape=jax.ShapeDtypeStruct((t, LANES), F32),
                              compiler_params=_cp("parallel"), name=name)(x, c, s1, s2)
    return pl.pallas_call(body, grid=(t // tr,), in_specs=[xs, tab, tab, tab], out_specs=xs,
                          out_shape=jax.ShapeDtypeStruct(x.shape, out_dtype),
                          compiler_params=_cp("parallel"), name=name)(x, c, s1, s2)


FLASH_BLK = 512


def _diag_mask(blk):
    return (_iota2((blk, blk), 1) // CHUNK) <= (_iota2((blk, blk), 0) // CHUNK)


def _flash_fwd(q, k, v, *, name):
    nh, t, _ = q.shape
    blk = min(FLASH_BLK, t)
    scale = MLA_QK ** -0.5

    def body(q_ref, k_ref, v_ref, o_ref, lse_ref):
        i = pl.program_id(1)
        qv = q_ref[0]

        def step(j, carry, masked):
            m, l, acc = carry
            off = pl.multiple_of(j * blk, blk)
            kb = k_ref[0, pl.ds(off, blk), :]
            vb = v_ref[0, pl.ds(off, blk), :]
            s = _dg(qv, kb, NT) * scale
            if masked:
                s = jnp.where(_diag_mask(blk), s, NEG_INF)
            m_new = jnp.maximum(m, jnp.max(s, axis=-1, keepdims=True))
            p = jnp.exp(s - m_new)
            alpha = jnp.exp(m - m_new)
            return m_new, alpha * l + jnp.sum(p, axis=-1, keepdims=True), alpha * acc + _dg(p, vb, NN)

        init = (jnp.full((blk, 1), NEG_INF, F32), jnp.zeros((blk, 1), F32), jnp.zeros((blk, MLA_V), F32))
        carry = lax.fori_loop(0, i, lambda j, c: step(j, c, False), init)
        m, l, acc = step(i, carry, True)
        o_ref[0] = (acc / l).astype(BF16)
        lse_ref[0] = m + jnp.log(l)

    qs = pl.BlockSpec((1, blk, LANES), lambda h, i: (h, i, 0))
    return pl.pallas_call(
        body, grid=(nh, t // blk),
        in_specs=[qs, pl.BlockSpec((1, t, LANES), lambda h, i: (h, 0, 0)), pl.BlockSpec((1, t, MLA_V), lambda h, i: (h, 0, 0))],
        out_specs=[pl.BlockSpec((1, blk, MLA_V), lambda h, i: (h, i, 0)), pl.BlockSpec((1, blk, 1), lambda h, i: (h, i, 0))],
        out_shape=[jax.ShapeDtypeStruct((nh, t, MLA_V), BF16), jax.ShapeDtypeStruct((nh, t, 1), F32)],
        compiler_params=_cp("parallel", "parallel"), name=name)(q, k, v)


def _flash_bwd(q, k, v, do, o, lse, *, name):
    nh, t, _ = q.shape
    blk = min(FLASH_BLK, t)
    nq = t // blk
    scale = MLA_QK ** -0.5

    def body(q_ref, k_ref, v_ref, do_ref, o_ref, lse_ref, dq_ref, dk_ref, dv_ref, dl):
        j = pl.program_id(1)

        @pl.when(j == 0)
        def _():
            dq_ref[...] = jnp.zeros_like(dq_ref)
            dl[...] = jnp.sum(do_ref[0].astype(F32) * o_ref[0].astype(F32), axis=-1, keepdims=True)

        kb = k_ref[0]
        vb = v_ref[0]

        def step(i, carry, masked):
            dk, dv = carry
            rows = pl.ds(pl.multiple_of(i * blk, blk), blk)
            qb = q_ref[0, rows, :]
            dob = do_ref[0, rows, :]
            s = _dg(qb, kb, NT) * scale
            if masked:
                s = jnp.where(_diag_mask(blk), s, NEG_INF)
            p = jnp.exp(s - lse_ref[0, rows, :])
            ds = p * (_dg(dob, vb, NT) - dl[rows, :]) * scale
            dq_ref[0, rows, :] += _dg(ds, kb, NN)
            return dk + _dg(ds, qb, TN), dv + _dg(p, dob, TN)

        carry = step(j, (jnp.zeros((blk, LANES), F32), jnp.zeros((blk, MLA_V), F32)), True)
        dk, dv = lax.fori_loop(j + 1, nq, lambda i, c: step(i, c, False), carry)
        dk_ref[0] = dk
        dv_ref[0] = dv

    ks = pl.BlockSpec((1, blk, LANES), lambda h, j: (h, j, 0))
    vs = pl.BlockSpec((1, blk, MLA_V), lambda h, j: (h, j, 0))
    fl = lambda w: pl.BlockSpec((1, t, w), lambda h, j: (h, 0, 0))
    return pl.pallas_call(
        body, grid=(nh, nq),
        in_specs=[fl(LANES), ks, vs, fl(MLA_V), fl(MLA_V), fl(1)],
        out_specs=[fl(LANES), ks, vs],
        out_shape=[jax.ShapeDtypeStruct((nh, t, LANES), F32), jax.ShapeDtypeStruct((nh, t, LANES), F32),
                   jax.ShapeDtypeStruct((nh, t, MLA_V), F32)],
        scratch_shapes=[pltpu.VMEM((t, 1), F32)],
        compiler_params=_cp("parallel", "arbitrary"), name=name)(q, k, v, do, o, lse)


def _heads_first(a, width):
    t = a.shape[0]
    return a.reshape(t, MLA_HEADS, width).transpose(1, 0, 2)


def _heads_last(a):
    return a.transpose(1, 0, 2).reshape(a.shape[1], -1)


def _mla_layer_fwd(h, pos, w, tag):
    t = h.shape[0]
    zq = jnp.zeros((1, MLA_Q_RANK), F32)
    zkv = jnp.zeros((1, MLA_KV_RANK), F32)
    p = _mm(h, w["w_in"], name=tag + "_in")
    cq, ckv, krp = p[:, :MLA_Q_RANK], p[:, MLA_Q_RANK:MLA_Q_RANK + MLA_KV_RANK], p[:, MLA_Q_RANK + MLA_KV_RANK:MLA_IN]
    qn = _premod_fwd(cq, w["q_norm_g"], zq, zq, name=tag + "_qnorm")
    kvn = _premod_fwd(ckv, w["kv_norm_g"], zkv, zkv, name=tag + "_kvnorm")
    q = _mm(qn, w["w_uq"], name=tag + "_uq")
    kv = _mm(kvn, w["w_ukv"], name=tag + "_ukv")
    q_pre = jnp.pad(_heads_first(q, MLA_QK), ((0, 0), (0, 0), (0, LANES - MLA_QK)))
    kv3 = _heads_first(kv, MLA_NOPE + MLA_V)
    k_pre = jnp.concatenate([kv3[:, :, :MLA_NOPE], jnp.broadcast_to(krp[None], (MLA_HEADS, t, MLA_ROPE)),
                             jnp.zeros((MLA_HEADS, t, LANES - MLA_QK), F32)], axis=-1)
    vh = kv3[:, :, MLA_NOPE:].astype(BF16)
    half = MLA_ROPE // 2
    freq = ROPE_THETA ** (-jnp.arange(half, dtype=F32) / half)
    inv_freq = jnp.concatenate([jnp.zeros((MLA_NOPE,), F32), freq, freq, jnp.zeros((LANES - MLA_QK,), F32)])[None]
    tabs = _rope_tables(pos.reshape(t, 1), inv_freq, name=tag + "_tables")
    qr = _rope(q_pre, tabs[0], tabs[1], tabs[2], out_dtype=BF16, name=tag + "_ropeq")
    kr = _rope(k_pre, tabs[0], tabs[1], tabs[2], out_dtype=BF16, name=tag + "_ropek")
    o, lse = _flash_fwd(qr, kr, vh, name=tag + "_attn")
    of = _heads_last(o)
    y = _mm(of, w["w_out"], name=tag + "_out")
    return y, (cq, ckv, qn, kvn, qr, kr, vh, o, lse, of, tabs)


def _mla_layer_bwd(dy, h, saved, w, tag, sink=None):
    cq, ckv, qn, kvn, qr, kr, vh, o, lse, of, tabs = saved
    t = h.shape[0]
    zq = jnp.zeros((1, MLA_Q_RANK), F32)
    zkv = jnp.zeros((1, MLA_KV_RANK), F32)
    dof = _mm(dy, w["w_out"], tb=True, out_dtype=BF16, name=tag + "_dof")
    dw_out, gf = _dw_out(of, dy, sink, tag + "_dwout")
    do = _heads_first(dof, MLA_V)
    dqr, dkr, dv = _flash_bwd(qr, kr, vh, do, o, lse, name=tag + "_attn_b")
    dq_pre = _rope(dqr, tabs[0], tabs[3], tabs[4], out_dtype=F32, name=tag + "_ropeq_b")
    dk_sum = _rope(dkr, tabs[0], tabs[3], tabs[4], out_dtype=F32, sum_heads=True, name=tag + "_ropek_b")
    dq = _heads_last(dq_pre[:, :, :MLA_QK])
    dkv = _heads_last(jnp.concatenate([dkr[:, :, :MLA_NOPE], dv], axis=-1))
    dw_uq = _mm(qn, dq, ta=True, out_dtype=BF16, name=tag + "_dwuq")
    dqn = _mm(dq, w["w_uq"], tb=True, name=tag + "_dqn")
    dw_ukv = _mm(kvn, dkv, ta=True, out_dtype=BF16, name=tag + "_dwukv")
    dkvn = _mm(dkv, w["w_ukv"], tb=True, name=tag + "_dkvn")
    dcq, dqg, _, _ = _premod_bwd(dqn, cq, w["q_norm_g"], zq, jnp.zeros_like(cq), name=tag + "_qnorm_b")
    dckv, dkvg, _, _ = _premod_bwd(dkvn, ckv, w["kv_norm_g"], zkv, jnp.zeros_like(ckv), name=tag + "_kvnorm_b")
    dp = jnp.concatenate([dcq, dckv, dk_sum[:, MLA_NOPE:MLA_QK], jnp.zeros((t, MLA_IN_PAD - MLA_IN), F32)], axis=1).astype(BF16)
    dw_in = _mm(h, dp, ta=True, out_dtype=BF16, name=tag + "_dwin")
    dh = _mm(dp, w["w_in"], tb=True, out_dtype=BF16, name=tag + "_dh")
    grads = dict(w_in=dw_in[:, :MLA_IN], q_norm_g=dqg[0], w_uq=dw_uq, kv_norm_g=dkvg[0], w_ukv=dw_ukv, w_out=dw_out)
    return dh, grads, gf


CONV_HALO = 8


def _conv_tiles(t):
    return min(512, t), 1024


def _gdn_conv_fwd(p, w, *, name):
    t = p.shape[0]
    tr, tc = _conv_tiles(t)
    hb = tr // CONV_HALO

    def body(x_ref, halo_ref, w_ref, o_ref, buf):
        i = pl.program_id(0)
        buf[0:CONV_HALO, :] = halo_ref[...] * (i > 0).astype(F32)
        buf[CONV_HALO:CONV_HALO + tr, :] = x_ref[...]
        base = CONV_HALO - (GDN_CONV - 1)
        acc = buf[pl.ds(base, tr), :] * w_ref[0:1, :]
        for j in range(1, GDN_CONV):
            acc = acc + buf[pl.ds(base + j, tr), :] * w_ref[j:j + 1, :]
        o_ref[...] = acc * _sigmoid(acc)

    return pl.pallas_call(
        body, grid=(t // tr, GDN_CONV_W // tc),
        in_specs=[pl.BlockSpec((tr, tc), lambda i, j: (i, j)),
                  pl.BlockSpec((CONV_HALO, tc), lambda i, j: (jnp.maximum(i * hb - 1, 0), j)),
                  pl.BlockSpec((GDN_CONV, tc), lambda i, j: (0, j))],
        out_specs=pl.BlockSpec((tr, tc), lambda i, j: (i, j)),
        out_shape=jax.ShapeDtypeStruct((t, GDN_CONV_W), F32),
        scratch_shapes=[pltpu.VMEM((tr + CONV_HALO, tc), F32)],
        compiler_params=_cp("parallel", "parallel"), name=name)(p, p, w)


def _gdn_conv_bwd(d, p, w, col0, *, name):
    t = p.shape[0]
    tr, tc = _conv_tiles(t)
    hb = tr // CONV_HALO
    nr = t // tr
    ext = tr + CONV_HALO

    def body(x_ref, xp_ref, xn_ref, d_ref, dn_ref, w_ref, dx_ref, dw_ref, bufx, bufd):
        i = pl.program_id(1)

        @pl.when(i == 0)
        def _():
            dw_ref[...] = jnp.zeros_like(dw_ref)

        last = (i < nr - 1).astype(F32)
        bufx[0:CONV_HALO, :] = xp_ref[...] * (i > 0).astype(F32)
        bufx[CONV_HALO:CONV_HALO + tr, :] = x_ref[...]
        bufx[CONV_HALO + tr:, :] = xn_ref[...] * last
        base = CONV_HALO - (GDN_CONV - 1)
        acc = bufx[pl.ds(base, ext), :] * w_ref[0:1, :]
        for j in range(1, GDN_CONV):
            acc = acc + bufx[pl.ds(base + j, ext), :] * w_ref[j:j + 1, :]
        sg = _sigmoid(acc)
        dsilu = sg * (1.0 + acc * (1.0 - sg))
        bufd[0:tr, :] = d_ref[...] * dsilu[0:tr, :]
        bufd[tr:, :] = dn_ref[...] * last * dsilu[tr:, :]
        dx = bufd[pl.ds(GDN_CONV - 1, tr), :] * w_ref[0:1, :]
        for j in range(1, GDN_CONV):
            dx = dx + bufd[pl.ds(GDN_CONV - 1 - j, tr), :] * w_ref[j:j + 1, :]
        dx_ref[...] = dx.astype(BF16)
        dc = bufd[0:tr, :]
        for j in range(GDN_CONV):
            dw_ref[j:j + 1, :] += jnp.sum(dc * bufx[pl.ds(base + j, tr), :], axis=0, keepdims=True)

    cb = col0 // tc
    width = d.shape[1]
    main = lambda off: pl.BlockSpec((tr, tc), lambda j, i: (i, j + off))
    prev = pl.BlockSpec((CONV_HALO, tc), lambda j, i: (jnp.maximum(i * hb - 1, 0), j + cb))
    nxt = lambda off: pl.BlockSpec((CONV_HALO, tc), lambda j, i: (jnp.minimum((i + 1) * hb, t // CONV_HALO - 1), j + off))
    wsp = lambda off: pl.BlockSpec((GDN_CONV, tc), lambda j, i: (0, j + off))
    return pl.pallas_call(
        body, grid=(width // tc, nr),
        in_specs=[main(cb), prev, nxt(cb), main(0), nxt(0), wsp(cb)], out_specs=[main(0), wsp(0)],
        out_shape=[jax.ShapeDtypeStruct((t, width), BF16), jax.ShapeDtypeStruct((GDN_CONV, width), F32)],
        scratch_shapes=[pltpu.VMEM((tr + 2 * CONV_HALO, tc), F32), pltpu.VMEM((ext, tc), F32)],
        compiler_params=_cp("parallel", "arbitrary"), name=name)(p, p, p, d, d, w)


def _dot_sel(sel, b, dims=NN, sel_first=True):
    s = sel.astype(BF16)
    b1 = b.astype(BF16)
    r1 = b - b1.astype(F32)
    b2 = r1.astype(BF16)
    b3 = (r1 - b2.astype(F32)).astype(BF16)
    if sel_first:
        d = lambda v: lax.dot_general(s, v, (dims, ((), ())), preferred_element_type=F32)
    else:
        d = lambda v: lax.dot_general(v, s, (dims, ((), ())), preferred_element_type=F32)
    return d(b1) + (d(b2) + d(b3))


def _gdn_chunks(qraws, kraws, vs, braws, araws, alogs, dtbs):
    nv = len(vs)
    row = _iota2((CHUNK, CHUNK), 0)
    col = _iota2((CHUNK, CHUNK), 1)
    strict = row > col
    triu = (row <= col).astype(F32)
    tril = (row >= col).astype(F32)
    ones = jnp.ones((CHUNK, CHUNK), F32)
    keys = []
    for qraw, kraw in zip(qraws, kraws):
        rq = lax.rsqrt(jnp.sum(qraw * qraw, axis=-1, keepdims=True) + EPS)
        rk = lax.rsqrt(jnp.sum(kraw * kraw, axis=-1, keepdims=True) + EPS)
        qn = qraw * rq
        keys.append(dict(rq=rq, rk=rk, qn=qn, qh=qn * (GDN_DK ** -0.5), kh=kraw * rk))
    kks = [_dg(kd["kh"], kd["kh"], NT) for kd in keys]
    cs = []
    for h in range(nv):
        c = dict(keys[h // 2])
        c.update(v=vs[h], kk=kks[h // 2], strict=strict, triu=triu)
        c["beta"] = _sigmoid(braws[h])
        c["ea"] = jnp.exp(alogs[h])
        c["xs"] = araws[h] + dtbs[h]
        c["g"] = -c["ea"] * _softplus(c["xs"])
        cs.append(c)
    gbs = [jnp.broadcast_to(c["g"], (CHUNK, LANES)) for c in cs]
    cums = [_dot_sel(tril, gb) for gb in gbs]
    cum_js = [_dot_sel(ones, gb[:, :CHUNK] * triu) for gb in gbs]
    for c, cum, cum_j in zip(cs, cums, cum_js):
        diff = jnp.where(strict, cum[:, :CHUNK] - cum_j, 0.0)
        c["dm"] = jnp.where(strict, jnp.exp(diff), 0.0)
        c["a"] = (c["beta"] * c["dm"]) * c["kk"]
        c_last = cum[CHUNK - 1:CHUNK, :]
        c["e"] = jnp.exp(cum)
        c["f"] = jnp.exp(c_last - cum)
        c["dec"] = jnp.exp(c_last)
        c["rv"] = c["beta"] * c["v"]
        c["rk_rhs"] = (c["beta"] * c["e"]) * c["kh"]
        c["ke"] = c["kh"] * c["f"]
    return cs


def _unit_lower_inverses(mats):
    eye = (_iota2((CHUNK, CHUNK), 0) == _iota2((CHUNK, CHUNK), 1)).astype(F32)
    ts = [eye - a for a in mats]
    pws = list(mats)
    for _ in range(5):
        pws = [_dot3(pw, pw) for pw in pws]
        ts = [t + _dot3(t, pw) for t, pw in zip(ts, pws)]
    return ts


GDN_HB = 16


def _gdn_specs(chunk_of):
    hb = GDN_HB
    kw = hb // 2 * GDN_DK
    vw = hb * GDN_DV
    qs = pl.BlockSpec((CHUNK, kw), lambda g, n: (chunk_of(n), g))
    ks = pl.BlockSpec((CHUNK, kw), lambda g, n: (chunk_of(n), GDN_QKW // kw + g))
    vs = pl.BlockSpec((CHUNK, vw), lambda g, n: (chunk_of(n), 2 * GDN_QKW // vw + g))
    zs = pl.BlockSpec((CHUNK, vw), lambda g, n: (chunk_of(n), GDN_CONV_W // vw + g))
    assert hb == GDN_V_HEADS and (GDN_CONV_W + GDN_VW) % LANES == 0
    gates = pl.BlockSpec((CHUNK, LANES), lambda g, n: (chunk_of(n), (GDN_CONV_W + GDN_VW) // LANES))
    one = pl.BlockSpec((hb, 1, 1), lambda g, n: (g, 0, 0))
    ng = pl.BlockSpec((1, GDN_DV), lambda g, n: (0, 0))
    hd = pl.BlockSpec((CHUNK, vw), lambda g, n: (chunk_of(n), g))
    return qs, ks, vs, zs, gates, one, ng, hd


def _gate_columns(gates):
    return ([gates[:, h:h + 1] for h in range(GDN_V_HEADS)],
            [gates[:, GDN_V_HEADS + h:GDN_V_HEADS + h + 1] for h in range(GDN_V_HEADS)])


def _gdn_fwd(qkv, p, alog, dtb, ng, *, name):
    t = qkv.shape[0]
    nc = t // CHUNK
    nh = GDN_V_HEADS

    def body(q_ref, k_ref, v_ref, z_ref, gates_ref, alog_ref, dtb_ref, ng_ref, og_ref, s_ref, t_ref, st):
        @pl.when(pl.program_id(1) == 0)
        def _():
            st[...] = jnp.zeros_like(st)

        hs = range(GDN_HB)
        kqs = [slice(j * GDN_DK, (j + 1) * GDN_DK) for j in range(GDN_HB // 2)]
        vsl = [slice(h * GDN_DV, (h + 1) * GDN_DV) for h in hs]
        braws, araws = _gate_columns(gates_ref[...])
        cs = _gdn_chunks([q_ref[:, s] for s in kqs], [k_ref[:, s] for s in kqs], [v_ref[:, s] for s in vsl],
                         braws, araws, [alog_ref[h] for h in hs], [dtb_ref[h] for h in hs])
        tms = _unit_lower_inverses([c["a"] for c in cs])
        s0 = [st[h] for h in hs]
        wv = [_dot3(tms[h], cs[h]["rv"]) for h in hs]
        wk = [_dot3(tms[h], cs[h]["rk_rhs"]) for h in hs]
        u = [wv[h] - _dg(wk[h], s0[h], NN) for h in hs]
        s1 = [cs[h]["dec"] * s0[h] + _dg(cs[h]["ke"], u[h], TN) for h in hs]
        o = [_dg(cs[h]["qh"], s1[h], NN) for h in hs]
        for h in hs:
            t_ref[h, 0] = tms[h]
            st[h] = s1[h]
            s_ref[h, 0] = s1[h]
            rn = lax.rsqrt(jnp.mean(o[h] * o[h], axis=-1, keepdims=True) + EPS)
            zv = z_ref[:, vsl[h]]
            og_ref[:, vsl[h]] = (((o[h] * rn) * ng_ref[...]) * (zv * _sigmoid(zv))).astype(BF16)

    qs, ks, vs, zs, gates, one, ngs, hd = _gdn_specs(lambda n: n)
    return pl.pallas_call(
        body, grid=(nh // GDN_HB, nc),
        in_specs=[qs, ks, vs, zs, gates, one, one, ngs],
        out_specs=[hd,
                   pl.BlockSpec((GDN_HB, 1, GDN_DK, GDN_DV), lambda g, n: (g, n, 0, 0)),
                   pl.BlockSpec((GDN_HB, 1, CHUNK, CHUNK), lambda g, n: (g, n, 0, 0))],
        out_shape=[jax.ShapeDtypeStruct((t, GDN_VW), BF16), jax.ShapeDtypeStruct((nh, nc, GDN_DK, GDN_DV), F32),
                   jax.ShapeDtypeStruct((nh, nc, CHUNK, CHUNK), F32)],
        scratch_shapes=[pltpu.VMEM((GDN_HB, GDN_DK, GDN_DV), F32)],
        compiler_params=_cp("parallel", "arbitrary"), name=name)(qkv, qkv, qkv, p, p, alog, dtb, ng)


def _gdn_bwd(qkv, p, alog, dtb, ng, dog, sall, tall, *, name):
    t = qkv.shape[0]
    nc = t // CHUNK
    nh = GDN_V_HEADS

    def body(q_ref, k_ref, v_ref, z_ref, gates_ref, alog_ref, dtb_ref, ng_ref, dog_ref, s1_ref, s0_ref, t_ref,
             dq_ref, dk_ref, dv_ref, dz_ref, dgates_ref, dalog_ref, ddtb_ref, dng_ref, gc):
        grp = pl.program_id(0)
        i = pl.program_id(1)

        @pl.when(i == 0)
        def _():
            gc[...] = jnp.zeros_like(gc)
            dalog_ref[...] = jnp.zeros_like(dalog_ref)
            ddtb_ref[...] = jnp.zeros_like(ddtb_ref)

        @pl.when((i == 0) & (grp == 0))
        def _():
            dng_ref[...] = jnp.zeros_like(dng_ref)

        has_prev = (i < nc - 1).astype(F32)
        ngv = ng_ref[...]
        ones = jnp.ones((CHUNK, LANES), F32)
        hs = range(GDN_HB)
        kqs = [slice(j * GDN_DK, (j + 1) * GDN_DK) for j in range(GDN_HB // 2)]
        vsl = [slice(h * GDN_DV, (h + 1) * GDN_DV) for h in hs]
        braws, araws = _gate_columns(gates_ref[...])
        cs = _gdn_chunks([q_ref[:, s] for s in kqs], [k_ref[:, s] for s in kqs], [v_ref[:, s] for s in vsl],
                         braws, araws, [alog_ref[h] for h in hs], [dtb_ref[h] for h in hs])
        tms = [t_ref[h, 0] for h in hs]
        s1 = [s1_ref[h, 0] for h in hs]
        s0 = [s0_ref[h, 0] * has_prev for h in hs]
        wv = [_dot3(tms[h], cs[h]["rv"]) for h in hs]
        wk = [_dot3(tms[h], cs[h]["rk_rhs"]) for h in hs]
        u = [wv[h] - _dg(wk[h], s0[h], NN) for h in hs]
        o = [_dg(cs[h]["qh"], s1[h], NN) for h in hs]
        dng = jnp.zeros((1, GDN_DV), F32)
        do = []
        for h in hs:
            zv = z_ref[:, vsl[h]]
            dogv = dog_ref[:, vsl[h]]
            rn = lax.rsqrt(jnp.mean(o[h] * o[h], axis=-1, keepdims=True) + EPS)
            zo = o[h] * rn
            sg = _sigmoid(zv)
            sl = zv * sg
            dng = dng + jnp.sum(dogv * zo * sl, axis=0, keepdims=True)
            dz_ref[:, vsl[h]] = (dogv * (zo * ngv) * (sg * (1.0 + zv * (1.0 - sg)))).astype(BF16)
            dzo = dogv * sl * ngv
            do.append(rn * (dzo - zo * jnp.mean(dzo * zo, axis=-1, keepdims=True)))
        dng_ref[...] += dng
        g_tot = [gc[h] + _dg(cs[h]["qh"], do[h], TN) for h in hs]
        dqh = [_dg(do[h], s1[h], NT) for h in hs]
        dke = [_dg(u[h], g_tot[h], NT) for h in hs]
        du = [_dg(cs[h]["ke"], g_tot[h], NN) for h in hs]
        gnew = [cs[h]["dec"] * g_tot[h] - _dg(wk[h], du[h], TN) for h in hs]
        dwk = [-_dg(du[h], s0[h], NT) for h in hs]
        drv = [_dot3(tms[h], du[h], TN) for h in hs]
        drk = [_dot3(tms[h], dwk[h], TN) for h in hs]
        da = [jnp.where(cs[h]["strict"], -(_dot3(drv[h], wv[h], NT) + _dot3(drk[h], wk[h], NT)), 0.0) for h in hs]
        mx = [da[h] * cs[h]["dm"] * cs[h]["kk"] for h in hs]
        aa = [mx[h] * cs[h]["beta"] for h in hs]
        colsum = [_dot_sel(ones, aa[h], TN, sel_first=False)[:, 0:1] for h in hs]
        bm = [(da[h] * cs[h]["beta"]) * cs[h]["dm"] for h in hs]
        dkh = [_dg(bm[h], cs[h]["kh"], NN) + _dg(bm[h], cs[h]["kh"], TN) for h in hs]
        dcum, dcl, dbeta = [], [], []
        for h in hs:
            c = cs[h]
            beta, kh, e, f, dec, ke = c["beta"], c["kh"], c["e"], c["f"], c["dec"], c["ke"]
            gc[h] = gnew[h]
            ddec = jnp.sum(jnp.sum(g_tot[h] * s0[h], axis=1, keepdims=True), axis=0, keepdims=True)
            dv_ref[:, vsl[h]] = beta * drv[h]
            db = jnp.sum(mx[h], axis=1, keepdims=True) + jnp.sum(drv[h] * c["v"], axis=1, keepdims=True)
            dbeta.append(db + jnp.sum(drk[h] * (e * kh), axis=1, keepdims=True))
            dkh[h] = dkh[h] + (beta * e) * drk[h] + f * dke[h]
            ef = jnp.sum(dke[h] * ke, axis=1, keepdims=True)
            dcum.append(jnp.sum(aa[h], axis=1, keepdims=True) - colsum[h] + jnp.sum(drk[h] * c["rk_rhs"], axis=1, keepdims=True) - ef)
            dcl.append(jnp.sum(ef, axis=0, keepdims=True) + ddec * dec[:, 0:1])
        dg = [_dot_sel(cs[h]["triu"], jnp.broadcast_to(dcum[h], (CHUNK, LANES)))[:, 0:1] + dcl[h] for h in hs]
        lane = _iota2((CHUNK, LANES), 1)
        dgates = jnp.zeros((CHUNK, LANES), F32)
        for h in hs:
            c = cs[h]
            beta = c["beta"]
            daraw = dg[h] * (-c["ea"]) * _sigmoid(c["xs"])
            dgates = jnp.where(lane == h, dbeta[h] * beta * (1.0 - beta), dgates)
            dgates = jnp.where(lane == GDN_V_HEADS + h, daraw, dgates)
            dalog_ref[h] += jnp.sum(dg[h] * c["g"], axis=0, keepdims=True)
            ddtb_ref[h] += jnp.sum(daraw, axis=0, keepdims=True)
        dgates_ref[...] = dgates.astype(BF16)
        for j, sl in enumerate(kqs):
            c = cs[2 * j]
            dn = (dqh[2 * j] + dqh[2 * j + 1]) * (GDN_DK ** -0.5)
            dks = dkh[2 * j] + dkh[2 * j + 1]
            dq_ref[:, sl] = c["rq"] * (dn - c["qn"] * jnp.sum(dn * c["qn"], axis=-1, keepdims=True))
            dk_ref[:, sl] = c["rk"] * (dks - c["kh"] * jnp.sum(dks * c["kh"], axis=-1, keepdims=True))

    rev = lambda n: nc - 1 - n
    qs, ks, vs, zs, gates, one, ngs, hd = _gdn_specs(rev)
    s1s = pl.BlockSpec((GDN_HB, 1, GDN_DK, GDN_DV), lambda g, n: (g, rev(n), 0, 0))
    s0s = pl.BlockSpec((GDN_HB, 1, GDN_DK, GDN_DV), lambda g, n: (g, jnp.maximum(rev(n) - 1, 0), 0, 0))
    ts = pl.BlockSpec((GDN_HB, 1, CHUNK, CHUNK), lambda g, n: (g, rev(n), 0, 0))
    dgs = pl.BlockSpec((CHUNK, LANES), lambda g, n: (rev(n), 0))
    big = jax.ShapeDtypeStruct((t, GDN_VW), F32)
    keyw = jax.ShapeDtypeStruct((t, GDN_QKW), F32)
    ones_s = jax.ShapeDtypeStruct((nh, 1, 1), F32)
    return pl.pallas_call(
        body, grid=(nh // GDN_HB, nc),
        in_specs=[qs, ks, vs, zs, gates, one, one, ngs, hd, s1s, s0s, ts],
        out_specs=[qs, qs, hd, hd, dgs, one, one, ngs],
        out_shape=[keyw, keyw, big, jax.ShapeDtypeStruct((t, GDN_VW), BF16), jax.ShapeDtypeStruct((t, LANES), BF16),
                   ones_s, ones_s, jax.ShapeDtypeStruct((1, GDN_DV), F32)],
        scratch_shapes=[pltpu.VMEM((GDN_HB, GDN_DK, GDN_DV), F32)],
        compiler_params=_cp("arbitrary", "arbitrary"), name=name)(qkv, qkv, qkv, p, p, alog, dtb, ng, dog, sall, sall, tall)


def _gdn_layer_fwd(h, w, tag):
    p = _mm(h, w["w_in"], name=tag + "_in")
    qkv = _gdn_conv_fwd(p, w["conv_w"], name=tag + "_conv")
    og, sall, tall = _gdn_fwd(qkv, p, w["a_log"], w["dt_bias"], w["norm_g"], name=tag + "_scan")
    y = _mm(og, w["w_out"], name=tag + "_out")
    return y, (p, qkv, og, sall, tall)


def _gdn_layer_bwd(dy, h, saved, w, tag, sink=None):
    p, qkv, og, sall, tall = saved
    t = h.shape[0]
    dog = _mm(dy, w["w_out"], tb=True, name=tag + "_dog")
    dw_out, gf = _dw_out(og, dy, sink, tag + "_dwout")
    dq, dk, dv, dz, dgates, dalog, ddtb, dng = _gdn_bwd(
        qkv, p, w["a_log"], w["dt_bias"], w["norm_g"], dog, sall, tall, name=tag + "_scanb")
    dpre_q, dcw_q = _gdn_conv_bwd(dq, p, w["conv_w"], 0, name=tag + "_convb_q")
    dpre_k, dcw_k = _gdn_conv_bwd(dk, p, w["conv_w"], GDN_QKW, name=tag + "_convb_k")
    dpre_v, dcw_v = _gdn_conv_bwd(dv, p, w["conv_w"], 2 * GDN_QKW, name=tag + "_convb_v")
    dconv_w = jnp.concatenate([dcw_q, dcw_k, dcw_v], axis=1)
    dp = jnp.concatenate([dpre_q, dpre_k, dpre_v, dz, dgates,
                          jnp.zeros((t, GDN_IN_PAD - GDN_CONV_W - GDN_VW - LANES), BF16)], axis=1)
    dw_in = _mm(h, dp, ta=True, out_dtype=BF16, name=tag + "_dwin")
    dh = _mm(dp, w["w_in"], tb=True, out_dtype=BF16, name=tag + "_dh")
    grads = dict(w_in=dw_in[:, :GDN_IN], conv_w=dconv_w, a_log=dalog[:, 0, 0], dt_bias=ddtb[:, 0, 0], norm_g=dng[0], w_out=dw_out)
    return dh, grads, gf


MESH_ID = pl.DeviceIdType.MESH
FLAT_W = 1024
FLAT_ROWS = 13056
FLAT_TILE = 384


def _exchange(name, ins, out_shapes, plan, n_remote, n_local):
    def body(*refs):
        in_refs = refs[:len(ins)]
        out_refs = refs[len(ins):len(ins) + len(out_shapes)]
        ssem, rsem, lsem = refs[len(ins) + len(out_shapes):]
        x, y, c = lax.axis_index("x"), lax.axis_index("y"), lax.axis_index("c")
        stages, local_copies = plan(x, y, c, in_refs, out_refs)
        assert sum(len(s) for s in stages) == n_remote and len(local_copies) == n_local
        locs = [pltpu.make_async_copy(s, d, lsem.at[i]) for i, (s, d) in enumerate(local_copies)]
        for cp in locs:
            cp.start()
        sent = []
        k = 0
        for stage in stages:
            arrivals = []
            for src, dst, peer, landing in stage:
                cp = pltpu.make_async_remote_copy(src_ref=src, dst_ref=dst, send_sem=ssem.at[k], recv_sem=rsem.at[k],
                                                  device_id=peer, device_id_type=MESH_ID)
                cp.start()
                sent.append(cp)
                arrivals.append(pltpu.make_async_remote_copy(src_ref=src, dst_ref=landing, send_sem=ssem.at[k],
                                                             recv_sem=rsem.at[k], device_id=peer, device_id_type=MESH_ID))
                k += 1
            for cp in arrivals:
                cp.wait_recv()
        for cp in sent:
            cp.wait_send()
        for cp in locs:
            cp.wait()

    hbm = pl.BlockSpec(memory_space=pl.ANY)
    return pl.pallas_call(
        body, in_specs=[hbm] * len(ins), out_specs=[hbm] * len(out_shapes), out_shape=out_shapes,
        scratch_shapes=[pltpu.SemaphoreType.DMA((n_remote,)), pltpu.SemaphoreType.DMA((n_remote,)),
                        pltpu.SemaphoreType.DMA((max(n_local, 1),))],
        name=name)(*ins)


def _other_chips(x, y):
    return [(1 - x, y), (x, 1 - y), (1 - x, 1 - y)]


def _all8_gather(a, *, name):
    def plan(x, y, c, ins, outs):
        (src,), (dst,) = ins, outs
        me = 4 * x + 2 * y + c
        stage = []
        for fx, fy, fc in [(0, 0, 1), (0, 1, 0), (0, 1, 1), (1, 0, 0), (1, 0, 1), (1, 1, 0), (1, 1, 1)]:
            px, py, pc = (1 - x if fx else x), (1 - y if fy else y), (1 - c if fc else c)
            stage.append((src, dst.at[me], (px, py, pc), dst.at[4 * px + 2 * py + pc]))
        return [stage], [(src, dst.at[me])]

    return _exchange(name, [a], [jax.ShapeDtypeStruct((8,) + a.shape, a.dtype)], plan, 7, 1)[0]


def _chip_gather(flat, *, name):
    rows = flat.shape[0]
    half = rows // 2

    def plan(x, y, c, ins, outs):
        (src,), (dst,) = ins, outs
        me = 2 * x + y
        mine = pl.ds(c * half, half)
        theirs = pl.ds((1 - c) * half, half)
        ici = [(src.at[mine], dst.at[me, mine], (px, py, c), dst.at[2 * px + py, mine]) for px, py in _other_chips(x, y)]
        d2d = [(dst.at[2 * px + py, mine], dst.at[2 * px + py, mine], (x, y, 1 - c), dst.at[2 * px + py, theirs])
               for px, py in _other_chips(x, y)]
        return [ici, d2d], []

    return _exchange(name, [flat], [jax.ShapeDtypeStruct((4,) + flat.shape, flat.dtype)], plan, 6, 0)[0]


def _add_sibling(gf, buf_a, core, *, name):
    _, rows, w = gf.shape
    half = rows // 2
    nb = half // FLAT_TILE

    def body(c_ref, g_ref, a_ref, o_ref):
        o_ref[...] = (g_ref[...].astype(F32) + a_ref[...].astype(F32)).astype(BF16)

    blk = (1, FLAT_TILE, w)
    return pl.pallas_call(
        body,
        grid_spec=pltpu.PrefetchScalarGridSpec(
            num_scalar_prefetch=1, grid=(4, nb),
            in_specs=[pl.BlockSpec(blk, lambda s, i, c_ref: (s, c_ref[0] * nb + i, 0)), pl.BlockSpec(blk, lambda s, i, c_ref: (s, i, 0))],
            out_specs=pl.BlockSpec(blk, lambda s, i, c_ref: (s, i, 0))),
        out_shape=jax.ShapeDtypeStruct((4, half, w), BF16), compiler_params=_cp("parallel", "parallel"), name=name)(core, gf, buf_a)


def _sum_chips(hsum, buf_b, chip, *, name):
    _, half, w = hsum.shape
    nb = half // FLAT_TILE

    def body(c_ref, h_ref, b0_ref, b1_ref, b2_ref, b3_ref, o_ref):
        me = c_ref[0]
        own = h_ref[0].astype(F32)
        acc = None
        for j, b_ref in enumerate((b0_ref, b1_ref, b2_ref, b3_ref)):
            term = jnp.where(me == j, own, b_ref[0].astype(F32))
            acc = term if acc is None else acc + term
        o_ref[...] = acc

    blk = (1, FLAT_TILE, w)

    def other(j):
        return pl.BlockSpec(blk, lambda i, c_ref: (jnp.where(c_ref[0] == j, (j + 1) % 4, j), i, 0))

    return pl.pallas_call(
        body,
        grid_spec=pltpu.PrefetchScalarGridSpec(
            num_scalar_prefetch=1, grid=(nb,),
            in_specs=[pl.BlockSpec(blk, lambda i, c_ref: (c_ref[0], i, 0))] + [other(j) for j in range(4)],
            out_specs=pl.BlockSpec((FLAT_TILE, w), lambda i, c_ref: (i, 0))),
        out_shape=jax.ShapeDtypeStruct((half, w), F32), compiler_params=_cp("parallel"), name=name)(chip, hsum, buf_b, buf_b, buf_b, buf_b)


def _sum_slots(buf, *, name):
    n, rows, w = buf.shape
    tr = _pick(rows, (FLAT_TILE, 8))

    def body(b_ref, o_ref):
        acc = b_ref[0]
        for s in range(1, n):
            acc = acc + b_ref[s]
        o_ref[...] = acc

    return pl.pallas_call(body, grid=(rows // tr,), in_specs=[pl.BlockSpec((n, tr, w), lambda i: (0, i, 0))],
                          out_specs=pl.BlockSpec((tr, w), lambda i: (i, 0)), out_shape=jax.ShapeDtypeStruct((rows, w), F32),
                          compiler_params=_cp("parallel"), name=name)(buf)


def _reduce_scatter(gf, core, chip, *, tag):
    _, rows, w = gf.shape
    half = rows // 2

    def plan_a(x, y, c, ins, outs):
        (src,), (dst,) = ins, outs
        return [[(src.at[:, pl.ds((1 - c) * half, half)], dst, (x, y, 1 - c), dst)]], []

    buf_a = _exchange(tag + "_sibling", [gf], [jax.ShapeDtypeStruct((4, half, w), gf.dtype)], plan_a, 1, 0)[0]
    hsum = _add_sibling(gf, buf_a, core, name=tag + "_add_sibling")

    def plan_b(x, y, c, ins, outs):
        (src,), (dst,) = ins, outs
        me = 2 * x + y
        stage = [(src.at[2 * px + py], dst.at[me], (px, py, c), dst.at[2 * px + py]) for px, py in _other_chips(x, y)]
        return [stage], []

    buf_b = _exchange(tag + "_chips", [hsum], [jax.ShapeDtypeStruct((4, half, w), BF16)], plan_b, 3, 0)[0]
    mine = _sum_chips(hsum, buf_b, chip, name=tag + "_sum_chips")

    def plan_c(x, y, c, ins, outs):
        (src,), (dst,) = ins, outs
        return [[(src, dst, (x, y, 1 - c), dst)]], []

    theirs = _exchange(tag + "_halves", [mine], [jax.ShapeDtypeStruct((half, w), F32)], plan_c, 1, 0)[0]
    first = core[0] == 0
    return jnp.concatenate([jnp.where(first, mine, theirs), jnp.where(first, theirs, mine)], axis=0)


WEIGHTS = ["ada_w", "ada_b", "norm_pre_g", "norm_post_g", "gla_w_in", "gla_w_gate_up", "gla_b_gate", "gla_head_g",
           "gla_w_out", "mla_w_in", "mla_q_norm_g", "mla_w_uq", "mla_kv_norm_g", "mla_w_ukv", "mla_w_out", "gdn_w_in",
           "gdn_conv_w", "gdn_a_log", "gdn_dt_bias", "gdn_norm_g", "gdn_w_out", "mlp_w_up", "mlp_w_down"]
PACK_BF16 = [("mlp_w_up", 2), ("mlp_w_down", 1), ("gdn_w_out", 1), ("gla_w_out", 1), ("mla_w_out", 1),
             ("gla_w_in", 2), ("mla_w_in", 1), ("mla_w_uq", 2), ("mla_w_ukv", 2), ("gdn_w_in", 2)]
N_DIRECT = 5
PACK_F32 = [("norm_pre_g", 2), ("norm_post_g", 2), ("gla_w_gate_up", 2), ("gla_b_gate", 1), ("gla_head_g", 1), ("gdn_conv_w", 2)]
REPLICATED_SMALL = ["mla_q_norm_g", "mla_kv_norm_g", "gdn_a_log", "gdn_dt_bias", "gdn_norm_g"]
MIXERS = ["gla", "mla", "gdn"]


def _silu_rows(a, *, name):
    def body(a_ref, o_ref):
        v = a_ref[...]
        o_ref[...] = v * _sigmoid(v)

    return pl.pallas_call(body, out_shape=jax.ShapeDtypeStruct(a.shape, F32), name=name)(a)


SMALL_ROWS = 16


def _piece_rows(size, mult):
    assert size % FLAT_W == 0
    return -(-(size // FLAT_W) // mult) * mult


def _to_rows(a, lead, mult):
    n = math.prod(a.shape[len(lead):])
    r = a.reshape(lead + (n // FLAT_W, FLAT_W))
    extra = _piece_rows(n, mult) - n // FLAT_W
    return jnp.pad(r, [(0, 0)] * len(lead) + [(0, extra), (0, 0)]) if extra else r


def _small_to_rows(parts, lead):
    flat = jnp.concatenate([p.reshape(lead + (-1,)) for p in parts], axis=-1)
    pad = SMALL_ROWS * FLAT_W - flat.shape[-1]
    return jnp.pad(flat, [(0, 0)] * len(lead) + [(0, pad)]).reshape(lead + (SMALL_ROWS, FLAT_W))


def _small_from_rows(rows, shards, lead):
    flat = rows.reshape(lead + (-1,))
    out, off = {}, 0
    for n, _ in PACK_F32:
        out[n] = flat[..., off:off + shards[n].size].reshape(lead + shards[n].shape)
        off += shards[n].size
    return out


def _pack_weights(shards):
    parts = [_to_rows(shards[n].astype(BF16), (), 16) for n, _ in PACK_BF16]
    small = _small_to_rows([shards[n] for n, _ in PACK_F32], ())
    parts.append(lax.bitcast_convert_type(small, BF16).reshape(2 * SMALL_ROWS, FLAT_W))
    flat = jnp.concatenate(parts, axis=0)
    return jnp.pad(flat, ((0, FLAT_ROWS - flat.shape[0]), (0, 0)))


def _unpack_weights(gathered, shards):
    full, off = {}, 0
    for n, ax in PACK_BF16:
        size = shards[n].size
        seg = gathered[:, off:off + size // FLAT_W].reshape((4,) + shards[n].shape)
        full[n] = jnp.concatenate([seg[j] for j in range(4)], axis=ax)
        off += _piece_rows(size, 16)
    small = lax.bitcast_convert_type(gathered[:, off:off + 2 * SMALL_ROWS].reshape(4, SMALL_ROWS, FLAT_W, 2), F32)
    for (n, ax), seg in zip(PACK_F32, _small_from_rows(small, shards, (4,)).values()):
        full[n] = jnp.concatenate([seg[j] for j in range(4)], axis=ax)
    return full


def _grad_layout(shards):
    layout, off = {}, 0
    for n, _ in PACK_BF16:
        layout[n] = (off, shards[n].size // shards[n].shape[0] // FLAT_W)
        off += _piece_rows(shards[n].size, 16)
    layout["small"] = (off, SMALL_ROWS)
    return layout


def _pack_grads(gf, grads, layout):
    by_chip = lambda g, ax: jnp.stack(jnp.split(g.astype(gf.dtype), 4, axis=ax - 1))
    parts = []
    for n, ax in PACK_BF16[N_DIRECT:]:
        rows = sum(g.size for g in grads[n]) // (4 * FLAT_W)
        parts += [by_chip(g, ax).reshape(4, -1, FLAT_W) for g in grads[n]]
        if _piece_rows(rows * FLAT_W, 16) > rows:
            parts.append(jnp.zeros((4, _piece_rows(rows * FLAT_W, 16) - rows, FLAT_W), gf.dtype))
    parts.append(_small_to_rows([jnp.stack([by_chip(g, ax) for g in grads[n]], axis=1) for n, ax in PACK_F32], (4,)))
    first = layout[PACK_BF16[N_DIRECT][0]][0]
    rest = jnp.concatenate(parts, axis=1)
    assert first + rest.shape[1] == layout["small"][0] + SMALL_ROWS
    return lax.dynamic_update_slice(gf, rest, (0, first, 0))


def _unpack_grads(reduced, shards):
    out, off = {}, 0
    for n, _ in PACK_BF16:
        size = shards[n].size
        out[n] = reduced[off:off + size // FLAT_W].reshape(shards[n].shape)
        off += _piece_rows(size, 16)
    out.update(_small_from_rows(reduced[off:off + SMALL_ROWS], shards, ()))
    return out


def _mixer_weights(kind, j, full, rep):
    if kind == "gla":
        return dict(w_in=jnp.pad(full["gla_w_in"][j], ((0, 0), (0, GLA_IN_PAD - GLA_IN))),
                    wg=jnp.pad(full["gla_w_gate_up"][j], ((0, LANES - GLA_RANK), (0, 0))),
                    bg=full["gla_b_gate"][j][None], hg=full["gla_head_g"][j][None], w_out=full["gla_w_out"][j])
    if kind == "mla":
        return dict(w_in=jnp.pad(full["mla_w_in"][j], ((0, 0), (0, MLA_IN_PAD - MLA_IN))), q_norm_g=rep["mla_q_norm_g"][j][None],
                    w_uq=full["mla_w_uq"][j], kv_norm_g=rep["mla_kv_norm_g"][j][None], w_ukv=full["mla_w_ukv"][j],
                    w_out=full["mla_w_out"][j])
    return dict(w_in=jnp.pad(full["gdn_w_in"][j], ((0, 0), (0, GDN_IN_PAD - GDN_IN))), conv_w=full["gdn_conv_w"][j],
                a_log=rep["gdn_a_log"][j][:, None, None], dt_bias=rep["gdn_dt_bias"][j][:, None, None],
                norm_g=rep["gdn_norm_g"][j][None], w_out=full["gdn_w_out"][j])


def _layer_fwd(xin, mod, gains, kind, mw, w_up, w_down, pos, tag):
    sh_m, sc_m, gt_m, sh_f, sc_f, gt_f = mod
    pre0, pre1, post0, post1 = gains
    h = _premod_fwd(xin, pre0, sc_m, sh_m, name=tag + "_pre0")
    if kind == "gla":
        y, saved = _gla_layer_fwd(h, mw, tag + "_gla")
    elif kind == "mla":
        y, saved = _mla_layer_fwd(h, pos, mw, tag + "_mla")
    else:
        y, saved = _gdn_layer_fwd(h, mw, tag + "_gdn")
    x1 = _postres_fwd(xin, y, post0, gt_m, name=tag + "_post0")
    h2 = _premod_fwd(x1, pre1, sc_f, sh_f, name=tag + "_pre1")
    act = _mm(h2, w_up, out_dtype=BF16, epi="relu2", name=tag + "_up")
    y2 = _mm(act, w_down, name=tag + "_down")
    x2 = _postres_fwd(x1, y2, post1, gt_f, name=tag + "_post1")
    return x2, (xin, h, y, saved, x1, h2, act, y2)


def _layer_bwd(g2, kept, mod, gains, kind, mw, w_up, w_down, tag, gf, rows):
    xin, h, y, saved, x1, h2, act, y2 = kept
    sh_m, sc_m, gt_m, sh_f, sc_f, gt_f = mod
    pre0, pre1, post0, post1 = gains
    dy2, dpost1, dgt_f = _postres_bwd(g2, y2, post1, gt_f, name=tag + "_post1_b")
    du = _mm(dy2, w_down, tb=True, out_dtype=BF16, epi="dact", aux=act, name=tag + "_du")
    gf = _mm(act, dy2, ta=True, out_dtype=gf.dtype, into=(gf, rows[1], "rows"), name=tag + "_dwdown")
    gf = _mm(h2, du, ta=True, out_dtype=gf.dtype, into=(gf, rows[0], "cols"), name=tag + "_dwup")
    dh2 = _mm(du, w_up, tb=True, out_dtype=BF16, name=tag + "_dh2")
    g1, dpre1, dsc_f, dsh_f = _premod_bwd(dh2, x1, pre1, sc_f, g2, name=tag + "_pre1_b")
    dy, dpost0, dgt_m = _postres_bwd(g1, y, post0, gt_m, name=tag + "_post0_b")
    mixer_bwd = dict(gla=_gla_layer_bwd, mla=_mla_layer_bwd, gdn=_gdn_layer_bwd)[kind]
    dh, mg, gf = mixer_bwd(dy, h, saved, mw, tag + "_" + kind, sink=(gf, rows[2]))
    g0, dpre0, dsc_m, dsh_m = _premod_bwd(dh, xin, pre0, sc_m, g1, name=tag + "_pre0_b")
    dmod = jnp.concatenate([dsh_m, dsc_m, dgt_m, dsh_f, dsc_f, dgt_f], axis=1)
    return g0, dmod, jnp.concatenate([dpre0, dpre1], axis=0), jnp.concatenate([dpost0, dpost1], axis=0), mg, gf


def kernel(x, c, positions, ada_w, ada_b, norm_pre_g, norm_post_g, gla_w_in, gla_w_gate_up, gla_b_gate, gla_head_g, gla_w_out, mla_w_in, mla_q_norm_g, mla_w_uq, mla_kv_norm_g, mla_w_ukv, mla_w_out, gdn_w_in, gdn_conv_w, gdn_a_log, gdn_dt_bias, gdn_norm_g, gdn_w_out, mlp_w_up, mlp_w_down, loss_target, m_ada_w, m_ada_b, m_norm_pre_g, m_norm_post_g, m_gla_w_in, m_gla_w_gate_up, m_gla_b_gate, m_gla_head_g, m_gla_w_out, m_mla_w_in, m_mla_q_norm_g, m_mla_w_uq, m_mla_kv_norm_g, m_mla_w_ukv, m_mla_w_out, m_gdn_w_in, m_gdn_conv_w, m_gdn_a_log, m_gdn_dt_bias, m_gdn_norm_g, m_gdn_w_out, m_mlp_w_up, m_mlp_w_down, v_ada_w, v_ada_b, v_norm_pre_g, v_norm_post_g, v_gla_w_in, v_gla_w_gate_up, v_gla_b_gate, v_gla_head_g, v_gla_w_out, v_mla_w_in, v_mla_q_norm_g, v_mla_w_uq, v_mla_kv_norm_g, v_mla_w_ukv, v_mla_w_out, v_gdn_w_in, v_gdn_conv_w, v_gdn_a_log, v_gdn_dt_bias, v_gdn_norm_g, v_gdn_w_out, v_mlp_w_up, v_mlp_w_down):
    w = dict(ada_w=ada_w, ada_b=ada_b, norm_pre_g=norm_pre_g, norm_post_g=norm_post_g, gla_w_in=gla_w_in,
             gla_w_gate_up=gla_w_gate_up, gla_b_gate=gla_b_gate, gla_head_g=gla_head_g, gla_w_out=gla_w_out, mla_w_in=mla_w_in,
             mla_q_norm_g=mla_q_norm_g, mla_w_uq=mla_w_uq, mla_kv_norm_g=mla_kv_norm_g, mla_w_ukv=mla_w_ukv, mla_w_out=mla_w_out,
             gdn_w_in=gdn_w_in, gdn_conv_w=gdn_conv_w, gdn_a_log=gdn_a_log, gdn_dt_bias=gdn_dt_bias, gdn_norm_g=gdn_norm_g,
             gdn_w_out=gdn_w_out, mlp_w_up=mlp_w_up, mlp_w_down=mlp_w_down)
    m = dict(zip(WEIGHTS, [m_ada_w, m_ada_b, m_norm_pre_g, m_norm_post_g, m_gla_w_in, m_gla_w_gate_up, m_gla_b_gate, m_gla_head_g,
                           m_gla_w_out, m_mla_w_in, m_mla_q_norm_g, m_mla_w_uq, m_mla_kv_norm_g, m_mla_w_ukv, m_mla_w_out,
                           m_gdn_w_in, m_gdn_conv_w, m_gdn_a_log, m_gdn_dt_bias, m_gdn_norm_g, m_gdn_w_out, m_mlp_w_up, m_mlp_w_down]))
    v = dict(zip(WEIGHTS, [v_ada_w, v_ada_b, v_norm_pre_g, v_norm_post_g, v_gla_w_in, v_gla_w_gate_up, v_gla_b_gate, v_gla_head_g,
                           v_gla_w_out, v_mla_w_in, v_mla_q_norm_g, v_mla_w_uq, v_mla_kv_norm_g, v_mla_w_ukv, v_mla_w_out,
                           v_gdn_w_in, v_gdn_conv_w, v_gdn_a_log, v_gdn_dt_bias, v_gdn_norm_g, v_gdn_w_out, v_mlp_w_up, v_mlp_w_down]))
    t = x.shape[1]
    ix, iy, ic = lax.axis_index("x"), lax.axis_index("y"), lax.axis_index("c")
    me = 4 * ix + 2 * iy + ic
    chip = 2 * ix + iy
    ada_cols = ada_w.shape[2]

    packed = _pack_weights(w)
    zero = jnp.zeros((), jnp.int32)
    gathered = lax.dynamic_update_slice(_chip_gather(packed, name="gather_weights"), packed[None], (chip, zero, zero))
    full = _unpack_weights(gathered, w)

    cond8 = _silu_rows(jnp.pad(c, ((0, 7), (0, 0))), name="cond_silu")
    cond16 = jnp.pad(_all8_gather(cond8, name="gather_cond")[:, 0, :], ((0, 8), (0, 0)))
    mod_cols = []
    for layer in range(DEPTH):
        bias = jnp.broadcast_to(lax.dynamic_slice_in_dim(ada_b[layer], chip * ada_cols, ada_cols)[None], (16, ada_cols))
        mod_cols.append(_mm(cond16, ada_w[layer], epi="add", aux=bias, name=f"ada{layer}")[:8])
    mod_all = _all8_gather(jnp.stack(mod_cols).reshape(DEPTH * 8, ada_cols), name="gather_mod")
    mod = jnp.concatenate([lax.dynamic_slice_in_dim(mod_all[2 * j].reshape(DEPTH, 8, ada_cols), me, 1, axis=1)[:, 0]
                           for j in range(4)], axis=1)

    def layer_args(layer):
        kind, j = MIXERS[layer % 3], layer // 3
        mods = [mod[layer, i * D_MODEL:(i + 1) * D_MODEL][None] for i in range(N_MOD)]
        gains = (full["norm_pre_g"][layer, 0:1], full["norm_pre_g"][layer, 1:2], full["norm_post_g"][layer, 0:1],
                 full["norm_post_g"][layer, 1:2])
        return kind, j, mods, gains, _mixer_weights(kind, j, full, w)

    xs = x[0]
    kept = []
    for layer in range(DEPTH):
        kind, j, mods, gains, mw = layer_args(layer)
        xs, keep = _layer_fwd(xs, mods, gains, kind, mw, full["mlp_w_up"][layer], full["mlp_w_down"][layer], positions[0], f"l{layer}")
        kept.append(keep)
    loss_row, g = _loss_head(xs, loss_target[0], name="loss_head")
    loss = lax.psum(loss_row[0, 0], ("x", "y", "c"))

    grads = {n: [None] * w[n].shape[0] for n, _ in PACK_BF16[N_DIRECT:] + PACK_F32}
    rep_grads = {}
    dmods = [None] * DEPTH
    layout = _grad_layout(w)
    row_of = lambda n, idx: layout[n][0] + idx * layout[n][1]
    gf = lax.empty((4, FLAT_ROWS, FLAT_W), BF16)
    for layer in reversed(range(DEPTH)):
        kind, j, mods, gains, mw = layer_args(layer)
        rows = (row_of("mlp_w_up", layer), row_of("mlp_w_down", layer), row_of(kind + "_w_out", j))
        g, dmods[layer], dpre, dpost, mg, gf = _layer_bwd(
            g, kept[layer], mods, gains, kind, mw, full["mlp_w_up"][layer], full["mlp_w_down"][layer], f"l{layer}", gf, rows)
        grads["norm_pre_g"][layer], grads["norm_post_g"][layer] = dpre, dpost
        for key, val in mg.items():
            name = kind + "_" + key
            if name in grads:
                grads[name][j] = val
            elif name in REPLICATED_SMALL:
                rep_grads[name] = val[None]

    rep_flat = jnp.concatenate([rep_grads[n].reshape(-1) for n in REPLICATED_SMALL])
    dbuf = jnp.concatenate([jnp.concatenate(dmods, axis=0), jnp.pad(rep_flat, (0, N_MOD * D_MODEL - rep_flat.shape[0]))[None],
                            jnp.zeros((3, N_MOD * D_MODEL), F32)], axis=0)
    dall = _all8_gather(dbuf, name="gather_dmod")
    dsum = _sum_slots(dall, name="sum_dmod")
    out_grads = {"ada_b": dsum[:DEPTH]}
    off = 0
    for n in REPLICATED_SMALL:
        out_grads[n] = dsum[DEPTH, off:off + w[n].size].reshape(w[n].shape)
        off += w[n].size
    dada = []
    for layer in range(DEPTH):
        dm16 = jnp.pad(lax.dynamic_slice_in_dim(dall[:, layer, :], chip * ada_cols, ada_cols, axis=1), ((0, 8), (0, 0)))
        dada.append(_mm(cond16, dm16, ta=True, name=f"dada{layer}"))
    out_grads["ada_w"] = jnp.stack(dada)

    reduced = _reduce_scatter(_pack_grads(gf, grads, layout), ic.reshape(1).astype(jnp.int32),
                              chip.reshape(1).astype(jnp.int32), tag="reduce_grads")
    out_grads.update(_unpack_grads(reduced, w))

    deltas, new_m, new_v = {}, {}, {}
    for n in WEIGHTS:
        deltas[n], new_m[n], new_v[n] = _adamw(w[n], out_grads[n], m[n], v[n], name="adamw_" + n)
    return (loss, g[None], *[out_grads[n] for n in WEIGHTS], *[deltas[n] for n in WEIGHTS],
            *[new_m[n] for n in WEIGHTS], *[new_v[n] for n in WEIGHTS])
```

```python
import functools
import math

import jax
import jax.numpy as jnp
from jax import lax
from jax.experimental import pallas as pl
from jax.experimental.pallas import tpu as pltpu

F32 = jnp.float32
BF16 = jnp.bfloat16

D_MODEL = 1024
DEPTH = 4
CHUNK = 64
EPS = 1e-6
NEG_INF = -1e30
N_MOD = 6

GLA_HEADS, GLA_DK, GLA_DV, GLA_RANK = 4, 128, 256, 16
GLA_KW, GLA_VW = GLA_HEADS * GLA_DK, GLA_HEADS * GLA_DV
GLA_IN = 2 * GLA_KW + 2 * GLA_VW + GLA_RANK
GLA_IN_PAD = 3200

MLA_HEADS, MLA_NOPE, MLA_ROPE, MLA_V = 16, 64, 32, 64
MLA_Q_RANK, MLA_KV_RANK = 384, 256
MLA_IN = MLA_Q_RANK + MLA_KV_RANK + MLA_ROPE
MLA_IN_PAD = 768
ROPE_THETA = 10000.0
MLA_QK = MLA_NOPE + MLA_ROPE
LANES = 128

GDN_K_HEADS, GDN_V_HEADS, GDN_DK, GDN_DV, GDN_CONV = 8, 16, 128, 128, 4
GDN_QKW, GDN_VW = GDN_K_HEADS * GDN_DK, GDN_V_HEADS * GDN_DV
GDN_CONV_W = 2 * GDN_QKW + GDN_VW
GDN_IN = GDN_CONV_W + GDN_VW + 2 * GDN_V_HEADS
GDN_IN_PAD = 6400

ADAM_LR, ADAM_B1, ADAM_B2, ADAM_EPS, ADAM_WD, ADAM_STEP = 0.001, 0.9, 0.999, 1e-08, 0.01, 10

VMEM_LIMIT = 56 * 1024 * 1024

NN = ((1,), (0,))
NT = ((1,), (1,))
TN = ((0,), (0,))


def _cp(*sem):
    return pltpu.CompilerParams(dimension_semantics=sem, vmem_limit_bytes=VMEM_LIMIT)


def _pick(n, cands):
    for c in cands:
        if n % c == 0:
            return c
    return n


def _dg(a, b, dims=NN):
    return lax.dot_general(a.astype(BF16), b.astype(BF16), (dims, ((), ())), preferred_element_type=F32)


def _dot3(a, b, dims=NN):
    ah = a.astype(BF16)
    al = (a - ah.astype(F32)).astype(BF16)
    bh = b.astype(BF16)
    bl = (b - bh.astype(F32)).astype(BF16)
    d = lambda u, v: lax.dot_general(u, v, (dims, ((), ())), preferred_element_type=F32)
    return d(ah, bh) + (d(ah, bl) + d(al, bh))


def _sigmoid(x):
    return 1.0 / (1.0 + jnp.exp(-x))


def _softplus(x):
    return jnp.maximum(x, 0.0) + jnp.log(1.0 + jnp.exp(-jnp.abs(x)))


def _iota2(shape, dim):
    return lax.broadcasted_iota(jnp.int32, shape, dim)


def _mm(a, b, *, ta=False, tb=False, out_dtype=F32, epi=None, aux=None, into=None, name):
    m = a.shape[1] if ta else a.shape[0]
    k = a.shape[0] if ta else a.shape[1]
    n = b.shape[0] if tb else b.shape[1]
    assert k == (b.shape[1] if tb else b.shape[0]), (a.shape, b.shape, ta, tb)
    m_tile = m // 4 if into is not None and into[2] == "rows" else m
    tm = _pick(m_tile, (2048, 1024, 512, 384, 256, 128))
    tn = _pick(n, (1024, 640, 512, 768, 384, 256, 128))
    tk = _pick(k, (1024, 640, 512, 768, 384, 256, 128))
    nk = k // tk
    dims = ((0 if ta else 1,), (1 if tb else 0,))

    def finish(r, x_ref, o_ref):
        if epi == "relu2":
            r = jnp.square(jnp.maximum(r, 0.0))
        elif epi == "dact":
            r = r * (2.0 * jnp.sqrt(x_ref[...].astype(F32)))
        elif epi == "add":
            r = r + x_ref[...]
        o_ref[...] = r.astype(out_dtype)

    n_in = 2 + (aux is not None) + (into is not None)

    def body(*refs):
        a_ref, b_ref = refs[:2]
        x_ref = refs[2] if aux is not None else None
        o_ref = refs[n_in]
        if nk == 1:
            finish(_dg(a_ref[...], b_ref[...], dims), x_ref, o_ref)
            return
        acc = refs[-1]
        kk = pl.program_id(2)

        @pl.when(kk == 0)
        def _():
            acc[...] = jnp.zeros_like(acc)

        acc[...] += _dg(a_ref[...], b_ref[...], dims)

        @pl.when(kk == nk - 1)
        def _():
            finish(acc[...], x_ref, o_ref)

    a_spec = pl.BlockSpec((tk, tm), lambda i, j, q: (q, i)) if ta else pl.BlockSpec((tm, tk), lambda i, j, q: (i, q))
    b_spec = pl.BlockSpec((tn, tk), lambda i, j, q: (j, q)) if tb else pl.BlockSpec((tk, tn), lambda i, j, q: (q, j))
    o_spec = pl.BlockSpec((tm, tn), lambda i, j, q: (i, j))
    in_specs = [a_spec, b_spec] + ([o_spec] if aux is not None else [])
    args = (a, b) + ((aux,) if aux is not None else ())
    out_shape = jax.ShapeDtypeStruct((m, n), out_dtype)
    aliases = {}
    if into is not None:
        dst, row0, axis = into
        assert dst.dtype == out_dtype and row0 % tm == 0 and tn == FLAT_W and n == (FLAT_W if axis == "rows" else 4 * FLAT_W)
        per = m_tile // tm
        if axis == "rows":
            o_spec = pl.BlockSpec((None, tm, tn), lambda i, j, q: (i // per, row0 // tm + i % per, 0))
        else:
            o_spec = pl.BlockSpec((None, tm, tn), lambda i, j, q: (j, row0 // tm + i, 0))
        in_specs.append(pl.BlockSpec(memory_space=pl.ANY))
        args += (dst,)
        out_shape = jax.ShapeDtypeStruct(dst.shape, dst.dtype)
        aliases = {n_in - 1: 0}
    return pl.pallas_call(
        body, grid=(m // tm, n // tn, nk), in_specs=in_specs, out_specs=o_spec, out_shape=out_shape,
        scratch_shapes=[pltpu.VMEM((tm, tn), F32)] if nk > 1 else [], input_output_aliases=aliases,
        compiler_params=_cp("parallel", "parallel", "arbitrary"), name=name)(*args)


def _row_tile(t):
    return _pick(t, (1024, 512, 256, 128, 64, 8))


def _premod_fwd(x, g, sc, sh, *, name):
    t, c = x.shape
    tr = _row_tile(t)

    def body(x_ref, g_ref, sc_ref, sh_ref, h_ref):
        xv = x_ref[...]
        r = lax.rsqrt(jnp.mean(xv * xv, axis=-1, keepdims=True) + EPS)
        h_ref[...] = (((xv * r) * g_ref[...]) * (1.0 + sc_ref[...]) + sh_ref[...]).astype(BF16)

    row = pl.BlockSpec((tr, c), lambda i: (i, 0))
    vec = pl.BlockSpec((1, c), lambda i: (0, 0))
    return pl.pallas_call(body, grid=(t // tr,), in_specs=[row, vec, vec, vec], out_specs=row,
                          out_shape=jax.ShapeDtypeStruct((t, c), BF16), compiler_params=_cp("parallel"), name=name)(x, g, sc, sh)


def _premod_bwd(dh, x, g, sc, gin, *, name):
    t, c = x.shape
    tr = _row_tile(t)

    def body(dh_ref, x_ref, g_ref, sc_ref, gin_ref, gout_ref, dg_ref, dsc_ref, dsh_ref):
        @pl.when(pl.program_id(0) == 0)
        def _():
            dg_ref[...] = jnp.zeros_like(dg_ref)
            dsc_ref[...] = jnp.zeros_like(dsc_ref)
            dsh_ref[...] = jnp.zeros_like(dsh_ref)

        xv = x_ref[...]
        dhv = dh_ref[...].astype(F32)
        gv = g_ref[...]
        one_sc = 1.0 + sc_ref[...]
        r = lax.rsqrt(jnp.mean(xv * xv, axis=-1, keepdims=True) + EPS)
        nv = xv * r
        dsh_ref[...] += jnp.sum(dhv, axis=0, keepdims=True)
        dsc_ref[...] += jnp.sum(dhv * (nv * gv), axis=0, keepdims=True)
        dg_ref[...] += jnp.sum(dhv * nv * one_sc, axis=0, keepdims=True)
        dn = dhv * gv * one_sc
        dx = r * (dn - nv * jnp.mean(dn * nv, axis=-1, keepdims=True))
        gout_ref[...] = gin_ref[...] + dx

    row = pl.BlockSpec((tr, c), lambda i: (i, 0))
    vec = pl.BlockSpec((1, c), lambda i: (0, 0))
    vs = jax.ShapeDtypeStruct((1, c), F32)
    return pl.pallas_call(body, grid=(t // tr,), in_specs=[row, row, vec, vec, row], out_specs=[row, vec, vec, vec],
                          out_shape=[jax.ShapeDtypeStruct((t, c), F32), vs, vs, vs],
                          compiler_params=_cp("arbitrary"), name=name)(dh, x, g, sc, gin)


def _postres_fwd(x, y, g, gt, *, name):
    t, c = x.shape
    tr = _row_tile(t)

    def body(x_ref, y_ref, g_ref, gt_ref, o_ref):
        yv = y_ref[...]
        r = lax.rsqrt(jnp.mean(yv * yv, axis=-1, keepdims=True) + EPS)
        o_ref[...] = x_ref[...] + gt_ref[...] * ((yv * r) * g_ref[...])

    row = pl.BlockSpec((tr, c), lambda i: (i, 0))
    vec = pl.BlockSpec((1, c), lambda i: (0, 0))
    return pl.pallas_call(body, grid=(t // tr,), in_specs=[row, row, vec, vec], out_specs=row,
                          out_shape=jax.ShapeDtypeStruct((t, c), F32), compiler_params=_cp("parallel"), name=name)(x, y, g, gt)


def _postres_bwd(gout, y, g, gt, *, name):
    t, c = y.shape
    tr = _row_tile(t)

    def body(go_ref, y_ref, g_ref, gt_ref, dy_ref, dg_ref, dgt_ref):
        @pl.when(pl.program_id(0) == 0)
        def _():
            dg_ref[...] = jnp.zeros_like(dg_ref)
            dgt_ref[...] = jnp.zeros_like(dgt_ref)

        yv = y_ref[...]
        gov = go_ref[...]
        gv = g_ref[...]
        gtv = gt_ref[...]
        r = lax.rsqrt(jnp.mean(yv * yv, axis=-1, keepdims=True) + EPS)
        z = yv * r
        dgt_ref[...] += jnp.sum(gov * (z * gv), axis=0, keepdims=True)
        dg_ref[...] += jnp.sum(gov * gtv * z, axis=0, keepdims=True)
        dz = gov * gtv * gv
        dy_ref[...] = (r * (dz - z * jnp.mean(dz * z, axis=-1, keepdims=True))).astype(BF16)

    row = pl.BlockSpec((tr, c), lambda i: (i, 0))
    vec = pl.BlockSpec((1, c), lambda i: (0, 0))
    vs = jax.ShapeDtypeStruct((1, c), F32)
    return pl.pallas_call(body, grid=(t // tr,), in_specs=[row, row, vec, vec], out_specs=[row, vec, vec],
                          out_shape=[jax.ShapeDtypeStruct((t, c), BF16), vs, vs],
                          compiler_params=_cp("arbitrary"), name=name)(gout, y, g, gt)


def _loss_head(y, tgt, *, name):
    t, c = y.shape
    tr = _row_tile(t)

    def body(y_ref, t_ref, l_ref, dy_ref):
        @pl.when(pl.program_id(0) == 0)
        def _():
            l_ref[...] = jnp.zeros_like(l_ref)

        d = y_ref[...] - t_ref[...]
        dy_ref[...] = d * (1.0 / c)
        l_ref[...] += 0.5 * jnp.sum(jnp.mean(d * d, axis=-1, keepdims=True))

    row = pl.BlockSpec((tr, c), lambda i: (i, 0))
    return pl.pallas_call(body, grid=(t // tr,), in_specs=[row, row],
                          out_specs=[pl.BlockSpec((1, LANES), lambda i: (0, 0)), row],
                          out_shape=[jax.ShapeDtypeStruct((1, LANES), F32), jax.ShapeDtypeStruct((t, c), F32)],
                          compiler_params=_cp("arbitrary"), name=name)(y, tgt)


def _adamw(w, g, m, v, *, name):
    shape = w.shape
    c = shape[-1]
    r = math.prod(shape[:-1])
    w2, g2, m2, v2 = (a.reshape(r, c) for a in (w, g, m, v))
    tr = r
    for cand in (1024, 512, 256, 128, 64, 32, 16, 8):
        if r % cand == 0 and cand * c * 4 <= (1 << 20):
            tr = cand
            break
    c1 = 1.0 - ADAM_B1 ** ADAM_STEP
    c2 = 1.0 - ADAM_B2 ** ADAM_STEP

    def body(w_ref, g_ref, m_ref, v_ref, d_ref, nm_ref, nv_ref):
        gv = g_ref[...]
        mn = ADAM_B1 * m_ref[...] + (1.0 - ADAM_B1) * gv
        vn = ADAM_B2 * v_ref[...] + (1.0 - ADAM_B2) * jnp.square(gv)
        m_hat = mn / c1
        v_hat = vn / c2
        d_ref[...] = -ADAM_LR * (m_hat / (jnp.sqrt(v_hat) + ADAM_EPS) + ADAM_WD * w_ref[...])
        nm_ref[...] = mn
        nv_ref[...] = vn

    blk = pl.BlockSpec((tr, c), lambda i: (i, 0))
    s = jax.ShapeDtypeStruct((r, c), F32)
    d, nm, nv = pl.pallas_call(body, grid=(r // tr,), in_specs=[blk] * 4, out_specs=[blk] * 3, out_shape=[s, s, s],
                               compiler_params=_cp("parallel"), name=name)(w2, g2, m2, v2)
    return d.reshape(shape), nm.reshape(shape), nv.reshape(shape)


def _gla_parts(p_ref, wg_ref, bg_ref):
    q = p_ref[:, 0:GLA_KW] * (GLA_DK ** -0.5)
    k = p_ref[:, GLA_KW:2 * GLA_KW]
    glr = p_ref[:, 2 * GLA_KW + 2 * GLA_VW:GLA_IN_PAD]
    gate = _dg(glr, wg_ref[...]) + bg_ref[...]
    log_a = (jnp.minimum(gate, 0.0) - jnp.log(1.0 + jnp.exp(-jnp.abs(gate)))) * (1.0 / 16.0)
    tril = (_iota2((CHUNK, CHUNK), 0) >= _iota2((CHUNK, CHUNK), 1)).astype(F32)
    cum = _dot_sel(tril, log_a)
    c_last = cum[CHUNK - 1:CHUNK, :]
    f = jnp.exp(c_last - cum)
    dec = jnp.exp(c_last)
    return q, k, glr, gate, f, k * f, dec


def _gla_fwd(p, wg, bg, hg, *, name):
    t = p.shape[0]
    nc = t // CHUNK

    def body(p_ref, wg_ref, bg_ref, hg_ref, og_ref, s_ref, st):
        @pl.when(pl.program_id(0) == 0)
        def _():
            st[...] = jnp.zeros_like(st)

        q, _, _, _, _, ke, dec = _gla_parts(p_ref, wg_ref, bg_ref)
        hs = range(GLA_HEADS)
        ks = [slice(h * GLA_DK, (h + 1) * GLA_DK) for h in hs]
        vs = [slice(2 * GLA_KW + h * GLA_DV, 2 * GLA_KW + (h + 1) * GLA_DV) for h in hs]
        rs = [slice(2 * GLA_KW + GLA_VW + h * GLA_DV, 2 * GLA_KW + GLA_VW + (h + 1) * GLA_DV) for h in hs]
        s_new = [st[h] * dec[:, ks[h]] + _dg(p_ref[:, vs[h]], ke[:, ks[h]], TN) for h in hs]
        o = [_dg(q[:, ks[h]], s_new[h], NT) for h in hs]
        for h in hs:
            st[h] = s_new[h]
            s_ref[0, h] = s_new[h]
            rn = lax.rsqrt(jnp.mean(o[h] * o[h], axis=-1, keepdims=True) + EPS)
            rv = p_ref[:, rs[h]]
            og_ref[:, h * GLA_DV:(h + 1) * GLA_DV] = (((o[h] * rn) * hg_ref[...]) * (rv * _sigmoid(rv))).astype(BF16)

    full = lambda a: pl.BlockSpec(a.shape, lambda n: (0,) * a.ndim)
    return pl.pallas_call(
        body, grid=(nc,),
        in_specs=[pl.BlockSpec((CHUNK, GLA_IN_PAD), lambda n: (n, 0)), full(wg), full(bg), full(hg)],
        out_specs=[pl.BlockSpec((CHUNK, GLA_VW), lambda n: (n, 0)),
                   pl.BlockSpec((1, GLA_HEADS, GLA_DV, GLA_DK), lambda n: (n, 0, 0, 0))],
        out_shape=[jax.ShapeDtypeStruct((t, GLA_VW), BF16), jax.ShapeDtypeStruct((nc, GLA_HEADS, GLA_DV, GLA_DK), F32)],
        scratch_shapes=[pltpu.VMEM((GLA_HEADS, GLA_DV, GLA_DK), F32)],
        compiler_params=_cp("arbitrary"), name=name)(p, wg, bg, hg)


def _gla_bwd(p, dog, sall, wg, bg, hg, *, name):
    t = p.shape[0]
    nc = t // CHUNK

    def body(p_ref, dog_ref, s1_ref, s0_ref, wg_ref, bg_ref, hg_ref, dp_ref, dwg_ref, dbg_ref, dhg_ref, gt):
        i = pl.program_id(0)

        @pl.when(i == 0)
        def _():
            gt[...] = jnp.zeros_like(gt)
            dwg_ref[...] = jnp.zeros_like(dwg_ref)
            dbg_ref[...] = jnp.zeros_like(dbg_ref)
            dhg_ref[...] = jnp.zeros_like(dhg_ref)

        has_prev = (i < nc - 1).astype(F32)
        q, k, glr, gate, f, ke, dec = _gla_parts(p_ref, wg_ref, bg_ref)
        hgv = hg_ref[...]
        hs = range(GLA_HEADS)
        ks = [slice(h * GLA_DK, (h + 1) * GLA_DK) for h in hs]
        vs = [slice(2 * GLA_KW + h * GLA_DV, 2 * GLA_KW + (h + 1) * GLA_DV) for h in hs]
        rs = [slice(2 * GLA_KW + GLA_VW + h * GLA_DV, 2 * GLA_KW + GLA_VW + (h + 1) * GLA_DV) for h in hs]
        s1 = [s1_ref[0, h] for h in hs]
        o = [_dg(q[:, ks[h]], s1[h], NT) for h in hs]
        dhg = jnp.zeros((1, GLA_DV), F32)
        do = []
        for h in hs:
            rv = p_ref[:, rs[h]]
            rn = lax.rsqrt(jnp.mean(o[h] * o[h], axis=-1, keepdims=True) + EPS)
            z = o[h] * rn
            sg = _sigmoid(rv)
            sl = rv * sg
            dogh = dog_ref[:, h * GLA_DV:(h + 1) * GLA_DV].astype(F32)
            dhg = dhg + jnp.sum(dogh * z * sl, axis=0, keepdims=True)
            dp_ref[:, rs[h]] = (dogh * (z * hgv) * (sg * (1.0 + rv * (1.0 - sg)))).astype(BF16)
            dz = dogh * sl * hgv
            do.append(rn * (dz - z * jnp.mean(dz * z, axis=-1, keepdims=True)))
        dhg_ref[...] += dhg
        g_tot = [gt[h] + _dg(do[h], q[:, ks[h]], TN) for h in hs]
        dq = [_dg(do[h], s1[h], NN) for h in hs]
        dke_parts = [_dg(p_ref[:, vs[h]], g_tot[h], NN) for h in hs]
        dv = [_dg(ke[:, ks[h]], g_tot[h], NT) for h in hs]
        ddec_parts = []
        for h in hs:
            dp_ref[:, ks[h]] = (dq[h] * (GLA_DK ** -0.5)).astype(BF16)
            dp_ref[:, vs[h]] = dv[h].astype(BF16)
            ddec_parts.append(jnp.sum(g_tot[h] * (s0_ref[0, h] * has_prev), axis=0, keepdims=True))
            gt[h] = g_tot[h] * dec[:, ks[h]]
        dke = jnp.concatenate(dke_parts, axis=1)
        ddec = jnp.concatenate(ddec_parts, axis=1)
        dp_ref[:, GLA_KW:2 * GLA_KW] = (dke * f).astype(BF16)
        stril = (_iota2((CHUNK, CHUNK), 0) > _iota2((CHUNK, CHUNK), 1)).astype(F32)
        dlog_a = _dot_sel(stril, dke * ke) + ddec * dec
        dgate = dlog_a * (1.0 / 16.0) * _sigmoid(-gate)
        dp_ref[:, 2 * GLA_KW + 2 * GLA_VW:GLA_IN_PAD] = _dg(dgate, wg_ref[...], NT).astype(BF16)
        dwg_ref[...] += _dg(glr, dgate, TN)
        dbg_ref[...] += jnp.sum(dgate, axis=0, keepdims=True)

    full = lambda a: pl.BlockSpec(a.shape, lambda n: (0,) * a.ndim)
    sblk = (1, GLA_HEADS, GLA_DV, GLA_DK)
    return pl.pallas_call(
        body, grid=(nc,),
        in_specs=[pl.BlockSpec((CHUNK, GLA_IN_PAD), lambda n: (nc - 1 - n, 0)),
                  pl.BlockSpec((CHUNK, GLA_VW), lambda n: (nc - 1 - n, 0)),
                  pl.BlockSpec(sblk, lambda n: (nc - 1 - n, 0, 0, 0)),
                  pl.BlockSpec(sblk, lambda n: (jnp.maximum(nc - 2 - n, 0), 0, 0, 0)),
                  full(wg), full(bg), full(hg)],
        out_specs=[pl.BlockSpec((CHUNK, GLA_IN_PAD), lambda n: (nc - 1 - n, 0)), full(wg), full(bg), full(hg)],
        out_shape=[jax.ShapeDtypeStruct((t, GLA_IN_PAD), BF16), jax.ShapeDtypeStruct(wg.shape, F32),
                   jax.ShapeDtypeStruct(bg.shape, F32), jax.ShapeDtypeStruct(hg.shape, F32)],
        scratch_shapes=[pltpu.VMEM((GLA_HEADS, GLA_DV, GLA_DK), F32)],
        compiler_params=_cp("arbitrary"), name=name)(p, dog, sall, sall, wg, bg, hg)


def _gla_layer_fwd(h, w, tag):
    p = _mm(h, w["w_in"], name=tag + "_in")
    og, sall = _gla_fwd(p, w["wg"], w["bg"], w["hg"], name=tag + "_scan")
    y = _mm(og, w["w_out"], name=tag + "_out")
    return y, (p, og, sall)


def _dw_out(act, dy, sink, name):
    if sink is None:
        return _mm(act, dy, ta=True, name=name), None
    return None, _mm(act, dy, ta=True, out_dtype=sink[0].dtype, into=(sink[0], sink[1], "rows"), name=name)


def _gla_layer_bwd(dy, h, saved, w, tag, sink=None):
    p, og, sall = saved
    dog = _mm(dy, w["w_out"], tb=True, name=tag + "_dog")
    dw_out, gf = _dw_out(og, dy, sink, tag + "_dwout")
    dp, dwg, dbg, dhg = _gla_bwd(p, dog, sall, w["wg"], w["bg"], w["hg"], name=tag + "_scanb")
    dw_in = _mm(h, dp, ta=True, out_dtype=BF16, name=tag + "_dwin")
    dh = _mm(dp, w["w_in"], tb=True, out_dtype=BF16, name=tag + "_dh")
    grads = dict(w_in=dw_in[:, :GLA_IN], w_gate_up=dwg[:GLA_RANK], b_gate=dbg[0], head_g=dhg[0], w_out=dw_out)
    return dh, grads, gf


def _rope_tables(pos, inv_freq, *, name):
    t = pos.shape[0]
    tr = _row_tile(t)
    half = MLA_ROPE // 2

    def body(p_ref, f_ref, c_ref, s1_ref, s2_ref, s1b_ref, s2b_ref):
        ang = p_ref[...].astype(F32) * f_ref[...]
        lane = _iota2((tr, LANES), 1)
        lo = (lane >= MLA_NOPE) & (lane < MLA_NOPE + half)
        hi = (lane >= MLA_NOPE + half) & (lane < MLA_QK)
        cs, sn = jnp.cos(ang), jnp.sin(ang)
        zero = jnp.zeros_like(cs)
        c_ref[...] = jnp.where(lane < MLA_NOPE, 1.0, jnp.where(lane < MLA_QK, cs, 0.0))
        s1_ref[...] = jnp.where(lo, -sn, zero)
        s2_ref[...] = jnp.where(hi, sn, zero)
        s1b_ref[...] = jnp.where(lo, sn, zero)
        s2b_ref[...] = jnp.where(hi, -sn, zero)

    row = pl.BlockSpec((tr, LANES), lambda i: (i, 0))
    s = jax.ShapeDtypeStruct((t, LANES), F32)
    return pl.pallas_call(body, grid=(t // tr,),
                          in_specs=[pl.BlockSpec((tr, 1), lambda i: (i, 0)), pl.BlockSpec((1, LANES), lambda i: (0, 0))],
                          out_specs=[row] * 5, out_shape=[s] * 5, compiler_params=_cp("parallel"), name=name)(pos, inv_freq)


def _rope(x, c, s1, s2, *, out_dtype, sum_heads=False, name):
    nh, t, _ = x.shape
    tr = _row_tile(t)
    half = MLA_ROPE // 2

    def body(x_ref, c_ref, s1_ref, s2_ref, o_ref):
        total = None
        for h in range(nh):
            xv = x_ref[h].astype(F32)
            y = xv * c_ref[...] + pltpu.roll(xv, LANES - half, 1) * s1_ref[...] + pltpu.roll(xv, half, 1) * s2_ref[...]
            if sum_heads:
                total = y if total is None else total + y
            else:
                o_ref[h] = y.astype(out_dtype)
        if sum_heads:
            o_ref[...] = total

    tab = pl.BlockSpec((tr, LANES), lambda i: (i, 0))
    xs = pl.BlockSpec((nh, tr, LANES), lambda i: (0, i, 0))
    if sum_heads:
        return pl.pallas_call(body, grid=(t // tr,), in_specs=[xs, tab, tab, tab], out_specs=tab,
                              out_shape=jax.ShapeDtypeStruct((t, LANES), F32),
                              compiler_params=_cp("parallel"), name=name)(x, c, s1, s2)
    return pl.pallas_call(body, grid=(t // tr,), in_specs=[xs, tab, tab, tab], out_specs=xs,
                          out_shape=jax.ShapeDtypeStruct(x.shape, out_dtype),
                          compiler_params=_cp("parallel"), name=name)(x, c, s1, s2)


FLASH_BLK = 512


def _diag_mask(blk):
    return (_iota2((blk, blk), 1) // CHUNK) <= (_iota2((blk, blk), 0) // CHUNK)


def _flash_fwd(q, k, v, *, name):
    nh, t, _ = q.shape
    blk = min(FLASH_BLK, t)
    bq = min(FLASH_BLK // 2, t)
    per = blk // bq
    scale = MLA_QK ** -0.5

    def body(q_ref, k_ref, v_ref, o_ref, lse_ref):
        i = pl.program_id(1)
        qv = q_ref[0]
        diag = i // per

        def step(j, carry, masked):
            m, l, acc = carry
            off = pl.multiple_of(j * blk, blk)
            kb = k_ref[0, pl.ds(off, blk), :]
            vb = v_ref[0, pl.ds(off, blk), :]
            s = _dg(qv, kb, NT) * scale
            if masked:
                rows = i * bq + _iota2((bq, blk), 0)
                cols = j * blk + _iota2((bq, blk), 1)
                s = jnp.where(cols // CHUNK <= rows // CHUNK, s, NEG_INF)
            m_new = jnp.maximum(m, jnp.max(s, axis=-1, keepdims=True))
            p = jnp.exp(s - m_new)
            alpha = jnp.exp(m - m_new)
            return m_new, alpha * l + jnp.sum(p, axis=-1, keepdims=True), alpha * acc + _dg(p, vb, NN)

        init = (jnp.full((bq, 1), NEG_INF, F32), jnp.zeros((bq, 1), F32), jnp.zeros((bq, MLA_V), F32))
        carry = lax.fori_loop(0, diag, lambda j, c: step(j, c, False), init)
        m, l, acc = step(diag, carry, True)
        o_ref[0] = (acc / l).astype(BF16)
        lse_ref[0] = m + jnp.log(l)

    qs = pl.BlockSpec((1, bq, LANES), lambda h, i: (h, i, 0))
    return pl.pallas_call(
        body, grid=(nh, t // bq),
        in_specs=[qs, pl.BlockSpec((1, t, LANES), lambda h, i: (h, 0, 0)), pl.BlockSpec((1, t, MLA_V), lambda h, i: (h, 0, 0))],
        out_specs=[pl.BlockSpec((1, bq, MLA_V), lambda h, i: (h, i, 0)), pl.BlockSpec((1, bq, 1), lambda h, i: (h, i, 0))],
        out_shape=[jax.ShapeDtypeStruct((nh, t, MLA_V), BF16), jax.ShapeDtypeStruct((nh, t, 1), F32)],
        compiler_params=_cp("parallel", "parallel"), name=name)(q, k, v)


def _flash_bwd(q, k, v, do, o, lse, *, name):
    nh, t, _ = q.shape
    blk = min(FLASH_BLK, t)
    nq = t // blk
    scale = MLA_QK ** -0.5

    def body(q_ref, k_ref, v_ref, do_ref, o_ref, lse_ref, dq_ref, dk_ref, dv_ref, dl):
        j = pl.program_id(1)

        @pl.when(j == 0)
        def _():
            dq_ref[...] = jnp.zeros_like(dq_ref)
            dl[...] = jnp.sum(do_ref[0].astype(F32) * o_ref[0].astype(F32), axis=-1, keepdims=True)

        kb = k_ref[0]
        vb = v_ref[0]

        def step(i, carry, masked):
            dk, dv = carry
            rows = pl.ds(pl.multiple_of(i * blk, blk), blk)
            qb = q_ref[0, rows, :]
            dob = do_ref[0, rows, :]
            s = _dg(qb, kb, NT) * scale
            if masked:
                s = jnp.where(_diag_mask(blk), s, NEG_INF)
            p = jnp.exp(s - lse_ref[0, rows, :])
            ds = p * (_dg(dob, vb, NT) - dl[rows, :]) * scale
            dq_ref[0, rows, :] += _dg(ds, kb, NN)
            return dk + _dg(ds, qb, TN), dv + _dg(p, dob, TN)

        carry = step(j, (jnp.zeros((blk, LANES), F32), jnp.zeros((blk, MLA_V), F32)), True)
        dk, dv = lax.fori_loop(j + 1, nq, lambda i, c: step(i, c, False), carry)
        dk_ref[0] = dk
        dv_ref[0] = dv

    ks = pl.BlockSpec((1, blk, LANES), lambda h, j: (h, j, 0))
    vs = pl.BlockSpec((1, blk, MLA_V), lambda h, j: (h, j, 0))
    fl = lambda w: pl.BlockSpec((1, t, w), lambda h, j: (h, 0, 0))
    return pl.pallas_call(
        body, grid=(nh, nq),
        in_specs=[fl(LANES), ks, vs, fl(MLA_V), fl(MLA_V), fl(1)],
        out_specs=[fl(LANES), ks, vs],
        out_shape=[jax.ShapeDtypeStruct((nh, t, LANES), F32), jax.ShapeDtypeStruct((nh, t, LANES), F32),
                   jax.ShapeDtypeStruct((nh, t, MLA_V), F32)],
        scratch_shapes=[pltpu.VMEM((t, 1), F32)],
        compiler_params=_cp("parallel", "arbitrary"), name=name)(q, k, v, do, o, lse)


def _heads_first(a, width):
    t = a.shape[0]
    return a.reshape(t, MLA_HEADS, width).transpose(1, 0, 2)


def _heads_last(a):
    return a.transpose(1, 0, 2).reshape(a.shape[1], -1)


def _mla_layer_fwd(h, pos, w, tag):
    t = h.shape[0]
    zq = jnp.zeros((1, MLA_Q_RANK), F32)
    zkv = jnp.zeros((1, MLA_KV_RANK), F32)
    p = _mm(h, w["w_in"], name=tag + "_in")
    cq, ckv, krp = p[:, :MLA_Q_RANK], p[:, MLA_Q_RANK:MLA_Q_RANK + MLA_KV_RANK], p[:, MLA_Q_RANK + MLA_KV_RANK:MLA_IN]
    qn = _premod_fwd(cq, w["q_norm_g"], zq, zq, name=tag + "_qnorm")
    kvn = _premod_fwd(ckv, w["kv_norm_g"], zkv, zkv, name=tag + "_kvnorm")
    q = _mm(qn, w["w_uq"], name=tag + "_uq")
    kv = _mm(kvn, w["w_ukv"], name=tag + "_ukv")
    q_pre = jnp.pad(_heads_first(q, MLA_QK), ((0, 0), (0, 0), (0, LANES - MLA_QK)))
    kv3 = _heads_first(kv, MLA_NOPE + MLA_V)
    k_pre = jnp.concatenate([kv3[:, :, :MLA_NOPE], jnp.broadcast_to(krp[None], (MLA_HEADS, t, MLA_ROPE)),
                             jnp.zeros((MLA_HEADS, t, LANES - MLA_QK), F32)], axis=-1)
    vh = kv3[:, :, MLA_NOPE:].astype(BF16)
    half = MLA_ROPE // 2
    freq = ROPE_THETA ** (-jnp.arange(half, dtype=F32) / half)
    inv_freq = jnp.concatenate([jnp.zeros((MLA_NOPE,), F32), freq, freq, jnp.zeros((LANES - MLA_QK,), F32)])[None]
    tabs = _rope_tables(pos.reshape(t, 1), inv_freq, name=tag + "_tables")
    qr = _rope(q_pre, tabs[0], tabs[1], tabs[2], out_dtype=BF16, name=tag + "_ropeq")
    kr = _rope(k_pre, tabs[0], tabs[1], tabs[2], out_dtype=BF16, name=tag + "_ropek")
    o, lse = _flash_fwd(qr, kr, vh, name=tag + "_attn")
    of = _heads_last(o)
    y = _mm(of, w["w_out"], name=tag + "_out")
    return y, (cq, ckv, qn, kvn, qr, kr, vh, o, lse, of, tabs)


def _mla_layer_bwd(dy, h, saved, w, tag, sink=None):
    cq, ckv, qn, kvn, qr, kr, vh, o, lse, of, tabs = saved
    t = h.shape[0]
    zq = jnp.zeros((1, MLA_Q_RANK), F32)
    zkv = jnp.zeros((1, MLA_KV_RANK), F32)
    dof = _mm(dy, w["w_out"], tb=True, out_dtype=BF16, name=tag + "_dof")
    dw_out, gf = _dw_out(of, dy, sink, tag + "_dwout")
    do = _heads_first(dof, MLA_V)
    dqr, dkr, dv = _flash_bwd(qr, kr, vh, do, o, lse, name=tag + "_attn_b")
    dq_pre = _rope(dqr, tabs[0], tabs[3], tabs[4], out_dtype=F32, name=tag + "_ropeq_b")
    dk_sum = _rope(dkr, tabs[0], tabs[3], tabs[4], out_dtype=F32, sum_heads=True, name=tag + "_ropek_b")
    dq = _heads_last(dq_pre[:, :, :MLA_QK])
    dkv = _heads_last(jnp.concatenate([dkr[:, :, :MLA_NOPE], dv], axis=-1))
    dw_uq = _mm(qn, dq, ta=True, out_dtype=BF16, name=tag + "_dwuq")
    dqn = _mm(dq, w["w_uq"], tb=True, name=tag + "_dqn")
    dw_ukv = _mm(kvn, dkv, ta=True, out_dtype=BF16, name=tag + "_dwukv")
    dkvn = _mm(dkv, w["w_ukv"], tb=True, name=tag + "_dkvn")
    dcq, dqg, _, _ = _premod_bwd(dqn, cq, w["q_norm_g"], zq, jnp.zeros_like(cq), name=tag + "_qnorm_b")
    dckv, dkvg, _, _ = _premod_bwd(dkvn, ckv, w["kv_norm_g"], zkv, jnp.zeros_like(ckv), name=tag + "_kvnorm_b")
    dp = jnp.concatenate([dcq, dckv, dk_sum[:, MLA_NOPE:MLA_QK], jnp.zeros((t, MLA_IN_PAD - MLA_IN), F32)], axis=1).astype(BF16)
    dw_in = _mm(h, dp, ta=True, out_dtype=BF16, name=tag + "_dwin")
    dh = _mm(dp, w["w_in"], tb=True, out_dtype=BF16, name=tag + "_dh")
    grads = dict(w_in=dw_in[:, :MLA_IN], q_norm_g=dqg[0], w_uq=dw_uq, kv_norm_g=dkvg[0], w_ukv=dw_ukv, w_out=dw_out)
    return dh, grads, gf


CONV_HALO = 8


def _conv_tiles(t):
    return min(512, t), 1024


def _gdn_conv_fwd(p, w, *, name):
    t = p.shape[0]
    tr, tc = _conv_tiles(t)
    hb = tr // CONV_HALO

    def body(x_ref, halo_ref, w_ref, o_ref, buf):
        i = pl.program_id(0)
        buf[0:CONV_HALO, :] = halo_ref[...] * (i > 0).astype(F32)
        buf[CONV_HALO:CONV_HALO + tr, :] = x_ref[...]
        base = CONV_HALO - (GDN_CONV - 1)
        acc = buf[pl.ds(base, tr), :] * w_ref[0:1, :]
        for j in range(1, GDN_CONV):
            acc = acc + buf[pl.ds(base + j, tr), :] * w_ref[j:j + 1, :]
        o_ref[...] = acc * _sigmoid(acc)

    return pl.pallas_call(
        body, grid=(t // tr, GDN_CONV_W // tc),
        in_specs=[pl.BlockSpec((tr, tc), lambda i, j: (i, j)),
                  pl.BlockSpec((CONV_HALO, tc), lambda i, j: (jnp.maximum(i * hb - 1, 0), j)),
                  pl.BlockSpec((GDN_CONV, tc), lambda i, j: (0, j))],
        out_specs=pl.BlockSpec((tr, tc), lambda i, j: (i, j)),
        out_shape=jax.ShapeDtypeStruct((t, GDN_CONV_W), F32),
        scratch_shapes=[pltpu.VMEM((tr + CONV_HALO, tc), F32)],
        compiler_params=_cp("parallel", "parallel"), name=name)(p, p, w)


def _gdn_conv_bwd(d, p, w, col0, *, name):
    t = p.shape[0]
    tr, tc = _conv_tiles(t)
    hb = tr // CONV_HALO
    nr = t // tr
    ext = tr + CONV_HALO

    def body(x_ref, xp_ref, xn_ref, d_ref, dn_ref, w_ref, dx_ref, dw_ref, bufx, bufd):
        i = pl.program_id(1)

        @pl.when(i == 0)
        def _():
            dw_ref[...] = jnp.zeros_like(dw_ref)

        last = (i < nr - 1).astype(F32)
        bufx[0:CONV_HALO, :] = xp_ref[...] * (i > 0).astype(F32)
        bufx[CONV_HALO:CONV_HALO + tr, :] = x_ref[...]
        bufx[CONV_HALO + tr:, :] = xn_ref[...] * last
        base = CONV_HALO - (GDN_CONV - 1)
        acc = bufx[pl.ds(base, ext), :] * w_ref[0:1, :]
        for j in range(1, GDN_CONV):
            acc = acc + bufx[pl.ds(base + j, ext), :] * w_ref[j:j + 1, :]
        sg = _sigmoid(acc)
        dsilu = sg * (1.0 + acc * (1.0 - sg))
        bufd[0:tr, :] = d_ref[...] * dsilu[0:tr, :]
        bufd[tr:, :] = dn_ref[...] * last * dsilu[tr:, :]
        dx = bufd[pl.ds(GDN_CONV - 1, tr), :] * w_ref[0:1, :]
        for j in range(1, GDN_CONV):
            dx = dx + bufd[pl.ds(GDN_CONV - 1 - j, tr), :] * w_ref[j:j + 1, :]
        dx_ref[...] = dx.astype(BF16)
        dc = bufd[0:tr, :]
        for j in range(GDN_CONV):
            dw_ref[j:j + 1, :] += jnp.sum(dc * bufx[pl.ds(base + j, tr), :], axis=0, keepdims=True)

    cb = col0 // tc
    width = d.shape[1]
    main = lambda off: pl.BlockSpec((tr, tc), lambda j, i: (i, j + off))
    prev = pl.BlockSpec((CONV_HALO, tc), lambda j, i: (jnp.maximum(i * hb - 1, 0), j + cb))
    nxt = lambda off: pl.BlockSpec((CONV_HALO, tc), lambda j, i: (jnp.minimum((i + 1) * hb, t // CONV_HALO - 1), j + off))
    wsp = lambda off: pl.BlockSpec((GDN_CONV, tc), lambda j, i: (0, j + off))
    return pl.pallas_call(
        body, grid=(width // tc, nr),
        in_specs=[main(cb), prev, nxt(cb), main(0), nxt(0), wsp(cb)], out_specs=[main(0), wsp(0)],
        out_shape=[jax.ShapeDtypeStruct((t, width), BF16), jax.ShapeDtypeStruct((GDN_CONV, width), F32)],
        scratch_shapes=[pltpu.VMEM((tr + 2 * CONV_HALO, tc), F32), pltpu.VMEM((ext, tc), F32)],
        compiler_params=_cp("parallel", "arbitrary"), name=name)(p, p, p, d, d, w)


def _dot_sel(sel, b, dims=NN, sel_first=True):
    s = sel.astype(BF16)
    b1 = b.astype(BF16)
    r1 = b - b1.astype(F32)
    b2 = r1.astype(BF16)
    b3 = (r1 - b2.astype(F32)).astype(BF16)
    if sel_first:
        d = lambda v: lax.dot_general(s, v, (dims, ((), ())), preferred_element_type=F32)
    else:
        d = lambda v: lax.dot_general(v, s, (dims, ((), ())), preferred_element_type=F32)
    return d(b1) + (d(b2) + d(b3))


def _gdn_chunks(qraws, kraws, vs, braws, araws, alogs, dtbs):
    nv = len(vs)
    row = _iota2((CHUNK, CHUNK), 0)
    col = _iota2((CHUNK, CHUNK), 1)
    strict = row > col
    triu = (row <= col).astype(F32)
    tril = (row >= col).astype(F32)
    ones = jnp.ones((CHUNK, CHUNK), F32)
    keys = []
    for qraw, kraw in zip(qraws, kraws):
        rq = lax.rsqrt(jnp.sum(qraw * qraw, axis=-1, keepdims=True) + EPS)
        rk = lax.rsqrt(jnp.sum(kraw * kraw, axis=-1, keepdims=True) + EPS)
        qn = qraw * rq
        keys.append(dict(rq=rq, rk=rk, qn=qn, qh=qn * (GDN_DK ** -0.5), kh=kraw * rk))
    kks = [_dg(kd["kh"], kd["kh"], NT) for kd in keys]
    cs = []
    for h in range(nv):
        c = dict(keys[h // 2])
        c.update(v=vs[h], kk=kks[h // 2], strict=strict, triu=triu)
        c["beta"] = _sigmoid(braws[h])
        c["ea"] = jnp.exp(alogs[h])
        c["xs"] = araws[h] + dtbs[h]
        c["g"] = -c["ea"] * _softplus(c["xs"])
        cs.append(c)
    gbs = [jnp.broadcast_to(c["g"], (CHUNK, LANES)) for c in cs]
    cums = [_dot_sel(tril, gb) for gb in gbs]
    cum_js = [_dot_sel(ones, gb[:, :CHUNK] * triu) for gb in gbs]
    for c, cum, cum_j in zip(cs, cums, cum_js):
        diff = jnp.where(strict, cum[:, :CHUNK] - cum_j, 0.0)
        c["dm"] = jnp.where(strict, jnp.exp(diff), 0.0)
        c["a"] = (c["beta"] * c["dm"]) * c["kk"]
        c_last = cum[CHUNK - 1:CHUNK, :]
        c["e"] = jnp.exp(cum)
        c["f"] = jnp.exp(c_last - cum)
        c["dec"] = jnp.exp(c_last)
        c["rv"] = c["beta"] * c["v"]
        c["rk_rhs"] = (c["beta"] * c["e"]) * c["kh"]
        c["ke"] = c["kh"] * c["f"]
    return cs


def _unit_lower_inverses(mats):
    eye = (_iota2((CHUNK, CHUNK), 0) == _iota2((CHUNK, CHUNK), 1)).astype(F32)
    ts = [eye - a for a in mats]
    pws = list(mats)
    for _ in range(5):
        pws = [_dot3(pw, pw) for pw in pws]
        ts = [t + _dot3(t, pw) for t, pw in zip(ts, pws)]
    return ts


GDN_HB = 16


def _gdn_specs(chunk_of):
    hb = GDN_HB
    kw = hb // 2 * GDN_DK
    vw = hb * GDN_DV
    qs = pl.BlockSpec((CHUNK, kw), lambda g, n: (chunk_of(n), g))
    ks = pl.BlockSpec((CHUNK, kw), lambda g, n: (chunk_of(n), GDN_QKW // kw + g))
    vs = pl.BlockSpec((CHUNK, vw), lambda g, n: (chunk_of(n), 2 * GDN_QKW // vw + g))
    zs = pl.BlockSpec((CHUNK, vw), lambda g, n: (chunk_of(n), GDN_CONV_W // vw + g))
    assert hb == GDN_V_HEADS and (GDN_CONV_W + GDN_VW) % LANES == 0
    gates = pl.BlockSpec((CHUNK, LANES), lambda g, n: (chunk_of(n), (GDN_CONV_W + GDN_VW) // LANES))
    one = pl.BlockSpec((hb, 1, 1), lambda g, n: (g, 0, 0))
    ng = pl.BlockSpec((1, GDN_DV), lambda g, n: (0, 0))
    hd = pl.BlockSpec((CHUNK, vw), lambda g, n: (chunk_of(n), g))
    return qs, ks, vs, zs, gates, one, ng, hd


def _gate_columns(gates):
    return ([gates[:, h:h + 1] for h in range(GDN_V_HEADS)],
            [gates[:, GDN_V_HEADS + h:GDN_V_HEADS + h + 1] for h in range(GDN_V_HEADS)])


def _gdn_fwd(qkv, p, alog, dtb, ng, *, name):
    t = qkv.shape[0]
    nc = t // CHUNK
    nh = GDN_V_HEADS

    def body(q_ref, k_ref, v_ref, z_ref, gates_ref, alog_ref, dtb_ref, ng_ref, og_ref, s_ref, t_ref, st):
        @pl.when(pl.program_id(1) == 0)
        def _():
            st[...] = jnp.zeros_like(st)

        hs = range(GDN_HB)
        kqs = [slice(j * GDN_DK, (j + 1) * GDN_DK) for j in range(GDN_HB // 2)]
        vsl = [slice(h * GDN_DV, (h + 1) * GDN_DV) for h in hs]
        braws, araws = _gate_columns(gates_ref[...])
        cs = _gdn_chunks([q_ref[:, s] for s in kqs], [k_ref[:, s] for s in kqs], [v_ref[:, s] for s in vsl],
                         braws, araws, [alog_ref[h] for h in hs], [dtb_ref[h] for h in hs])
        tms = _unit_lower_inverses([c["a"] for c in cs])
        s0 = [st[h] for h in hs]
        wv = [_dot3(tms[h], cs[h]["rv"]) for h in hs]
        wk = [_dot3(tms[h], cs[h]["rk_rhs"]) for h in hs]
        u = [wv[h] - _dg(wk[h], s0[h], NN) for h in hs]
        s1 = [cs[h]["dec"] * s0[h] + _dg(cs[h]["ke"], u[h], TN) for h in hs]
        o = [_dg(cs[h]["qh"], s1[h], NN) for h in hs]
        for h in hs:
            t_ref[h, 0] = tms[h]
            st[h] = s1[h]
            s_ref[h, 0] = s1[h]
            rn = lax.rsqrt(jnp.mean(o[h] * o[h], axis=-1, keepdims=True) + EPS)
            zv = z_ref[:, vsl[h]]
            og_ref[:, vsl[h]] = (((o[h] * rn) * ng_ref[...]) * (zv * _sigmoid(zv))).astype(BF16)

    qs, ks, vs, zs, gates, one, ngs, hd = _gdn_specs(lambda n: n)
    return pl.pallas_call(
        body, grid=(nh // GDN_HB, nc),
        in_specs=[qs, ks, vs, zs, gates, one, one, ngs],
        out_specs=[hd,
                   pl.BlockSpec((GDN_HB, 1, GDN_DK, GDN_DV), lambda g, n: (g, n, 0, 0)),
                   pl.BlockSpec((GDN_HB, 1, CHUNK, CHUNK), lambda g, n: (g, n, 0, 0))],
        out_shape=[jax.ShapeDtypeStruct((t, GDN_VW), BF16), jax.ShapeDtypeStruct((nh, nc, GDN_DK, GDN_DV), F32),
                   jax.ShapeDtypeStruct((nh, nc, CHUNK, CHUNK), F32)],
        scratch_shapes=[pltpu.VMEM((GDN_HB, GDN_DK, GDN_DV), F32)],
        compiler_params=_cp("parallel", "arbitrary"), name=name)(qkv, qkv, qkv, p, p, alog, dtb, ng)


def _gdn_bwd(qkv, p, alog, dtb, ng, dog, sall, tall, *, name):
    t = qkv.shape[0]
    nc = t // CHUNK
    nh = GDN_V_HEADS

    def body(q_ref, k_ref, v_ref, z_ref, gates_ref, alog_ref, dtb_ref, ng_ref, dog_ref, s1_ref, s0_ref, t_ref,
             dq_ref, dk_ref, dv_ref, dz_ref, dgates_ref, dalog_ref, ddtb_ref, dng_ref, gc):
        grp = pl.program_id(0)
        i = pl.program_id(1)

        @pl.when(i == 0)
        def _():
            gc[...] = jnp.zeros_like(gc)
            dalog_ref[...] = jnp.zeros_like(dalog_ref)
            ddtb_ref[...] = jnp.zeros_like(ddtb_ref)

        @pl.when((i == 0) & (grp == 0))
        def _():
            dng_ref[...] = jnp.zeros_like(dng_ref)

        has_prev = (i < nc - 1).astype(F32)
        ngv = ng_ref[...]
        ones = jnp.ones((CHUNK, LANES), F32)
        hs = range(GDN_HB)
        kqs = [slice(j * GDN_DK, (j + 1) * GDN_DK) for j in range(GDN_HB // 2)]
        vsl = [slice(h * GDN_DV, (h + 1) * GDN_DV) for h in hs]
        braws, araws = _gate_columns(gates_ref[...])
        cs = _gdn_chunks([q_ref[:, s] for s in kqs], [k_ref[:, s] for s in kqs], [v_ref[:, s] for s in vsl],
                         braws, araws, [alog_ref[h] for h in hs], [dtb_ref[h] for h in hs])
        tms = [t_ref[h, 0] for h in hs]
        s1 = [s1_ref[h, 0] for h in hs]
        s0 = [s0_ref[h, 0] * has_prev for h in hs]
        wv = [_dot3(tms[h], cs[h]["rv"]) for h in hs]
        wk = [_dot3(tms[h], cs[h]["rk_rhs"]) for h in hs]
        u = [wv[h] - _dg(wk[h], s0[h], NN) for h in hs]
        o = [_dg(cs[h]["qh"], s1[h], NN) for h in hs]
        dng = jnp.zeros((1, GDN_DV), F32)
        do = []
        for h in hs:
            zv = z_ref[:, vsl[h]]
            dogv = dog_ref[:, vsl[h]]
            rn = lax.rsqrt(jnp.mean(o[h] * o[h], axis=-1, keepdims=True) + EPS)
            zo = o[h] * rn
            sg = _sigmoid(zv)
            sl = zv * sg
            dng = dng + jnp.sum(dogv * zo * sl, axis=0, keepdims=True)
            dz_ref[:, vsl[h]] = (dogv * (zo * ngv) * (sg * (1.0 + zv * (1.0 - sg)))).astype(BF16)
            dzo = dogv * sl * ngv
            do.append(rn * (dzo - zo * jnp.mean(dzo * zo, axis=-1, keepdims=True)))
        dng_ref[...] += dng
        g_tot = [gc[h] + _dg(cs[h]["qh"], do[h], TN) for h in hs]
        dqh = [_dg(do[h], s1[h], NT) for h in hs]
        dke = [_dg(u[h], g_tot[h], NT) for h in hs]
        du = [_dg(cs[h]["ke"], g_tot[h], NN) for h in hs]
        gnew = [cs[h]["dec"] * g_tot[h] - _dg(wk[h], du[h], TN) for h in hs]
        dwk = [-_dg(du[h], s0[h], NT) for h in hs]
        drv = [_dot3(tms[h], du[h], TN) for h in hs]
        drk = [_dot3(tms[h], dwk[h], TN) for h in hs]
        da = [jnp.where(cs[h]["strict"], -(_dot3(drv[h], wv[h], NT) + _dot3(drk[h], wk[h], NT)), 0.0) for h in hs]
        mx = [da[h] * cs[h]["dm"] * cs[h]["kk"] for h in hs]
        aa = [mx[h] * cs[h]["beta"] for h in hs]
        colsum = [_dot_sel(ones, aa[h], TN, sel_first=False)[:, 0:1] for h in hs]
        bm = [(da[h] * cs[h]["beta"]) * cs[h]["dm"] for h in hs]
        dkh = [_dg(bm[h], cs[h]["kh"], NN) + _dg(bm[h], cs[h]["kh"], TN) for h in hs]
        dcum, dcl, dbeta = [], [], []
        for h in hs:
            c = cs[h]
            beta, kh, e, f, dec, ke = c["beta"], c["kh"], c["e"], c["f"], c["dec"], c["ke"]
            gc[h] = gnew[h]
            ddec = jnp.sum(jnp.sum(g_tot[h] * s0[h], axis=1, keepdims=True), axis=0, keepdims=True)
            dv_ref[:, vsl[h]] = beta * drv[h]
            db = jnp.sum(mx[h], axis=1, keepdims=True) + jnp.sum(drv[h] * c["v"], axis=1, keepdims=True)
            dbeta.append(db + jnp.sum(drk[h] * (e * kh), axis=1, keepdims=True))
            dkh[h] = dkh[h] + (beta * e) * drk[h] + f * dke[h]
            ef = jnp.sum(dke[h] * ke, axis=1, keepdims=True)
            dcum.append(jnp.sum(aa[h], axis=1, keepdims=True) - colsum[h] + jnp.sum(drk[h] * c["rk_rhs"], axis=1, keepdims=True) - ef)
            dcl.append(jnp.sum(ef, axis=0, keepdims=True) + ddec * dec[:, 0:1])
        dg = [_dot_sel(cs[h]["triu"], jnp.broadcast_to(dcum[h], (CHUNK, LANES)))[:, 0:1] + dcl[h] for h in hs]
        lane = _iota2((CHUNK, LANES), 1)
        dgates = jnp.zeros((CHUNK, LANES), F32)
        for h in hs:
            c = cs[h]
            beta = c["beta"]
            daraw = dg[h] * (-c["ea"]) * _sigmoid(c["xs"])
            dgates = jnp.where(lane == h, dbeta[h] * beta * (1.0 - beta), dgates)
            dgates = jnp.where(lane == GDN_V_HEADS + h, daraw, dgates)
            dalog_ref[h] += jnp.sum(dg[h] * c["g"], axis=0, keepdims=True)
            ddtb_ref[h] += jnp.sum(daraw, axis=0, keepdims=True)
        dgates_ref[...] = dgates.astype(BF16)
        for j, sl in enumerate(kqs):
            c = cs[2 * j]
            dn = (dqh[2 * j] + dqh[2 * j + 1]) * (GDN_DK ** -0.5)
            dks = dkh[2 * j] + dkh[2 * j + 1]
            dq_ref[:, sl] = c["rq"] * (dn - c["qn"] * jnp.sum(dn * c["qn"], axis=-1, keepdims=True))
            dk_ref[:, sl] = c["rk"] * (dks - c["kh"] * jnp.sum(dks * c["kh"], axis=-1, keepdims=True))

    rev = lambda n: nc - 1 - n
    qs, ks, vs, zs, gates, one, ngs, hd = _gdn_specs(rev)
    s1s = pl.BlockSpec((GDN_HB, 1, GDN_DK, GDN_DV), lambda g, n: (g, rev(n), 0, 0))
    s0s = pl.BlockSpec((GDN_HB, 1, GDN_DK, GDN_DV), lambda g, n: (g, jnp.maximum(rev(n) - 1, 0), 0, 0))
    ts = pl.BlockSpec((GDN_HB, 1, CHUNK, CHUNK), lambda g, n: (g, rev(n), 0, 0))
    dgs = pl.BlockSpec((CHUNK, LANES), lambda g, n: (rev(n), 0))
    big = jax.ShapeDtypeStruct((t, GDN_VW), F32)
    keyw = jax.ShapeDtypeStruct((t, GDN_QKW), F32)
    ones_s = jax.ShapeDtypeStruct((nh, 1, 1), F32)
    return pl.pallas_call(
        body, grid=(nh // GDN_HB, nc),
        in_specs=[qs, ks, vs, zs, gates, one, one, ngs, hd, s1s, s0s, ts],
        out_specs=[qs, qs, hd, hd, dgs, one, one, ngs],
        out_shape=[keyw, keyw, big, jax.ShapeDtypeStruct((t, GDN_VW), BF16), jax.ShapeDtypeStruct((t, LANES), BF16),
                   ones_s, ones_s, jax.ShapeDtypeStruct((1, GDN_DV), F32)],
        scratch_shapes=[pltpu.VMEM((GDN_HB, GDN_DK, GDN_DV), F32)],
        compiler_params=_cp("arbitrary", "arbitrary"), name=name)(qkv, qkv, qkv, p, p, alog, dtb, ng, dog, sall, sall, tall)


def _gdn_layer_fwd(h, w, tag):
    p = _mm(h, w["w_in"], name=tag + "_in")
    qkv = _gdn_conv_fwd(p, w["conv_w"], name=tag + "_conv")
    og, sall, tall = _gdn_fwd(qkv, p, w["a_log"], w["dt_bias"], w["norm_g"], name=tag + "_scan")
    y = _mm(og, w["w_out"], name=tag + "_out")
    return y, (p, qkv, og, sall, tall)


def _gdn_layer_bwd(dy, h, saved, w, tag, sink=None):
    p, qkv, og, sall, tall = saved
    t = h.shape[0]
    dog = _mm(dy, w["w_out"], tb=True, name=tag + "_dog")
    dw_out, gf = _dw_out(og, dy, sink, tag + "_dwout")
    dq, dk, dv, dz, dgates, dalog, ddtb, dng = _gdn_bwd(
        qkv, p, w["a_log"], w["dt_bias"], w["norm_g"], dog, sall, tall, name=tag + "_scanb")
    dpre_q, dcw_q = _gdn_conv_bwd(dq, p, w["conv_w"], 0, name=tag + "_convb_q")
    dpre_k, dcw_k = _gdn_conv_bwd(dk, p, w["conv_w"], GDN_QKW, name=tag + "_convb_k")
    dpre_v, dcw_v = _gdn_conv_bwd(dv, p, w["conv_w"], 2 * GDN_QKW, name=tag + "_convb_v")
    dconv_w = jnp.concatenate([dcw_q, dcw_k, dcw_v], axis=1)
    dp = jnp.concatenate([dpre_q, dpre_k, dpre_v, dz, dgates,
                          jnp.zeros((t, GDN_IN_PAD - GDN_CONV_W - GDN_VW - LANES), BF16)], axis=1)
    dw_in = _mm(h, dp, ta=True, out_dtype=BF16, name=tag + "_dwin")
    dh = _mm(dp, w["w_in"], tb=True, out_dtype=BF16, name=tag + "_dh")
    grads = dict(w_in=dw_in[:, :GDN_IN], conv_w=dconv_w, a_log=dalog[:, 0, 0], dt_bias=ddtb[:, 0, 0], norm_g=dng[0], w_out=dw_out)
    return dh, grads, gf


MESH_ID = pl.DeviceIdType.MESH
FLAT_W = 1024
FLAT_ROWS = 13056
FLAT_TILE = 384


def _exchange(name, ins, out_shapes, plan, n_remote, n_local):
    def body(*refs):
        in_refs = refs[:len(ins)]
        out_refs = refs[len(ins):len(ins) + len(out_shapes)]
        ssem, rsem, lsem = refs[len(ins) + len(out_shapes):]
        x, y, c = lax.axis_index("x"), lax.axis_index("y"), lax.axis_index("c")
        stages, local_copies = plan(x, y, c, in_refs, out_refs)
        assert sum(len(s) for s in stages) == n_remote and len(local_copies) == n_local
        locs = [pltpu.make_async_copy(s, d, lsem.at[i]) for i, (s, d) in enumerate(local_copies)]
        for cp in locs:
            cp.start()
        sent = []
        k = 0
        for stage in stages:
            arrivals = []
            for src, dst, peer, landing in stage:
                cp = pltpu.make_async_remote_copy(src_ref=src, dst_ref=dst, send_sem=ssem.at[k], recv_sem=rsem.at[k],
                                                  device_id=peer, device_id_type=MESH_ID)
                cp.start()
                sent.append(cp)
                arrivals.append(pltpu.make_async_remote_copy(src_ref=src, dst_ref=landing, send_sem=ssem.at[k],
                                                             recv_sem=rsem.at[k], device_id=peer, device_id_type=MESH_ID))
                k += 1
            for cp in arrivals:
                cp.wait_recv()
        for cp in sent:
            cp.wait_send()
        for cp in locs:
            cp.wait()

    hbm = pl.BlockSpec(memory_space=pl.ANY)
    return pl.pallas_call(
        body, in_specs=[hbm] * len(ins), out_specs=[hbm] * len(out_shapes), out_shape=out_shapes,
        scratch_shapes=[pltpu.SemaphoreType.DMA((n_remote,)), pltpu.SemaphoreType.DMA((n_remote,)),
                        pltpu.SemaphoreType.DMA((max(n_local, 1),))],
        name=name)(*ins)


def _other_chips(x, y):
    return [(1 - x, y), (x, 1 - y), (1 - x, 1 - y)]


def _all8_gather(a, *, name):
    def plan(x, y, c, ins, outs):
        (src,), (dst,) = ins, outs
        me = 4 * x + 2 * y + c
        stage = []
        for fx, fy, fc in [(0, 0, 1), (0, 1, 0), (0, 1, 1), (1, 0, 0), (1, 0, 1), (1, 1, 0), (1, 1, 1)]:
            px, py, pc = (1 - x if fx else x), (1 - y if fy else y), (1 - c if fc else c)
            stage.append((src, dst.at[me], (px, py, pc), dst.at[4 * px + 2 * py + pc]))
        return [stage], [(src, dst.at[me])]

    return _exchange(name, [a], [jax.ShapeDtypeStruct((8,) + a.shape, a.dtype)], plan, 7, 1)[0]


def _chip_gather(flat, *, name):
    rows = flat.shape[0]
    half = rows // 2

    def plan(x, y, c, ins, outs):
        (src,), (dst,) = ins, outs
        me = 2 * x + y
        mine = pl.ds(c * half, half)
        theirs = pl.ds((1 - c) * half, half)
        ici = [(src.at[mine], dst.at[me, mine], (px, py, c), dst.at[2 * px + py, mine]) for px, py in _other_chips(x, y)]
        d2d = [(dst.at[2 * px + py, mine], dst.at[2 * px + py, mine], (x, y, 1 - c), dst.at[2 * px + py, theirs])
               for px, py in _other_chips(x, y)]
        return [ici, d2d], []

    return _exchange(name, [flat], [jax.ShapeDtypeStruct((4,) + flat.shape, flat.dtype)], plan, 6, 0)[0]


def _add_sibling(gf, buf_a, core, *, name):
    _, rows, w = gf.shape
    half = rows // 2
    nb = half // FLAT_TILE

    def body(c_ref, g_ref, a_ref, o_ref):
        o_ref[...] = (g_ref[...].astype(F32) + a_ref[...].astype(F32)).astype(BF16)

    blk = (1, FLAT_TILE, w)
    return pl.pallas_call(
        body,
        grid_spec=pltpu.PrefetchScalarGridSpec(
            num_scalar_prefetch=1, grid=(4, nb),
            in_specs=[pl.BlockSpec(blk, lambda s, i, c_ref: (s, c_ref[0] * nb + i, 0)), pl.BlockSpec(blk, lambda s, i, c_ref: (s, i, 0))],
            out_specs=pl.BlockSpec(blk, lambda s, i, c_ref: (s, i, 0))),
        out_shape=jax.ShapeDtypeStruct((4, half, w), BF16), compiler_params=_cp("parallel", "parallel"), name=name)(core, gf, buf_a)


def _sum_chips(hsum, buf_b, chip, *, name):
    _, half, w = hsum.shape
    nb = half // FLAT_TILE

    def body(c_ref, h_ref, b0_ref, b1_ref, b2_ref, b3_ref, o_ref):
        me = c_ref[0]
        own = h_ref[0].astype(F32)
        acc = None
        for j, b_ref in enumerate((b0_ref, b1_ref, b2_ref, b3_ref)):
            term = jnp.where(me == j, own, b_ref[0].astype(F32))
            acc = term if acc is None else acc + term
        o_ref[...] = acc

    blk = (1, FLAT_TILE, w)

    def other(j):
        return pl.BlockSpec(blk, lambda i, c_ref: (jnp.where(c_ref[0] == j, (j + 1) % 4, j), i, 0))

    return pl.pallas_call(
        body,
        grid_spec=pltpu.PrefetchScalarGridSpec(
            num_scalar_prefetch=1, grid=(nb,),
            in_specs=[pl.BlockSpec(blk, lambda i, c_ref: (c_ref[0], i, 0))] + [other(j) for j in range(4)],
            out_specs=pl.BlockSpec((FLAT_TILE, w), lambda i, c_ref: (i, 0))),
        out_shape=jax.ShapeDtypeStruct((half, w), F32), compiler_params=_cp("parallel"), name=name)(chip, hsum, buf_b, buf_b, buf_b, buf_b)


def _sum_slots(buf, *, name):
    n, rows, w = buf.shape
    tr = _pick(rows, (FLAT_TILE, 8))

    def body(b_ref, o_ref):
        acc = b_ref[0]
        for s in range(1, n):
            acc = acc + b_ref[s]
        o_ref[...] = acc

    return pl.pallas_call(body, grid=(rows // tr,), in_specs=[pl.BlockSpec((n, tr, w), lambda i: (0, i, 0))],
                          out_specs=pl.BlockSpec((tr, w), lambda i: (i, 0)), out_shape=jax.ShapeDtypeStruct((rows, w), F32),
                          compiler_params=_cp("parallel"), name=name)(buf)


def _reduce_scatter(gf, core, chip, *, tag):
    _, rows, w = gf.shape
    half = rows // 2

    def plan_a(x, y, c, ins, outs):
        (src,), (dst,) = ins, outs
        return [[(src.at[:, pl.ds((1 - c) * half, half)], dst, (x, y, 1 - c), dst)]], []

    buf_a = _exchange(tag + "_sibling", [gf], [jax.ShapeDtypeStruct((4, half, w), gf.dtype)], plan_a, 1, 0)[0]
    hsum = _add_sibling(gf, buf_a, core, name=tag + "_add_sibling")

    def plan_b(x, y, c, ins, outs):
        (src,), (dst,) = ins, outs
        me = 2 * x + y
        stage = [(src.at[2 * px + py], dst.at[me], (px, py, c), dst.at[2 * px + py]) for px, py in _other_chips(x, y)]
        return [stage], []

    buf_b = _exchange(tag + "_chips", [hsum], [jax.ShapeDtypeStruct((4, half, w), BF16)], plan_b, 3, 0)[0]
    mine = _sum_chips(hsum, buf_b, chip, name=tag + "_sum_chips")

    def plan_c(x, y, c, ins, outs):
        (src,), (dst,) = ins, outs
        return [[(src, dst, (x, y, 1 - c), dst)]], []

    theirs = _exchange(tag + "_halves", [mine], [jax.ShapeDtypeStruct((half, w), F32)], plan_c, 1, 0)[0]
    first = core[0] == 0
    return jnp.concatenate([jnp.where(first, mine, theirs), jnp.where(first, theirs, mine)], axis=0)


WEIGHTS = ["ada_w", "ada_b", "norm_pre_g", "norm_post_g", "gla_w_in", "gla_w_gate_up", "gla_b_gate", "gla_head_g",
           "gla_w_out", "mla_w_in", "mla_q_norm_g", "mla_w_uq", "mla_kv_norm_g", "mla_w_ukv", "mla_w_out", "gdn_w_in",
           "gdn_conv_w", "gdn_a_log", "gdn_dt_bias", "gdn_norm_g", "gdn_w_out", "mlp_w_up", "mlp_w_down"]
PACK_BF16 = [("mlp_w_up", 2), ("mlp_w_down", 1), ("gdn_w_out", 1), ("gla_w_out", 1), ("mla_w_out", 1),
             ("gla_w_in", 2), ("mla_w_in", 1), ("mla_w_uq", 2), ("mla_w_ukv", 2), ("gdn_w_in", 2)]
N_DIRECT = 5
PACK_F32 = [("norm_pre_g", 2), ("norm_post_g", 2), ("gla_w_gate_up", 2), ("gla_b_gate", 1), ("gla_head_g", 1), ("gdn_conv_w", 2)]
REPLICATED_SMALL = ["mla_q_norm_g", "mla_kv_norm_g", "gdn_a_log", "gdn_dt_bias", "gdn_norm_g"]
MIXERS = ["gla", "mla", "gdn"]


def _silu_rows(a, *, name):
    def body(a_ref, o_ref):
        v = a_ref[...]
        o_ref[...] = v * _sigmoid(v)

    return pl.pallas_call(body, out_shape=jax.ShapeDtypeStruct(a.shape, F32), name=name)(a)


SMALL_ROWS = 16


def _piece_rows(size, mult):
    assert size % FLAT_W == 0
    return -(-(size // FLAT_W) // mult) * mult


def _to_rows(a, lead, mult):
    n = math.prod(a.shape[len(lead):])
    r = a.reshape(lead + (n // FLAT_W, FLAT_W))
    extra = _piece_rows(n, mult) - n // FLAT_W
    return jnp.pad(r, [(0, 0)] * len(lead) + [(0, extra), (0, 0)]) if extra else r


def _small_to_rows(parts, lead):
    flat = jnp.concatenate([p.reshape(lead + (-1,)) for p in parts], axis=-1)
    pad = SMALL_ROWS * FLAT_W - flat.shape[-1]
    return jnp.pad(flat, [(0, 0)] * len(lead) + [(0, pad)]).reshape(lead + (SMALL_ROWS, FLAT_W))


def _small_from_rows(rows, shards, lead):
    flat = rows.reshape(lead + (-1,))
    out, off = {}, 0
    for n, _ in PACK_F32:
        out[n] = flat[..., off:off + shards[n].size].reshape(lead + shards[n].shape)
        off += shards[n].size
    return out


def _pack_weights(shards):
    parts = [_to_rows(shards[n].astype(BF16), (), 16) for n, _ in PACK_BF16]
    small = _small_to_rows([shards[n] for n, _ in PACK_F32], ())
    parts.append(lax.bitcast_convert_type(small, BF16).reshape(2 * SMALL_ROWS, FLAT_W))
    flat = jnp.concatenate(parts, axis=0)
    return jnp.pad(flat, ((0, FLAT_ROWS - flat.shape[0]), (0, 0)))


def _unpack_weights(gathered, shards):
    full, off = {}, 0
    for n, ax in PACK_BF16:
        size = shards[n].size
        seg = gathered[:, off:off + size // FLAT_W].reshape((4,) + shards[n].shape)
        full[n] = jnp.concatenate([seg[j] for j in range(4)], axis=ax)
        off += _piece_rows(size, 16)
    small = lax.bitcast_convert_type(gathered[:, off:off + 2 * SMALL_ROWS].reshape(4, SMALL_ROWS, FLAT_W, 2), F32)
    for (n, ax), seg in zip(PACK_F32, _small_from_rows(small, shards, (4,)).values()):
        full[n] = jnp.concatenate([seg[j] for j in range(4)], axis=ax)
    return full


def _grad_layout(shards):
    layout, off = {}, 0
    for n, _ in PACK_BF16:
        layout[n] = (off, shards[n].size // shards[n].shape[0] // FLAT_W)
        off += _piece_rows(shards[n].size, 16)
    layout["small"] = (off, SMALL_ROWS)
    return layout


def _pack_grads(gf, grads, layout):
    by_chip = lambda g, ax: jnp.stack(jnp.split(g.astype(gf.dtype), 4, axis=ax - 1))
    parts = []
    for n, ax in PACK_BF16[N_DIRECT:]:
        rows = sum(g.size for g in grads[n]) // (4 * FLAT_W)
        parts += [by_chip(g, ax).reshape(4, -1, FLAT_W) for g in grads[n]]
        if _piece_rows(rows * FLAT_W, 16) > rows:
            parts.append(jnp.zeros((4, _piece_rows(rows * FLAT_W, 16) - rows, FLAT_W), gf.dtype))
    parts.append(_small_to_rows([jnp.stack([by_chip(g, ax) for g in grads[n]], axis=1) for n, ax in PACK_F32], (4,)))
    first = layout[PACK_BF16[N_DIRECT][0]][0]
    rest = jnp.concatenate(parts, axis=1)
    assert first + rest.shape[1] == layout["small"][0] + SMALL_ROWS
    return lax.dynamic_update_slice(gf, rest, (0, first, 0))


def _unpack_grads(reduced, shards):
    out, off = {}, 0
    for n, _ in PACK_BF16:
        size = shards[n].size
        out[n] = reduced[off:off + size // FLAT_W].reshape(shards[n].shape)
        off += _piece_rows(size, 16)
    out.update(_small_from_rows(reduced[off:off + SMALL_ROWS], shards, ()))
    return out


def _mixer_weights(kind, j, full, rep):
    if kind == "gla":
        return dict(w_in=jnp.pad(full["gla_w_in"][j], ((0, 0), (0, GLA_IN_PAD - GLA_IN))),
                    wg=jnp.pad(full["gla_w_gate_up"][j], ((0, LANES - GLA_RANK), (0, 0))),
                    bg=full["gla_b_gate"][j][None], hg=full["gla_head_g"][j][None], w_out=full["gla_w_out"][j])
    if kind == "mla":
        return dict(w_in=jnp.pad(full["mla_w_in"][j], ((0, 0), (0, MLA_IN_PAD - MLA_IN))), q_norm_g=rep["mla_q_norm_g"][j][None],
                    w_uq=full["mla_w_uq"][j], kv_norm_g=rep["mla_kv_norm_g"][j][None], w_ukv=full["mla_w_ukv"][j],
                    w_out=full["mla_w_out"][j])
    return dict(w_in=jnp.pad(full["gdn_w_in"][j], ((0, 0), (0, GDN_IN_PAD - GDN_IN))), conv_w=full["gdn_conv_w"][j],
                a_log=rep["gdn_a_log"][j][:, None, None], dt_bias=rep["gdn_dt_bias"][j][:, None, None],
                norm_g=rep["gdn_norm_g"][j][None], w_out=full["gdn_w_out"][j])


def _layer_fwd(xin, mod, gains, kind, mw, w_up, w_down, pos, tag):
    sh_m, sc_m, gt_m, sh_f, sc_f, gt_f = mod
    pre0, pre1, post0, post1 = gains
    h = _premod_fwd(xin, pre0, sc_m, sh_m, name=tag + "_pre0")
    if kind == "gla":
        y, saved = _gla_layer_fwd(h, mw, tag + "_gla")
    elif kind == "mla":
        y, saved = _mla_layer_fwd(h, pos, mw, tag + "_mla")
    else:
        y, saved = _gdn_layer_fwd(h, mw, tag + "_gdn")
    x1 = _postres_fwd(xin, y, post0, gt_m, name=tag + "_post0")
    h2 = _premod_fwd(x1, pre1, sc_f, sh_f, name=tag + "_pre1")
    act = _mm(h2, w_up, out_dtype=BF16, epi="relu2", name=tag + "_up")
    y2 = _mm(act, w_down, name=tag + "_down")
    x2 = _postres_fwd(x1, y2, post1, gt_f, name=tag + "_post1")
    return x2, (xin, h, y, saved, x1, h2, act, y2)


def _layer_bwd(g2, kept, mod, gains, kind, mw, w_up, w_down, tag, gf, rows):
    xin, h, y, saved, x1, h2, act, y2 = kept
    sh_m, sc_m, gt_m, sh_f, sc_f, gt_f = mod
    pre0, pre1, post0, post1 = gains
    dy2, dpost1, dgt_f = _postres_bwd(g2, y2, post1, gt_f, name=tag + "_post1_b")
    du = _mm(dy2, w_down, tb=True, out_dtype=BF16, epi="dact", aux=act, name=tag + "_du")
    gf = _mm(act, dy2, ta=True, out_dtype=gf.dtype, into=(gf, rows[1], "rows"), name=tag + "_dwdown")
    gf = _mm(h2, du, ta=True, out_dtype=gf.dtype, into=(gf, rows[0], "cols"), name=tag + "_dwup")
    dh2 = _mm(du, w_up, tb=True, out_dtype=BF16, name=tag + "_dh2")
    g1, dpre1, dsc_f, dsh_f = _premod_bwd(dh2, x1, pre1, sc_f, g2, name=tag + "_pre1_b")
    dy, dpost0, dgt_m = _postres_bwd(g1, y, post0, gt_m, name=tag + "_post0_b")
    mixer_bwd = dict(gla=_gla_layer_bwd, mla=_mla_layer_bwd, gdn=_gdn_layer_bwd)[kind]
    dh, mg, gf = mixer_bwd(dy, h, saved, mw, tag + "_" + kind, sink=(gf, rows[2]))
    g0, dpre0, dsc_m, dsh_m = _premod_bwd(dh, xin, pre0, sc_m, g1, name=tag + "_pre0_b")
    dmod = jnp.concatenate([dsh_m, dsc_m, dgt_m, dsh_f, dsc_f, dgt_f], axis=1)
    return g0, dmod, jnp.concatenate([dpre0, dpre1], axis=0), jnp.concatenate([dpost0, dpost1], axis=0), mg, gf


def kernel(x, c, positions, ada_w, ada_b, norm_pre_g, norm_post_g, gla_w_in, gla_w_gate_up, gla_b_gate, gla_head_g, gla_w_out, mla_w_in, mla_q_norm_g, mla_w_uq, mla_kv_norm_g, mla_w_ukv, mla_w_out, gdn_w_in, gdn_conv_w, gdn_a_log, gdn_dt_bias, gdn_norm_g, gdn_w_out, mlp_w_up, mlp_w_down, loss_target, m_ada_w, m_ada_b, m_norm_pre_g, m_norm_post_g, m_gla_w_in, m_gla_w_gate_up, m_gla_b_gate, m_gla_head_g, m_gla_w_out, m_mla_w_in, m_mla_q_norm_g, m_mla_w_uq, m_mla_kv_norm_g, m_mla_w_ukv, m_mla_w_out, m_gdn_w_in, m_gdn_conv_w, m_gdn_a_log, m_gdn_dt_bias, m_gdn_norm_g, m_gdn_w_out, m_mlp_w_up, m_mlp_w_down, v_ada_w, v_ada_b, v_norm_pre_g, v_norm_post_g, v_gla_w_in, v_gla_w_gate_up, v_gla_b_gate, v_gla_head_g, v_gla_w_out, v_mla_w_in, v_mla_q_norm_g, v_mla_w_uq, v_mla_kv_norm_g, v_mla_w_ukv, v_mla_w_out, v_gdn_w_in, v_gdn_conv_w, v_gdn_a_log, v_gdn_dt_bias, v_gdn_norm_g, v_gdn_w_out, v_mlp_w_up, v_mlp_w_down):
    w = dict(ada_w=ada_w, ada_b=ada_b, norm_pre_g=norm_pre_g, norm_post_g=norm_post_g, gla_w_in=gla_w_in,
             gla_w_gate_up=gla_w_gate_up, gla_b_gate=gla_b_gate, gla_head_g=gla_head_g, gla_w_out=gla_w_out, mla_w_in=mla_w_in,
             mla_q_norm_g=mla_q_norm_g, mla_w_uq=mla_w_uq, mla_kv_norm_g=mla_kv_norm_g, mla_w_ukv=mla_w_ukv, mla_w_out=mla_w_out,
             gdn_w_in=gdn_w_in, gdn_conv_w=gdn_conv_w, gdn_a_log=gdn_a_log, gdn_dt_bias=gdn_dt_bias, gdn_norm_g=gdn_norm_g,
             gdn_w_out=gdn_w_out, mlp_w_up=mlp_w_up, mlp_w_down=mlp_w_down)
    m = dict(zip(WEIGHTS, [m_ada_w, m_ada_b, m_norm_pre_g, m_norm_post_g, m_gla_w_in, m_gla_w_gate_up, m_gla_b_gate, m_gla_head_g,
                           m_gla_w_out, m_mla_w_in, m_mla_q_norm_g, m_mla_w_uq, m_mla_kv_norm_g, m_mla_w_ukv, m_mla_w_out,
                           m_gdn_w_in, m_gdn_conv_w, m_gdn_a_log, m_gdn_dt_bias, m_gdn_norm_g, m_gdn_w_out, m_mlp_w_up, m_mlp_w_down]))
    v = dict(zip(WEIGHTS, [v_ada_w, v_ada_b, v_norm_pre_g, v_norm_post_g, v_gla_w_in, v_gla_w_gate_up, v_gla_b_gate, v_gla_head_g,
                           v_gla_w_out, v_mla_w_in, v_mla_q_norm_g, v_mla_w_uq, v_mla_kv_norm_g, v_mla_w_ukv, v_mla_w_out,
                           v_gdn_w_in, v_gdn_conv_w, v_gdn_a_log, v_gdn_dt_bias, v_gdn_norm_g, v_gdn_w_out, v_mlp_w_up, v_mlp_w_down]))
    t = x.shape[1]
    ix, iy, ic = lax.axis_index("x"), lax.axis_index("y"), lax.axis_index("c")
    me = 4 * ix + 2 * iy + ic
    chip = 2 * ix + iy
    ada_cols = ada_w.shape[2]

    packed = _pack_weights(w)
    zero = jnp.zeros((), jnp.int32)
    gathered = lax.dynamic_update_slice(_chip_gather(packed, name="gather_weights"), packed[None], (chip, zero, zero))
    full = _unpack_weights(gathered, w)

    cond8 = _silu_rows(jnp.pad(c, ((0, 7), (0, 0))), name="cond_silu")
    cond16 = jnp.pad(_all8_gather(cond8, name="gather_cond")[:, 0, :], ((0, 8), (0, 0)))
    mod_cols = []
    for layer in range(DEPTH):
        bias = jnp.broadcast_to(lax.dynamic_slice_in_dim(ada_b[layer], chip * ada_cols, ada_cols)[None], (16, ada_cols))
        mod_cols.append(_mm(cond16, ada_w[layer], epi="add", aux=bias, name=f"ada{layer}")[:8])
    mod_all = _all8_gather(jnp.stack(mod_cols).reshape(DEPTH * 8, ada_cols), name="gather_mod")
    mod = jnp.concatenate([lax.dynamic_slice_in_dim(mod_all[2 * j].reshape(DEPTH, 8, ada_cols), me, 1, axis=1)[:, 0]
                           for j in range(4)], axis=1)

    def layer_args(layer):
        kind, j = MIXERS[layer % 3], layer // 3
        mods = [mod[layer, i * D_MODEL:(i + 1) * D_MODEL][None] for i in range(N_MOD)]
        gains = (full["norm_pre_g"][layer, 0:1], full["norm_pre_g"][layer, 1:2], full["norm_post_g"][layer, 0:1],
                 full["norm_post_g"][layer, 1:2])
        return kind, j, mods, gains, _mixer_weights(kind, j, full, w)

    xs = x[0]
    kept = []
    for layer in range(DEPTH):
        kind, j, mods, gains, mw = layer_args(layer)
        xs, keep = _layer_fwd(xs, mods, gains, kind, mw, full["mlp_w_up"][layer], full["mlp_w_down"][layer], positions[0], f"l{layer}")
        kept.append(keep)
    loss_row, g = _loss_head(xs, loss_target[0], name="loss_head")
    loss = lax.psum(loss_row[0, 0], ("x", "y", "c"))

    grads = {n: [None] * w[n].shape[0] for n, _ in PACK_BF16[N_DIRECT:] + PACK_F32}
    rep_grads = {}
    dmods = [None] * DEPTH
    layout = _grad_layout(w)
    row_of = lambda n, idx: layout[n][0] + idx * layout[n][1]
    gf = lax.empty((4, FLAT_ROWS, FLAT_W), BF16)
    for layer in reversed(range(DEPTH)):
        kind, j, mods, gains, mw = layer_args(layer)
        rows = (row_of("mlp_w_up", layer), row_of("mlp_w_down", layer), row_of(kind + "_w_out", j))
        g, dmods[layer], dpre, dpost, mg, gf = _layer_bwd(
            g, kept[layer], mods, gains, kind, mw, full["mlp_w_up"][layer], full["mlp_w_down"][layer], f"l{layer}", gf, rows)
        grads["norm_pre_g"][layer], grads["norm_post_g"][layer] = dpre, dpost
        for key, val in mg.items():
            name = kind + "_" + key
            if name in grads:
                grads[name][j] = val
            elif name in REPLICATED_SMALL:
                rep_grads[name] = val[None]

    rep_flat = jnp.concatenate([rep_grads[n].reshape(-1) for n in REPLICATED_SMALL])
    dbuf = jnp.concatenate([jnp.concatenate(dmods, axis=0), jnp.pad(rep_flat, (0, N_MOD * D_MODEL - rep_flat.shape[0]))[None],
                            jnp.zeros((3, N_MOD * D_MODEL), F32)], axis=0)
    dall = _all8_gather(dbuf, name="gather_dmod")
    dsum = _sum_slots(dall, name="sum_dmod")
    out_grads = {"ada_b": dsum[:DEPTH]}
    off = 0
    for n in REPLICATED_SMALL:
        out_grads[n] = dsum[DEPTH, off:off + w[n].size].reshape(w[n].shape)
        off += w[n].size
    dada = []
    for layer in range(DEPTH):
        dm16 = jnp.pad(lax.dynamic_slice_in_dim(dall[:, layer, :], chip * ada_cols, ada_cols, axis=1), ((0, 8), (0, 0)))
        dada.append(_mm(cond16, dm16, ta=True, name=f"dada{layer}"))
    out_grads["ada_w"] = jnp.stack(dada)

    reduced = _reduce_scatter(_pack_grads(gf, grads, layout), ic.reshape(1).astype(jnp.int32),
                              chip.reshape(1).astype(jnp.int32), tag="reduce_grads")
    out_grads.update(_unpack_grads(reduced, w))

    deltas, new_m, new_v = {}, {}, {}
    for n in WEIGHTS:
        deltas[n], new_m[n], new_v[n] = _adamw(w[n], out_grads[n], m[n], v[n], name="adamw_" + n)
    return (loss, g[None], *[out_grads[n] for n in WEIGHTS], *[deltas[n] for n in WEIGHTS],
            *[new_m[n] for n in WEIGHTS], *[new_v[n] for n in WEIGHTS])
```
